```python
import math
import jax, jax.numpy as jnp
from jax import lax
import numpy as np

D_MODEL = 1024
BATCH = 8
SEQ = 2048
DEPTH = 2

N_BRANCH = 4
BRANCH_W = D_MODEL // 2
DN_HEADS = 4
DN_DK = BRANCH_W // DN_HEADS
DN_DV = BRANCH_W // DN_HEADS
DN_CHUNK = 64
CONV_W = 4
GM_GROUPS = 4
GM_CH = BRANCH_W // GM_GROUPS
GM_CHUNK = 128
SW_HEADS = 8
SW_KV_HEADS = 2
SW_HD = BRANCH_W // SW_HEADS
SW_WINDOW = 128
SW_BLOCK = 128
MEM_LEN = 256
XM_HEADS = 4
XM_HD = BRANCH_W // XM_HEADS

EPS = 1e-6
NEG_INF = -1e30

IN_SPLITS = [
    3 * BRANCH_W,
    BRANCH_W,
    DN_HEADS,
    DN_HEADS,
    2 * BRANCH_W,
    BRANCH_W,
    SW_HEADS * SW_HD,
    SW_KV_HEADS * SW_HD,
    SW_KV_HEADS * SW_HD,
    BRANCH_W,
    XM_HEADS * XM_HD,
    BRANCH_W,
    N_BRANCH * D_MODEL,
]
D_IN = sum(IN_SPLITS)

kernel_name = "hybrid_parallel_gated_deltanet_gmlp_swa_memory"


def _rmsnorm(x, g):
    xf = x.astype(jnp.float32)
    y = xf * lax.rsqrt(jnp.mean(xf * xf, axis=-1, keepdims=True) + EPS)
    return (y * g.astype(jnp.float32)).astype(x.dtype)


def _l2norm(x):
    return x * lax.rsqrt(jnp.sum(x * x, axis=-1, keepdims=True) + EPS)


def _split_cols(cols):
    idx, acc = [], 0
    for s in IN_SPLITS[:-1]:
        acc += s
        idx.append(acc)
    return jnp.split(cols, idx, axis=-1)


def _causal_dwconv(x, w):
    c = x.shape[-1]
    return lax.conv_general_dilated(
        x, w[:, None, :].astype(x.dtype), window_strides=(1,),
        padding=[(CONV_W - 1, 0)], dimension_numbers=("NWC", "WIO", "NWC"),
        feature_group_count=c)


def _gated_delta_chunked(q, k, v, g, beta):
    B, T, H, dk = q.shape
    dv = v.shape[-1]
    C = DN_CHUNK
    n = T // C

    def blk(a):
        return jnp.moveaxis(a.reshape((B, n, C, H) + a.shape[3:]), 3, 1)

    q, k, v, g, beta = blk(q), blk(k), blk(v), blk(g), blk(beta)
    gc = jnp.cumsum(g, axis=-1)
    diff = gc[..., :, None] - gc[..., None, :]
    ii = jnp.arange(C)
    strict = ii[:, None] > ii[None, :]
    incl = ii[:, None] >= ii[None, :]
    kb = k * beta[..., None]
    L = jnp.where(strict, jnp.einsum("bhncd,bhnsd->bhncs", kb, k)
                  * jnp.exp(jnp.where(strict, diff, 0.0)), 0.0)
    eye = jnp.eye(C, dtype=q.dtype)
    rhs = jnp.concatenate([v * beta[..., None], kb * jnp.exp(gc)[..., None]], axis=-1)
    sol = lax.linalg.triangular_solve(eye + L, rhs, left_side=True, lower=True,
                                      unit_diagonal=True)
    u, w = sol[..., :dv], sol[..., dv:]
    a_qk = jnp.where(incl, jnp.einsum("bhncd,bhnsd->bhncs", q, k)
                     * jnp.exp(jnp.where(incl, diff, 0.0)), 0.0)
    g_last = gc[..., -1]
    qg = q * jnp.exp(gc)[..., None]
    kd = k * jnp.exp(g_last[..., None] - gc)[..., None]
    d_last = jnp.exp(g_last)
    xs = tuple(jnp.moveaxis(a, 2, 0) for a in (qg, kd, u, w, a_qk, d_last))

    def step(S, inp):
        qg_c, kd_c, u_c, w_c, a_c, d_c = inp
        v_new = u_c - jnp.einsum("bhck,bhkv->bhcv", w_c, S)
        o = (jnp.einsum("bhck,bhkv->bhcv", qg_c, S)
             + jnp.einsum("bhcs,bhsv->bhcv", a_c, v_new))
        S = S * d_c[..., None, None] + jnp.einsum("bhck,bhcv->bhkv", kd_c, v_new)
        return S, o

    S0 = jnp.zeros((B, H, dk, dv), jnp.float32)
    _, o = lax.scan(step, S0, xs)
    o = jnp.moveaxis(o, 0, 2)
    return jnp.moveaxis(o, 1, 3).reshape(B, T, H, dv)


def _deltanet_branch(qkv, z, b_logit, a_logit, conv_w, a_log, dt_bias, o_norm):
    B, T, _ = qkv.shape
    qkv = jax.nn.silu(_causal_dwconv(qkv, conv_w)).astype(jnp.float32)
    q, k, v = jnp.split(qkv, 3, axis=-1)
    q = _l2norm(q.reshape(B, T, DN_HEADS, DN_DK)) * (DN_DK ** -0.5)
    k = _l2norm(k.reshape(B, T, DN_HEADS, DN_DK))
    v = v.reshape(B, T, DN_HEADS, DN_DV)
    beta = jax.nn.sigmoid(b_logit.astype(jnp.float32))
    g = -jnp.exp(a_log.astype(jnp.float32)) * jax.nn.softplus(
        a_logit.astype(jnp.float32) + dt_bias.astype(jnp.float32))
    o = _gated_delta_chunked(q, k, v, g, beta)
    o = _rmsnorm(o, o_norm) * jax.nn.silu(z.astype(jnp.float32).reshape(B, T, DN_HEADS, DN_DV))
    return o.reshape(B, T, BRANCH_W).astype(z.dtype)


def _spatial_gating_branch(uv, z, v_gain, w_s, b_s):
    B, T, _ = uv.shape
    u, v = jnp.split(jax.nn.gelu(uv), 2, axis=-1)
    v = _rmsnorm(v, v_gain)
    n = T // GM_CHUNK
    vb = v.reshape(B, n, GM_CHUNK, GM_GROUPS, GM_CH)
    causal = jnp.tril(jnp.ones((GM_CHUNK, GM_CHUNK), dtype=bool))
    ws = jnp.where(causal, w_s, 0.0).astype(v.dtype)
    s = jnp.einsum("gpq,bnqgc->bnpgc", ws, vb) + b_s.T.astype(v.dtype)[None, None, :, :, None]
    return u * s.reshape(B, T, BRANCH_W) * jax.nn.silu(z)


def _sliding_window_branch(q, k, v, z, sinks):
    B, T, _ = q.shape
    P = SW_BLOCK
    n = T // P
    G = SW_HEADS // SW_KV_HEADS
    q = q.reshape(B, n, P, SW_KV_HEADS, G, SW_HD)
    k = k.reshape(B, n, P, SW_KV_HEADS, SW_HD)
    v = v.reshape(B, n, P, SW_KV_HEADS, SW_HD)
    pad = jnp.zeros_like(k[:, :1])
    kb = jnp.concatenate([jnp.concatenate([pad, k[:, :-1]], axis=1), k], axis=2)
    vb = jnp.concatenate([jnp.concatenate([pad, v[:, :-1]], axis=1), v], axis=2)
    s = jnp.einsum("bnqkgd,bnskd->bnkgqs", q, kb).astype(jnp.float32) * (SW_HD ** -0.5)
    qi = jnp.arange(P)[:, None]
    kj = jnp.arange(2 * P)[None, :]
    dist = qi + P - kj
    blk = jnp.arange(n)[:, None, None]
    valid = (dist >= 0) & (dist < SW_WINDOW) & (blk * P - P + kj >= 0)
    s = jnp.where(valid[None, :, None, None], s, NEG_INF)
    sink = jnp.broadcast_to(
        sinks.astype(jnp.float32).reshape(SW_KV_HEADS, G)[None, None, :, :, None, None],
        s.shape[:-1] + (1,))
    p = jax.nn.softmax(jnp.concatenate([s, sink], axis=-1), axis=-1)[..., :-1]
    o = jnp.einsum("bnkgqs,bnskd->bnqkgd", p.astype(vb.dtype), vb).reshape(B, T, BRANCH_W)
    return o * jax.nn.silu(z)


def _memory_branch(q, z, mem_kv):
    B, T, _ = q.shape
    q = q.reshape(B, T, XM_HEADS, XM_HD)
    mk, mv = jnp.split(mem_kv, 2, axis=-1)
    mk = mk.reshape(B, -1, XM_HEADS, XM_HD)
    mv = mv.reshape(B, -1, XM_HEADS, XM_HD)
    s = jnp.einsum("bthd,bmhd->bhtm", q, mk).astype(jnp.float32) * (XM_HD ** -0.5)
    p = jax.nn.softmax(s, axis=-1)
    o = jnp.einsum("bhtm,bmhd->bthd", p.astype(mv.dtype), mv).reshape(B, T, BRANCH_W)
    return o * jax.nn.silu(z)


def _layer(x, mem, norm_pre, norm_post, norm_mem, w_in, conv_w, a_log, dt_bias, dn_norm,
           gm_norm, spatial_w, spatial_b, sinks, w_mem_kv, w_up, w_out):
    B, T, D = x.shape
    h = _rmsnorm(x, norm_pre)
    cols = h @ w_in
    (dn_qkv, dn_z, dn_b, dn_a, gm_uv, gm_z, sw_q, sw_k, sw_v, sw_z,
     xm_q, xm_z, gate_logits) = _split_cols(cols)
    mem_kv = _rmsnorm(mem, norm_mem) @ w_mem_kv
    y_a = _deltanet_branch(dn_qkv, dn_z, dn_b, dn_a, conv_w, a_log, dt_bias, dn_norm)
    y_b = _spatial_gating_branch(gm_uv, gm_z, gm_norm, spatial_w, spatial_b)
    y_c = _sliding_window_branch(sw_q, sw_k, sw_v, sw_z, sinks)
    y_m = _memory_branch(xm_q, xm_z, mem_kv)
    ys = jnp.stack([y_a, y_b, y_c, y_m], axis=2)
    proj = jnp.einsum("btnc,ncd->btnd", ys, w_up)
    gates = jax.nn.sigmoid(gate_logits.reshape(B, T, N_BRANCH, D))
    merged = jnp.sum(gates * proj, axis=2)
    out = merged @ w_out
    return x + _rmsnorm(out, norm_post)


def _fwd_setup_inputs(seed: int = 0) -> dict:
    key = jax.random.key(seed)
    ks = jax.random.split(key, 20)
    f32 = jnp.float32
    nrm = lambda k, shape, scale: jax.random.normal(k, shape, f32) * scale
    dt = jnp.exp(jax.random.uniform(ks[8], (DEPTH, DN_HEADS), f32,
                                    math.log(1e-3), math.log(1e-1)))
    return {
        "x": nrm(ks[0], (BATCH, SEQ, D_MODEL), 1.0),
        "mem": nrm(ks[1], (BATCH, MEM_LEN, D_MODEL), 1.0),
        "norm_pre": 1.0 + nrm(ks[2], (DEPTH, D_MODEL), 0.1),
        "norm_post": 1.0 + nrm(ks[3], (DEPTH, D_MODEL), 0.1),
        "norm_mem": 1.0 + nrm(ks[4], (DEPTH, D_MODEL), 0.1),
        "w_in": nrm(ks[5], (DEPTH, D_MODEL, D_IN), D_MODEL ** -0.5),
        "conv_w": nrm(ks[6], (DEPTH, CONV_W, 3 * BRANCH_W), CONV_W ** -0.5),
        "a_log": jnp.log(jax.random.uniform(ks[7], (DEPTH, DN_HEADS), f32, 1.0, 16.0)),
        "dt_bias": dt + jnp.log(-jnp.expm1(-dt)),
        "dn_norm": 1.0 + nrm(ks[9], (DEPTH, DN_DV), 0.1),
        "gm_norm": 1.0 + nrm(ks[10], (DEPTH, BRANCH_W), 0.1),
        "spatial_w": nrm(ks[11], (DEPTH, GM_GROUPS, GM_CHUNK, GM_CHUNK), GM_CHUNK ** -0.5),
        "spatial_b": 1.0 + nrm(ks[12], (DEPTH, GM_GROUPS, GM_CHUNK), 0.1),
        "sinks": nrm(ks[13], (DEPTH, SW_HEADS), 1.0),
        "w_mem_kv": nrm(ks[14], (DEPTH, D_MODEL, 2 * BRANCH_W), D_MODEL ** -0.5),
        "w_up": nrm(ks[15], (DEPTH, N_BRANCH, BRANCH_W, D_MODEL), BRANCH_W ** -0.5),
        "w_out": nrm(ks[16], (DEPTH, D_MODEL, D_MODEL), D_MODEL ** -0.5),
    }


def _fwd_reference(x, mem, norm_pre, norm_post, norm_mem, w_in, conv_w, a_log, dt_bias, dn_norm,
              gm_norm, spatial_w, spatial_b, sinks, w_mem_kv, w_up, w_out):
    for l in range(DEPTH):
        x = _layer(x, mem, norm_pre[l], norm_post[l], norm_mem[l], w_in[l], conv_w[l],
                   a_log[l], dt_bias[l], dn_norm[l], gm_norm[l], spatial_w[l], spatial_b[l],
                   sinks[l], w_mem_kv[l], w_up[l], w_out[l])
    return x


import jax as _jax
import jax.numpy as _jnp

TWIN_FORMAT = 'train_step'
FWD_PARAMS = ['x', 'mem', 'norm_pre', 'norm_post', 'norm_mem', 'w_in', 'conv_w', 'a_log', 'dt_bias', 'dn_norm', 'gm_norm', 'spatial_w', 'spatial_b', 'sinks', 'w_mem_kv', 'w_up', 'w_out']
TWIN_WEIGHTS = ['norm_pre', 'norm_post', 'norm_mem', 'w_in', 'conv_w', 'a_log', 'dt_bias', 'dn_norm', 'gm_norm', 'spatial_w', 'spatial_b', 'sinks', 'w_mem_kv', 'w_up', 'w_out']
TWIN_DIFF_INPUT = 'x'
TWIN_INPUTS = ['x', 'mem', 'norm_pre', 'norm_post', 'norm_mem', 'w_in', 'conv_w', 'a_log', 'dt_bias', 'dn_norm', 'gm_norm', 'spatial_w', 'spatial_b', 'sinks', 'w_mem_kv', 'w_up', 'w_out', 'loss_target', 'm_norm_pre', 'm_norm_post', 'm_norm_mem', 'm_w_in', 'm_conv_w', 'm_a_log', 'm_dt_bias', 'm_dn_norm', 'm_gm_norm', 'm_spatial_w', 'm_spatial_b', 'm_sinks', 'm_w_mem_kv', 'm_w_up', 'm_w_out', 'v_norm_pre', 'v_norm_post', 'v_norm_mem', 'v_w_in', 'v_conv_w', 'v_a_log', 'v_dt_bias', 'v_dn_norm', 'v_gm_norm', 'v_spatial_w', 'v_spatial_b', 'v_sinks', 'v_w_mem_kv', 'v_w_up', 'v_w_out']
TWIN_OUTPUTS = ['loss', 'grad_x', 'grad_norm_pre', 'grad_norm_post', 'grad_norm_mem', 'grad_w_in', 'grad_conv_w', 'grad_a_log', 'grad_dt_bias', 'grad_dn_norm', 'grad_gm_norm', 'grad_spatial_w', 'grad_spatial_b', 'grad_sinks', 'grad_w_mem_kv', 'grad_w_up', 'grad_w_out', 'delta_norm_pre', 'delta_norm_post', 'delta_norm_mem', 'delta_w_in', 'delta_conv_w', 'delta_a_log', 'delta_dt_bias', 'delta_dn_norm', 'delta_gm_norm', 'delta_spatial_w', 'delta_spatial_b', 'delta_sinks', 'delta_w_mem_kv', 'delta_w_up', 'delta_w_out', 'new_m_norm_pre', 'new_m_norm_post', 'new_m_norm_mem', 'new_m_w_in', 'new_m_conv_w', 'new_m_a_log', 'new_m_dt_bias', 'new_m_dn_norm', 'new_m_gm_norm', 'new_m_spatial_w', 'new_m_spatial_b', 'new_m_sinks', 'new_m_w_mem_kv', 'new_m_w_up', 'new_m_w_out', 'new_v_norm_pre', 'new_v_norm_post', 'new_v_norm_mem', 'new_v_w_in', 'new_v_conv_w', 'new_v_a_log', 'new_v_dt_bias', 'new_v_dn_norm', 'new_v_gm_norm', 'new_v_spatial_w', 'new_v_spatial_b', 'new_v_sinks', 'new_v_w_mem_kv', 'new_v_w_up', 'new_v_w_out']
TWIN_LEAF_KINDS = {'loss': 'loss', 'grad_x': 'grad_x', 'grad_norm_pre': 'grad_w', 'grad_norm_post': 'grad_w', 'grad_norm_mem': 'grad_w', 'grad_w_in': 'grad_w', 'grad_conv_w': 'grad_w', 'grad_a_log': 'grad_w', 'grad_dt_bias': 'grad_w', 'grad_dn_norm': 'grad_w', 'grad_gm_norm': 'grad_w', 'grad_spatial_w': 'grad_w', 'grad_spatial_b': 'grad_w', 'grad_sinks': 'grad_w', 'grad_w_mem_kv': 'grad_w', 'grad_w_up': 'grad_w', 'grad_w_out': 'grad_w', 'delta_norm_pre': 'delta_w', 'delta_norm_post': 'delta_w', 'delta_norm_mem': 'delta_w', 'delta_w_in': 'delta_w', 'delta_conv_w': 'delta_w', 'delta_a_log': 'delta_w', 'delta_dt_bias': 'delta_w', 'delta_dn_norm': 'delta_w', 'delta_gm_norm': 'delta_w', 'delta_spatial_w': 'delta_w', 'delta_spatial_b': 'delta_w', 'delta_sinks': 'delta_w', 'delta_w_mem_kv': 'delta_w', 'delta_w_up': 'delta_w', 'delta_w_out': 'delta_w', 'new_m_norm_pre': 'new_m', 'new_m_norm_post': 'new_m', 'new_m_norm_mem': 'new_m', 'new_m_w_in': 'new_m', 'new_m_conv_w': 'new_m', 'new_m_a_log': 'new_m', 'new_m_dt_bias': 'new_m', 'new_m_dn_norm': 'new_m', 'new_m_gm_norm': 'new_m', 'new_m_spatial_w': 'new_m', 'new_m_spatial_b': 'new_m', 'new_m_sinks': 'new_m', 'new_m_w_mem_kv': 'new_m', 'new_m_w_up': 'new_m', 'new_m_w_out': 'new_m', 'new_v_norm_pre': 'new_v', 'new_v_norm_post': 'new_v', 'new_v_norm_mem': 'new_v', 'new_v_w_in': 'new_v', 'new_v_conv_w': 'new_v', 'new_v_a_log': 'new_v', 'new_v_dt_bias': 'new_v', 'new_v_dn_norm': 'new_v', 'new_v_gm_norm': 'new_v', 'new_v_spatial_w': 'new_v', 'new_v_spatial_b': 'new_v', 'new_v_sinks': 'new_v', 'new_v_w_mem_kv': 'new_v', 'new_v_w_up': 'new_v', 'new_v_w_out': 'new_v'}


def _forward(args):
    return _fwd_reference(*[args[k] for k in FWD_PARAMS])


def _output_shape():
    out = _jax.eval_shape(lambda: _forward(_fwd_setup_inputs(0)))
    return out.shape, out.dtype

N_MICROBATCH = 1
ADAM_LR = 0.001
ADAM_B1 = 0.9
ADAM_B2 = 0.999
ADAM_EPS = 1e-08
ADAM_WD = 0.01
ADAM_STEP = 10
PER_EXAMPLE_BATCH_AXIS = {'x': 0, 'mem': 0, 'loss_target': 0}
SHARED_INPUTS = []
_WEIGHT_DTYPES = {'norm_pre': _jnp.float32, 'norm_post': _jnp.float32, 'norm_mem': _jnp.float32, 'w_in': _jnp.float32, 'conv_w': _jnp.float32, 'a_log': _jnp.float32, 'dt_bias': _jnp.float32, 'dn_norm': _jnp.float32, 'gm_norm': _jnp.float32, 'spatial_w': _jnp.float32, 'spatial_b': _jnp.float32, 'sinks': _jnp.float32, 'w_mem_kv': _jnp.float32, 'w_up': _jnp.float32, 'w_out': _jnp.float32}
MOMENT_SCALE = {'norm_pre': 5.663067e-01, 'norm_post': 1.609327e+01, 'norm_mem': 4.399178e-02, 'w_in': 1.806846e-01, 'conv_w': 3.845264e-01, 'a_log': 9.286209e-01, 'dt_bias': 8.832318e-01, 'dn_norm': 1.828172e+00, 'gm_norm': 1.449047e-01, 'spatial_w': 1.413790e-01, 'spatial_b': 2.026890e-01, 'sinks': 8.278106e-02, 'w_mem_kv': 4.099735e-02, 'w_up': 3.236827e-01, 'w_out': 6.790569e-01}


def _to_microbatches(a, axis):
    t = _jnp.moveaxis(a, axis, 0)
    t = t.reshape((N_MICROBATCH, t.shape[0] // N_MICROBATCH) + t.shape[1:])
    return _jnp.moveaxis(t, 1, axis + 1)


def setup_inputs(seed: int = 0) -> dict:
    inp = _fwd_setup_inputs(seed)
    key = _jax.random.fold_in(_jax.random.key(seed), 7919)
    shape, _ = _output_shape()
    out = dict(inp)
    out["loss_target"] = _jax.random.normal(_jax.random.fold_in(key, 0), shape, _jnp.float32)
    for i, name in enumerate(TWIN_WEIGHTS):
        w = inp[name].astype(_jnp.float32)
        if MOMENT_SCALE is None:
            s = _jnp.sqrt(_jnp.mean(_jnp.square(w)) + 1e-30)
        else:
            s = MOMENT_SCALE[name]
        km, kv = _jax.random.split(_jax.random.fold_in(key, i + 1))
        out[name] = w
        out["m_" + name] = s * _jax.random.normal(km, w.shape, _jnp.float32)
        out["v_" + name] = (s * s) * _jax.random.uniform(kv, w.shape, _jnp.float32, 0.5, 1.5)
    if N_MICROBATCH > 1:
        for name, axis in PER_EXAMPLE_BATCH_AXIS.items():
            out[name] = _to_microbatches(out[name], axis)
    return {'x': out['x'], 'mem': out['mem'], 'norm_pre': out['norm_pre'], 'norm_post': out['norm_post'], 'norm_mem': out['norm_mem'], 'w_in': out['w_in'], 'conv_w': out['conv_w'], 'a_log': out['a_log'], 'dt_bias': out['dt_bias'], 'dn_norm': out['dn_norm'], 'gm_norm': out['gm_norm'], 'spatial_w': out['spatial_w'], 'spatial_b': out['spatial_b'], 'sinks': out['sinks'], 'w_mem_kv': out['w_mem_kv'], 'w_up': out['w_up'], 'w_out': out['w_out'], 'loss_target': out['loss_target'], 'm_norm_pre': out['m_norm_pre'], 'm_norm_post': out['m_norm_post'], 'm_norm_mem': out['m_norm_mem'], 'm_w_in': out['m_w_in'], 'm_conv_w': out['m_conv_w'], 'm_a_log': out['m_a_log'], 'm_dt_bias': out['m_dt_bias'], 'm_dn_norm': out['m_dn_norm'], 'm_gm_norm': out['m_gm_norm'], 'm_spatial_w': out['m_spatial_w'], 'm_spatial_b': out['m_spatial_b'], 'm_sinks': out['m_sinks'], 'm_w_mem_kv': out['m_w_mem_kv'], 'm_w_up': out['m_w_up'], 'm_w_out': out['m_w_out'], 'v_norm_pre': out['v_norm_pre'], 'v_norm_post': out['v_norm_post'], 'v_norm_mem': out['v_norm_mem'], 'v_w_in': out['v_w_in'], 'v_conv_w': out['v_conv_w'], 'v_a_log': out['v_a_log'], 'v_dt_bias': out['v_dt_bias'], 'v_dn_norm': out['v_dn_norm'], 'v_gm_norm': out['v_gm_norm'], 'v_spatial_w': out['v_spatial_w'], 'v_spatial_b': out['v_spatial_b'], 'v_sinks': out['v_sinks'], 'v_w_mem_kv': out['v_w_mem_kv'], 'v_w_up': out['v_w_up'], 'v_w_out': out['v_w_out']}


def _loss(weights, diff, rest, loss_target):
    with _jax.named_scope("forward"):
        args = {**rest, TWIN_DIFF_INPUT: diff, **{k: w.astype(_WEIGHT_DTYPES[k]) for k, w in weights.items()}}
        y = _forward(args)
    with _jax.named_scope("loss_head"):
        err = _jnp.square(y.astype(_jnp.float32) - loss_target)
        return 0.5 * _jnp.sum(_jnp.mean(err, axis=-1)) if err.ndim else 0.5 * err


def _adamw(w, g, m, v):
    m = ADAM_B1 * m + (1.0 - ADAM_B1) * g
    v = ADAM_B2 * v + (1.0 - ADAM_B2) * _jnp.square(g)
    m_hat = m / (1.0 - ADAM_B1 ** ADAM_STEP)
    v_hat = v / (1.0 - ADAM_B2 ** ADAM_STEP)
    delta = -ADAM_LR * (m_hat / (_jnp.sqrt(v_hat) + ADAM_EPS) + ADAM_WD * w)
    return delta, m, v


def reference(x, mem, norm_pre, norm_post, norm_mem, w_in, conv_w, a_log, dt_bias, dn_norm, gm_norm, spatial_w, spatial_b, sinks, w_mem_kv, w_up, w_out, loss_target, m_norm_pre, m_norm_post, m_norm_mem, m_w_in, m_conv_w, m_a_log, m_dt_bias, m_dn_norm, m_gm_norm, m_spatial_w, m_spatial_b, m_sinks, m_w_mem_kv, m_w_up, m_w_out, v_norm_pre, v_norm_post, v_norm_mem, v_w_in, v_conv_w, v_a_log, v_dt_bias, v_dn_norm, v_gm_norm, v_spatial_w, v_spatial_b, v_sinks, v_w_mem_kv, v_w_up, v_w_out):
    given = dict(x=x, mem=mem, norm_pre=norm_pre, norm_post=norm_post, norm_mem=norm_mem, w_in=w_in, conv_w=conv_w, a_log=a_log, dt_bias=dt_bias, dn_norm=dn_norm, gm_norm=gm_norm, spatial_w=spatial_w, spatial_b=spatial_b, sinks=sinks, w_mem_kv=w_mem_kv, w_up=w_up, w_out=w_out, loss_target=loss_target, m_norm_pre=m_norm_pre, m_norm_post=m_norm_post, m_norm_mem=m_norm_mem, m_w_in=m_w_in, m_conv_w=m_conv_w, m_a_log=m_a_log, m_dt_bias=m_dt_bias, m_dn_norm=m_dn_norm, m_gm_norm=m_gm_norm, m_spatial_w=m_spatial_w, m_spatial_b=m_spatial_b, m_sinks=m_sinks, m_w_mem_kv=m_w_mem_kv, m_w_up=m_w_up, m_w_out=m_w_out, v_norm_pre=v_norm_pre, v_norm_post=v_norm_post, v_norm_mem=v_norm_mem, v_w_in=v_w_in, v_conv_w=v_conv_w, v_a_log=v_a_log, v_dt_bias=v_dt_bias, v_dn_norm=v_dn_norm, v_gm_norm=v_gm_norm, v_spatial_w=v_spatial_w, v_spatial_b=v_spatial_b, v_sinks=v_sinks, v_w_mem_kv=v_w_mem_kv, v_w_up=v_w_up, v_w_out=v_w_out)
    weights = {n: given[n] for n in TWIN_WEIGHTS}
    shared = {n: given[n] for n in SHARED_INPUTS}
    per_example = {n: given[n] for n in ['x', 'mem']}
    grad_fn = _jax.value_and_grad(_loss, argnums=(0, 1))

    def one_microbatch(ex, loss_target):
        ex = dict(ex)
        diff = ex.pop(TWIN_DIFF_INPUT)
        return grad_fn(weights, diff, {**shared, **ex}, loss_target)

    if N_MICROBATCH == 1:
        loss, (grad_w, grad_x) = one_microbatch(per_example, given["loss_target"])
    else:
        def body(carry, xs):
            loss_sum, grad_sum = carry
            l_k, (gw_k, gx_k) = one_microbatch(xs[0], xs[1])
            with _jax.named_scope("update"):
                return (loss_sum + l_k, _jax.tree.map(_jnp.add, grad_sum, gw_k)), gx_k

        init = (_jnp.zeros((), _jnp.float32), _jax.tree.map(_jnp.zeros_like, weights))
        (loss, grad_w), grad_x = _jax.lax.scan(body, init, (per_example, given["loss_target"]))
    with _jax.named_scope("update"):
        delta_w, new_m, new_v = {}, {}, {}
        for n in TWIN_WEIGHTS:
            delta_w[n], new_m[n], new_v[n] = _adamw(weights[n], grad_w[n], given["m_" + n], given["v_" + n])
    return (loss, grad_x, *[grad_w[n] for n in TWIN_WEIGHTS], *[delta_w[n] for n in TWIN_WEIGHTS],
            *[new_m[n] for n in TWIN_WEIGHTS], *[new_v[n] for n in TWIN_WEIGHTS])
```

```python
import functools
import math

import jax
import jax.numpy as jnp
from jax import lax
from jax.experimental import pallas as pl
from jax.experimental.pallas import tpu as pltpu

MESH = pl.DeviceIdType.MESH
N_DEV = 8

D_MODEL = 1024
DEPTH = 2
N_BRANCH = 4
BRANCH_W = 512
DN_HEADS = 4
CONV_W = 4
GM_GROUPS = 4
SW_HEADS = 8
EPS = 1e-6
NEG_INF = -1e30

D_IN = 9992
W_IN_SHARD = D_IN // N_DEV
W_IN_SHARD_PAD = 1280
D_IN_AL = 10240

ADAM_LR = 0.001
ADAM_B1 = 0.9
ADAM_B2 = 0.999
ADAM_EPS = 1e-08
ADAM_WD = 0.01
ADAM_STEP = 10

ROWS_WIN = DEPTH * W_IN_SHARD_PAD
ROWS_WMEM = DEPTH * 128
ROWS_WUP = DEPTH * N_BRANCH * BRANCH_W // 8
ROWS_WOUT = DEPTH * 128
ROWS_SLOT = ROWS_WIN + ROWS_WMEM + ROWS_WUP + ROWS_WOUT

VMEM_LIMIT = 56 * 1024 * 1024

BF = jnp.bfloat16
F32 = jnp.float32
DN_C = 128
DN_D = 128
HALO = 8
BLK = 128


def _my_place():
    return lax.axis_index("x"), lax.axis_index("y"), lax.axis_index("c")


def _all_gather_slots(p, name):
    r, w = p.shape

    def body(p_ref, out_ref, send_sems, recv_sems, local_sem):
        x, y, c = _my_place()
        me, sibling = (x, y, c), (x, y, 1 - c)
        chips = [(1 - x, y), (x, 1 - y), (1 - x, 1 - y)]

        def slot(px, py, pc):
            return out_ref.at[4 * px + 2 * py + pc]

        def copy(k, block, to, src=None):
            return pltpu.make_async_remote_copy(
                src_ref=slot(*block) if src is None else src,
                dst_ref=slot(*block),
                send_sem=send_sems.at[k], recv_sem=recv_sems.at[k],
                device_id=to, device_id_type=MESH)

        mine = pltpu.make_async_copy(p_ref, slot(*me), local_sem)
        mine.start()
        first = [copy(0, me, sibling, src=p_ref)]
        first += [copy(1 + j, me, (*chip, c), src=p_ref) for j, chip in enumerate(chips)]
        for cp in first:
            cp.start()
        passed = [copy(4 + j, (*chip, c), sibling) for j, chip in enumerate(chips)]
        for j, chip in enumerate(chips):
            copy(1 + j, (*chip, c), me).wait_recv()
            passed[j].start()
        copy(0, sibling, me).wait_recv()
        for j, chip in enumerate(chips):
            copy(4 + j, (*chip, 1 - c), me).wait_recv()
        for cp in first + passed:
            cp.wait_send()
        mine.wait()

    return pl.pallas_call(
        body, name=name,
        out_shape=jax.ShapeDtypeStruct((N_DEV, r, w), p.dtype),
        in_specs=[pl.BlockSpec(memory_space=pl.ANY)],
        out_specs=pl.BlockSpec(memory_space=pl.ANY),
        scratch_shapes=[pltpu.SemaphoreType.DMA((7,)), pltpu.SemaphoreType.DMA((7,)),
                        pltpu.SemaphoreType.DMA(())],
    )(p)


def _exchange_slots(g, name):
    _, r, w = g.shape

    def body(g_ref, out_ref, send_sems, recv_sems, local_sem):
        x, y, c = _my_place()
        my_slot = 4 * x + 2 * y + c
        own = pltpu.make_async_copy(g_ref.at[my_slot], out_ref.at[my_slot], local_sem)
        own.start()
        copies = []
        for k in range(1, N_DEV):
            px = 1 - x if (k >> 2) & 1 else x
            py = 1 - y if (k >> 1) & 1 else y
            pc = 1 - c if k & 1 else c
            peer_slot = 4 * px + 2 * py + pc
            copies.append((
                pltpu.make_async_remote_copy(
                    src_ref=g_ref.at[peer_slot], dst_ref=out_ref.at[my_slot],
                    send_sem=send_sems.at[k - 1], recv_sem=recv_sems.at[k - 1],
                    device_id=(px, py, pc), device_id_type=MESH),
                pltpu.make_async_remote_copy(
                    src_ref=g_ref.at[peer_slot], dst_ref=out_ref.at[peer_slot],
                    send_sem=send_sems.at[k - 1], recv_sem=recv_sems.at[k - 1],
                    device_id=(px, py, pc), device_id_type=MESH)))
        for send, _ in copies:
            send.start()
        for send, recv in copies:
            send.wait_send()
            recv.wait_recv()
        own.wait()

    return pl.pallas_call(
        body, name=name,
        out_shape=jax.ShapeDtypeStruct(g.shape, g.dtype),
        in_specs=[pl.BlockSpec(memory_space=pl.ANY)],
        out_specs=pl.BlockSpec(memory_space=pl.ANY),
        scratch_shapes=[pltpu.SemaphoreType.DMA((7,)), pltpu.SemaphoreType.DMA((7,)),
                        pltpu.SemaphoreType.DMA(())],
    )(g)


def _all_gather_rows_vmem(v, name):
    m_per, n = v.shape

    def body(v_ref, out_ref, send_sems, recv_sems, local_sem):
        x, y, c = _my_place()
        me, sibling = (x, y, c), (x, y, 1 - c)
        chips = [(1 - x, y), (x, 1 - y), (1 - x, 1 - y)]

        def rows(px, py, pc):
            return out_ref.at[pl.ds((4 * px + 2 * py + pc) * m_per, m_per), :]

        def copy(k, block, to, src=None):
            return pltpu.make_async_remote_copy(
                src_ref=rows(*block) if src is None else src, dst_ref=rows(*block),
                send_sem=send_sems.at[k], recv_sem=recv_sems.at[k],
                device_id=to, device_id_type=MESH)

        mine = pltpu.make_async_copy(v_ref, rows(*me), local_sem)
        mine.start()
        first = [copy(0, me, sibling, src=v_ref)]
        first += [copy(1 + j, me, (*chip, c), src=v_ref) for j, chip in enumerate(chips)]
        for cp in first:
            cp.start()
        passed = [copy(4 + j, (*chip, c), sibling) for j, chip in enumerate(chips)]
        for j, chip in enumerate(chips):
            copy(1 + j, (*chip, c), me).wait_recv()
            passed[j].start()
        copy(0, sibling, me).wait_recv()
        for j, chip in enumerate(chips):
            copy(4 + j, (*chip, 1 - c), me).wait_recv()
        for cp in first + passed:
            cp.wait_send()
        mine.wait()

    return pl.pallas_call(
        body, name=name,
        out_shape=jax.ShapeDtypeStruct((N_DEV * m_per, n), v.dtype),
        in_specs=[pl.BlockSpec(memory_space=pltpu.VMEM)],
        out_specs=pl.BlockSpec(memory_space=pltpu.VMEM),
        scratch_shapes=[pltpu.SemaphoreType.DMA((7,)), pltpu.SemaphoreType.DMA((7,)),
                        pltpu.SemaphoreType.DMA(())],
    )(v)


def _all_reduce_vmem(v, name):
    def body(v_ref, out_ref, buf, send_sems, recv_sems):
        x, y, c = _my_place()
        peers = [(x, y, 1 - c), (1 - x, y, c), (x, 1 - y, c)]
        out_ref[...] = v_ref[...]
        for step, peer in enumerate(peers):
            cp = pltpu.make_async_remote_copy(
                src_ref=out_ref, dst_ref=buf.at[step],
                send_sem=send_sems.at[step], recv_sem=recv_sems.at[step],
                device_id=peer, device_id_type=MESH)
            cp.start()
            cp.wait()
            out_ref[...] = out_ref[...] + buf[step]

    return pl.pallas_call(
        body, name=name,
        out_shape=jax.ShapeDtypeStruct(v.shape, v.dtype),
        in_specs=[pl.BlockSpec(memory_space=pltpu.VMEM)],
        out_specs=pl.BlockSpec(memory_space=pltpu.VMEM),
        scratch_shapes=[pltpu.VMEM((3,) + v.shape, v.dtype),
                        pltpu.SemaphoreType.DMA((3,)), pltpu.SemaphoreType.DMA((3,))],
    )(v)


def _pick(n, pref):
    t = min(n, pref)
    while n % t:
        t //= 2
    return t


_DIMS = {"nn": (((1,), (0,)), ((), ())),
         "nt": (((1,), (1,)), ((), ())),
         "tn": (((0,), (0,)), ((), ()))}


def _matmul(a, b, mode, out_dtype, name):
    if mode == "nn":
        (m, k), (_, n) = a.shape, b.shape
    elif mode == "nt":
        (m, k), (n, _) = a.shape, b.shape
    else:
        (k, m), (_, n) = a.shape, b.shape
    tm, tn, tk = _pick(m, 512), _pick(n, 1024), _pick(k, 1024)
    nk = k // tk

    def body(a_ref, b_ref, o_ref, acc_ref):
        kk = pl.program_id(2)

        @pl.when(kk == 0)
        def _():
            acc_ref[...] = jnp.zeros_like(acc_ref)

        acc_ref[...] += lax.dot_general(
            a_ref[...].astype(BF), b_ref[...].astype(BF), _DIMS[mode], preferred_element_type=F32)

        @pl.when(kk == nk - 1)
        def _():
            o_ref[...] = acc_ref[...].astype(o_ref.dtype)

    a_spec = (pl.BlockSpec((tk, tm), lambda i, j, kk: (kk, i)) if mode == "tn"
              else pl.BlockSpec((tm, tk), lambda i, j, kk: (i, kk)))
    b_spec = (pl.BlockSpec((tn, tk), lambda i, j, kk: (j, kk)) if mode == "nt"
              else pl.BlockSpec((tk, tn), lambda i, j, kk: (kk, j)))
    return pl.pallas_call(
        body, name=name,
        out_shape=jax.ShapeDtypeStruct((m, n), out_dtype),
        grid=(m // tm, n // tn, nk),
        in_specs=[a_spec, b_spec],
        out_specs=pl.BlockSpec((tm, tn), lambda i, j, kk: (i, j)),
        scratch_shapes=[pltpu.VMEM((tm, tn), F32)],
        compiler_params=pltpu.CompilerParams(
            dimension_semantics=("parallel", "parallel", "arbitrary"),
            vmem_limit_bytes=VMEM_LIMIT),
    )(a, b)


def _reduce_slots(recv, name):
    _, r, w = recv.shape
    tr = _pick(r, 256)

    def body(r_ref, o_ref):
        acc = r_ref[0].astype(F32)
        for s in range(1, N_DEV):
            acc = acc + r_ref[s].astype(F32)
        o_ref[...] = acc

    return pl.pallas_call(
        body, name=name,
        out_shape=jax.ShapeDtypeStruct((r, w), F32),
        grid=(r // tr,),
        in_specs=[pl.BlockSpec((N_DEV, tr, w), lambda i: (0, i, 0))],
        out_specs=pl.BlockSpec((tr, w), lambda i: (i, 0)),
        compiler_params=pltpu.CompilerParams(dimension_semantics=("parallel",)),
    )(recv)


def _adamw(w, g, m, v, name):
    shape = w.shape
    cols = shape[-1]
    rows = math.prod(shape[:-1])
    w2, g2, m2, v2 = (t.reshape(rows, cols) for t in (w, g, m, v))
    tr = _pick(rows, 128)
    c1 = 1.0 - ADAM_B1 ** ADAM_STEP
    c2 = 1.0 - ADAM_B2 ** ADAM_STEP

    def body(w_ref, g_ref, m_ref, v_ref, d_ref, nm_ref, nv_ref):
        gg = g_ref[...]
        nm = ADAM_B1 * m_ref[...] + (1.0 - ADAM_B1) * gg
        nv = ADAM_B2 * v_ref[...] + (1.0 - ADAM_B2) * (gg * gg)
        m_hat = nm / c1
        v_hat = nv / c2
        d_ref[...] = -ADAM_LR * (m_hat / (jnp.sqrt(v_hat) + ADAM_EPS) + ADAM_WD * w_ref[...])
        nm_ref[...] = nm
        nv_ref[...] = nv

    spec = pl.BlockSpec((tr, cols), lambda i: (i, 0))
    out = pl.pallas_call(
        body, name=name,
        out_shape=[jax.ShapeDtypeStruct((rows, cols), F32)] * 3,
        grid=(rows // tr,),
        in_specs=[spec] * 4, out_specs=[spec] * 3,
        compiler_params=pltpu.CompilerParams(dimension_semantics=("parallel",)),
    )(w2, g2, m2, v2)
    return tuple(t.reshape(shape) for t in out)


_VJP = {"nn": (("nt", "gb"), ("tn", "ag")),
        "nt": (("nn", "gb"), ("tn", "ga")),
        "tn": (("nt", "bg"), ("nn", "ag"))}


def _make_dot(cast, precision):
    def raw(mode, a, b):
        return lax.dot_general(cast(a), cast(b), _DIMS[mode], precision=precision,
                               preferred_element_type=F32)

    @functools.partial(jax.custom_vjp, nondiff_argnums=(0,))
    def dot(mode, a, b):
        return raw(mode, a, b)

    def fwd(mode, a, b):
        return raw(mode, a, b), (a, b)

    def bwd(mode, res, g):
        a, b = res
        pick = {"a": a, "b": b, "g": g}
        (ma, ta), (mb, tb) = _VJP[mode]
        return dot(ma, pick[ta[0]], pick[ta[1]]), dot(mb, pick[tb[0]], pick[tb[1]])

    dot.defvjp(fwd, bwd)
    return dot


bdot = _make_dot(lambda t: t.astype(BF), None)
hdot = _make_dot(lambda t: t, lax.Precision.HIGHEST)


def _sigmoid(x):
    return 1.0 / (1.0 + jnp.exp(-x))


def _softplus(x):
    return jnp.maximum(x, 0.0) + jnp.log(1.0 + jnp.exp(-jnp.abs(x)))


def _dn_chunk(S, xs, ba, z, cw, al, dt, dn):
    C = DN_C
    pre = xs[0] * cw[0] + xs[1] * cw[1] + xs[2] * cw[2] + xs[3] * cw[3]
    qkv = pre * _sigmoid(pre)
    lane = lax.broadcasted_iota(jnp.int32, (1, 128), 1)
    sub = lax.broadcasted_iota(jnp.int32, (C, 1), 0)
    row_i = lax.broadcasted_iota(jnp.int32, (C, C), 0)
    col_i = lax.broadcasted_iota(jnp.int32, (C, C), 1)
    strict = row_i > col_i
    incl = row_i >= col_i
    g_all = jnp.where((lane >= 4) & (lane < 8), -jnp.exp(al) * _softplus(ba + dt), 0.0)
    gc_all = hdot("nn", incl.astype(F32), g_all)
    gc_all_t = gc_all.T
    beta_all = _sigmoid(ba)
    glast_all = jnp.sum(jnp.where(sub == C - 1, gc_all, 0.0), axis=0, keepdims=True)
    eye = (row_i == col_i).astype(F32)
    ys, s_new = [], []
    for h in range(DN_HEADS):
        q = qkv[:, 128 * h:128 * (h + 1)]
        k = qkv[:, 512 + 128 * h:512 + 128 * (h + 1)]
        v = qkv[:, 1024 + 128 * h:1024 + 128 * (h + 1)]
        q = q * lax.rsqrt(jnp.sum(q * q, axis=1, keepdims=True) + EPS) * (DN_D ** -0.5)
        k = k * lax.rsqrt(jnp.sum(k * k, axis=1, keepdims=True) + EPS)
        beta = jnp.sum(jnp.where(lane == h, beta_all, 0.0), axis=1, keepdims=True)
        gc = jnp.sum(jnp.where(lane == 4 + h, gc_all, 0.0), axis=1, keepdims=True)
        gc_row = jnp.sum(jnp.where(sub == 4 + h, gc_all_t, 0.0), axis=0, keepdims=True)
        g_last = jnp.sum(jnp.where(lane == 4 + h, glast_all, 0.0), axis=1, keepdims=True)
        diff = gc - gc_row
        kb = k * beta
        L = jnp.where(strict, bdot("nt", kb, k) * jnp.exp(jnp.where(strict, diff, 0.0)), 0.0)
        t_inv = eye - L
        p = L
        for _ in range(6):
            p = hdot("nn", p, p)
            t_inv = t_inv + hdot("nn", t_inv, p)
        u = hdot("nn", t_inv, v * beta)
        w = hdot("nn", t_inv, kb * jnp.exp(gc))
        a_qk = jnp.where(incl, bdot("nt", q, k) * jnp.exp(jnp.where(incl, diff, 0.0)), 0.0)
        qg = q * jnp.exp(gc)
        kd = k * jnp.exp(g_last - gc)
        v_new = u - bdot("nn", w, S[h])
        o = bdot("nn", qg, S[h]) + bdot("nn", a_qk, v_new)
        s_new.append(S[h] * jnp.exp(g_last) + bdot("tn", kd, v_new))
        o = o * lax.rsqrt(jnp.mean(o * o, axis=1, keepdims=True) + EPS) * dn
        zh = z[:, 128 * h:128 * (h + 1)]
        ys.append(o * (zh * _sigmoid(zh)))
    return jnp.concatenate(ys, axis=1), tuple(s_new)


def _load_shifted(xbuf, x_ref, halo_ref, first):
    xbuf[0:HALO, :] = jnp.where(first, 0.0, halo_ref[...])
    xbuf[HALO:HALO + DN_C, :] = x_ref[...]
    return [xbuf[HALO - 3 + k:HALO - 3 + k + DN_C, :] for k in range(4)]


def dn_forward(cols, col_blocks, cw, al, dt, dn, name):
    T = cols.shape[0]
    n = T // DN_C
    jx, jz, jb = col_blocks

    def body(x_ref, halo_ref, z_ref, ba_ref, cw_ref, al_ref, dt_ref, dn_ref, y_ref, ss_ref, s_scr, xbuf):
        i = pl.program_id(0)

        @pl.when(i == 0)
        def _():
            s_scr[...] = jnp.zeros_like(s_scr)

        xs = _load_shifted(xbuf, x_ref, halo_ref, i == 0)
        ss_ref[0] = s_scr[...]
        S = [s_scr[h] for h in range(DN_HEADS)]
        cws = [cw_ref[k:k + 1, :] for k in range(4)]
        y, s_new = _dn_chunk(S, xs, ba_ref[...], z_ref[...], cws, al_ref[...], dt_ref[...], dn_ref[...])
        y_ref[...] = y
        for h in range(DN_HEADS):
            s_scr[h] = s_new[h]

    per = DN_C // HALO
    full = lambda shape: pl.BlockSpec(shape, lambda i: (0,) * len(shape))
    return pl.pallas_call(
        body, name=name,
        out_shape=[jax.ShapeDtypeStruct((T, 512), F32),
                   jax.ShapeDtypeStruct((n, DN_HEADS, DN_D, DN_D), F32)],
        grid=(n,),
        in_specs=[pl.BlockSpec((DN_C, 1536), lambda i: (i, jx)),
                  pl.BlockSpec((HALO, 1536), lambda i: (jnp.maximum(i * per - 1, 0), jx)),
                  pl.BlockSpec((DN_C, 512), lambda i: (i, jz)),
                  pl.BlockSpec((DN_C, 128), lambda i: (i, jb)),
                  full((4, 1536)), full((1, 128)), full((1, 128)), full((1, 128))],
        out_specs=[pl.BlockSpec((DN_C, 512), lambda i: (i, 0)),
                   pl.BlockSpec((1, DN_HEADS, DN_D, DN_D), lambda i: (i, 0, 0, 0))],
        scratch_shapes=[pltpu.VMEM((DN_HEADS, DN_D, DN_D), F32), pltpu.VMEM((HALO + DN_C, 1536), F32)],
        compiler_params=pltpu.CompilerParams(dimension_semantics=("arbitrary",)),
    )(cols, cols, cols, cols, cw, al, dt, dn)


def dn_backward(cols, col_blocks, cw, al, dt, dn, ss, dy, name):
    T = cols.shape[0]
    n = T // DN_C
    jx, jz, jb = col_blocks

    def body(x_ref, halo_ref, z_ref, ba_ref, cw_ref, al_ref, dt_ref, dn_ref, ss_ref, dy_ref,
             dx_ref, dz_ref, dba_ref, dcw_ref, dal_ref, ddt_ref, ddn_ref, ds_scr, xbuf, dbuf, carry):
        i = pl.program_id(0)

        @pl.when(i == 0)
        def _():
            ds_scr[...] = jnp.zeros_like(ds_scr)
            carry[...] = jnp.zeros_like(carry)
            dcw_ref[...] = jnp.zeros_like(dcw_ref)
            dal_ref[...] = jnp.zeros_like(dal_ref)
            ddt_ref[...] = jnp.zeros_like(ddt_ref)
            ddn_ref[...] = jnp.zeros_like(ddn_ref)

        xs = _load_shifted(xbuf, x_ref, halo_ref, i == n - 1)
        S = [ss_ref[0, h] for h in range(DN_HEADS)]
        cws = [cw_ref[k:k + 1, :] for k in range(4)]

        def f(S, xs, ba, z, cws, al, dt, dn):
            return _dn_chunk(S, xs, ba, z, cws, al, dt, dn)

        _, vjp = jax.vjp(f, S, xs, ba_ref[...], z_ref[...], cws, al_ref[...], dt_ref[...], dn_ref[...])
        dS, dxs, dba, dz, dcws, dal, ddt, ddn = vjp((dy_ref[...], tuple(ds_scr[h] for h in range(DN_HEADS))))
        for h in range(DN_HEADS):
            ds_scr[h] = dS[h]
        dbuf[...] = jnp.zeros_like(dbuf)
        for k in range(4):
            lo = HALO - 3 + k
            dbuf[lo:lo + DN_C, :] += dxs[k]
        dbuf[DN_C:DN_C + HALO, :] += carry[...]
        dx_ref[...] = dbuf[HALO:HALO + DN_C, :]
        carry[...] = dbuf[0:HALO, :]
        dz_ref[...] = dz
        dba_ref[...] = dba
        for k in range(4):
            dcw_ref[k:k + 1, :] += dcws[k]
        dal_ref[...] += dal
        ddt_ref[...] += ddt
        ddn_ref[...] += ddn

    per = DN_C // HALO
    rev = lambda i: n - 1 - i
    full = lambda shape: pl.BlockSpec(shape, lambda i: (0,) * len(shape))
    return pl.pallas_call(
        body, name=name,
        out_shape=[jax.ShapeDtypeStruct((T, 1536), F32), jax.ShapeDtypeStruct((T, 512), F32),
                   jax.ShapeDtypeStruct((T, 128), F32), jax.ShapeDtypeStruct((4, 1536), F32),
                   jax.ShapeDtypeStruct((1, 128), F32), jax.ShapeDtypeStruct((1, 128), F32),
                   jax.ShapeDtypeStruct((1, 128), F32)],
        grid=(n,),
        in_specs=[pl.BlockSpec((DN_C, 1536), lambda i: (rev(i), jx)),
                  pl.BlockSpec((HALO, 1536), lambda i: (jnp.maximum(rev(i) * per - 1, 0), jx)),
                  pl.BlockSpec((DN_C, 512), lambda i: (rev(i), jz)),
                  pl.BlockSpec((DN_C, 128), lambda i: (rev(i), jb)),
                  full((4, 1536)), full((1, 128)), full((1, 128)), full((1, 128)),
                  pl.BlockSpec((1, DN_HEADS, DN_D, DN_D), lambda i: (rev(i), 0, 0, 0)),
                  pl.BlockSpec((DN_C, 512), lambda i: (rev(i), 0))],
        out_specs=[pl.BlockSpec((DN_C, 1536), lambda i: (rev(i), 0)),
                   pl.BlockSpec((DN_C, 512), lambda i: (rev(i), 0)),
                   pl.BlockSpec((DN_C, 128), lambda i: (rev(i), 0)),
                   full((4, 1536)), full((1, 128)), full((1, 128)), full((1, 128))],
        scratch_shapes=[pltpu.VMEM((DN_HEADS, DN_D, DN_D), F32), pltpu.VMEM((HALO + DN_C, 1536), F32),
                        pltpu.VMEM((HALO + DN_C, 1536), F32), pltpu.VMEM((HALO, 1536), F32)],
        compiler_params=pltpu.CompilerParams(dimension_semantics=("arbitrary",)),
    )(cols, cols, cols, cols, cw, al, dt, dn, ss, dy)


def _full(shape):
    return pl.BlockSpec(shape, lambda i: (0,) * len(shape))


def _silu(x):
    return x * _sigmoid(x)


def _gelu(x):
    return 0.5 * x * (1.0 + jnp.tanh(0.7978845608028654 * (x + 0.044715 * (x * x * x))))


def _lane_col(mat, idx):
    lane = lax.broadcasted_iota(jnp.int32, (1, mat.shape[1]), 1)
    return jnp.sum(jnp.where(lane == idx, mat, 0.0), axis=1, keepdims=True)


def _gm_chunk(uv, z, gain, ws, bt):
    g = _gelu(uv)
    u, v = g[:, :512], g[:, 512:]
    v = v * lax.rsqrt(jnp.mean(v * v, axis=1, keepdims=True) + EPS) * gain
    row_i = lax.broadcasted_iota(jnp.int32, (BLK, BLK), 0)
    col_i = lax.broadcasted_iota(jnp.int32, (BLK, BLK), 1)
    causal = row_i >= col_i
    ss = []
    for grp in range(4):
        wg = jnp.where(causal, ws[grp], 0.0)
        ss.append(bdot("nn", wg, v[:, BLK * grp:BLK * (grp + 1)]) + _lane_col(bt, grp))
    return u * jnp.concatenate(ss, axis=1) * _silu(z)


def gm_forward(cols, col_blocks, gain, ws, bt, name):
    T = cols.shape[0]
    juv, jz = col_blocks

    def body(uv_ref, z_ref, gain_ref, ws_ref, bt_ref, y_ref):
        y_ref[...] = _gm_chunk(uv_ref[...], z_ref[...], gain_ref[...], [ws_ref[g] for g in range(4)], bt_ref[...])

    return pl.pallas_call(
        body, name=name, out_shape=jax.ShapeDtypeStruct((T, 512), F32), grid=(T // BLK,),
        in_specs=[pl.BlockSpec((BLK, 1024), lambda i: (i, juv)), pl.BlockSpec((BLK, 512), lambda i: (i, jz)),
                  _full((1, 512)), _full((4, BLK, BLK)), _full((BLK, BLK))],
        out_specs=pl.BlockSpec((BLK, 512), lambda i: (i, 0)),
        compiler_params=pltpu.CompilerParams(dimension_semantics=("parallel",)),
    )(cols, cols, gain, ws, bt)


def gm_backward(cols, col_blocks, gain, ws, bt, dy, name):
    T = cols.shape[0]
    juv, jz = col_blocks

    def body(uv_ref, z_ref, gain_ref, ws_ref, bt_ref, dy_ref, duv_ref, dz_ref, dgain_ref, dws_ref, dbt_ref):
        @pl.when(pl.program_id(0) == 0)
        def _():
            dgain_ref[...] = jnp.zeros_like(dgain_ref)
            dws_ref[...] = jnp.zeros_like(dws_ref)
            dbt_ref[...] = jnp.zeros_like(dbt_ref)

        _, vjp = jax.vjp(_gm_chunk, uv_ref[...], z_ref[...], gain_ref[...], [ws_ref[g] for g in range(4)], bt_ref[...])
        duv, dz, dgain, dws, dbt = vjp(dy_ref[...])
        duv_ref[...] = duv
        dz_ref[...] = dz
        dgain_ref[...] += dgain
        for g in range(4):
            dws_ref[g] += dws[g]
        dbt_ref[...] += dbt

    return pl.pallas_call(
        body, name=name,
        out_shape=[jax.ShapeDtypeStruct((T, 1024), F32), jax.ShapeDtypeStruct((T, 512), F32),
                   jax.ShapeDtypeStruct((1, 512), F32), jax.ShapeDtypeStruct((4, BLK, BLK), F32),
                   jax.ShapeDtypeStruct((BLK, BLK), F32)],
        grid=(T // BLK,),
        in_specs=[pl.BlockSpec((BLK, 1024), lambda i: (i, juv)), pl.BlockSpec((BLK, 512), lambda i: (i, jz)),
                  _full((1, 512)), _full((4, BLK, BLK)), _full((BLK, BLK)), pl.BlockSpec((BLK, 512), lambda i: (i, 0))],
        out_specs=[pl.BlockSpec((BLK, 1024), lambda i: (i, 0)), pl.BlockSpec((BLK, 512), lambda i: (i, 0)),
                   _full((1, 512)), _full((4, BLK, BLK)), _full((BLK, BLK))],
        compiler_params=pltpu.CompilerParams(dimension_semantics=("arbitrary",)),
    )(cols, cols, gain, ws, bt, dy)


def _sw_block(first, q, kp, kc, vp, vc, z, sinks):
    P = BLK
    lane = lax.broadcasted_iota(jnp.int32, (1, 128), 1)
    r = lax.broadcasted_iota(jnp.int32, (128, 128), 0)
    c = lax.broadcasted_iota(jnp.int32, (128, 128), 1)
    swap = (c == (r + 64) % 128).astype(F32)
    k2 = jnp.concatenate([kp, kc], axis=0)
    v2 = jnp.concatenate([vp, vc], axis=0)
    k2s = bdot("nn", k2, swap)
    v2s = bdot("nn", v2, swap)
    qi = lax.broadcasted_iota(jnp.int32, (P, 2 * P), 0)
    kj = lax.broadcasted_iota(jnp.int32, (P, 2 * P), 1)
    dist = qi + P - kj
    valid = (dist >= 0) & (dist < P) & ((kj >= P) | jnp.logical_not(first))
    outs = []
    for j in range(4):
        acc = jnp.zeros((P, 128), F32)
        for half in range(2):
            h = 2 * j + half
            kv = h // 4
            in_half = (lane >= 64 * half) & (lane < 64 * half + 64)
            qh = jnp.where(in_half, q[:, 128 * j:128 * (j + 1)], 0.0)
            same = (half == kv)
            s = bdot("nt", qh, k2 if same else k2s) * (64 ** -0.5)
            s = jnp.where(valid, s, NEG_INF)
            sink = _lane_col(sinks, h)
            m = lax.stop_gradient(jnp.maximum(jnp.max(s, axis=1, keepdims=True), sink))
            e = jnp.exp(s - m)
            p = e / (jnp.sum(e, axis=1, keepdims=True) + jnp.exp(sink - m))
            o = bdot("nn", p, v2 if same else v2s)
            acc = acc + jnp.where(in_half, o, 0.0)
        outs.append(acc)
    return jnp.concatenate(outs, axis=1) * _silu(z)


def _sw_specs(col_blocks, idx):
    jq, jk, jv, jz = col_blocks
    prev = lambda i: jnp.maximum(idx(i) - 1, 0)
    return [pl.BlockSpec((BLK, 512), lambda i: (idx(i), jq)),
            pl.BlockSpec((BLK, 128), lambda i: (prev(i), jk)), pl.BlockSpec((BLK, 128), lambda i: (idx(i), jk)),
            pl.BlockSpec((BLK, 128), lambda i: (prev(i), jv)), pl.BlockSpec((BLK, 128), lambda i: (idx(i), jv)),
            pl.BlockSpec((BLK, 512), lambda i: (idx(i), jz)), _full((1, 128))]


def sw_forward(cols, col_blocks, sinks, name):
    T = cols.shape[0]

    def body(q_ref, kp_ref, kc_ref, vp_ref, vc_ref, z_ref, s_ref, y_ref):
        y_ref[...] = _sw_block(pl.program_id(0) == 0, q_ref[...], kp_ref[...], kc_ref[...], vp_ref[...], vc_ref[...],
                               z_ref[...], s_ref[...])

    return pl.pallas_call(
        body, name=name, out_shape=jax.ShapeDtypeStruct((T, 512), F32), grid=(T // BLK,),
        in_specs=_sw_specs(col_blocks, lambda i: i),
        out_specs=pl.BlockSpec((BLK, 512), lambda i: (i, 0)),
        compiler_params=pltpu.CompilerParams(dimension_semantics=("parallel",)),
    )(cols, cols, cols, cols, cols, cols, sinks)


def sw_backward(cols, col_blocks, sinks, dy, name):
    T = cols.shape[0]
    n = T // BLK
    rev = lambda i: n - 1 - i

    def body(q_ref, kp_ref, kc_ref, vp_ref, vc_ref, z_ref, s_ref, dy_ref,
             dq_ref, dk_ref, dv_ref, dz_ref, ds_ref, kcarry, vcarry):
        i = pl.program_id(0)

        @pl.when(i == 0)
        def _():
            kcarry[...] = jnp.zeros_like(kcarry)
            vcarry[...] = jnp.zeros_like(vcarry)
            ds_ref[...] = jnp.zeros_like(ds_ref)

        f = functools.partial(_sw_block, i == n - 1)
        _, vjp = jax.vjp(f, q_ref[...], kp_ref[...], kc_ref[...], vp_ref[...], vc_ref[...], z_ref[...], s_ref[...])
        dq, dkp, dkc, dvp, dvc, dz, dsk = vjp(dy_ref[...])
        dq_ref[...] = dq
        dz_ref[...] = dz
        dk_ref[...] = dkc + kcarry[...]
        dv_ref[...] = dvc + vcarry[...]
        kcarry[...] = dkp
        vcarry[...] = dvp
        ds_ref[...] += dsk

    return pl.pallas_call(
        body, name=name,
        out_shape=[jax.ShapeDtypeStruct((T, 512), F32), jax.ShapeDtypeStruct((T, 128), F32),
                   jax.ShapeDtypeStruct((T, 128), F32), jax.ShapeDtypeStruct((T, 512), F32),
                   jax.ShapeDtypeStruct((1, 128), F32)],
        grid=(n,),
        in_specs=_sw_specs(col_blocks, rev) + [pl.BlockSpec((BLK, 512), lambda i: (rev(i), 0))],
        out_specs=[pl.BlockSpec((BLK, 512), lambda i: (rev(i), 0)), pl.BlockSpec((BLK, 128), lambda i: (rev(i), 0)),
                   pl.BlockSpec((BLK, 128), lambda i: (rev(i), 0)), pl.BlockSpec((BLK, 512), lambda i: (rev(i), 0)),
                   _full((1, 128))],
        scratch_shapes=[pltpu.VMEM((BLK, 128), F32), pltpu.VMEM((BLK, 128), F32)],
        compiler_params=pltpu.CompilerParams(dimension_semantics=("arbitrary",)),
    )(cols, cols, cols, cols, cols, cols, sinks, dy)


XM_TQ = 256


def _xm_block(q, z, mkv):
    outs = []
    for h in range(4):
        s = bdot("nt", q[:, 128 * h:128 * (h + 1)], mkv[:, 128 * h:128 * (h + 1)]) * (128 ** -0.5)
        m = lax.stop_gradient(jnp.max(s, axis=1, keepdims=True))
        e = jnp.exp(s - m)
        p = e / jnp.sum(e, axis=1, keepdims=True)
        outs.append(bdot("nn", p, mkv[:, 512 + 128 * h:512 + 128 * (h + 1)]))
    return jnp.concatenate(outs, axis=1) * _silu(z)


def xm_forward(cols, col_blocks, mkv, name):
    T = cols.shape[0]
    jq, jz = col_blocks

    def body(q_ref, z_ref, m_ref, y_ref):
        y_ref[...] = _xm_block(q_ref[...], z_ref[...], m_ref[...])

    return pl.pallas_call(
        body, name=name, out_shape=jax.ShapeDtypeStruct((T, 512), F32), grid=(T // XM_TQ,),
        in_specs=[pl.BlockSpec((XM_TQ, 512), lambda i: (i, jq)), pl.BlockSpec((XM_TQ, 512), lambda i: (i, jz)),
                  _full(mkv.shape)],
        out_specs=pl.BlockSpec((XM_TQ, 512), lambda i: (i, 0)),
        compiler_params=pltpu.CompilerParams(dimension_semantics=("parallel",)),
    )(cols, cols, mkv)


def xm_backward(cols, col_blocks, mkv, dy, name):
    T = cols.shape[0]
    jq, jz = col_blocks

    def body(q_ref, z_ref, m_ref, dy_ref, dq_ref, dz_ref, dm_ref):
        @pl.when(pl.program_id(0) == 0)
        def _():
            dm_ref[...] = jnp.zeros_like(dm_ref)

        _, vjp = jax.vjp(_xm_block, q_ref[...], z_ref[...], m_ref[...])
        dq, dz, dm = vjp(dy_ref[...])
        dq_ref[...] = dq
        dz_ref[...] = dz
        dm_ref[...] += dm

    return pl.pallas_call(
        body, name=name,
        out_shape=[jax.ShapeDtypeStruct((T, 512), F32), jax.ShapeDtypeStruct((T, 512), F32),
                   jax.ShapeDtypeStruct(mkv.shape, F32)],
        grid=(T // XM_TQ,),
        in_specs=[pl.BlockSpec((XM_TQ, 512), lambda i: (i, jq)), pl.BlockSpec((XM_TQ, 512), lambda i: (i, jz)),
                  _full(mkv.shape), pl.BlockSpec((XM_TQ, 512), lambda i: (i, 0))],
        out_specs=[pl.BlockSpec((XM_TQ, 512), lambda i: (i, 0)), pl.BlockSpec((XM_TQ, 512), lambda i: (i, 0)),
                   _full(mkv.shape)],
        compiler_params=pltpu.CompilerParams(dimension_semantics=("arbitrary",)),
    )(cols, cols, mkv, dy)


def _rms(x, gain):
    return x * lax.rsqrt(jnp.mean(x * x, axis=1, keepdims=True) + EPS) * gain


def memkv_forward(mem, gain, w, name):
    def body(m_ref, g_ref, w_ref, o_ref):
        o_ref[...] = bdot("nn", _rms(m_ref[...], g_ref[...]), w_ref[...])

    return pl.pallas_call(body, name=name, out_shape=jax.ShapeDtypeStruct(mem.shape, F32),
                          compiler_params=pltpu.CompilerParams(vmem_limit_bytes=VMEM_LIMIT))(mem, gain, w)


def memkv_backward(mem, gain, w, dkv, name):
    def body(m_ref, g_ref, w_ref, d_ref, dg_ref, dw_ref):
        mem_v = m_ref[...]
        _, vjp = jax.vjp(lambda g, ww: bdot("nn", _rms(mem_v, g), ww), g_ref[...], w_ref[...].astype(F32))
        dg, dw = vjp(d_ref[...])
        dg_ref[...] = dg
        dw_ref[...] = dw

    return pl.pallas_call(body, name=name,
                          out_shape=[jax.ShapeDtypeStruct(gain.shape, F32), jax.ShapeDtypeStruct(w.shape, F32)],
                          compiler_params=pltpu.CompilerParams(vmem_limit_bytes=VMEM_LIMIT))(mem, gain, w, dkv)


MG_TB = 128


def _merge_block(ys, gl, wup, wout, gpost):
    merged = None
    for n in range(4):
        t = _sigmoid(gl[:, 1024 * n:1024 * (n + 1)]) * bdot("nn", ys[n], wup[n])
        merged = t if merged is None else merged + t
    out = bdot("nn", merged, wout)
    return _rms(out, gpost)


def merge_forward(ys, cols, jgate, x, wup, wout, gpost, name):
    T = x.shape[0]
    TB = 256

    def body(ya, yb, yc, ym, gl_ref, x_ref, wup_ref, wout_ref, gp_ref, o_ref):
        upd = _merge_block([ya[...], yb[...], yc[...], ym[...]], gl_ref[...],
                           [wup_ref[n] for n in range(4)], wout_ref[...], gp_ref[...])
        o_ref[...] = x_ref[...] + upd

    yspec = pl.BlockSpec((TB, 512), lambda i: (i, 0))
    return pl.pallas_call(
        body, name=name, out_shape=jax.ShapeDtypeStruct((T, 1024), F32), grid=(T // TB,),
        in_specs=[yspec] * 4 + [pl.BlockSpec((TB, 4096), lambda i: (i, jgate)),
                                pl.BlockSpec((TB, 1024), lambda i: (i, 0)),
                                _full(wup.shape), _full(wout.shape), _full((1, 1024))],
        out_specs=pl.BlockSpec((TB, 1024), lambda i: (i, 0)),
        compiler_params=pltpu.CompilerParams(dimension_semantics=("parallel",), vmem_limit_bytes=VMEM_LIMIT),
    )(*ys, cols, x, wup, wout, gpost)


def merge_backward(ys, cols, jgate, wup, wout, gpost, dx, name):
    T = dx.shape[0]
    TB = MG_TB

    def body(ya, yb, yc, ym, gl_ref, wup_ref, wout_ref, gp_ref, dx_ref,
             dya, dyb, dyc, dym, dgl_ref, dwup_ref, dwout_ref, dgp_ref):
        @pl.when(pl.program_id(0) == 0)
        def _():
            dwup_ref[...] = jnp.zeros_like(dwup_ref)
            dwout_ref[...] = jnp.zeros_like(dwout_ref)
            dgp_ref[...] = jnp.zeros_like(dgp_ref)

        _, vjp = jax.vjp(_merge_block, [ya[...], yb[...], yc[...], ym[...]], gl_ref[...],
                         [wup_ref[n].astype(F32) for n in range(4)], wout_ref[...].astype(F32), gp_ref[...])
        dys, dgl, dwup, dwout, dgp = vjp(dx_ref[...])
        for ref, val in zip((dya, dyb, dyc, dym), dys):
            ref[...] = val
        dgl_ref[...] = dgl
        for n in range(4):
            dwup_ref[n] += dwup[n]
        dwout_ref[...] += dwout
        dgp_ref[...] += dgp

    yspec = pl.BlockSpec((TB, 512), lambda i: (i, 0))
    return pl.pallas_call(
        body, name=name,
        out_shape=[jax.ShapeDtypeStruct((T, 512), F32)] * 4 + [
            jax.ShapeDtypeStruct((T, 4096), F32), jax.ShapeDtypeStruct(wup.shape, F32),
            jax.ShapeDtypeStruct(wout.shape, F32), jax.ShapeDtypeStruct((1, 1024), F32)],
        grid=(T // TB,),
        in_specs=[yspec] * 4 + [pl.BlockSpec((TB, 4096), lambda i: (i, jgate)),
                                _full(wup.shape), _full(wout.shape), _full((1, 1024)),
                                pl.BlockSpec((TB, 1024), lambda i: (i, 0))],
        out_specs=[yspec] * 4 + [pl.BlockSpec((TB, 4096), lambda i: (i, 0)),
                                 _full(wup.shape), _full(wout.shape), _full((1, 1024))],
        compiler_params=pltpu.CompilerParams(dimension_semantics=("arbitrary",), vmem_limit_bytes=VMEM_LIMIT),
    )(*ys, cols, wup, wout, gpost, dx)


NB = 256


def prenorm_forward(x, gain, name):
    T = x.shape[0]

    def body(x_ref, g_ref, o_ref):
        o_ref[...] = _rms(x_ref[...], g_ref[...]).astype(BF)

    return pl.pallas_call(
        body, name=name, out_shape=jax.ShapeDtypeStruct(x.shape, BF), grid=(T // NB,),
        in_specs=[pl.BlockSpec((NB, 1024), lambda i: (i, 0)), _full((1, 1024))],
        out_specs=pl.BlockSpec((NB, 1024), lambda i: (i, 0)),
        compiler_params=pltpu.CompilerParams(dimension_semantics=("parallel",)),
    )(x, gain)


def prenorm_backward(x, gain, dh, dres, name):
    T = x.shape[0]

    def body(x_ref, g_ref, dh_ref, dr_ref, dx_ref, dg_ref):
        @pl.when(pl.program_id(0) == 0)
        def _():
            dg_ref[...] = jnp.zeros_like(dg_ref)

        _, vjp = jax.vjp(_rms, x_ref[...], g_ref[...])
        dxn, dg = vjp(dh_ref[...])
        dx_ref[...] = dr_ref[...] + dxn
        dg_ref[...] += dg

    spec = pl.BlockSpec((NB, 1024), lambda i: (i, 0))
    return pl.pallas_call(
        body, name=name,
        out_shape=[jax.ShapeDtypeStruct(x.shape, F32), jax.ShapeDtypeStruct((1, 1024), F32)], grid=(T // NB,),
        in_specs=[spec, _full((1, 1024)), spec, spec], out_specs=[spec, _full((1, 1024))],
        compiler_params=pltpu.CompilerParams(dimension_semantics=("arbitrary",)),
    )(x, gain, dh, dres)


def loss_head(y, target, name):
    T, D = y.shape

    def body(y_ref, t_ref, l_ref, d_ref):
        @pl.when(pl.program_id(0) == 0)
        def _():
            l_ref[...] = jnp.zeros_like(l_ref)

        err = y_ref[...] - t_ref[...]
        d_ref[...] = err * (1.0 / D)
        l_ref[...] += jnp.full(l_ref.shape, 0.5 * jnp.sum(jnp.mean(err * err, axis=1, keepdims=True)), F32)

    spec = pl.BlockSpec((NB, D), lambda i: (i, 0))
    return pl.pallas_call(
        body, name=name,
        out_shape=[jax.ShapeDtypeStruct((1, 128), F32), jax.ShapeDtypeStruct(y.shape, F32)], grid=(T // NB,),
        in_specs=[spec, spec], out_specs=[_full((1, 128)), spec],
        compiler_params=pltpu.CompilerParams(dimension_semantics=("arbitrary",)),
    )(y, target)


CB_DN = (0, 3, 78)
CB_GM = (2, 6)
CB_SW = (7, 76, 77, 16)
CB_XM = (17, 18)
CB_GATE = 1
_ALIGNED_FROM_NATURAL = ((0, 2048), (2056, 1024), (3080, 512), (3592, 512), (5896, 4096), (4360, 512),
                         (4872, 512), (5384, 512), (4104, 128), (4232, 128), (2048, 8))
_NATURAL_FROM_ALIGNED = ((0, 2048), (9984, 8), (2048, 1024), (3072, 512), (3584, 512), (9728, 128), (9856, 128),
                         (8192, 512), (8704, 512), (9216, 512), (4096, 4096))


def _pack_weight_slot(w_in, w_mem_kv, w_up, w_out):
    win = jnp.pad(w_in.astype(BF), ((0, 0), (0, 0), (0, W_IN_SHARD_PAD - W_IN_SHARD)))
    return jnp.concatenate([
        win.reshape(ROWS_WIN, 1024), w_mem_kv.astype(BF).reshape(ROWS_WMEM, 1024),
        w_up.astype(BF).reshape(ROWS_WUP, 1024), w_out.astype(BF).reshape(ROWS_WOUT, 1024)], axis=0)


def _unpack_gathered(gth):
    r0, r1, r2 = ROWS_WIN, ROWS_WIN + ROWS_WMEM, ROWS_WIN + ROWS_WMEM + ROWS_WUP
    win = gth[:, :r0].reshape(N_DEV, DEPTH, D_MODEL, W_IN_SHARD_PAD)[..., :W_IN_SHARD]
    win = jnp.transpose(win, (1, 2, 0, 3)).reshape(DEPTH, D_MODEL, D_IN)
    parts = [win[..., s:s + w] for s, w in _ALIGNED_FROM_NATURAL]
    used = sum(w for _, w in _ALIGNED_FROM_NATURAL)
    parts.append(jnp.zeros((DEPTH, D_MODEL, D_IN_AL - used), win.dtype))
    w_in_al = jnp.concatenate(parts, axis=-1)
    wmem = gth[:, r0:r1].reshape(N_DEV, DEPTH, 128, 1024)
    wmem = jnp.transpose(wmem, (1, 0, 2, 3)).reshape(DEPTH, 1024, 1024)
    wup = gth[:, r1:r2].reshape(N_DEV, DEPTH, N_BRANCH, BRANCH_W, 128)
    wup = jnp.transpose(wup, (1, 2, 3, 0, 4)).reshape(DEPTH, N_BRANCH, BRANCH_W, 1024)
    wout = gth[:, r2:].reshape(N_DEV, DEPTH, 128, 1024)
    wout = jnp.transpose(wout, (1, 0, 2, 3)).reshape(DEPTH, 1024, 1024)
    return w_in_al, wmem, wup, wout


def _pack_grad_slots(d_in_al, d_mem, d_up, d_out):
    din = jnp.concatenate([d_in_al[..., s:s + w] for s, w in _NATURAL_FROM_ALIGNED], axis=-1).astype(BF)
    din = din.reshape(DEPTH, D_MODEL, N_DEV, W_IN_SHARD)
    din = jnp.pad(din, ((0, 0), (0, 0), (0, 0), (0, W_IN_SHARD_PAD - W_IN_SHARD)))
    din = jnp.transpose(din, (2, 0, 1, 3)).reshape(N_DEV, ROWS_WIN, 1024)
    dmem = jnp.transpose(d_mem.astype(BF).reshape(DEPTH, N_DEV, 128, 1024), (1, 0, 2, 3)).reshape(N_DEV, ROWS_WMEM, 1024)
    dup = jnp.transpose(d_up.astype(BF).reshape(DEPTH, N_BRANCH, BRANCH_W, N_DEV, 128), (3, 0, 1, 2, 4))
    dup = dup.reshape(N_DEV, ROWS_WUP, 1024)
    dout = jnp.transpose(d_out.astype(BF).reshape(DEPTH, N_DEV, 128, 1024), (1, 0, 2, 3)).reshape(N_DEV, ROWS_WOUT, 1024)
    return jnp.concatenate([din, dmem, dup, dout], axis=1)


def _unpack_own_grads(g):
    r0, r1, r2 = ROWS_WIN, ROWS_WIN + ROWS_WMEM, ROWS_WIN + ROWS_WMEM + ROWS_WUP
    g_in = g[:r0].reshape(DEPTH, D_MODEL, W_IN_SHARD_PAD)[..., :W_IN_SHARD]
    g_mem = g[r0:r1].reshape(DEPTH, 128, 1024)
    g_up = g[r1:r2].reshape(DEPTH, N_BRANCH, BRANCH_W, 128)
    g_out = g[r2:].reshape(DEPTH, 128, 1024)
    return g_in, g_mem, g_up, g_out


SMALL_VEC_W = 1024


def _pack_small(parts):
    flat = jnp.concatenate([p.reshape(-1).astype(F32) for p in parts])
    n = flat.shape[0]
    rows = -(-n // SMALL_VEC_W)
    rows = -(-rows // 8) * 8
    return jnp.pad(flat, (0, rows * SMALL_VEC_W - n)).reshape(rows, SMALL_VEC_W)


def _unpack_small(vec, shapes):
    flat = vec.reshape(-1)
    out, off = [], 0
    for s in shapes:
        n = math.prod(s)
        out.append(flat[off:off + n].reshape(s))
        off += n
    return out


def _lanes(vec, at):
    return jnp.zeros((1, 128), F32).at[0, at:at + vec.shape[0]].set(vec)


SMALL_NAMES = ("norm_pre", "norm_post", "norm_mem", "a_log", "dt_bias", "dn_norm", "gm_norm",
               "spatial_w", "spatial_b", "sinks")


def _forward_backward(x, mem, target, small, conv_full, big):
    w_in_al, w_mem, w_up, w_out = big
    layers = []
    xl = x
    for l in range(DEPTH):
        t = "l%d_" % l
        p = dict(
            gpre=small["norm_pre"][l][None], gpost=small["norm_post"][l][None], gmem=small["norm_mem"][l][None],
            cw=conv_full[l], al=_lanes(small["a_log"][l], 4), dt=_lanes(small["dt_bias"][l], 4),
            dnn=small["dn_norm"][l][None], gain=small["gm_norm"][l][None], ws=small["spatial_w"][l],
            bt=jnp.zeros((128, 128), F32).at[:, :GM_GROUPS].set(small["spatial_b"][l].T),
            sinks=_lanes(small["sinks"][l], 0))
        h = prenorm_forward(xl, p["gpre"], t + "prenorm")
        cols = _matmul(h, w_in_al[l], "nn", F32, t + "w_in")
        mkv = memkv_forward(mem, p["gmem"], w_mem[l], t + "memkv")
        ya, ss = dn_forward(cols, CB_DN, p["cw"], p["al"], p["dt"], p["dnn"], t + "deltanet")
        yb = gm_forward(cols, CB_GM, p["gain"], p["ws"], p["bt"], t + "gmlp")
        yc = sw_forward(cols, CB_SW, p["sinks"], t + "swa")
        ym = xm_forward(cols, CB_XM, mkv, t + "memattn")
        xn = merge_forward([ya, yb, yc, ym], cols, CB_GATE, xl, w_up[l], w_out[l], p["gpost"], t + "merge")
        layers.append(dict(p, x=xl, h=h, cols=cols, mkv=mkv, ss=ss, ys=[ya, yb, yc, ym]))
        xl = xn
    loss, dx = loss_head(xl, target, "loss_head")

    gs = {n: [None] * DEPTH for n in SMALL_NAMES + ("conv_w", "w_in", "w_mem_kv", "w_up", "w_out")}
    for l in reversed(range(DEPTH)):
        t = "l%d_" % l
        s = layers[l]
        cols = s["cols"]
        dya, dyb, dyc, dym, dgl, dwup, dwout, dgpost = merge_backward(
            s["ys"], cols, CB_GATE, w_up[l], w_out[l], s["gpost"], dx, t + "merge_bwd")
        dq_m, dz_m, dmkv = xm_backward(cols, CB_XM, s["mkv"], dym, t + "memattn_bwd")
        dgmem, dwmem = memkv_backward(mem, s["gmem"], w_mem[l], dmkv, t + "memkv_bwd")
        dq_s, dk_s, dv_s, dz_s, dsinks = sw_backward(cols, CB_SW, s["sinks"], dyc, t + "swa_bwd")
        duv, dz_g, dgain, dws, dbt = gm_backward(cols, CB_GM, s["gain"], s["ws"], s["bt"], dyb, t + "gmlp_bwd")
        dqkv, dz_d, dba, dcw, dal, ddt, ddn = dn_backward(
            cols, CB_DN, s["cw"], s["al"], s["dt"], s["dnn"], s["ss"], dya, t + "deltanet_bwd")
        T = cols.shape[0]
        dcols = jnp.concatenate([dqkv, dz_d, duv, dz_g, dq_s, dgl, dz_s, dq_m, dz_m, dk_s, dv_s, dba,
                                 jnp.zeros((T, D_IN_AL - 10112), F32)], axis=1)
        dh = _matmul(dcols, w_in_al[l], "nt", F32, t + "w_in_bwd_x")
        gs["w_in"][l] = _matmul(s["h"], dcols, "tn", BF, t + "w_in_bwd_w")
        dx, dgpre = prenorm_backward(s["x"], s["gpre"], dh, dx, t + "prenorm_bwd")
        gs["norm_pre"][l], gs["norm_post"][l], gs["norm_mem"][l] = dgpre[0], dgpost[0], dgmem[0]
        gs["a_log"][l], gs["dt_bias"][l], gs["dn_norm"][l] = dal[0, 4:8], ddt[0, 4:8], ddn[0]
        gs["gm_norm"][l], gs["spatial_w"][l], gs["spatial_b"][l] = dgain[0], dws, dbt[:, :GM_GROUPS].T
        gs["sinks"][l], gs["conv_w"][l] = dsinks[0, :SW_HEADS], dcw
        gs["w_mem_kv"][l], gs["w_up"][l], gs["w_out"][l] = dwmem, dwup, dwout
    return loss, dx, {n: jnp.stack(v) for n, v in gs.items()}


def kernel(x, mem, norm_pre, norm_post, norm_mem, w_in, conv_w, a_log, dt_bias, dn_norm, gm_norm, spatial_w, spatial_b, sinks, w_mem_kv, w_up, w_out, loss_target, m_norm_pre, m_norm_post, m_norm_mem, m_w_in, m_conv_w, m_a_log, m_dt_bias, m_dn_norm, m_gm_norm, m_spatial_w, m_spatial_b, m_sinks, m_w_mem_kv, m_w_up, m_w_out, v_norm_pre, v_norm_post, v_norm_mem, v_w_in, v_conv_w, v_a_log, v_dt_bias, v_dn_norm, v_gm_norm, v_spatial_w, v_spatial_b, v_sinks, v_w_mem_kv, v_w_up, v_w_out):
    xi, yi, ci = _my_place()
    my_slot = 4 * xi + 2 * yi + ci
    conv_shard = conv_w.shape[-1]

    gathered = _all_gather_slots(_pack_weight_slot(w_in, w_mem_kv, w_up, w_out), "gather_weights")
    big = _unpack_gathered(gathered)
    conv_rows = _all_gather_rows_vmem(conv_w.reshape(DEPTH * CONV_W, conv_shard), "gather_conv_w")
    conv_full = jnp.transpose(conv_rows.reshape(N_DEV, DEPTH, CONV_W, conv_shard), (1, 2, 0, 3))
    conv_full = conv_full.reshape(DEPTH, CONV_W, N_DEV * conv_shard)

    small = dict(norm_pre=norm_pre, norm_post=norm_post, norm_mem=norm_mem, a_log=a_log,
                 dt_bias=dt_bias, dn_norm=dn_norm, gm_norm=gm_norm, spatial_w=spatial_w,
                 spatial_b=spatial_b, sinks=sinks)
    loss, dx, gs = _forward_backward(x[0], mem[0], loss_target[0], small, conv_full, big)

    recv = _exchange_slots(_pack_grad_slots(gs["w_in"], gs["w_mem_kv"], gs["w_up"], gs["w_out"]), "exchange_grads")
    g_in, g_mem, g_up, g_out = _unpack_own_grads(_reduce_slots(recv, "reduce_grads"))

    small_parts = [loss[0, :1]] + [gs[n] for n in SMALL_NAMES] + [gs["conv_w"]]
    small_shapes = [p.shape for p in small_parts]
    tot = _unpack_small(_all_reduce_vmem(_pack_small(small_parts), "all_reduce_small"), small_shapes)
    loss_tot = tot[0][0]
    g_small = dict(zip(SMALL_NAMES, tot[1:1 + len(SMALL_NAMES)]))
    g_conv = lax.dynamic_slice_in_dim(tot[-1], my_slot * conv_shard, conv_shard, axis=2)

    given = dict(norm_pre=(norm_pre, m_norm_pre, v_norm_pre), norm_post=(norm_post, m_norm_post, v_norm_post),
                 norm_mem=(norm_mem, m_norm_mem, v_norm_mem), a_log=(a_log, m_a_log, v_a_log),
                 dt_bias=(dt_bias, m_dt_bias, v_dt_bias), dn_norm=(dn_norm, m_dn_norm, v_dn_norm),
                 gm_norm=(gm_norm, m_gm_norm, v_gm_norm), spatial_w=(spatial_w, m_spatial_w, v_spatial_w),
                 spatial_b=(spatial_b, m_spatial_b, v_spatial_b), sinks=(sinks, m_sinks, v_sinks),
                 conv_w=(conv_w, m_conv_w, v_conv_w))
    packed_names = SMALL_NAMES + ("conv_w",)
    g_all = dict(g_small, conv_w=g_conv)
    pshapes = [given[n][0].shape for n in packed_names]
    pw, pm, pv = (_pack_small([given[n][i] for n in packed_names]) for i in range(3))
    pd, pnm, pnv = _adamw(pw, _pack_small([g_all[n] for n in packed_names]), pm, pv, "adamw_small")
    upd = {n: t for n, t in zip(packed_names, zip(_unpack_small(pd, pshapes), _unpack_small(pnm, pshapes),
                                                  _unpack_small(pnv, pshapes)))}
    upd["w_in"] = _adamw(w_in, g_in, m_w_in, v_w_in, "adamw_w_in")
    upd["w_mem_kv"] = _adamw(w_mem_kv, g_mem, m_w_mem_kv, v_w_mem_kv, "adamw_w_mem_kv")
    upd["w_up"] = _adamw(w_up, g_up, m_w_up, v_w_up, "adamw_w_up")
    upd["w_out"] = _adamw(w_out, g_out, m_w_out, v_w_out, "adamw_w_out")
    grads = dict(g_all, w_in=g_in, w_mem_kv=g_mem, w_up=g_up, w_out=g_out)

    order = ("norm_pre", "norm_post", "norm_mem", "w_in", "conv_w", "a_log", "dt_bias", "dn_norm",
             "gm_norm", "spatial_w", "spatial_b", "sinks", "w_mem_kv", "w_up", "w_out")
    return (loss_tot, dx[None], *[grads[n] for n in order], *[upd[n][0] for n in order],
            *[upd[n][1] for n in order], *[upd[n][2] for n in order])
```

```python
import functools
import math

import jax
import jax.numpy as jnp
from jax import lax
from jax.experimental import pallas as pl
from jax.experimental.pallas import tpu as pltpu

MESH = pl.DeviceIdType.MESH
N_DEV = 8

D_MODEL = 1024
DEPTH = 2
N_BRANCH = 4
BRANCH_W = 512
DN_HEADS = 4
CONV_W = 4
GM_GROUPS = 4
SW_HEADS = 8
EPS = 1e-6
NEG_INF = -1e30

D_IN = 9992
W_IN_SHARD = D_IN // N_DEV
W_IN_SHARD_PAD = 1280
D_IN_AL = 10752
DN_W, SW_W, GM_W, XM_W = 2560, 1536, 1536, 1024

ADAM_LR = 0.001
ADAM_B1 = 0.9
ADAM_B2 = 0.999
ADAM_EPS = 1e-08
ADAM_WD = 0.01
ADAM_STEP = 10

VMEM_LIMIT = 56 * 1024 * 1024

BF = jnp.bfloat16
F32 = jnp.float32
DN_C = 128
DN_D = 128
HALO = 8
BLK = 128


def _my_place():
    return lax.axis_index("x"), lax.axis_index("y"), lax.axis_index("c")


_ANY = pl.BlockSpec(memory_space=pl.ANY)


def _all_gather_slots(parts, name):
    n = len(parts)

    def body(*refs):
        p_refs, out_refs = refs[:n], refs[n:2 * n]
        send_sems, recv_sems, local_sems = refs[2 * n:]
        x, y, c = _my_place()
        me, sibling = (x, y, c), (x, y, 1 - c)
        chips = [(1 - x, y), (x, 1 - y), (1 - x, 1 - y)]

        def copy(a, k, block, to, src=None):
            px, py, pc = block
            slot = out_refs[a].at[4 * px + 2 * py + pc]
            return pltpu.make_async_remote_copy(
                src_ref=slot if src is None else src, dst_ref=slot,
                send_sem=send_sems.at[7 * a + k], recv_sem=recv_sems.at[7 * a + k],
                device_id=to, device_id_type=MESH)

        mine = [pltpu.make_async_copy(p_refs[a], out_refs[a].at[4 * x + 2 * y + c], local_sems.at[a])
                for a in range(n)]
        for cp in mine:
            cp.start()
        first = []
        for a in range(n):
            first.append(copy(a, 0, me, sibling, src=p_refs[a]))
            first += [copy(a, 1 + j, me, (*chip, c), src=p_refs[a]) for j, chip in enumerate(chips)]
        for cp in first:
            cp.start()
        passed = []
        for j, chip in enumerate(chips):
            for a in range(n):
                copy(a, 1 + j, (*chip, c), me).wait_recv()
                fwd = copy(a, 4 + j, (*chip, c), sibling)
                fwd.start()
                passed.append(fwd)
        for a in range(n):
            copy(a, 0, sibling, me).wait_recv()
            for j, chip in enumerate(chips):
                copy(a, 4 + j, (*chip, 1 - c), me).wait_recv()
        for cp in first + passed:
            cp.wait_send()
        for cp in mine:
            cp.wait()

    return pl.pallas_call(
        body, name=name,
        out_shape=[jax.ShapeDtypeStruct((N_DEV,) + p.shape, p.dtype) for p in parts],
        in_specs=[_ANY] * n, out_specs=[_ANY] * n,
        scratch_shapes=[pltpu.SemaphoreType.DMA((7 * n,)), pltpu.SemaphoreType.DMA((7 * n,)),
                        pltpu.SemaphoreType.DMA((n,))],
    )(*parts)


def _exchange_sibling(parts, name):
    n = len(parts)

    def body(*refs):
        g_refs, out_refs = refs[:n], refs[n:2 * n]
        send_sems, recv_sems = refs[2 * n:]
        x, y, c = _my_place()
        copies = [pltpu.make_async_remote_copy(
            src_ref=g_refs[a].at[:, 1 - c], dst_ref=out_refs[a],
            send_sem=send_sems.at[a], recv_sem=recv_sems.at[a],
            device_id=(x, y, 1 - c), device_id_type=MESH) for a in range(n)]
        for cp in copies:
            cp.start()
        for cp in copies:
            cp.wait()

    return pl.pallas_call(
        body, name=name,
        out_shape=[jax.ShapeDtypeStruct((4,) + g.shape[2:], g.dtype) for g in parts],
        in_specs=[_ANY] * n, out_specs=[_ANY] * n,
        scratch_shapes=[pltpu.SemaphoreType.DMA((n,)), pltpu.SemaphoreType.DMA((n,))],
    )(*parts)


def _exchange_chips(parts, name):
    n = len(parts)

    def body(*refs):
        p_refs, out_refs = refs[:n], refs[n:2 * n]
        send_sems, recv_sems, local_sems = refs[2 * n:]
        x, y, c = _my_place()
        my_chip = 2 * x + y
        own = [pltpu.make_async_copy(p_refs[a].at[my_chip], out_refs[a].at[my_chip], local_sems.at[a])
               for a in range(n)]
        for cp in own:
            cp.start()
        sends, recvs = [], []
        for k in range(1, 4):
            px = 1 - x if (k >> 1) & 1 else x
            py = 1 - y if k & 1 else y
            peer_chip = 2 * px + py
            for a in range(n):
                sem = 3 * a + k - 1
                sends.append(pltpu.make_async_remote_copy(
                    src_ref=p_refs[a].at[peer_chip], dst_ref=out_refs[a].at[my_chip],
                    send_sem=send_sems.at[sem], recv_sem=recv_sems.at[sem],
                    device_id=(px, py, c), device_id_type=MESH))
                recvs.append(pltpu.make_async_remote_copy(
                    src_ref=p_refs[a].at[peer_chip], dst_ref=out_refs[a].at[peer_chip],
                    send_sem=send_sems.at[sem], recv_sem=recv_sems.at[sem],
                    device_id=(px, py, c), device_id_type=MESH))
        for cp in sends:
            cp.start()
        for send, recv in zip(sends, recvs):
            send.wait_send()
            recv.wait_recv()
        for cp in own:
            cp.wait()

    return pl.pallas_call(
        body, name=name,
        out_shape=[jax.ShapeDtypeStruct(p.shape, p.dtype) for p in parts],
        in_specs=[_ANY] * n, out_specs=[_ANY] * n,
        scratch_shapes=[pltpu.SemaphoreType.DMA((3 * n,)), pltpu.SemaphoreType.DMA((3 * n,)),
                        pltpu.SemaphoreType.DMA((n,))],
    )(*parts)


def _all_gather_rows_vmem(v, name):
    m_per, n = v.shape

    def body(v_ref, out_ref, send_sems, recv_sems, local_sem):
        x, y, c = _my_place()
        me, sibling = (x, y, c), (x, y, 1 - c)
        chips = [(1 - x, y), (x, 1 - y), (1 - x, 1 - y)]

        def rows(px, py, pc):
            return out_ref.at[pl.ds((4 * px + 2 * py + pc) * m_per, m_per), :]

        def copy(k, block, to, src=None):
            return pltpu.make_async_remote_copy(
                src_ref=rows(*block) if src is None else src, dst_ref=rows(*block),
                send_sem=send_sems.at[k], recv_sem=recv_sems.at[k],
                device_id=to, device_id_type=MESH)

        mine = pltpu.make_async_copy(v_ref, rows(*me), local_sem)
        mine.start()
        first = [copy(0, me, sibling, src=v_ref)]
        first += [copy(1 + j, me, (*chip, c), src=v_ref) for j, chip in enumerate(chips)]
        for cp in first:
            cp.start()
        passed = [copy(4 + j, (*chip, c), sibling) for j, chip in enumerate(chips)]
        for j, chip in enumerate(chips):
            copy(1 + j, (*chip, c), me).wait_recv()
            passed[j].start()
        copy(0, sibling, me).wait_recv()
        for j, chip in enumerate(chips):
            copy(4 + j, (*chip, 1 - c), me).wait_recv()
        for cp in first + passed:
            cp.wait_send()
        mine.wait()

    return pl.pallas_call(
        body, name=name,
        out_shape=jax.ShapeDtypeStruct((N_DEV * m_per, n), v.dtype),
        in_specs=[pl.BlockSpec(memory_space=pltpu.VMEM)],
        out_specs=pl.BlockSpec(memory_space=pltpu.VMEM),
        scratch_shapes=[pltpu.SemaphoreType.DMA((7,)), pltpu.SemaphoreType.DMA((7,)),
                        pltpu.SemaphoreType.DMA(())],
    )(v)


def _all_reduce_vmem(v, name):
    def body(v_ref, out_ref, buf, send_sems, recv_sems):
        x, y, c = _my_place()
        peers = [(x, y, 1 - c), (1 - x, y, c), (x, 1 - y, c)]
        out_ref[...] = v_ref[...]
        for step, peer in enumerate(peers):
            cp = pltpu.make_async_remote_copy(
                src_ref=out_ref, dst_ref=buf.at[step],
                send_sem=send_sems.at[step], recv_sem=recv_sems.at[step],
                device_id=peer, device_id_type=MESH)
            cp.start()
            cp.wait()
            out_ref[...] = out_ref[...] + buf[step]

    return pl.pallas_call(
        body, name=name,
        out_shape=jax.ShapeDtypeStruct(v.shape, v.dtype),
        in_specs=[pl.BlockSpec(memory_space=pltpu.VMEM)],
        out_specs=pl.BlockSpec(memory_space=pltpu.VMEM),
        scratch_shapes=[pltpu.VMEM((3,) + v.shape, v.dtype),
                        pltpu.SemaphoreType.DMA((3,)), pltpu.SemaphoreType.DMA((3,))],
    )(v)


def _pick(n, pref):
    if n <= pref:
        return n
    t = pref - pref % 128
    while t > 0 and n % t:
        t -= 128
    return t if t > 0 else n


_DIMS = {"nn": (((1,), (0,)), ((), ())),
         "nt": (((1,), (1,)), ((), ())),
         "tn": (((0,), (0,)), ((), ()))}


def _matmul(a, b, mode, out_dtype, name):
    if mode == "nn":
        (m, k), (_, n) = a.shape, b.shape
    elif mode == "nt":
        (m, k), (n, _) = a.shape, b.shape
    else:
        (k, m), (_, n) = a.shape, b.shape
    tm, tn, tk = _pick(m, 512), _pick(n, 1536), _pick(k, 1024)
    nk = k // tk

    def body(a_ref, b_ref, o_ref, acc_ref):
        kk = pl.program_id(2)

        @pl.when(kk == 0)
        def _():
            acc_ref[...] = jnp.zeros_like(acc_ref)

        acc_ref[...] += lax.dot_general(
            a_ref[...].astype(BF), b_ref[...].astype(BF), _DIMS[mode], preferred_element_type=F32)

        @pl.when(kk == nk - 1)
        def _():
            o_ref[...] = acc_ref[...].astype(o_ref.dtype)

    a_spec = (pl.BlockSpec((tk, tm), lambda i, j, kk: (kk, i)) if mode == "tn"
              else pl.BlockSpec((tm, tk), lambda i, j, kk: (i, kk)))
    b_spec = (pl.BlockSpec((tn, tk), lambda i, j, kk: (j, kk)) if mode == "nt"
              else pl.BlockSpec((tk, tn), lambda i, j, kk: (kk, j)))
    return pl.pallas_call(
        body, name=name,
        out_shape=jax.ShapeDtypeStruct((m, n), out_dtype),
        grid=(m // tm, n // tn, nk),
        in_specs=[a_spec, b_spec],
        out_specs=pl.BlockSpec((tm, tn), lambda i, j, kk: (i, j)),
        scratch_shapes=[pltpu.VMEM((tm, tn), F32)],
        compiler_params=pltpu.CompilerParams(
            dimension_semantics=("parallel", "parallel", "arbitrary"),
            vmem_limit_bytes=VMEM_LIMIT),
    )(a, b)


def _rows2d(t, lead):
    return t.reshape(t.shape[:lead] + (math.prod(t.shape[lead:-1]), t.shape[-1]))


def _pair_sum(g, theirs, name):
    g3, t3 = _rows2d(g, 2), _rows2d(theirs, 1)
    _, r, w = t3.shape
    tr = _pick(r, 512)

    def body(g_ref, t_ref, o_ref):
        c = lax.axis_index("c")
        mine = jnp.where(c == 0, g_ref[0, 0], g_ref[0, 1])
        o_ref[0] = (mine.astype(F32) + t_ref[0].astype(F32)).astype(o_ref.dtype)

    out = pl.pallas_call(
        body, name=name,
        out_shape=jax.ShapeDtypeStruct(t3.shape, t3.dtype),
        grid=(4, r // tr),
        in_specs=[pl.BlockSpec((1, 2, tr, w), lambda q, i: (q, 0, i, 0)),
                  pl.BlockSpec((1, tr, w), lambda q, i: (q, i, 0))],
        out_specs=pl.BlockSpec((1, tr, w), lambda q, i: (q, i, 0)),
        compiler_params=pltpu.CompilerParams(dimension_semantics=("parallel", "parallel")),
    )(g3, t3)
    return out.reshape(theirs.shape)


def _adam_update(w, g, m, v):
    c1 = 1.0 - ADAM_B1 ** ADAM_STEP
    c2 = 1.0 - ADAM_B2 ** ADAM_STEP
    nm = ADAM_B1 * m + (1.0 - ADAM_B1) * g
    nv = ADAM_B2 * v + (1.0 - ADAM_B2) * (g * g)
    delta = -ADAM_LR * ((nm / c1) / (jnp.sqrt(nv / c2) + ADAM_EPS) + ADAM_WD * w)
    return delta, nm, nv


def _sum_adamw(parts, w, m, v, name):
    shape = w.shape
    cols = shape[-1]
    p3 = _rows2d(parts, 1)
    w2, m2, v2 = (_rows2d(t, 0) for t in (w, m, v))
    rows = w2.shape[0]
    tr = _pick(rows, 128)

    def body(p_ref, w_ref, m_ref, v_ref, g_ref, d_ref, nm_ref, nv_ref):
        g = p_ref[0, :, :cols].astype(F32)
        for q in range(1, 4):
            g = g + p_ref[q, :, :cols].astype(F32)
        d, nm, nv = _adam_update(w_ref[...], g, m_ref[...], v_ref[...])
        g_ref[...] = g
        d_ref[...] = d
        nm_ref[...] = nm
        nv_ref[...] = nv

    spec = pl.BlockSpec((tr, cols), lambda i: (i, 0))
    out = pl.pallas_call(
        body, name=name,
        out_shape=[jax.ShapeDtypeStruct((rows, cols), F32)] * 4,
        grid=(rows // tr,),
        in_specs=[pl.BlockSpec((4, tr, p3.shape[-1]), lambda i: (0, i, 0)), spec, spec, spec],
        out_specs=[spec] * 4,
        compiler_params=pltpu.CompilerParams(dimension_semantics=("parallel",)),
    )(p3, w2, m2, v2)
    return tuple(t.reshape(shape) for t in out)


def _adamw(w, g, m, v, name):
    rows, cols = w.shape
    tr = _pick(rows, 128)

    def body(w_ref, g_ref, m_ref, v_ref, d_ref, nm_ref, nv_ref):
        d, nm, nv = _adam_update(w_ref[...], g_ref[...], m_ref[...], v_ref[...])
        d_ref[...] = d
        nm_ref[...] = nm
        nv_ref[...] = nv

    spec = pl.BlockSpec((tr, cols), lambda i: (i, 0))
    return pl.pallas_call(
        body, name=name,
        out_shape=[jax.ShapeDtypeStruct((rows, cols), F32)] * 3,
        grid=(rows // tr,),
        in_specs=[spec] * 4, out_specs=[spec] * 3,
        compiler_params=pltpu.CompilerParams(dimension_semantics=("parallel",)),
    )(w, g, m, v)


_VJP = {"nn": (("nt", "gb"), ("tn", "ag")),
        "nt": (("nn", "gb"), ("tn", "ga")),
        "tn": (("nt", "bg"), ("nn", "ag"))}


def _make_dot(cast, precision):
    def raw(mode, a, b):
        return lax.dot_general(cast(a), cast(b), _DIMS[mode], precision=precision,
                               preferred_element_type=F32)

    @functools.partial(jax.custom_vjp, nondiff_argnums=(0,))
    def dot(mode, a, b):
        return raw(mode, a, b)

    def fwd(mode, a, b):
        return raw(mode, a, b), (a, b)

    def bwd(mode, res, g):
        a, b = res
        pick = {"a": a, "b": b, "g": g}
        (ma, ta), (mb, tb) = _VJP[mode]
        return dot(ma, pick[ta[0]], pick[ta[1]]), dot(mb, pick[tb[0]], pick[tb[1]])

    dot.defvjp(fwd, bwd)
    return dot


bdot = _make_dot(lambda t: t.astype(BF), None)
hdot = _make_dot(lambda t: t, lax.Precision.HIGHEST)


def _xdot(mode, a, b):
    return lax.dot_general(a, b, _DIMS[mode], precision=lax.Precision.HIGH, preferred_element_type=F32)


def _unit_lower_inverse(L):
    n = L.shape[0]
    eye = (lax.broadcasted_iota(jnp.int32, (n, n), 0) == lax.broadcasted_iota(jnp.int32, (n, n), 1)).astype(F32)
    t_inv = eye - L
    p = L
    for _ in range(6):
        p = _xdot("nn", p, p)
        t_inv = t_inv + _xdot("nn", t_inv, p)
    return t_inv


@jax.custom_vjp
def _tri_solve(L, rhs, t_inv):
    return _xdot("nn", t_inv, rhs)


def _tri_solve_fwd(L, rhs, t_inv):
    sol = _xdot("nn", t_inv, rhs)
    return sol, (t_inv, sol)


def _tri_solve_bwd(res, dsol):
    t_inv, sol = res
    drhs = _xdot("tn", t_inv, dsol)
    return -_xdot("nt", drhs, sol), drhs, jnp.zeros_like(t_inv)


_tri_solve.defvjp(_tri_solve_fwd, _tri_solve_bwd)


def _sigmoid(x):
    return 1.0 / (1.0 + jnp.exp(-x))


def _softplus(x):
    return jnp.maximum(x, 0.0) + jnp.log(1.0 + jnp.exp(-jnp.abs(x)))


def _dn_chunk(S, xs, ba, z, cw, al, dt, dn, t_saved=None):
    C = DN_C
    pre = xs[0] * cw[0] + xs[1] * cw[1] + xs[2] * cw[2] + xs[3] * cw[3]
    qkv = pre * _sigmoid(pre)
    lane = lax.broadcasted_iota(jnp.int32, (1, 128), 1)
    sub = lax.broadcasted_iota(jnp.int32, (C, 1), 0)
    row_i = lax.broadcasted_iota(jnp.int32, (C, C), 0)
    col_i = lax.broadcasted_iota(jnp.int32, (C, C), 1)
    strict = row_i > col_i
    incl = row_i >= col_i
    g_all = jnp.where((lane >= 4) & (lane < 8), -jnp.exp(al) * _softplus(ba + dt), 0.0)
    gc_all = hdot("nn", incl.astype(F32), g_all)
    gc_all_t = gc_all.T
    beta_all = _sigmoid(ba)
    glast_all = jnp.sum(jnp.where(sub == C - 1, gc_all, 0.0), axis=0, keepdims=True)
    ys, s_new, t_invs = [], [], []
    for h in range(DN_HEADS):
        q = qkv[:, 128 * h:128 * (h + 1)]
        k = qkv[:, 512 + 128 * h:512 + 128 * (h + 1)]
        v = qkv[:, 1024 + 128 * h:1024 + 128 * (h + 1)]
        q = q * lax.rsqrt(jnp.sum(q * q, axis=1, keepdims=True) + EPS) * (DN_D ** -0.5)
        k = k * lax.rsqrt(jnp.sum(k * k, axis=1, keepdims=True) + EPS)
        beta = jnp.sum(jnp.where(lane == h, beta_all, 0.0), axis=1, keepdims=True)
        gc = jnp.sum(jnp.where(lane == 4 + h, gc_all, 0.0), axis=1, keepdims=True)
        gc_row = jnp.sum(jnp.where(sub == 4 + h, gc_all_t, 0.0), axis=0, keepdims=True)
        g_last = jnp.sum(jnp.where(lane == 4 + h, glast_all, 0.0), axis=1, keepdims=True)
        diff = gc - gc_row
        kb = k * beta
        L = jnp.where(strict, bdot("nt", kb, k) * jnp.exp(jnp.where(strict, diff, 0.0)), 0.0)
        t_inv = _unit_lower_inverse(L) if t_saved is None else t_saved[h]
        t_invs.append(t_inv)
        sol = _tri_solve(L, jnp.concatenate([v * beta, kb * jnp.exp(gc)], axis=1), t_inv)
        u, w = sol[:, :DN_D], sol[:, DN_D:]
        a_qk = jnp.where(incl, bdot("nt", q, k) * jnp.exp(jnp.where(incl, diff, 0.0)), 0.0)
        qg = q * jnp.exp(gc)
        kd = k * jnp.exp(g_last - gc)
        v_new = u - bdot("nn", w, S[h])
        o = bdot("nn", qg, S[h]) + bdot("nn", a_qk, v_new)
        s_new.append(S[h] * jnp.exp(g_last) + bdot("tn", kd, v_new))
        o = o * lax.rsqrt(jnp.mean(o * o, axis=1, keepdims=True) + EPS) * dn
        zh = z[:, 128 * h:128 * (h + 1)]
        ys.append(o * (zh * _sigmoid(zh)))
    return jnp.concatenate(ys, axis=1), tuple(s_new), tuple(t_invs)


def _load_shifted(xbuf, x_ref, halo_ref, first):
    xbuf[0:HALO, :] = jnp.where(first, 0.0, halo_ref[:, 0:1536])
    xbuf[HALO:HALO + DN_C, :] = x_ref[:, 0:1536]
    return [xbuf[HALO - 3 + k:HALO - 3 + k + DN_C, :] for k in range(4)]


def dn_forward(cols, jblk, cw, al, dt, dn, name):
    T = cols.shape[0]
    n = T // DN_C

    def body(x_ref, halo_ref, cw_ref, al_ref, dt_ref, dn_ref, y_ref, ss_ref, ts_ref, s_scr, xbuf):
        i = pl.program_id(0)

        @pl.when(i == 0)
        def _():
            s_scr[...] = jnp.zeros_like(s_scr)

        xs = _load_shifted(xbuf, x_ref, halo_ref, i == 0)
        ss_ref[0] = s_scr[...]
        S = [s_scr[h] for h in range(DN_HEADS)]
        cws = [cw_ref[k:k + 1, :] for k in range(4)]
        y, s_new, t_invs = _dn_chunk(S, xs, x_ref[:, 2048:2176], x_ref[:, 1536:2048], cws,
                                     al_ref[...], dt_ref[...], dn_ref[...])
        y_ref[...] = y
        for h in range(DN_HEADS):
            s_scr[h] = s_new[h]
            ts_ref[0, h] = t_invs[h]

    per = DN_C // HALO
    full = lambda shape: pl.BlockSpec(shape, lambda i: (0,) * len(shape))
    return pl.pallas_call(
        body, name=name,
        out_shape=[jax.ShapeDtypeStruct((T, 512), F32),
                   jax.ShapeDtypeStruct((n, DN_HEADS, DN_D, DN_D), F32),
                   jax.ShapeDtypeStruct((n, DN_HEADS, DN_D, DN_D), F32)],
        grid=(n,),
        in_specs=[pl.BlockSpec((DN_C, DN_W), lambda i: (i, jblk)),
                  pl.BlockSpec((HALO, DN_W), lambda i: (jnp.maximum(i * per - 1, 0), jblk)),
                  full((4, 1536)), full((1, 128)), full((1, 128)), full((1, 128))],
        out_specs=[pl.BlockSpec((DN_C, 512), lambda i: (i, 0)),
                   pl.BlockSpec((1, DN_HEADS, DN_D, DN_D), lambda i: (i, 0, 0, 0)),
                   pl.BlockSpec((1, DN_HEADS, DN_D, DN_D), lambda i: (i, 0, 0, 0))],
        scratch_shapes=[pltpu.VMEM((DN_HEADS, DN_D, DN_D), F32), pltpu.VMEM((HALO + DN_C, 1536), F32)],
        compiler_params=pltpu.CompilerParams(dimension_semantics=("arbitrary",)),
    )(cols, cols, cw, al, dt, dn)


def dn_backward(cols, jblk, cw, al, dt, dn, ss, ts, dy, dcols, name):
    T = cols.shape[0]
    n = T // DN_C

    def body(x_ref, halo_ref, cw_ref, al_ref, dt_ref, dn_ref, ss_ref, ts_ref, dy_ref, dcols_in,
             dx_ref, dcw_ref, dal_ref, ddt_ref, ddn_ref, ds_scr, xbuf, dbuf, carry):
        i = pl.program_id(0)

        @pl.when(i == 0)
        def _():
            ds_scr[...] = jnp.zeros_like(ds_scr)
            carry[...] = jnp.zeros_like(carry)
            dcw_ref[...] = jnp.zeros_like(dcw_ref)
            dal_ref[...] = jnp.zeros_like(dal_ref)
            ddt_ref[...] = jnp.zeros_like(ddt_ref)
            ddn_ref[...] = jnp.zeros_like(ddn_ref)

        xs = _load_shifted(xbuf, x_ref, halo_ref, i == n - 1)
        S = [ss_ref[0, h] for h in range(DN_HEADS)]
        cws = [cw_ref[k:k + 1, :] for k in range(4)]

        t_saved = [ts_ref[0, h] for h in range(DN_HEADS)]

        def f(S, xs, ba, z, cws, al, dt, dn):
            return _dn_chunk(S, xs, ba, z, cws, al, dt, dn, t_saved)[:2]

        _, vjp = jax.vjp(f, S, xs, x_ref[:, 2048:2176], x_ref[:, 1536:2048], cws, al_ref[...], dt_ref[...], dn_ref[...])
        dS, dxs, dba, dz, dcws, dal, ddt, ddn = vjp((dy_ref[...], tuple(ds_scr[h] for h in range(DN_HEADS))))
        for h in range(DN_HEADS):
            ds_scr[h] = dS[h]
        dbuf[...] = jnp.zeros_like(dbuf)
        for k in range(4):
            lo = HALO - 3 + k
            dbuf[lo:lo + DN_C, :] += dxs[k]
        dbuf[DN_C:DN_C + HALO, :] += carry[...]
        dx_ref[...] = jnp.concatenate([dbuf[HALO:HALO + DN_C, :], dz, dba,
                                       jnp.zeros((DN_C, DN_W - 2176), F32)], axis=1)
        carry[...] = dbuf[0:HALO, :]
        for k in range(4):
            dcw_ref[k:k + 1, :] += dcws[k]
        dal_ref[...] += dal
        ddt_ref[...] += ddt
        ddn_ref[...] += ddn

    per = DN_C // HALO
    rev = lambda i: n - 1 - i
    full = lambda shape: pl.BlockSpec(shape, lambda i: (0,) * len(shape))
    return pl.pallas_call(
        body, name=name,
        out_shape=[jax.ShapeDtypeStruct(dcols.shape, F32), jax.ShapeDtypeStruct((4, 1536), F32),
                   jax.ShapeDtypeStruct((1, 128), F32), jax.ShapeDtypeStruct((1, 128), F32),
                   jax.ShapeDtypeStruct((1, 128), F32)],
        grid=(n,),
        in_specs=[pl.BlockSpec((DN_C, DN_W), lambda i: (rev(i), jblk)),
                  pl.BlockSpec((HALO, DN_W), lambda i: (jnp.maximum(rev(i) * per - 1, 0), jblk)),
                  full((4, 1536)), full((1, 128)), full((1, 128)), full((1, 128)),
                  pl.BlockSpec((1, DN_HEADS, DN_D, DN_D), lambda i: (rev(i), 0, 0, 0)),
                  pl.BlockSpec((1, DN_HEADS, DN_D, DN_D), lambda i: (rev(i), 0, 0, 0)),
                  pl.BlockSpec((DN_C, 512), lambda i: (rev(i), 0)), _ANY],
        out_specs=[pl.BlockSpec((DN_C, DN_W), lambda i: (rev(i), jblk)),
                   full((4, 1536)), full((1, 128)), full((1, 128)), full((1, 128))],
        scratch_shapes=[pltpu.VMEM((DN_HEADS, DN_D, DN_D), F32), pltpu.VMEM((HALO + DN_C, 1536), F32),
                        pltpu.VMEM((HALO + DN_C, 1536), F32), pltpu.VMEM((HALO, 1536), F32)],
        input_output_aliases={9: 0},
        compiler_params=pltpu.CompilerParams(dimension_semantics=("arbitrary",)),
    )(cols, cols, cw, al, dt, dn, ss, ts, dy, dcols)


def _full(shape):
    return pl.BlockSpec(shape, lambda i: (0,) * len(shape))


def _silu(x):
    return x * _sigmoid(x)


def _gelu(x):
    return 0.5 * x * (1.0 + jnp.tanh(0.7978845608028654 * (x + 0.044715 * (x * x * x))))


def _lane_col(mat, idx):
    lane = lax.broadcasted_iota(jnp.int32, (1, mat.shape[1]), 1)
    return jnp.sum(jnp.where(lane == idx, mat, 0.0), axis=1, keepdims=True)


def _gm_chunk(uv, z, gain, ws, bt):
    g = _gelu(uv)
    u, v = g[:, :512], g[:, 512:]
    v = v * lax.rsqrt(jnp.mean(v * v, axis=1, keepdims=True) + EPS) * gain
    row_i = lax.broadcasted_iota(jnp.int32, (BLK, BLK), 0)
    col_i = lax.broadcasted_iota(jnp.int32, (BLK, BLK), 1)
    causal = row_i >= col_i
    ss = []
    for grp in range(4):
        wg = jnp.where(causal, ws[grp], 0.0)
        ss.append(bdot("nn", wg, v[:, BLK * grp:BLK * (grp + 1)]) + _lane_col(bt, grp))
    return u * jnp.concatenate(ss, axis=1) * _silu(z)


def gm_forward(cols, jblk, gain, ws, bt, name):
    T = cols.shape[0]

    def body(x_ref, gain_ref, ws_ref, bt_ref, y_ref):
        y_ref[...] = _gm_chunk(x_ref[:, 0:1024], x_ref[:, 1024:1536], gain_ref[...],
                               [ws_ref[g] for g in range(4)], bt_ref[...])

    return pl.pallas_call(
        body, name=name, out_shape=jax.ShapeDtypeStruct((T, 512), F32), grid=(T // BLK,),
        in_specs=[pl.BlockSpec((BLK, GM_W), lambda i: (i, jblk)),
                  _full((1, 512)), _full((4, BLK, BLK)), _full((BLK, BLK))],
        out_specs=pl.BlockSpec((BLK, 512), lambda i: (i, 0)),
        compiler_params=pltpu.CompilerParams(dimension_semantics=("parallel",)),
    )(cols, gain, ws, bt)


def gm_backward(cols, jblk, gain, ws, bt, dy, dcols, name):
    T = cols.shape[0]

    def body(x_ref, gain_ref, ws_ref, bt_ref, dy_ref, dcols_in, dx_ref, dgain_ref, dws_ref, dbt_ref):
        @pl.when(pl.program_id(0) == 0)
        def _():
            dgain_ref[...] = jnp.zeros_like(dgain_ref)
            dws_ref[...] = jnp.zeros_like(dws_ref)
            dbt_ref[...] = jnp.zeros_like(dbt_ref)

        _, vjp = jax.vjp(_gm_chunk, x_ref[:, 0:1024], x_ref[:, 1024:1536], gain_ref[...],
                         [ws_ref[g] for g in range(4)], bt_ref[...])
        duv, dz, dgain, dws, dbt = vjp(dy_ref[...])
        dx_ref[...] = jnp.concatenate([duv, dz], axis=1)
        dgain_ref[...] += dgain
        for g in range(4):
            dws_ref[g] += dws[g]
        dbt_ref[...] += dbt

    return pl.pallas_call(
        body, name=name,
        out_shape=[jax.ShapeDtypeStruct(dcols.shape, F32), jax.ShapeDtypeStruct((1, 512), F32),
                   jax.ShapeDtypeStruct((4, BLK, BLK), F32), jax.ShapeDtypeStruct((BLK, BLK), F32)],
        grid=(T // BLK,),
        in_specs=[pl.BlockSpec((BLK, GM_W), lambda i: (i, jblk)),
                  _full((1, 512)), _full((4, BLK, BLK)), _full((BLK, BLK)),
                  pl.BlockSpec((BLK, 512), lambda i: (i, 0)), _ANY],
        out_specs=[pl.BlockSpec((BLK, GM_W), lambda i: (i, jblk)),
                   _full((1, 512)), _full((4, BLK, BLK)), _full((BLK, BLK))],
        input_output_aliases={5: 0},
        compiler_params=pltpu.CompilerParams(dimension_semantics=("arbitrary",)),
    )(cols, gain, ws, bt, dy, dcols)


def _sw_block(first, q, kp, kc, vp, vc, z, sinks):
    P = BLK
    lane = lax.broadcasted_iota(jnp.int32, (1, 128), 1)
    r = lax.broadcasted_iota(jnp.int32, (128, 128), 0)
    c = lax.broadcasted_iota(jnp.int32, (128, 128), 1)
    swap = (c == (r + 64) % 128).astype(F32)
    k2 = jnp.concatenate([kp, kc], axis=0)
    v2 = jnp.concatenate([vp, vc], axis=0)
    k2s = bdot("nn", k2, swap)
    v2s = bdot("nn", v2, swap)
    qi = lax.broadcasted_iota(jnp.int32, (P, 2 * P), 0)
    kj = lax.broadcasted_iota(jnp.int32, (P, 2 * P), 1)
    dist = qi + P - kj
    valid = (dist >= 0) & (dist < P) & ((kj >= P) | jnp.logical_not(first))
    outs = []
    for j in range(4):
        acc = jnp.zeros((P, 128), F32)
        for half in range(2):
            h = 2 * j + half
            kv = h // 4
            in_half = (lane >= 64 * half) & (lane < 64 * half + 64)
            qh = jnp.where(in_half, q[:, 128 * j:128 * (j + 1)], 0.0)
            same = (half == kv)
            s = bdot("nt", qh, k2 if same else k2s) * (64 ** -0.5)
            s = jnp.where(valid, s, NEG_INF)
            sink = _lane_col(sinks, h)
            m = lax.stop_gradient(jnp.maximum(jnp.max(s, axis=1, keepdims=True), sink))
            e = jnp.exp(s - m)
            p = e / (jnp.sum(e, axis=1, keepdims=True) + jnp.exp(sink - m))
            o = bdot("nn", p, v2 if same else v2s)
            acc = acc + jnp.where(in_half, o, 0.0)
        outs.append(acc)
    return jnp.concatenate(outs, axis=1) * _silu(z)


def _sw_specs(jblk, idx):
    prev = lambda i: jnp.maximum(idx(i) - 1, 0)
    jk = (jblk * SW_W + 1024) // 128
    return [pl.BlockSpec((BLK, SW_W), lambda i: (idx(i), jblk)),
            pl.BlockSpec((BLK, 128), lambda i: (prev(i), jk)),
            pl.BlockSpec((BLK, 128), lambda i: (prev(i), jk + 1)), _full((1, 128))]


def sw_forward(cols, jblk, sinks, name):
    T = cols.shape[0]

    def body(x_ref, kp_ref, vp_ref, s_ref, y_ref):
        y_ref[...] = _sw_block(pl.program_id(0) == 0, x_ref[:, 0:512], kp_ref[...], x_ref[:, 1024:1152],
                               vp_ref[...], x_ref[:, 1152:1280], x_ref[:, 512:1024], s_ref[...])

    return pl.pallas_call(
        body, name=name, out_shape=jax.ShapeDtypeStruct((T, 512), F32), grid=(T // BLK,),
        in_specs=_sw_specs(jblk, lambda i: i),
        out_specs=pl.BlockSpec((BLK, 512), lambda i: (i, 0)),
        compiler_params=pltpu.CompilerParams(dimension_semantics=("parallel",)),
    )(cols, cols, cols, sinks)


def sw_backward(cols, jblk, sinks, dy, dcols, name):
    T = cols.shape[0]
    n = T // BLK
    rev = lambda i: n - 1 - i

    def body(x_ref, kp_ref, vp_ref, s_ref, dy_ref, dcols_in, dx_ref, ds_ref, kcarry, vcarry):
        i = pl.program_id(0)

        @pl.when(i == 0)
        def _():
            kcarry[...] = jnp.zeros_like(kcarry)
            vcarry[...] = jnp.zeros_like(vcarry)
            ds_ref[...] = jnp.zeros_like(ds_ref)

        f = functools.partial(_sw_block, i == n - 1)
        _, vjp = jax.vjp(f, x_ref[:, 0:512], kp_ref[...], x_ref[:, 1024:1152], vp_ref[...], x_ref[:, 1152:1280],
                         x_ref[:, 512:1024], s_ref[...])
        dq, dkp, dkc, dvp, dvc, dz, dsk = vjp(dy_ref[...])
        dx_ref[...] = jnp.concatenate([dq, dz, dkc + kcarry[...], dvc + vcarry[...],
                                       jnp.zeros((BLK, SW_W - 1280), F32)], axis=1)
        kcarry[...] = dkp
        vcarry[...] = dvp
        ds_ref[...] += dsk

    return pl.pallas_call(
        body, name=name,
        out_shape=[jax.ShapeDtypeStruct(dcols.shape, F32), jax.ShapeDtypeStruct((1, 128), F32)],
        grid=(n,),
        in_specs=_sw_specs(jblk, rev) + [pl.BlockSpec((BLK, 512), lambda i: (rev(i), 0)), _ANY],
        out_specs=[pl.BlockSpec((BLK, SW_W), lambda i: (rev(i), jblk)), _full((1, 128))],
        scratch_shapes=[pltpu.VMEM((BLK, 128), F32), pltpu.VMEM((BLK, 128), F32)],
        input_output_aliases={5: 0},
        compiler_params=pltpu.CompilerParams(dimension_semantics=("arbitrary",)),
    )(cols, cols, cols, sinks, dy, dcols)


XM_TQ = 256


def _xm_block(q, z, mkv):
    outs = []
    for h in range(4):
        s = bdot("nt", q[:, 128 * h:128 * (h + 1)], mkv[:, 128 * h:128 * (h + 1)]) * (128 ** -0.5)
        m = lax.stop_gradient(jnp.max(s, axis=1, keepdims=True))
        e = jnp.exp(s - m)
        p = e / jnp.sum(e, axis=1, keepdims=True)
        outs.append(bdot("nn", p, mkv[:, 512 + 128 * h:512 + 128 * (h + 1)]))
    return jnp.concatenate(outs, axis=1) * _silu(z)


def xm_forward(cols, jblk, mkv, name):
    T = cols.shape[0]

    def body(x_ref, m_ref, y_ref):
        y_ref[...] = _xm_block(x_ref[:, 0:512], x_ref[:, 512:1024], m_ref[...])

    return pl.pallas_call(
        body, name=name, out_shape=jax.ShapeDtypeStruct((T, 512), F32), grid=(T // XM_TQ,),
        in_specs=[pl.BlockSpec((XM_TQ, XM_W), lambda i: (i, jblk)), _full(mkv.shape)],
        out_specs=pl.BlockSpec((XM_TQ, 512), lambda i: (i, 0)),
        compiler_params=pltpu.CompilerParams(dimension_semantics=("parallel",)),
    )(cols, mkv)


def xm_backward(cols, jblk, mkv, dy, dcols, name):
    T = cols.shape[0]

    def body(x_ref, m_ref, dy_ref, dcols_in, dx_ref, dm_ref):
        @pl.when(pl.program_id(0) == 0)
        def _():
            dm_ref[...] = jnp.zeros_like(dm_ref)

        _, vjp = jax.vjp(_xm_block, x_ref[:, 0:512], x_ref[:, 512:1024], m_ref[...])
        dq, dz, dm = vjp(dy_ref[...])
        dx_ref[...] = jnp.concatenate([dq, dz], axis=1)
        dm_ref[...] += dm

    return pl.pallas_call(
        body, name=name,
        out_shape=[jax.ShapeDtypeStruct(dcols.shape, F32), jax.ShapeDtypeStruct(mkv.shape, F32)],
        grid=(T // XM_TQ,),
        in_specs=[pl.BlockSpec((XM_TQ, XM_W), lambda i: (i, jblk)), _full(mkv.shape),
                  pl.BlockSpec((XM_TQ, 512), lambda i: (i, 0)), _ANY],
        out_specs=[pl.BlockSpec((XM_TQ, XM_W), lambda i: (i, jblk)), _full(mkv.shape)],
        input_output_aliases={3: 0},
        compiler_params=pltpu.CompilerParams(dimension_semantics=("arbitrary",)),
    )(cols, mkv, dy, dcols)


def _rms(x, gain):
    return x * lax.rsqrt(jnp.mean(x * x, axis=1, keepdims=True) + EPS) * gain


def memkv_forward(mem, gain, w, name):
    def body(m_ref, g_ref, w_ref, o_ref):
        o_ref[...] = bdot("nn", _rms(m_ref[...], g_ref[...]), w_ref[...])

    return pl.pallas_call(body, name=name, out_shape=jax.ShapeDtypeStruct(mem.shape, F32),
                          compiler_params=pltpu.CompilerParams(vmem_limit_bytes=VMEM_LIMIT))(mem, gain, w)


def memkv_backward(mem, gain, w, dkv, name):
    def body(m_ref, g_ref, w_ref, d_ref, dg_ref, dw_ref):
        mem_v = m_ref[...]
        _, vjp = jax.vjp(lambda g, ww: bdot("nn", _rms(mem_v, g), ww), g_ref[...], w_ref[...].astype(F32))
        dg, dw = vjp(d_ref[...])
        dg_ref[...] = dg
        dw_ref[...] = dw

    return pl.pallas_call(body, name=name,
                          out_shape=[jax.ShapeDtypeStruct(gain.shape, F32), jax.ShapeDtypeStruct(w.shape, F32)],
                          compiler_params=pltpu.CompilerParams(vmem_limit_bytes=VMEM_LIMIT))(mem, gain, w, dkv)


MG_TB = 128


def _merge_block(ys, gl, wup, wout, gpost):
    merged = None
    for n in range(4):
        t = _sigmoid(gl[:, 1024 * n:1024 * (n + 1)]) * bdot("nn", ys[n], wup[n])
        merged = t if merged is None else merged + t
    out = bdot("nn", merged, wout)
    return _rms(out, gpost)


def merge_forward(ys, cols, jgate, x, wup, wout, gpost, name):
    T = x.shape[0]
    TB = 256

    def body(ya, yb, yc, ym, gl_ref, x_ref, wup_ref, wout_ref, gp_ref, o_ref):
        upd = _merge_block([ya[...], yb[...], yc[...], ym[...]], gl_ref[...],
                           [wup_ref[n] for n in range(4)], wout_ref[...], gp_ref[...])
        o_ref[...] = x_ref[...] + upd

    yspec = pl.BlockSpec((TB, 512), lambda i: (i, 0))
    return pl.pallas_call(
        body, name=name, out_shape=jax.ShapeDtypeStruct((T, 1024), F32), grid=(T // TB,),
        in_specs=[yspec] * 4 + [pl.BlockSpec((TB, 4096), lambda i: (i, jgate)),
                                pl.BlockSpec((TB, 1024), lambda i: (i, 0)),
                                _full(wup.shape), _full(wout.shape), _full((1, 1024))],
        out_specs=pl.BlockSpec((TB, 1024), lambda i: (i, 0)),
        compiler_params=pltpu.CompilerParams(dimension_semantics=("parallel",), vmem_limit_bytes=VMEM_LIMIT),
    )(*ys, cols, x, wup, wout, gpost)


def merge_backward(ys, cols, jgate, wup, wout, gpost, dx, name):
    T = dx.shape[0]
    TB = MG_TB

    def body(ya, yb, yc, ym, gl_ref, wup_ref, wout_ref, gp_ref, dx_ref,
             dgl_ref, dya, dyb, dyc, dym, dwup_ref, dwout_ref, dgp_ref):
        @pl.when(pl.program_id(0) == 0)
        def _():
            dwup_ref[...] = jnp.zeros_like(dwup_ref)
            dwout_ref[...] = jnp.zeros_like(dwout_ref)
            dgp_ref[...] = jnp.zeros_like(dgp_ref)

        _, vjp = jax.vjp(_merge_block, [ya[...], yb[...], yc[...], ym[...]], gl_ref[...],
                         [wup_ref[n].astype(F32) for n in range(4)], wout_ref[...].astype(F32), gp_ref[...])
        dys, dgl, dwup, dwout, dgp = vjp(dx_ref[...])
        for ref, val in zip((dya, dyb, dyc, dym), dys):
            ref[...] = val
        dgl_ref[...] = dgl
        for n in range(4):
            dwup_ref[n] += dwup[n]
        dwout_ref[...] += dwout
        dgp_ref[...] += dgp

    yspec = pl.BlockSpec((TB, 512), lambda i: (i, 0))
    return pl.pallas_call(
        body, name=name,
        out_shape=[jax.ShapeDtypeStruct(cols.shape, F32)] + [jax.ShapeDtypeStruct((T, 512), F32)] * 4 + [
            jax.ShapeDtypeStruct(wup.shape, F32), jax.ShapeDtypeStruct(wout.shape, F32),
            jax.ShapeDtypeStruct((1, 1024), F32)],
        grid=(T // TB,),
        in_specs=[yspec] * 4 + [pl.BlockSpec((TB, 4096), lambda i: (i, jgate)),
                                _full(wup.shape), _full(wout.shape), _full((1, 1024)),
                                pl.BlockSpec((TB, 1024), lambda i: (i, 0))],
        out_specs=[pl.BlockSpec((TB, 4096), lambda i: (i, jgate))] + [yspec] * 4 + [
            _full(wup.shape), _full(wout.shape), _full((1, 1024))],
        compiler_params=pltpu.CompilerParams(dimension_semantics=("arbitrary",), vmem_limit_bytes=VMEM_LIMIT),
    )(*ys, cols, wup, wout, gpost, dx)


NB = 256


def prenorm_forward(x, gain, name):
    T = x.shape[0]

    def body(x_ref, g_ref, o_ref):
        o_ref[...] = _rms(x_ref[...], g_ref[...]).astype(BF)

    return pl.pallas_call(
        body, name=name, out_shape=jax.ShapeDtypeStruct(x.shape, BF), grid=(T // NB,),
        in_specs=[pl.BlockSpec((NB, 1024), lambda i: (i, 0)), _full((1, 1024))],
        out_specs=pl.BlockSpec((NB, 1024), lambda i: (i, 0)),
        compiler_params=pltpu.CompilerParams(dimension_semantics=("parallel",)),
    )(x, gain)


def prenorm_backward(x, gain, dh, dres, name):
    T = x.shape[0]

    def body(x_ref, g_ref, dh_ref, dr_ref, dx_ref, dg_ref):
        @pl.when(pl.program_id(0) == 0)
        def _():
            dg_ref[...] = jnp.zeros_like(dg_ref)

        _, vjp = jax.vjp(_rms, x_ref[...], g_ref[...])
        dxn, dg = vjp(dh_ref[...])
        dx_ref[...] = dr_ref[...] + dxn
        dg_ref[...] += dg

    spec = pl.BlockSpec((NB, 1024), lambda i: (i, 0))
    return pl.pallas_call(
        body, name=name,
        out_shape=[jax.ShapeDtypeStruct(x.shape, F32), jax.ShapeDtypeStruct((1, 1024), F32)], grid=(T // NB,),
        in_specs=[spec, _full((1, 1024)), spec, spec], out_specs=[spec, _full((1, 1024))],
        compiler_params=pltpu.CompilerParams(dimension_semantics=("arbitrary",)),
    )(x, gain, dh, dres)


def loss_head(y, target, name):
    T, D = y.shape

    def body(y_ref, t_ref, l_ref, d_ref):
        @pl.when(pl.program_id(0) == 0)
        def _():
            l_ref[...] = jnp.zeros_like(l_ref)

        err = y_ref[...] - t_ref[...]
        d_ref[...] = err * (1.0 / D)
        l_ref[...] += jnp.full(l_ref.shape, 0.5 * jnp.sum(jnp.mean(err * err, axis=1, keepdims=True)), F32)

    spec = pl.BlockSpec((NB, D), lambda i: (i, 0))
    return pl.pallas_call(
        body, name=name,
        out_shape=[jax.ShapeDtypeStruct((1, 128), F32), jax.ShapeDtypeStruct(y.shape, F32)], grid=(T // NB,),
        in_specs=[spec, spec], out_specs=[_full((1, 128)), spec],
        compiler_params=pltpu.CompilerParams(dimension_semantics=("arbitrary",)),
    )(y, target)


JB_GATE, JB_XM, JB_DN, JB_SW, JB_GM = 0, 4, 2, 5, 6
_ALIGNED_PIECES = ((5896, 4096), (4872, 512), (5384, 512), (0, 2048), (2048, 8), 504, (3592, 512), (4360, 512),
                   (4104, 128), (4232, 128), 256, (2056, 1024), (3080, 512))
_NATURAL_FROM_ALIGNED = ((5120, 2048), (7168, 8), (9216, 1024), (10240, 512), (7680, 512), (8704, 128), (8832, 128),
                         (8192, 512), (4096, 512), (4608, 512), (0, 4096))


def _natural_range(slots, start, width):
    out = []
    while width > 0:
        j, i = divmod(start, W_IN_SHARD)
        take = min(width, W_IN_SHARD - i)
        out.append(slots[j, :, :, i:i + take])
        start, width = start + take, width - take
    return out


def _aligned_w_in(slots):
    parts = []
    for piece in _ALIGNED_PIECES:
        if isinstance(piece, int):
            parts.append(jnp.zeros(slots.shape[1:3] + (piece,), slots.dtype))
        else:
            parts += _natural_range(slots, *piece)
    return jnp.concatenate(parts, axis=-1)


def _slots_of_aligned(d_al):
    slots = []
    for s in range(N_DEV):
        lo, hi = s * W_IN_SHARD, (s + 1) * W_IN_SHARD
        parts, nat = [], 0
        for a_start, width in _NATURAL_FROM_ALIGNED:
            b, e = max(lo, nat), min(hi, nat + width)
            if b < e:
                parts.append(d_al[..., a_start + b - nat:a_start + e - nat])
            nat += width
        parts.append(jnp.zeros(d_al.shape[:2] + (W_IN_SHARD_PAD - W_IN_SHARD,), d_al.dtype))
        slots.append(jnp.concatenate(parts, axis=-1))
    return jnp.stack(slots)


SMALL_VEC_W = 1024


def _pack_small(parts):
    flat = jnp.concatenate([p.reshape(-1).astype(F32) for p in parts])
    n = flat.shape[0]
    rows = -(-n // SMALL_VEC_W)
    rows = -(-rows // 8) * 8
    return jnp.pad(flat, (0, rows * SMALL_VEC_W - n)).reshape(rows, SMALL_VEC_W)


def _unpack_small(vec, shapes):
    flat = vec.reshape(-1)
    out, off = [], 0
    for s in shapes:
        n = math.prod(s)
        out.append(flat[off:off + n].reshape(s))
        off += n
    return out


def _lanes(vec, at):
    return jnp.zeros((1, 128), F32).at[0, at:at + vec.shape[0]].set(vec)


SMALL_NAMES = ("norm_pre", "norm_post", "norm_mem", "a_log", "dt_bias", "dn_norm", "gm_norm",
               "spatial_w", "spatial_b", "sinks")


def _forward_backward(x, mem, target, small, conv_full, big):
    w_in_al, w_mem, w_up, w_out = big
    layers = []
    xl = x
    for l in range(DEPTH):
        t = "l%d_" % l
        p = dict(
            gpre=small["norm_pre"][l][None], gpost=small["norm_post"][l][None], gmem=small["norm_mem"][l][None],
            cw=conv_full[l], al=_lanes(small["a_log"][l], 4), dt=_lanes(small["dt_bias"][l], 4),
            dnn=small["dn_norm"][l][None], gain=small["gm_norm"][l][None], ws=small["spatial_w"][l],
            bt=jnp.zeros((128, 128), F32).at[:, :GM_GROUPS].set(small["spatial_b"][l].T),
            sinks=_lanes(small["sinks"][l], 0))
        h = prenorm_forward(xl, p["gpre"], t + "prenorm")
        cols = _matmul(h, w_in_al[l], "nn", F32, t + "w_in")
        mkv = memkv_forward(mem, p["gmem"], w_mem[l], t + "memkv")
        ya, ss, ts = dn_forward(cols, JB_DN, p["cw"], p["al"], p["dt"], p["dnn"], t + "deltanet")
        yb = gm_forward(cols, JB_GM, p["gain"], p["ws"], p["bt"], t + "gmlp")
        yc = sw_forward(cols, JB_SW, p["sinks"], t + "swa")
        ym = xm_forward(cols, JB_XM, mkv, t + "memattn")
        xn = merge_forward([ya, yb, yc, ym], cols, JB_GATE, xl, w_up[l], w_out[l], p["gpost"], t + "merge")
        layers.append(dict(p, x=xl, h=h, cols=cols, mkv=mkv, ss=ss, ts=ts, ys=[ya, yb, yc, ym]))
        xl = xn
    loss, dx = loss_head(xl, target, "loss_head")

    gs = {n: [None] * DEPTH for n in SMALL_NAMES + ("conv_w", "w_in", "w_mem_kv", "w_up", "w_out")}
    for l in reversed(range(DEPTH)):
        t = "l%d_" % l
        s = layers[l]
        cols = s["cols"]
        dcols, dya, dyb, dyc, dym, dwup, dwout, dgpost = merge_backward(
            s["ys"], cols, JB_GATE, w_up[l], w_out[l], s["gpost"], dx, t + "merge_bwd")
        dcols, dmkv = xm_backward(cols, JB_XM, s["mkv"], dym, dcols, t + "memattn_bwd")
        dgmem, dwmem = memkv_backward(mem, s["gmem"], w_mem[l], dmkv, t + "memkv_bwd")
        dcols, dsinks = sw_backward(cols, JB_SW, s["sinks"], dyc, dcols, t + "swa_bwd")
        dcols, dgain, dws, dbt = gm_backward(cols, JB_GM, s["gain"], s["ws"], s["bt"], dyb, dcols, t + "gmlp_bwd")
        dcols, dcw, dal, ddt, ddn = dn_backward(
            cols, JB_DN, s["cw"], s["al"], s["dt"], s["dnn"], s["ss"], s["ts"], dya, dcols, t + "deltanet_bwd")
        dh = _matmul(dcols, w_in_al[l], "nt", F32, t + "w_in_bwd_x")
        gs["w_in"][l] = _matmul(s["h"], dcols, "tn", BF, t + "w_in_bwd_w")
        dx, dgpre = prenorm_backward(s["x"], s["gpre"], dh, dx, t + "prenorm_bwd")
        gs["norm_pre"][l], gs["norm_post"][l], gs["norm_mem"][l] = dgpre[0], dgpost[0], dgmem[0]
        gs["a_log"][l], gs["dt_bias"][l], gs["dn_norm"][l] = dal[0, 4:8], ddt[0, 4:8], ddn[0]
        gs["gm_norm"][l], gs["spatial_w"][l], gs["spatial_b"][l] = dgain[0], dws, dbt[:, :GM_GROUPS].T
        gs["sinks"][l], gs["conv_w"][l] = dsinks[0, :SW_HEADS], dcw
        gs["w_mem_kv"][l], gs["w_up"][l], gs["w_out"][l] = dwmem, dwup, dwout
    return loss, dx, {n: jnp.stack(v) for n, v in gs.items()}


def kernel(x, mem, norm_pre, norm_post, norm_mem, w_in, conv_w, a_log, dt_bias, dn_norm, gm_norm, spatial_w, spatial_b, sinks, w_mem_kv, w_up, w_out, loss_target, m_norm_pre, m_norm_post, m_norm_mem, m_w_in, m_conv_w, m_a_log, m_dt_bias, m_dn_norm, m_gm_norm, m_spatial_w, m_spatial_b, m_sinks, m_w_mem_kv, m_w_up, m_w_out, v_norm_pre, v_norm_post, v_norm_mem, v_w_in, v_conv_w, v_a_log, v_dt_bias, v_dn_norm, v_gm_norm, v_spatial_w, v_spatial_b, v_sinks, v_w_mem_kv, v_w_up, v_w_out):
    xi, yi, ci = _my_place()
    my_slot = 4 * xi + 2 * yi + ci
    conv_shard = conv_w.shape[-1]

    w_in_pad = jnp.pad(w_in.astype(BF), ((0, 0), (0, 0), (0, W_IN_SHARD_PAD - W_IN_SHARD)))
    s_in, s_mem, s_up, s_out = _all_gather_slots(
        [w_in_pad, w_mem_kv.astype(BF), w_up.astype(BF), w_out.astype(BF)], "gather_weights")
    big = (_aligned_w_in(s_in),
           jnp.transpose(s_mem, (1, 0, 2, 3)).reshape(DEPTH, D_MODEL, 2 * BRANCH_W),
           jnp.transpose(s_up, (1, 2, 3, 0, 4)).reshape(DEPTH, N_BRANCH, BRANCH_W, D_MODEL),
           jnp.transpose(s_out, (1, 0, 2, 3)).reshape(DEPTH, D_MODEL, D_MODEL))
    conv_rows = _all_gather_rows_vmem(conv_w.reshape(DEPTH * CONV_W, conv_shard), "gather_conv_w")
    conv_full = jnp.transpose(conv_rows.reshape(N_DEV, DEPTH, CONV_W, conv_shard), (1, 2, 0, 3))
    conv_full = conv_full.reshape(DEPTH, CONV_W, N_DEV * conv_shard)

    small = dict(norm_pre=norm_pre, norm_post=norm_post, norm_mem=norm_mem, a_log=a_log,
                 dt_bias=dt_bias, dn_norm=dn_norm, gm_norm=gm_norm, spatial_w=spatial_w,
                 spatial_b=spatial_b, sinks=sinks)
    loss, dx, gs = _forward_backward(x[0], mem[0], loss_target[0], small, conv_full, big)

    n_chip = N_DEV // 2
    g_slots = [
        _slots_of_aligned(gs["w_in"]),
        jnp.transpose(gs["w_mem_kv"].astype(BF).reshape(DEPTH, N_DEV, 128, 2 * BRANCH_W), (1, 0, 2, 3)),
        jnp.transpose(gs["w_up"].astype(BF).reshape(DEPTH, N_BRANCH, BRANCH_W, N_DEV, 128), (3, 0, 1, 2, 4)),
        jnp.transpose(gs["w_out"].astype(BF).reshape(DEPTH, N_DEV, 128, D_MODEL), (1, 0, 2, 3))]
    g_slots = [g.reshape((n_chip, 2) + g.shape[1:]) for g in g_slots]
    theirs = _exchange_sibling(g_slots, "exchange_sibling")
    chip_sums = [_pair_sum(g, t, "pair_sum_%d" % i) for i, (g, t) in enumerate(zip(g_slots, theirs))]
    parts = _exchange_chips(chip_sums, "exchange_chips")

    small_parts = [loss[0, :1]] + [gs[n] for n in SMALL_NAMES] + [gs["conv_w"]]
    small_shapes = [p.shape for p in small_parts]
    tot = _unpack_small(_all_reduce_vmem(_pack_small(small_parts), "all_reduce_small"), small_shapes)
    loss_tot = tot[0][0]
    g_small = dict(zip(SMALL_NAMES, tot[1:1 + len(SMALL_NAMES)]))
    g_conv = lax.dynamic_slice_in_dim(tot[-1], my_slot * conv_shard, conv_shard, axis=2)

    given = dict(norm_pre=(norm_pre, m_norm_pre, v_norm_pre), norm_post=(norm_post, m_norm_post, v_norm_post),
                 norm_mem=(norm_mem, m_norm_mem, v_norm_mem), a_log=(a_log, m_a_log, v_a_log),
                 dt_bias=(dt_bias, m_dt_bias, v_dt_bias), dn_norm=(dn_norm, m_dn_norm, v_dn_norm),
                 gm_norm=(gm_norm, m_gm_norm, v_gm_norm), spatial_w=(spatial_w, m_spatial_w, v_spatial_w),
                 spatial_b=(spatial_b, m_spatial_b, v_spatial_b), sinks=(sinks, m_sinks, v_sinks),
                 conv_w=(conv_w, m_conv_w, v_conv_w))
    packed_names = SMALL_NAMES + ("conv_w",)
    grads = dict(g_small, conv_w=g_conv)
    pshapes = [given[n][0].shape for n in packed_names]
    pw, pm, pv = (_pack_small([given[n][i] for n in packed_names]) for i in range(3))
    pd, pnm, pnv = _adamw(pw, _pack_small([grads[n] for n in packed_names]), pm, pv, "adamw_small")
    upd = {n: t for n, t in zip(packed_names, zip(_unpack_small(pd, pshapes), _unpack_small(pnm, pshapes),
                                                  _unpack_small(pnv, pshapes)))}
    for name, part, (w, m, v) in (("w_in", parts[0], (w_in, m_w_in, v_w_in)),
                                  ("w_mem_kv", parts[1], (w_mem_kv, m_w_mem_kv, v_w_mem_kv)),
                                  ("w_up", parts[2], (w_up, m_w_up, v_w_up)),
                                  ("w_out", parts[3], (w_out, m_w_out, v_w_out))):
        g, d, nm, nv = _sum_adamw(part, w, m, v, "adamw_" + name)
        grads[name], upd[name] = g, (d, nm, nv)

    order = ("norm_pre", "norm_post", "norm_mem", "w_in", "conv_w", "a_log", "dt_bias", "dn_norm",
             "gm_norm", "spatial_w", "spatial_b", "sinks", "w_mem_kv", "w_up", "w_out")
    return (loss_tot, dx[None], *[grads[n] for n in order], *[upd[n][0] for n in order],
            *[upd[n][1] for n in order], *[upd[n][2] for n in order])
```

```python
import functools
import math

import jax
import jax.numpy as jnp
from jax import lax
from jax.experimental import pallas as pl
from jax.experimental.pallas import tpu as pltpu

MESH = pl.DeviceIdType.MESH
N_DEV = 8

D_MODEL = 1024
DEPTH = 2
N_BRANCH = 4
BRANCH_W = 512
DN_HEADS = 4
CONV_W = 4
GM_GROUPS = 4
SW_HEADS = 8
EPS = 1e-6
NEG_INF = -1e30

D_IN = 9992
W_IN_SHARD = D_IN // N_DEV
W_IN_SHARD_PAD = 1280
D_IN_AL = 10752
DN_W, SW_W, GM_W, XM_W = 2560, 1536, 1536, 1024

ADAM_LR = 0.001
ADAM_B1 = 0.9
ADAM_B2 = 0.999
ADAM_EPS = 1e-08
ADAM_WD = 0.01
ADAM_STEP = 10

VMEM_LIMIT = 56 * 1024 * 1024

BF = jnp.bfloat16
F32 = jnp.float32
DN_C = 128
DN_D = 128
HALO = 8
BLK = 128


def _my_place():
    return lax.axis_index("x"), lax.axis_index("y"), lax.axis_index("c")


_ANY = pl.BlockSpec(memory_space=pl.ANY)


def _all_gather_slots(parts, name):
    n = len(parts)

    def body(*refs):
        p_refs, out_refs = refs[:n], refs[n:2 * n]
        send_sems, recv_sems, local_sems = refs[2 * n:]
        x, y, c = _my_place()
        me, sibling = (x, y, c), (x, y, 1 - c)
        chips = [(1 - x, y), (x, 1 - y), (1 - x, 1 - y)]

        def copy(a, k, block, to, src=None):
            px, py, pc = block
            slot = out_refs[a].at[4 * px + 2 * py + pc]
            return pltpu.make_async_remote_copy(
                src_ref=slot if src is None else src, dst_ref=slot,
                send_sem=send_sems.at[7 * a + k], recv_sem=recv_sems.at[7 * a + k],
                device_id=to, device_id_type=MESH)

        mine = [pltpu.make_async_copy(p_refs[a], out_refs[a].at[4 * x + 2 * y + c], local_sems.at[a])
                for a in range(n)]
        for cp in mine:
            cp.start()
        first = []
        for a in range(n):
            first.append(copy(a, 0, me, sibling, src=p_refs[a]))
            first += [copy(a, 1 + j, me, (*chip, c), src=p_refs[a]) for j, chip in enumerate(chips)]
        for cp in first:
            cp.start()
        passed = []
        for j, chip in enumerate(chips):
            for a in range(n):
                copy(a, 1 + j, (*chip, c), me).wait_recv()
                fwd = copy(a, 4 + j, (*chip, c), sibling)
                fwd.start()
                passed.append(fwd)
        for a in range(n):
            copy(a, 0, sibling, me).wait_recv()
            for j, chip in enumerate(chips):
                copy(a, 4 + j, (*chip, 1 - c), me).wait_recv()
        for cp in first + passed:
            cp.wait_send()
        for cp in mine:
            cp.wait()

    return pl.pallas_call(
        body, name=name,
        out_shape=[jax.ShapeDtypeStruct((N_DEV,) + p.shape, p.dtype) for p in parts],
        in_specs=[_ANY] * n, out_specs=[_ANY] * n,
        scratch_shapes=[pltpu.SemaphoreType.DMA((7 * n,)), pltpu.SemaphoreType.DMA((7 * n,)),
                        pltpu.SemaphoreType.DMA((n,))],
    )(*parts)


def _exchange_sibling(parts, name):
    n = len(parts)

    def body(*refs):
        g_refs, out_refs = refs[:n], refs[n:2 * n]
        send_sems, recv_sems = refs[2 * n:]
        x, y, c = _my_place()
        copies = [pltpu.make_async_remote_copy(
            src_ref=g_refs[a].at[:, 1 - c], dst_ref=out_refs[a],
            send_sem=send_sems.at[a], recv_sem=recv_sems.at[a],
            device_id=(x, y, 1 - c), device_id_type=MESH) for a in range(n)]
        for cp in copies:
            cp.start()
        for cp in copies:
            cp.wait()

    return pl.pallas_call(
        body, name=name,
        out_shape=[jax.ShapeDtypeStruct((4,) + g.shape[2:], g.dtype) for g in parts],
        in_specs=[_ANY] * n, out_specs=[_ANY] * n,
        scratch_shapes=[pltpu.SemaphoreType.DMA((n,)), pltpu.SemaphoreType.DMA((n,))],
    )(*parts)


def _exchange_chips(parts, name):
    n = len(parts)

    def body(*refs):
        p_refs, out_refs = refs[:n], refs[n:2 * n]
        send_sems, recv_sems, local_sems = refs[2 * n:]
        x, y, c = _my_place()
        my_chip = 2 * x + y
        own = [pltpu.make_async_copy(p_refs[a].at[my_chip], out_refs[a].at[my_chip], local_sems.at[a])
               for a in range(n)]
        for cp in own:
            cp.start()
        sends, recvs = [], []
        for k in range(1, 4):
            px = 1 - x if (k >> 1) & 1 else x
            py = 1 - y if k & 1 else y
            peer_chip = 2 * px + py
            for a in range(n):
                sem = 3 * a + k - 1
                sends.append(pltpu.make_async_remote_copy(
                    src_ref=p_refs[a].at[peer_chip], dst_ref=out_refs[a].at[my_chip],
                    send_sem=send_sems.at[sem], recv_sem=recv_sems.at[sem],
                    device_id=(px, py, c), device_id_type=MESH))
                recvs.append(pltpu.make_async_remote_copy(
                    src_ref=p_refs[a].at[peer_chip], dst_ref=out_refs[a].at[peer_chip],
                    send_sem=send_sems.at[sem], recv_sem=recv_sems.at[sem],
                    device_id=(px, py, c), device_id_type=MESH))
        for cp in sends:
            cp.start()
        for send, recv in zip(sends, recvs):
            send.wait_send()
            recv.wait_recv()
        for cp in own:
            cp.wait()

    return pl.pallas_call(
        body, name=name,
        out_shape=[jax.ShapeDtypeStruct(p.shape, p.dtype) for p in parts],
        in_specs=[_ANY] * n, out_specs=[_ANY] * n,
        scratch_shapes=[pltpu.SemaphoreType.DMA((3 * n,)), pltpu.SemaphoreType.DMA((3 * n,)),
                        pltpu.SemaphoreType.DMA((n,))],
    )(*parts)


_HBM = pl.BlockSpec(memory_space=pltpu.HBM)
_SEM = pl.BlockSpec(memory_space=pltpu.SEMAPHORE)
_EFFECT = pltpu.SideEffectType.DATAFLOW_SIDE_EFFECTING


def _peer(x, y, c, k):
    return (1 - x if (k >> 2) & 1 else x, 1 - y if (k >> 1) & 1 else y, 1 - c if k & 1 else c)


def _spread_start(srcs, scatter, name):
    n = len(srcs)
    lands = [lax.empty(s.shape if scatter else (N_DEV,) + s.shape, s.dtype) for s in srcs]

    def body(*refs):
        src_refs, land_refs = refs[:n], refs[n:2 * n]
        send_sems, recv_sems = refs[2 * n:2 * n + 2]
        token = refs[-1]
        x, y, c = _my_place()
        my_slot = 4 * x + 2 * y + c
        for a in range(n):
            for k in range(N_DEV):
                px, py, pc = _peer(x, y, c, k)
                src = src_refs[a].at[4 * px + 2 * py + pc] if scatter else src_refs[a]
                pltpu.make_async_remote_copy(
                    src_ref=src, dst_ref=land_refs[a].at[my_slot],
                    send_sem=send_sems.at[a], recv_sem=recv_sems.at[a],
                    device_id=(px, py, pc), device_id_type=MESH).start()
        token[...] = jnp.zeros_like(token)

    out = pl.pallas_call(
        body, name=name,
        out_shape=[pltpu.SemaphoreType.DMA((n,)), pltpu.SemaphoreType.DMA((n,))]
        + [pltpu.HBM(s.shape, s.dtype) for s in srcs] + [pltpu.HBM(l.shape, l.dtype) for l in lands]
        + [jax.ShapeDtypeStruct((8, 128), F32)],
        in_specs=[_HBM] * (2 * n),
        out_specs=[_SEM, _SEM] + [_HBM] * (2 * n) + [pl.BlockSpec(memory_space=pltpu.VMEM)],
        input_output_aliases={i: 2 + i for i in range(2 * n)},
        compiler_params=pltpu.CompilerParams(has_side_effects=_EFFECT),
    )(*[pltpu.with_memory_space_constraint(s, pltpu.HBM) for s in srcs],
      *[pltpu.with_memory_space_constraint(l, pltpu.HBM) for l in lands])
    return out[0], out[1], out[2:2 + n], out[2 + n:2 + 2 * n], out[-1]


def _spread_wait(send_sems, recv_sems, srcs, lands, after, name):
    n = len(srcs)

    def body(*refs):
        land_refs = refs[n:2 * n]
        send_sems, recv_sems = refs[2 * n:2 * n + 2]
        x, y, c = _my_place()
        for a in range(n):
            whole = pltpu.make_async_remote_copy(
                src_ref=land_refs[a], dst_ref=land_refs[a],
                send_sem=send_sems.at[a], recv_sem=recv_sems.at[a],
                device_id=(x, y, c), device_id_type=MESH)
            whole.wait_send()
            whole.wait_recv()

    out = pl.pallas_call(
        body, name=name,
        out_shape=[pltpu.HBM(s.shape, s.dtype) for s in srcs] + [pltpu.HBM(l.shape, l.dtype) for l in lands],
        in_specs=[_HBM] * (2 * n) + [_SEM, _SEM, _ANY],
        out_specs=[_HBM] * (2 * n),
        input_output_aliases={i: i for i in range(2 * n)},
        compiler_params=pltpu.CompilerParams(has_side_effects=_EFFECT),
    )(*srcs, *lands, send_sems, recv_sems, after)
    return out[n:]


def _all_gather_rows_vmem(v, name):
    m_per, n = v.shape

    def body(v_ref, out_ref, send_sems, recv_sems, local_sem):
        x, y, c = _my_place()
        me, sibling = (x, y, c), (x, y, 1 - c)
        chips = [(1 - x, y), (x, 1 - y), (1 - x, 1 - y)]

        def rows(px, py, pc):
            return out_ref.at[pl.ds((4 * px + 2 * py + pc) * m_per, m_per), :]

        def copy(k, block, to, src=None):
            return pltpu.make_async_remote_copy(
                src_ref=rows(*block) if src is None else src, dst_ref=rows(*block),
                send_sem=send_sems.at[k], recv_sem=recv_sems.at[k],
                device_id=to, device_id_type=MESH)

        mine = pltpu.make_async_copy(v_ref, rows(*me), local_sem)
        mine.start()
        first = [copy(0, me, sibling, src=v_ref)]
        first += [copy(1 + j, me, (*chip, c), src=v_ref) for j, chip in enumerate(chips)]
        for cp in first:
            cp.start()
        passed = [copy(4 + j, (*chip, c), sibling) for j, chip in enumerate(chips)]
        for j, chip in enumerate(chips):
            copy(1 + j, (*chip, c), me).wait_recv()
            passed[j].start()
        copy(0, sibling, me).wait_recv()
        for j, chip in enumerate(chips):
            copy(4 + j, (*chip, 1 - c), me).wait_recv()
        for cp in first + passed:
            cp.wait_send()
        mine.wait()

    return pl.pallas_call(
        body, name=name,
        out_shape=jax.ShapeDtypeStruct((N_DEV * m_per, n), v.dtype),
        in_specs=[pl.BlockSpec(memory_space=pltpu.VMEM)],
        out_specs=pl.BlockSpec(memory_space=pltpu.VMEM),
        scratch_shapes=[pltpu.SemaphoreType.DMA((7,)), pltpu.SemaphoreType.DMA((7,)),
                        pltpu.SemaphoreType.DMA(())],
    )(v)


def _all_reduce_vmem(v, name):
    def body(v_ref, out_ref, buf, send_sems, recv_sems):
        x, y, c = _my_place()
        peers = [(x, y, 1 - c), (1 - x, y, c), (x, 1 - y, c)]
        out_ref[...] = v_ref[...]
        for step, peer in enumerate(peers):
            cp = pltpu.make_async_remote_copy(
                src_ref=out_ref, dst_ref=buf.at[step],
                send_sem=send_sems.at[step], recv_sem=recv_sems.at[step],
                device_id=peer, device_id_type=MESH)
            cp.start()
            cp.wait()
            out_ref[...] = out_ref[...] + buf[step]

    return pl.pallas_call(
        body, name=name,
        out_shape=jax.ShapeDtypeStruct(v.shape, v.dtype),
        in_specs=[pl.BlockSpec(memory_space=pltpu.VMEM)],
        out_specs=pl.BlockSpec(memory_space=pltpu.VMEM),
        scratch_shapes=[pltpu.VMEM((3,) + v.shape, v.dtype),
                        pltpu.SemaphoreType.DMA((3,)), pltpu.SemaphoreType.DMA((3,))],
    )(v)


def _pick(n, pref):
    if n <= pref:
        return n
    t = pref - pref % 128
    while t > 0 and n % t:
        t -= 128
    return t if t > 0 else n


_DIMS = {"nn": (((1,), (0,)), ((), ())),
         "nt": (((1,), (1,)), ((), ())),
         "tn": (((0,), (0,)), ((), ()))}


def _matmul(a, b, mode, out_dtype, name):
    if mode == "nn":
        (m, k), (_, n) = a.shape, b.shape
    elif mode == "nt":
        (m, k), (n, _) = a.shape, b.shape
    else:
        (k, m), (_, n) = a.shape, b.shape
    tm, tn, tk = _pick(m, 512), _pick(n, 1536), _pick(k, 1024)
    nk = k // tk

    def body(a_ref, b_ref, o_ref, acc_ref):
        kk = pl.program_id(2)

        @pl.when(kk == 0)
        def _():
            acc_ref[...] = jnp.zeros_like(acc_ref)

        acc_ref[...] += lax.dot_general(
            a_ref[...].astype(BF), b_ref[...].astype(BF), _DIMS[mode], preferred_element_type=F32)

        @pl.when(kk == nk - 1)
        def _():
            o_ref[...] = acc_ref[...].astype(o_ref.dtype)

    a_spec = (pl.BlockSpec((tk, tm), lambda i, j, kk: (kk, i)) if mode == "tn"
              else pl.BlockSpec((tm, tk), lambda i, j, kk: (i, kk)))
    b_spec = (pl.BlockSpec((tn, tk), lambda i, j, kk: (j, kk)) if mode == "nt"
              else pl.BlockSpec((tk, tn), lambda i, j, kk: (kk, j)))
    return pl.pallas_call(
        body, name=name,
        out_shape=jax.ShapeDtypeStruct((m, n), out_dtype),
        grid=(m // tm, n // tn, nk),
        in_specs=[a_spec, b_spec],
        out_specs=pl.BlockSpec((tm, tn), lambda i, j, kk: (i, j)),
        scratch_shapes=[pltpu.VMEM((tm, tn), F32)],
        compiler_params=pltpu.CompilerParams(
            dimension_semantics=("parallel", "parallel", "arbitrary"),
            vmem_limit_bytes=VMEM_LIMIT),
    )(a, b)


def _rows2d(t, lead):
    return t.reshape(t.shape[:lead] + (math.prod(t.shape[lead:-1]), t.shape[-1]))


def _pair_sum(g, theirs, name):
    g3, t3 = _rows2d(g, 2), _rows2d(theirs, 1)
    _, r, w = t3.shape
    tr = _pick(r, 512)

    def body(g_ref, t_ref, o_ref):
        c = lax.axis_index("c")
        mine = jnp.where(c == 0, g_ref[0, 0], g_ref[0, 1])
        o_ref[0] = (mine.astype(F32) + t_ref[0].astype(F32)).astype(o_ref.dtype)

    out = pl.pallas_call(
        body, name=name,
        out_shape=jax.ShapeDtypeStruct(t3.shape, t3.dtype),
        grid=(4, r // tr),
        in_specs=[pl.BlockSpec((1, 2, tr, w), lambda q, i: (q, 0, i, 0)),
                  pl.BlockSpec((1, tr, w), lambda q, i: (q, i, 0))],
        out_specs=pl.BlockSpec((1, tr, w), lambda q, i: (q, i, 0)),
        compiler_params=pltpu.CompilerParams(dimension_semantics=("parallel", "parallel")),
    )(g3, t3)
    return out.reshape(theirs.shape)


def _adam_update(w, g, m, v):
    c1 = 1.0 - ADAM_B1 ** ADAM_STEP
    c2 = 1.0 - ADAM_B2 ** ADAM_STEP
    nm = ADAM_B1 * m + (1.0 - ADAM_B1) * g
    nv = ADAM_B2 * v + (1.0 - ADAM_B2) * (g * g)
    delta = -ADAM_LR * ((nm / c1) / (jnp.sqrt(nv / c2) + ADAM_EPS) + ADAM_WD * w)
    return delta, nm, nv


def _sum_adamw(parts, w, m, v, layer, carry, name):
    shape = w.shape
    cols = shape[-1]
    p3 = _rows2d(parts, 1)
    w3, m3, v3 = (_rows2d(t, 1) for t in (w, m, v))
    rows = w3.shape[1]
    tr = _pick(rows, 128)
    n_parts = p3.shape[0]

    def body(p_ref, w_ref, m_ref, v_ref, *rest):
        g_ref, d_ref, nm_ref, nv_ref = rest[-4:]
        g = p_ref[0, :, :cols].astype(F32)
        for q in range(1, n_parts):
            g = g + p_ref[q, :, :cols].astype(F32)
        d, nm, nv = _adam_update(w_ref[0], g, m_ref[0], v_ref[0])
        g_ref[0] = g
        d_ref[0] = d
        nm_ref[0] = nm
        nv_ref[0] = nv

    spec = pl.BlockSpec((1, tr, cols), lambda i: (layer, i, 0))
    extra = [] if carry is None else [_rows2d(t, 1) for t in carry]
    out = pl.pallas_call(
        body, name=name,
        out_shape=[jax.ShapeDtypeStruct(w3.shape, F32)] * 4,
        grid=(rows // tr,),
        in_specs=[pl.BlockSpec((n_parts, tr, p3.shape[-1]), lambda i: (0, i, 0)), spec, spec, spec] + [_ANY] * len(extra),
        out_specs=[spec] * 4,
        input_output_aliases={4 + i: i for i in range(len(extra))},
        compiler_params=pltpu.CompilerParams(dimension_semantics=("parallel",)),
    )(p3, w3, m3, v3, *extra)
    return tuple(t.reshape(shape) for t in out)


def _adamw(w, g, m, v, name):
    rows, cols = w.shape
    tr = _pick(rows, 128)

    def body(w_ref, g_ref, m_ref, v_ref, d_ref, nm_ref, nv_ref):
        d, nm, nv = _adam_update(w_ref[...], g_ref[...], m_ref[...], v_ref[...])
        d_ref[...] = d
        nm_ref[...] = nm
        nv_ref[...] = nv

    spec = pl.BlockSpec((tr, cols), lambda i: (i, 0))
    return pl.pallas_call(
        body, name=name,
        out_shape=[jax.ShapeDtypeStruct((rows, cols), F32)] * 3,
        grid=(rows // tr,),
        in_specs=[spec] * 4, out_specs=[spec] * 3,
        compiler_params=pltpu.CompilerParams(dimension_semantics=("parallel",)),
    )(w, g, m, v)


_VJP = {"nn": (("nt", "gb"), ("tn", "ag")),
        "nt": (("nn", "gb"), ("tn", "ga")),
        "tn": (("nt", "bg"), ("nn", "ag"))}


def _make_dot(cast, precision):
    def raw(mode, a, b):
        return lax.dot_general(cast(a), cast(b), _DIMS[mode], precision=precision,
                               preferred_element_type=F32)

    @functools.partial(jax.custom_vjp, nondiff_argnums=(0,))
    def dot(mode, a, b):
        return raw(mode, a, b)

    def fwd(mode, a, b):
        return raw(mode, a, b), (a, b)

    def bwd(mode, res, g):
        a, b = res
        pick = {"a": a, "b": b, "g": g}
        (ma, ta), (mb, tb) = _VJP[mode]
        return dot(ma, pick[ta[0]], pick[ta[1]]), dot(mb, pick[tb[0]], pick[tb[1]])

    dot.defvjp(fwd, bwd)
    return dot


bdot = _make_dot(lambda t: t.astype(BF), None)
hdot = _make_dot(lambda t: t, lax.Precision.HIGHEST)


def _xdot(mode, a, b):
    return lax.dot_general(a, b, _DIMS[mode], precision=lax.Precision.HIGH, preferred_element_type=F32)


def _unit_lower_inverse(L):
    n = L.shape[0]
    eye = (lax.broadcasted_iota(jnp.int32, (n, n), 0) == lax.broadcasted_iota(jnp.int32, (n, n), 1)).astype(F32)
    t_inv = eye - L
    p = L
    for _ in range(6):
        p = _xdot("nn", p, p)
        t_inv = t_inv + _xdot("nn", t_inv, p)
    return t_inv


@jax.custom_vjp
def _tri_solve(L, rhs, t_inv):
    return _xdot("nn", t_inv, rhs)


def _tri_solve_fwd(L, rhs, t_inv):
    sol = _xdot("nn", t_inv, rhs)
    return sol, (t_inv, sol)


def _tri_solve_bwd(res, dsol):
    t_inv, sol = res
    drhs = _xdot("tn", t_inv, dsol)
    return -_xdot("nt", drhs, sol), drhs, jnp.zeros_like(t_inv)


_tri_solve.defvjp(_tri_solve_fwd, _tri_solve_bwd)


def _sigmoid(x):
    return 1.0 / (1.0 + jnp.exp(-x))


def _softplus(x):
    return jnp.maximum(x, 0.0) + jnp.log(1.0 + jnp.exp(-jnp.abs(x)))


def _dn_chunk(S, xs, ba, z, cw, al, dt, dn, t_saved=None):
    C = DN_C
    pre = xs[0] * cw[0] + xs[1] * cw[1] + xs[2] * cw[2] + xs[3] * cw[3]
    qkv = pre * _sigmoid(pre)
    lane = lax.broadcasted_iota(jnp.int32, (1, 128), 1)
    sub = lax.broadcasted_iota(jnp.int32, (C, 1), 0)
    row_i = lax.broadcasted_iota(jnp.int32, (C, C), 0)
    col_i = lax.broadcasted_iota(jnp.int32, (C, C), 1)
    strict = row_i > col_i
    incl = row_i >= col_i
    g_all = jnp.where((lane >= 4) & (lane < 8), -jnp.exp(al) * _softplus(ba + dt), 0.0)
    gc_all = hdot("nn", incl.astype(F32), g_all)
    gc_all_t = gc_all.T
    beta_all = _sigmoid(ba)
    glast_all = jnp.sum(jnp.where(sub == C - 1, gc_all, 0.0), axis=0, keepdims=True)
    ys, s_new, t_invs = [], [], []
    for h in range(DN_HEADS):
        q = qkv[:, 128 * h:128 * (h + 1)]
        k = qkv[:, 512 + 128 * h:512 + 128 * (h + 1)]
        v = qkv[:, 1024 + 128 * h:1024 + 128 * (h + 1)]
        q = q * lax.rsqrt(jnp.sum(q * q, axis=1, keepdims=True) + EPS) * (DN_D ** -0.5)
        k = k * lax.rsqrt(jnp.sum(k * k, axis=1, keepdims=True) + EPS)
        beta = jnp.sum(jnp.where(lane == h, beta_all, 0.0), axis=1, keepdims=True)
        gc = jnp.sum(jnp.where(lane == 4 + h, gc_all, 0.0), axis=1, keepdims=True)
        gc_row = jnp.sum(jnp.where(sub == 4 + h, gc_all_t, 0.0), axis=0, keepdims=True)
        g_last = jnp.sum(jnp.where(lane == 4 + h, glast_all, 0.0), axis=1, keepdims=True)
        diff = gc - gc_row
        kb = k * beta
        L = jnp.where(strict, bdot("nt", kb, k) * jnp.exp(jnp.where(strict, diff, 0.0)), 0.0)
        t_inv = _unit_lower_inverse(L) if t_saved is None else t_saved[h]
        t_invs.append(t_inv)
        sol = _tri_solve(L, jnp.concatenate([v * beta, kb * jnp.exp(gc)], axis=1), t_inv)
        u, w = sol[:, :DN_D], sol[:, DN_D:]
        a_qk = jnp.where(incl, bdot("nt", q, k) * jnp.exp(jnp.where(incl, diff, 0.0)), 0.0)
        qg = q * jnp.exp(gc)
        kd = k * jnp.exp(g_last - gc)
        v_new = u - bdot("nn", w, S[h])
        o = bdot("nn", qg, S[h]) + bdot("nn", a_qk, v_new)
        s_new.append(S[h] * jnp.exp(g_last) + bdot("tn", kd, v_new))
        o = o * lax.rsqrt(jnp.mean(o * o, axis=1, keepdims=True) + EPS) * dn
        zh = z[:, 128 * h:128 * (h + 1)]
        ys.append(o * (zh * _sigmoid(zh)))
    return jnp.concatenate(ys, axis=1), tuple(s_new), tuple(t_invs)


def _load_shifted(xbuf, x_ref, halo_ref, first):
    xbuf[0:HALO, :] = jnp.where(first, 0.0, halo_ref[:, 0:1536])
    xbuf[HALO:HALO + DN_C, :] = x_ref[:, 0:1536]
    return [xbuf[HALO - 3 + k:HALO - 3 + k + DN_C, :] for k in range(4)]


def dn_forward(cols, jblk, cw, al, dt, dn, name):
    T = cols.shape[0]
    n = T // DN_C

    def body(x_ref, halo_ref, cw_ref, al_ref, dt_ref, dn_ref, y_ref, ss_ref, ts_ref, s_scr, xbuf):
        i = pl.program_id(0)

        @pl.when(i == 0)
        def _():
            s_scr[...] = jnp.zeros_like(s_scr)

        xs = _load_shifted(xbuf, x_ref, halo_ref, i == 0)
        ss_ref[0] = s_scr[...]
        S = [s_scr[h] for h in range(DN_HEADS)]
        cws = [cw_ref[k:k + 1, :] for k in range(4)]
        y, s_new, t_invs = _dn_chunk(S, xs, x_ref[:, 2048:2176], x_ref[:, 1536:2048], cws,
                                     al_ref[...], dt_ref[...], dn_ref[...])
        y_ref[...] = y
        for h in range(DN_HEADS):
            s_scr[h] = s_new[h]
            ts_ref[0, h] = t_invs[h]

    per = DN_C // HALO
    full = lambda shape: pl.BlockSpec(shape, lambda i: (0,) * len(shape))
    return pl.pallas_call(
        body, name=name,
        out_shape=[jax.ShapeDtypeStruct((T, 512), F32),
                   jax.ShapeDtypeStruct((n, DN_HEADS, DN_D, DN_D), F32),
                   jax.ShapeDtypeStruct((n, DN_HEADS, DN_D, DN_D), F32)],
        grid=(n,),
        in_specs=[pl.BlockSpec((DN_C, DN_W), lambda i: (i, jblk)),
                  pl.BlockSpec((HALO, DN_W), lambda i: (jnp.maximum(i * per - 1, 0), jblk)),
                  full((4, 1536)), full((1, 128)), full((1, 128)), full((1, 128))],
        out_specs=[pl.BlockSpec((DN_C, 512), lambda i: (i, 0)),
                   pl.BlockSpec((1, DN_HEADS, DN_D, DN_D), lambda i: (i, 0, 0, 0)),
                   pl.BlockSpec((1, DN_HEADS, DN_D, DN_D), lambda i: (i, 0, 0, 0))],
        scratch_shapes=[pltpu.VMEM((DN_HEADS, DN_D, DN_D), F32), pltpu.VMEM((HALO + DN_C, 1536), F32)],
        compiler_params=pltpu.CompilerParams(dimension_semantics=("arbitrary",)),
    )(cols, cols, cw, al, dt, dn)


def dn_backward(cols, jblk, cw, al, dt, dn, ss, ts, dy, dcols, name):
    T = cols.shape[0]
    n = T // DN_C

    def body(x_ref, halo_ref, cw_ref, al_ref, dt_ref, dn_ref, ss_ref, ts_ref, dy_ref, dcols_in,
             dx_ref, dcw_ref, dal_ref, ddt_ref, ddn_ref, ds_scr, xbuf, dbuf, carry):
        i = pl.program_id(0)

        @pl.when(i == 0)
        def _():
            ds_scr[...] = jnp.zeros_like(ds_scr)
            carry[...] = jnp.zeros_like(carry)
            dcw_ref[...] = jnp.zeros_like(dcw_ref)
            dal_ref[...] = jnp.zeros_like(dal_ref)
            ddt_ref[...] = jnp.zeros_like(ddt_ref)
            ddn_ref[...] = jnp.zeros_like(ddn_ref)

        xs = _load_shifted(xbuf, x_ref, halo_ref, i == n - 1)
        S = [ss_ref[0, h] for h in range(DN_HEADS)]
        cws = [cw_ref[k:k + 1, :] for k in range(4)]

        t_saved = [ts_ref[0, h] for h in range(DN_HEADS)]

        def f(S, xs, ba, z, cws, al, dt, dn):
            return _dn_chunk(S, xs, ba, z, cws, al, dt, dn, t_saved)[:2]

        _, vjp = jax.vjp(f, S, xs, x_ref[:, 2048:2176], x_ref[:, 1536:2048], cws, al_ref[...], dt_ref[...], dn_ref[...])
        dS, dxs, dba, dz, dcws, dal, ddt, ddn = vjp((dy_ref[...], tuple(ds_scr[h] for h in range(DN_HEADS))))
        for h in range(DN_HEADS):
            ds_scr[h] = dS[h]
        dbuf[...] = jnp.zeros_like(dbuf)
        for k in range(4):
            lo = HALO - 3 + k
            dbuf[lo:lo + DN_C, :] += dxs[k]
        dbuf[DN_C:DN_C + HALO, :] += carry[...]
        dx_ref[...] = jnp.concatenate([dbuf[HALO:HALO + DN_C, :], dz, dba,
                                       jnp.zeros((DN_C, DN_W - 2176), F32)], axis=1)
        carry[...] = dbuf[0:HALO, :]
        for k in range(4):
            dcw_ref[k:k + 1, :] += dcws[k]
        dal_ref[...] += dal
        ddt_ref[...] += ddt
        ddn_ref[...] += ddn

    per = DN_C // HALO
    rev = lambda i: n - 1 - i
    full = lambda shape: pl.BlockSpec(shape, lambda i: (0,) * len(shape))
    return pl.pallas_call(
        body, name=name,
        out_shape=[jax.ShapeDtypeStruct(dcols.shape, F32), jax.ShapeDtypeStruct((4, 1536), F32),
                   jax.ShapeDtypeStruct((1, 128), F32), jax.ShapeDtypeStruct((1, 128), F32),
                   jax.ShapeDtypeStruct((1, 128), F32)],
        grid=(n,),
        in_specs=[pl.BlockSpec((DN_C, DN_W), lambda i: (rev(i), jblk)),
                  pl.BlockSpec((HALO, DN_W), lambda i: (jnp.maximum(rev(i) * per - 1, 0), jblk)),
                  full((4, 1536)), full((1, 128)), full((1, 128)), full((1, 128)),
                  pl.BlockSpec((1, DN_HEADS, DN_D, DN_D), lambda i: (rev(i), 0, 0, 0)),
                  pl.BlockSpec((1, DN_HEADS, DN_D, DN_D), lambda i: (rev(i), 0, 0, 0)),
                  pl.BlockSpec((DN_C, 512), lambda i: (rev(i), 0)), _ANY],
        out_specs=[pl.BlockSpec((DN_C, DN_W), lambda i: (rev(i), jblk)),
                   full((4, 1536)), full((1, 128)), full((1, 128)), full((1, 128))],
        scratch_shapes=[pltpu.VMEM((DN_HEADS, DN_D, DN_D), F32), pltpu.VMEM((HALO + DN_C, 1536), F32),
                        pltpu.VMEM((HALO + DN_C, 1536), F32), pltpu.VMEM((HALO, 1536), F32)],
        input_output_aliases={9: 0},
        compiler_params=pltpu.CompilerParams(dimension_semantics=("arbitrary",)),
    )(cols, cols, cw, al, dt, dn, ss, ts, dy, dcols)


def _full(shape):
    return pl.BlockSpec(shape, lambda i: (0,) * len(shape))


def _silu(x):
    return x * _sigmoid(x)


def _gelu(x):
    return 0.5 * x * (1.0 + jnp.tanh(0.7978845608028654 * (x + 0.044715 * (x * x * x))))


def _lane_col(mat, idx):
    lane = lax.broadcasted_iota(jnp.int32, (1, mat.shape[1]), 1)
    return jnp.sum(jnp.where(lane == idx, mat, 0.0), axis=1, keepdims=True)


def _gm_chunk(uv, z, gain, ws, bt):
    g = _gelu(uv)
    u, v = g[:, :512], g[:, 512:]
    v = v * lax.rsqrt(jnp.mean(v * v, axis=1, keepdims=True) + EPS) * gain
    row_i = lax.broadcasted_iota(jnp.int32, (BLK, BLK), 0)
    col_i = lax.broadcasted_iota(jnp.int32, (BLK, BLK), 1)
    causal = row_i >= col_i
    ss = []
    for grp in range(4):
        wg = jnp.where(causal, ws[grp], 0.0)
        ss.append(bdot("nn", wg, v[:, BLK * grp:BLK * (grp + 1)]) + _lane_col(bt, grp))
    return u * jnp.concatenate(ss, axis=1) * _silu(z)


def gm_forward(cols, jblk, gain, ws, bt, name):
    T = cols.shape[0]

    def body(x_ref, gain_ref, ws_ref, bt_ref, y_ref):
        y_ref[...] = _gm_chunk(x_ref[:, 0:1024], x_ref[:, 1024:1536], gain_ref[...],
                               [ws_ref[g] for g in range(4)], bt_ref[...])

    return pl.pallas_call(
        body, name=name, out_shape=jax.ShapeDtypeStruct((T, 512), F32), grid=(T // BLK,),
        in_specs=[pl.BlockSpec((BLK, GM_W), lambda i: (i, jblk)),
                  _full((1, 512)), _full((4, BLK, BLK)), _full((BLK, BLK))],
        out_specs=pl.BlockSpec((BLK, 512), lambda i: (i, 0)),
        compiler_params=pltpu.CompilerParams(dimension_semantics=("parallel",)),
    )(cols, gain, ws, bt)


def gm_backward(cols, jblk, gain, ws, bt, dy, dcols, name):
    T = cols.shape[0]

    def body(x_ref, gain_ref, ws_ref, bt_ref, dy_ref, dcols_in, dx_ref, dgain_ref, dws_ref, dbt_ref):
        @pl.when(pl.program_id(0) == 0)
        def _():
            dgain_ref[...] = jnp.zeros_like(dgain_ref)
            dws_ref[...] = jnp.zeros_like(dws_ref)
            dbt_ref[...] = jnp.zeros_like(dbt_ref)

        _, vjp = jax.vjp(_gm_chunk, x_ref[:, 0:1024], x_ref[:, 1024:1536], gain_ref[...],
                         [ws_ref[g] for g in range(4)], bt_ref[...])
        duv, dz, dgain, dws, dbt = vjp(dy_ref[...])
        dx_ref[...] = jnp.concatenate([duv, dz], axis=1)
        dgain_ref[...] += dgain
        for g in range(4):
            dws_ref[g] += dws[g]
        dbt_ref[...] += dbt

    return pl.pallas_call(
        body, name=name,
        out_shape=[jax.ShapeDtypeStruct(dcols.shape, F32), jax.ShapeDtypeStruct((1, 512), F32),
                   jax.ShapeDtypeStruct((4, BLK, BLK), F32), jax.ShapeDtypeStruct((BLK, BLK), F32)],
        grid=(T // BLK,),
        in_specs=[pl.BlockSpec((BLK, GM_W), lambda i: (i, jblk)),
                  _full((1, 512)), _full((4, BLK, BLK)), _full((BLK, BLK)),
                  pl.BlockSpec((BLK, 512), lambda i: (i, 0)), _ANY],
        out_specs=[pl.BlockSpec((BLK, GM_W), lambda i: (i, jblk)),
                   _full((1, 512)), _full((4, BLK, BLK)), _full((BLK, BLK))],
        input_output_aliases={5: 0},
        compiler_params=pltpu.CompilerParams(dimension_semantics=("arbitrary",)),
    )(cols, gain, ws, bt, dy, dcols)


def _sw_block(first, q, kp, kc, vp, vc, z, sinks):
    P = BLK
    lane = lax.broadcasted_iota(jnp.int32, (1, 128), 1)
    r = lax.broadcasted_iota(jnp.int32, (128, 128), 0)
    c = lax.broadcasted_iota(jnp.int32, (128, 128), 1)
    swap = (c == (r + 64) % 128).astype(F32)
    k2 = jnp.concatenate([kp, kc], axis=0)
    v2 = jnp.concatenate([vp, vc], axis=0)
    k2s = bdot("nn", k2, swap)
    v2s = bdot("nn", v2, swap)
    qi = lax.broadcasted_iota(jnp.int32, (P, 2 * P), 0)
    kj = lax.broadcasted_iota(jnp.int32, (P, 2 * P), 1)
    dist = qi + P - kj
    valid = (dist >= 0) & (dist < P) & ((kj >= P) | jnp.logical_not(first))
    outs = []
    for j in range(4):
        acc = jnp.zeros((P, 128), F32)
        for half in range(2):
            h = 2 * j + half
            kv = h // 4
            in_half = (lane >= 64 * half) & (lane < 64 * half + 64)
            qh = jnp.where(in_half, q[:, 128 * j:128 * (j + 1)], 0.0)
            same = (half == kv)
            s = bdot("nt", qh, k2 if same else k2s) * (64 ** -0.5)
            s = jnp.where(valid, s, NEG_INF)
            sink = _lane_col(sinks, h)
            m = lax.stop_gradient(jnp.maximum(jnp.max(s, axis=1, keepdims=True), sink))
            e = jnp.exp(s - m)
            p = e / (jnp.sum(e, axis=1, keepdims=True) + jnp.exp(sink - m))
            o = bdot("nn", p, v2 if same else v2s)
            acc = acc + jnp.where(in_half, o, 0.0)
        outs.append(acc)
    return jnp.concatenate(outs, axis=1) * _silu(z)


def _sw_specs(jblk, idx):
    prev = lambda i: jnp.maximum(idx(i) - 1, 0)
    jk = (jblk * SW_W + 1024) // 128
    return [pl.BlockSpec((BLK, SW_W), lambda i: (idx(i), jblk)),
            pl.BlockSpec((BLK, 128), lambda i: (prev(i), jk)),
            pl.BlockSpec((BLK, 128), lambda i: (prev(i), jk + 1)), _full((1, 128))]


def sw_forward(cols, jblk, sinks, name):
    T = cols.shape[0]

    def body(x_ref, kp_ref, vp_ref, s_ref, y_ref):
        y_ref[...] = _sw_block(pl.program_id(0) == 0, x_ref[:, 0:512], kp_ref[...], x_ref[:, 1024:1152],
                               vp_ref[...], x_ref[:, 1152:1280], x_ref[:, 512:1024], s_ref[...])

    return pl.pallas_call(
        body, name=name, out_shape=jax.ShapeDtypeStruct((T, 512), F32), grid=(T // BLK,),
        in_specs=_sw_specs(jblk, lambda i: i),
        out_specs=pl.BlockSpec((BLK, 512), lambda i: (i, 0)),
        compiler_params=pltpu.CompilerParams(dimension_semantics=("parallel",)),
    )(cols, cols, cols, sinks)


def sw_backward(cols, jblk, sinks, dy, dcols, name):
    T = cols.shape[0]
    n = T // BLK
    rev = lambda i: n - 1 - i

    def body(x_ref, kp_ref, vp_ref, s_ref, dy_ref, dcols_in, dx_ref, ds_ref, kcarry, vcarry):
        i = pl.program_id(0)

        @pl.when(i == 0)
        def _():
            kcarry[...] = jnp.zeros_like(kcarry)
            vcarry[...] = jnp.zeros_like(vcarry)
            ds_ref[...] = jnp.zeros_like(ds_ref)

        f = functools.partial(_sw_block, i == n - 1)
        _, vjp = jax.vjp(f, x_ref[:, 0:512], kp_ref[...], x_ref[:, 1024:1152], vp_ref[...], x_ref[:, 1152:1280],
                         x_ref[:, 512:1024], s_ref[...])
        dq, dkp, dkc, dvp, dvc, dz, dsk = vjp(dy_ref[...])
        dx_ref[...] = jnp.concatenate([dq, dz, dkc + kcarry[...], dvc + vcarry[...],
                                       jnp.zeros((BLK, SW_W - 1280), F32)], axis=1)
        kcarry[...] = dkp
        vcarry[...] = dvp
        ds_ref[...] += dsk

    return pl.pallas_call(
        body, name=name,
        out_shape=[jax.ShapeDtypeStruct(dcols.shape, F32), jax.ShapeDtypeStruct((1, 128), F32)],
        grid=(n,),
        in_specs=_sw_specs(jblk, rev) + [pl.BlockSpec((BLK, 512), lambda i: (rev(i), 0)), _ANY],
        out_specs=[pl.BlockSpec((BLK, SW_W), lambda i: (rev(i), jblk)), _full((1, 128))],
        scratch_shapes=[pltpu.VMEM((BLK, 128), F32), pltpu.VMEM((BLK, 128), F32)],
        input_output_aliases={5: 0},
        compiler_params=pltpu.CompilerParams(dimension_semantics=("arbitrary",)),
    )(cols, cols, cols, sinks, dy, dcols)


XM_TQ = 256


def _xm_block(q, z, mkv):
    outs = []
    for h in range(4):
        s = bdot("nt", q[:, 128 * h:128 * (h + 1)], mkv[:, 128 * h:128 * (h + 1)]) * (128 ** -0.5)
        m = lax.stop_gradient(jnp.max(s, axis=1, keepdims=True))
        e = jnp.exp(s - m)
        p = e / jnp.sum(e, axis=1, keepdims=True)
        outs.append(bdot("nn", p, mkv[:, 512 + 128 * h:512 + 128 * (h + 1)]))
    return jnp.concatenate(outs, axis=1) * _silu(z)


def xm_forward(cols, jblk, mkv, name):
    T = cols.shape[0]

    def body(x_ref, m_ref, y_ref):
        y_ref[...] = _xm_block(x_ref[:, 0:512], x_ref[:, 512:1024], m_ref[...])

    return pl.pallas_call(
        body, name=name, out_shape=jax.ShapeDtypeStruct((T, 512), F32), grid=(T // XM_TQ,),
        in_specs=[pl.BlockSpec((XM_TQ, XM_W), lambda i: (i, jblk)), _full(mkv.shape)],
        out_specs=pl.BlockSpec((XM_TQ, 512), lambda i: (i, 0)),
        compiler_params=pltpu.CompilerParams(dimension_semantics=("parallel",)),
    )(cols, mkv)


def xm_backward(cols, jblk, mkv, dy, dcols, name):
    T = cols.shape[0]

    def body(x_ref, m_ref, dy_ref, dcols_in, dx_ref, dm_ref):
        @pl.when(pl.program_id(0) == 0)
        def _():
            dm_ref[...] = jnp.zeros_like(dm_ref)

        _, vjp = jax.vjp(_xm_block, x_ref[:, 0:512], x_ref[:, 512:1024], m_ref[...])
        dq, dz, dm = vjp(dy_ref[...])
        dx_ref[...] = jnp.concatenate([dq, dz], axis=1)
        dm_ref[...] += dm

    return pl.pallas_call(
        body, name=name,
        out_shape=[jax.ShapeDtypeStruct(dcols.shape, F32), jax.ShapeDtypeStruct(mkv.shape, F32)],
        grid=(T // XM_TQ,),
        in_specs=[pl.BlockSpec((XM_TQ, XM_W), lambda i: (i, jblk)), _full(mkv.shape),
                  pl.BlockSpec((XM_TQ, 512), lambda i: (i, 0)), _ANY],
        out_specs=[pl.BlockSpec((XM_TQ, XM_W), lambda i: (i, jblk)), _full(mkv.shape)],
        input_output_aliases={3: 0},
        compiler_params=pltpu.CompilerParams(dimension_semantics=("arbitrary",)),
    )(cols, mkv, dy, dcols)


def _rms(x, gain):
    return x * lax.rsqrt(jnp.mean(x * x, axis=1, keepdims=True) + EPS) * gain


def memkv_forward(mem, gain, w, name):
    def body(m_ref, g_ref, w_ref, o_ref):
        o_ref[...] = bdot("nn", _rms(m_ref[...], g_ref[...]), w_ref[...])

    return pl.pallas_call(body, name=name, out_shape=jax.ShapeDtypeStruct(mem.shape, F32),
                          compiler_params=pltpu.CompilerParams(vmem_limit_bytes=VMEM_LIMIT))(mem, gain, w)


def memkv_backward(mem, gain, w, dkv, name):
    def body(m_ref, g_ref, w_ref, d_ref, dg_ref, dw_ref):
        mem_v = m_ref[...]
        _, vjp = jax.vjp(lambda g, ww: bdot("nn", _rms(mem_v, g), ww), g_ref[...], w_ref[...].astype(F32))
        dg, dw = vjp(d_ref[...])
        dg_ref[...] = dg
        dw_ref[...] = dw

    return pl.pallas_call(body, name=name,
                          out_shape=[jax.ShapeDtypeStruct(gain.shape, F32), jax.ShapeDtypeStruct(w.shape, F32)],
                          compiler_params=pltpu.CompilerParams(vmem_limit_bytes=VMEM_LIMIT))(mem, gain, w, dkv)


MG_TB = 128


def _merge_block(ys, gl, wup, wout, gpost):
    merged = None
    for n in range(4):
        t = _sigmoid(gl[:, 1024 * n:1024 * (n + 1)]) * bdot("nn", ys[n], wup[n])
        merged = t if merged is None else merged + t
    out = bdot("nn", merged, wout)
    return _rms(out, gpost)


def merge_forward(ys, cols, jgate, x, wup, wout, gpost, name):
    T = x.shape[0]
    TB = 256

    def body(ya, yb, yc, ym, gl_ref, x_ref, wup_ref, wout_ref, gp_ref, o_ref):
        upd = _merge_block([ya[...], yb[...], yc[...], ym[...]], gl_ref[...],
                           [wup_ref[n] for n in range(4)], wout_ref[...], gp_ref[...])
        o_ref[...] = x_ref[...] + upd

    yspec = pl.BlockSpec((TB, 512), lambda i: (i, 0))
    return pl.pallas_call(
        body, name=name, out_shape=jax.ShapeDtypeStruct((T, 1024), F32), grid=(T // TB,),
        in_specs=[yspec] * 4 + [pl.BlockSpec((TB, 4096), lambda i: (i, jgate)),
                                pl.BlockSpec((TB, 1024), lambda i: (i, 0)),
                                _full(wup.shape), _full(wout.shape), _full((1, 1024))],
        out_specs=pl.BlockSpec((TB, 1024), lambda i: (i, 0)),
        compiler_params=pltpu.CompilerParams(dimension_semantics=("parallel",), vmem_limit_bytes=VMEM_LIMIT),
    )(*ys, cols, x, wup, wout, gpost)


def merge_backward(ys, cols, jgate, wup, wout, gpost, dx, name):
    T = dx.shape[0]
    TB = MG_TB

    def body(ya, yb, yc, ym, gl_ref, wup_ref, wout_ref, gp_ref, dx_ref,
             dgl_ref, dya, dyb, dyc, dym, dwup_ref, dwout_ref, dgp_ref):
        @pl.when(pl.program_id(0) == 0)
        def _():
            dwup_ref[...] = jnp.zeros_like(dwup_ref)
            dwout_ref[...] = jnp.zeros_like(dwout_ref)
            dgp_ref[...] = jnp.zeros_like(dgp_ref)

        _, vjp = jax.vjp(_merge_block, [ya[...], yb[...], yc[...], ym[...]], gl_ref[...],
                         [wup_ref[n].astype(F32) for n in range(4)], wout_ref[...].astype(F32), gp_ref[...])
        dys, dgl, dwup, dwout, dgp = vjp(dx_ref[...])
        for ref, val in zip((dya, dyb, dyc, dym), dys):
            ref[...] = val
        dgl_ref[...] = dgl
        for n in range(4):
            dwup_ref[n] += dwup[n]
        dwout_ref[...] += dwout
        dgp_ref[...] += dgp

    yspec = pl.BlockSpec((TB, 512), lambda i: (i, 0))
    return pl.pallas_call(
        body, name=name,
        out_shape=[jax.ShapeDtypeStruct(cols.shape, F32)] + [jax.ShapeDtypeStruct((T, 512), F32)] * 4 + [
            jax.ShapeDtypeStruct(wup.shape, F32), jax.ShapeDtypeStruct(wout.shape, F32),
            jax.ShapeDtypeStruct((1, 1024), F32)],
        grid=(T // TB,),
        in_specs=[yspec] * 4 + [pl.BlockSpec((TB, 4096), lambda i: (i, jgate)),
                                _full(wup.shape), _full(wout.shape), _full((1, 1024)),
                                pl.BlockSpec((TB, 1024), lambda i: (i, 0))],
        out_specs=[pl.BlockSpec((TB, 4096), lambda i: (i, jgate))] + [yspec] * 4 + [
            _full(wup.shape), _full(wout.shape), _full((1, 1024))],
        compiler_params=pltpu.CompilerParams(dimension_semantics=("arbitrary",), vmem_limit_bytes=VMEM_LIMIT),
    )(*ys, cols, wup, wout, gpost, dx)


NB = 256


def prenorm_forward(x, gain, name):
    T = x.shape[0]

    def body(x_ref, g_ref, o_ref):
        o_ref[...] = _rms(x_ref[...], g_ref[...]).astype(BF)

    return pl.pallas_call(
        body, name=name, out_shape=jax.ShapeDtypeStruct(x.shape, BF), grid=(T // NB,),
        in_specs=[pl.BlockSpec((NB, 1024), lambda i: (i, 0)), _full((1, 1024))],
        out_specs=pl.BlockSpec((NB, 1024), lambda i: (i, 0)),
        compiler_params=pltpu.CompilerParams(dimension_semantics=("parallel",)),
    )(x, gain)


def prenorm_backward(x, gain, dh, dres, name):
    T = x.shape[0]

    def body(x_ref, g_ref, dh_ref, dr_ref, dx_ref, dg_ref):
        @pl.when(pl.program_id(0) == 0)
        def _():
            dg_ref[...] = jnp.zeros_like(dg_ref)

        _, vjp = jax.vjp(_rms, x_ref[...], g_ref[...])
        dxn, dg = vjp(dh_ref[...])
        dx_ref[...] = dr_ref[...] + dxn
        dg_ref[...] += dg

    spec = pl.BlockSpec((NB, 1024), lambda i: (i, 0))
    return pl.pallas_call(
        body, name=name,
        out_shape=[jax.ShapeDtypeStruct(x.shape, F32), jax.ShapeDtypeStruct((1, 1024), F32)], grid=(T // NB,),
        in_specs=[spec, _full((1, 1024)), spec, spec], out_specs=[spec, _full((1, 1024))],
        compiler_params=pltpu.CompilerParams(dimension_semantics=("arbitrary",)),
    )(x, gain, dh, dres)


def loss_head(y, target, name):
    T, D = y.shape

    def body(y_ref, t_ref, l_ref, d_ref):
        @pl.when(pl.program_id(0) == 0)
        def _():
            l_ref[...] = jnp.zeros_like(l_ref)

        err = y_ref[...] - t_ref[...]
        d_ref[...] = err * (1.0 / D)
        l_ref[...] += jnp.full(l_ref.shape, 0.5 * jnp.sum(jnp.mean(err * err, axis=1, keepdims=True)), F32)

    spec = pl.BlockSpec((NB, D), lambda i: (i, 0))
    return pl.pallas_call(
        body, name=name,
        out_shape=[jax.ShapeDtypeStruct((1, 128), F32), jax.ShapeDtypeStruct(y.shape, F32)], grid=(T // NB,),
        in_specs=[spec, spec], out_specs=[_full((1, 128)), spec],
        compiler_params=pltpu.CompilerParams(dimension_semantics=("arbitrary",)),
    )(y, target)


JB_GATE, JB_XM, JB_DN, JB_SW, JB_GM = 0, 4, 2, 5, 6
_ALIGNED_PIECES = ((5896, 4096), (4872, 512), (5384, 512), (0, 2048), (2048, 8), 504, (3592, 512), (4360, 512),
                   (4104, 128), (4232, 128), 256, (2056, 1024), (3080, 512))
_NATURAL_FROM_ALIGNED = ((5120, 2048), (7168, 8), (9216, 1024), (10240, 512), (7680, 512), (8704, 128), (8832, 128),
                         (8192, 512), (4096, 512), (4608, 512), (0, 4096))


def _natural_range(slots, start, width):
    out = []
    while width > 0:
        j, i = divmod(start, W_IN_SHARD)
        take = min(width, W_IN_SHARD - i)
        out.append(slots[j, :, i:i + take])
        start, width = start + take, width - take
    return out


def _aligned_w_in(slots):
    parts = []
    for piece in _ALIGNED_PIECES:
        if isinstance(piece, int):
            parts.append(jnp.zeros(slots.shape[1:2] + (piece,), slots.dtype))
        else:
            parts += _natural_range(slots, *piece)
    return jnp.concatenate(parts, axis=-1)


def _slots_of_aligned(d_al):
    slots = []
    for s in range(N_DEV):
        lo, hi = s * W_IN_SHARD, (s + 1) * W_IN_SHARD
        parts, nat = [], 0
        for a_start, width in _NATURAL_FROM_ALIGNED:
            b, e = max(lo, nat), min(hi, nat + width)
            if b < e:
                parts.append(d_al[..., a_start + b - nat:a_start + e - nat])
            nat += width
        parts.append(jnp.zeros(d_al.shape[:1] + (W_IN_SHARD_PAD - W_IN_SHARD,), d_al.dtype))
        slots.append(jnp.concatenate(parts, axis=-1))
    return jnp.stack(slots)


SMALL_VEC_W = 1024


def _pack_small(parts):
    rows = []
    for p in parts:
        flat = p.reshape(-1).astype(F32)
        r = -(-flat.shape[0] // SMALL_VEC_W)
        rows.append(jnp.pad(flat, (0, r * SMALL_VEC_W - flat.shape[0])).reshape(r, SMALL_VEC_W))
    vec = jnp.concatenate(rows, axis=0)
    return jnp.pad(vec, ((0, -vec.shape[0] % 8), (0, 0)))


def _unpack_small(vec, shapes):
    out, off = [], 0
    for s in shapes:
        n = math.prod(s)
        r = -(-n // SMALL_VEC_W)
        out.append(vec[off:off + r].reshape(-1)[:n].reshape(s))
        off += r
    return out


def _lanes(vec, at):
    return jnp.zeros((1, 128), F32).at[0, at:at + vec.shape[0]].set(vec)


SMALL_NAMES = ("norm_pre", "norm_post", "norm_mem", "a_log", "dt_bias", "dn_norm", "gm_norm",
               "spatial_w", "spatial_b", "sinks")


def _whole_weights(s_in, s_mem, s_up, s_out):
    return (_aligned_w_in(s_in), s_mem.reshape(D_MODEL, 2 * BRANCH_W),
            jnp.transpose(s_up, (1, 2, 0, 3)).reshape(N_BRANCH, BRANCH_W, D_MODEL), s_out.reshape(D_MODEL, D_MODEL))


def _grad_slots(d_in_al, d_mem, d_up, d_out):
    return [_slots_of_aligned(d_in_al), d_mem.astype(BF).reshape(N_DEV, 128, 2 * BRANCH_W),
            jnp.transpose(d_up.astype(BF).reshape(N_BRANCH, BRANCH_W, N_DEV, 128), (2, 0, 1, 3)),
            d_out.astype(BF).reshape(N_DEV, 128, D_MODEL)]


def _layer_params(l, small, conv_full, token):
    return dict(
        gpre=small["norm_pre"][l][None] + token, gpost=small["norm_post"][l][None], gmem=small["norm_mem"][l][None],
        cw=conv_full[l], al=_lanes(small["a_log"][l], 4), dt=_lanes(small["dt_bias"][l], 4),
        dnn=small["dn_norm"][l][None], gain=small["gm_norm"][l][None], ws=small["spatial_w"][l],
        bt=jnp.zeros((128, 128), F32).at[:, :GM_GROUPS].set(small["spatial_b"][l].T),
        sinks=_lanes(small["sinks"][l], 0))


def _layer_forward(l, xl, mem, p, weights):
    w_in_al, w_mem, w_up, w_out = weights
    t = "l%d_" % l
    h = prenorm_forward(xl, p["gpre"], t + "prenorm")
    cols = _matmul(h, w_in_al, "nn", F32, t + "w_in")
    mkv = memkv_forward(mem, p["gmem"], w_mem, t + "memkv")
    ya, ss, ts = dn_forward(cols, JB_DN, p["cw"], p["al"], p["dt"], p["dnn"], t + "deltanet")
    yb = gm_forward(cols, JB_GM, p["gain"], p["ws"], p["bt"], t + "gmlp")
    yc = sw_forward(cols, JB_SW, p["sinks"], t + "swa")
    ym = xm_forward(cols, JB_XM, mkv, t + "memattn")
    xn = merge_forward([ya, yb, yc, ym], cols, JB_GATE, xl, w_up, w_out, p["gpost"], t + "merge")
    return xn, dict(p, x=xl, h=h, cols=cols, mkv=mkv, ss=ss, ts=ts, ys=[ya, yb, yc, ym])


def _layer_backward(l, s, mem, weights, dx, token):
    w_in_al, w_mem, w_up, w_out = weights
    t = "l%d_" % l
    cols = s["cols"]
    dcols, dya, dyb, dyc, dym, dwup, dwout, dgpost = merge_backward(
        s["ys"], cols, JB_GATE, w_up, w_out, s["gpost"] + token, dx, t + "merge_bwd")
    dcols, dmkv = xm_backward(cols, JB_XM, s["mkv"], dym, dcols, t + "memattn_bwd")
    dgmem, dwmem = memkv_backward(mem, s["gmem"], w_mem, dmkv, t + "memkv_bwd")
    dcols, dsinks = sw_backward(cols, JB_SW, s["sinks"], dyc, dcols, t + "swa_bwd")
    dcols, dgain, dws, dbt = gm_backward(cols, JB_GM, s["gain"], s["ws"], s["bt"], dyb, dcols, t + "gmlp_bwd")
    dcols, dcw, dal, ddt, ddn = dn_backward(
        cols, JB_DN, s["cw"], s["al"], s["dt"], s["dnn"], s["ss"], s["ts"], dya, dcols, t + "deltanet_bwd")
    dh = _matmul(dcols, w_in_al, "nt", F32, t + "w_in_bwd_x")
    dwin = _matmul(s["h"], dcols, "tn", BF, t + "w_in_bwd_w")
    dx, dgpre = prenorm_backward(s["x"], s["gpre"], dh, dx, t + "prenorm_bwd")
    gsmall = dict(norm_pre=dgpre[0], norm_post=dgpost[0], norm_mem=dgmem[0], a_log=dal[0, 4:8], dt_bias=ddt[0, 4:8],
                  dn_norm=ddn[0], gm_norm=dgain[0], spatial_w=dws, spatial_b=dbt[:, :GM_GROUPS].T,
                  sinks=dsinks[0, :SW_HEADS], conv_w=dcw)
    return dx, gsmall, (dwin, dwmem, dwup, dwout)


def kernel(x, mem, norm_pre, norm_post, norm_mem, w_in, conv_w, a_log, dt_bias, dn_norm, gm_norm, spatial_w, spatial_b, sinks, w_mem_kv, w_up, w_out, loss_target, m_norm_pre, m_norm_post, m_norm_mem, m_w_in, m_conv_w, m_a_log, m_dt_bias, m_dn_norm, m_gm_norm, m_spatial_w, m_spatial_b, m_sinks, m_w_mem_kv, m_w_up, m_w_out, v_norm_pre, v_norm_post, v_norm_mem, v_w_in, v_conv_w, v_a_log, v_dt_bias, v_dn_norm, v_gm_norm, v_spatial_w, v_spatial_b, v_sinks, v_w_mem_kv, v_w_up, v_w_out):
    xi, yi, ci = _my_place()
    my_slot = 4 * xi + 2 * yi + ci
    conv_shard = conv_w.shape[-1]
    x2, mem2, target = x[0], mem[0], loss_target[0]

    w_in_pad = jnp.pad(w_in.astype(BF), ((0, 0), (0, 0), (0, W_IN_SHARD_PAD - W_IN_SHARD)))
    shards = [[w_in_pad[l], w_mem_kv[l].astype(BF), w_up[l].astype(BF), w_out[l].astype(BF)] for l in range(DEPTH)]
    slots0 = _all_gather_slots(shards[0], "gather_weights_l0")
    ag_send, ag_recv, ag_src, ag_land, ag_token = _spread_start(shards[1], False, "gather_weights_l1_start")
    conv_rows = _all_gather_rows_vmem(conv_w.reshape(DEPTH * CONV_W, conv_shard), "gather_conv_w")
    conv_full = jnp.transpose(conv_rows.reshape(N_DEV, DEPTH, CONV_W, conv_shard), (1, 2, 0, 3))
    conv_full = conv_full.reshape(DEPTH, CONV_W, N_DEV * conv_shard)
    small = dict(norm_pre=norm_pre, norm_post=norm_post, norm_mem=norm_mem, a_log=a_log,
                 dt_bias=dt_bias, dn_norm=dn_norm, gm_norm=gm_norm, spatial_w=spatial_w,
                 spatial_b=spatial_b, sinks=sinks)

    weights0 = _whole_weights(*slots0)
    x1, saved0 = _layer_forward(0, x2, mem2, _layer_params(0, small, conv_full, ag_token[0, 0]), weights0)
    weights1 = _whole_weights(*_spread_wait(ag_send, ag_recv, ag_src, ag_land, x1, "gather_weights_l1_wait"))
    x_out, saved1 = _layer_forward(1, x1, mem2, _layer_params(1, small, conv_full, 0.0), weights1)
    loss, dx = loss_head(x_out, target, "loss_head")

    dx, gsmall1, gbig1 = _layer_backward(1, saved1, mem2, weights1, dx, 0.0)
    rs_send, rs_recv, rs_src, rs_land, rs_token = _spread_start(_grad_slots(*gbig1), True, "exchange_grads_l1_start")
    dx, gsmall0, gbig0 = _layer_backward(0, saved0, mem2, weights0, dx, rs_token[0, 0])
    parts1 = _spread_wait(rs_send, rs_recv, rs_src, rs_land, dx, "exchange_grads_l1_wait")
    g_slots = [g.reshape((N_DEV // 2, 2) + g.shape[1:]) for g in _grad_slots(*gbig0)]
    theirs = _exchange_sibling(g_slots, "exchange_sibling_l0")
    chip_sums = [_pair_sum(g, t, "pair_sum_l0_%d" % i) for i, (g, t) in enumerate(zip(g_slots, theirs))]
    parts0 = _exchange_chips(chip_sums, "exchange_chips_l0")

    packed_names = SMALL_NAMES + ("conv_w",)
    gs = {n: jnp.stack([gsmall0[n], gsmall1[n]]) for n in packed_names}
    small_parts = [loss[0, :1]] + [gs[n] for n in packed_names]
    tot = _unpack_small(_all_reduce_vmem(_pack_small(small_parts), "all_reduce_small"), [p.shape for p in small_parts])
    loss_tot = tot[0][0]
    grads = dict(zip(packed_names, tot[1:]))
    grads["conv_w"] = lax.dynamic_slice_in_dim(grads["conv_w"], my_slot * conv_shard, conv_shard, axis=2)

    given = dict(norm_pre=(norm_pre, m_norm_pre, v_norm_pre), norm_post=(norm_post, m_norm_post, v_norm_post),
                 norm_mem=(norm_mem, m_norm_mem, v_norm_mem), a_log=(a_log, m_a_log, v_a_log),
                 dt_bias=(dt_bias, m_dt_bias, v_dt_bias), dn_norm=(dn_norm, m_dn_norm, v_dn_norm),
                 gm_norm=(gm_norm, m_gm_norm, v_gm_norm), spatial_w=(spatial_w, m_spatial_w, v_spatial_w),
                 spatial_b=(spatial_b, m_spatial_b, v_spatial_b), sinks=(sinks, m_sinks, v_sinks),
                 conv_w=(conv_w, m_conv_w, v_conv_w))
    pshapes = [given[n][0].shape for n in packed_names]
    pw, pm, pv = (_pack_small([given[n][i] for n in packed_names]) for i in range(3))
    pd, pnm, pnv = _adamw(pw, _pack_small([grads[n] for n in packed_names]), pm, pv, "adamw_small")
    upd = {n: t for n, t in zip(packed_names, zip(_unpack_small(pd, pshapes), _unpack_small(pnm, pshapes),
                                                  _unpack_small(pnv, pshapes)))}
    for i, (name, (w, m, v)) in enumerate((("w_in", (w_in, m_w_in, v_w_in)),
                                           ("w_mem_kv", (w_mem_kv, m_w_mem_kv, v_w_mem_kv)),
                                           ("w_up", (w_up, m_w_up, v_w_up)), ("w_out", (w_out, m_w_out, v_w_out)))):
        first = _sum_adamw(parts1[i], w, m, v, 1, None, "adamw_%s_l1" % name)
        g, d, nm, nv = _sum_adamw(parts0[i], w, m, v, 0, first, "adamw_%s_l0" % name)
        grads[name], upd[name] = g, (d, nm, nv)

    order = ("norm_pre", "norm_post", "norm_mem", "w_in", "conv_w", "a_log", "dt_bias", "dn_norm",
             "gm_norm", "spatial_w", "spatial_b", "sinks", "w_mem_kv", "w_up", "w_out")
    return (loss_tot, dx[None], *[grads[n] for n in order], *[upd[n][0] for n in order],
            *[upd[n][1] for n in order], *[upd[n][2] for n in order])
```

```python
import functools
import math

import jax
import jax.numpy as jnp
from jax import lax
from jax.experimental import pallas as pl
from jax.experimental.pallas import tpu as pltpu

MESH = pl.DeviceIdType.MESH
N_DEV = 8

D_MODEL = 1024
DEPTH = 2
N_BRANCH = 4
BRANCH_W = 512
DN_HEADS = 4
CONV_W = 4
GM_GROUPS = 4
SW_HEADS = 8
EPS = 1e-6
NEG_INF = -1e30

D_IN = 9992
W_IN_SHARD = D_IN // N_DEV
W_IN_SHARD_PAD = 1280
D_IN_AL = 10752
DN_W, SW_W, GM_W, XM_W = 2560, 1536, 1536, 1024

ADAM_LR = 0.001
ADAM_B1 = 0.9
ADAM_B2 = 0.999
ADAM_EPS = 1e-08
ADAM_WD = 0.01
ADAM_STEP = 10

VMEM_LIMIT = 56 * 1024 * 1024

BF = jnp.bfloat16
F32 = jnp.float32
DN_C = 128
DN_D = 128
HALO = 8
BLK = 128


def _my_place():
    return lax.axis_index("x"), lax.axis_index("y"), lax.axis_index("c")


_ANY = pl.BlockSpec(memory_space=pl.ANY)


def _all_gather_slots(parts, name):
    n = len(parts)

    def body(*refs):
        p_refs, out_refs = refs[:n], refs[n:2 * n]
        send_sems, recv_sems, local_sems = refs[2 * n:]
        x, y, c = _my_place()
        me, sibling = (x, y, c), (x, y, 1 - c)
        chips = [(1 - x, y), (x, 1 - y), (1 - x, 1 - y)]

        def copy(a, k, block, to, src=None):
            px, py, pc = block
            slot = out_refs[a].at[4 * px + 2 * py + pc]
            return pltpu.make_async_remote_copy(
                src_ref=slot if src is None else src, dst_ref=slot,
                send_sem=send_sems.at[7 * a + k], recv_sem=recv_sems.at[7 * a + k],
                device_id=to, device_id_type=MESH)

        mine = [pltpu.make_async_copy(p_refs[a], out_refs[a].at[4 * x + 2 * y + c], local_sems.at[a])
                for a in range(n)]
        for cp in mine:
            cp.start()
        first = []
        for a in range(n):
            first.append(copy(a, 0, me, sibling, src=p_refs[a]))
            first += [copy(a, 1 + j, me, (*chip, c), src=p_refs[a]) for j, chip in enumerate(chips)]
        for cp in first:
            cp.start()
        passed = []
        for j, chip in enumerate(chips):
            for a in range(n):
                copy(a, 1 + j, (*chip, c), me).wait_recv()
                fwd = copy(a, 4 + j, (*chip, c), sibling)
                fwd.start()
                passed.append(fwd)
        for a in range(n):
            copy(a, 0, sibling, me).wait_recv()
            for j, chip in enumerate(chips):
                copy(a, 4 + j, (*chip, 1 - c), me).wait_recv()
        for cp in first + passed:
            cp.wait_send()
        for cp in mine:
            cp.wait()

    return pl.pallas_call(
        body, name=name,
        out_shape=[jax.ShapeDtypeStruct((N_DEV,) + p.shape, p.dtype) for p in parts],
        in_specs=[_ANY] * n, out_specs=[_ANY] * n,
        scratch_shapes=[pltpu.SemaphoreType.DMA((7 * n,)), pltpu.SemaphoreType.DMA((7 * n,)),
                        pltpu.SemaphoreType.DMA((n,))],
    )(*parts)


def _exchange_sibling(parts, name):
    n = len(parts)

    def body(*refs):
        g_refs, out_refs = refs[:n], refs[n:2 * n]
        send_sems, recv_sems = refs[2 * n:]
        x, y, c = _my_place()
        copies = [pltpu.make_async_remote_copy(
            src_ref=g_refs[a].at[:, 1 - c], dst_ref=out_refs[a],
            send_sem=send_sems.at[a], recv_sem=recv_sems.at[a],
            device_id=(x, y, 1 - c), device_id_type=MESH) for a in range(n)]
        for cp in copies:
            cp.start()
        for cp in copies:
            cp.wait()

    return pl.pallas_call(
        body, name=name,
        out_shape=[jax.ShapeDtypeStruct((4,) + g.shape[2:], g.dtype) for g in parts],
        in_specs=[_ANY] * n, out_specs=[_ANY] * n,
        scratch_shapes=[pltpu.SemaphoreType.DMA((n,)), pltpu.SemaphoreType.DMA((n,))],
    )(*parts)


def _exchange_chips(parts, name):
    n = len(parts)

    def body(*refs):
        p_refs, out_refs = refs[:n], refs[n:2 * n]
        send_sems, recv_sems, local_sems = refs[2 * n:]
        x, y, c = _my_place()
        my_chip = 2 * x + y
        own = [pltpu.make_async_copy(p_refs[a].at[my_chip], out_refs[a].at[my_chip], local_sems.at[a])
               for a in range(n)]
        for cp in own:
            cp.start()
        sends, recvs = [], []
        for k in range(1, 4):
            px = 1 - x if (k >> 1) & 1 else x
            py = 1 - y if k & 1 else y
            peer_chip = 2 * px + py
            for a in range(n):
                sem = 3 * a + k - 1
                sends.append(pltpu.make_async_remote_copy(
                    src_ref=p_refs[a].at[peer_chip], dst_ref=out_refs[a].at[my_chip],
                    send_sem=send_sems.at[sem], recv_sem=recv_sems.at[sem],
                    device_id=(px, py, c), device_id_type=MESH))
                recvs.append(pltpu.make_async_remote_copy(
                    src_ref=p_refs[a].at[peer_chip], dst_ref=out_refs[a].at[peer_chip],
                    send_sem=send_sems.at[sem], recv_sem=recv_sems.at[sem],
                    device_id=(px, py, c), device_id_type=MESH))
        for cp in sends:
            cp.start()
        for send, recv in zip(sends, recvs):
            send.wait_send()
            recv.wait_recv()
        for cp in own:
            cp.wait()

    return pl.pallas_call(
        body, name=name,
        out_shape=[jax.ShapeDtypeStruct(p.shape, p.dtype) for p in parts],
        in_specs=[_ANY] * n, out_specs=[_ANY] * n,
        scratch_shapes=[pltpu.SemaphoreType.DMA((3 * n,)), pltpu.SemaphoreType.DMA((3 * n,)),
                        pltpu.SemaphoreType.DMA((n,))],
    )(*parts)


_HBM = pl.BlockSpec(memory_space=pltpu.HBM)
_SEM = pl.BlockSpec(memory_space=pltpu.SEMAPHORE)
_EFFECT = pltpu.SideEffectType.DATAFLOW_SIDE_EFFECTING


def _peer(x, y, c, k):
    return (1 - x if (k >> 2) & 1 else x, 1 - y if (k >> 1) & 1 else y, 1 - c if k & 1 else c)


def _spread_start(srcs, scatter, name):
    n = len(srcs)
    lands = [lax.empty(s.shape if scatter else (N_DEV,) + s.shape, s.dtype) for s in srcs]

    def body(*refs):
        src_refs, land_refs = refs[:n], refs[n:2 * n]
        send_sems, recv_sems = refs[2 * n:2 * n + 2]
        token = refs[-1]
        x, y, c = _my_place()
        my_slot = 4 * x + 2 * y + c
        for a in range(n):
            for k in range(N_DEV):
                px, py, pc = _peer(x, y, c, k)
                src = src_refs[a].at[4 * px + 2 * py + pc] if scatter else src_refs[a]
                pltpu.make_async_remote_copy(
                    src_ref=src, dst_ref=land_refs[a].at[my_slot],
                    send_sem=send_sems.at[a], recv_sem=recv_sems.at[a],
                    device_id=(px, py, pc), device_id_type=MESH).start()
        token[...] = jnp.zeros_like(token)

    out = pl.pallas_call(
        body, name=name,
        out_shape=[pltpu.SemaphoreType.DMA((n,)), pltpu.SemaphoreType.DMA((n,))]
        + [pltpu.HBM(s.shape, s.dtype) for s in srcs] + [pltpu.HBM(l.shape, l.dtype) for l in lands]
        + [jax.ShapeDtypeStruct((8, 128), F32)],
        in_specs=[_HBM] * (2 * n),
        out_specs=[_SEM, _SEM] + [_HBM] * (2 * n) + [pl.BlockSpec(memory_space=pltpu.VMEM)],
        input_output_aliases={i: 2 + i for i in range(2 * n)},
        compiler_params=pltpu.CompilerParams(has_side_effects=_EFFECT),
    )(*[pltpu.with_memory_space_constraint(s, pltpu.HBM) for s in srcs],
      *[pltpu.with_memory_space_constraint(l, pltpu.HBM) for l in lands])
    return out[0], out[1], out[2:2 + n], out[2 + n:2 + 2 * n], out[-1]


def _spread_wait(send_sems, recv_sems, srcs, lands, after, name):
    n = len(srcs)

    def body(*refs):
        land_refs = refs[n:2 * n]
        send_sems, recv_sems = refs[2 * n:2 * n + 2]
        x, y, c = _my_place()
        for a in range(n):
            whole = pltpu.make_async_remote_copy(
                src_ref=land_refs[a], dst_ref=land_refs[a],
                send_sem=send_sems.at[a], recv_sem=recv_sems.at[a],
                device_id=(x, y, c), device_id_type=MESH)
            whole.wait_send()
            whole.wait_recv()

    out = pl.pallas_call(
        body, name=name,
        out_shape=[pltpu.HBM(s.shape, s.dtype) for s in srcs] + [pltpu.HBM(l.shape, l.dtype) for l in lands],
        in_specs=[_HBM] * (2 * n) + [_SEM, _SEM, _ANY],
        out_specs=[_HBM] * (2 * n),
        input_output_aliases={i: i for i in range(2 * n)},
        compiler_params=pltpu.CompilerParams(has_side_effects=_EFFECT),
    )(*srcs, *lands, send_sems, recv_sems, after)
    return out[n:]


def _all_reduce_vmem(v, name):
    def body(v_ref, out_ref, buf, send_sems, recv_sems):
        x, y, c = _my_place()
        peers = [(x, y, 1 - c), (1 - x, y, c), (x, 1 - y, c)]
        out_ref[...] = v_ref[...]
        for step, peer in enumerate(peers):
            cp = pltpu.make_async_remote_copy(
                src_ref=out_ref, dst_ref=buf.at[step],
                send_sem=send_sems.at[step], recv_sem=recv_sems.at[step],
                device_id=peer, device_id_type=MESH)
            cp.start()
            cp.wait()
            out_ref[...] = out_ref[...] + buf[step]

    return pl.pallas_call(
        body, name=name,
        out_shape=jax.ShapeDtypeStruct(v.shape, v.dtype),
        in_specs=[pl.BlockSpec(memory_space=pltpu.VMEM)],
        out_specs=pl.BlockSpec(memory_space=pltpu.VMEM),
        scratch_shapes=[pltpu.VMEM((3,) + v.shape, v.dtype),
                        pltpu.SemaphoreType.DMA((3,)), pltpu.SemaphoreType.DMA((3,))],
    )(v)


def _pick(n, pref):
    if n <= pref:
        return n
    t = pref - pref % 128
    while t > 0 and n % t:
        t -= 128
    return t if t > 0 else n


_DIMS = {"nn": (((1,), (0,)), ((), ())),
         "nt": (((1,), (1,)), ((), ())),
         "tn": (((0,), (0,)), ((), ()))}


def _matmul(a, b, mode, out_dtype, name):
    if mode == "nn":
        (m, k), (_, n) = a.shape, b.shape
    elif mode == "nt":
        (m, k), (n, _) = a.shape, b.shape
    else:
        (k, m), (_, n) = a.shape, b.shape
    tm, tn, tk = _pick(m, 512), _pick(n, 1536), _pick(k, 1024)
    nk = k // tk

    def body(a_ref, b_ref, o_ref, acc_ref):
        kk = pl.program_id(2)

        @pl.when(kk == 0)
        def _():
            acc_ref[...] = jnp.zeros_like(acc_ref)

        acc_ref[...] += lax.dot_general(
            a_ref[...].astype(BF), b_ref[...].astype(BF), _DIMS[mode], preferred_element_type=F32)

        @pl.when(kk == nk - 1)
        def _():
            o_ref[...] = acc_ref[...].astype(o_ref.dtype)

    a_spec = (pl.BlockSpec((tk, tm), lambda i, j, kk: (kk, i)) if mode == "tn"
              else pl.BlockSpec((tm, tk), lambda i, j, kk: (i, kk)))
    b_spec = (pl.BlockSpec((tn, tk), lambda i, j, kk: (j, kk)) if mode == "nt"
              else pl.BlockSpec((tk, tn), lambda i, j, kk: (kk, j)))
    return pl.pallas_call(
        body, name=name,
        out_shape=jax.ShapeDtypeStruct((m, n), out_dtype),
        grid=(m // tm, n // tn, nk),
        in_specs=[a_spec, b_spec],
        out_specs=pl.BlockSpec((tm, tn), lambda i, j, kk: (i, j)),
        scratch_shapes=[pltpu.VMEM((tm, tn), F32)],
        compiler_params=pltpu.CompilerParams(
            dimension_semantics=("parallel", "parallel", "arbitrary"),
            vmem_limit_bytes=VMEM_LIMIT),
    )(a, b)


def _rows2d(t, lead):
    return t.reshape(t.shape[:lead] + (math.prod(t.shape[lead:-1]), t.shape[-1]))


def _pair_sum(g, theirs, name):
    g3, t3 = _rows2d(g, 2), _rows2d(theirs, 1)
    _, r, w = t3.shape
    tr = _pick(r, 512)

    def body(g_ref, t_ref, o_ref):
        c = lax.axis_index("c")
        mine = jnp.where(c == 0, g_ref[0, 0], g_ref[0, 1])
        o_ref[0] = (mine.astype(F32) + t_ref[0].astype(F32)).astype(o_ref.dtype)

    out = pl.pallas_call(
        body, name=name,
        out_shape=jax.ShapeDtypeStruct(t3.shape, t3.dtype),
        grid=(4, r // tr),
        in_specs=[pl.BlockSpec((1, 2, tr, w), lambda q, i: (q, 0, i, 0)),
                  pl.BlockSpec((1, tr, w), lambda q, i: (q, i, 0))],
        out_specs=pl.BlockSpec((1, tr, w), lambda q, i: (q, i, 0)),
        compiler_params=pltpu.CompilerParams(dimension_semantics=("parallel", "parallel")),
    )(g3, t3)
    return out.reshape(theirs.shape)


def _adam_update(w, g, m, v):
    c1 = 1.0 - ADAM_B1 ** ADAM_STEP
    c2 = 1.0 - ADAM_B2 ** ADAM_STEP
    nm = ADAM_B1 * m + (1.0 - ADAM_B1) * g
    nv = ADAM_B2 * v + (1.0 - ADAM_B2) * (g * g)
    delta = -ADAM_LR * ((nm / c1) / (jnp.sqrt(nv / c2) + ADAM_EPS) + ADAM_WD * w)
    return delta, nm, nv


def _sum_adamw(parts, w, m, v, layer, carry, name):
    shape = w.shape
    cols = shape[-1]
    p3 = _rows2d(parts, 1)
    w3, m3, v3 = (_rows2d(t, 1) for t in (w, m, v))
    rows = w3.shape[1]
    tr = _pick(rows, 128)
    n_parts = p3.shape[0]

    def body(p_ref, w_ref, m_ref, v_ref, *rest):
        g_ref, d_ref, nm_ref, nv_ref = rest[-4:]
        g = p_ref[0, :, :cols].astype(F32)
        for q in range(1, n_parts):
            g = g + p_ref[q, :, :cols].astype(F32)
        d, nm, nv = _adam_update(w_ref[0], g, m_ref[0], v_ref[0])
        g_ref[0] = g
        d_ref[0] = d
        nm_ref[0] = nm
        nv_ref[0] = nv

    spec = pl.BlockSpec((1, tr, cols), lambda i: (layer, i, 0))
    extra = [] if carry is None else [_rows2d(t, 1) for t in carry]
    out = pl.pallas_call(
        body, name=name,
        out_shape=[jax.ShapeDtypeStruct(w3.shape, F32)] * 4,
        grid=(rows // tr,),
        in_specs=[pl.BlockSpec((n_parts, tr, p3.shape[-1]), lambda i: (0, i, 0)), spec, spec, spec] + [_ANY] * len(extra),
        out_specs=[spec] * 4,
        input_output_aliases={4 + i: i for i in range(len(extra))},
        compiler_params=pltpu.CompilerParams(dimension_semantics=("parallel",)),
    )(p3, w3, m3, v3, *extra)
    return tuple(t.reshape(shape) for t in out)


def _adamw(w, g, m, v, name):
    rows, cols = w.shape
    tr = _pick(rows, 128)

    def body(w_ref, g_ref, m_ref, v_ref, d_ref, nm_ref, nv_ref):
        d, nm, nv = _adam_update(w_ref[...], g_ref[...], m_ref[...], v_ref[...])
        d_ref[...] = d
        nm_ref[...] = nm
        nv_ref[...] = nv

    spec = pl.BlockSpec((tr, cols), lambda i: (i, 0))
    return pl.pallas_call(
        body, name=name,
        out_shape=[jax.ShapeDtypeStruct((rows, cols), F32)] * 3,
        grid=(rows // tr,),
        in_specs=[spec] * 4, out_specs=[spec] * 3,
        compiler_params=pltpu.CompilerParams(dimension_semantics=("parallel",)),
    )(w, g, m, v)


_VJP = {"nn": (("nt", "gb"), ("tn", "ag")),
        "nt": (("nn", "gb"), ("tn", "ga")),
        "tn": (("nt", "bg"), ("nn", "ag"))}


def _make_dot(cast, precision):
    def raw(mode, a, b):
        return lax.dot_general(cast(a), cast(b), _DIMS[mode], precision=precision,
                               preferred_element_type=F32)

    @functools.partial(jax.custom_vjp, nondiff_argnums=(0,))
    def dot(mode, a, b):
        return raw(mode, a, b)

    def fwd(mode, a, b):
        return raw(mode, a, b), (a, b)

    def bwd(mode, res, g):
        a, b = res
        pick = {"a": a, "b": b, "g": g}
        (ma, ta), (mb, tb) = _VJP[mode]
        return dot(ma, pick[ta[0]], pick[ta[1]]), dot(mb, pick[tb[0]], pick[tb[1]])

    dot.defvjp(fwd, bwd)
    return dot


bdot = _make_dot(lambda t: t.astype(BF), None)
hdot = _make_dot(lambda t: t, lax.Precision.HIGHEST)


def _xdot(mode, a, b):
    return lax.dot_general(a, b, _DIMS[mode], precision=lax.Precision.HIGH, preferred_element_type=F32)


def _unit_lower_inverse(L):
    n = L.shape[0]
    eye = (lax.broadcasted_iota(jnp.int32, (n, n), 0) == lax.broadcasted_iota(jnp.int32, (n, n), 1)).astype(F32)
    t_inv = eye - L
    p = L
    for _ in range(6):
        p = _xdot("nn", p, p)
        t_inv = t_inv + _xdot("nn", t_inv, p)
    return t_inv


@jax.custom_vjp
def _tri_solve(L, rhs, t_inv):
    return _xdot("nn", t_inv, rhs)


def _tri_solve_fwd(L, rhs, t_inv):
    sol = _xdot("nn", t_inv, rhs)
    return sol, (t_inv, sol)


def _tri_solve_bwd(res, dsol):
    t_inv, sol = res
    drhs = _xdot("tn", t_inv, dsol)
    return -_xdot("nt", drhs, sol), drhs, jnp.zeros_like(t_inv)


_tri_solve.defvjp(_tri_solve_fwd, _tri_solve_bwd)


def _sigmoid(x):
    return 1.0 / (1.0 + jnp.exp(-x))


def _softplus(x):
    return jnp.maximum(x, 0.0) + jnp.log(1.0 + jnp.exp(-jnp.abs(x)))


def _dn_chunk(S, xs, ba, z, cw, al, dt, dn, t_saved=None):
    C = DN_C
    pre = xs[0] * cw[0] + xs[1] * cw[1] + xs[2] * cw[2] + xs[3] * cw[3]
    qkv = pre * _sigmoid(pre)
    lane = lax.broadcasted_iota(jnp.int32, (1, 128), 1)
    sub = lax.broadcasted_iota(jnp.int32, (C, 1), 0)
    row_i = lax.broadcasted_iota(jnp.int32, (C, C), 0)
    col_i = lax.broadcasted_iota(jnp.int32, (C, C), 1)
    strict = row_i > col_i
    incl = row_i >= col_i
    g_all = jnp.where((lane >= 4) & (lane < 8), -jnp.exp(al) * _softplus(ba + dt), 0.0)
    gc_all = hdot("nn", incl.astype(F32), g_all)
    gc_all_t = gc_all.T
    beta_all = _sigmoid(ba)
    glast_all = jnp.sum(jnp.where(sub == C - 1, gc_all, 0.0), axis=0, keepdims=True)
    ys, s_new, t_invs = [], [], []
    for h in range(DN_HEADS):
        q = qkv[:, 128 * h:128 * (h + 1)]
        k = qkv[:, 512 + 128 * h:512 + 128 * (h + 1)]
        v = qkv[:, 1024 + 128 * h:1024 + 128 * (h + 1)]
        q = q * lax.rsqrt(jnp.sum(q * q, axis=1, keepdims=True) + EPS) * (DN_D ** -0.5)
        k = k * lax.rsqrt(jnp.sum(k * k, axis=1, keepdims=True) + EPS)
        beta = jnp.sum(jnp.where(lane == h, beta_all, 0.0), axis=1, keepdims=True)
        gc = jnp.sum(jnp.where(lane == 4 + h, gc_all, 0.0), axis=1, keepdims=True)
        gc_row = jnp.sum(jnp.where(sub == 4 + h, gc_all_t, 0.0), axis=0, keepdims=True)
        g_last = jnp.sum(jnp.where(lane == 4 + h, glast_all, 0.0), axis=1, keepdims=True)
        diff = gc - gc_row
        kb = k * beta
        L = jnp.where(strict, bdot("nt", kb, k) * jnp.exp(jnp.where(strict, diff, 0.0)), 0.0)
        t_inv = _unit_lower_inverse(L) if t_saved is None else t_saved[h]
        t_invs.append(t_inv)
        sol = _tri_solve(L, jnp.concatenate([v * beta, kb * jnp.exp(gc)], axis=1), t_inv)
        u, w = sol[:, :DN_D], sol[:, DN_D:]
        a_qk = jnp.where(incl, bdot("nt", q, k) * jnp.exp(jnp.where(incl, diff, 0.0)), 0.0)
        qg = q * jnp.exp(gc)
        kd = k * jnp.exp(g_last - gc)
        v_new = u - bdot("nn", w, S[h])
        o = bdot("nn", qg, S[h]) + bdot("nn", a_qk, v_new)
        s_new.append(S[h] * jnp.exp(g_last) + bdot("tn", kd, v_new))
        o = o * lax.rsqrt(jnp.mean(o * o, axis=1, keepdims=True) + EPS) * dn
        zh = z[:, 128 * h:128 * (h + 1)]
        ys.append(o * (zh * _sigmoid(zh)))
    return jnp.concatenate(ys, axis=1), tuple(s_new), tuple(t_invs)


def _load_shifted(xbuf, x_ref, halo_ref, first):
    xbuf[0:HALO, :] = jnp.where(first, 0.0, halo_ref[:, 0:1536])
    xbuf[HALO:HALO + DN_C, :] = x_ref[:, 0:1536]
    return [xbuf[HALO - 3 + k:HALO - 3 + k + DN_C, :] for k in range(4)]


def dn_forward(cols, jblk, cw, al, dt, dn, name):
    T = cols.shape[0]
    n = T // DN_C

    def body(x_ref, halo_ref, cw_ref, al_ref, dt_ref, dn_ref, y_ref, ss_ref, ts_ref, s_scr, xbuf):
        i = pl.program_id(0)

        @pl.when(i == 0)
        def _():
            s_scr[...] = jnp.zeros_like(s_scr)

        xs = _load_shifted(xbuf, x_ref, halo_ref, i == 0)
        ss_ref[0] = s_scr[...]
        S = [s_scr[h] for h in range(DN_HEADS)]
        cws = [cw_ref[k:k + 1, :] for k in range(4)]
        y, s_new, t_invs = _dn_chunk(S, xs, x_ref[:, 2048:2176], x_ref[:, 1536:2048], cws,
                                     al_ref[...], dt_ref[...], dn_ref[...])
        y_ref[...] = y
        for h in range(DN_HEADS):
            s_scr[h] = s_new[h]
            ts_ref[0, h] = t_invs[h]

    per = DN_C // HALO
    full = lambda shape: pl.BlockSpec(shape, lambda i: (0,) * len(shape))
    return pl.pallas_call(
        body, name=name,
        out_shape=[jax.ShapeDtypeStruct((T, 512), F32),
                   jax.ShapeDtypeStruct((n, DN_HEADS, DN_D, DN_D), F32),
                   jax.ShapeDtypeStruct((n, DN_HEADS, DN_D, DN_D), F32)],
        grid=(n,),
        in_specs=[pl.BlockSpec((DN_C, DN_W), lambda i: (i, jblk)),
                  pl.BlockSpec((HALO, DN_W), lambda i: (jnp.maximum(i * per - 1, 0), jblk)),
                  full((4, 1536)), full((1, 128)), full((1, 128)), full((1, 128))],
        out_specs=[pl.BlockSpec((DN_C, 512), lambda i: (i, 0)),
                   pl.BlockSpec((1, DN_HEADS, DN_D, DN_D), lambda i: (i, 0, 0, 0)),
                   pl.BlockSpec((1, DN_HEADS, DN_D, DN_D), lambda i: (i, 0, 0, 0))],
        scratch_shapes=[pltpu.VMEM((DN_HEADS, DN_D, DN_D), F32), pltpu.VMEM((HALO + DN_C, 1536), F32)],
        compiler_params=pltpu.CompilerParams(dimension_semantics=("arbitrary",)),
    )(cols, cols, cw, al, dt, dn)


def dn_backward(cols, jblk, cw, al, dt, dn, ss, ts, dy, dcols, name):
    T = cols.shape[0]
    n = T // DN_C

    def body(x_ref, halo_ref, cw_ref, al_ref, dt_ref, dn_ref, ss_ref, ts_ref, dy_ref, dcols_in,
             dx_ref, dcw_ref, dal_ref, ddt_ref, ddn_ref, ds_scr, xbuf, dbuf, carry):
        i = pl.program_id(0)

        @pl.when(i == 0)
        def _():
            ds_scr[...] = jnp.zeros_like(ds_scr)
            carry[...] = jnp.zeros_like(carry)
            dcw_ref[...] = jnp.zeros_like(dcw_ref)
            dal_ref[...] = jnp.zeros_like(dal_ref)
            ddt_ref[...] = jnp.zeros_like(ddt_ref)
            ddn_ref[...] = jnp.zeros_like(ddn_ref)

        xs = _load_shifted(xbuf, x_ref, halo_ref, i == n - 1)
        S = [ss_ref[0, h] for h in range(DN_HEADS)]
        cws = [cw_ref[k:k + 1, :] for k in range(4)]

        t_saved = [ts_ref[0, h] for h in range(DN_HEADS)]

        def f(S, xs, ba, z, cws, al, dt, dn):
            return _dn_chunk(S, xs, ba, z, cws, al, dt, dn, t_saved)[:2]

        _, vjp = jax.vjp(f, S, xs, x_ref[:, 2048:2176], x_ref[:, 1536:2048], cws, al_ref[...], dt_ref[...], dn_ref[...])
        dS, dxs, dba, dz, dcws, dal, ddt, ddn = vjp((dy_ref[...], tuple(ds_scr[h] for h in range(DN_HEADS))))
        for h in range(DN_HEADS):
            ds_scr[h] = dS[h]
        dbuf[...] = jnp.zeros_like(dbuf)
        for k in range(4):
            lo = HALO - 3 + k
            dbuf[lo:lo + DN_C, :] += dxs[k]
        dbuf[DN_C:DN_C + HALO, :] += carry[...]
        dx_ref[...] = jnp.concatenate([dbuf[HALO:HALO + DN_C, :], dz, dba,
                                       jnp.zeros((DN_C, DN_W - 2176), F32)], axis=1)
        carry[...] = dbuf[0:HALO, :]
        for k in range(4):
            dcw_ref[k:k + 1, :] += dcws[k]
        dal_ref[...] += dal
        ddt_ref[...] += ddt
        ddn_ref[...] += ddn

    per = DN_C // HALO
    rev = lambda i: n - 1 - i
    full = lambda shape: pl.BlockSpec(shape, lambda i: (0,) * len(shape))
    return pl.pallas_call(
        body, name=name,
        out_shape=[jax.ShapeDtypeStruct(dcols.shape, F32), jax.ShapeDtypeStruct((4, 1536), F32),
                   jax.ShapeDtypeStruct((1, 128), F32), jax.ShapeDtypeStruct((1, 128), F32),
                   jax.ShapeDtypeStruct((1, 128), F32)],
        grid=(n,),
        in_specs=[pl.BlockSpec((DN_C, DN_W), lambda i: (rev(i), jblk)),
                  pl.BlockSpec((HALO, DN_W), lambda i: (jnp.maximum(rev(i) * per - 1, 0), jblk)),
                  full((4, 1536)), full((1, 128)), full((1, 128)), full((1, 128)),
                  pl.BlockSpec((1, DN_HEADS, DN_D, DN_D), lambda i: (rev(i), 0, 0, 0)),
                  pl.BlockSpec((1, DN_HEADS, DN_D, DN_D), lambda i: (rev(i), 0, 0, 0)),
                  pl.BlockSpec((DN_C, 512), lambda i: (rev(i), 0)), _ANY],
        out_specs=[pl.BlockSpec((DN_C, DN_W), lambda i: (rev(i), jblk)),
                   full((4, 1536)), full((1, 128)), full((1, 128)), full((1, 128))],
        scratch_shapes=[pltpu.VMEM((DN_HEADS, DN_D, DN_D), F32), pltpu.VMEM((HALO + DN_C, 1536), F32),
                        pltpu.VMEM((HALO + DN_C, 1536), F32), pltpu.VMEM((HALO, 1536), F32)],
        input_output_aliases={9: 0},
        compiler_params=pltpu.CompilerParams(dimension_semantics=("arbitrary",)),
    )(cols, cols, cw, al, dt, dn, ss, ts, dy, dcols)


def _full(shape):
    return pl.BlockSpec(shape, lambda i: (0,) * len(shape))


def _silu(x):
    return x * _sigmoid(x)


def _gelu(x):
    return 0.5 * x * (1.0 + jnp.tanh(0.7978845608028654 * (x + 0.044715 * (x * x * x))))


def _lane_col(mat, idx):
    lane = lax.broadcasted_iota(jnp.int32, (1, mat.shape[1]), 1)
    return jnp.sum(jnp.where(lane == idx, mat, 0.0), axis=1, keepdims=True)


def _gm_chunk(uv, z, gain, ws, bt):
    g = _gelu(uv)
    u, v = g[:, :512], g[:, 512:]
    v = v * lax.rsqrt(jnp.mean(v * v, axis=1, keepdims=True) + EPS) * gain
    row_i = lax.broadcasted_iota(jnp.int32, (BLK, BLK), 0)
    col_i = lax.broadcasted_iota(jnp.int32, (BLK, BLK), 1)
    causal = row_i >= col_i
    ss = []
    for grp in range(4):
        wg = jnp.where(causal, ws[grp], 0.0)
        ss.append(bdot("nn", wg, v[:, BLK * grp:BLK * (grp + 1)]) + _lane_col(bt, grp))
    return u * jnp.concatenate(ss, axis=1) * _silu(z)


def gm_forward(cols, jblk, gain, ws, bt, name):
    T = cols.shape[0]

    def body(x_ref, gain_ref, ws_ref, bt_ref, y_ref):
        y_ref[...] = _gm_chunk(x_ref[:, 0:1024], x_ref[:, 1024:1536], gain_ref[...],
                               [ws_ref[g] for g in range(4)], bt_ref[...])

    return pl.pallas_call(
        body, name=name, out_shape=jax.ShapeDtypeStruct((T, 512), F32), grid=(T // BLK,),
        in_specs=[pl.BlockSpec((BLK, GM_W), lambda i: (i, jblk)),
                  _full((1, 512)), _full((4, BLK, BLK)), _full((BLK, BLK))],
        out_specs=pl.BlockSpec((BLK, 512), lambda i: (i, 0)),
        compiler_params=pltpu.CompilerParams(dimension_semantics=("parallel",)),
    )(cols, gain, ws, bt)


def gm_backward(cols, jblk, gain, ws, bt, dy, dcols, name):
    T = cols.shape[0]

    def body(x_ref, gain_ref, ws_ref, bt_ref, dy_ref, dcols_in, dx_ref, dgain_ref, dws_ref, dbt_ref):
        @pl.when(pl.program_id(0) == 0)
        def _():
            dgain_ref[...] = jnp.zeros_like(dgain_ref)
            dws_ref[...] = jnp.zeros_like(dws_ref)
            dbt_ref[...] = jnp.zeros_like(dbt_ref)

        _, vjp = jax.vjp(_gm_chunk, x_ref[:, 0:1024], x_ref[:, 1024:1536], gain_ref[...],
                         [ws_ref[g] for g in range(4)], bt_ref[...])
        duv, dz, dgain, dws, dbt = vjp(dy_ref[...])
        dx_ref[...] = jnp.concatenate([duv, dz], axis=1)
        dgain_ref[...] += dgain
        for g in range(4):
            dws_ref[g] += dws[g]
        dbt_ref[...] += dbt

    return pl.pallas_call(
        body, name=name,
        out_shape=[jax.ShapeDtypeStruct(dcols.shape, F32), jax.ShapeDtypeStruct((1, 512), F32),
                   jax.ShapeDtypeStruct((4, BLK, BLK), F32), jax.ShapeDtypeStruct((BLK, BLK), F32)],
        grid=(T // BLK,),
        in_specs=[pl.BlockSpec((BLK, GM_W), lambda i: (i, jblk)),
                  _full((1, 512)), _full((4, BLK, BLK)), _full((BLK, BLK)),
                  pl.BlockSpec((BLK, 512), lambda i: (i, 0)), _ANY],
        out_specs=[pl.BlockSpec((BLK, GM_W), lambda i: (i, jblk)),
                   _full((1, 512)), _full((4, BLK, BLK)), _full((BLK, BLK))],
        input_output_aliases={5: 0},
        compiler_params=pltpu.CompilerParams(dimension_semantics=("arbitrary",)),
    )(cols, gain, ws, bt, dy, dcols)


def _sw_block(first, q, kp, kc, vp, vc, z, sinks):
    P = BLK
    lane = lax.broadcasted_iota(jnp.int32, (1, 128), 1)
    r = lax.broadcasted_iota(jnp.int32, (128, 128), 0)
    c = lax.broadcasted_iota(jnp.int32, (128, 128), 1)
    swap = (c == (r + 64) % 128).astype(F32)
    k2 = jnp.concatenate([kp, kc], axis=0)
    v2 = jnp.concatenate([vp, vc], axis=0)
    k2s = bdot("nn", k2, swap)
    v2s = bdot("nn", v2, swap)
    qi = lax.broadcasted_iota(jnp.int32, (P, 2 * P), 0)
    kj = lax.broadcasted_iota(jnp.int32, (P, 2 * P), 1)
    dist = qi + P - kj
    valid = (dist >= 0) & (dist < P) & ((kj >= P) | jnp.logical_not(first))
    outs = []
    for j in range(4):
        acc = jnp.zeros((P, 128), F32)
        for half in range(2):
            h = 2 * j + half
            kv = h // 4
            in_half = (lane >= 64 * half) & (lane < 64 * half + 64)
            qh = jnp.where(in_half, q[:, 128 * j:128 * (j + 1)], 0.0)
            same = (half == kv)
            s = bdot("nt", qh, k2 if same else k2s) * (64 ** -0.5)
            s = jnp.where(valid, s, NEG_INF)
            sink = _lane_col(sinks, h)
            m = lax.stop_gradient(jnp.maximum(jnp.max(s, axis=1, keepdims=True), sink))
            e = jnp.exp(s - m)
            p = e / (jnp.sum(e, axis=1, keepdims=True) + jnp.exp(sink - m))
            o = bdot("nn", p, v2 if same else v2s)
            acc = acc + jnp.where(in_half, o, 0.0)
        outs.append(acc)
    return jnp.concatenate(outs, axis=1) * _silu(z)


def _sw_specs(jblk, idx):
    prev = lambda i: jnp.maximum(idx(i) - 1, 0)
    jk = (jblk * SW_W + 1024) // 128
    return [pl.BlockSpec((BLK, SW_W), lambda i: (idx(i), jblk)),
            pl.BlockSpec((BLK, 128), lambda i: (prev(i), jk)),
            pl.BlockSpec((BLK, 128), lambda i: (prev(i), jk + 1)), _full((1, 128))]


def sw_forward(cols, jblk, sinks, name):
    T = cols.shape[0]

    def body(x_ref, kp_ref, vp_ref, s_ref, y_ref):
        y_ref[...] = _sw_block(pl.program_id(0) == 0, x_ref[:, 0:512], kp_ref[...], x_ref[:, 1024:1152],
                               vp_ref[...], x_ref[:, 1152:1280], x_ref[:, 512:1024], s_ref[...])

    return pl.pallas_call(
        body, name=name, out_shape=jax.ShapeDtypeStruct((T, 512), F32), grid=(T // BLK,),
        in_specs=_sw_specs(jblk, lambda i: i),
        out_specs=pl.BlockSpec((BLK, 512), lambda i: (i, 0)),
        compiler_params=pltpu.CompilerParams(dimension_semantics=("parallel",)),
    )(cols, cols, cols, sinks)


def sw_backward(cols, jblk, sinks, dy, dcols, name):
    T = cols.shape[0]
    n = T // BLK
    rev = lambda i: n - 1 - i

    def body(x_ref, kp_ref, vp_ref, s_ref, dy_ref, dcols_in, dx_ref, ds_ref, kcarry, vcarry):
        i = pl.program_id(0)

        @pl.when(i == 0)
        def _():
            kcarry[...] = jnp.zeros_like(kcarry)
            vcarry[...] = jnp.zeros_like(vcarry)
            ds_ref[...] = jnp.zeros_like(ds_ref)

        f = functools.partial(_sw_block, i == n - 1)
        _, vjp = jax.vjp(f, x_ref[:, 0:512], kp_ref[...], x_ref[:, 1024:1152], vp_ref[...], x_ref[:, 1152:1280],
                         x_ref[:, 512:1024], s_ref[...])
        dq, dkp, dkc, dvp, dvc, dz, dsk = vjp(dy_ref[...])
        dx_ref[...] = jnp.concatenate([dq, dz, dkc + kcarry[...], dvc + vcarry[...],
                                       jnp.zeros((BLK, SW_W - 1280), F32)], axis=1)
        kcarry[...] = dkp
        vcarry[...] = dvp
        ds_ref[...] += dsk

    return pl.pallas_call(
        body, name=name,
        out_shape=[jax.ShapeDtypeStruct(dcols.shape, F32), jax.ShapeDtypeStruct((1, 128), F32)],
        grid=(n,),
        in_specs=_sw_specs(jblk, rev) + [pl.BlockSpec((BLK, 512), lambda i: (rev(i), 0)), _ANY],
        out_specs=[pl.BlockSpec((BLK, SW_W), lambda i: (rev(i), jblk)), _full((1, 128))],
        scratch_shapes=[pltpu.VMEM((BLK, 128), F32), pltpu.VMEM((BLK, 128), F32)],
        input_output_aliases={5: 0},
        compiler_params=pltpu.CompilerParams(dimension_semantics=("arbitrary",)),
    )(cols, cols, cols, sinks, dy, dcols)


XM_TQ = 256


def _xm_block(q, z, mkv):
    outs = []
    for h in range(4):
        s = bdot("nt", q[:, 128 * h:128 * (h + 1)], mkv[:, 128 * h:128 * (h + 1)]) * (128 ** -0.5)
        m = lax.stop_gradient(jnp.max(s, axis=1, keepdims=True))
        e = jnp.exp(s - m)
        p = e / jnp.sum(e, axis=1, keepdims=True)
        outs.append(bdot("nn", p, mkv[:, 512 + 128 * h:512 + 128 * (h + 1)]))
    return jnp.concatenate(outs, axis=1) * _silu(z)


def xm_forward(cols, jblk, mkv, name):
    T = cols.shape[0]

    def body(x_ref, m_ref, y_ref):
        y_ref[...] = _xm_block(x_ref[:, 0:512], x_ref[:, 512:1024], m_ref[...])

    return pl.pallas_call(
        body, name=name, out_shape=jax.ShapeDtypeStruct((T, 512), F32), grid=(T // XM_TQ,),
        in_specs=[pl.BlockSpec((XM_TQ, XM_W), lambda i: (i, jblk)), _full(mkv.shape)],
        out_specs=pl.BlockSpec((XM_TQ, 512), lambda i: (i, 0)),
        compiler_params=pltpu.CompilerParams(dimension_semantics=("parallel",)),
    )(cols, mkv)


def xm_backward(cols, jblk, mkv, dy, dcols, name):
    T = cols.shape[0]

    def body(x_ref, m_ref, dy_ref, dcols_in, dx_ref, dm_ref):
        @pl.when(pl.program_id(0) == 0)
        def _():
            dm_ref[...] = jnp.zeros_like(dm_ref)

        _, vjp = jax.vjp(_xm_block, x_ref[:, 0:512], x_ref[:, 512:1024], m_ref[...])
        dq, dz, dm = vjp(dy_ref[...])
        dx_ref[...] = jnp.concatenate([dq, dz], axis=1)
        dm_ref[...] += dm

    return pl.pallas_call(
        body, name=name,
        out_shape=[jax.ShapeDtypeStruct(dcols.shape, F32), jax.ShapeDtypeStruct(mkv.shape, F32)],
        grid=(T // XM_TQ,),
        in_specs=[pl.BlockSpec((XM_TQ, XM_W), lambda i: (i, jblk)), _full(mkv.shape),
                  pl.BlockSpec((XM_TQ, 512), lambda i: (i, 0)), _ANY],
        out_specs=[pl.BlockSpec((XM_TQ, XM_W), lambda i: (i, jblk)), _full(mkv.shape)],
        input_output_aliases={3: 0},
        compiler_params=pltpu.CompilerParams(dimension_semantics=("arbitrary",)),
    )(cols, mkv, dy, dcols)


def _rms(x, gain):
    return x * lax.rsqrt(jnp.mean(x * x, axis=1, keepdims=True) + EPS) * gain


def memkv_forward(mem, gain, w, name):
    def body(m_ref, g_ref, w_ref, o_ref):
        o_ref[...] = bdot("nn", _rms(m_ref[...], g_ref[...]), w_ref[...])

    return pl.pallas_call(body, name=name, out_shape=jax.ShapeDtypeStruct(mem.shape, F32),
                          compiler_params=pltpu.CompilerParams(vmem_limit_bytes=VMEM_LIMIT))(mem, gain, w)


def memkv_backward(mem, gain, w, dkv, name):
    def body(m_ref, g_ref, w_ref, d_ref, dg_ref, dw_ref):
        mem_v = m_ref[...]
        _, vjp = jax.vjp(lambda g, ww: bdot("nn", _rms(mem_v, g), ww), g_ref[...], w_ref[...].astype(F32))
        dg, dw = vjp(d_ref[...])
        dg_ref[...] = dg
        dw_ref[...] = dw

    return pl.pallas_call(body, name=name,
                          out_shape=[jax.ShapeDtypeStruct(gain.shape, F32), jax.ShapeDtypeStruct(w.shape, F32)],
                          compiler_params=pltpu.CompilerParams(vmem_limit_bytes=VMEM_LIMIT))(mem, gain, w, dkv)


MG_TB = 128


def _merge_block(ys, gl, wup, wout, gpost):
    merged = None
    for n in range(4):
        t = _sigmoid(gl[:, 1024 * n:1024 * (n + 1)]) * bdot("nn", ys[n], wup[n])
        merged = t if merged is None else merged + t
    out = bdot("nn", merged, wout)
    return _rms(out, gpost)


def merge_forward(ys, cols, jgate, x, wup, wout, gpost, name):
    T = x.shape[0]
    TB = 256

    def body(ya, yb, yc, ym, gl_ref, x_ref, wup_ref, wout_ref, gp_ref, o_ref):
        upd = _merge_block([ya[...], yb[...], yc[...], ym[...]], gl_ref[...],
                           [wup_ref[n] for n in range(4)], wout_ref[...], gp_ref[...])
        o_ref[...] = x_ref[...] + upd

    yspec = pl.BlockSpec((TB, 512), lambda i: (i, 0))
    return pl.pallas_call(
        body, name=name, out_shape=jax.ShapeDtypeStruct((T, 1024), F32), grid=(T // TB,),
        in_specs=[yspec] * 4 + [pl.BlockSpec((TB, 4096), lambda i: (i, jgate)),
                                pl.BlockSpec((TB, 1024), lambda i: (i, 0)),
                                _full(wup.shape), _full(wout.shape), _full((1, 1024))],
        out_specs=pl.BlockSpec((TB, 1024), lambda i: (i, 0)),
        compiler_params=pltpu.CompilerParams(dimension_semantics=("parallel",), vmem_limit_bytes=VMEM_LIMIT),
    )(*ys, cols, x, wup, wout, gpost)


def merge_backward(ys, cols, jgate, wup, wout, gpost, dx, name):
    T = dx.shape[0]
    TB = MG_TB

    def body(ya, yb, yc, ym, gl_ref, wup_ref, wout_ref, gp_ref, dx_ref,
             dgl_ref, dya, dyb, dyc, dym, dwup_ref, dwout_ref, dgp_ref):
        @pl.when(pl.program_id(0) == 0)
        def _():
            dwup_ref[...] = jnp.zeros_like(dwup_ref)
            dwout_ref[...] = jnp.zeros_like(dwout_ref)
            dgp_ref[...] = jnp.zeros_like(dgp_ref)

        _, vjp = jax.vjp(_merge_block, [ya[...], yb[...], yc[...], ym[...]], gl_ref[...],
                         [wup_ref[n].astype(F32) for n in range(4)], wout_ref[...].astype(F32), gp_ref[...])
        dys, dgl, dwup, dwout, dgp = vjp(dx_ref[...])
        for ref, val in zip((dya, dyb, dyc, dym), dys):
            ref[...] = val
        dgl_ref[...] = dgl
        for n in range(4):
            dwup_ref[n] += dwup[n]
        dwout_ref[...] += dwout
        dgp_ref[...] += dgp

    yspec = pl.BlockSpec((TB, 512), lambda i: (i, 0))
    return pl.pallas_call(
        body, name=name,
        out_shape=[jax.ShapeDtypeStruct(cols.shape, F32)] + [jax.ShapeDtypeStruct((T, 512), F32)] * 4 + [
            jax.ShapeDtypeStruct(wup.shape, F32), jax.ShapeDtypeStruct(wout.shape, F32),
            jax.ShapeDtypeStruct((1, 1024), F32)],
        grid=(T // TB,),
        in_specs=[yspec] * 4 + [pl.BlockSpec((TB, 4096), lambda i: (i, jgate)),
                                _full(wup.shape), _full(wout.shape), _full((1, 1024)),
                                pl.BlockSpec((TB, 1024), lambda i: (i, 0))],
        out_specs=[pl.BlockSpec((TB, 4096), lambda i: (i, jgate))] + [yspec] * 4 + [
            _full(wup.shape), _full(wout.shape), _full((1, 1024))],
        compiler_params=pltpu.CompilerParams(dimension_semantics=("arbitrary",), vmem_limit_bytes=VMEM_LIMIT),
    )(*ys, cols, wup, wout, gpost, dx)


NB = 256


def prenorm_forward(x, gain, name):
    T = x.shape[0]

    def body(x_ref, g_ref, o_ref):
        o_ref[...] = _rms(x_ref[...], g_ref[...]).astype(BF)

    return pl.pallas_call(
        body, name=name, out_shape=jax.ShapeDtypeStruct(x.shape, BF), grid=(T // NB,),
        in_specs=[pl.BlockSpec((NB, 1024), lambda i: (i, 0)), _full((1, 1024))],
        out_specs=pl.BlockSpec((NB, 1024), lambda i: (i, 0)),
        compiler_params=pltpu.CompilerParams(dimension_semantics=("parallel",)),
    )(x, gain)


def prenorm_backward(x, gain, dh, dres, name):
    T = x.shape[0]

    def body(x_ref, g_ref, dh_ref, dr_ref, dx_ref, dg_ref):
        @pl.when(pl.program_id(0) == 0)
        def _():
            dg_ref[...] = jnp.zeros_like(dg_ref)

        _, vjp = jax.vjp(_rms, x_ref[...], g_ref[...])
        dxn, dg = vjp(dh_ref[...])
        dx_ref[...] = dr_ref[...] + dxn
        dg_ref[...] += dg

    spec = pl.BlockSpec((NB, 1024), lambda i: (i, 0))
    return pl.pallas_call(
        body, name=name,
        out_shape=[jax.ShapeDtypeStruct(x.shape, F32), jax.ShapeDtypeStruct((1, 1024), F32)], grid=(T // NB,),
        in_specs=[spec, _full((1, 1024)), spec, spec], out_specs=[spec, _full((1, 1024))],
        compiler_params=pltpu.CompilerParams(dimension_semantics=("arbitrary",)),
    )(x, gain, dh, dres)


def loss_head(y, target, name):
    T, D = y.shape

    def body(y_ref, t_ref, l_ref, d_ref):
        @pl.when(pl.program_id(0) == 0)
        def _():
            l_ref[...] = jnp.zeros_like(l_ref)

        err = y_ref[...] - t_ref[...]
        d_ref[...] = err * (1.0 / D)
        l_ref[...] += jnp.full(l_ref.shape, 0.5 * jnp.sum(jnp.mean(err * err, axis=1, keepdims=True)), F32)

    spec = pl.BlockSpec((NB, D), lambda i: (i, 0))
    return pl.pallas_call(
        body, name=name,
        out_shape=[jax.ShapeDtypeStruct((1, 128), F32), jax.ShapeDtypeStruct(y.shape, F32)], grid=(T // NB,),
        in_specs=[spec, spec], out_specs=[_full((1, 128)), spec],
        compiler_params=pltpu.CompilerParams(dimension_semantics=("arbitrary",)),
    )(y, target)


JB_GATE, JB_XM, JB_DN, JB_SW, JB_GM = 0, 4, 2, 5, 6
_ALIGNED_PIECES = ((5896, 4096), (4872, 512), (5384, 512), (0, 2048), (2048, 8), 504, (3592, 512), (4360, 512),
                   (4104, 128), (4232, 128), 256, (2056, 1024), (3080, 512))
_NATURAL_FROM_ALIGNED = ((5120, 2048), (7168, 8), (9216, 1024), (10240, 512), (7680, 512), (8704, 128), (8832, 128),
                         (8192, 512), (4096, 512), (4608, 512), (0, 4096))


def _natural_range(slots, start, width):
    out = []
    while width > 0:
        j, i = divmod(start, W_IN_SHARD)
        take = min(width, W_IN_SHARD - i)
        out.append(slots[j, :, i:i + take])
        start, width = start + take, width - take
    return out


def _aligned_w_in(slots):
    parts = []
    for piece in _ALIGNED_PIECES:
        if isinstance(piece, int):
            parts.append(jnp.zeros(slots.shape[1:2] + (piece,), slots.dtype))
        else:
            parts += _natural_range(slots, *piece)
    return jnp.concatenate(parts, axis=-1)


def _slots_of_aligned(d_al):
    slots = []
    for s in range(N_DEV):
        lo, hi = s * W_IN_SHARD, (s + 1) * W_IN_SHARD
        parts, nat = [], 0
        for a_start, width in _NATURAL_FROM_ALIGNED:
            b, e = max(lo, nat), min(hi, nat + width)
            if b < e:
                parts.append(d_al[..., a_start + b - nat:a_start + e - nat])
            nat += width
        parts.append(jnp.zeros(d_al.shape[:1] + (W_IN_SHARD_PAD - W_IN_SHARD,), d_al.dtype))
        slots.append(jnp.concatenate(parts, axis=-1))
    return jnp.stack(slots)


SMALL_VEC_W = 1024


def _pack_small(parts):
    rows = []
    for p in parts:
        flat = p.reshape(-1).astype(F32)
        r = -(-flat.shape[0] // SMALL_VEC_W)
        rows.append(jnp.pad(flat, (0, r * SMALL_VEC_W - flat.shape[0])).reshape(r, SMALL_VEC_W))
    vec = jnp.concatenate(rows, axis=0)
    return jnp.pad(vec, ((0, -vec.shape[0] % 8), (0, 0)))


def _unpack_small(vec, shapes):
    out, off = [], 0
    for s in shapes:
        n = math.prod(s)
        r = -(-n // SMALL_VEC_W)
        out.append(vec[off:off + r].reshape(-1)[:n].reshape(s))
        off += r
    return out


def _lanes(vec, at):
    return jnp.zeros((1, 128), F32).at[0, at:at + vec.shape[0]].set(vec)


SMALL_NAMES = ("norm_pre", "norm_post", "norm_mem", "a_log", "dt_bias", "dn_norm", "gm_norm",
               "spatial_w", "spatial_b", "sinks")


def _whole_weights(s_in, s_mem, s_up, s_out):
    return (_aligned_w_in(s_in), s_mem.reshape(D_MODEL, 2 * BRANCH_W),
            jnp.transpose(s_up, (1, 2, 0, 3)).reshape(N_BRANCH, BRANCH_W, D_MODEL), s_out.reshape(D_MODEL, D_MODEL))


def _grad_slots(d_in_al, d_mem, d_up, d_out):
    return [_slots_of_aligned(d_in_al), d_mem.astype(BF).reshape(N_DEV, 128, 2 * BRANCH_W),
            jnp.transpose(d_up.astype(BF).reshape(N_BRANCH, BRANCH_W, N_DEV, 128), (2, 0, 1, 3)),
            d_out.astype(BF).reshape(N_DEV, 128, D_MODEL)]


def _layer_params(l, small, conv_full, token):
    return dict(
        gpre=small["norm_pre"][l][None] + token, gpost=small["norm_post"][l][None], gmem=small["norm_mem"][l][None],
        cw=conv_full[l], al=_lanes(small["a_log"][l], 4), dt=_lanes(small["dt_bias"][l], 4),
        dnn=small["dn_norm"][l][None], gain=small["gm_norm"][l][None], ws=small["spatial_w"][l],
        bt=jnp.zeros((128, 128), F32).at[:, :GM_GROUPS].set(small["spatial_b"][l].T),
        sinks=_lanes(small["sinks"][l], 0))


def _layer_forward(l, xl, mem, p, weights):
    w_in_al, w_mem, w_up, w_out = weights
    t = "l%d_" % l
    h = prenorm_forward(xl, p["gpre"], t + "prenorm")
    cols = _matmul(h, w_in_al, "nn", F32, t + "w_in")
    mkv = memkv_forward(mem, p["gmem"], w_mem, t + "memkv")
    ya, ss, ts = dn_forward(cols, JB_DN, p["cw"], p["al"], p["dt"], p["dnn"], t + "deltanet")
    yb = gm_forward(cols, JB_GM, p["gain"], p["ws"], p["bt"], t + "gmlp")
    yc = sw_forward(cols, JB_SW, p["sinks"], t + "swa")
    ym = xm_forward(cols, JB_XM, mkv, t + "memattn")
    xn = merge_forward([ya, yb, yc, ym], cols, JB_GATE, xl, w_up, w_out, p["gpost"], t + "merge")
    return xn, dict(p, x=xl, h=h, cols=cols, mkv=mkv, ss=ss, ts=ts, ys=[ya, yb, yc, ym])


def _layer_backward(l, s, mem, weights, dx, token):
    w_in_al, w_mem, w_up, w_out = weights
    t = "l%d_" % l
    cols = s["cols"]
    dcols, dya, dyb, dyc, dym, dwup, dwout, dgpost = merge_backward(
        s["ys"], cols, JB_GATE, w_up, w_out, s["gpost"] + token, dx, t + "merge_bwd")
    dcols, dmkv = xm_backward(cols, JB_XM, s["mkv"], dym, dcols, t + "memattn_bwd")
    dgmem, dwmem = memkv_backward(mem, s["gmem"], w_mem, dmkv, t + "memkv_bwd")
    dcols, dsinks = sw_backward(cols, JB_SW, s["sinks"], dyc, dcols, t + "swa_bwd")
    dcols, dgain, dws, dbt = gm_backward(cols, JB_GM, s["gain"], s["ws"], s["bt"], dyb, dcols, t + "gmlp_bwd")
    dcols, dcw, dal, ddt, ddn = dn_backward(
        cols, JB_DN, s["cw"], s["al"], s["dt"], s["dnn"], s["ss"], s["ts"], dya, dcols, t + "deltanet_bwd")
    dh = _matmul(dcols, w_in_al, "nt", F32, t + "w_in_bwd_x")
    dwin = _matmul(s["h"], dcols, "tn", BF, t + "w_in_bwd_w")
    dx, dgpre = prenorm_backward(s["x"], s["gpre"], dh, dx, t + "prenorm_bwd")
    gsmall = dict(norm_pre=dgpre[0], norm_post=dgpost[0], norm_mem=dgmem[0], a_log=dal[0, 4:8], dt_bias=ddt[0, 4:8],
                  dn_norm=ddn[0], gm_norm=dgain[0], spatial_w=dws, spatial_b=dbt[:, :GM_GROUPS].T,
                  sinks=dsinks[0, :SW_HEADS], conv_w=dcw)
    return dx, gsmall, (dwin, dwmem, dwup, dwout)


def kernel(x, mem, norm_pre, norm_post, norm_mem, w_in, conv_w, a_log, dt_bias, dn_norm, gm_norm, spatial_w, spatial_b, sinks, w_mem_kv, w_up, w_out, loss_target, m_norm_pre, m_norm_post, m_norm_mem, m_w_in, m_conv_w, m_a_log, m_dt_bias, m_dn_norm, m_gm_norm, m_spatial_w, m_spatial_b, m_sinks, m_w_mem_kv, m_w_up, m_w_out, v_norm_pre, v_norm_post, v_norm_mem, v_w_in, v_conv_w, v_a_log, v_dt_bias, v_dn_norm, v_gm_norm, v_spatial_w, v_spatial_b, v_sinks, v_w_mem_kv, v_w_up, v_w_out):
    xi, yi, ci = _my_place()
    my_slot = 4 * xi + 2 * yi + ci
    conv_shard = conv_w.shape[-1]
    x2, mem2, target = x[0], mem[0], loss_target[0]

    w_in_pad = jnp.pad(w_in.astype(BF), ((0, 0), (0, 0), (0, W_IN_SHARD_PAD - W_IN_SHARD)))
    shards = [[w_in_pad[l], w_mem_kv[l].astype(BF), w_up[l].astype(BF), w_out[l].astype(BF)] for l in range(DEPTH)]
    *slots0, conv_slots = _all_gather_slots(shards[0] + [conv_w], "gather_weights_l0")
    ag_send, ag_recv, ag_src, ag_land, ag_token = _spread_start(shards[1], False, "gather_weights_l1_start")
    conv_full = jnp.transpose(conv_slots, (1, 2, 0, 3)).reshape(DEPTH, CONV_W, N_DEV * conv_shard)
    small = dict(norm_pre=norm_pre, norm_post=norm_post, norm_mem=norm_mem, a_log=a_log,
                 dt_bias=dt_bias, dn_norm=dn_norm, gm_norm=gm_norm, spatial_w=spatial_w,
                 spatial_b=spatial_b, sinks=sinks)

    weights0 = _whole_weights(*slots0)
    x1, saved0 = _layer_forward(0, x2, mem2, _layer_params(0, small, conv_full, ag_token[0, 0]), weights0)
    weights1 = _whole_weights(*_spread_wait(ag_send, ag_recv, ag_src, ag_land, x1, "gather_weights_l1_wait"))
    x_out, saved1 = _layer_forward(1, x1, mem2, _layer_params(1, small, conv_full, 0.0), weights1)
    loss, dx = loss_head(x_out, target, "loss_head")

    dx, gsmall1, gbig1 = _layer_backward(1, saved1, mem2, weights1, dx, 0.0)
    rs_send, rs_recv, rs_src, rs_land, rs_token = _spread_start(_grad_slots(*gbig1), True, "exchange_grads_l1_start")
    dx, gsmall0, gbig0 = _layer_backward(0, saved0, mem2, weights0, dx, rs_token[0, 0])
    parts1 = _spread_wait(rs_send, rs_recv, rs_src, rs_land, dx, "exchange_grads_l1_wait")
    g_slots = [g.reshape((N_DEV // 2, 2) + g.shape[1:]) for g in _grad_slots(*gbig0)]
    theirs = _exchange_sibling(g_slots, "exchange_sibling_l0")
    chip_sums = [_pair_sum(g, t, "pair_sum_l0_%d" % i) for i, (g, t) in enumerate(zip(g_slots, theirs))]
    parts0 = _exchange_chips(chip_sums, "exchange_chips_l0")

    packed_names = SMALL_NAMES + ("conv_w",)
    gs = {n: jnp.stack([gsmall0[n], gsmall1[n]]) for n in packed_names}
    small_parts = [loss[0, :1]] + [gs[n] for n in packed_names]
    tot = _unpack_small(_all_reduce_vmem(_pack_small(small_parts), "all_reduce_small"), [p.shape for p in small_parts])
    loss_tot = tot[0][0]
    grads = dict(zip(packed_names, tot[1:]))
    grads["conv_w"] = lax.dynamic_slice_in_dim(grads["conv_w"], my_slot * conv_shard, conv_shard, axis=2)

    given = dict(norm_pre=(norm_pre, m_norm_pre, v_norm_pre), norm_post=(norm_post, m_norm_post, v_norm_post),
                 norm_mem=(norm_mem, m_norm_mem, v_norm_mem), a_log=(a_log, m_a_log, v_a_log),
                 dt_bias=(dt_bias, m_dt_bias, v_dt_bias), dn_norm=(dn_norm, m_dn_norm, v_dn_norm),
                 gm_norm=(gm_norm, m_gm_norm, v_gm_norm), spatial_w=(spatial_w, m_spatial_w, v_spatial_w),
                 spatial_b=(spatial_b, m_spatial_b, v_spatial_b), sinks=(sinks, m_sinks, v_sinks),
                 conv_w=(conv_w, m_conv_w, v_conv_w))
    pshapes = [given[n][0].shape for n in packed_names]
    pw, pm, pv = (_pack_small([given[n][i] for n in packed_names]) for i in range(3))
    pd, pnm, pnv = _adamw(pw, _pack_small([grads[n] for n in packed_names]), pm, pv, "adamw_small")
    upd = {n: t for n, t in zip(packed_names, zip(_unpack_small(pd, pshapes), _unpack_small(pnm, pshapes),
                                                  _unpack_small(pnv, pshapes)))}
    for i, (name, (w, m, v)) in enumerate((("w_in", (w_in, m_w_in, v_w_in)),
                                           ("w_mem_kv", (w_mem_kv, m_w_mem_kv, v_w_mem_kv)),
                                           ("w_up", (w_up, m_w_up, v_w_up)), ("w_out", (w_out, m_w_out, v_w_out)))):
        first = _sum_adamw(parts1[i], w, m, v, 1, None, "adamw_%s_l1" % name)
        g, d, nm, nv = _sum_adamw(parts0[i], w, m, v, 0, first, "adamw_%s_l0" % name)
        grads[name], upd[name] = g, (d, nm, nv)

    order = ("norm_pre", "norm_post", "norm_mem", "w_in", "conv_w", "a_log", "dt_bias", "dn_norm",
             "gm_norm", "spatial_w", "spatial_b", "sinks", "w_mem_kv", "w_up", "w_out")
    return (loss_tot, dx[None], *[grads[n] for n in order], *[upd[n][0] for n in order],
            *[upd[n][1] for n in order], *[upd[n][2] for n in order])
```

```python
import functools
import math

import jax
import jax.numpy as jnp
from jax import lax
from jax.experimental import pallas as pl
from jax.experimental.pallas import tpu as pltpu

MESH = pl.DeviceIdType.MESH
N_DEV = 8

D_MODEL = 1024
DEPTH = 2
N_BRANCH = 4
BRANCH_W = 512
DN_HEADS = 4
CONV_W = 4
GM_GROUPS = 4
SW_HEADS = 8
EPS = 1e-6
NEG_INF = -1e30

D_IN = 9992
W_IN_SHARD = D_IN // N_DEV
W_IN_SHARD_PAD = 1280
D_IN_AL = 10752
DN_W, SW_W, GM_W, XM_W = 2560, 1536, 1536, 1024

ADAM_LR = 0.001
ADAM_B1 = 0.9
ADAM_B2 = 0.999
ADAM_EPS = 1e-08
ADAM_WD = 0.01
ADAM_STEP = 10

VMEM_LIMIT = 56 * 1024 * 1024

BF = jnp.bfloat16
F32 = jnp.float32
DN_C = 128
DN_D = 128
HALO = 8
BLK = 128


def _my_place():
    return lax.axis_index("x"), lax.axis_index("y"), lax.axis_index("c")


_ANY = pl.BlockSpec(memory_space=pl.ANY)


def _all_gather_slots(parts, name):
    n = len(parts)

    def body(*refs):
        p_refs, out_refs = refs[:n], refs[n:2 * n]
        send_sems, recv_sems, local_sems = refs[2 * n:]
        x, y, c = _my_place()
        me, sibling = (x, y, c), (x, y, 1 - c)
        chips = [(1 - x, y), (x, 1 - y), (1 - x, 1 - y)]

        def copy(a, k, block, to, src=None):
            px, py, pc = block
            slot = out_refs[a].at[4 * px + 2 * py + pc]
            return pltpu.make_async_remote_copy(
                src_ref=slot if src is None else src, dst_ref=slot,
                send_sem=send_sems.at[7 * a + k], recv_sem=recv_sems.at[7 * a + k],
                device_id=to, device_id_type=MESH)

        mine = [pltpu.make_async_copy(p_refs[a], out_refs[a].at[4 * x + 2 * y + c], local_sems.at[a])
                for a in range(n)]
        for cp in mine:
            cp.start()
        first = []
        for a in range(n):
            first.append(copy(a, 0, me, sibling, src=p_refs[a]))
            first += [copy(a, 1 + j, me, (*chip, c), src=p_refs[a]) for j, chip in enumerate(chips)]
        for cp in first:
            cp.start()
        passed = []
        for j, chip in enumerate(chips):
            for a in range(n):
                copy(a, 1 + j, (*chip, c), me).wait_recv()
                fwd = copy(a, 4 + j, (*chip, c), sibling)
                fwd.start()
                passed.append(fwd)
        for a in range(n):
            copy(a, 0, sibling, me).wait_recv()
            for j, chip in enumerate(chips):
                copy(a, 4 + j, (*chip, 1 - c), me).wait_recv()
        for cp in first + passed:
            cp.wait_send()
        for cp in mine:
            cp.wait()

    return pl.pallas_call(
        body, name=name,
        out_shape=[jax.ShapeDtypeStruct((N_DEV,) + p.shape, p.dtype) for p in parts],
        in_specs=[_ANY] * n, out_specs=[_ANY] * n,
        scratch_shapes=[pltpu.SemaphoreType.DMA((7 * n,)), pltpu.SemaphoreType.DMA((7 * n,)),
                        pltpu.SemaphoreType.DMA((n,))],
    )(*parts)


def _exchange_sibling(parts, name):
    n = len(parts)

    def body(*refs):
        g_refs, out_refs = refs[:n], refs[n:2 * n]
        send_sems, recv_sems = refs[2 * n:]
        x, y, c = _my_place()
        copies = [pltpu.make_async_remote_copy(
            src_ref=g_refs[a].at[:, 1 - c], dst_ref=out_refs[a],
            send_sem=send_sems.at[a], recv_sem=recv_sems.at[a],
            device_id=(x, y, 1 - c), device_id_type=MESH) for a in range(n)]
        for cp in copies:
            cp.start()
        for cp in copies:
            cp.wait()

    return pl.pallas_call(
        body, name=name,
        out_shape=[jax.ShapeDtypeStruct((4,) + g.shape[2:], g.dtype) for g in parts],
        in_specs=[_ANY] * n, out_specs=[_ANY] * n,
        scratch_shapes=[pltpu.SemaphoreType.DMA((n,)), pltpu.SemaphoreType.DMA((n,))],
    )(*parts)


def _exchange_chips(parts, name):
    n = len(parts)

    def body(*refs):
        p_refs, out_refs = refs[:n], refs[n:2 * n]
        send_sems, recv_sems, local_sems = refs[2 * n:]
        x, y, c = _my_place()
        my_chip = 2 * x + y
        own = [pltpu.make_async_copy(p_refs[a].at[my_chip], out_refs[a].at[my_chip], local_sems.at[a])
               for a in range(n)]
        for cp in own:
            cp.start()
        sends, recvs = [], []
        for k in range(1, 4):
            px = 1 - x if (k >> 1) & 1 else x
            py = 1 - y if k & 1 else y
            peer_chip = 2 * px + py
            for a in range(n):
                sem = 3 * a + k - 1
                sends.append(pltpu.make_async_remote_copy(
                    src_ref=p_refs[a].at[peer_chip], dst_ref=out_refs[a].at[my_chip],
                    send_sem=send_sems.at[sem], recv_sem=recv_sems.at[sem],
                    device_id=(px, py, c), device_id_type=MESH))
                recvs.append(pltpu.make_async_remote_copy(
                    src_ref=p_refs[a].at[peer_chip], dst_ref=out_refs[a].at[peer_chip],
                    send_sem=send_sems.at[sem], recv_sem=recv_sems.at[sem],
                    device_id=(px, py, c), device_id_type=MESH))
        for cp in sends:
            cp.start()
        for send, recv in zip(sends, recvs):
            send.wait_send()
            recv.wait_recv()
        for cp in own:
            cp.wait()

    return pl.pallas_call(
        body, name=name,
        out_shape=[jax.ShapeDtypeStruct(p.shape, p.dtype) for p in parts],
        in_specs=[_ANY] * n, out_specs=[_ANY] * n,
        scratch_shapes=[pltpu.SemaphoreType.DMA((3 * n,)), pltpu.SemaphoreType.DMA((3 * n,)),
                        pltpu.SemaphoreType.DMA((n,))],
    )(*parts)


_HBM = pl.BlockSpec(memory_space=pltpu.HBM)
_SEM = pl.BlockSpec(memory_space=pltpu.SEMAPHORE)
_EFFECT = pltpu.SideEffectType.DATAFLOW_SIDE_EFFECTING


def _peer(x, y, c, k):
    return (1 - x if (k >> 2) & 1 else x, 1 - y if (k >> 1) & 1 else y, 1 - c if k & 1 else c)


def _spread_start(srcs, scatter, name):
    n = len(srcs)
    lands = [lax.empty(s.shape if scatter else (N_DEV,) + s.shape, s.dtype) for s in srcs]

    def body(*refs):
        src_refs, land_refs = refs[:n], refs[n:2 * n]
        send_sems, recv_sems = refs[2 * n:2 * n + 2]
        token = refs[-1]
        x, y, c = _my_place()
        my_slot = 4 * x + 2 * y + c
        for a in range(n):
            for k in range(N_DEV):
                px, py, pc = _peer(x, y, c, k)
                src = src_refs[a].at[4 * px + 2 * py + pc] if scatter else src_refs[a]
                pltpu.make_async_remote_copy(
                    src_ref=src, dst_ref=land_refs[a].at[my_slot],
                    send_sem=send_sems.at[a], recv_sem=recv_sems.at[a],
                    device_id=(px, py, pc), device_id_type=MESH).start()
        token[...] = jnp.zeros_like(token)

    out = pl.pallas_call(
        body, name=name,
        out_shape=[pltpu.SemaphoreType.DMA((n,)), pltpu.SemaphoreType.DMA((n,))]
        + [pltpu.HBM(s.shape, s.dtype) for s in srcs] + [pltpu.HBM(l.shape, l.dtype) for l in lands]
        + [jax.ShapeDtypeStruct((8, 128), F32)],
        in_specs=[_HBM] * (2 * n),
        out_specs=[_SEM, _SEM] + [_HBM] * (2 * n) + [pl.BlockSpec(memory_space=pltpu.VMEM)],
        input_output_aliases={i: 2 + i for i in range(2 * n)},
        compiler_params=pltpu.CompilerParams(has_side_effects=_EFFECT),
    )(*[pltpu.with_memory_space_constraint(s, pltpu.HBM) for s in srcs],
      *[pltpu.with_memory_space_constraint(l, pltpu.HBM) for l in lands])
    return out[0], out[1], out[2:2 + n], out[2 + n:2 + 2 * n], out[-1]


def _spread_wait(send_sems, recv_sems, srcs, lands, after, name):
    n = len(srcs)

    def body(*refs):
        land_refs = refs[n:2 * n]
        send_sems, recv_sems = refs[2 * n:2 * n + 2]
        x, y, c = _my_place()
        for a in range(n):
            whole = pltpu.make_async_remote_copy(
                src_ref=land_refs[a], dst_ref=land_refs[a],
                send_sem=send_sems.at[a], recv_sem=recv_sems.at[a],
                device_id=(x, y, c), device_id_type=MESH)
            whole.wait_send()
            whole.wait_recv()

    out = pl.pallas_call(
        body, name=name,
        out_shape=[pltpu.HBM(s.shape, s.dtype) for s in srcs] + [pltpu.HBM(l.shape, l.dtype) for l in lands],
        in_specs=[_HBM] * (2 * n) + [_SEM, _SEM, _ANY],
        out_specs=[_HBM] * (2 * n),
        input_output_aliases={i: i for i in range(2 * n)},
        compiler_params=pltpu.CompilerParams(has_side_effects=_EFFECT),
    )(*srcs, *lands, send_sems, recv_sems, after)
    return out[n:]


def _all_reduce_vmem(v, name):
    def body(v_ref, out_ref, buf, send_sems, recv_sems):
        x, y, c = _my_place()
        peers = [(x, y, 1 - c), (1 - x, y, c), (x, 1 - y, c)]
        out_ref[...] = v_ref[...]
        for step, peer in enumerate(peers):
            cp = pltpu.make_async_remote_copy(
                src_ref=out_ref, dst_ref=buf.at[step],
                send_sem=send_sems.at[step], recv_sem=recv_sems.at[step],
                device_id=peer, device_id_type=MESH)
            cp.start()
            cp.wait()
            out_ref[...] = out_ref[...] + buf[step]

    return pl.pallas_call(
        body, name=name,
        out_shape=jax.ShapeDtypeStruct(v.shape, v.dtype),
        in_specs=[pl.BlockSpec(memory_space=pltpu.VMEM)],
        out_specs=pl.BlockSpec(memory_space=pltpu.VMEM),
        scratch_shapes=[pltpu.VMEM((3,) + v.shape, v.dtype),
                        pltpu.SemaphoreType.DMA((3,)), pltpu.SemaphoreType.DMA((3,))],
    )(v)


def _pick(n, pref):
    if n <= pref:
        return n
    t = pref - pref % 128
    while t > 0 and n % t:
        t -= 128
    return t if t > 0 else n


_DIMS = {"nn": (((1,), (0,)), ((), ())),
         "nt": (((1,), (1,)), ((), ())),
         "tn": (((0,), (0,)), ((), ()))}


def _matmul(a, b, mode, out_dtype, tiles, name):
    (m, k) = a.shape
    n = b.shape[1] if mode == "nn" else b.shape[0]
    tm, tn, tk = (_pick(d, t) for d, t in zip((m, n, k), tiles))
    nk = k // tk

    def product(a_ref, b_ref):
        return lax.dot_general(a_ref[...].astype(BF), b_ref[...].astype(BF), _DIMS[mode], preferred_element_type=F32)

    def body_whole_k(a_ref, b_ref, o_ref):
        o_ref[...] = product(a_ref, b_ref).astype(o_ref.dtype)

    def body_split_k(a_ref, b_ref, o_ref, acc_ref):
        kk = pl.program_id(2)

        @pl.when(kk == 0)
        def _():
            acc_ref[...] = jnp.zeros_like(acc_ref)

        acc_ref[...] += product(a_ref, b_ref)

        @pl.when(kk == nk - 1)
        def _():
            o_ref[...] = acc_ref[...].astype(o_ref.dtype)

    b_spec = (pl.BlockSpec((tn, tk), lambda i, j, kk: (j, kk)) if mode == "nt"
              else pl.BlockSpec((tk, tn), lambda i, j, kk: (kk, j)))
    return pl.pallas_call(
        body_whole_k if nk == 1 else body_split_k, name=name,
        out_shape=jax.ShapeDtypeStruct((m, n), out_dtype),
        grid=(m // tm, n // tn, nk),
        in_specs=[pl.BlockSpec((tm, tk), lambda i, j, kk: (i, kk)), b_spec],
        out_specs=pl.BlockSpec((tm, tn), lambda i, j, kk: (i, j)),
        scratch_shapes=[] if nk == 1 else [pltpu.VMEM((tm, tn), F32)],
        compiler_params=pltpu.CompilerParams(
            dimension_semantics=("parallel", "parallel", "arbitrary"),
            vmem_limit_bytes=VMEM_LIMIT),
    )(a, b)


def _rows2d(t, lead):
    return t.reshape(t.shape[:lead] + (math.prod(t.shape[lead:-1]), t.shape[-1]))


def _pair_sum(g, theirs, name):
    g3, t3 = _rows2d(g, 2), _rows2d(theirs, 1)
    _, r, w = t3.shape
    tr = _pick(r, 512)

    def body(g_ref, t_ref, o_ref):
        c = lax.axis_index("c")
        mine = jnp.where(c == 0, g_ref[0, 0], g_ref[0, 1])
        o_ref[0] = (mine.astype(F32) + t_ref[0].astype(F32)).astype(o_ref.dtype)

    out = pl.pallas_call(
        body, name=name,
        out_shape=jax.ShapeDtypeStruct(t3.shape, t3.dtype),
        grid=(4, r // tr),
        in_specs=[pl.BlockSpec((1, 2, tr, w), lambda q, i: (q, 0, i, 0)),
                  pl.BlockSpec((1, tr, w), lambda q, i: (q, i, 0))],
        out_specs=pl.BlockSpec((1, tr, w), lambda q, i: (q, i, 0)),
        compiler_params=pltpu.CompilerParams(dimension_semantics=("parallel", "parallel")),
    )(g3, t3)
    return out.reshape(theirs.shape)


def _adam_update(w, g, m, v):
    c1 = 1.0 - ADAM_B1 ** ADAM_STEP
    c2 = 1.0 - ADAM_B2 ** ADAM_STEP
    nm = ADAM_B1 * m + (1.0 - ADAM_B1) * g
    nv = ADAM_B2 * v + (1.0 - ADAM_B2) * (g * g)
    delta = -ADAM_LR * ((nm / c1) / (jnp.sqrt(nv / c2) + ADAM_EPS) + ADAM_WD * w)
    return delta, nm, nv


def _sum_adamw(parts, w, m, v, layer, carry, name):
    shape = w.shape
    cols = shape[-1]
    p3 = _rows2d(parts, 1)
    w3, m3, v3 = (_rows2d(t, 1) for t in (w, m, v))
    rows = w3.shape[1]
    tr = _pick(rows, 128)
    n_parts = p3.shape[0]

    def body(p_ref, w_ref, m_ref, v_ref, *rest):
        g_ref, d_ref, nm_ref, nv_ref = rest[-4:]
        g = p_ref[0, :, :cols].astype(F32)
        for q in range(1, n_parts):
            g = g + p_ref[q, :, :cols].astype(F32)
        d, nm, nv = _adam_update(w_ref[0], g, m_ref[0], v_ref[0])
        g_ref[0] = g
        d_ref[0] = d
        nm_ref[0] = nm
        nv_ref[0] = nv

    spec = pl.BlockSpec((1, tr, cols), lambda i: (layer, i, 0))
    extra = [] if carry is None else [_rows2d(t, 1) for t in carry]
    out = pl.pallas_call(
        body, name=name,
        out_shape=[jax.ShapeDtypeStruct(w3.shape, F32)] * 4,
        grid=(rows // tr,),
        in_specs=[pl.BlockSpec((n_parts, tr, p3.shape[-1]), lambda i: (0, i, 0)), spec, spec, spec] + [_ANY] * len(extra),
        out_specs=[spec] * 4,
        input_output_aliases={4 + i: i for i in range(len(extra))},
        compiler_params=pltpu.CompilerParams(dimension_semantics=("parallel",)),
    )(p3, w3, m3, v3, *extra)
    return tuple(t.reshape(shape) for t in out)


def _adamw(w, g, m, v, name):
    rows, cols = w.shape
    tr = _pick(rows, 128)

    def body(w_ref, g_ref, m_ref, v_ref, d_ref, nm_ref, nv_ref):
        d, nm, nv = _adam_update(w_ref[...], g_ref[...], m_ref[...], v_ref[...])
        d_ref[...] = d
        nm_ref[...] = nm
        nv_ref[...] = nv

    spec = pl.BlockSpec((tr, cols), lambda i: (i, 0))
    return pl.pallas_call(
        body, name=name,
        out_shape=[jax.ShapeDtypeStruct((rows, cols), F32)] * 3,
        grid=(rows // tr,),
        in_specs=[spec] * 4, out_specs=[spec] * 3,
        compiler_params=pltpu.CompilerParams(dimension_semantics=("parallel",)),
    )(w, g, m, v)


_VJP = {"nn": (("nt", "gb"), ("tn", "ag")),
        "nt": (("nn", "gb"), ("tn", "ga")),
        "tn": (("nt", "bg"), ("nn", "ag"))}


def _make_dot(cast, precision):
    def raw(mode, a, b):
        return lax.dot_general(cast(a), cast(b), _DIMS[mode], precision=precision,
                               preferred_element_type=F32)

    @functools.partial(jax.custom_vjp, nondiff_argnums=(0,))
    def dot(mode, a, b):
        return raw(mode, a, b)

    def fwd(mode, a, b):
        return raw(mode, a, b), (a, b)

    def bwd(mode, res, g):
        a, b = res
        pick = {"a": a, "b": b, "g": g}
        (ma, ta), (mb, tb) = _VJP[mode]
        return dot(ma, pick[ta[0]], pick[ta[1]]), dot(mb, pick[tb[0]], pick[tb[1]])

    dot.defvjp(fwd, bwd)
    return dot


bdot = _make_dot(lambda t: t.astype(BF), None)
hdot = _make_dot(lambda t: t, lax.Precision.HIGHEST)


def _xdot(mode, a, b):
    return lax.dot_general(a, b, _DIMS[mode], precision=lax.Precision.HIGH, preferred_element_type=F32)


def _unit_lower_inverse(Ls):
    n = Ls[0].shape[0]
    batched = (((2,), (1,)), ((0,), (0,)))
    mm = lambda a, b: lax.dot_general(a, b, batched, precision=lax.Precision.HIGH, preferred_element_type=F32)
    eye = (lax.broadcasted_iota(jnp.int32, (n, n), 0) == lax.broadcasted_iota(jnp.int32, (n, n), 1)).astype(F32)
    p = jnp.stack(Ls)
    t_inv = eye[None] - p
    for _ in range(6):
        p = mm(p, p)
        t_inv = t_inv + mm(t_inv, p)
    return [t_inv[h] for h in range(len(Ls))]


@jax.custom_vjp
def _tri_solve(L, rhs, t_inv):
    return _xdot("nn", t_inv, rhs)


def _tri_solve_fwd(L, rhs, t_inv):
    sol = _xdot("nn", t_inv, rhs)
    return sol, (t_inv, sol)


def _tri_solve_bwd(res, dsol):
    t_inv, sol = res
    drhs = _xdot("tn", t_inv, dsol)
    return -_xdot("nt", drhs, sol), drhs, jnp.zeros_like(t_inv)


_tri_solve.defvjp(_tri_solve_fwd, _tri_solve_bwd)


def _sigmoid(x):
    return 1.0 / (1.0 + jnp.exp(-x))


def _softplus(x):
    return jnp.maximum(x, 0.0) + jnp.log(1.0 + jnp.exp(-jnp.abs(x)))


def _dn_chunk(S, xs, ba, z, cw, al, dt, dn, t_saved=None):
    C = DN_C
    pre = xs[0] * cw[0] + xs[1] * cw[1] + xs[2] * cw[2] + xs[3] * cw[3]
    qkv = pre * _sigmoid(pre)
    lane = lax.broadcasted_iota(jnp.int32, (1, 128), 1)
    sub = lax.broadcasted_iota(jnp.int32, (C, 1), 0)
    row_i = lax.broadcasted_iota(jnp.int32, (C, C), 0)
    col_i = lax.broadcasted_iota(jnp.int32, (C, C), 1)
    strict = row_i > col_i
    incl = row_i >= col_i
    g_all = jnp.where((lane >= 4) & (lane < 8), -jnp.exp(al) * _softplus(ba + dt), 0.0)
    gc_all = hdot("nn", incl.astype(F32), g_all)
    gc_all_t = gc_all.T
    beta_all = _sigmoid(ba)
    glast_all = jnp.sum(jnp.where(sub == C - 1, gc_all, 0.0), axis=0, keepdims=True)
    heads = []
    for h in range(DN_HEADS):
        q = qkv[:, 128 * h:128 * (h + 1)]
        k = qkv[:, 512 + 128 * h:512 + 128 * (h + 1)]
        v = qkv[:, 1024 + 128 * h:1024 + 128 * (h + 1)]
        q = q * lax.rsqrt(jnp.sum(q * q, axis=1, keepdims=True) + EPS) * (DN_D ** -0.5)
        k = k * lax.rsqrt(jnp.sum(k * k, axis=1, keepdims=True) + EPS)
        beta = jnp.sum(jnp.where(lane == h, beta_all, 0.0), axis=1, keepdims=True)
        gc = jnp.sum(jnp.where(lane == 4 + h, gc_all, 0.0), axis=1, keepdims=True)
        gc_row = jnp.sum(jnp.where(sub == 4 + h, gc_all_t, 0.0), axis=0, keepdims=True)
        g_last = jnp.sum(jnp.where(lane == 4 + h, glast_all, 0.0), axis=1, keepdims=True)
        diff = gc - gc_row
        kb = k * beta
        L = jnp.where(strict, bdot("nt", kb, k) * jnp.exp(jnp.where(strict, diff, 0.0)), 0.0)
        heads.append((q, k, v, beta, gc, g_last, diff, kb, L))
    t_invs = _unit_lower_inverse([hd[-1] for hd in heads]) if t_saved is None else t_saved
    ys, s_new = [], []
    for h, (q, k, v, beta, gc, g_last, diff, kb, L) in enumerate(heads):
        sol = _tri_solve(L, jnp.concatenate([v * beta, kb * jnp.exp(gc)], axis=1), t_invs[h])
        u, w = sol[:, :DN_D], sol[:, DN_D:]
        a_qk = jnp.where(incl, bdot("nt", q, k) * jnp.exp(jnp.where(incl, diff, 0.0)), 0.0)
        qg = q * jnp.exp(gc)
        kd = k * jnp.exp(g_last - gc)
        v_new = u - bdot("nn", w, S[h])
        o = bdot("nn", qg, S[h]) + bdot("nn", a_qk, v_new)
        s_new.append(S[h] * jnp.exp(g_last) + bdot("tn", kd, v_new))
        o = o * lax.rsqrt(jnp.mean(o * o, axis=1, keepdims=True) + EPS) * dn
        zh = z[:, 128 * h:128 * (h + 1)]
        ys.append(o * (zh * _sigmoid(zh)))
    return jnp.concatenate(ys, axis=1), tuple(s_new), tuple(t_invs)


def _load_shifted(xbuf, x_ref, halo_ref, first):
    xbuf[0:HALO, :] = jnp.where(first, 0.0, halo_ref[:, 0:1536])
    xbuf[HALO:HALO + DN_C, :] = x_ref[:, 0:1536]
    return [xbuf[HALO - 3 + k:HALO - 3 + k + DN_C, :] for k in range(4)]


def dn_forward(cols, jblk, cw, al, dt, dn, name):
    T = cols.shape[0]
    n = T // DN_C

    def body(x_ref, halo_ref, cw_ref, al_ref, dt_ref, dn_ref, y_ref, ss_ref, ts_ref, s_scr, xbuf):
        i = pl.program_id(0)

        @pl.when(i == 0)
        def _():
            s_scr[...] = jnp.zeros_like(s_scr)

        xs = _load_shifted(xbuf, x_ref, halo_ref, i == 0)
        ss_ref[0] = s_scr[...]
        S = [s_scr[h] for h in range(DN_HEADS)]
        cws = [cw_ref[k:k + 1, :] for k in range(4)]
        y, s_new, t_invs = _dn_chunk(S, xs, x_ref[:, 2048:2176], x_ref[:, 1536:2048], cws,
                                     al_ref[...], dt_ref[...], dn_ref[...])
        y_ref[...] = y
        for h in range(DN_HEADS):
            s_scr[h] = s_new[h]
            ts_ref[0, h] = t_invs[h]

    per = DN_C // HALO
    full = lambda shape: pl.BlockSpec(shape, lambda i: (0,) * len(shape))
    return pl.pallas_call(
        body, name=name,
        out_shape=[jax.ShapeDtypeStruct((T, 512), F32),
                   jax.ShapeDtypeStruct((n, DN_HEADS, DN_D, DN_D), F32),
                   jax.ShapeDtypeStruct((n, DN_HEADS, DN_D, DN_D), F32)],
        grid=(n,),
        in_specs=[pl.BlockSpec((DN_C, DN_W), lambda i: (i, jblk)),
                  pl.BlockSpec((HALO, DN_W), lambda i: (jnp.maximum(i * per - 1, 0), jblk)),
                  full((4, 1536)), full((1, 128)), full((1, 128)), full((1, 128))],
        out_specs=[pl.BlockSpec((DN_C, 512), lambda i: (i, 0)),
                   pl.BlockSpec((1, DN_HEADS, DN_D, DN_D), lambda i: (i, 0, 0, 0)),
                   pl.BlockSpec((1, DN_HEADS, DN_D, DN_D), lambda i: (i, 0, 0, 0))],
        scratch_shapes=[pltpu.VMEM((DN_HEADS, DN_D, DN_D), F32), pltpu.VMEM((HALO + DN_C, 1536), F32)],
        compiler_params=pltpu.CompilerParams(dimension_semantics=("arbitrary",)),
    )(cols, cols, cw, al, dt, dn)


def dn_backward(cols, jblk, cw, al, dt, dn, ss, ts, dy, dcols, name):
    T = cols.shape[0]
    n = T // DN_C

    def body(x_ref, halo_ref, cw_ref, al_ref, dt_ref, dn_ref, ss_ref, ts_ref, dy_ref, dcols_in,
             dx_ref, dcw_ref, dal_ref, ddt_ref, ddn_ref, ds_scr, xbuf, dbuf, carry):
        i = pl.program_id(0)

        @pl.when(i == 0)
        def _():
            ds_scr[...] = jnp.zeros_like(ds_scr)
            carry[...] = jnp.zeros_like(carry)
            dcw_ref[...] = jnp.zeros_like(dcw_ref)
            dal_ref[...] = jnp.zeros_like(dal_ref)
            ddt_ref[...] = jnp.zeros_like(ddt_ref)
            ddn_ref[...] = jnp.zeros_like(ddn_ref)

        xs = _load_shifted(xbuf, x_ref, halo_ref, i == n - 1)
        S = [ss_ref[0, h] for h in range(DN_HEADS)]
        cws = [cw_ref[k:k + 1, :] for k in range(4)]

        t_saved = [ts_ref[0, h] for h in range(DN_HEADS)]

        def f(S, xs, ba, z, cws, al, dt, dn):
            return _dn_chunk(S, xs, ba, z, cws, al, dt, dn, t_saved)[:2]

        _, vjp = jax.vjp(f, S, xs, x_ref[:, 2048:2176], x_ref[:, 1536:2048], cws, al_ref[...], dt_ref[...], dn_ref[...])
        dS, dxs, dba, dz, dcws, dal, ddt, ddn = vjp((dy_ref[...], tuple(ds_scr[h] for h in range(DN_HEADS))))
        for h in range(DN_HEADS):
            ds_scr[h] = dS[h]
        dbuf[...] = jnp.zeros_like(dbuf)
        for k in range(4):
            lo = HALO - 3 + k
            dbuf[lo:lo + DN_C, :] += dxs[k]
        dbuf[DN_C:DN_C + HALO, :] += carry[...]
        dx_ref[...] = jnp.concatenate([dbuf[HALO:HALO + DN_C, :], dz, dba,
                                       jnp.zeros((DN_C, DN_W - 2176), F32)], axis=1).astype(dx_ref.dtype)
        carry[...] = dbuf[0:HALO, :]
        for k in range(4):
            dcw_ref[k:k + 1, :] += dcws[k]
        dal_ref[...] += dal
        ddt_ref[...] += ddt
        ddn_ref[...] += ddn

    per = DN_C // HALO
    rev = lambda i: n - 1 - i
    full = lambda shape: pl.BlockSpec(shape, lambda i: (0,) * len(shape))
    return pl.pallas_call(
        body, name=name,
        out_shape=[jax.ShapeDtypeStruct(dcols.shape, dcols.dtype),jax.ShapeDtypeStruct((4, 1536), F32),
                   jax.ShapeDtypeStruct((1, 128), F32), jax.ShapeDtypeStruct((1, 128), F32),
                   jax.ShapeDtypeStruct((1, 128), F32)],
        grid=(n,),
        in_specs=[pl.BlockSpec((DN_C, DN_W), lambda i: (rev(i), jblk)),
                  pl.BlockSpec((HALO, DN_W), lambda i: (jnp.maximum(rev(i) * per - 1, 0), jblk)),
                  full((4, 1536)), full((1, 128)), full((1, 128)), full((1, 128)),
                  pl.BlockSpec((1, DN_HEADS, DN_D, DN_D), lambda i: (rev(i), 0, 0, 0)),
                  pl.BlockSpec((1, DN_HEADS, DN_D, DN_D), lambda i: (rev(i), 0, 0, 0)),
                  pl.BlockSpec((DN_C, 512), lambda i: (rev(i), 0)), _ANY],
        out_specs=[pl.BlockSpec((DN_C, DN_W), lambda i: (rev(i), jblk)),
                   full((4, 1536)), full((1, 128)), full((1, 128)), full((1, 128))],
        scratch_shapes=[pltpu.VMEM((DN_HEADS, DN_D, DN_D), F32), pltpu.VMEM((HALO + DN_C, 1536), F32),
                        pltpu.VMEM((HALO + DN_C, 1536), F32), pltpu.VMEM((HALO, 1536), F32)],
        input_output_aliases={9: 0},
        compiler_params=pltpu.CompilerParams(dimension_semantics=("arbitrary",)),
    )(cols, cols, cw, al, dt, dn, ss, ts, dy, dcols)


def _full(shape):
    return pl.BlockSpec(shape, lambda i: (0,) * len(shape))


def _silu(x):
    return x * _sigmoid(x)


def _gelu(x):
    return 0.5 * x * (1.0 + jnp.tanh(0.7978845608028654 * (x + 0.044715 * (x * x * x))))


def _lane_col(mat, idx):
    lane = lax.broadcasted_iota(jnp.int32, (1, mat.shape[1]), 1)
    return jnp.sum(jnp.where(lane == idx, mat, 0.0), axis=1, keepdims=True)


def _gm_chunk(uv, z, gain, ws, bt):
    g = _gelu(uv)
    u, v = g[:, :512], g[:, 512:]
    v = v * lax.rsqrt(jnp.mean(v * v, axis=1, keepdims=True) + EPS) * gain
    row_i = lax.broadcasted_iota(jnp.int32, (BLK, BLK), 0)
    col_i = lax.broadcasted_iota(jnp.int32, (BLK, BLK), 1)
    causal = row_i >= col_i
    ss = []
    for grp in range(4):
        wg = jnp.where(causal, ws[grp], 0.0)
        ss.append(bdot("nn", wg, v[:, BLK * grp:BLK * (grp + 1)]) + _lane_col(bt, grp))
    return u * jnp.concatenate(ss, axis=1) * _silu(z)


def gm_forward(cols, jblk, gain, ws, bt, name):
    T = cols.shape[0]

    def body(x_ref, gain_ref, ws_ref, bt_ref, y_ref):
        y_ref[...] = _gm_chunk(x_ref[:, 0:1024], x_ref[:, 1024:1536], gain_ref[...],
                               [ws_ref[g] for g in range(4)], bt_ref[...])

    return pl.pallas_call(
        body, name=name, out_shape=jax.ShapeDtypeStruct((T, 512), F32), grid=(T // BLK,),
        in_specs=[pl.BlockSpec((BLK, GM_W), lambda i: (i, jblk)),
                  _full((1, 512)), _full((4, BLK, BLK)), _full((BLK, BLK))],
        out_specs=pl.BlockSpec((BLK, 512), lambda i: (i, 0)),
        compiler_params=pltpu.CompilerParams(dimension_semantics=("parallel",)),
    )(cols, gain, ws, bt)


def gm_backward(cols, jblk, gain, ws, bt, dy, dcols, name):
    T = cols.shape[0]

    def body(x_ref, gain_ref, ws_ref, bt_ref, dy_ref, dcols_in, dx_ref, dgain_ref, dws_ref, dbt_ref):
        @pl.when(pl.program_id(0) == 0)
        def _():
            dgain_ref[...] = jnp.zeros_like(dgain_ref)
            dws_ref[...] = jnp.zeros_like(dws_ref)
            dbt_ref[...] = jnp.zeros_like(dbt_ref)

        _, vjp = jax.vjp(_gm_chunk, x_ref[:, 0:1024], x_ref[:, 1024:1536], gain_ref[...],
                         [ws_ref[g] for g in range(4)], bt_ref[...])
        duv, dz, dgain, dws, dbt = vjp(dy_ref[...])
        dx_ref[...] = jnp.concatenate([duv, dz], axis=1).astype(dx_ref.dtype)
        dgain_ref[...] += dgain
        for g in range(4):
            dws_ref[g] += dws[g]
        dbt_ref[...] += dbt

    return pl.pallas_call(
        body, name=name,
        out_shape=[jax.ShapeDtypeStruct(dcols.shape, dcols.dtype),jax.ShapeDtypeStruct((1, 512), F32),
                   jax.ShapeDtypeStruct((4, BLK, BLK), F32), jax.ShapeDtypeStruct((BLK, BLK), F32)],
        grid=(T // BLK,),
        in_specs=[pl.BlockSpec((BLK, GM_W), lambda i: (i, jblk)),
                  _full((1, 512)), _full((4, BLK, BLK)), _full((BLK, BLK)),
                  pl.BlockSpec((BLK, 512), lambda i: (i, 0)), _ANY],
        out_specs=[pl.BlockSpec((BLK, GM_W), lambda i: (i, jblk)),
                   _full((1, 512)), _full((4, BLK, BLK)), _full((BLK, BLK))],
        input_output_aliases={5: 0},
        compiler_params=pltpu.CompilerParams(dimension_semantics=("arbitrary",)),
    )(cols, gain, ws, bt, dy, dcols)


def _sw_block(first, q, kp, kc, vp, vc, z, sinks):
    P = BLK
    lane = lax.broadcasted_iota(jnp.int32, (1, 128), 1)
    r = lax.broadcasted_iota(jnp.int32, (128, 128), 0)
    c = lax.broadcasted_iota(jnp.int32, (128, 128), 1)
    swap = (c == (r + 64) % 128).astype(F32)
    k2 = jnp.concatenate([kp, kc], axis=0)
    v2 = jnp.concatenate([vp, vc], axis=0)
    k2s = bdot("nn", k2, swap)
    v2s = bdot("nn", v2, swap)
    qi = lax.broadcasted_iota(jnp.int32, (P, 2 * P), 0)
    kj = lax.broadcasted_iota(jnp.int32, (P, 2 * P), 1)
    dist = qi + P - kj
    valid = (dist >= 0) & (dist < P) & ((kj >= P) | jnp.logical_not(first))
    outs = []
    for j in range(4):
        acc = jnp.zeros((P, 128), F32)
        for half in range(2):
            h = 2 * j + half
            kv = h // 4
            in_half = (lane >= 64 * half) & (lane < 64 * half + 64)
            qh = jnp.where(in_half, q[:, 128 * j:128 * (j + 1)], 0.0)
            same = (half == kv)
            s = bdot("nt", qh, k2 if same else k2s) * (64 ** -0.5)
            s = jnp.where(valid, s, NEG_INF)
            sink = _lane_col(sinks, h)
            m = lax.stop_gradient(jnp.maximum(jnp.max(s, axis=1, keepdims=True), sink))
            e = jnp.exp(s - m)
            p = e / (jnp.sum(e, axis=1, keepdims=True) + jnp.exp(sink - m))
            o = bdot("nn", p, v2 if same else v2s)
            acc = acc + jnp.where(in_half, o, 0.0)
        outs.append(acc)
    return jnp.concatenate(outs, axis=1) * _silu(z)


def _sw_specs(jblk, idx):
    prev = lambda i: jnp.maximum(idx(i) - 1, 0)
    jk = (jblk * SW_W + 1024) // 128
    return [pl.BlockSpec((BLK, SW_W), lambda i: (idx(i), jblk)),
            pl.BlockSpec((BLK, 128), lambda i: (prev(i), jk)),
            pl.BlockSpec((BLK, 128), lambda i: (prev(i), jk + 1)), _full((1, 128))]


def sw_forward(cols, jblk, sinks, name):
    T = cols.shape[0]

    def body(x_ref, kp_ref, vp_ref, s_ref, y_ref):
        y_ref[...] = _sw_block(pl.program_id(0) == 0, x_ref[:, 0:512], kp_ref[...], x_ref[:, 1024:1152],
                               vp_ref[...], x_ref[:, 1152:1280], x_ref[:, 512:1024], s_ref[...])

    return pl.pallas_call(
        body, name=name, out_shape=jax.ShapeDtypeStruct((T, 512), F32), grid=(T // BLK,),
        in_specs=_sw_specs(jblk, lambda i: i),
        out_specs=pl.BlockSpec((BLK, 512), lambda i: (i, 0)),
        compiler_params=pltpu.CompilerParams(dimension_semantics=("parallel",)),
    )(cols, cols, cols, sinks)


def sw_backward(cols, jblk, sinks, dy, dcols, name):
    T = cols.shape[0]
    n = T // BLK
    rev = lambda i: n - 1 - i

    def body(x_ref, kp_ref, vp_ref, s_ref, dy_ref, dcols_in, dx_ref, ds_ref, kcarry, vcarry):
        i = pl.program_id(0)

        @pl.when(i == 0)
        def _():
            kcarry[...] = jnp.zeros_like(kcarry)
            vcarry[...] = jnp.zeros_like(vcarry)
            ds_ref[...] = jnp.zeros_like(ds_ref)

        f = functools.partial(_sw_block, i == n - 1)
        _, vjp = jax.vjp(f, x_ref[:, 0:512], kp_ref[...], x_ref[:, 1024:1152], vp_ref[...], x_ref[:, 1152:1280],
                         x_ref[:, 512:1024], s_ref[...])
        dq, dkp, dkc, dvp, dvc, dz, dsk = vjp(dy_ref[...])
        dx_ref[...] = jnp.concatenate([dq, dz, dkc + kcarry[...], dvc + vcarry[...],
                                       jnp.zeros((BLK, SW_W - 1280), F32)], axis=1).astype(dx_ref.dtype)
        kcarry[...] = dkp
        vcarry[...] = dvp
        ds_ref[...] += dsk

    return pl.pallas_call(
        body, name=name,
        out_shape=[jax.ShapeDtypeStruct(dcols.shape, dcols.dtype),jax.ShapeDtypeStruct((1, 128), F32)],
        grid=(n,),
        in_specs=_sw_specs(jblk, rev) + [pl.BlockSpec((BLK, 512), lambda i: (rev(i), 0)), _ANY],
        out_specs=[pl.BlockSpec((BLK, SW_W), lambda i: (rev(i), jblk)), _full((1, 128))],
        scratch_shapes=[pltpu.VMEM((BLK, 128), F32), pltpu.VMEM((BLK, 128), F32)],
        input_output_aliases={5: 0},
        compiler_params=pltpu.CompilerParams(dimension_semantics=("arbitrary",)),
    )(cols, cols, cols, sinks, dy, dcols)


XM_TQ = 256


def _xm_block(q, z, mkv):
    outs = []
    for h in range(4):
        s = bdot("nt", q[:, 128 * h:128 * (h + 1)], mkv[:, 128 * h:128 * (h + 1)]) * (128 ** -0.5)
        m = lax.stop_gradient(jnp.max(s, axis=1, keepdims=True))
        e = jnp.exp(s - m)
        p = e / jnp.sum(e, axis=1, keepdims=True)
        outs.append(bdot("nn", p, mkv[:, 512 + 128 * h:512 + 128 * (h + 1)]))
    return jnp.concatenate(outs, axis=1) * _silu(z)


def xm_forward(cols, jblk, mkv, name):
    T = cols.shape[0]

    def body(x_ref, m_ref, y_ref):
        y_ref[...] = _xm_block(x_ref[:, 0:512], x_ref[:, 512:1024], m_ref[...])

    return pl.pallas_call(
        body, name=name, out_shape=jax.ShapeDtypeStruct((T, 512), F32), grid=(T // XM_TQ,),
        in_specs=[pl.BlockSpec((XM_TQ, XM_W), lambda i: (i, jblk)), _full(mkv.shape)],
        out_specs=pl.BlockSpec((XM_TQ, 512), lambda i: (i, 0)),
        compiler_params=pltpu.CompilerParams(dimension_semantics=("parallel",)),
    )(cols, mkv)


def xm_backward(cols, jblk, mkv, dy, dcols, name):
    T = cols.shape[0]

    def body(x_ref, m_ref, dy_ref, dcols_in, dx_ref, dm_ref):
        @pl.when(pl.program_id(0) == 0)
        def _():
            dm_ref[...] = jnp.zeros_like(dm_ref)

        _, vjp = jax.vjp(_xm_block, x_ref[:, 0:512], x_ref[:, 512:1024], m_ref[...])
        dq, dz, dm = vjp(dy_ref[...])
        dx_ref[...] = jnp.concatenate([dq, dz], axis=1).astype(dx_ref.dtype)
        dm_ref[...] += dm

    return pl.pallas_call(
        body, name=name,
        out_shape=[jax.ShapeDtypeStruct(dcols.shape, dcols.dtype),jax.ShapeDtypeStruct(mkv.shape, F32)],
        grid=(T // XM_TQ,),
        in_specs=[pl.BlockSpec((XM_TQ, XM_W), lambda i: (i, jblk)), _full(mkv.shape),
                  pl.BlockSpec((XM_TQ, 512), lambda i: (i, 0)), _ANY],
        out_specs=[pl.BlockSpec((XM_TQ, XM_W), lambda i: (i, jblk)), _full(mkv.shape)],
        input_output_aliases={3: 0},
        compiler_params=pltpu.CompilerParams(dimension_semantics=("arbitrary",)),
    )(cols, mkv, dy, dcols)


def _rms(x, gain):
    return x * lax.rsqrt(jnp.mean(x * x, axis=1, keepdims=True) + EPS) * gain


def memkv_forward(mem, gain, w, name):
    def body(m_ref, g_ref, w_ref, o_ref):
        o_ref[...] = bdot("nn", _rms(m_ref[...], g_ref[...]), w_ref[...])

    return pl.pallas_call(body, name=name, out_shape=jax.ShapeDtypeStruct(mem.shape, F32),
                          compiler_params=pltpu.CompilerParams(vmem_limit_bytes=VMEM_LIMIT))(mem, gain, w)


def memkv_backward(mem, gain, w, dkv, name):
    def body(m_ref, g_ref, w_ref, d_ref, dg_ref, dw_ref):
        mem_v = m_ref[...]
        _, vjp = jax.vjp(lambda g, ww: bdot("nn", _rms(mem_v, g), ww), g_ref[...], w_ref[...].astype(F32))
        dg, dw = vjp(d_ref[...])
        dg_ref[...] = dg
        dw_ref[...] = dw

    return pl.pallas_call(body, name=name,
                          out_shape=[jax.ShapeDtypeStruct(gain.shape, F32), jax.ShapeDtypeStruct(w.shape, F32)],
                          compiler_params=pltpu.CompilerParams(vmem_limit_bytes=VMEM_LIMIT))(mem, gain, w, dkv)


MG_TB = 128


def _merge_block(ys, gl, wup, wout, gpost):
    merged = None
    for n in range(4):
        t = _sigmoid(gl[:, 1024 * n:1024 * (n + 1)]) * bdot("nn", ys[n], wup[n])
        merged = t if merged is None else merged + t
    out = bdot("nn", merged, wout)
    return _rms(out, gpost)


def merge_forward(ys, cols, jgate, x, wup, wout, gpost, name):
    T = x.shape[0]
    TB = 256

    def body(ya, yb, yc, ym, gl_ref, x_ref, wup_ref, wout_ref, gp_ref, o_ref):
        upd = _merge_block([ya[...], yb[...], yc[...], ym[...]], gl_ref[...],
                           [wup_ref[n] for n in range(4)], wout_ref[...], gp_ref[...])
        o_ref[...] = x_ref[...] + upd

    yspec = pl.BlockSpec((TB, 512), lambda i: (i, 0))
    return pl.pallas_call(
        body, name=name, out_shape=jax.ShapeDtypeStruct((T, 1024), F32), grid=(T // TB,),
        in_specs=[yspec] * 4 + [pl.BlockSpec((TB, 4096), lambda i: (i, jgate)),
                                pl.BlockSpec((TB, 1024), lambda i: (i, 0)),
                                _full(wup.shape), _full(wout.shape), _full((1, 1024))],
        out_specs=pl.BlockSpec((TB, 1024), lambda i: (i, 0)),
        compiler_params=pltpu.CompilerParams(dimension_semantics=("parallel",), vmem_limit_bytes=VMEM_LIMIT),
    )(*ys, cols, x, wup, wout, gpost)


def merge_backward(ys, cols, jgate, wup, wout, gpost, dx, name):
    T = dx.shape[0]
    TB = MG_TB

    def body(ya, yb, yc, ym, gl_ref, wup_ref, wout_ref, gp_ref, dx_ref,
             dgl_ref, dya, dyb, dyc, dym, dwup_ref, dwout_ref, dgp_ref):
        @pl.when(pl.program_id(0) == 0)
        def _():
            dwup_ref[...] = jnp.zeros_like(dwup_ref)
            dwout_ref[...] = jnp.zeros_like(dwout_ref)
            dgp_ref[...] = jnp.zeros_like(dgp_ref)

        _, vjp = jax.vjp(_merge_block, [ya[...], yb[...], yc[...], ym[...]], gl_ref[...],
                         [wup_ref[n].astype(F32) for n in range(4)], wout_ref[...].astype(F32), gp_ref[...])
        dys, dgl, dwup, dwout, dgp = vjp(dx_ref[...])
        for ref, val in zip((dya, dyb, dyc, dym), dys):
            ref[...] = val
        dgl_ref[...] = dgl.astype(dgl_ref.dtype)
        for n in range(4):
            dwup_ref[n] += dwup[n]
        dwout_ref[...] += dwout
        dgp_ref[...] += dgp

    yspec = pl.BlockSpec((TB, 512), lambda i: (i, 0))
    return pl.pallas_call(
        body, name=name,
        out_shape=[jax.ShapeDtypeStruct(cols.shape, BF)] + [jax.ShapeDtypeStruct((T, 512), F32)] * 4 + [
            jax.ShapeDtypeStruct(wup.shape, F32), jax.ShapeDtypeStruct(wout.shape, F32),
            jax.ShapeDtypeStruct((1, 1024), F32)],
        grid=(T // TB,),
        in_specs=[yspec] * 4 + [pl.BlockSpec((TB, 4096), lambda i: (i, jgate)),
                                _full(wup.shape), _full(wout.shape), _full((1, 1024)),
                                pl.BlockSpec((TB, 1024), lambda i: (i, 0))],
        out_specs=[pl.BlockSpec((TB, 4096), lambda i: (i, jgate))] + [yspec] * 4 + [
            _full(wup.shape), _full(wout.shape), _full((1, 1024))],
        compiler_params=pltpu.CompilerParams(dimension_semantics=("arbitrary",), vmem_limit_bytes=VMEM_LIMIT),
    )(*ys, cols, wup, wout, gpost, dx)


NB = 256


def prenorm_forward(x, gain, name):
    T, D = x.shape

    def body(x_ref, g_ref, o_ref, ot_ref):
        h = _rms(x_ref[...], g_ref[...])
        o_ref[...] = h.astype(BF)
        ot_ref[...] = h.T.astype(BF)

    return pl.pallas_call(
        body, name=name,
        out_shape=[jax.ShapeDtypeStruct((T, D), BF), jax.ShapeDtypeStruct((D, T), BF)], grid=(T // NB,),
        in_specs=[pl.BlockSpec((NB, D), lambda i: (i, 0)), _full((1, D))],
        out_specs=[pl.BlockSpec((NB, D), lambda i: (i, 0)), pl.BlockSpec((D, NB), lambda i: (0, i))],
        compiler_params=pltpu.CompilerParams(dimension_semantics=("parallel",)),
    )(x, gain)


def prenorm_backward(x, gain, dh, dres, name):
    T = x.shape[0]

    def body(x_ref, g_ref, dh_ref, dr_ref, dx_ref, dg_ref):
        @pl.when(pl.program_id(0) == 0)
        def _():
            dg_ref[...] = jnp.zeros_like(dg_ref)

        _, vjp = jax.vjp(_rms, x_ref[...], g_ref[...])
        dxn, dg = vjp(dh_ref[...])
        dx_ref[...] = dr_ref[...] + dxn
        dg_ref[...] += dg

    spec = pl.BlockSpec((NB, 1024), lambda i: (i, 0))
    return pl.pallas_call(
        body, name=name,
        out_shape=[jax.ShapeDtypeStruct(x.shape, F32), jax.ShapeDtypeStruct((1, 1024), F32)], grid=(T // NB,),
        in_specs=[spec, _full((1, 1024)), spec, spec], out_specs=[spec, _full((1, 1024))],
        compiler_params=pltpu.CompilerParams(dimension_semantics=("arbitrary",)),
    )(x, gain, dh, dres)


def loss_head(y, target, name):
    T, D = y.shape

    def body(y_ref, t_ref, l_ref, d_ref):
        @pl.when(pl.program_id(0) == 0)
        def _():
            l_ref[...] = jnp.zeros_like(l_ref)

        err = y_ref[...] - t_ref[...]
        d_ref[...] = err * (1.0 / D)
        l_ref[...] += jnp.full(l_ref.shape, 0.5 * jnp.sum(jnp.mean(err * err, axis=1, keepdims=True)), F32)

    spec = pl.BlockSpec((NB, D), lambda i: (i, 0))
    return pl.pallas_call(
        body, name=name,
        out_shape=[jax.ShapeDtypeStruct((1, 128), F32), jax.ShapeDtypeStruct(y.shape, F32)], grid=(T // NB,),
        in_specs=[spec, spec], out_specs=[_full((1, 128)), spec],
        compiler_params=pltpu.CompilerParams(dimension_semantics=("arbitrary",)),
    )(y, target)


JB_GATE, JB_XM, JB_DN, JB_SW, JB_GM = 0, 4, 2, 5, 6
_ALIGNED_PIECES = ((5896, 4096), (4872, 512), (5384, 512), (0, 2048), (2048, 8), 504, (3592, 512), (4360, 512),
                   (4104, 128), (4232, 128), 256, (2056, 1024), (3080, 512))
_NATURAL_FROM_ALIGNED = ((5120, 2048), (7168, 8), (9216, 1024), (10240, 512), (7680, 512), (8704, 128), (8832, 128),
                         (8192, 512), (4096, 512), (4608, 512), (0, 4096))


def _natural_range(slots, start, width):
    out = []
    while width > 0:
        j, i = divmod(start, W_IN_SHARD)
        take = min(width, W_IN_SHARD - i)
        out.append(slots[j, :, i:i + take])
        start, width = start + take, width - take
    return out


def _aligned_w_in(slots):
    parts = []
    for piece in _ALIGNED_PIECES:
        if isinstance(piece, int):
            parts.append(jnp.zeros(slots.shape[1:2] + (piece,), slots.dtype))
        else:
            parts += _natural_range(slots, *piece)
    return jnp.concatenate(parts, axis=-1)


def _slots_of_aligned(d_al):
    slots = []
    for s in range(N_DEV):
        lo, hi = s * W_IN_SHARD, (s + 1) * W_IN_SHARD
        parts, nat = [], 0
        for a_start, width in _NATURAL_FROM_ALIGNED:
            b, e = max(lo, nat), min(hi, nat + width)
            if b < e:
                parts.append(d_al[..., a_start + b - nat:a_start + e - nat])
            nat += width
        parts.append(jnp.zeros(d_al.shape[:1] + (W_IN_SHARD_PAD - W_IN_SHARD,), d_al.dtype))
        slots.append(jnp.concatenate(parts, axis=-1))
    return jnp.stack(slots)


SMALL_VEC_W = 1024


def _pack_small(parts):
    rows = []
    for p in parts:
        flat = p.reshape(-1).astype(F32)
        r = -(-flat.shape[0] // SMALL_VEC_W)
        rows.append(jnp.pad(flat, (0, r * SMALL_VEC_W - flat.shape[0])).reshape(r, SMALL_VEC_W))
    vec = jnp.concatenate(rows, axis=0)
    return jnp.pad(vec, ((0, -vec.shape[0] % 8), (0, 0)))


def _unpack_small(vec, shapes):
    out, off = [], 0
    for s in shapes:
        n = math.prod(s)
        r = -(-n // SMALL_VEC_W)
        out.append(vec[off:off + r].reshape(-1)[:n].reshape(s))
        off += r
    return out


def _lanes(vec, at):
    return jnp.zeros((1, 128), F32).at[0, at:at + vec.shape[0]].set(vec)


SMALL_NAMES = ("norm_pre", "norm_post", "norm_mem", "a_log", "dt_bias", "dn_norm", "gm_norm",
               "spatial_w", "spatial_b", "sinks")


def _whole_weights(s_in, s_mem, s_up, s_out):
    return (_aligned_w_in(s_in), s_mem.reshape(D_MODEL, 2 * BRANCH_W),
            jnp.transpose(s_up, (1, 2, 0, 3)).reshape(N_BRANCH, BRANCH_W, D_MODEL), s_out.reshape(D_MODEL, D_MODEL))


def _grad_slots(d_in_al, d_mem, d_up, d_out):
    return [_slots_of_aligned(d_in_al), d_mem.astype(BF).reshape(N_DEV, 128, 2 * BRANCH_W),
            jnp.transpose(d_up.astype(BF).reshape(N_BRANCH, BRANCH_W, N_DEV, 128), (2, 0, 1, 3)),
            d_out.astype(BF).reshape(N_DEV, 128, D_MODEL)]


def _layer_params(l, small, conv_full, token):
    return dict(
        gpre=small["norm_pre"][l][None] + token, gpost=small["norm_post"][l][None], gmem=small["norm_mem"][l][None],
        cw=conv_full[l], al=_lanes(small["a_log"][l], 4), dt=_lanes(small["dt_bias"][l], 4),
        dnn=small["dn_norm"][l][None], gain=small["gm_norm"][l][None], ws=small["spatial_w"][l],
        bt=jnp.zeros((128, 128), F32).at[:, :GM_GROUPS].set(small["spatial_b"][l].T),
        sinks=_lanes(small["sinks"][l], 0))


def _layer_forward(l, xl, mem, p, weights):
    w_in_al, w_mem, w_up, w_out = weights
    t = "l%d_" % l
    h, h_t = prenorm_forward(xl, p["gpre"], t + "prenorm")
    cols = _matmul(h, w_in_al, "nn", F32, (1024, 1536, 1024), t + "w_in")
    mkv = memkv_forward(mem, p["gmem"], w_mem, t + "memkv")
    ya, ss, ts = dn_forward(cols, JB_DN, p["cw"], p["al"], p["dt"], p["dnn"], t + "deltanet")
    yb = gm_forward(cols, JB_GM, p["gain"], p["ws"], p["bt"], t + "gmlp")
    yc = sw_forward(cols, JB_SW, p["sinks"], t + "swa")
    ym = xm_forward(cols, JB_XM, mkv, t + "memattn")
    xn = merge_forward([ya, yb, yc, ym], cols, JB_GATE, xl, w_up, w_out, p["gpost"], t + "merge")
    return xn, dict(p, x=xl, h_t=h_t, cols=cols, mkv=mkv, ss=ss, ts=ts, ys=[ya, yb, yc, ym])


def _layer_backward(l, s, mem, weights, dx, token):
    w_in_al, w_mem, w_up, w_out = weights
    t = "l%d_" % l
    cols = s["cols"]
    dcols, dya, dyb, dyc, dym, dwup, dwout, dgpost = merge_backward(
        s["ys"], cols, JB_GATE, w_up, w_out, s["gpost"] + token, dx, t + "merge_bwd")
    dcols, dmkv = xm_backward(cols, JB_XM, s["mkv"], dym, dcols, t + "memattn_bwd")
    dgmem, dwmem = memkv_backward(mem, s["gmem"], w_mem, dmkv, t + "memkv_bwd")
    dcols, dsinks = sw_backward(cols, JB_SW, s["sinks"], dyc, dcols, t + "swa_bwd")
    dcols, dgain, dws, dbt = gm_backward(cols, JB_GM, s["gain"], s["ws"], s["bt"], dyb, dcols, t + "gmlp_bwd")
    dcols, dcw, dal, ddt, ddn = dn_backward(
        cols, JB_DN, s["cw"], s["al"], s["dt"], s["dnn"], s["ss"], s["ts"], dya, dcols, t + "deltanet_bwd")
    dh = _matmul(dcols, w_in_al, "nt", F32, (1024, 1024, 1024), t + "w_in_bwd_x")
    dwin = _matmul(s["h_t"], dcols, "nn", BF, (1024, 1536, 2048), t + "w_in_bwd_w")
    dx, dgpre = prenorm_backward(s["x"], s["gpre"], dh, dx, t + "prenorm_bwd")
    gsmall = dict(norm_pre=dgpre[0], norm_post=dgpost[0], norm_mem=dgmem[0], a_log=dal[0, 4:8], dt_bias=ddt[0, 4:8],
                  dn_norm=ddn[0], gm_norm=dgain[0], spatial_w=dws, spatial_b=dbt[:, :GM_GROUPS].T,
                  sinks=dsinks[0, :SW_HEADS], conv_w=dcw)
    return dx, gsmall, (dwin, dwmem, dwup, dwout)


def kernel(x, mem, norm_pre, norm_post, norm_mem, w_in, conv_w, a_log, dt_bias, dn_norm, gm_norm, spatial_w, spatial_b, sinks, w_mem_kv, w_up, w_out, loss_target, m_norm_pre, m_norm_post, m_norm_mem, m_w_in, m_conv_w, m_a_log, m_dt_bias, m_dn_norm, m_gm_norm, m_spatial_w, m_spatial_b, m_sinks, m_w_mem_kv, m_w_up, m_w_out, v_norm_pre, v_norm_post, v_norm_mem, v_w_in, v_conv_w, v_a_log, v_dt_bias, v_dn_norm, v_gm_norm, v_spatial_w, v_spatial_b, v_sinks, v_w_mem_kv, v_w_up, v_w_out):
    xi, yi, ci = _my_place()
    my_slot = 4 * xi + 2 * yi + ci
    conv_shard = conv_w.shape[-1]
    x2, mem2, target = x[0], mem[0], loss_target[0]

    w_in_pad = jnp.pad(w_in.astype(BF), ((0, 0), (0, 0), (0, W_IN_SHARD_PAD - W_IN_SHARD)))
    shards = [[w_in_pad[l], w_mem_kv[l].astype(BF), w_up[l].astype(BF), w_out[l].astype(BF)] for l in range(DEPTH)]
    *slots0, conv_slots = _all_gather_slots(shards[0] + [conv_w], "gather_weights_l0")
    ag_send, ag_recv, ag_src, ag_land, ag_token = _spread_start(shards[1], False, "gather_weights_l1_start")
    conv_full = jnp.transpose(conv_slots, (1, 2, 0, 3)).reshape(DEPTH, CONV_W, N_DEV * conv_shard)
    small = dict(norm_pre=norm_pre, norm_post=norm_post, norm_mem=norm_mem, a_log=a_log,
                 dt_bias=dt_bias, dn_norm=dn_norm, gm_norm=gm_norm, spatial_w=spatial_w,
                 spatial_b=spatial_b, sinks=sinks)

    weights0 = _whole_weights(*slots0)
    x1, saved0 = _layer_forward(0, x2, mem2, _layer_params(0, small, conv_full, ag_token[0, 0]), weights0)
    weights1 = _whole_weights(*_spread_wait(ag_send, ag_recv, ag_src, ag_land, x1, "gather_weights_l1_wait"))
    x_out, saved1 = _layer_forward(1, x1, mem2, _layer_params(1, small, conv_full, 0.0), weights1)
    loss, dx = loss_head(x_out, target, "loss_head")

    dx, gsmall1, gbig1 = _layer_backward(1, saved1, mem2, weights1, dx, 0.0)
    rs_send, rs_recv, rs_src, rs_land, rs_token = _spread_start(_grad_slots(*gbig1), True, "exchange_grads_l1_start")
    dx, gsmall0, gbig0 = _layer_backward(0, saved0, mem2, weights0, dx, rs_token[0, 0])
    parts1 = _spread_wait(rs_send, rs_recv, rs_src, rs_land, dx, "exchange_grads_l1_wait")
    g_slots = [g.reshape((N_DEV // 2, 2) + g.shape[1:]) for g in _grad_slots(*gbig0)]
    theirs = _exchange_sibling(g_slots, "exchange_sibling_l0")
    chip_sums = [_pair_sum(g, t, "pair_sum_l0_%d" % i) for i, (g, t) in enumerate(zip(g_slots, theirs))]
    parts0 = _exchange_chips(chip_sums, "exchange_chips_l0")

    packed_names = SMALL_NAMES + ("conv_w",)
    gs = {n: jnp.stack([gsmall0[n], gsmall1[n]]) for n in packed_names}
    small_parts = [loss[0, :1]] + [gs[n] for n in packed_names]
    tot = _unpack_small(_all_reduce_vmem(_pack_small(small_parts), "all_reduce_small"), [p.shape for p in small_parts])
    loss_tot = tot[0][0]
    grads = dict(zip(packed_names, tot[1:]))
    grads["conv_w"] = lax.dynamic_slice_in_dim(grads["conv_w"], my_slot * conv_shard, conv_shard, axis=2)

    given = dict(norm_pre=(norm_pre, m_norm_pre, v_norm_pre), norm_post=(norm_post, m_norm_post, v_norm_post),
                 norm_mem=(norm_mem, m_norm_mem, v_norm_mem), a_log=(a_log, m_a_log, v_a_log),
                 dt_bias=(dt_bias, m_dt_bias, v_dt_bias), dn_norm=(dn_norm, m_dn_norm, v_dn_norm),
                 gm_norm=(gm_norm, m_gm_norm, v_gm_norm), spatial_w=(spatial_w, m_spatial_w, v_spatial_w),
                 spatial_b=(spatial_b, m_spatial_b, v_spatial_b), sinks=(sinks, m_sinks, v_sinks),
                 conv_w=(conv_w, m_conv_w, v_conv_w))
    pshapes = [given[n][0].shape for n in packed_names]
    pw, pm, pv = (_pack_small([given[n][i] for n in packed_names]) for i in range(3))
    pd, pnm, pnv = _adamw(pw, _pack_small([grads[n] for n in packed_names]), pm, pv, "adamw_small")
    upd = {n: t for n, t in zip(packed_names, zip(_unpack_small(pd, pshapes), _unpack_small(pnm, pshapes),
                                                  _unpack_small(pnv, pshapes)))}
    for i, (name, (w, m, v)) in enumerate((("w_in", (w_in, m_w_in, v_w_in)),
                                           ("w_mem_kv", (w_mem_kv, m_w_mem_kv, v_w_mem_kv)),
                                           ("w_up", (w_up, m_w_up, v_w_up)), ("w_out", (w_out, m_w_out, v_w_out)))):
        first = _sum_adamw(parts1[i], w, m, v, 1, None, "adamw_%s_l1" % name)
        g, d, nm, nv = _sum_adamw(parts0[i], w, m, v, 0, first, "adamw_%s_l0" % name)
        grads[name], upd[name] = g, (d, nm, nv)

    order = ("norm_pre", "norm_post", "norm_mem", "w_in", "conv_w", "a_log", "dt_bias", "dn_norm",
             "gm_norm", "spatial_w", "spatial_b", "sinks", "w_mem_kv", "w_up", "w_out")
    return (loss_tot, dx[None], *[grads[n] for n in order], *[upd[n][0] for n in order],
            *[upd[n][1] for n in order], *[upd[n][2] for n in order])
```

```python
import functools
import math

import jax
import jax.numpy as jnp
from jax import lax
from jax.experimental import pallas as pl
from jax.experimental.pallas import tpu as pltpu

MESH = pl.DeviceIdType.MESH
N_DEV = 8

D_MODEL = 1024
DEPTH = 2
N_BRANCH = 4
BRANCH_W = 512
DN_HEADS = 4
CONV_W = 4
GM_GROUPS = 4
SW_HEADS = 8
EPS = 1e-6
NEG_INF = -1e30

D_IN = 9992
W_IN_SHARD = D_IN // N_DEV
W_IN_SHARD_PAD = 1280
D_IN_AL = 10752
DN_W, SW_W, GM_W, XM_W = 2560, 1536, 1536, 1024

ADAM_LR = 0.001
ADAM_B1 = 0.9
ADAM_B2 = 0.999
ADAM_EPS = 1e-08
ADAM_WD = 0.01
ADAM_STEP = 10

VMEM_LIMIT = 56 * 1024 * 1024

BF = jnp.bfloat16
F32 = jnp.float32
DN_C = 128
DN_D = 128
HALO = 8
BLK = 128


def _my_place():
    return lax.axis_index("x"), lax.axis_index("y"), lax.axis_index("c")


_ANY = pl.BlockSpec(memory_space=pl.ANY)


def _all_gather_slots(parts, name):
    n = len(parts)

    def body(*refs):
        p_refs, out_refs = refs[:n], refs[n:2 * n]
        send_sems, recv_sems, local_sems = refs[2 * n:]
        x, y, c = _my_place()
        me, sibling = (x, y, c), (x, y, 1 - c)
        chips = [(1 - x, y), (x, 1 - y), (1 - x, 1 - y)]

        def copy(a, k, block, to, src=None):
            px, py, pc = block
            slot = out_refs[a].at[4 * px + 2 * py + pc]
            return pltpu.make_async_remote_copy(
                src_ref=slot if src is None else src, dst_ref=slot,
                send_sem=send_sems.at[7 * a + k], recv_sem=recv_sems.at[7 * a + k],
                device_id=to, device_id_type=MESH)

        mine = [pltpu.make_async_copy(p_refs[a], out_refs[a].at[4 * x + 2 * y + c], local_sems.at[a])
                for a in range(n)]
        for cp in mine:
            cp.start()
        first = []
        for a in range(n):
            first.append(copy(a, 0, me, sibling, src=p_refs[a]))
            first += [copy(a, 1 + j, me, (*chip, c), src=p_refs[a]) for j, chip in enumerate(chips)]
        for cp in first:
            cp.start()
        passed = []
        for j, chip in enumerate(chips):
            for a in range(n):
                copy(a, 1 + j, (*chip, c), me).wait_recv()
                fwd = copy(a, 4 + j, (*chip, c), sibling)
                fwd.start()
                passed.append(fwd)
        for a in range(n):
            copy(a, 0, sibling, me).wait_recv()
            for j, chip in enumerate(chips):
                copy(a, 4 + j, (*chip, 1 - c), me).wait_recv()
        for cp in first + passed:
            cp.wait_send()
        for cp in mine:
            cp.wait()

    return pl.pallas_call(
        body, name=name,
        out_shape=[jax.ShapeDtypeStruct((N_DEV,) + p.shape, p.dtype) for p in parts],
        in_specs=[_ANY] * n, out_specs=[_ANY] * n,
        scratch_shapes=[pltpu.SemaphoreType.DMA((7 * n,)), pltpu.SemaphoreType.DMA((7 * n,)),
                        pltpu.SemaphoreType.DMA((n,))],
    )(*parts)


def _exchange_sibling(parts, name):
    n = len(parts)

    def body(*refs):
        g_refs, out_refs = refs[:n], refs[n:2 * n]
        send_sems, recv_sems = refs[2 * n:]
        x, y, c = _my_place()
        copies = [pltpu.make_async_remote_copy(
            src_ref=g_refs[a].at[:, 1 - c], dst_ref=out_refs[a],
            send_sem=send_sems.at[a], recv_sem=recv_sems.at[a],
            device_id=(x, y, 1 - c), device_id_type=MESH) for a in range(n)]
        for cp in copies:
            cp.start()
        for cp in copies:
            cp.wait()

    return pl.pallas_call(
        body, name=name,
        out_shape=[jax.ShapeDtypeStruct((4,) + g.shape[2:], g.dtype) for g in parts],
        in_specs=[_ANY] * n, out_specs=[_ANY] * n,
        scratch_shapes=[pltpu.SemaphoreType.DMA((n,)), pltpu.SemaphoreType.DMA((n,))],
    )(*parts)


_HBM = pl.BlockSpec(memory_space=pltpu.HBM)
_SEM = pl.BlockSpec(memory_space=pltpu.SEMAPHORE)
_EFFECT = pltpu.SideEffectType.DATAFLOW_SIDE_EFFECTING


def _peer(x, y, c, k):
    return (1 - x if (k >> 2) & 1 else x, 1 - y if (k >> 1) & 1 else y, 1 - c if k & 1 else c)


def _spread_start(srcs, mode, name):
    n = len(srcs)
    lands = [lax.empty((N_DEV,) + s.shape if mode == "gather" else s.shape, s.dtype) for s in srcs]
    peers = range(0, N_DEV, 2) if mode == "chips" else range(N_DEV)

    def body(*refs):
        src_refs, land_refs = refs[:n], refs[n:2 * n]
        send_sems, recv_sems = refs[2 * n:2 * n + 2]
        token = refs[-1]
        x, y, c = _my_place()
        for a in range(n):
            for k in peers:
                px, py, pc = _peer(x, y, c, k)
                if mode == "chips":
                    src, mine = src_refs[a].at[2 * px + py], 2 * x + y
                else:
                    src = src_refs[a].at[4 * px + 2 * py + pc] if mode == "scatter" else src_refs[a]
                    mine = 4 * x + 2 * y + c
                pltpu.make_async_remote_copy(
                    src_ref=src, dst_ref=land_refs[a].at[mine],
                    send_sem=send_sems.at[a], recv_sem=recv_sems.at[a],
                    device_id=(px, py, pc), device_id_type=MESH).start()
        token[...] = jnp.zeros_like(token)

    out = pl.pallas_call(
        body, name=name,
        out_shape=[pltpu.SemaphoreType.DMA((n,)), pltpu.SemaphoreType.DMA((n,))]
        + [pltpu.HBM(s.shape, s.dtype) for s in srcs] + [pltpu.HBM(l.shape, l.dtype) for l in lands]
        + [jax.ShapeDtypeStruct((8, 128), F32)],
        in_specs=[_HBM] * (2 * n),
        out_specs=[_SEM, _SEM] + [_HBM] * (2 * n) + [pl.BlockSpec(memory_space=pltpu.VMEM)],
        input_output_aliases={i: 2 + i for i in range(2 * n)},
        compiler_params=pltpu.CompilerParams(has_side_effects=_EFFECT),
    )(*[pltpu.with_memory_space_constraint(s, pltpu.HBM) for s in srcs],
      *[pltpu.with_memory_space_constraint(l, pltpu.HBM) for l in lands])
    return out[0], out[1], out[2:2 + n], out[2 + n:2 + 2 * n], out[-1]


def _spread_wait(send_sems, recv_sems, srcs, lands, which, after, name):
    n = len(srcs)

    def body(*refs):
        land_refs = refs[n:2 * n]
        send_sems, recv_sems = refs[2 * n:2 * n + 2]
        x, y, c = _my_place()
        for a in which:
            whole = pltpu.make_async_remote_copy(
                src_ref=land_refs[a], dst_ref=land_refs[a],
                send_sem=send_sems.at[a], recv_sem=recv_sems.at[a],
                device_id=(x, y, c), device_id_type=MESH)
            whole.wait_send()
            whole.wait_recv()

    out = pl.pallas_call(
        body, name=name,
        out_shape=[pltpu.HBM(s.shape, s.dtype) for s in srcs] + [pltpu.HBM(l.shape, l.dtype) for l in lands],
        in_specs=[_HBM] * (2 * n) + [_SEM, _SEM, _ANY],
        out_specs=[_HBM] * (2 * n),
        input_output_aliases={i: i for i in range(2 * n)},
        compiler_params=pltpu.CompilerParams(has_side_effects=_EFFECT),
    )(*srcs, *lands, send_sems, recv_sems, after)
    return out[:n], out[n:]


def _all_reduce_vmem(v, name):
    def body(v_ref, out_ref, buf, send_sems, recv_sems):
        x, y, c = _my_place()
        peers = [(x, y, 1 - c), (1 - x, y, c), (x, 1 - y, c)]
        out_ref[...] = v_ref[...]
        for step, peer in enumerate(peers):
            cp = pltpu.make_async_remote_copy(
                src_ref=out_ref, dst_ref=buf.at[step],
                send_sem=send_sems.at[step], recv_sem=recv_sems.at[step],
                device_id=peer, device_id_type=MESH)
            cp.start()
            cp.wait()
            out_ref[...] = out_ref[...] + buf[step]

    return pl.pallas_call(
        body, name=name,
        out_shape=jax.ShapeDtypeStruct(v.shape, v.dtype),
        in_specs=[pl.BlockSpec(memory_space=pltpu.VMEM)],
        out_specs=pl.BlockSpec(memory_space=pltpu.VMEM),
        scratch_shapes=[pltpu.VMEM((3,) + v.shape, v.dtype),
                        pltpu.SemaphoreType.DMA((3,)), pltpu.SemaphoreType.DMA((3,))],
    )(v)


def _pick(n, pref):
    if n <= pref:
        return n
    t = pref - pref % 128
    while t > 0 and n % t:
        t -= 128
    return t if t > 0 else n


_DIMS = {"nn": (((1,), (0,)), ((), ())),
         "nt": (((1,), (1,)), ((), ())),
         "tn": (((0,), (0,)), ((), ()))}


def _matmul(a, b, mode, out_dtype, tiles, name):
    (m, k) = a.shape
    n = b.shape[1] if mode == "nn" else b.shape[0]
    tm, tn, tk = (_pick(d, t) for d, t in zip((m, n, k), tiles))
    nk = k // tk

    def product(a_ref, b_ref):
        return lax.dot_general(a_ref[...].astype(BF), b_ref[...].astype(BF), _DIMS[mode], preferred_element_type=F32)

    def body_whole_k(a_ref, b_ref, o_ref):
        o_ref[...] = product(a_ref, b_ref).astype(o_ref.dtype)

    def body_split_k(a_ref, b_ref, o_ref, acc_ref):
        kk = pl.program_id(2)

        @pl.when(kk == 0)
        def _():
            acc_ref[...] = jnp.zeros_like(acc_ref)

        acc_ref[...] += product(a_ref, b_ref)

        @pl.when(kk == nk - 1)
        def _():
            o_ref[...] = acc_ref[...].astype(o_ref.dtype)

    b_spec = (pl.BlockSpec((tn, tk), lambda i, j, kk: (j, kk)) if mode == "nt"
              else pl.BlockSpec((tk, tn), lambda i, j, kk: (kk, j)))
    return pl.pallas_call(
        body_whole_k if nk == 1 else body_split_k, name=name,
        out_shape=jax.ShapeDtypeStruct((m, n), out_dtype),
        grid=(m // tm, n // tn, nk),
        in_specs=[pl.BlockSpec((tm, tk), lambda i, j, kk: (i, kk)), b_spec],
        out_specs=pl.BlockSpec((tm, tn), lambda i, j, kk: (i, j)),
        scratch_shapes=[] if nk == 1 else [pltpu.VMEM((tm, tn), F32)],
        compiler_params=pltpu.CompilerParams(
            dimension_semantics=("parallel", "parallel", "arbitrary"),
            vmem_limit_bytes=VMEM_LIMIT),
    )(a, b)


def _rows2d(t, lead):
    return t.reshape(t.shape[:lead] + (math.prod(t.shape[lead:-1]), t.shape[-1]))


def _pair_sum(g, theirs, name):
    g3, t3 = _rows2d(g, 2), _rows2d(theirs, 1)
    _, r, w = t3.shape
    tr = _pick(r, 512)

    def body(g_ref, t_ref, o_ref):
        c = lax.axis_index("c")
        mine = jnp.where(c == 0, g_ref[0, 0], g_ref[0, 1])
        o_ref[0] = (mine.astype(F32) + t_ref[0].astype(F32)).astype(o_ref.dtype)

    out = pl.pallas_call(
        body, name=name,
        out_shape=jax.ShapeDtypeStruct(t3.shape, t3.dtype),
        grid=(4, r // tr),
        in_specs=[pl.BlockSpec((1, 2, tr, w), lambda q, i: (q, 0, i, 0)),
                  pl.BlockSpec((1, tr, w), lambda q, i: (q, i, 0))],
        out_specs=pl.BlockSpec((1, tr, w), lambda q, i: (q, i, 0)),
        compiler_params=pltpu.CompilerParams(dimension_semantics=("parallel", "parallel")),
    )(g3, t3)
    return out.reshape(theirs.shape)


def _adam_update(w, g, m, v):
    c1 = 1.0 - ADAM_B1 ** ADAM_STEP
    c2 = 1.0 - ADAM_B2 ** ADAM_STEP
    nm = ADAM_B1 * m + (1.0 - ADAM_B1) * g
    nv = ADAM_B2 * v + (1.0 - ADAM_B2) * (g * g)
    delta = -ADAM_LR * ((nm / c1) / (jnp.sqrt(nv / c2) + ADAM_EPS) + ADAM_WD * w)
    return delta, nm, nv


def _sum_adamw(parts, w, m, v, layer, carry, after, name):
    shape = w.shape
    cols = shape[-1]
    p3 = _rows2d(parts, 1)
    w3, m3, v3 = (_rows2d(t, 1) for t in (w, m, v))
    rows = w3.shape[1]
    tr = _pick(rows, 128)
    n_parts = p3.shape[0]

    def body(p_ref, w_ref, m_ref, v_ref, *rest):
        g_ref, d_ref, nm_ref, nv_ref = rest[-4:]
        g = p_ref[0, :, :cols].astype(F32)
        for q in range(1, n_parts):
            g = g + p_ref[q, :, :cols].astype(F32)
        d, nm, nv = _adam_update(w_ref[0], g, m_ref[0], v_ref[0])
        g_ref[0] = g
        d_ref[0] = d
        nm_ref[0] = nm
        nv_ref[0] = nv

    spec = pl.BlockSpec((1, tr, cols), lambda i: (layer, i, 0))
    extra = [] if carry is None else [_rows2d(t, 1) for t in carry]
    tail = [] if after is None else [after]
    out = pl.pallas_call(
        body, name=name,
        out_shape=[jax.ShapeDtypeStruct(w3.shape, F32)] * 4,
        grid=(rows // tr,),
        in_specs=[pl.BlockSpec((n_parts, tr, p3.shape[-1]), lambda i: (0, i, 0)), spec, spec, spec] + [_ANY] * len(extra + tail),
        out_specs=[spec] * 4,
        input_output_aliases={4 + i: i for i in range(len(extra))},
        compiler_params=pltpu.CompilerParams(dimension_semantics=("parallel",)),
    )(p3, w3, m3, v3, *extra, *tail)
    return tuple(t.reshape(shape) for t in out)


def _adamw(w, g, m, v, name):
    rows, cols = w.shape
    tr = _pick(rows, 128)

    def body(w_ref, g_ref, m_ref, v_ref, d_ref, nm_ref, nv_ref):
        d, nm, nv = _adam_update(w_ref[...], g_ref[...], m_ref[...], v_ref[...])
        d_ref[...] = d
        nm_ref[...] = nm
        nv_ref[...] = nv

    spec = pl.BlockSpec((tr, cols), lambda i: (i, 0))
    return pl.pallas_call(
        body, name=name,
        out_shape=[jax.ShapeDtypeStruct((rows, cols), F32)] * 3,
        grid=(rows // tr,),
        in_specs=[spec] * 4, out_specs=[spec] * 3,
        compiler_params=pltpu.CompilerParams(dimension_semantics=("parallel",)),
    )(w, g, m, v)


_VJP = {"nn": (("nt", "gb"), ("tn", "ag")),
        "nt": (("nn", "gb"), ("tn", "ga")),
        "tn": (("nt", "bg"), ("nn", "ag"))}


def _make_dot(cast, precision):
    def raw(mode, a, b):
        return lax.dot_general(cast(a), cast(b), _DIMS[mode], precision=precision,
                               preferred_element_type=F32)

    @functools.partial(jax.custom_vjp, nondiff_argnums=(0,))
    def dot(mode, a, b):
        return raw(mode, a, b)

    def fwd(mode, a, b):
        return raw(mode, a, b), (a, b)

    def bwd(mode, res, g):
        a, b = res
        pick = {"a": a, "b": b, "g": g}
        (ma, ta), (mb, tb) = _VJP[mode]
        return dot(ma, pick[ta[0]], pick[ta[1]]), dot(mb, pick[tb[0]], pick[tb[1]])

    dot.defvjp(fwd, bwd)
    return dot


bdot = _make_dot(lambda t: t.astype(BF), None)
hdot = _make_dot(lambda t: t, lax.Precision.HIGHEST)


def _xdot(mode, a, b):
    return lax.dot_general(a, b, _DIMS[mode], precision=lax.Precision.HIGH, preferred_element_type=F32)


def _unit_lower_inverse(Ls):
    n = Ls[0].shape[0]
    batched = (((2,), (1,)), ((0,), (0,)))
    mm = lambda a, b: lax.dot_general(a, b, batched, precision=lax.Precision.HIGH, preferred_element_type=F32)
    eye = (lax.broadcasted_iota(jnp.int32, (n, n), 0) == lax.broadcasted_iota(jnp.int32, (n, n), 1)).astype(F32)
    p = jnp.stack(Ls)
    t_inv = eye[None] - p
    for _ in range(6):
        p = mm(p, p)
        t_inv = t_inv + mm(t_inv, p)
    return [t_inv[h] for h in range(len(Ls))]


@jax.custom_vjp
def _tri_solve(L, rhs, t_inv):
    return _xdot("nn", t_inv, rhs)


def _tri_solve_fwd(L, rhs, t_inv):
    sol = _xdot("nn", t_inv, rhs)
    return sol, (t_inv, sol)


def _tri_solve_bwd(res, dsol):
    t_inv, sol = res
    drhs = _xdot("tn", t_inv, dsol)
    return -_xdot("nt", drhs, sol), drhs, jnp.zeros_like(t_inv)


_tri_solve.defvjp(_tri_solve_fwd, _tri_solve_bwd)


def _sigmoid(x):
    return 1.0 / (1.0 + jnp.exp(-x))


def _softplus(x):
    return jnp.maximum(x, 0.0) + jnp.log(1.0 + jnp.exp(-jnp.abs(x)))


def _dn_chunk(S, xs, ba, z, cw, al, dt, dn, t_saved=None):
    C = DN_C
    pre = xs[0] * cw[0] + xs[1] * cw[1] + xs[2] * cw[2] + xs[3] * cw[3]
    qkv = pre * _sigmoid(pre)
    lane = lax.broadcasted_iota(jnp.int32, (1, 128), 1)
    sub = lax.broadcasted_iota(jnp.int32, (C, 1), 0)
    row_i = lax.broadcasted_iota(jnp.int32, (C, C), 0)
    col_i = lax.broadcasted_iota(jnp.int32, (C, C), 1)
    strict = row_i > col_i
    incl = row_i >= col_i
    g_all = jnp.where((lane >= 4) & (lane < 8), -jnp.exp(al) * _softplus(ba + dt), 0.0)
    gc_all = hdot("nn", incl.astype(F32), g_all)
    gc_all_t = gc_all.T
    beta_all = _sigmoid(ba)
    glast_all = jnp.sum(jnp.where(sub == C - 1, gc_all, 0.0), axis=0, keepdims=True)
    heads = []
    for h in range(DN_HEADS):
        q = qkv[:, 128 * h:128 * (h + 1)]
        k = qkv[:, 512 + 128 * h:512 + 128 * (h + 1)]
        v = qkv[:, 1024 + 128 * h:1024 + 128 * (h + 1)]
        q = q * lax.rsqrt(jnp.sum(q * q, axis=1, keepdims=True) + EPS) * (DN_D ** -0.5)
        k = k * lax.rsqrt(jnp.sum(k * k, axis=1, keepdims=True) + EPS)
        beta = jnp.sum(jnp.where(lane == h, beta_all, 0.0), axis=1, keepdims=True)
        gc = jnp.sum(jnp.where(lane == 4 + h, gc_all, 0.0), axis=1, keepdims=True)
        gc_row = jnp.sum(jnp.where(sub == 4 + h, gc_all_t, 0.0), axis=0, keepdims=True)
        g_last = jnp.sum(jnp.where(lane == 4 + h, glast_all, 0.0), axis=1, keepdims=True)
        diff = gc - gc_row
        kb = k * beta
        L = jnp.where(strict, bdot("nt", kb, k) * jnp.exp(jnp.where(strict, diff, 0.0)), 0.0)
        heads.append((q, k, v, beta, gc, g_last, diff, kb, L))
    t_invs = _unit_lower_inverse([hd[-1] for hd in heads]) if t_saved is None else t_saved
    ys, s_new = [], []
    for h, (q, k, v, beta, gc, g_last, diff, kb, L) in enumerate(heads):
        sol = _tri_solve(L, jnp.concatenate([v * beta, kb * jnp.exp(gc)], axis=1), t_invs[h])
        u, w = sol[:, :DN_D], sol[:, DN_D:]
        a_qk = jnp.where(incl, bdot("nt", q, k) * jnp.exp(jnp.where(incl, diff, 0.0)), 0.0)
        qg = q * jnp.exp(gc)
        kd = k * jnp.exp(g_last - gc)
        v_new = u - bdot("nn", w, S[h])
        o = bdot("nn", qg, S[h]) + bdot("nn", a_qk, v_new)
        s_new.append(S[h] * jnp.exp(g_last) + bdot("tn", kd, v_new))
        o = o * lax.rsqrt(jnp.mean(o * o, axis=1, keepdims=True) + EPS) * dn
        zh = z[:, 128 * h:128 * (h + 1)]
        ys.append(o * (zh * _sigmoid(zh)))
    return jnp.concatenate(ys, axis=1), tuple(s_new), tuple(t_invs)


def _load_shifted(xbuf, x_ref, halo_ref, first):
    xbuf[0:HALO, :] = jnp.where(first, 0.0, halo_ref[:, 0:1536])
    xbuf[HALO:HALO + DN_C, :] = x_ref[:, 0:1536]
    return [xbuf[HALO - 3 + k:HALO - 3 + k + DN_C, :] for k in range(4)]


def dn_forward(cols, jblk, cw, al, dt, dn, name):
    T = cols.shape[0]
    n = T // DN_C

    def body(x_ref, halo_ref, cw_ref, al_ref, dt_ref, dn_ref, y_ref, ss_ref, ts_ref, s_scr, xbuf):
        i = pl.program_id(0)

        @pl.when(i == 0)
        def _():
            s_scr[...] = jnp.zeros_like(s_scr)

        xs = _load_shifted(xbuf, x_ref, halo_ref, i == 0)
        ss_ref[0] = s_scr[...]
        S = [s_scr[h] for h in range(DN_HEADS)]
        cws = [cw_ref[k:k + 1, :] for k in range(4)]
        y, s_new, t_invs = _dn_chunk(S, xs, x_ref[:, 2048:2176], x_ref[:, 1536:2048], cws,
                                     al_ref[...], dt_ref[...], dn_ref[...])
        y_ref[...] = y
        for h in range(DN_HEADS):
            s_scr[h] = s_new[h]
            ts_ref[0, h] = t_invs[h]

    per = DN_C // HALO
    full = lambda shape: pl.BlockSpec(shape, lambda i: (0,) * len(shape))
    return pl.pallas_call(
        body, name=name,
        out_shape=[jax.ShapeDtypeStruct((T, 512), F32),
                   jax.ShapeDtypeStruct((n, DN_HEADS, DN_D, DN_D), F32),
                   jax.ShapeDtypeStruct((n, DN_HEADS, DN_D, DN_D), F32)],
        grid=(n,),
        in_specs=[pl.BlockSpec((DN_C, DN_W), lambda i: (i, jblk)),
                  pl.BlockSpec((HALO, DN_W), lambda i: (jnp.maximum(i * per - 1, 0), jblk)),
                  full((4, 1536)), full((1, 128)), full((1, 128)), full((1, 128))],
        out_specs=[pl.BlockSpec((DN_C, 512), lambda i: (i, 0)),
                   pl.BlockSpec((1, DN_HEADS, DN_D, DN_D), lambda i: (i, 0, 0, 0)),
                   pl.BlockSpec((1, DN_HEADS, DN_D, DN_D), lambda i: (i, 0, 0, 0))],
        scratch_shapes=[pltpu.VMEM((DN_HEADS, DN_D, DN_D), F32), pltpu.VMEM((HALO + DN_C, 1536), F32)],
        compiler_params=pltpu.CompilerParams(dimension_semantics=("arbitrary",)),
    )(cols, cols, cw, al, dt, dn)


def dn_backward(cols, jblk, cw, al, dt, dn, ss, ts, dy, dcols, name):
    T = cols.shape[0]
    n = T // DN_C

    def body(x_ref, halo_ref, cw_ref, al_ref, dt_ref, dn_ref, ss_ref, ts_ref, dy_ref, dcols_in,
             dx_ref, dcw_ref, dal_ref, ddt_ref, ddn_ref, ds_scr, xbuf, dbuf, carry):
        i = pl.program_id(0)

        @pl.when(i == 0)
        def _():
            ds_scr[...] = jnp.zeros_like(ds_scr)
            carry[...] = jnp.zeros_like(carry)
            dcw_ref[...] = jnp.zeros_like(dcw_ref)
            dal_ref[...] = jnp.zeros_like(dal_ref)
            ddt_ref[...] = jnp.zeros_like(ddt_ref)
            ddn_ref[...] = jnp.zeros_like(ddn_ref)

        xs = _load_shifted(xbuf, x_ref, halo_ref, i == n - 1)
        S = [ss_ref[0, h] for h in range(DN_HEADS)]
        cws = [cw_ref[k:k + 1, :] for k in range(4)]

        t_saved = [ts_ref[0, h] for h in range(DN_HEADS)]

        def f(S, xs, ba, z, cws, al, dt, dn):
            return _dn_chunk(S, xs, ba, z, cws, al, dt, dn, t_saved)[:2]

        _, vjp = jax.vjp(f, S, xs, x_ref[:, 2048:2176], x_ref[:, 1536:2048], cws, al_ref[...], dt_ref[...], dn_ref[...])
        dS, dxs, dba, dz, dcws, dal, ddt, ddn = vjp((dy_ref[...], tuple(ds_scr[h] for h in range(DN_HEADS))))
        for h in range(DN_HEADS):
            ds_scr[h] = dS[h]
        dbuf[...] = jnp.zeros_like(dbuf)
        for k in range(4):
            lo = HALO - 3 + k
            dbuf[lo:lo + DN_C, :] += dxs[k]
        dbuf[DN_C:DN_C + HALO, :] += carry[...]
        dx_ref[...] = jnp.concatenate([dbuf[HALO:HALO + DN_C, :], dz, dba,
                                       jnp.zeros((DN_C, DN_W - 2176), F32)], axis=1).astype(dx_ref.dtype)
        carry[...] = dbuf[0:HALO, :]
        for k in range(4):
            dcw_ref[k:k + 1, :] += dcws[k]
        dal_ref[...] += dal
        ddt_ref[...] += ddt
        ddn_ref[...] += ddn

    per = DN_C // HALO
    rev = lambda i: n - 1 - i
    full = lambda shape: pl.BlockSpec(shape, lambda i: (0,) * len(shape))
    return pl.pallas_call(
        body, name=name,
        out_shape=[jax.ShapeDtypeStruct(dcols.shape, dcols.dtype),jax.ShapeDtypeStruct((4, 1536), F32),
                   jax.ShapeDtypeStruct((1, 128), F32), jax.ShapeDtypeStruct((1, 128), F32),
                   jax.ShapeDtypeStruct((1, 128), F32)],
        grid=(n,),
        in_specs=[pl.BlockSpec((DN_C, DN_W), lambda i: (rev(i), jblk)),
                  pl.BlockSpec((HALO, DN_W), lambda i: (jnp.maximum(rev(i) * per - 1, 0), jblk)),
                  full((4, 1536)), full((1, 128)), full((1, 128)), full((1, 128)),
                  pl.BlockSpec((1, DN_HEADS, DN_D, DN_D), lambda i: (rev(i), 0, 0, 0)),
                  pl.BlockSpec((1, DN_HEADS, DN_D, DN_D), lambda i: (rev(i), 0, 0, 0)),
                  pl.BlockSpec((DN_C, 512), lambda i: (rev(i), 0)), _ANY],
        out_specs=[pl.BlockSpec((DN_C, DN_W), lambda i: (rev(i), jblk)),
                   full((4, 1536)), full((1, 128)), full((1, 128)), full((1, 128))],
        scratch_shapes=[pltpu.VMEM((DN_HEADS, DN_D, DN_D), F32), pltpu.VMEM((HALO + DN_C, 1536), F32),
                        pltpu.VMEM((HALO + DN_C, 1536), F32), pltpu.VMEM((HALO, 1536), F32)],
        input_output_aliases={9: 0},
        compiler_params=pltpu.CompilerParams(dimension_semantics=("arbitrary",)),
    )(cols, cols, cw, al, dt, dn, ss, ts, dy, dcols)


def _full(shape):
    return pl.BlockSpec(shape, lambda i: (0,) * len(shape))


def _silu(x):
    return x * _sigmoid(x)


def _gelu(x):
    return 0.5 * x * (1.0 + jnp.tanh(0.7978845608028654 * (x + 0.044715 * (x * x * x))))


def _lane_col(mat, idx):
    lane = lax.broadcasted_iota(jnp.int32, (1, mat.shape[1]), 1)
    return jnp.sum(jnp.where(lane == idx, mat, 0.0), axis=1, keepdims=True)


def _gm_chunk(uv, z, gain, ws, bt):
    g = _gelu(uv)
    u, v = g[:, :512], g[:, 512:]
    v = v * lax.rsqrt(jnp.mean(v * v, axis=1, keepdims=True) + EPS) * gain
    row_i = lax.broadcasted_iota(jnp.int32, (BLK, BLK), 0)
    col_i = lax.broadcasted_iota(jnp.int32, (BLK, BLK), 1)
    causal = row_i >= col_i
    ss = []
    for grp in range(4):
        wg = jnp.where(causal, ws[grp], 0.0)
        ss.append(bdot("nn", wg, v[:, BLK * grp:BLK * (grp + 1)]) + _lane_col(bt, grp))
    return u * jnp.concatenate(ss, axis=1) * _silu(z)


def gm_forward(cols, jblk, gain, ws, bt, name):
    T = cols.shape[0]

    def body(x_ref, gain_ref, ws_ref, bt_ref, y_ref):
        y_ref[...] = _gm_chunk(x_ref[:, 0:1024], x_ref[:, 1024:1536], gain_ref[...],
                               [ws_ref[g] for g in range(4)], bt_ref[...])

    return pl.pallas_call(
        body, name=name, out_shape=jax.ShapeDtypeStruct((T, 512), F32), grid=(T // BLK,),
        in_specs=[pl.BlockSpec((BLK, GM_W), lambda i: (i, jblk)),
                  _full((1, 512)), _full((4, BLK, BLK)), _full((BLK, BLK))],
        out_specs=pl.BlockSpec((BLK, 512), lambda i: (i, 0)),
        compiler_params=pltpu.CompilerParams(dimension_semantics=("parallel",)),
    )(cols, gain, ws, bt)


def gm_backward(cols, jblk, gain, ws, bt, dy, dcols, name):
    T = cols.shape[0]

    def body(x_ref, gain_ref, ws_ref, bt_ref, dy_ref, dcols_in, dx_ref, dgain_ref, dws_ref, dbt_ref):
        @pl.when(pl.program_id(0) == 0)
        def _():
            dgain_ref[...] = jnp.zeros_like(dgain_ref)
            dws_ref[...] = jnp.zeros_like(dws_ref)
            dbt_ref[...] = jnp.zeros_like(dbt_ref)

        _, vjp = jax.vjp(_gm_chunk, x_ref[:, 0:1024], x_ref[:, 1024:1536], gain_ref[...],
                         [ws_ref[g] for g in range(4)], bt_ref[...])
        duv, dz, dgain, dws, dbt = vjp(dy_ref[...])
        dx_ref[...] = jnp.concatenate([duv, dz], axis=1).astype(dx_ref.dtype)
        dgain_ref[...] += dgain
        for g in range(4):
            dws_ref[g] += dws[g]
        dbt_ref[...] += dbt

    return pl.pallas_call(
        body, name=name,
        out_shape=[jax.ShapeDtypeStruct(dcols.shape, dcols.dtype),jax.ShapeDtypeStruct((1, 512), F32),
                   jax.ShapeDtypeStruct((4, BLK, BLK), F32), jax.ShapeDtypeStruct((BLK, BLK), F32)],
        grid=(T // BLK,),
        in_specs=[pl.BlockSpec((BLK, GM_W), lambda i: (i, jblk)),
                  _full((1, 512)), _full((4, BLK, BLK)), _full((BLK, BLK)),
                  pl.BlockSpec((BLK, 512), lambda i: (i, 0)), _ANY],
        out_specs=[pl.BlockSpec((BLK, GM_W), lambda i: (i, jblk)),
                   _full((1, 512)), _full((4, BLK, BLK)), _full((BLK, BLK))],
        input_output_aliases={5: 0},
        compiler_params=pltpu.CompilerParams(dimension_semantics=("arbitrary",)),
    )(cols, gain, ws, bt, dy, dcols)


def _sw_block(first, q, kp, kc, vp, vc, z, sinks):
    P = BLK
    lane = lax.broadcasted_iota(jnp.int32, (1, 128), 1)
    r = lax.broadcasted_iota(jnp.int32, (128, 128), 0)
    c = lax.broadcasted_iota(jnp.int32, (128, 128), 1)
    swap = (c == (r + 64) % 128).astype(F32)
    k2 = jnp.concatenate([kp, kc], axis=0)
    v2 = jnp.concatenate([vp, vc], axis=0)
    k2s = bdot("nn", k2, swap)
    v2s = bdot("nn", v2, swap)
    qi = lax.broadcasted_iota(jnp.int32, (P, 2 * P), 0)
    kj = lax.broadcasted_iota(jnp.int32, (P, 2 * P), 1)
    dist = qi + P - kj
    valid = (dist >= 0) & (dist < P) & ((kj >= P) | jnp.logical_not(first))
    outs = []
    for j in range(4):
        acc = jnp.zeros((P, 128), F32)
        for half in range(2):
            h = 2 * j + half
            kv = h // 4
            in_half = (lane >= 64 * half) & (lane < 64 * half + 64)
            qh = jnp.where(in_half, q[:, 128 * j:128 * (j + 1)], 0.0)
            same = (half == kv)
            s = bdot("nt", qh, k2 if same else k2s) * (64 ** -0.5)
            s = jnp.where(valid, s, NEG_INF)
            sink = _lane_col(sinks, h)
            m = lax.stop_gradient(jnp.maximum(jnp.max(s, axis=1, keepdims=True), sink))
            e = jnp.exp(s - m)
            p = e / (jnp.sum(e, axis=1, keepdims=True) + jnp.exp(sink - m))
            o = bdot("nn", p, v2 if same else v2s)
            acc = acc + jnp.where(in_half, o, 0.0)
        outs.append(acc)
    return jnp.concatenate(outs, axis=1) * _silu(z)


def _sw_specs(jblk, idx):
    prev = lambda i: jnp.maximum(idx(i) - 1, 0)
    jk = (jblk * SW_W + 1024) // 128
    return [pl.BlockSpec((BLK, SW_W), lambda i: (idx(i), jblk)),
            pl.BlockSpec((BLK, 128), lambda i: (prev(i), jk)),
            pl.BlockSpec((BLK, 128), lambda i: (prev(i), jk + 1)), _full((1, 128))]


def sw_forward(cols, jblk, sinks, name):
    T = cols.shape[0]

    def body(x_ref, kp_ref, vp_ref, s_ref, y_ref):
        y_ref[...] = _sw_block(pl.program_id(0) == 0, x_ref[:, 0:512], kp_ref[...], x_ref[:, 1024:1152],
                               vp_ref[...], x_ref[:, 1152:1280], x_ref[:, 512:1024], s_ref[...])

    return pl.pallas_call(
        body, name=name, out_shape=jax.ShapeDtypeStruct((T, 512), F32), grid=(T // BLK,),
        in_specs=_sw_specs(jblk, lambda i: i),
        out_specs=pl.BlockSpec((BLK, 512), lambda i: (i, 0)),
        compiler_params=pltpu.CompilerParams(dimension_semantics=("parallel",)),
    )(cols, cols, cols, sinks)


def sw_backward(cols, jblk, sinks, dy, dcols, name):
    T = cols.shape[0]
    n = T // BLK
    rev = lambda i: n - 1 - i

    def body(x_ref, kp_ref, vp_ref, s_ref, dy_ref, dcols_in, dx_ref, ds_ref, kcarry, vcarry):
        i = pl.program_id(0)

        @pl.when(i == 0)
        def _():
            kcarry[...] = jnp.zeros_like(kcarry)
            vcarry[...] = jnp.zeros_like(vcarry)
            ds_ref[...] = jnp.zeros_like(ds_ref)

        f = functools.partial(_sw_block, i == n - 1)
        _, vjp = jax.vjp(f, x_ref[:, 0:512], kp_ref[...], x_ref[:, 1024:1152], vp_ref[...], x_ref[:, 1152:1280],
                         x_ref[:, 512:1024], s_ref[...])
        dq, dkp, dkc, dvp, dvc, dz, dsk = vjp(dy_ref[...])
        dx_ref[...] = jnp.concatenate([dq, dz, dkc + kcarry[...], dvc + vcarry[...],
                                       jnp.zeros((BLK, SW_W - 1280), F32)], axis=1).astype(dx_ref.dtype)
        kcarry[...] = dkp
        vcarry[...] = dvp
        ds_ref[...] += dsk

    return pl.pallas_call(
        body, name=name,
        out_shape=[jax.ShapeDtypeStruct(dcols.shape, dcols.dtype),jax.ShapeDtypeStruct((1, 128), F32)],
        grid=(n,),
        in_specs=_sw_specs(jblk, rev) + [pl.BlockSpec((BLK, 512), lambda i: (rev(i), 0)), _ANY],
        out_specs=[pl.BlockSpec((BLK, SW_W), lambda i: (rev(i), jblk)), _full((1, 128))],
        scratch_shapes=[pltpu.VMEM((BLK, 128), F32), pltpu.VMEM((BLK, 128), F32)],
        input_output_aliases={5: 0},
        compiler_params=pltpu.CompilerParams(dimension_semantics=("arbitrary",)),
    )(cols, cols, cols, sinks, dy, dcols)


XM_TQ = 256


def _xm_block(q, z, mkv):
    outs = []
    for h in range(4):
        s = bdot("nt", q[:, 128 * h:128 * (h + 1)], mkv[:, 128 * h:128 * (h + 1)]) * (128 ** -0.5)
        m = lax.stop_gradient(jnp.max(s, axis=1, keepdims=True))
        e = jnp.exp(s - m)
        p = e / jnp.sum(e, axis=1, keepdims=True)
        outs.append(bdot("nn", p, mkv[:, 512 + 128 * h:512 + 128 * (h + 1)]))
    return jnp.concatenate(outs, axis=1) * _silu(z)


def xm_forward(cols, jblk, mkv, name):
    T = cols.shape[0]

    def body(x_ref, m_ref, y_ref):
        y_ref[...] = _xm_block(x_ref[:, 0:512], x_ref[:, 512:1024], m_ref[...])

    return pl.pallas_call(
        body, name=name, out_shape=jax.ShapeDtypeStruct((T, 512), F32), grid=(T // XM_TQ,),
        in_specs=[pl.BlockSpec((XM_TQ, XM_W), lambda i: (i, jblk)), _full(mkv.shape)],
        out_specs=pl.BlockSpec((XM_TQ, 512), lambda i: (i, 0)),
        compiler_params=pltpu.CompilerParams(dimension_semantics=("parallel",)),
    )(cols, mkv)


def xm_backward(cols, jblk, mkv, dy, dcols, name):
    T = cols.shape[0]

    def body(x_ref, m_ref, dy_ref, dcols_in, dx_ref, dm_ref):
        @pl.when(pl.program_id(0) == 0)
        def _():
            dm_ref[...] = jnp.zeros_like(dm_ref)

        _, vjp = jax.vjp(_xm_block, x_ref[:, 0:512], x_ref[:, 512:1024], m_ref[...])
        dq, dz, dm = vjp(dy_ref[...])
        dx_ref[...] = jnp.concatenate([dq, dz], axis=1).astype(dx_ref.dtype)
        dm_ref[...] += dm

    return pl.pallas_call(
        body, name=name,
        out_shape=[jax.ShapeDtypeStruct(dcols.shape, dcols.dtype),jax.ShapeDtypeStruct(mkv.shape, F32)],
        grid=(T // XM_TQ,),
        in_specs=[pl.BlockSpec((XM_TQ, XM_W), lambda i: (i, jblk)), _full(mkv.shape),
                  pl.BlockSpec((XM_TQ, 512), lambda i: (i, 0)), _ANY],
        out_specs=[pl.BlockSpec((XM_TQ, XM_W), lambda i: (i, jblk)), _full(mkv.shape)],
        input_output_aliases={3: 0},
        compiler_params=pltpu.CompilerParams(dimension_semantics=("arbitrary",)),
    )(cols, mkv, dy, dcols)


def _rms(x, gain):
    return x * lax.rsqrt(jnp.mean(x * x, axis=1, keepdims=True) + EPS) * gain


def memkv_forward(mem, gain, w, name):
    def body(m_ref, g_ref, w_ref, o_ref):
        o_ref[...] = bdot("nn", _rms(m_ref[...], g_ref[...]), w_ref[...])

    return pl.pallas_call(body, name=name, out_shape=jax.ShapeDtypeStruct(mem.shape, F32),
                          compiler_params=pltpu.CompilerParams(vmem_limit_bytes=VMEM_LIMIT))(mem, gain, w)


def memkv_backward(mem, gain, w, dkv, name):
    def body(m_ref, g_ref, w_ref, d_ref, dg_ref, dw_ref):
        mem_v = m_ref[...]
        _, vjp = jax.vjp(lambda g, ww: bdot("nn", _rms(mem_v, g), ww), g_ref[...], w_ref[...].astype(F32))
        dg, dw = vjp(d_ref[...])
        dg_ref[...] = dg
        dw_ref[...] = dw

    return pl.pallas_call(body, name=name,
                          out_shape=[jax.ShapeDtypeStruct(gain.shape, F32), jax.ShapeDtypeStruct(w.shape, F32)],
                          compiler_params=pltpu.CompilerParams(vmem_limit_bytes=VMEM_LIMIT))(mem, gain, w, dkv)


MG_TB = 128


def _merge_block(ys, gl, wup, wout, gpost):
    merged = None
    for n in range(4):
        t = _sigmoid(gl[:, 1024 * n:1024 * (n + 1)]) * bdot("nn", ys[n], wup[n])
        merged = t if merged is None else merged + t
    out = bdot("nn", merged, wout)
    return _rms(out, gpost)


def merge_forward(ys, cols, jgate, x, wup, wout, gpost, name):
    T = x.shape[0]
    TB = 256

    def body(ya, yb, yc, ym, gl_ref, x_ref, wup_ref, wout_ref, gp_ref, o_ref):
        upd = _merge_block([ya[...], yb[...], yc[...], ym[...]], gl_ref[...],
                           [wup_ref[n] for n in range(4)], wout_ref[...], gp_ref[...])
        o_ref[...] = x_ref[...] + upd

    yspec = pl.BlockSpec((TB, 512), lambda i: (i, 0))
    return pl.pallas_call(
        body, name=name, out_shape=jax.ShapeDtypeStruct((T, 1024), F32), grid=(T // TB,),
        in_specs=[yspec] * 4 + [pl.BlockSpec((TB, 4096), lambda i: (i, jgate)),
                                pl.BlockSpec((TB, 1024), lambda i: (i, 0)),
                                _full(wup.shape), _full(wout.shape), _full((1, 1024))],
        out_specs=pl.BlockSpec((TB, 1024), lambda i: (i, 0)),
        compiler_params=pltpu.CompilerParams(dimension_semantics=("parallel",), vmem_limit_bytes=VMEM_LIMIT),
    )(*ys, cols, x, wup, wout, gpost)


def merge_backward(ys, cols, jgate, wup, wout, gpost, dx, name):
    T = dx.shape[0]
    TB = MG_TB

    def body(ya, yb, yc, ym, gl_ref, wup_ref, wout_ref, gp_ref, dx_ref,
             dgl_ref, dya, dyb, dyc, dym, dwup_ref, dwout_ref, dgp_ref):
        @pl.when(pl.program_id(0) == 0)
        def _():
            dwup_ref[...] = jnp.zeros_like(dwup_ref)
            dwout_ref[...] = jnp.zeros_like(dwout_ref)
            dgp_ref[...] = jnp.zeros_like(dgp_ref)

        _, vjp = jax.vjp(_merge_block, [ya[...], yb[...], yc[...], ym[...]], gl_ref[...],
                         [wup_ref[n].astype(F32) for n in range(4)], wout_ref[...].astype(F32), gp_ref[...])
        dys, dgl, dwup, dwout, dgp = vjp(dx_ref[...])
        for ref, val in zip((dya, dyb, dyc, dym), dys):
            ref[...] = val
        dgl_ref[...] = dgl.astype(dgl_ref.dtype)
        for n in range(4):
            dwup_ref[n] += dwup[n]
        dwout_ref[...] += dwout
        dgp_ref[...] += dgp

    yspec = pl.BlockSpec((TB, 512), lambda i: (i, 0))
    return pl.pallas_call(
        body, name=name,
        out_shape=[jax.ShapeDtypeStruct(cols.shape, BF)] + [jax.ShapeDtypeStruct((T, 512), F32)] * 4 + [
            jax.ShapeDtypeStruct(wup.shape, F32), jax.ShapeDtypeStruct(wout.shape, F32),
            jax.ShapeDtypeStruct((1, 1024), F32)],
        grid=(T // TB,),
        in_specs=[yspec] * 4 + [pl.BlockSpec((TB, 4096), lambda i: (i, jgate)),
                                _full(wup.shape), _full(wout.shape), _full((1, 1024)),
                                pl.BlockSpec((TB, 1024), lambda i: (i, 0))],
        out_specs=[pl.BlockSpec((TB, 4096), lambda i: (i, jgate))] + [yspec] * 4 + [
            _full(wup.shape), _full(wout.shape), _full((1, 1024))],
        compiler_params=pltpu.CompilerParams(dimension_semantics=("arbitrary",), vmem_limit_bytes=VMEM_LIMIT),
    )(*ys, cols, wup, wout, gpost, dx)


NB = 256


def prenorm_forward(x, gain, name):
    T, D = x.shape

    def body(x_ref, g_ref, o_ref, ot_ref):
        h = _rms(x_ref[...], g_ref[...])
        o_ref[...] = h.astype(BF)
        ot_ref[...] = h.T.astype(BF)

    return pl.pallas_call(
        body, name=name,
        out_shape=[jax.ShapeDtypeStruct((T, D), BF), jax.ShapeDtypeStruct((D, T), BF)], grid=(T // NB,),
        in_specs=[pl.BlockSpec((NB, D), lambda i: (i, 0)), _full((1, D))],
        out_specs=[pl.BlockSpec((NB, D), lambda i: (i, 0)), pl.BlockSpec((D, NB), lambda i: (0, i))],
        compiler_params=pltpu.CompilerParams(dimension_semantics=("parallel",)),
    )(x, gain)


def prenorm_backward(x, gain, dh, dres, name):
    T = x.shape[0]

    def body(x_ref, g_ref, dh_ref, dr_ref, dx_ref, dg_ref):
        @pl.when(pl.program_id(0) == 0)
        def _():
            dg_ref[...] = jnp.zeros_like(dg_ref)

        _, vjp = jax.vjp(_rms, x_ref[...], g_ref[...])
        dxn, dg = vjp(dh_ref[...])
        dx_ref[...] = dr_ref[...] + dxn
        dg_ref[...] += dg

    spec = pl.BlockSpec((NB, 1024), lambda i: (i, 0))
    return pl.pallas_call(
        body, name=name,
        out_shape=[jax.ShapeDtypeStruct(x.shape, F32), jax.ShapeDtypeStruct((1, 1024), F32)], grid=(T // NB,),
        in_specs=[spec, _full((1, 1024)), spec, spec], out_specs=[spec, _full((1, 1024))],
        compiler_params=pltpu.CompilerParams(dimension_semantics=("arbitrary",)),
    )(x, gain, dh, dres)


def loss_head(y, target, name):
    T, D = y.shape

    def body(y_ref, t_ref, l_ref, d_ref):
        @pl.when(pl.program_id(0) == 0)
        def _():
            l_ref[...] = jnp.zeros_like(l_ref)

        err = y_ref[...] - t_ref[...]
        d_ref[...] = err * (1.0 / D)
        l_ref[...] += jnp.full(l_ref.shape, 0.5 * jnp.sum(jnp.mean(err * err, axis=1, keepdims=True)), F32)

    spec = pl.BlockSpec((NB, D), lambda i: (i, 0))
    return pl.pallas_call(
        body, name=name,
        out_shape=[jax.ShapeDtypeStruct((1, 128), F32), jax.ShapeDtypeStruct(y.shape, F32)], grid=(T // NB,),
        in_specs=[spec, spec], out_specs=[_full((1, 128)), spec],
        compiler_params=pltpu.CompilerParams(dimension_semantics=("arbitrary",)),
    )(y, target)


JB_GATE, JB_XM, JB_DN, JB_SW, JB_GM = 0, 4, 2, 5, 6
_ALIGNED_PIECES = ((5896, 4096), (4872, 512), (5384, 512), (0, 2048), (2048, 8), 504, (3592, 512), (4360, 512),
                   (4104, 128), (4232, 128), 256, (2056, 1024), (3080, 512))
_NATURAL_FROM_ALIGNED = ((5120, 2048), (7168, 8), (9216, 1024), (10240, 512), (7680, 512), (8704, 128), (8832, 128),
                         (8192, 512), (4096, 512), (4608, 512), (0, 4096))


def _natural_range(slots, start, width):
    out = []
    while width > 0:
        j, i = divmod(start, W_IN_SHARD)
        take = min(width, W_IN_SHARD - i)
        out.append(slots[j, :, i:i + take])
        start, width = start + take, width - take
    return out


def _aligned_w_in(slots):
    parts = []
    for piece in _ALIGNED_PIECES:
        if isinstance(piece, int):
            parts.append(jnp.zeros(slots.shape[1:2] + (piece,), slots.dtype))
        else:
            parts += _natural_range(slots, *piece)
    return jnp.concatenate(parts, axis=-1)


def _slots_of_aligned(d_al):
    slots = []
    for s in range(N_DEV):
        lo, hi = s * W_IN_SHARD, (s + 1) * W_IN_SHARD
        parts, nat = [], 0
        for a_start, width in _NATURAL_FROM_ALIGNED:
            b, e = max(lo, nat), min(hi, nat + width)
            if b < e:
                parts.append(d_al[..., a_start + b - nat:a_start + e - nat])
            nat += width
        parts.append(jnp.zeros(d_al.shape[:1] + (W_IN_SHARD_PAD - W_IN_SHARD,), d_al.dtype))
        slots.append(jnp.concatenate(parts, axis=-1))
    return jnp.stack(slots)


SMALL_VEC_W = 1024


def _pack_small(parts):
    rows = []
    for p in parts:
        flat = p.reshape(-1).astype(F32)
        r = -(-flat.shape[0] // SMALL_VEC_W)
        rows.append(jnp.pad(flat, (0, r * SMALL_VEC_W - flat.shape[0])).reshape(r, SMALL_VEC_W))
    vec = jnp.concatenate(rows, axis=0)
    return jnp.pad(vec, ((0, -vec.shape[0] % 8), (0, 0)))


def _unpack_small(vec, shapes):
    out, off = [], 0
    for s in shapes:
        n = math.prod(s)
        r = -(-n // SMALL_VEC_W)
        out.append(vec[off:off + r].reshape(-1)[:n].reshape(s))
        off += r
    return out


def _lanes(vec, at):
    return jnp.zeros((1, 128), F32).at[0, at:at + vec.shape[0]].set(vec)


SMALL_NAMES = ("norm_pre", "norm_post", "norm_mem", "a_log", "dt_bias", "dn_norm", "gm_norm",
               "spatial_w", "spatial_b", "sinks")


def _other_weights(s_mem, s_up, s_out):
    return (s_mem.reshape(D_MODEL, 2 * BRANCH_W),
            jnp.transpose(s_up, (1, 2, 0, 3)).reshape(N_BRANCH, BRANCH_W, D_MODEL), s_out.reshape(D_MODEL, D_MODEL))


def _grad_slots(d_in_al, d_mem, d_up, d_out):
    return [_slots_of_aligned(d_in_al), d_mem.astype(BF).reshape(N_DEV, 128, 2 * BRANCH_W),
            jnp.transpose(d_up.astype(BF).reshape(N_BRANCH, BRANCH_W, N_DEV, 128), (2, 0, 1, 3)),
            d_out.astype(BF).reshape(N_DEV, 128, D_MODEL)]


def _layer_params(l, small, conv_full, token):
    return dict(
        gpre=small["norm_pre"][l][None] + token, gpost=small["norm_post"][l][None], gmem=small["norm_mem"][l][None],
        cw=conv_full[l], al=_lanes(small["a_log"][l], 4), dt=_lanes(small["dt_bias"][l], 4),
        dnn=small["dn_norm"][l][None], gain=small["gm_norm"][l][None], ws=small["spatial_w"][l],
        bt=jnp.zeros((128, 128), F32).at[:, :GM_GROUPS].set(small["spatial_b"][l].T),
        sinks=_lanes(small["sinks"][l], 0))


def _layer_forward(l, xl, mem, p, w_in_al, other_weights):
    t = "l%d_" % l
    h, h_t = prenorm_forward(xl, p["gpre"], t + "prenorm")
    cols = _matmul(h, w_in_al, "nn", F32, (1024, 1536, 1024), t + "w_in")
    w_mem, w_up, w_out = other_weights(cols)
    mkv = memkv_forward(mem, p["gmem"], w_mem, t + "memkv")
    ya, ss, ts = dn_forward(cols, JB_DN, p["cw"], p["al"], p["dt"], p["dnn"], t + "deltanet")
    yb = gm_forward(cols, JB_GM, p["gain"], p["ws"], p["bt"], t + "gmlp")
    yc = sw_forward(cols, JB_SW, p["sinks"], t + "swa")
    ym = xm_forward(cols, JB_XM, mkv, t + "memattn")
    xn = merge_forward([ya, yb, yc, ym], cols, JB_GATE, xl, w_up, w_out, p["gpost"], t + "merge")
    return xn, dict(p, x=xl, h_t=h_t, cols=cols, mkv=mkv, ss=ss, ts=ts, ys=[ya, yb, yc, ym]), (w_in_al, w_mem, w_up, w_out)


def _layer_backward(l, s, mem, weights, dx, token):
    w_in_al, w_mem, w_up, w_out = weights
    t = "l%d_" % l
    cols = s["cols"]
    dcols, dya, dyb, dyc, dym, dwup, dwout, dgpost = merge_backward(
        s["ys"], cols, JB_GATE, w_up, w_out, s["gpost"] + token, dx, t + "merge_bwd")
    dcols, dmkv = xm_backward(cols, JB_XM, s["mkv"], dym, dcols, t + "memattn_bwd")
    dgmem, dwmem = memkv_backward(mem, s["gmem"], w_mem, dmkv, t + "memkv_bwd")
    dcols, dsinks = sw_backward(cols, JB_SW, s["sinks"], dyc, dcols, t + "swa_bwd")
    dcols, dgain, dws, dbt = gm_backward(cols, JB_GM, s["gain"], s["ws"], s["bt"], dyb, dcols, t + "gmlp_bwd")
    dcols, dcw, dal, ddt, ddn = dn_backward(
        cols, JB_DN, s["cw"], s["al"], s["dt"], s["dnn"], s["ss"], s["ts"], dya, dcols, t + "deltanet_bwd")
    dh = _matmul(dcols, w_in_al, "nt", F32, (1024, 1024, 1024), t + "w_in_bwd_x")
    dwin = _matmul(s["h_t"], dcols, "nn", BF, (1024, 1536, 2048), t + "w_in_bwd_w")
    dx, dgpre = prenorm_backward(s["x"], s["gpre"], dh, dx, t + "prenorm_bwd")
    gsmall = dict(norm_pre=dgpre[0], norm_post=dgpost[0], norm_mem=dgmem[0], a_log=dal[0, 4:8], dt_bias=ddt[0, 4:8],
                  dn_norm=ddn[0], gm_norm=dgain[0], spatial_w=dws, spatial_b=dbt[:, :GM_GROUPS].T,
                  sinks=dsinks[0, :SW_HEADS], conv_w=dcw)
    return dx, gsmall, (dwin, dwmem, dwup, dwout)


def kernel(x, mem, norm_pre, norm_post, norm_mem, w_in, conv_w, a_log, dt_bias, dn_norm, gm_norm, spatial_w, spatial_b, sinks, w_mem_kv, w_up, w_out, loss_target, m_norm_pre, m_norm_post, m_norm_mem, m_w_in, m_conv_w, m_a_log, m_dt_bias, m_dn_norm, m_gm_norm, m_spatial_w, m_spatial_b, m_sinks, m_w_mem_kv, m_w_up, m_w_out, v_norm_pre, v_norm_post, v_norm_mem, v_w_in, v_conv_w, v_a_log, v_dt_bias, v_dn_norm, v_gm_norm, v_spatial_w, v_spatial_b, v_sinks, v_w_mem_kv, v_w_up, v_w_out):
    xi, yi, ci = _my_place()
    my_slot = 4 * xi + 2 * yi + ci
    conv_shard = conv_w.shape[-1]
    x2, mem2, target = x[0], mem[0], loss_target[0]

    w_in_pad = jnp.pad(w_in.astype(BF), ((0, 0), (0, 0), (0, W_IN_SHARD_PAD - W_IN_SHARD)))
    shards = [[w_in_pad[l], w_mem_kv[l].astype(BF), w_up[l].astype(BF), w_out[l].astype(BF)] for l in range(DEPTH)]
    *slots0, conv_slots = _all_gather_slots(shards[0] + [conv_w], "gather_weights_l0")
    ag_send, ag_recv, ag_src, ag_land, ag_token = _spread_start(shards[1], "gather", "gather_weights_l1_start")
    conv_full = jnp.transpose(conv_slots, (1, 2, 0, 3)).reshape(DEPTH, CONV_W, N_DEV * conv_shard)
    small = dict(norm_pre=norm_pre, norm_post=norm_post, norm_mem=norm_mem, a_log=a_log,
                 dt_bias=dt_bias, dn_norm=dn_norm, gm_norm=gm_norm, spatial_w=spatial_w,
                 spatial_b=spatial_b, sinks=sinks)

    x1, saved0, weights0 = _layer_forward(0, x2, mem2, _layer_params(0, small, conv_full, ag_token[0, 0]),
                                          _aligned_w_in(slots0[0]), lambda cols: _other_weights(*slots0[1:]))
    ag_src, ag_land = _spread_wait(ag_send, ag_recv, ag_src, ag_land, [0], x1, "gather_weights_l1_wait_w_in")

    def late_weights(cols):
        _, land = _spread_wait(ag_send, ag_recv, ag_src, ag_land, [1, 2, 3], cols, "gather_weights_l1_wait_rest")
        return _other_weights(*land[1:])

    x_out, saved1, weights1 = _layer_forward(1, x1, mem2, _layer_params(1, small, conv_full, 0.0),
                                             _aligned_w_in(ag_land[0]), late_weights)
    loss, dx = loss_head(x_out, target, "loss_head")

    dx, gsmall1, gbig1 = _layer_backward(1, saved1, mem2, weights1, dx, 0.0)
    rs_send, rs_recv, rs_src, rs_land, rs_token = _spread_start(_grad_slots(*gbig1), "scatter", "exchange_grads_l1_start")
    dx, gsmall0, gbig0 = _layer_backward(0, saved0, mem2, weights0, dx, rs_token[0, 0])
    _, parts1 = _spread_wait(rs_send, rs_recv, rs_src, rs_land, range(4), dx, "exchange_grads_l1_wait")

    packed_names = SMALL_NAMES + ("conv_w",)
    gs = {n: jnp.stack([gsmall0[n], gsmall1[n]]) for n in packed_names}
    small_parts = [loss[0, :1]] + [gs[n] for n in packed_names]
    tot = _unpack_small(_all_reduce_vmem(_pack_small(small_parts), "all_reduce_small"), [p.shape for p in small_parts])
    loss_tot = tot[0][0]
    grads = dict(zip(packed_names, tot[1:]))
    grads["conv_w"] = lax.dynamic_slice_in_dim(grads["conv_w"], my_slot * conv_shard, conv_shard, axis=2)

    g_slots = [g.reshape((N_DEV // 2, 2) + g.shape[1:]) for g in _grad_slots(*gbig0)]
    theirs = _exchange_sibling(g_slots, "exchange_sibling_l0")
    chip_sums = [_pair_sum(g, t, "pair_sum_l0_%d" % i) for i, (g, t) in enumerate(zip(g_slots, theirs))]
    ch_send, ch_recv, ch_src, ch_land, ch_token = _spread_start(chip_sums, "chips", "exchange_chips_l0_start")

    given = dict(norm_pre=(norm_pre, m_norm_pre, v_norm_pre), norm_post=(norm_post, m_norm_post, v_norm_post),
                 norm_mem=(norm_mem, m_norm_mem, v_norm_mem), a_log=(a_log, m_a_log, v_a_log),
                 dt_bias=(dt_bias, m_dt_bias, v_dt_bias), dn_norm=(dn_norm, m_dn_norm, v_dn_norm),
                 gm_norm=(gm_norm, m_gm_norm, v_gm_norm), spatial_w=(spatial_w, m_spatial_w, v_spatial_w),
                 spatial_b=(spatial_b, m_spatial_b, v_spatial_b), sinks=(sinks, m_sinks, v_sinks),
                 conv_w=(conv_w, m_conv_w, v_conv_w))
    pshapes = [given[n][0].shape for n in packed_names]
    pw, pm, pv = (_pack_small([given[n][i] for n in packed_names]) for i in range(3))
    pd, pnm, pnv = _adamw(pw + ch_token[0, 0], _pack_small([grads[n] for n in packed_names]), pm, pv, "adamw_small")
    upd = {n: t for n, t in zip(packed_names, zip(_unpack_small(pd, pshapes), _unpack_small(pnm, pshapes),
                                                  _unpack_small(pnv, pshapes)))}
    big = (("w_in", (w_in, m_w_in, v_w_in)), ("w_mem_kv", (w_mem_kv, m_w_mem_kv, v_w_mem_kv)),
           ("w_up", (w_up, m_w_up, v_w_up)), ("w_out", (w_out, m_w_out, v_w_out)))
    first = [_sum_adamw(parts1[i], w, m, v, 1, None, ch_token, "adamw_%s_l1" % name)
             for i, (name, (w, m, v)) in enumerate(big)]
    _, parts0 = _spread_wait(ch_send, ch_recv, ch_src, ch_land, range(4), first[-1][0], "exchange_chips_l0_wait")
    for i, (name, (w, m, v)) in enumerate(big):
        g, d, nm, nv = _sum_adamw(parts0[i], w, m, v, 0, first[i], None, "adamw_%s_l0" % name)
        grads[name], upd[name] = g, (d, nm, nv)

    order = ("norm_pre", "norm_post", "norm_mem", "w_in", "conv_w", "a_log", "dt_bias", "dn_norm",
             "gm_norm", "spatial_w", "spatial_b", "sinks", "w_mem_kv", "w_up", "w_out")
    return (loss_tot, dx[None], *[grads[n] for n in order], *[upd[n][0] for n in order],
            *[upd[n][1] for n in order], *[upd[n][2] for n in order])
```

```python
import functools
import math

import jax
import jax.numpy as jnp
from jax import lax
from jax.experimental import pallas as pl
from jax.experimental.pallas import tpu as pltpu

MESH = pl.DeviceIdType.MESH
N_DEV = 8

D_MODEL = 1024
DEPTH = 2
N_BRANCH = 4
BRANCH_W = 512
DN_HEADS = 4
CONV_W = 4
GM_GROUPS = 4
SW_HEADS = 8
EPS = 1e-6
NEG_INF = -1e30

D_IN = 9992
W_IN_SHARD = D_IN // N_DEV
W_IN_SHARD_PAD = 1280
D_IN_AL = 10752
DN_W, SW_W, GM_W, XM_W = 2560, 1536, 1536, 1024

ADAM_LR = 0.001
ADAM_B1 = 0.9
ADAM_B2 = 0.999
ADAM_EPS = 1e-08
ADAM_WD = 0.01
ADAM_STEP = 10

VMEM_LIMIT = 56 * 1024 * 1024

BF = jnp.bfloat16
F32 = jnp.float32
DN_C = 128
DN_D = 128
HALO = 8
BLK = 128


def _my_place():
    return lax.axis_index("x"), lax.axis_index("y"), lax.axis_index("c")


_ANY = pl.BlockSpec(memory_space=pl.ANY)


def _all_gather_slots(parts, name):
    n = len(parts)

    def body(*refs):
        p_refs, out_refs = refs[:n], refs[n:2 * n]
        send_sems, recv_sems, local_sems = refs[2 * n:]
        x, y, c = _my_place()
        me, sibling = (x, y, c), (x, y, 1 - c)
        chips = [(1 - x, y), (x, 1 - y), (1 - x, 1 - y)]

        def copy(a, k, block, to, src=None):
            px, py, pc = block
            slot = out_refs[a].at[4 * px + 2 * py + pc]
            return pltpu.make_async_remote_copy(
                src_ref=slot if src is None else src, dst_ref=slot,
                send_sem=send_sems.at[7 * a + k], recv_sem=recv_sems.at[7 * a + k],
                device_id=to, device_id_type=MESH)

        mine = [pltpu.make_async_copy(p_refs[a], out_refs[a].at[4 * x + 2 * y + c], local_sems.at[a])
                for a in range(n)]
        for cp in mine:
            cp.start()
        first = []
        for a in range(n):
            first.append(copy(a, 0, me, sibling, src=p_refs[a]))
            first += [copy(a, 1 + j, me, (*chip, c), src=p_refs[a]) for j, chip in enumerate(chips)]
        for cp in first:
            cp.start()
        passed = []
        for j, chip in enumerate(chips):
            for a in range(n):
                copy(a, 1 + j, (*chip, c), me).wait_recv()
                fwd = copy(a, 4 + j, (*chip, c), sibling)
                fwd.start()
                passed.append(fwd)
        for a in range(n):
            copy(a, 0, sibling, me).wait_recv()
            for j, chip in enumerate(chips):
                copy(a, 4 + j, (*chip, 1 - c), me).wait_recv()
        for cp in first + passed:
            cp.wait_send()
        for cp in mine:
            cp.wait()

    return pl.pallas_call(
        body, name=name,
        out_shape=[jax.ShapeDtypeStruct((N_DEV,) + p.shape, p.dtype) for p in parts],
        in_specs=[_ANY] * n, out_specs=[_ANY] * n,
        scratch_shapes=[pltpu.SemaphoreType.DMA((7 * n,)), pltpu.SemaphoreType.DMA((7 * n,)),
                        pltpu.SemaphoreType.DMA((n,))],
    )(*parts)


def _exchange_sibling(parts, name):
    n = len(parts)

    def body(*refs):
        g_refs, out_refs = refs[:n], refs[n:2 * n]
        send_sems, recv_sems = refs[2 * n:]
        x, y, c = _my_place()
        copies = [pltpu.make_async_remote_copy(
            src_ref=g_refs[a].at[:, 1 - c], dst_ref=out_refs[a],
            send_sem=send_sems.at[a], recv_sem=recv_sems.at[a],
            device_id=(x, y, 1 - c), device_id_type=MESH) for a in range(n)]
        for cp in copies:
            cp.start()
        for cp in copies:
            cp.wait()

    return pl.pallas_call(
        body, name=name,
        out_shape=[jax.ShapeDtypeStruct((4,) + g.shape[2:], g.dtype) for g in parts],
        in_specs=[_ANY] * n, out_specs=[_ANY] * n,
        scratch_shapes=[pltpu.SemaphoreType.DMA((n,)), pltpu.SemaphoreType.DMA((n,))],
    )(*parts)


_HBM = pl.BlockSpec(memory_space=pltpu.HBM)
_SEM = pl.BlockSpec(memory_space=pltpu.SEMAPHORE)
_EFFECT = pltpu.SideEffectType.DATAFLOW_SIDE_EFFECTING


def _peer(x, y, c, k):
    return (1 - x if (k >> 2) & 1 else x, 1 - y if (k >> 1) & 1 else y, 1 - c if k & 1 else c)


def _spread_start(srcs, mode, name):
    n = len(srcs)
    lands = [lax.empty((N_DEV,) + s.shape if mode == "gather" else s.shape, s.dtype) for s in srcs]
    peers = range(0, N_DEV, 2) if mode == "chips" else range(N_DEV)

    def body(*refs):
        src_refs, land_refs = refs[:n], refs[n:2 * n]
        send_sems, recv_sems = refs[2 * n:2 * n + 2]
        token = refs[-1]
        x, y, c = _my_place()
        for a in range(n):
            for k in peers:
                px, py, pc = _peer(x, y, c, k)
                if mode == "chips":
                    src, mine = src_refs[a].at[2 * px + py], 2 * x + y
                else:
                    src = src_refs[a].at[4 * px + 2 * py + pc] if mode == "scatter" else src_refs[a]
                    mine = 4 * x + 2 * y + c
                pltpu.make_async_remote_copy(
                    src_ref=src, dst_ref=land_refs[a].at[mine],
                    send_sem=send_sems.at[a], recv_sem=recv_sems.at[a],
                    device_id=(px, py, pc), device_id_type=MESH).start()
        token[...] = jnp.zeros_like(token)

    out = pl.pallas_call(
        body, name=name,
        out_shape=[pltpu.SemaphoreType.DMA((n,)), pltpu.SemaphoreType.DMA((n,))]
        + [pltpu.HBM(s.shape, s.dtype) for s in srcs] + [pltpu.HBM(l.shape, l.dtype) for l in lands]
        + [jax.ShapeDtypeStruct((8, 128), F32)],
        in_specs=[_HBM] * (2 * n),
        out_specs=[_SEM, _SEM] + [_HBM] * (2 * n) + [pl.BlockSpec(memory_space=pltpu.VMEM)],
        input_output_aliases={i: 2 + i for i in range(2 * n)},
        compiler_params=pltpu.CompilerParams(has_side_effects=_EFFECT),
    )(*[pltpu.with_memory_space_constraint(s, pltpu.HBM) for s in srcs],
      *[pltpu.with_memory_space_constraint(l, pltpu.HBM) for l in lands])
    return out[0], out[1], out[2:2 + n], out[2 + n:2 + 2 * n], out[-1]


def _spread_wait(send_sems, recv_sems, srcs, lands, which, after, name):
    n = len(srcs)

    def body(*refs):
        land_refs = refs[n:2 * n]
        send_sems, recv_sems = refs[2 * n:2 * n + 2]
        x, y, c = _my_place()
        for a in which:
            whole = pltpu.make_async_remote_copy(
                src_ref=land_refs[a], dst_ref=land_refs[a],
                send_sem=send_sems.at[a], recv_sem=recv_sems.at[a],
                device_id=(x, y, c), device_id_type=MESH)
            whole.wait_send()
            whole.wait_recv()

    out = pl.pallas_call(
        body, name=name,
        out_shape=[pltpu.HBM(s.shape, s.dtype) for s in srcs] + [pltpu.HBM(l.shape, l.dtype) for l in lands],
        in_specs=[_HBM] * (2 * n) + [_SEM, _SEM, _ANY],
        out_specs=[_HBM] * (2 * n),
        input_output_aliases={i: i for i in range(2 * n)},
        compiler_params=pltpu.CompilerParams(has_side_effects=_EFFECT),
    )(*srcs, *lands, send_sems, recv_sems, after)
    return out[:n], out[n:]


def _all_reduce_vmem(v, name):
    def body(v_ref, out_ref, buf, send_sems, recv_sems):
        x, y, c = _my_place()
        peers = [(x, y, 1 - c), (1 - x, y, c), (x, 1 - y, c)]
        out_ref[...] = v_ref[...]
        for step, peer in enumerate(peers):
            cp = pltpu.make_async_remote_copy(
                src_ref=out_ref, dst_ref=buf.at[step],
                send_sem=send_sems.at[step], recv_sem=recv_sems.at[step],
                device_id=peer, device_id_type=MESH)
            cp.start()
            cp.wait()
            out_ref[...] = out_ref[...] + buf[step]

    return pl.pallas_call(
        body, name=name,
        out_shape=jax.ShapeDtypeStruct(v.shape, v.dtype),
        in_specs=[pl.BlockSpec(memory_space=pltpu.VMEM)],
        out_specs=pl.BlockSpec(memory_space=pltpu.VMEM),
        scratch_shapes=[pltpu.VMEM((3,) + v.shape, v.dtype),
                        pltpu.SemaphoreType.DMA((3,)), pltpu.SemaphoreType.DMA((3,))],
    )(v)


def _pick(n, pref):
    if n <= pref:
        return n
    t = pref - pref % 128
    while t > 0 and n % t:
        t -= 128
    return t if t > 0 else n


_DIMS = {"nn": (((1,), (0,)), ((), ())),
         "nt": (((1,), (1,)), ((), ())),
         "tn": (((0,), (0,)), ((), ()))}


def _matmul(a, b, mode, out_dtype, tiles, name):
    (m, k) = a.shape
    n = b.shape[1] if mode == "nn" else b.shape[0]
    tm, tn, tk = (_pick(d, t) for d, t in zip((m, n, k), tiles))
    nk = k // tk

    def product(a_ref, b_ref):
        return lax.dot_general(a_ref[...].astype(BF), b_ref[...].astype(BF), _DIMS[mode], preferred_element_type=F32)

    def body_whole_k(a_ref, b_ref, o_ref):
        o_ref[...] = product(a_ref, b_ref).astype(o_ref.dtype)

    def body_split_k(a_ref, b_ref, o_ref, acc_ref):
        kk = pl.program_id(2)

        @pl.when(kk == 0)
        def _():
            acc_ref[...] = jnp.zeros_like(acc_ref)

        acc_ref[...] += product(a_ref, b_ref)

        @pl.when(kk == nk - 1)
        def _():
            o_ref[...] = acc_ref[...].astype(o_ref.dtype)

    b_spec = (pl.BlockSpec((tn, tk), lambda i, j, kk: (j, kk)) if mode == "nt"
              else pl.BlockSpec((tk, tn), lambda i, j, kk: (kk, j)))
    return pl.pallas_call(
        body_whole_k if nk == 1 else body_split_k, name=name,
        out_shape=jax.ShapeDtypeStruct((m, n), out_dtype),
        grid=(m // tm, n // tn, nk),
        in_specs=[pl.BlockSpec((tm, tk), lambda i, j, kk: (i, kk)), b_spec],
        out_specs=pl.BlockSpec((tm, tn), lambda i, j, kk: (i, j)),
        scratch_shapes=[] if nk == 1 else [pltpu.VMEM((tm, tn), F32)],
        compiler_params=pltpu.CompilerParams(
            dimension_semantics=("parallel", "parallel", "arbitrary"),
            vmem_limit_bytes=VMEM_LIMIT),
    )(a, b)


def _rows2d(t, lead):
    return t.reshape(t.shape[:lead] + (math.prod(t.shape[lead:-1]), t.shape[-1]))


def _pair_sum(g, theirs, name):
    g3, t3 = _rows2d(g, 2), _rows2d(theirs, 1)
    _, r, w = t3.shape
    tr = _pick(r, 512)

    def body(g_ref, t_ref, o_ref):
        c = lax.axis_index("c")
        mine = jnp.where(c == 0, g_ref[0, 0], g_ref[0, 1])
        o_ref[0] = (mine.astype(F32) + t_ref[0].astype(F32)).astype(o_ref.dtype)

    out = pl.pallas_call(
        body, name=name,
        out_shape=jax.ShapeDtypeStruct(t3.shape, t3.dtype),
        grid=(4, r // tr),
        in_specs=[pl.BlockSpec((1, 2, tr, w), lambda q, i: (q, 0, i, 0)),
                  pl.BlockSpec((1, tr, w), lambda q, i: (q, i, 0))],
        out_specs=pl.BlockSpec((1, tr, w), lambda q, i: (q, i, 0)),
        compiler_params=pltpu.CompilerParams(dimension_semantics=("parallel", "parallel")),
    )(g3, t3)
    return out.reshape(theirs.shape)


def _adam_update(w, g, m, v):
    c1 = 1.0 - ADAM_B1 ** ADAM_STEP
    c2 = 1.0 - ADAM_B2 ** ADAM_STEP
    nm = ADAM_B1 * m + (1.0 - ADAM_B1) * g
    nv = ADAM_B2 * v + (1.0 - ADAM_B2) * (g * g)
    delta = -ADAM_LR * ((nm / c1) / (jnp.sqrt(nv / c2) + ADAM_EPS) + ADAM_WD * w)
    return delta, nm, nv


def _sum_adamw(parts, w, m, v, layer, carry, after, name):
    shape = w.shape
    cols = shape[-1]
    p3 = _rows2d(parts, 1)
    w3, m3, v3 = (_rows2d(t, 1) for t in (w, m, v))
    rows = w3.shape[1]
    tr = _pick(rows, 128)
    n_parts = p3.shape[0]

    def body(p_ref, w_ref, m_ref, v_ref, *rest):
        g_ref, d_ref, nm_ref, nv_ref = rest[-4:]
        g = p_ref[0, :, :cols].astype(F32)
        for q in range(1, n_parts):
            g = g + p_ref[q, :, :cols].astype(F32)
        d, nm, nv = _adam_update(w_ref[0], g, m_ref[0], v_ref[0])
        g_ref[0] = g
        d_ref[0] = d
        nm_ref[0] = nm
        nv_ref[0] = nv

    spec = pl.BlockSpec((1, tr, cols), lambda i: (layer, i, 0))
    extra = [] if carry is None else [_rows2d(t, 1) for t in carry]
    tail = [] if after is None else [after]
    out = pl.pallas_call(
        body, name=name,
        out_shape=[jax.ShapeDtypeStruct(w3.shape, F32)] * 4,
        grid=(rows // tr,),
        in_specs=[pl.BlockSpec((n_parts, tr, p3.shape[-1]), lambda i: (0, i, 0)), spec, spec, spec] + [_ANY] * len(extra + tail),
        out_specs=[spec] * 4,
        input_output_aliases={4 + i: i for i in range(len(extra))},
        compiler_params=pltpu.CompilerParams(dimension_semantics=("parallel",)),
    )(p3, w3, m3, v3, *extra, *tail)
    return tuple(t.reshape(shape) for t in out)


def _sum_adamw_t(parts, w, m, v, name):
    rows = parts[0].shape[2]
    tr = 128
    assert rows % tr == 0 and rows >= w.shape[0]

    def body(*refs):
        p_refs, (w_ref, m_ref, v_ref), (g_ref, d_ref, nm_ref, nv_ref) = refs[:DEPTH], refs[DEPTH:DEPTH + 3], refs[DEPTH + 3:]
        for l in range(DEPTH):
            g = p_refs[l][0].astype(F32)
            for q in range(1, p_refs[l].shape[0]):
                g = g + p_refs[l][q].astype(F32)
            g = g.T
            d, nm, nv = _adam_update(w_ref[:, l, :], g, m_ref[:, l, :], v_ref[:, l, :])
            g_ref[:, l, :] = g
            d_ref[:, l, :] = d
            nm_ref[:, l, :] = nm
            nv_ref[:, l, :] = nv

    spec = pl.BlockSpec((tr,) + w.shape[1:], lambda i: (i, 0, 0))
    return pl.pallas_call(
        body, name=name,
        out_shape=[jax.ShapeDtypeStruct(w.shape, F32)] * 4,
        grid=(rows // tr,),
        in_specs=[pl.BlockSpec((p.shape[0], p.shape[1], tr), lambda i: (0, 0, i)) for p in parts] + [spec] * 3,
        out_specs=[spec] * 4,
        compiler_params=pltpu.CompilerParams(dimension_semantics=("parallel",)),
    )(*parts, w, m, v)


def _adamw(w, g, m, v, name):
    rows, cols = w.shape
    tr = _pick(rows, 128)

    def body(w_ref, g_ref, m_ref, v_ref, d_ref, nm_ref, nv_ref):
        d, nm, nv = _adam_update(w_ref[...], g_ref[...], m_ref[...], v_ref[...])
        d_ref[...] = d
        nm_ref[...] = nm
        nv_ref[...] = nv

    spec = pl.BlockSpec((tr, cols), lambda i: (i, 0))
    return pl.pallas_call(
        body, name=name,
        out_shape=[jax.ShapeDtypeStruct((rows, cols), F32)] * 3,
        grid=(rows // tr,),
        in_specs=[spec] * 4, out_specs=[spec] * 3,
        compiler_params=pltpu.CompilerParams(dimension_semantics=("parallel",)),
    )(w, g, m, v)


_VJP = {"nn": (("nt", "gb"), ("tn", "ag")),
        "nt": (("nn", "gb"), ("tn", "ga")),
        "tn": (("nt", "bg"), ("nn", "ag"))}


def _make_dot(cast, precision):
    def raw(mode, a, b):
        return lax.dot_general(cast(a), cast(b), _DIMS[mode], precision=precision,
                               preferred_element_type=F32)

    @functools.partial(jax.custom_vjp, nondiff_argnums=(0,))
    def dot(mode, a, b):
        return raw(mode, a, b)

    def fwd(mode, a, b):
        return raw(mode, a, b), (a, b)

    def bwd(mode, res, g):
        a, b = res
        pick = {"a": a, "b": b, "g": g}
        (ma, ta), (mb, tb) = _VJP[mode]
        return dot(ma, pick[ta[0]], pick[ta[1]]), dot(mb, pick[tb[0]], pick[tb[1]])

    dot.defvjp(fwd, bwd)
    return dot


bdot = _make_dot(lambda t: t.astype(BF), None)
hdot = _make_dot(lambda t: t, lax.Precision.HIGHEST)


def _xdot(mode, a, b):
    return lax.dot_general(a, b, _DIMS[mode], precision=lax.Precision.HIGH, preferred_element_type=F32)


def _unit_lower_inverse(Ls):
    n = Ls[0].shape[0]
    batched = (((2,), (1,)), ((0,), (0,)))
    mm = lambda a, b: lax.dot_general(a, b, batched, precision=lax.Precision.HIGH, preferred_element_type=F32)
    eye = (lax.broadcasted_iota(jnp.int32, (n, n), 0) == lax.broadcasted_iota(jnp.int32, (n, n), 1)).astype(F32)
    p = jnp.stack(Ls)
    t_inv = eye[None] - p
    for _ in range(6):
        p = mm(p, p)
        t_inv = t_inv + mm(t_inv, p)
    return [t_inv[h] for h in range(len(Ls))]


@jax.custom_vjp
def _tri_solve(L, rhs, t_inv):
    return _xdot("nn", t_inv, rhs)


def _tri_solve_fwd(L, rhs, t_inv):
    sol = _xdot("nn", t_inv, rhs)
    return sol, (t_inv, sol)


def _tri_solve_bwd(res, dsol):
    t_inv, sol = res
    drhs = _xdot("tn", t_inv, dsol)
    return -_xdot("nt", drhs, sol), drhs, jnp.zeros_like(t_inv)


_tri_solve.defvjp(_tri_solve_fwd, _tri_solve_bwd)


def _sigmoid(x):
    return 1.0 / (1.0 + jnp.exp(-x))


def _softplus(x):
    return jnp.maximum(x, 0.0) + jnp.log(1.0 + jnp.exp(-jnp.abs(x)))


def _dn_chunk(S, xs, ba, z, cw, al, dt, dn, t_saved=None):
    C = DN_C
    pre = xs[0] * cw[0] + xs[1] * cw[1] + xs[2] * cw[2] + xs[3] * cw[3]
    qkv = pre * _sigmoid(pre)
    lane = lax.broadcasted_iota(jnp.int32, (1, 128), 1)
    sub = lax.broadcasted_iota(jnp.int32, (C, 1), 0)
    row_i = lax.broadcasted_iota(jnp.int32, (C, C), 0)
    col_i = lax.broadcasted_iota(jnp.int32, (C, C), 1)
    strict = row_i > col_i
    incl = row_i >= col_i
    g_all = jnp.where((lane >= 4) & (lane < 8), -jnp.exp(al) * _softplus(ba + dt), 0.0)
    gc_all = hdot("nn", incl.astype(F32), g_all)
    gc_all_t = gc_all.T
    beta_all = _sigmoid(ba)
    glast_all = jnp.sum(jnp.where(sub == C - 1, gc_all, 0.0), axis=0, keepdims=True)
    heads = []
    for h in range(DN_HEADS):
        q = qkv[:, 128 * h:128 * (h + 1)]
        k = qkv[:, 512 + 128 * h:512 + 128 * (h + 1)]
        v = qkv[:, 1024 + 128 * h:1024 + 128 * (h + 1)]
        q = q * lax.rsqrt(jnp.sum(q * q, axis=1, keepdims=True) + EPS) * (DN_D ** -0.5)
        k = k * lax.rsqrt(jnp.sum(k * k, axis=1, keepdims=True) + EPS)
        beta = jnp.sum(jnp.where(lane == h, beta_all, 0.0), axis=1, keepdims=True)
        gc = jnp.sum(jnp.where(lane == 4 + h, gc_all, 0.0), axis=1, keepdims=True)
        gc_row = jnp.sum(jnp.where(sub == 4 + h, gc_all_t, 0.0), axis=0, keepdims=True)
        g_last = jnp.sum(jnp.where(lane == 4 + h, glast_all, 0.0), axis=1, keepdims=True)
        diff = gc - gc_row
        kb = k * beta
        L = jnp.where(strict, bdot("nt", kb, k) * jnp.exp(jnp.where(strict, diff, 0.0)), 0.0)
        heads.append((q, k, v, beta, gc, g_last, diff, kb, L))
    t_invs = _unit_lower_inverse([hd[-1] for hd in heads]) if t_saved is None else t_saved
    ys, s_new = [], []
    for h, (q, k, v, beta, gc, g_last, diff, kb, L) in enumerate(heads):
        sol = _tri_solve(L, jnp.concatenate([v * beta, kb * jnp.exp(gc)], axis=1), t_invs[h])
        u, w = sol[:, :DN_D], sol[:, DN_D:]
        a_qk = jnp.where(incl, bdot("nt", q, k) * jnp.exp(jnp.where(incl, diff, 0.0)), 0.0)
        qg = q * jnp.exp(gc)
        kd = k * jnp.exp(g_last - gc)
        v_new = u - bdot("nn", w, S[h])
        o = bdot("nn", qg, S[h]) + bdot("nn", a_qk, v_new)
        s_new.append(S[h] * jnp.exp(g_last) + bdot("tn", kd, v_new))
        o = o * lax.rsqrt(jnp.mean(o * o, axis=1, keepdims=True) + EPS) * dn
        zh = z[:, 128 * h:128 * (h + 1)]
        ys.append(o * (zh * _sigmoid(zh)))
    return jnp.concatenate(ys, axis=1), tuple(s_new), tuple(t_invs)


def _load_shifted(xbuf, x_ref, halo_ref, first):
    xbuf[0:HALO, :] = jnp.where(first, 0.0, halo_ref[:, 0:1536])
    xbuf[HALO:HALO + DN_C, :] = x_ref[:, 0:1536]
    return [xbuf[HALO - 3 + k:HALO - 3 + k + DN_C, :] for k in range(4)]


def dn_forward(cols, jblk, cw, al, dt, dn, name):
    T = cols.shape[0]
    n = T // DN_C

    def body(x_ref, halo_ref, cw_ref, al_ref, dt_ref, dn_ref, y_ref, ss_ref, ts_ref, s_scr, xbuf):
        i = pl.program_id(0)

        @pl.when(i == 0)
        def _():
            s_scr[...] = jnp.zeros_like(s_scr)

        xs = _load_shifted(xbuf, x_ref, halo_ref, i == 0)
        ss_ref[0] = s_scr[...]
        S = [s_scr[h] for h in range(DN_HEADS)]
        cws = [cw_ref[k:k + 1, :] for k in range(4)]
        y, s_new, t_invs = _dn_chunk(S, xs, x_ref[:, 2048:2176], x_ref[:, 1536:2048], cws,
                                     al_ref[...], dt_ref[...], dn_ref[...])
        y_ref[...] = y
        for h in range(DN_HEADS):
            s_scr[h] = s_new[h]
            ts_ref[0, h] = t_invs[h]

    per = DN_C // HALO
    full = lambda shape: pl.BlockSpec(shape, lambda i: (0,) * len(shape))
    return pl.pallas_call(
        body, name=name,
        out_shape=[jax.ShapeDtypeStruct((T, 512), F32),
                   jax.ShapeDtypeStruct((n, DN_HEADS, DN_D, DN_D), F32),
                   jax.ShapeDtypeStruct((n, DN_HEADS, DN_D, DN_D), F32)],
        grid=(n,),
        in_specs=[pl.BlockSpec((DN_C, DN_W), lambda i: (i, jblk)),
                  pl.BlockSpec((HALO, DN_W), lambda i: (jnp.maximum(i * per - 1, 0), jblk)),
                  full((4, 1536)), full((1, 128)), full((1, 128)), full((1, 128))],
        out_specs=[pl.BlockSpec((DN_C, 512), lambda i: (i, 0)),
                   pl.BlockSpec((1, DN_HEADS, DN_D, DN_D), lambda i: (i, 0, 0, 0)),
                   pl.BlockSpec((1, DN_HEADS, DN_D, DN_D), lambda i: (i, 0, 0, 0))],
        scratch_shapes=[pltpu.VMEM((DN_HEADS, DN_D, DN_D), F32), pltpu.VMEM((HALO + DN_C, 1536), F32)],
        compiler_params=pltpu.CompilerParams(dimension_semantics=("arbitrary",)),
    )(cols, cols, cw, al, dt, dn)


def dn_backward(cols, jblk, cw, al, dt, dn, ss, ts, dy, dcols, name):
    T = cols.shape[0]
    n = T // DN_C

    def body(x_ref, halo_ref, cw_ref, al_ref, dt_ref, dn_ref, ss_ref, ts_ref, dy_ref, dcols_in,
             dx_ref, dcw_ref, dal_ref, ddt_ref, ddn_ref, ds_scr, xbuf, dbuf, carry):
        i = pl.program_id(0)

        @pl.when(i == 0)
        def _():
            ds_scr[...] = jnp.zeros_like(ds_scr)
            carry[...] = jnp.zeros_like(carry)
            dcw_ref[...] = jnp.zeros_like(dcw_ref)
            dal_ref[...] = jnp.zeros_like(dal_ref)
            ddt_ref[...] = jnp.zeros_like(ddt_ref)
            ddn_ref[...] = jnp.zeros_like(ddn_ref)

        xs = _load_shifted(xbuf, x_ref, halo_ref, i == n - 1)
        S = [ss_ref[0, h] for h in range(DN_HEADS)]
        cws = [cw_ref[k:k + 1, :] for k in range(4)]

        t_saved = [ts_ref[0, h] for h in range(DN_HEADS)]

        def f(S, xs, ba, z, cws, al, dt, dn):
            return _dn_chunk(S, xs, ba, z, cws, al, dt, dn, t_saved)[:2]

        _, vjp = jax.vjp(f, S, xs, x_ref[:, 2048:2176], x_ref[:, 1536:2048], cws, al_ref[...], dt_ref[...], dn_ref[...])
        dS, dxs, dba, dz, dcws, dal, ddt, ddn = vjp((dy_ref[...], tuple(ds_scr[h] for h in range(DN_HEADS))))
        for h in range(DN_HEADS):
            ds_scr[h] = dS[h]
        dbuf[...] = jnp.zeros_like(dbuf)
        for k in range(4):
            lo = HALO - 3 + k
            dbuf[lo:lo + DN_C, :] += dxs[k]
        dbuf[DN_C:DN_C + HALO, :] += carry[...]
        dx_ref[...] = jnp.concatenate([dbuf[HALO:HALO + DN_C, :], dz, dba,
                                       jnp.zeros((DN_C, DN_W - 2176), F32)], axis=1).astype(dx_ref.dtype)
        carry[...] = dbuf[0:HALO, :]
        for k in range(4):
            dcw_ref[k:k + 1, :] += dcws[k]
        dal_ref[...] += dal
        ddt_ref[...] += ddt
        ddn_ref[...] += ddn

    per = DN_C // HALO
    rev = lambda i: n - 1 - i
    full = lambda shape: pl.BlockSpec(shape, lambda i: (0,) * len(shape))
    return pl.pallas_call(
        body, name=name,
        out_shape=[jax.ShapeDtypeStruct(dcols.shape, dcols.dtype),jax.ShapeDtypeStruct((4, 1536), F32),
                   jax.ShapeDtypeStruct((1, 128), F32), jax.ShapeDtypeStruct((1, 128), F32),
                   jax.ShapeDtypeStruct((1, 128), F32)],
        grid=(n,),
        in_specs=[pl.BlockSpec((DN_C, DN_W), lambda i: (rev(i), jblk)),
                  pl.BlockSpec((HALO, DN_W), lambda i: (jnp.maximum(rev(i) * per - 1, 0), jblk)),
                  full((4, 1536)), full((1, 128)), full((1, 128)), full((1, 128)),
                  pl.BlockSpec((1, DN_HEADS, DN_D, DN_D), lambda i: (rev(i), 0, 0, 0)),
                  pl.BlockSpec((1, DN_HEADS, DN_D, DN_D), lambda i: (rev(i), 0, 0, 0)),
                  pl.BlockSpec((DN_C, 512), lambda i: (rev(i), 0)), _ANY],
        out_specs=[pl.BlockSpec((DN_C, DN_W), lambda i: (rev(i), jblk)),
                   full((4, 1536)), full((1, 128)), full((1, 128)), full((1, 128))],
        scratch_shapes=[pltpu.VMEM((DN_HEADS, DN_D, DN_D), F32), pltpu.VMEM((HALO + DN_C, 1536), F32),
                        pltpu.VMEM((HALO + DN_C, 1536), F32), pltpu.VMEM((HALO, 1536), F32)],
        input_output_aliases={9: 0},
        compiler_params=pltpu.CompilerParams(dimension_semantics=("arbitrary",)),
    )(cols, cols, cw, al, dt, dn, ss, ts, dy, dcols)


def _full(shape):
    return pl.BlockSpec(shape, lambda i: (0,) * len(shape))


def _silu(x):
    return x * _sigmoid(x)


def _gelu(x):
    return 0.5 * x * (1.0 + jnp.tanh(0.7978845608028654 * (x + 0.044715 * (x * x * x))))


def _lane_col(mat, idx):
    lane = lax.broadcasted_iota(jnp.int32, (1, mat.shape[1]), 1)
    return jnp.sum(jnp.where(lane == idx, mat, 0.0), axis=1, keepdims=True)


def _gm_chunk(uv, z, gain, ws, bt):
    g = _gelu(uv)
    u, v = g[:, :512], g[:, 512:]
    v = v * lax.rsqrt(jnp.mean(v * v, axis=1, keepdims=True) + EPS) * gain
    row_i = lax.broadcasted_iota(jnp.int32, (BLK, BLK), 0)
    col_i = lax.broadcasted_iota(jnp.int32, (BLK, BLK), 1)
    causal = row_i >= col_i
    ss = []
    for grp in range(4):
        wg = jnp.where(causal, ws[grp], 0.0)
        ss.append(bdot("nn", wg, v[:, BLK * grp:BLK * (grp + 1)]) + _lane_col(bt, grp))
    return u * jnp.concatenate(ss, axis=1) * _silu(z)


def gm_forward(cols, jblk, gain, ws, bt, name):
    T = cols.shape[0]

    def body(x_ref, gain_ref, ws_ref, bt_ref, y_ref):
        y_ref[...] = _gm_chunk(x_ref[:, 0:1024], x_ref[:, 1024:1536], gain_ref[...],
                               [ws_ref[g] for g in range(4)], bt_ref[...])

    return pl.pallas_call(
        body, name=name, out_shape=jax.ShapeDtypeStruct((T, 512), F32), grid=(T // BLK,),
        in_specs=[pl.BlockSpec((BLK, GM_W), lambda i: (i, jblk)),
                  _full((1, 512)), _full((4, BLK, BLK)), _full((BLK, BLK))],
        out_specs=pl.BlockSpec((BLK, 512), lambda i: (i, 0)),
        compiler_params=pltpu.CompilerParams(dimension_semantics=("parallel",)),
    )(cols, gain, ws, bt)


def gm_backward(cols, jblk, gain, ws, bt, dy, dcols, name):
    T = cols.shape[0]

    def body(x_ref, gain_ref, ws_ref, bt_ref, dy_ref, dcols_in, dx_ref, dgain_ref, dws_ref, dbt_ref):
        @pl.when(pl.program_id(0) == 0)
        def _():
            dgain_ref[...] = jnp.zeros_like(dgain_ref)
            dws_ref[...] = jnp.zeros_like(dws_ref)
            dbt_ref[...] = jnp.zeros_like(dbt_ref)

        _, vjp = jax.vjp(_gm_chunk, x_ref[:, 0:1024], x_ref[:, 1024:1536], gain_ref[...],
                         [ws_ref[g] for g in range(4)], bt_ref[...])
        duv, dz, dgain, dws, dbt = vjp(dy_ref[...])
        dx_ref[...] = jnp.concatenate([duv, dz], axis=1).astype(dx_ref.dtype)
        dgain_ref[...] += dgain
        for g in range(4):
            dws_ref[g] += dws[g]
        dbt_ref[...] += dbt

    return pl.pallas_call(
        body, name=name,
        out_shape=[jax.ShapeDtypeStruct(dcols.shape, dcols.dtype),jax.ShapeDtypeStruct((1, 512), F32),
                   jax.ShapeDtypeStruct((4, BLK, BLK), F32), jax.ShapeDtypeStruct((BLK, BLK), F32)],
        grid=(T // BLK,),
        in_specs=[pl.BlockSpec((BLK, GM_W), lambda i: (i, jblk)),
                  _full((1, 512)), _full((4, BLK, BLK)), _full((BLK, BLK)),
                  pl.BlockSpec((BLK, 512), lambda i: (i, 0)), _ANY],
        out_specs=[pl.BlockSpec((BLK, GM_W), lambda i: (i, jblk)),
                   _full((1, 512)), _full((4, BLK, BLK)), _full((BLK, BLK))],
        input_output_aliases={5: 0},
        compiler_params=pltpu.CompilerParams(dimension_semantics=("arbitrary",)),
    )(cols, gain, ws, bt, dy, dcols)


def _sw_block(first, q, kp, kc, vp, vc, z, sinks):
    P = BLK
    lane = lax.broadcasted_iota(jnp.int32, (1, 128), 1)
    r = lax.broadcasted_iota(jnp.int32, (128, 128), 0)
    c = lax.broadcasted_iota(jnp.int32, (128, 128), 1)
    swap = (c == (r + 64) % 128).astype(F32)
    k2 = jnp.concatenate([kp, kc], axis=0)
    v2 = jnp.concatenate([vp, vc], axis=0)
    k2s = bdot("nn", k2, swap)
    v2s = bdot("nn", v2, swap)
    qi = lax.broadcasted_iota(jnp.int32, (P, 2 * P), 0)
    kj = lax.broadcasted_iota(jnp.int32, (P, 2 * P), 1)
    dist = qi + P - kj
    valid = (dist >= 0) & (dist < P) & ((kj >= P) | jnp.logical_not(first))
    outs = []
    for j in range(4):
        acc = jnp.zeros((P, 128), F32)
        for half in range(2):
            h = 2 * j + half
            kv = h // 4
            in_half = (lane >= 64 * half) & (lane < 64 * half + 64)
            qh = jnp.where(in_half, q[:, 128 * j:128 * (j + 1)], 0.0)
            same = (half == kv)
            s = bdot("nt", qh, k2 if same else k2s) * (64 ** -0.5)
            s = jnp.where(valid, s, NEG_INF)
            sink = _lane_col(sinks, h)
            m = lax.stop_gradient(jnp.maximum(jnp.max(s, axis=1, keepdims=True), sink))
            e = jnp.exp(s - m)
            p = e / (jnp.sum(e, axis=1, keepdims=True) + jnp.exp(sink - m))
            o = bdot("nn", p, v2 if same else v2s)
            acc = acc + jnp.where(in_half, o, 0.0)
        outs.append(acc)
    return jnp.concatenate(outs, axis=1) * _silu(z)


def _sw_specs(jblk, idx):
    prev = lambda i: jnp.maximum(idx(i) - 1, 0)
    jk = (jblk * SW_W + 1024) // 128
    return [pl.BlockSpec((BLK, SW_W), lambda i: (idx(i), jblk)),
            pl.BlockSpec((BLK, 128), lambda i: (prev(i), jk)),
            pl.BlockSpec((BLK, 128), lambda i: (prev(i), jk + 1)), _full((1, 128))]


def sw_forward(cols, jblk, sinks, name):
    T = cols.shape[0]

    def body(x_ref, kp_ref, vp_ref, s_ref, y_ref):
        y_ref[...] = _sw_block(pl.program_id(0) == 0, x_ref[:, 0:512], kp_ref[...], x_ref[:, 1024:1152],
                               vp_ref[...], x_ref[:, 1152:1280], x_ref[:, 512:1024], s_ref[...])

    return pl.pallas_call(
        body, name=name, out_shape=jax.ShapeDtypeStruct((T, 512), F32), grid=(T // BLK,),
        in_specs=_sw_specs(jblk, lambda i: i),
        out_specs=pl.BlockSpec((BLK, 512), lambda i: (i, 0)),
        compiler_params=pltpu.CompilerParams(dimension_semantics=("parallel",)),
    )(cols, cols, cols, sinks)


def sw_backward(cols, jblk, sinks, dy, dcols, name):
    T = cols.shape[0]
    n = T // BLK
    rev = lambda i: n - 1 - i

    def body(x_ref, kp_ref, vp_ref, s_ref, dy_ref, dcols_in, dx_ref, ds_ref, kcarry, vcarry):
        i = pl.program_id(0)

        @pl.when(i == 0)
        def _():
            kcarry[...] = jnp.zeros_like(kcarry)
            vcarry[...] = jnp.zeros_like(vcarry)
            ds_ref[...] = jnp.zeros_like(ds_ref)

        f = functools.partial(_sw_block, i == n - 1)
        _, vjp = jax.vjp(f, x_ref[:, 0:512], kp_ref[...], x_ref[:, 1024:1152], vp_ref[...], x_ref[:, 1152:1280],
                         x_ref[:, 512:1024], s_ref[...])
        dq, dkp, dkc, dvp, dvc, dz, dsk = vjp(dy_ref[...])
        dx_ref[...] = jnp.concatenate([dq, dz, dkc + kcarry[...], dvc + vcarry[...],
                                       jnp.zeros((BLK, SW_W - 1280), F32)], axis=1).astype(dx_ref.dtype)
        kcarry[...] = dkp
        vcarry[...] = dvp
        ds_ref[...] += dsk

    return pl.pallas_call(
        body, name=name,
        out_shape=[jax.ShapeDtypeStruct(dcols.shape, dcols.dtype),jax.ShapeDtypeStruct((1, 128), F32)],
        grid=(n,),
        in_specs=_sw_specs(jblk, rev) + [pl.BlockSpec((BLK, 512), lambda i: (rev(i), 0)), _ANY],
        out_specs=[pl.BlockSpec((BLK, SW_W), lambda i: (rev(i), jblk)), _full((1, 128))],
        scratch_shapes=[pltpu.VMEM((BLK, 128), F32), pltpu.VMEM((BLK, 128), F32)],
        input_output_aliases={5: 0},
        compiler_params=pltpu.CompilerParams(dimension_semantics=("arbitrary",)),
    )(cols, cols, cols, sinks, dy, dcols)


XM_TQ = 256


def _xm_block(q, z, mkv):
    outs = []
    for h in range(4):
        s = bdot("nt", q[:, 128 * h:128 * (h + 1)], mkv[:, 128 * h:128 * (h + 1)]) * (128 ** -0.5)
        m = lax.stop_gradient(jnp.max(s, axis=1, keepdims=True))
        e = jnp.exp(s - m)
        p = e / jnp.sum(e, axis=1, keepdims=True)
        outs.append(bdot("nn", p, mkv[:, 512 + 128 * h:512 + 128 * (h + 1)]))
    return jnp.concatenate(outs, axis=1) * _silu(z)


def xm_forward(cols, jblk, mkv, name):
    T = cols.shape[0]

    def body(x_ref, m_ref, y_ref):
        y_ref[...] = _xm_block(x_ref[:, 0:512], x_ref[:, 512:1024], m_ref[...])

    return pl.pallas_call(
        body, name=name, out_shape=jax.ShapeDtypeStruct((T, 512), F32), grid=(T // XM_TQ,),
        in_specs=[pl.BlockSpec((XM_TQ, XM_W), lambda i: (i, jblk)), _full(mkv.shape)],
        out_specs=pl.BlockSpec((XM_TQ, 512), lambda i: (i, 0)),
        compiler_params=pltpu.CompilerParams(dimension_semantics=("parallel",)),
    )(cols, mkv)


def xm_backward(cols, jblk, mkv, dy, dcols, name):
    T = cols.shape[0]

    def body(x_ref, m_ref, dy_ref, dcols_in, dx_ref, dm_ref):
        @pl.when(pl.program_id(0) == 0)
        def _():
            dm_ref[...] = jnp.zeros_like(dm_ref)

        _, vjp = jax.vjp(_xm_block, x_ref[:, 0:512], x_ref[:, 512:1024], m_ref[...])
        dq, dz, dm = vjp(dy_ref[...])
        dx_ref[...] = jnp.concatenate([dq, dz], axis=1).astype(dx_ref.dtype)
        dm_ref[...] += dm

    return pl.pallas_call(
        body, name=name,
        out_shape=[jax.ShapeDtypeStruct(dcols.shape, dcols.dtype),jax.ShapeDtypeStruct(mkv.shape, F32)],
        grid=(T // XM_TQ,),
        in_specs=[pl.BlockSpec((XM_TQ, XM_W), lambda i: (i, jblk)), _full(mkv.shape),
                  pl.BlockSpec((XM_TQ, 512), lambda i: (i, 0)), _ANY],
        out_specs=[pl.BlockSpec((XM_TQ, XM_W), lambda i: (i, jblk)), _full(mkv.shape)],
        input_output_aliases={3: 0},
        compiler_params=pltpu.CompilerParams(dimension_semantics=("arbitrary",)),
    )(cols, mkv, dy, dcols)


def _rms(x, gain):
    return x * lax.rsqrt(jnp.mean(x * x, axis=1, keepdims=True) + EPS) * gain


def memkv_forward(mem, gain, w, name):
    def body(m_ref, g_ref, w_ref, o_ref):
        o_ref[...] = bdot("nn", _rms(m_ref[...], g_ref[...]), w_ref[...])

    return pl.pallas_call(body, name=name, out_shape=jax.ShapeDtypeStruct(mem.shape, F32),
                          compiler_params=pltpu.CompilerParams(vmem_limit_bytes=VMEM_LIMIT))(mem, gain, w)


def memkv_backward(mem, gain, w, dkv, name):
    def body(m_ref, g_ref, w_ref, d_ref, dg_ref, dw_ref):
        mem_v = m_ref[...]
        _, vjp = jax.vjp(lambda g, ww: bdot("nn", _rms(mem_v, g), ww), g_ref[...], w_ref[...].astype(F32))
        dg, dw = vjp(d_ref[...])
        dg_ref[...] = dg
        dw_ref[...] = dw

    return pl.pallas_call(body, name=name,
                          out_shape=[jax.ShapeDtypeStruct(gain.shape, F32), jax.ShapeDtypeStruct(w.shape, F32)],
                          compiler_params=pltpu.CompilerParams(vmem_limit_bytes=VMEM_LIMIT))(mem, gain, w, dkv)


MG_TB = 128


def _merge_block(ys, gl, wup, wout, gpost):
    merged = None
    for n in range(4):
        t = _sigmoid(gl[:, 1024 * n:1024 * (n + 1)]) * bdot("nn", ys[n], wup[n])
        merged = t if merged is None else merged + t
    out = bdot("nn", merged, wout)
    return _rms(out, gpost)


def merge_forward(ys, cols, jgate, x, wup, wout, gpost, name):
    T = x.shape[0]
    TB = 256

    def body(ya, yb, yc, ym, gl_ref, x_ref, wup_ref, wout_ref, gp_ref, o_ref):
        upd = _merge_block([ya[...], yb[...], yc[...], ym[...]], gl_ref[...],
                           [wup_ref[n] for n in range(4)], wout_ref[...], gp_ref[...])
        o_ref[...] = x_ref[...] + upd

    yspec = pl.BlockSpec((TB, 512), lambda i: (i, 0))
    return pl.pallas_call(
        body, name=name, out_shape=jax.ShapeDtypeStruct((T, 1024), F32), grid=(T // TB,),
        in_specs=[yspec] * 4 + [pl.BlockSpec((TB, 4096), lambda i: (i, jgate)),
                                pl.BlockSpec((TB, 1024), lambda i: (i, 0)),
                                _full(wup.shape), _full(wout.shape), _full((1, 1024))],
        out_specs=pl.BlockSpec((TB, 1024), lambda i: (i, 0)),
        compiler_params=pltpu.CompilerParams(dimension_semantics=("parallel",), vmem_limit_bytes=VMEM_LIMIT),
    )(*ys, cols, x, wup, wout, gpost)


def merge_backward(ys, cols, jgate, wup, wout, gpost, dx, name):
    T = dx.shape[0]
    TB = MG_TB

    def body(ya, yb, yc, ym, gl_ref, wup_ref, wout_ref, gp_ref, dx_ref,
             dgl_ref, dya, dyb, dyc, dym, dwup_ref, dwout_ref, dgp_ref):
        @pl.when(pl.program_id(0) == 0)
        def _():
            dwup_ref[...] = jnp.zeros_like(dwup_ref)
            dwout_ref[...] = jnp.zeros_like(dwout_ref)
            dgp_ref[...] = jnp.zeros_like(dgp_ref)

        _, vjp = jax.vjp(_merge_block, [ya[...], yb[...], yc[...], ym[...]], gl_ref[...],
                         [wup_ref[n].astype(F32) for n in range(4)], wout_ref[...].astype(F32), gp_ref[...])
        dys, dgl, dwup, dwout, dgp = vjp(dx_ref[...])
        for ref, val in zip((dya, dyb, dyc, dym), dys):
            ref[...] = val
        dgl_ref[...] = dgl.astype(dgl_ref.dtype)
        for n in range(4):
            dwup_ref[n] += dwup[n]
        dwout_ref[...] += dwout
        dgp_ref[...] += dgp

    yspec = pl.BlockSpec((TB, 512), lambda i: (i, 0))
    return pl.pallas_call(
        body, name=name,
        out_shape=[jax.ShapeDtypeStruct(cols.shape, BF)] + [jax.ShapeDtypeStruct((T, 512), F32)] * 4 + [
            jax.ShapeDtypeStruct(wup.shape, F32), jax.ShapeDtypeStruct(wout.shape, F32),
            jax.ShapeDtypeStruct((1, 1024), F32)],
        grid=(T // TB,),
        in_specs=[yspec] * 4 + [pl.BlockSpec((TB, 4096), lambda i: (i, jgate)),
                                _full(wup.shape), _full(wout.shape), _full((1, 1024)),
                                pl.BlockSpec((TB, 1024), lambda i: (i, 0))],
        out_specs=[pl.BlockSpec((TB, 4096), lambda i: (i, jgate))] + [yspec] * 4 + [
            _full(wup.shape), _full(wout.shape), _full((1, 1024))],
        compiler_params=pltpu.CompilerParams(dimension_semantics=("arbitrary",), vmem_limit_bytes=VMEM_LIMIT),
    )(*ys, cols, wup, wout, gpost, dx)


NB = 256


def prenorm_forward(x, gain, name):
    T, D = x.shape

    def body(x_ref, g_ref, o_ref, ot_ref):
        h = _rms(x_ref[...], g_ref[...])
        o_ref[...] = h.astype(BF)
        ot_ref[...] = h.T.astype(BF)

    return pl.pallas_call(
        body, name=name,
        out_shape=[jax.ShapeDtypeStruct((T, D), BF), jax.ShapeDtypeStruct((D, T), BF)], grid=(T // NB,),
        in_specs=[pl.BlockSpec((NB, D), lambda i: (i, 0)), _full((1, D))],
        out_specs=[pl.BlockSpec((NB, D), lambda i: (i, 0)), pl.BlockSpec((D, NB), lambda i: (0, i))],
        compiler_params=pltpu.CompilerParams(dimension_semantics=("parallel",)),
    )(x, gain)


def prenorm_backward(x, gain, dh, dres, name):
    T = x.shape[0]

    def body(x_ref, g_ref, dh_ref, dr_ref, dx_ref, dg_ref):
        @pl.when(pl.program_id(0) == 0)
        def _():
            dg_ref[...] = jnp.zeros_like(dg_ref)

        _, vjp = jax.vjp(_rms, x_ref[...], g_ref[...])
        dxn, dg = vjp(dh_ref[...])
        dx_ref[...] = dr_ref[...] + dxn
        dg_ref[...] += dg

    spec = pl.BlockSpec((NB, 1024), lambda i: (i, 0))
    return pl.pallas_call(
        body, name=name,
        out_shape=[jax.ShapeDtypeStruct(x.shape, F32), jax.ShapeDtypeStruct((1, 1024), F32)], grid=(T // NB,),
        in_specs=[spec, _full((1, 1024)), spec, spec], out_specs=[spec, _full((1, 1024))],
        compiler_params=pltpu.CompilerParams(dimension_semantics=("arbitrary",)),
    )(x, gain, dh, dres)


def loss_head(y, target, name):
    T, D = y.shape

    def body(y_ref, t_ref, l_ref, d_ref):
        @pl.when(pl.program_id(0) == 0)
        def _():
            l_ref[...] = jnp.zeros_like(l_ref)

        err = y_ref[...] - t_ref[...]
        d_ref[...] = err * (1.0 / D)
        l_ref[...] += jnp.full(l_ref.shape, 0.5 * jnp.sum(jnp.mean(err * err, axis=1, keepdims=True)), F32)

    spec = pl.BlockSpec((NB, D), lambda i: (i, 0))
    return pl.pallas_call(
        body, name=name,
        out_shape=[jax.ShapeDtypeStruct((1, 128), F32), jax.ShapeDtypeStruct(y.shape, F32)], grid=(T // NB,),
        in_specs=[spec, spec], out_specs=[_full((1, 128)), spec],
        compiler_params=pltpu.CompilerParams(dimension_semantics=("arbitrary",)),
    )(y, target)


JB_GATE, JB_XM, JB_DN, JB_SW, JB_GM = 0, 4, 2, 5, 6
_ALIGNED_PIECES = ((5896, 4096), (4872, 512), (5384, 512), (0, 2048), (2048, 8), 504, (3592, 512), (4360, 512),
                   (4104, 128), (4232, 128), 256, (2056, 1024), (3080, 512))
_NATURAL_FROM_ALIGNED = ((5120, 2048), (7168, 8), (9216, 1024), (10240, 512), (7680, 512), (8704, 128), (8832, 128),
                         (8192, 512), (4096, 512), (4608, 512), (0, 4096))


def _natural_range(slots, start, width):
    out = []
    while width > 0:
        j, i = divmod(start, W_IN_SHARD)
        take = min(width, W_IN_SHARD - i)
        out.append(slots[j, :, i:i + take])
        start, width = start + take, width - take
    return out


def _aligned_w_in(slots):
    parts = []
    for piece in _ALIGNED_PIECES:
        if isinstance(piece, int):
            parts.append(jnp.zeros(slots.shape[1:2] + (piece,), slots.dtype))
        else:
            parts += _natural_range(slots, *piece)
    return jnp.concatenate(parts, axis=-1)


def _slots_of_aligned(d_al):
    slots = []
    for s in range(N_DEV):
        lo, hi = s * W_IN_SHARD, (s + 1) * W_IN_SHARD
        parts, nat = [], 0
        for a_start, width in _NATURAL_FROM_ALIGNED:
            b, e = max(lo, nat), min(hi, nat + width)
            if b < e:
                parts.append(d_al[..., a_start + b - nat:a_start + e - nat])
            nat += width
        parts.append(jnp.zeros(d_al.shape[:1] + (W_IN_SHARD_PAD - W_IN_SHARD,), d_al.dtype))
        slots.append(jnp.concatenate(parts, axis=-1))
    return jnp.stack(slots)


SMALL_VEC_W = 1024


def _pack_small(parts):
    rows = []
    for p in parts:
        flat = p.reshape(-1).astype(F32)
        r = -(-flat.shape[0] // SMALL_VEC_W)
        rows.append(jnp.pad(flat, (0, r * SMALL_VEC_W - flat.shape[0])).reshape(r, SMALL_VEC_W))
    vec = jnp.concatenate(rows, axis=0)
    return jnp.pad(vec, ((0, -vec.shape[0] % 8), (0, 0)))


def _unpack_small(vec, shapes):
    out, off = [], 0
    for s in shapes:
        n = math.prod(s)
        r = -(-n // SMALL_VEC_W)
        out.append(vec[off:off + r].reshape(-1)[:n].reshape(s))
        off += r
    return out


def _lanes(vec, at):
    return jnp.zeros((1, 128), F32).at[0, at:at + vec.shape[0]].set(vec)


SMALL_NAMES = ("norm_pre", "norm_post", "norm_mem", "a_log", "dt_bias", "dn_norm", "gm_norm",
               "spatial_w", "spatial_b", "sinks")


def _other_weights(s_mem, s_up, s_out):
    return (s_mem.reshape(D_MODEL, 2 * BRANCH_W),
            jnp.transpose(s_up, (1, 2, 0, 3)).reshape(N_BRANCH, BRANCH_W, D_MODEL), s_out.reshape(D_MODEL, D_MODEL))


def _grad_slots(d_in_al, d_mem, d_up, d_out):
    return [_slots_of_aligned(d_in_al), d_mem.astype(BF).reshape(N_DEV, 128, 2 * BRANCH_W),
            jnp.transpose(d_up.astype(BF).reshape(N_BRANCH, BRANCH_W, N_DEV, 128), (2, 0, 1, 3)),
            d_out.astype(BF).reshape(N_DEV, 128, D_MODEL)]


def _layer_params(l, small, conv_full, token):
    return dict(
        gpre=small["norm_pre"][l][None] + token, gpost=small["norm_post"][l][None], gmem=small["norm_mem"][l][None],
        cw=conv_full[l], al=_lanes(small["a_log"][l], 4), dt=_lanes(small["dt_bias"][l], 4),
        dnn=small["dn_norm"][l][None], gain=small["gm_norm"][l][None], ws=small["spatial_w"][l],
        bt=jnp.zeros((128, 128), F32).at[:, :GM_GROUPS].set(small["spatial_b"][l].T),
        sinks=_lanes(small["sinks"][l], 0))


def _layer_forward(l, xl, mem, p, w_in_al, other_weights):
    t = "l%d_" % l
    h, h_t = prenorm_forward(xl, p["gpre"], t + "prenorm")
    cols = _matmul(h, w_in_al, "nn", F32, (1024, 1536, 1024), t + "w_in")
    w_mem, w_up, w_out = other_weights(cols)
    mkv = memkv_forward(mem, p["gmem"], w_mem, t + "memkv")
    ya, ss, ts = dn_forward(cols, JB_DN, p["cw"], p["al"], p["dt"], p["dnn"], t + "deltanet")
    yb = gm_forward(cols, JB_GM, p["gain"], p["ws"], p["bt"], t + "gmlp")
    yc = sw_forward(cols, JB_SW, p["sinks"], t + "swa")
    ym = xm_forward(cols, JB_XM, mkv, t + "memattn")
    xn = merge_forward([ya, yb, yc, ym], cols, JB_GATE, xl, w_up, w_out, p["gpost"], t + "merge")
    return xn, dict(p, x=xl, h_t=h_t, cols=cols, mkv=mkv, ss=ss, ts=ts, ys=[ya, yb, yc, ym]), (w_in_al, w_mem, w_up, w_out)


def _layer_backward(l, s, mem, weights, dx, token):
    w_in_al, w_mem, w_up, w_out = weights
    t = "l%d_" % l
    cols = s["cols"]
    dcols, dya, dyb, dyc, dym, dwup, dwout, dgpost = merge_backward(
        s["ys"], cols, JB_GATE, w_up, w_out, s["gpost"] + token, dx, t + "merge_bwd")
    dcols, dmkv = xm_backward(cols, JB_XM, s["mkv"], dym, dcols, t + "memattn_bwd")
    dgmem, dwmem = memkv_backward(mem, s["gmem"], w_mem, dmkv, t + "memkv_bwd")
    dcols, dsinks = sw_backward(cols, JB_SW, s["sinks"], dyc, dcols, t + "swa_bwd")
    dcols, dgain, dws, dbt = gm_backward(cols, JB_GM, s["gain"], s["ws"], s["bt"], dyb, dcols, t + "gmlp_bwd")
    dcols, dcw, dal, ddt, ddn = dn_backward(
        cols, JB_DN, s["cw"], s["al"], s["dt"], s["dnn"], s["ss"], s["ts"], dya, dcols, t + "deltanet_bwd")
    dh = _matmul(dcols, w_in_al, "nt", F32, (1024, 1024, 1024), t + "w_in_bwd_x")
    dwin = _matmul(s["h_t"], dcols, "nn", BF, (1024, 1536, 2048), t + "w_in_bwd_w")
    dx, dgpre = prenorm_backward(s["x"], s["gpre"], dh, dx, t + "prenorm_bwd")
    gsmall = dict(norm_pre=dgpre[0], norm_post=dgpost[0], norm_mem=dgmem[0], a_log=dal[0, 4:8], dt_bias=ddt[0, 4:8],
                  dn_norm=ddn[0], gm_norm=dgain[0], spatial_w=dws, spatial_b=dbt[:, :GM_GROUPS].T,
                  sinks=dsinks[0, :SW_HEADS], conv_w=dcw)
    return dx, gsmall, (dwin, dwmem, dwup, dwout)


def kernel(x, mem, norm_pre, norm_post, norm_mem, w_in, conv_w, a_log, dt_bias, dn_norm, gm_norm, spatial_w, spatial_b, sinks, w_mem_kv, w_up, w_out, loss_target, m_norm_pre, m_norm_post, m_norm_mem, m_w_in, m_conv_w, m_a_log, m_dt_bias, m_dn_norm, m_gm_norm, m_spatial_w, m_spatial_b, m_sinks, m_w_mem_kv, m_w_up, m_w_out, v_norm_pre, v_norm_post, v_norm_mem, v_w_in, v_conv_w, v_a_log, v_dt_bias, v_dn_norm, v_gm_norm, v_spatial_w, v_spatial_b, v_sinks, v_w_mem_kv, v_w_up, v_w_out):
    xi, yi, ci = _my_place()
    my_slot = 4 * xi + 2 * yi + ci
    conv_shard = conv_w.shape[-1]
    x2, mem2, target = x[0], mem[0], loss_target[0]

    w_in_pad = jnp.pad(w_in.astype(BF), ((0, 0), (0, 0), (0, W_IN_SHARD_PAD - W_IN_SHARD)))
    shards = [[w_in_pad[l], w_mem_kv[l].astype(BF), w_up[l].astype(BF), w_out[l].astype(BF)] for l in range(DEPTH)]
    *slots0, conv_slots = _all_gather_slots(shards[0] + [conv_w], "gather_weights_l0")
    ag_send, ag_recv, ag_src, ag_land, ag_token = _spread_start(shards[1], "gather", "gather_weights_l1_start")
    conv_full = jnp.transpose(conv_slots, (1, 2, 0, 3)).reshape(DEPTH, CONV_W, N_DEV * conv_shard)
    small = dict(norm_pre=norm_pre, norm_post=norm_post, norm_mem=norm_mem, a_log=a_log,
                 dt_bias=dt_bias, dn_norm=dn_norm, gm_norm=gm_norm, spatial_w=spatial_w,
                 spatial_b=spatial_b, sinks=sinks)

    x1, saved0, weights0 = _layer_forward(0, x2, mem2, _layer_params(0, small, conv_full, ag_token[0, 0]),
                                          _aligned_w_in(slots0[0]), lambda cols: _other_weights(*slots0[1:]))
    ag_src, ag_land = _spread_wait(ag_send, ag_recv, ag_src, ag_land, [0], x1, "gather_weights_l1_wait_w_in")

    def late_weights(cols):
        _, land = _spread_wait(ag_send, ag_recv, ag_src, ag_land, [1, 2, 3], cols, "gather_weights_l1_wait_rest")
        return _other_weights(*land[1:])

    x_out, saved1, weights1 = _layer_forward(1, x1, mem2, _layer_params(1, small, conv_full, 0.0),
                                             _aligned_w_in(ag_land[0]), late_weights)
    loss, dx = loss_head(x_out, target, "loss_head")

    dx, gsmall1, gbig1 = _layer_backward(1, saved1, mem2, weights1, dx, 0.0)
    rs_send, rs_recv, rs_src, rs_land, rs_token = _spread_start(_grad_slots(*gbig1), "scatter", "exchange_grads_l1_start")
    dx, gsmall0, gbig0 = _layer_backward(0, saved0, mem2, weights0, dx, rs_token[0, 0])
    _, parts1 = _spread_wait(rs_send, rs_recv, rs_src, rs_land, range(4), dx, "exchange_grads_l1_wait")

    packed_names = SMALL_NAMES + ("conv_w",)
    gs = {n: jnp.stack([gsmall0[n], gsmall1[n]]) for n in packed_names}
    small_parts = [loss[0, :1]] + [gs[n] for n in packed_names]
    tot = _unpack_small(_all_reduce_vmem(_pack_small(small_parts), "all_reduce_small"), [p.shape for p in small_parts])
    loss_tot = tot[0][0]
    grads = dict(zip(packed_names, tot[1:]))
    grads["conv_w"] = lax.dynamic_slice_in_dim(grads["conv_w"], my_slot * conv_shard, conv_shard, axis=2)

    g_slots = [g.reshape((N_DEV // 2, 2) + g.shape[1:]) for g in _grad_slots(*gbig0)]
    theirs = _exchange_sibling(g_slots, "exchange_sibling_l0")
    chip_sums = [_pair_sum(g, t, "pair_sum_l0_%d" % i) for i, (g, t) in enumerate(zip(g_slots, theirs))]
    ch_send, ch_recv, ch_src, ch_land, ch_token = _spread_start(chip_sums, "chips", "exchange_chips_l0_start")

    given = dict(norm_pre=(norm_pre, m_norm_pre, v_norm_pre), norm_post=(norm_post, m_norm_post, v_norm_post),
                 norm_mem=(norm_mem, m_norm_mem, v_norm_mem), a_log=(a_log, m_a_log, v_a_log),
                 dt_bias=(dt_bias, m_dt_bias, v_dt_bias), dn_norm=(dn_norm, m_dn_norm, v_dn_norm),
                 gm_norm=(gm_norm, m_gm_norm, v_gm_norm), spatial_w=(spatial_w, m_spatial_w, v_spatial_w),
                 spatial_b=(spatial_b, m_spatial_b, v_spatial_b), sinks=(sinks, m_sinks, v_sinks),
                 conv_w=(conv_w, m_conv_w, v_conv_w))
    pshapes = [given[n][0].shape for n in packed_names]
    pw, pm, pv = (_pack_small([given[n][i] for n in packed_names]) for i in range(3))
    pd, pnm, pnv = _adamw(pw + ch_token[0, 0], _pack_small([grads[n] for n in packed_names]), pm, pv, "adamw_small")
    upd = {n: t for n, t in zip(packed_names, zip(_unpack_small(pd, pshapes), _unpack_small(pnm, pshapes),
                                                  _unpack_small(pnv, pshapes)))}
    big = (("w_mem_kv", (w_mem_kv, m_w_mem_kv, v_w_mem_kv)), ("w_up", (w_up, m_w_up, v_w_up)),
           ("w_out", (w_out, m_w_out, v_w_out)))
    first = [_sum_adamw(parts1[1 + i], w, m, v, 1, None, ch_token, "adamw_%s_l1" % name)
             for i, (name, (w, m, v)) in enumerate(big)]
    _, parts0 = _spread_wait(ch_send, ch_recv, ch_src, ch_land, range(4), first[-1][0], "exchange_chips_l0_wait")
    for i, (name, (w, m, v)) in enumerate(big):
        g, d, nm, nv = _sum_adamw(parts0[1 + i], w, m, v, 0, first[i], None, "adamw_%s_l0" % name)
        grads[name], upd[name] = g, (d, nm, nv)
    w_in_t, m_w_in_t, v_w_in_t = (jnp.transpose(t, (2, 0, 1)) for t in (w_in, m_w_in, v_w_in))
    g, d, nm, nv = (jnp.transpose(t, (1, 2, 0)) for t in
                    _sum_adamw_t([parts0[0], parts1[0]], w_in_t, m_w_in_t, v_w_in_t, "adamw_w_in"))
    grads["w_in"], upd["w_in"] = g, (d, nm, nv)

    order = ("norm_pre", "norm_post", "norm_mem", "w_in", "conv_w", "a_log", "dt_bias", "dn_norm",
             "gm_norm", "spatial_w", "spatial_b", "sinks", "w_mem_kv", "w_up", "w_out")
    return (loss_tot, dx[None], *[grads[n] for n in order], *[upd[n][0] for n in order],
            *[upd[n][1] for n in order], *[upd[n][2] for n in order])
```

```python
import functools
import math

import jax
import jax.numpy as jnp
from jax import lax
from jax.experimental import pallas as pl
from jax.experimental.pallas import tpu as pltpu

MESH = pl.DeviceIdType.MESH
N_DEV = 8

D_MODEL = 1024
DEPTH = 2
N_BRANCH = 4
BRANCH_W = 512
DN_HEADS = 4
CONV_W = 4
GM_GROUPS = 4
SW_HEADS = 8
EPS = 1e-6
NEG_INF = -1e30

D_IN = 9992
W_IN_SHARD = D_IN // N_DEV
W_IN_SHARD_PAD = 1280
D_IN_AL = 10752
DN_W, SW_W, GM_W, XM_W = 2560, 1536, 1536, 1024

ADAM_LR = 0.001
ADAM_B1 = 0.9
ADAM_B2 = 0.999
ADAM_EPS = 1e-08
ADAM_WD = 0.01
ADAM_STEP = 10

VMEM_LIMIT = 56 * 1024 * 1024

BF = jnp.bfloat16
F32 = jnp.float32
DN_C = 128
DN_D = 128
HALO = 8
BLK = 128


def _my_place():
    return lax.axis_index("x"), lax.axis_index("y"), lax.axis_index("c")


_ANY = pl.BlockSpec(memory_space=pl.ANY)


def _all_gather_slots(parts, name):
    n = len(parts)

    def body(*refs):
        p_refs, out_refs = refs[:n], refs[n:2 * n]
        send_sems, recv_sems, local_sems = refs[2 * n:]
        x, y, c = _my_place()
        me, sibling = (x, y, c), (x, y, 1 - c)
        chips = [(1 - x, y), (x, 1 - y), (1 - x, 1 - y)]

        def copy(a, k, block, to, src=None):
            px, py, pc = block
            slot = out_refs[a].at[4 * px + 2 * py + pc]
            return pltpu.make_async_remote_copy(
                src_ref=slot if src is None else src, dst_ref=slot,
                send_sem=send_sems.at[7 * a + k], recv_sem=recv_sems.at[7 * a + k],
                device_id=to, device_id_type=MESH)

        mine = [pltpu.make_async_copy(p_refs[a], out_refs[a].at[4 * x + 2 * y + c], local_sems.at[a])
                for a in range(n)]
        for cp in mine:
            cp.start()
        first = []
        for a in range(n):
            first.append(copy(a, 0, me, sibling, src=p_refs[a]))
            first += [copy(a, 1 + j, me, (*chip, c), src=p_refs[a]) for j, chip in enumerate(chips)]
        for cp in first:
            cp.start()
        passed = []
        for j, chip in enumerate(chips):
            for a in range(n):
                copy(a, 1 + j, (*chip, c), me).wait_recv()
                fwd = copy(a, 4 + j, (*chip, c), sibling)
                fwd.start()
                passed.append(fwd)
        for a in range(n):
            copy(a, 0, sibling, me).wait_recv()
            for j, chip in enumerate(chips):
                copy(a, 4 + j, (*chip, 1 - c), me).wait_recv()
        for cp in first + passed:
            cp.wait_send()
        for cp in mine:
            cp.wait()

    return pl.pallas_call(
        body, name=name,
        out_shape=[jax.ShapeDtypeStruct((N_DEV,) + p.shape, p.dtype) for p in parts],
        in_specs=[_ANY] * n, out_specs=[_ANY] * n,
        scratch_shapes=[pltpu.SemaphoreType.DMA((7 * n,)), pltpu.SemaphoreType.DMA((7 * n,)),
                        pltpu.SemaphoreType.DMA((n,))],
    )(*parts)


def _exchange_sibling(parts, name):
    n = len(parts)

    def body(*refs):
        g_refs, out_refs = refs[:n], refs[n:2 * n]
        send_sems, recv_sems = refs[2 * n:]
        x, y, c = _my_place()
        copies = [pltpu.make_async_remote_copy(
            src_ref=g_refs[a].at[:, 1 - c], dst_ref=out_refs[a],
            send_sem=send_sems.at[a], recv_sem=recv_sems.at[a],
            device_id=(x, y, 1 - c), device_id_type=MESH) for a in range(n)]
        for cp in copies:
            cp.start()
        for cp in copies:
            cp.wait()

    return pl.pallas_call(
        body, name=name,
        out_shape=[jax.ShapeDtypeStruct((4,) + g.shape[2:], g.dtype) for g in parts],
        in_specs=[_ANY] * n, out_specs=[_ANY] * n,
        scratch_shapes=[pltpu.SemaphoreType.DMA((n,)), pltpu.SemaphoreType.DMA((n,))],
    )(*parts)


_HBM = pl.BlockSpec(memory_space=pltpu.HBM)
_SEM = pl.BlockSpec(memory_space=pltpu.SEMAPHORE)
_EFFECT = pltpu.SideEffectType.DATAFLOW_SIDE_EFFECTING


def _peer(x, y, c, k):
    return (1 - x if (k >> 2) & 1 else x, 1 - y if (k >> 1) & 1 else y, 1 - c if k & 1 else c)


def _spread_start(srcs, mode, name):
    n = len(srcs)
    lands = [lax.empty((N_DEV,) + s.shape if mode == "gather" else s.shape, s.dtype) for s in srcs]
    peers = range(0, N_DEV, 2) if mode == "chips" else range(N_DEV)

    def body(*refs):
        src_refs, land_refs = refs[:n], refs[n:2 * n]
        send_sems, recv_sems = refs[2 * n:2 * n + 2]
        token = refs[-1]
        x, y, c = _my_place()
        for a in range(n):
            for k in peers:
                px, py, pc = _peer(x, y, c, k)
                if mode == "chips":
                    src, mine = src_refs[a].at[2 * px + py], 2 * x + y
                else:
                    src = src_refs[a].at[4 * px + 2 * py + pc] if mode == "scatter" else src_refs[a]
                    mine = 4 * x + 2 * y + c
                pltpu.make_async_remote_copy(
                    src_ref=src, dst_ref=land_refs[a].at[mine],
                    send_sem=send_sems.at[a], recv_sem=recv_sems.at[a],
                    device_id=(px, py, pc), device_id_type=MESH).start()
        token[...] = jnp.zeros_like(token)

    out = pl.pallas_call(
        body, name=name,
        out_shape=[pltpu.SemaphoreType.DMA((n,)), pltpu.SemaphoreType.DMA((n,))]
        + [pltpu.HBM(s.shape, s.dtype) for s in srcs] + [pltpu.HBM(l.shape, l.dtype) for l in lands]
        + [jax.ShapeDtypeStruct((8, 128), F32)],
        in_specs=[_HBM] * (2 * n),
        out_specs=[_SEM, _SEM] + [_HBM] * (2 * n) + [pl.BlockSpec(memory_space=pltpu.VMEM)],
        input_output_aliases={i: 2 + i for i in range(2 * n)},
        compiler_params=pltpu.CompilerParams(has_side_effects=_EFFECT),
    )(*[pltpu.with_memory_space_constraint(s, pltpu.HBM) for s in srcs],
      *[pltpu.with_memory_space_constraint(l, pltpu.HBM) for l in lands])
    return out[0], out[1], out[2:2 + n], out[2 + n:2 + 2 * n], out[-1]


def _spread_wait(send_sems, recv_sems, srcs, lands, which, after, name):
    n = len(srcs)

    def body(*refs):
        land_refs = refs[n:2 * n]
        send_sems, recv_sems = refs[2 * n:2 * n + 2]
        x, y, c = _my_place()
        for a in which:
            whole = pltpu.make_async_remote_copy(
                src_ref=land_refs[a], dst_ref=land_refs[a],
                send_sem=send_sems.at[a], recv_sem=recv_sems.at[a],
                device_id=(x, y, c), device_id_type=MESH)
            whole.wait_send()
            whole.wait_recv()

    out = pl.pallas_call(
        body, name=name,
        out_shape=[pltpu.HBM(s.shape, s.dtype) for s in srcs] + [pltpu.HBM(l.shape, l.dtype) for l in lands],
        in_specs=[_HBM] * (2 * n) + [_SEM, _SEM, _ANY],
        out_specs=[_HBM] * (2 * n),
        input_output_aliases={i: i for i in range(2 * n)},
        compiler_params=pltpu.CompilerParams(has_side_effects=_EFFECT),
    )(*srcs, *lands, send_sems, recv_sems, after)
    return out[:n], out[n:]


def _all_reduce_vmem(v, name):
    def body(v_ref, out_ref, buf, send_sems, recv_sems):
        x, y, c = _my_place()
        peers = [(x, y, 1 - c), (1 - x, y, c), (x, 1 - y, c)]
        out_ref[...] = v_ref[...]
        for step, peer in enumerate(peers):
            cp = pltpu.make_async_remote_copy(
                src_ref=out_ref, dst_ref=buf.at[step],
                send_sem=send_sems.at[step], recv_sem=recv_sems.at[step],
                device_id=peer, device_id_type=MESH)
            cp.start()
            cp.wait()
            out_ref[...] = out_ref[...] + buf[step]

    return pl.pallas_call(
        body, name=name,
        out_shape=jax.ShapeDtypeStruct(v.shape, v.dtype),
        in_specs=[pl.BlockSpec(memory_space=pltpu.VMEM)],
        out_specs=pl.BlockSpec(memory_space=pltpu.VMEM),
        scratch_shapes=[pltpu.VMEM((3,) + v.shape, v.dtype),
                        pltpu.SemaphoreType.DMA((3,)), pltpu.SemaphoreType.DMA((3,))],
    )(v)


def _pick(n, pref):
    if n <= pref:
        return n
    t = pref - pref % 128
    while t > 0 and n % t:
        t -= 128
    return t if t > 0 else n


_DIMS = {"nn": (((1,), (0,)), ((), ())),
         "nt": (((1,), (1,)), ((), ())),
         "tn": (((0,), (0,)), ((), ()))}


def _matmul(a, b, mode, out_dtype, tiles, name):
    (m, k) = a.shape
    n = b.shape[1] if mode == "nn" else b.shape[0]
    tm, tn, tk = (_pick(d, t) for d, t in zip((m, n, k), tiles))
    nk = k // tk

    def product(a_ref, b_ref):
        return lax.dot_general(a_ref[...].astype(BF), b_ref[...].astype(BF), _DIMS[mode], preferred_element_type=F32)

    def body_whole_k(a_ref, b_ref, o_ref):
        o_ref[...] = product(a_ref, b_ref).astype(o_ref.dtype)

    def body_split_k(a_ref, b_ref, o_ref, acc_ref):
        kk = pl.program_id(2)

        @pl.when(kk == 0)
        def _():
            acc_ref[...] = jnp.zeros_like(acc_ref)

        acc_ref[...] += product(a_ref, b_ref)

        @pl.when(kk == nk - 1)
        def _():
            o_ref[...] = acc_ref[...].astype(o_ref.dtype)

    b_spec = (pl.BlockSpec((tn, tk), lambda i, j, kk: (j, kk)) if mode == "nt"
              else pl.BlockSpec((tk, tn), lambda i, j, kk: (kk, j)))
    return pl.pallas_call(
        body_whole_k if nk == 1 else body_split_k, name=name,
        out_shape=jax.ShapeDtypeStruct((m, n), out_dtype),
        grid=(m // tm, n // tn, nk),
        in_specs=[pl.BlockSpec((tm, tk), lambda i, j, kk: (i, kk)), b_spec],
        out_specs=pl.BlockSpec((tm, tn), lambda i, j, kk: (i, j)),
        scratch_shapes=[] if nk == 1 else [pltpu.VMEM((tm, tn), F32)],
        compiler_params=pltpu.CompilerParams(
            dimension_semantics=("parallel", "parallel", "arbitrary"),
            vmem_limit_bytes=VMEM_LIMIT),
    )(a, b)


def _rows2d(t, lead):
    return t.reshape(t.shape[:lead] + (math.prod(t.shape[lead:-1]), t.shape[-1]))


def _pair_sum(g, theirs, name):
    g3, t3 = _rows2d(g, 2), _rows2d(theirs, 1)
    _, r, w = t3.shape
    tr = _pick(r, 512)

    def body(g_ref, t_ref, o_ref):
        c = lax.axis_index("c")
        mine = jnp.where(c == 0, g_ref[0, 0], g_ref[0, 1])
        o_ref[0] = (mine.astype(F32) + t_ref[0].astype(F32)).astype(o_ref.dtype)

    out = pl.pallas_call(
        body, name=name,
        out_shape=jax.ShapeDtypeStruct(t3.shape, t3.dtype),
        grid=(4, r // tr),
        in_specs=[pl.BlockSpec((1, 2, tr, w), lambda q, i: (q, 0, i, 0)),
                  pl.BlockSpec((1, tr, w), lambda q, i: (q, i, 0))],
        out_specs=pl.BlockSpec((1, tr, w), lambda q, i: (q, i, 0)),
        compiler_params=pltpu.CompilerParams(dimension_semantics=("parallel", "parallel")),
    )(g3, t3)
    return out.reshape(theirs.shape)


def _adam_update(w, g, m, v):
    c1 = 1.0 - ADAM_B1 ** ADAM_STEP
    c2 = 1.0 - ADAM_B2 ** ADAM_STEP
    nm = ADAM_B1 * m + (1.0 - ADAM_B1) * g
    nv = ADAM_B2 * v + (1.0 - ADAM_B2) * (g * g)
    delta = -ADAM_LR * ((nm / c1) / (jnp.sqrt(nv / c2) + ADAM_EPS) + ADAM_WD * w)
    return delta, nm, nv


def _sum_adamw(parts, w, m, v, layer, carry, after, name):
    shape = w.shape
    cols = shape[-1]
    p3 = _rows2d(parts, 1)
    w3, m3, v3 = (_rows2d(t, 1) for t in (w, m, v))
    rows = w3.shape[1]
    tr = _pick(rows, 128)
    n_parts = p3.shape[0]

    def body(p_ref, w_ref, m_ref, v_ref, *rest):
        g_ref, d_ref, nm_ref, nv_ref = rest[-4:]
        g = p_ref[0, :, :cols].astype(F32)
        for q in range(1, n_parts):
            g = g + p_ref[q, :, :cols].astype(F32)
        d, nm, nv = _adam_update(w_ref[0], g, m_ref[0], v_ref[0])
        g_ref[0] = g
        d_ref[0] = d
        nm_ref[0] = nm
        nv_ref[0] = nv

    spec = pl.BlockSpec((1, tr, cols), lambda i: (layer, i, 0))
    extra = [] if carry is None else [_rows2d(t, 1) for t in carry]
    tail = [] if after is None else [after]
    out = pl.pallas_call(
        body, name=name,
        out_shape=[jax.ShapeDtypeStruct(w3.shape, F32)] * 4,
        grid=(rows // tr,),
        in_specs=[pl.BlockSpec((n_parts, tr, p3.shape[-1]), lambda i: (0, i, 0)), spec, spec, spec] + [_ANY] * len(extra + tail),
        out_specs=[spec] * 4,
        input_output_aliases={4 + i: i for i in range(len(extra))},
        compiler_params=pltpu.CompilerParams(dimension_semantics=("parallel",)),
    )(p3, w3, m3, v3, *extra, *tail)
    return tuple(t.reshape(shape) for t in out)


def _sum_adamw_t(parts, w, m, v, name):
    rows = parts[0].shape[2]
    tr = 128
    assert rows % tr == 0 and rows >= w.shape[0]

    def body(*refs):
        p_refs, (w_ref, m_ref, v_ref), (g_ref, d_ref, nm_ref, nv_ref) = refs[:DEPTH], refs[DEPTH:DEPTH + 3], refs[DEPTH + 3:]
        for l in range(DEPTH):
            g = p_refs[l][0].astype(F32)
            for q in range(1, p_refs[l].shape[0]):
                g = g + p_refs[l][q].astype(F32)
            g = g.T
            d, nm, nv = _adam_update(w_ref[:, l, :], g, m_ref[:, l, :], v_ref[:, l, :])
            g_ref[:, l, :] = g
            d_ref[:, l, :] = d
            nm_ref[:, l, :] = nm
            nv_ref[:, l, :] = nv

    spec = pl.BlockSpec((tr,) + w.shape[1:], lambda i: (i, 0, 0))
    return pl.pallas_call(
        body, name=name,
        out_shape=[jax.ShapeDtypeStruct(w.shape, F32)] * 4,
        grid=(rows // tr,),
        in_specs=[pl.BlockSpec((p.shape[0], p.shape[1], tr), lambda i: (0, 0, i)) for p in parts] + [spec] * 3,
        out_specs=[spec] * 4,
        compiler_params=pltpu.CompilerParams(dimension_semantics=("parallel",)),
    )(*parts, w, m, v)


def _adamw(w, g, m, v, name):
    rows, cols = w.shape
    tr = _pick(rows, 128)

    def body(w_ref, g_ref, m_ref, v_ref, d_ref, nm_ref, nv_ref):
        d, nm, nv = _adam_update(w_ref[...], g_ref[...], m_ref[...], v_ref[...])
        d_ref[...] = d
        nm_ref[...] = nm
        nv_ref[...] = nv

    spec = pl.BlockSpec((tr, cols), lambda i: (i, 0))
    return pl.pallas_call(
        body, name=name,
        out_shape=[jax.ShapeDtypeStruct((rows, cols), F32)] * 3,
        grid=(rows // tr,),
        in_specs=[spec] * 4, out_specs=[spec] * 3,
        compiler_params=pltpu.CompilerParams(dimension_semantics=("parallel",)),
    )(w, g, m, v)


_VJP = {"nn": (("nt", "gb"), ("tn", "ag")),
        "nt": (("nn", "gb"), ("tn", "ga")),
        "tn": (("nt", "bg"), ("nn", "ag"))}


def _make_dot(cast, precision):
    def raw(mode, a, b):
        return lax.dot_general(cast(a), cast(b), _DIMS[mode], precision=precision,
                               preferred_element_type=F32)

    @functools.partial(jax.custom_vjp, nondiff_argnums=(0,))
    def dot(mode, a, b):
        return raw(mode, a, b)

    def fwd(mode, a, b):
        return raw(mode, a, b), (a, b)

    def bwd(mode, res, g):
        a, b = res
        pick = {"a": a, "b": b, "g": g}
        (ma, ta), (mb, tb) = _VJP[mode]
        return dot(ma, pick[ta[0]], pick[ta[1]]), dot(mb, pick[tb[0]], pick[tb[1]])

    dot.defvjp(fwd, bwd)
    return dot


bdot = _make_dot(lambda t: t.astype(BF), None)
hdot = _make_dot(lambda t: t, lax.Precision.HIGHEST)


def _xdot(mode, a, b):
    return lax.dot_general(a, b, _DIMS[mode], precision=lax.Precision.HIGH, preferred_element_type=F32)


def _unit_lower_inverse(Ls):
    n = Ls[0].shape[0]
    batched = (((2,), (1,)), ((0,), (0,)))
    mm = lambda a, b: lax.dot_general(a, b, batched, precision=lax.Precision.HIGH, preferred_element_type=F32)
    eye = (lax.broadcasted_iota(jnp.int32, (n, n), 0) == lax.broadcasted_iota(jnp.int32, (n, n), 1)).astype(F32)
    p = jnp.stack(Ls)
    t_inv = eye[None] - p
    for _ in range(6):
        p = mm(p, p)
        t_inv = t_inv + mm(t_inv, p)
    return [t_inv[h] for h in range(len(Ls))]


@jax.custom_vjp
def _tri_solve(L, rhs, t_inv):
    return _xdot("nn", t_inv, rhs)


def _tri_solve_fwd(L, rhs, t_inv):
    sol = _xdot("nn", t_inv, rhs)
    return sol, (t_inv, sol)


def _tri_solve_bwd(res, dsol):
    t_inv, sol = res
    drhs = _xdot("tn", t_inv, dsol)
    return -_xdot("nt", drhs, sol), drhs, jnp.zeros_like(t_inv)


_tri_solve.defvjp(_tri_solve_fwd, _tri_solve_bwd)


def _sigmoid(x):
    return 1.0 / (1.0 + jnp.exp(-x))


def _softplus(x):
    return jnp.maximum(x, 0.0) + jnp.log(1.0 + jnp.exp(-jnp.abs(x)))


def _dn_chunk(S, xs, ba, z, cw, al, dt, dn, t_saved=None):
    C = DN_C
    pre = xs[0] * cw[0] + xs[1] * cw[1] + xs[2] * cw[2] + xs[3] * cw[3]
    qkv = pre * _sigmoid(pre)
    lane = lax.broadcasted_iota(jnp.int32, (1, 128), 1)
    sub = lax.broadcasted_iota(jnp.int32, (C, 1), 0)
    row_i = lax.broadcasted_iota(jnp.int32, (C, C), 0)
    col_i = lax.broadcasted_iota(jnp.int32, (C, C), 1)
    strict = row_i > col_i
    incl = row_i >= col_i
    g_all = jnp.where((lane >= 4) & (lane < 8), -jnp.exp(al) * _softplus(ba + dt), 0.0)
    gc_all = hdot("nn", incl.astype(F32), g_all)
    gc_all_t = gc_all.T
    beta_all = _sigmoid(ba)
    glast_all = jnp.sum(jnp.where(sub == C - 1, gc_all, 0.0), axis=0, keepdims=True)
    heads = []
    for h in range(DN_HEADS):
        q = qkv[:, 128 * h:128 * (h + 1)]
        k = qkv[:, 512 + 128 * h:512 + 128 * (h + 1)]
        v = qkv[:, 1024 + 128 * h:1024 + 128 * (h + 1)]
        q = q * lax.rsqrt(jnp.sum(q * q, axis=1, keepdims=True) + EPS) * (DN_D ** -0.5)
        k = k * lax.rsqrt(jnp.sum(k * k, axis=1, keepdims=True) + EPS)
        beta = jnp.sum(jnp.where(lane == h, beta_all, 0.0), axis=1, keepdims=True)
        gc = jnp.sum(jnp.where(lane == 4 + h, gc_all, 0.0), axis=1, keepdims=True)
        gc_row = jnp.sum(jnp.where(sub == 4 + h, gc_all_t, 0.0), axis=0, keepdims=True)
        g_last = jnp.sum(jnp.where(lane == 4 + h, glast_all, 0.0), axis=1, keepdims=True)
        diff = gc - gc_row
        kb = k * beta
        L = jnp.where(strict, bdot("nt", kb, k) * jnp.exp(jnp.where(strict, diff, 0.0)), 0.0)
        heads.append((q, k, v, beta, gc, g_last, diff, kb, L))
    t_invs = _unit_lower_inverse([hd[-1] for hd in heads]) if t_saved is None else t_saved
    ys, s_new = [], []
    for h, (q, k, v, beta, gc, g_last, diff, kb, L) in enumerate(heads):
        sol = _tri_solve(L, jnp.concatenate([v * beta, kb * jnp.exp(gc)], axis=1), t_invs[h])
        u, w = sol[:, :DN_D], sol[:, DN_D:]
        a_qk = jnp.where(incl, bdot("nt", q, k) * jnp.exp(jnp.where(incl, diff, 0.0)), 0.0)
        qg = q * jnp.exp(gc)
        kd = k * jnp.exp(g_last - gc)
        v_new = u - bdot("nn", w, S[h])
        o = bdot("nn", qg, S[h]) + bdot("nn", a_qk, v_new)
        s_new.append(S[h] * jnp.exp(g_last) + bdot("tn", kd, v_new))
        o = o * lax.rsqrt(jnp.mean(o * o, axis=1, keepdims=True) + EPS) * dn
        zh = z[:, 128 * h:128 * (h + 1)]
        ys.append(o * (zh * _sigmoid(zh)))
    return jnp.concatenate(ys, axis=1), tuple(s_new), tuple(t_invs)


def _load_shifted(xbuf, x_ref, halo_ref, first):
    xbuf[0:HALO, :] = jnp.where(first, 0.0, halo_ref[:, 0:1536])
    xbuf[HALO:HALO + DN_C, :] = x_ref[:, 0:1536]
    return [xbuf[HALO - 3 + k:HALO - 3 + k + DN_C, :] for k in range(4)]


def dn_forward(cols, jblk, cw, al, dt, dn, name):
    T = cols.shape[0]
    n = T // DN_C

    def body(x_ref, halo_ref, cw_ref, al_ref, dt_ref, dn_ref, y_ref, ss_ref, ts_ref, s_scr, xbuf):
        i = pl.program_id(0)

        @pl.when(i == 0)
        def _():
            s_scr[...] = jnp.zeros_like(s_scr)

        xs = _load_shifted(xbuf, x_ref, halo_ref, i == 0)
        ss_ref[0] = s_scr[...]
        S = [s_scr[h] for h in range(DN_HEADS)]
        cws = [cw_ref[k:k + 1, :] for k in range(4)]
        y, s_new, t_invs = _dn_chunk(S, xs, x_ref[:, 2048:2176], x_ref[:, 1536:2048], cws,
                                     al_ref[...], dt_ref[...], dn_ref[...])
        y_ref[...] = y
        for h in range(DN_HEADS):
            s_scr[h] = s_new[h]
            ts_ref[0, h] = t_invs[h]

    per = DN_C // HALO
    full = lambda shape: pl.BlockSpec(shape, lambda i: (0,) * len(shape))
    return pl.pallas_call(
        body, name=name,
        out_shape=[jax.ShapeDtypeStruct((T, 512), F32),
                   jax.ShapeDtypeStruct((n, DN_HEADS, DN_D, DN_D), F32),
                   jax.ShapeDtypeStruct((n, DN_HEADS, DN_D, DN_D), F32)],
        grid=(n,),
        in_specs=[pl.BlockSpec((DN_C, DN_W), lambda i: (i, jblk)),
                  pl.BlockSpec((HALO, DN_W), lambda i: (jnp.maximum(i * per - 1, 0), jblk)),
                  full((4, 1536)), full((1, 128)), full((1, 128)), full((1, 128))],
        out_specs=[pl.BlockSpec((DN_C, 512), lambda i: (i, 0)),
                   pl.BlockSpec((1, DN_HEADS, DN_D, DN_D), lambda i: (i, 0, 0, 0)),
                   pl.BlockSpec((1, DN_HEADS, DN_D, DN_D), lambda i: (i, 0, 0, 0))],
        scratch_shapes=[pltpu.VMEM((DN_HEADS, DN_D, DN_D), F32), pltpu.VMEM((HALO + DN_C, 1536), F32)],
        compiler_params=pltpu.CompilerParams(dimension_semantics=("arbitrary",)),
    )(cols, cols, cw, al, dt, dn)


def dn_backward(cols, jblk, cw, al, dt, dn, ss, ts, dy, dcols, name):
    T = cols.shape[0]
    n = T // DN_C

    def body(x_ref, halo_ref, cw_ref, al_ref, dt_ref, dn_ref, ss_ref, ts_ref, dy_ref, dcols_in,
             dx_ref, dcw_ref, dal_ref, ddt_ref, ddn_ref, ds_scr, xbuf, dbuf, carry):
        i = pl.program_id(0)

        @pl.when(i == 0)
        def _():
            ds_scr[...] = jnp.zeros_like(ds_scr)
            carry[...] = jnp.zeros_like(carry)
            dcw_ref[...] = jnp.zeros_like(dcw_ref)
            dal_ref[...] = jnp.zeros_like(dal_ref)
            ddt_ref[...] = jnp.zeros_like(ddt_ref)
            ddn_ref[...] = jnp.zeros_like(ddn_ref)

        xs = _load_shifted(xbuf, x_ref, halo_ref, i == n - 1)
        S = [ss_ref[0, h] for h in range(DN_HEADS)]
        cws = [cw_ref[k:k + 1, :] for k in range(4)]

        t_saved = [ts_ref[0, h] for h in range(DN_HEADS)]

        def f(S, xs, ba, z, cws, al, dt, dn):
            return _dn_chunk(S, xs, ba, z, cws, al, dt, dn, t_saved)[:2]

        _, vjp = jax.vjp(f, S, xs, x_ref[:, 2048:2176], x_ref[:, 1536:2048], cws, al_ref[...], dt_ref[...], dn_ref[...])
        dS, dxs, dba, dz, dcws, dal, ddt, ddn = vjp((dy_ref[...], tuple(ds_scr[h] for h in range(DN_HEADS))))
        for h in range(DN_HEADS):
            ds_scr[h] = dS[h]
        dbuf[...] = jnp.zeros_like(dbuf)
        for k in range(4):
            lo = HALO - 3 + k
            dbuf[lo:lo + DN_C, :] += dxs[k]
        dbuf[DN_C:DN_C + HALO, :] += carry[...]
        dx_ref[...] = jnp.concatenate([dbuf[HALO:HALO + DN_C, :], dz, dba,
                                       jnp.zeros((DN_C, DN_W - 2176), F32)], axis=1).astype(dx_ref.dtype)
        carry[...] = dbuf[0:HALO, :]
        for k in range(4):
            dcw_ref[k:k + 1, :] += dcws[k]
        dal_ref[...] += dal
        ddt_ref[...] += ddt
        ddn_ref[...] += ddn

    per = DN_C // HALO
    rev = lambda i: n - 1 - i
    full = lambda shape: pl.BlockSpec(shape, lambda i: (0,) * len(shape))
    return pl.pallas_call(
        body, name=name,
        out_shape=[jax.ShapeDtypeStruct(dcols.shape, dcols.dtype),jax.ShapeDtypeStruct((4, 1536), F32),
                   jax.ShapeDtypeStruct((1, 128), F32), jax.ShapeDtypeStruct((1, 128), F32),
                   jax.ShapeDtypeStruct((1, 128), F32)],
        grid=(n,),
        in_specs=[pl.BlockSpec((DN_C, DN_W), lambda i: (rev(i), jblk)),
                  pl.BlockSpec((HALO, DN_W), lambda i: (jnp.maximum(rev(i) * per - 1, 0), jblk)),
                  full((4, 1536)), full((1, 128)), full((1, 128)), full((1, 128)),
                  pl.BlockSpec((1, DN_HEADS, DN_D, DN_D), lambda i: (rev(i), 0, 0, 0)),
                  pl.BlockSpec((1, DN_HEADS, DN_D, DN_D), lambda i: (rev(i), 0, 0, 0)),
                  pl.BlockSpec((DN_C, 512), lambda i: (rev(i), 0)), _ANY],
        out_specs=[pl.BlockSpec((DN_C, DN_W), lambda i: (rev(i), jblk)),
                   full((4, 1536)), full((1, 128)), full((1, 128)), full((1, 128))],
        scratch_shapes=[pltpu.VMEM((DN_HEADS, DN_D, DN_D), F32), pltpu.VMEM((HALO + DN_C, 1536), F32),
                        pltpu.VMEM((HALO + DN_C, 1536), F32), pltpu.VMEM((HALO, 1536), F32)],
        input_output_aliases={9: 0},
        compiler_params=pltpu.CompilerParams(dimension_semantics=("arbitrary",)),
    )(cols, cols, cw, al, dt, dn, ss, ts, dy, dcols)


def _full(shape):
    return pl.BlockSpec(shape, lambda i: (0,) * len(shape))


def _silu(x):
    return x * _sigmoid(x)


def _gelu(x):
    return 0.5 * x * (1.0 + jnp.tanh(0.7978845608028654 * (x + 0.044715 * (x * x * x))))


def _lane_col(mat, idx):
    lane = lax.broadcasted_iota(jnp.int32, (1, mat.shape[1]), 1)
    return jnp.sum(jnp.where(lane == idx, mat, 0.0), axis=1, keepdims=True)


def _gm_chunk(uv, z, gain, ws, bt):
    g = _gelu(uv)
    u, v = g[:, :512], g[:, 512:]
    v = v * lax.rsqrt(jnp.mean(v * v, axis=1, keepdims=True) + EPS) * gain
    row_i = lax.broadcasted_iota(jnp.int32, (BLK, BLK), 0)
    col_i = lax.broadcasted_iota(jnp.int32, (BLK, BLK), 1)
    causal = row_i >= col_i
    ss = []
    for grp in range(4):
        wg = jnp.where(causal, ws[grp], 0.0)
        ss.append(bdot("nn", wg, v[:, BLK * grp:BLK * (grp + 1)]) + _lane_col(bt, grp))
    return u * jnp.concatenate(ss, axis=1) * _silu(z)


def gm_forward(cols, jblk, gain, ws, bt, name):
    T = cols.shape[0]

    def body(x_ref, gain_ref, ws_ref, bt_ref, y_ref):
        y_ref[...] = _gm_chunk(x_ref[:, 0:1024], x_ref[:, 1024:1536], gain_ref[...],
                               [ws_ref[g] for g in range(4)], bt_ref[...])

    return pl.pallas_call(
        body, name=name, out_shape=jax.ShapeDtypeStruct((T, 512), F32), grid=(T // BLK,),
        in_specs=[pl.BlockSpec((BLK, GM_W), lambda i: (i, jblk)),
                  _full((1, 512)), _full((4, BLK, BLK)), _full((BLK, BLK))],
        out_specs=pl.BlockSpec((BLK, 512), lambda i: (i, 0)),
        compiler_params=pltpu.CompilerParams(dimension_semantics=("parallel",)),
    )(cols, gain, ws, bt)


def gm_backward(cols, jblk, gain, ws, bt, dy, dcols, name):
    T = cols.shape[0]

    def body(x_ref, gain_ref, ws_ref, bt_ref, dy_ref, dcols_in, dx_ref, dgain_ref, dws_ref, dbt_ref):
        @pl.when(pl.program_id(0) == 0)
        def _():
            dgain_ref[...] = jnp.zeros_like(dgain_ref)
            dws_ref[...] = jnp.zeros_like(dws_ref)
            dbt_ref[...] = jnp.zeros_like(dbt_ref)

        _, vjp = jax.vjp(_gm_chunk, x_ref[:, 0:1024], x_ref[:, 1024:1536], gain_ref[...],
                         [ws_ref[g] for g in range(4)], bt_ref[...])
        duv, dz, dgain, dws, dbt = vjp(dy_ref[...])
        dx_ref[...] = jnp.concatenate([duv, dz], axis=1).astype(dx_ref.dtype)
        dgain_ref[...] += dgain
        for g in range(4):
            dws_ref[g] += dws[g]
        dbt_ref[...] += dbt

    return pl.pallas_call(
        body, name=name,
        out_shape=[jax.ShapeDtypeStruct(dcols.shape, dcols.dtype),jax.ShapeDtypeStruct((1, 512), F32),
                   jax.ShapeDtypeStruct((4, BLK, BLK), F32), jax.ShapeDtypeStruct((BLK, BLK), F32)],
        grid=(T // BLK,),
        in_specs=[pl.BlockSpec((BLK, GM_W), lambda i: (i, jblk)),
                  _full((1, 512)), _full((4, BLK, BLK)), _full((BLK, BLK)),
                  pl.BlockSpec((BLK, 512), lambda i: (i, 0)), _ANY],
        out_specs=[pl.BlockSpec((BLK, GM_W), lambda i: (i, jblk)),
                   _full((1, 512)), _full((4, BLK, BLK)), _full((BLK, BLK))],
        input_output_aliases={5: 0},
        compiler_params=pltpu.CompilerParams(dimension_semantics=("arbitrary",)),
    )(cols, gain, ws, bt, dy, dcols)


def _sw_block(first, q, kp, kc, vp, vc, z, sinks):
    P = BLK
    lane = lax.broadcasted_iota(jnp.int32, (1, 128), 1)
    r = lax.broadcasted_iota(jnp.int32, (128, 128), 0)
    c = lax.broadcasted_iota(jnp.int32, (128, 128), 1)
    swap = (c == (r + 64) % 128).astype(F32)
    k2 = jnp.concatenate([kp, kc], axis=0)
    v2 = jnp.concatenate([vp, vc], axis=0)
    k2s = bdot("nn", k2, swap)
    v2s = bdot("nn", v2, swap)
    qi = lax.broadcasted_iota(jnp.int32, (P, 2 * P), 0)
    kj = lax.broadcasted_iota(jnp.int32, (P, 2 * P), 1)
    dist = qi + P - kj
    valid = (dist >= 0) & (dist < P) & ((kj >= P) | jnp.logical_not(first))
    outs = []
    for j in range(4):
        acc = jnp.zeros((P, 128), F32)
        for half in range(2):
            h = 2 * j + half
            kv = h // 4
            in_half = (lane >= 64 * half) & (lane < 64 * half + 64)
            qh = jnp.where(in_half, q[:, 128 * j:128 * (j + 1)], 0.0)
            same = (half == kv)
            s = bdot("nt", qh, k2 if same else k2s) * (64 ** -0.5)
            s = jnp.where(valid, s, NEG_INF)
            sink = _lane_col(sinks, h)
            m = lax.stop_gradient(jnp.maximum(jnp.max(s, axis=1, keepdims=True), sink))
            e = jnp.exp(s - m)
            p = e / (jnp.sum(e, axis=1, keepdims=True) + jnp.exp(sink - m))
            o = bdot("nn", p, v2 if same else v2s)
            acc = acc + jnp.where(in_half, o, 0.0)
        outs.append(acc)
    return jnp.concatenate(outs, axis=1) * _silu(z)


def _sw_specs(jblk, idx):
    prev = lambda i: jnp.maximum(idx(i) - 1, 0)
    jk = (jblk * SW_W + 1024) // 128
    return [pl.BlockSpec((BLK, SW_W), lambda i: (idx(i), jblk)),
            pl.BlockSpec((BLK, 128), lambda i: (prev(i), jk)),
            pl.BlockSpec((BLK, 128), lambda i: (prev(i), jk + 1)), _full((1, 128))]


def sw_forward(cols, jblk, sinks, name):
    T = cols.shape[0]

    def body(x_ref, kp_ref, vp_ref, s_ref, y_ref):
        y_ref[...] = _sw_block(pl.program_id(0) == 0, x_ref[:, 0:512], kp_ref[...], x_ref[:, 1024:1152],
                               vp_ref[...], x_ref[:, 1152:1280], x_ref[:, 512:1024], s_ref[...])

    return pl.pallas_call(
        body, name=name, out_shape=jax.ShapeDtypeStruct((T, 512), F32), grid=(T // BLK,),
        in_specs=_sw_specs(jblk, lambda i: i),
        out_specs=pl.BlockSpec((BLK, 512), lambda i: (i, 0)),
        compiler_params=pltpu.CompilerParams(dimension_semantics=("parallel",)),
    )(cols, cols, cols, sinks)


def sw_backward(cols, jblk, sinks, dy, dcols, name):
    T = cols.shape[0]
    n = T // BLK
    rev = lambda i: n - 1 - i

    def body(x_ref, kp_ref, vp_ref, s_ref, dy_ref, dcols_in, dx_ref, ds_ref, kcarry, vcarry):
        i = pl.program_id(0)

        @pl.when(i == 0)
        def _():
            kcarry[...] = jnp.zeros_like(kcarry)
            vcarry[...] = jnp.zeros_like(vcarry)
            ds_ref[...] = jnp.zeros_like(ds_ref)

        f = functools.partial(_sw_block, i == n - 1)
        _, vjp = jax.vjp(f, x_ref[:, 0:512], kp_ref[...], x_ref[:, 1024:1152], vp_ref[...], x_ref[:, 1152:1280],
                         x_ref[:, 512:1024], s_ref[...])
        dq, dkp, dkc, dvp, dvc, dz, dsk = vjp(dy_ref[...])
        dx_ref[...] = jnp.concatenate([dq, dz, dkc + kcarry[...], dvc + vcarry[...],
                                       jnp.zeros((BLK, SW_W - 1280), F32)], axis=1).astype(dx_ref.dtype)
        kcarry[...] = dkp
        vcarry[...] = dvp
        ds_ref[...] += dsk

    return pl.pallas_call(
        body, name=name,
        out_shape=[jax.ShapeDtypeStruct(dcols.shape, dcols.dtype),jax.ShapeDtypeStruct((1, 128), F32)],
        grid=(n,),
        in_specs=_sw_specs(jblk, rev) + [pl.BlockSpec((BLK, 512), lambda i: (rev(i), 0)), _ANY],
        out_specs=[pl.BlockSpec((BLK, SW_W), lambda i: (rev(i), jblk)), _full((1, 128))],
        scratch_shapes=[pltpu.VMEM((BLK, 128), F32), pltpu.VMEM((BLK, 128), F32)],
        input_output_aliases={5: 0},
        compiler_params=pltpu.CompilerParams(dimension_semantics=("arbitrary",)),
    )(cols, cols, cols, sinks, dy, dcols)


XM_TQ = 256


def _xm_block(q, z, mkv):
    outs = []
    for h in range(4):
        s = bdot("nt", q[:, 128 * h:128 * (h + 1)], mkv[:, 128 * h:128 * (h + 1)]) * (128 ** -0.5)
        m = lax.stop_gradient(jnp.max(s, axis=1, keepdims=True))
        e = jnp.exp(s - m)
        p = e / jnp.sum(e, axis=1, keepdims=True)
        outs.append(bdot("nn", p, mkv[:, 512 + 128 * h:512 + 128 * (h + 1)]))
    return jnp.concatenate(outs, axis=1) * _silu(z)


def xm_forward(cols, jblk, mkv, name):
    T = cols.shape[0]

    def body(x_ref, m_ref, y_ref):
        y_ref[...] = _xm_block(x_ref[:, 0:512], x_ref[:, 512:1024], m_ref[...])

    return pl.pallas_call(
        body, name=name, out_shape=jax.ShapeDtypeStruct((T, 512), F32), grid=(T // XM_TQ,),
        in_specs=[pl.BlockSpec((XM_TQ, XM_W), lambda i: (i, jblk)), _full(mkv.shape)],
        out_specs=pl.BlockSpec((XM_TQ, 512), lambda i: (i, 0)),
        compiler_params=pltpu.CompilerParams(dimension_semantics=("parallel",)),
    )(cols, mkv)


def xm_backward(cols, jblk, mkv, dy, dcols, name):
    T = cols.shape[0]

    def body(x_ref, m_ref, dy_ref, dcols_in, dx_ref, dm_ref):
        @pl.when(pl.program_id(0) == 0)
        def _():
            dm_ref[...] = jnp.zeros_like(dm_ref)

        _, vjp = jax.vjp(_xm_block, x_ref[:, 0:512], x_ref[:, 512:1024], m_ref[...])
        dq, dz, dm = vjp(dy_ref[...])
        dx_ref[...] = jnp.concatenate([dq, dz], axis=1).astype(dx_ref.dtype)
        dm_ref[...] += dm

    return pl.pallas_call(
        body, name=name,
        out_shape=[jax.ShapeDtypeStruct(dcols.shape, dcols.dtype),jax.ShapeDtypeStruct(mkv.shape, F32)],
        grid=(T // XM_TQ,),
        in_specs=[pl.BlockSpec((XM_TQ, XM_W), lambda i: (i, jblk)), _full(mkv.shape),
                  pl.BlockSpec((XM_TQ, 512), lambda i: (i, 0)), _ANY],
        out_specs=[pl.BlockSpec((XM_TQ, XM_W), lambda i: (i, jblk)), _full(mkv.shape)],
        input_output_aliases={3: 0},
        compiler_params=pltpu.CompilerParams(dimension_semantics=("arbitrary",)),
    )(cols, mkv, dy, dcols)


def _rms(x, gain):
    return x * lax.rsqrt(jnp.mean(x * x, axis=1, keepdims=True) + EPS) * gain


def memkv_forward(mem, gain, w, name):
    def body(m_ref, g_ref, w_ref, o_ref):
        o_ref[...] = bdot("nn", _rms(m_ref[...], g_ref[...]), w_ref[...])

    return pl.pallas_call(body, name=name, out_shape=jax.ShapeDtypeStruct(mem.shape, F32),
                          compiler_params=pltpu.CompilerParams(vmem_limit_bytes=VMEM_LIMIT))(mem, gain, w)


def memkv_backward(mem, gain, w, dkv, name):
    def body(m_ref, g_ref, w_ref, d_ref, dg_ref, dw_ref):
        mem_v = m_ref[...]
        _, vjp = jax.vjp(lambda g, ww: bdot("nn", _rms(mem_v, g), ww), g_ref[...], w_ref[...].astype(F32))
        dg, dw = vjp(d_ref[...])
        dg_ref[...] = dg
        dw_ref[...] = dw

    return pl.pallas_call(body, name=name,
                          out_shape=[jax.ShapeDtypeStruct(gain.shape, F32), jax.ShapeDtypeStruct(w.shape, F32)],
                          compiler_params=pltpu.CompilerParams(vmem_limit_bytes=VMEM_LIMIT))(mem, gain, w, dkv)


MG_TB = 128


def _merge_block(ys, gl, wup, wout, gpost):
    merged = None
    for n in range(4):
        t = _sigmoid(gl[:, 1024 * n:1024 * (n + 1)]) * bdot("nn", ys[n], wup[n])
        merged = t if merged is None else merged + t
    out = bdot("nn", merged, wout)
    return _rms(out, gpost)


def merge_forward(ys, cols, jgate, x, wup, wout, gpost, name):
    T = x.shape[0]
    TB = 256

    def body(ya, yb, yc, ym, gl_ref, x_ref, wup_ref, wout_ref, gp_ref, o_ref):
        upd = _merge_block([ya[...], yb[...], yc[...], ym[...]], gl_ref[...],
                           [wup_ref[n] for n in range(4)], wout_ref[...], gp_ref[...])
        o_ref[...] = x_ref[...] + upd

    yspec = pl.BlockSpec((TB, 512), lambda i: (i, 0))
    return pl.pallas_call(
        body, name=name, out_shape=jax.ShapeDtypeStruct((T, 1024), F32), grid=(T // TB,),
        in_specs=[yspec] * 4 + [pl.BlockSpec((TB, 4096), lambda i: (i, jgate)),
                                pl.BlockSpec((TB, 1024), lambda i: (i, 0)),
                                _full(wup.shape), _full(wout.shape), _full((1, 1024))],
        out_specs=pl.BlockSpec((TB, 1024), lambda i: (i, 0)),
        compiler_params=pltpu.CompilerParams(dimension_semantics=("parallel",), vmem_limit_bytes=VMEM_LIMIT),
    )(*ys, cols, x, wup, wout, gpost)


def merge_backward(ys, cols, jgate, wup, wout, gpost, dx, name):
    T = dx.shape[0]
    TB = MG_TB

    def body(ya, yb, yc, ym, gl_ref, wup_ref, wout_ref, gp_ref, dx_ref,
             dgl_ref, dya, dyb, dyc, dym, dwup_ref, dwout_ref, dgp_ref):
        @pl.when(pl.program_id(0) == 0)
        def _():
            dwup_ref[...] = jnp.zeros_like(dwup_ref)
            dwout_ref[...] = jnp.zeros_like(dwout_ref)
            dgp_ref[...] = jnp.zeros_like(dgp_ref)

        _, vjp = jax.vjp(_merge_block, [ya[...], yb[...], yc[...], ym[...]], gl_ref[...],
                         [wup_ref[n].astype(F32) for n in range(4)], wout_ref[...].astype(F32), gp_ref[...])
        dys, dgl, dwup, dwout, dgp = vjp(dx_ref[...])
        for ref, val in zip((dya, dyb, dyc, dym), dys):
            ref[...] = val
        dgl_ref[...] = dgl.astype(dgl_ref.dtype)
        for n in range(4):
            dwup_ref[n] += dwup[n]
        dwout_ref[...] += dwout
        dgp_ref[...] += dgp

    yspec = pl.BlockSpec((TB, 512), lambda i: (i, 0))
    return pl.pallas_call(
        body, name=name,
        out_shape=[jax.ShapeDtypeStruct(cols.shape, BF)] + [jax.ShapeDtypeStruct((T, 512), F32)] * 4 + [
            jax.ShapeDtypeStruct(wup.shape, F32), jax.ShapeDtypeStruct(wout.shape, F32),
            jax.ShapeDtypeStruct((1, 1024), F32)],
        grid=(T // TB,),
        in_specs=[yspec] * 4 + [pl.BlockSpec((TB, 4096), lambda i: (i, jgate)),
                                _full(wup.shape), _full(wout.shape), _full((1, 1024)),
                                pl.BlockSpec((TB, 1024), lambda i: (i, 0))],
        out_specs=[pl.BlockSpec((TB, 4096), lambda i: (i, jgate))] + [yspec] * 4 + [
            _full(wup.shape), _full(wout.shape), _full((1, 1024))],
        compiler_params=pltpu.CompilerParams(dimension_semantics=("arbitrary",), vmem_limit_bytes=VMEM_LIMIT),
    )(*ys, cols, wup, wout, gpost, dx)


NB = 256


def prenorm_forward(x, gain, name):
    T, D = x.shape

    def body(x_ref, g_ref, o_ref, ot_ref):
        h = _rms(x_ref[...], g_ref[...])
        o_ref[...] = h.astype(BF)
        ot_ref[...] = h.T.astype(BF)

    return pl.pallas_call(
        body, name=name,
        out_shape=[jax.ShapeDtypeStruct((T, D), BF), jax.ShapeDtypeStruct((D, T), BF)], grid=(T // NB,),
        in_specs=[pl.BlockSpec((NB, D), lambda i: (i, 0)), _full((1, D))],
        out_specs=[pl.BlockSpec((NB, D), lambda i: (i, 0)), pl.BlockSpec((D, NB), lambda i: (0, i))],
        compiler_params=pltpu.CompilerParams(dimension_semantics=("parallel",)),
    )(x, gain)


def prenorm_backward(x, gain, dh, dres, name):
    T = x.shape[0]

    def body(x_ref, g_ref, dh_ref, dr_ref, dx_ref, dg_ref):
        @pl.when(pl.program_id(0) == 0)
        def _():
            dg_ref[...] = jnp.zeros_like(dg_ref)

        _, vjp = jax.vjp(_rms, x_ref[...], g_ref[...])
        dxn, dg = vjp(dh_ref[...])
        dx_ref[...] = dr_ref[...] + dxn
        dg_ref[...] += dg

    spec = pl.BlockSpec((NB, 1024), lambda i: (i, 0))
    return pl.pallas_call(
        body, name=name,
        out_shape=[jax.ShapeDtypeStruct(x.shape, F32), jax.ShapeDtypeStruct((1, 1024), F32)], grid=(T // NB,),
        in_specs=[spec, _full((1, 1024)), spec, spec], out_specs=[spec, _full((1, 1024))],
        compiler_params=pltpu.CompilerParams(dimension_semantics=("arbitrary",)),
    )(x, gain, dh, dres)


def loss_head(y, target, name):
    T, D = y.shape

    def body(y_ref, t_ref, l_ref, d_ref):
        @pl.when(pl.program_id(0) == 0)
        def _():
            l_ref[...] = jnp.zeros_like(l_ref)

        err = y_ref[...] - t_ref[...]
        d_ref[...] = err * (1.0 / D)
        l_ref[...] += jnp.full(l_ref.shape, 0.5 * jnp.sum(jnp.mean(err * err, axis=1, keepdims=True)), F32)

    spec = pl.BlockSpec((NB, D), lambda i: (i, 0))
    return pl.pallas_call(
        body, name=name,
        out_shape=[jax.ShapeDtypeStruct((1, 128), F32), jax.ShapeDtypeStruct(y.shape, F32)], grid=(T // NB,),
        in_specs=[spec, spec], out_specs=[_full((1, 128)), spec],
        compiler_params=pltpu.CompilerParams(dimension_semantics=("arbitrary",)),
    )(y, target)


JB_GATE, JB_XM, JB_DN, JB_SW, JB_GM = 0, 4, 2, 5, 6
_ALIGNED_PIECES = ((5896, 4096), (4872, 512), (5384, 512), (0, 2048), (2048, 8), 504, (3592, 512), (4360, 512),
                   (4104, 128), (4232, 128), 256, (2056, 1024), (3080, 512))
_NATURAL_FROM_ALIGNED = ((5120, 2048), (7168, 8), (9216, 1024), (10240, 512), (7680, 512), (8704, 128), (8832, 128),
                         (8192, 512), (4096, 512), (4608, 512), (0, 4096))


def _natural_range(slots, start, width):
    out = []
    while width > 0:
        j, i = divmod(start, W_IN_SHARD)
        take = min(width, W_IN_SHARD - i)
        out.append(slots[j, :, i:i + take])
        start, width = start + take, width - take
    return out


def _aligned_w_in(slots):
    parts = []
    for piece in _ALIGNED_PIECES:
        if isinstance(piece, int):
            parts.append(jnp.zeros(slots.shape[1:2] + (piece,), slots.dtype))
        else:
            parts += _natural_range(slots, *piece)
    return jnp.concatenate(parts, axis=-1)


def _slots_of_aligned(d_al):
    slots = []
    for s in range(N_DEV):
        lo, hi = s * W_IN_SHARD, (s + 1) * W_IN_SHARD
        parts, nat = [], 0
        for a_start, width in _NATURAL_FROM_ALIGNED:
            b, e = max(lo, nat), min(hi, nat + width)
            if b < e:
                parts.append(d_al[..., a_start + b - nat:a_start + e - nat])
            nat += width
        parts.append(jnp.zeros(d_al.shape[:1] + (W_IN_SHARD_PAD - W_IN_SHARD,), d_al.dtype))
        slots.append(jnp.concatenate(parts, axis=-1))
    return jnp.stack(slots)


SMALL_VEC_W = 1024


def _pack_small(parts):
    rows = []
    for p in parts:
        flat = p.reshape(-1).astype(F32)
        r = -(-flat.shape[0] // SMALL_VEC_W)
        rows.append(jnp.pad(flat, (0, r * SMALL_VEC_W - flat.shape[0])).reshape(r, SMALL_VEC_W))
    vec = jnp.concatenate(rows, axis=0)
    return jnp.pad(vec, ((0, -vec.shape[0] % 8), (0, 0)))


def _unpack_small(vec, shapes):
    out, off = [], 0
    for s in shapes:
        n = math.prod(s)
        r = -(-n // SMALL_VEC_W)
        out.append(vec[off:off + r].reshape(-1)[:n].reshape(s))
        off += r
    return out


def _lanes(vec, at):
    return jnp.zeros((1, 128), F32).at[0, at:at + vec.shape[0]].set(vec)


SMALL_NAMES = ("norm_pre", "norm_post", "norm_mem", "a_log", "dt_bias", "dn_norm", "gm_norm",
               "spatial_w", "spatial_b", "sinks")


def _other_weights(s_mem, s_up, s_out):
    return (s_mem.reshape(D_MODEL, 2 * BRANCH_W),
            jnp.transpose(s_up, (1, 2, 0, 3)).reshape(N_BRANCH, BRANCH_W, D_MODEL), s_out.reshape(D_MODEL, D_MODEL))


def _grad_slots(d_in_al, d_mem, d_up, d_out):
    return [None if d_in_al is None else _slots_of_aligned(d_in_al), d_mem.astype(BF).reshape(N_DEV, 128, 2 * BRANCH_W),
            jnp.transpose(d_up.astype(BF).reshape(N_BRANCH, BRANCH_W, N_DEV, 128), (2, 0, 1, 3)),
            d_out.astype(BF).reshape(N_DEV, 128, D_MODEL)]


def _layer_params(l, small, conv_full, token):
    return dict(
        gpre=small["norm_pre"][l][None] + token, gpost=small["norm_post"][l][None], gmem=small["norm_mem"][l][None],
        cw=conv_full[l], al=_lanes(small["a_log"][l], 4), dt=_lanes(small["dt_bias"][l], 4),
        dnn=small["dn_norm"][l][None], gain=small["gm_norm"][l][None], ws=small["spatial_w"][l],
        bt=jnp.zeros((128, 128), F32).at[:, :GM_GROUPS].set(small["spatial_b"][l].T),
        sinks=_lanes(small["sinks"][l], 0))


def _layer_forward(l, xl, mem, p, w_in_al, other_weights):
    t = "l%d_" % l
    h, h_t = prenorm_forward(xl, p["gpre"], t + "prenorm")
    cols = _matmul(h, w_in_al, "nn", F32, (1024, 1536, 1024), t + "w_in")
    ya, ss, ts = dn_forward(cols, JB_DN, p["cw"], p["al"], p["dt"], p["dnn"], t + "deltanet")
    yb = gm_forward(cols, JB_GM, p["gain"], p["ws"], p["bt"], t + "gmlp")
    yc = sw_forward(cols, JB_SW, p["sinks"], t + "swa")
    w_mem, w_up, w_out = other_weights(yc)
    mkv = memkv_forward(mem, p["gmem"], w_mem, t + "memkv")
    ym = xm_forward(cols, JB_XM, mkv, t + "memattn")
    xn = merge_forward([ya, yb, yc, ym], cols, JB_GATE, xl, w_up, w_out, p["gpost"], t + "merge")
    return xn, dict(p, x=xl, h_t=h_t, cols=cols, mkv=mkv, ss=ss, ts=ts, ys=[ya, yb, yc, ym]), (w_in_al, w_mem, w_up, w_out)


def _layer_backward(l, s, mem, weights, dx, token, early=None):
    w_in_al, w_mem, w_up, w_out = weights
    t = "l%d_" % l
    cols = s["cols"]
    dcols, dya, dyb, dyc, dym, dwup, dwout, dgpost = merge_backward(
        s["ys"], cols, JB_GATE, w_up, w_out, s["gpost"] + token, dx, t + "merge_bwd")
    dcols, dmkv = xm_backward(cols, JB_XM, s["mkv"], dym, dcols, t + "memattn_bwd")
    dgmem, dwmem = memkv_backward(mem, s["gmem"], w_mem, dmkv, t + "memkv_bwd")
    sinks = s["sinks"] if early is None else s["sinks"] + early(dwmem, dwup, dwout)
    dcols, dsinks = sw_backward(cols, JB_SW, sinks, dyc, dcols, t + "swa_bwd")
    dcols, dgain, dws, dbt = gm_backward(cols, JB_GM, s["gain"], s["ws"], s["bt"], dyb, dcols, t + "gmlp_bwd")
    dcols, dcw, dal, ddt, ddn = dn_backward(
        cols, JB_DN, s["cw"], s["al"], s["dt"], s["dnn"], s["ss"], s["ts"], dya, dcols, t + "deltanet_bwd")
    dh = _matmul(dcols, w_in_al, "nt", F32, (1024, 1024, 1024), t + "w_in_bwd_x")
    dwin = _matmul(s["h_t"], dcols, "nn", BF, (1024, 1536, 2048), t + "w_in_bwd_w")
    dx, dgpre = prenorm_backward(s["x"], s["gpre"], dh, dx, t + "prenorm_bwd")
    gsmall = dict(norm_pre=dgpre[0], norm_post=dgpost[0], norm_mem=dgmem[0], a_log=dal[0, 4:8], dt_bias=ddt[0, 4:8],
                  dn_norm=ddn[0], gm_norm=dgain[0], spatial_w=dws, spatial_b=dbt[:, :GM_GROUPS].T,
                  sinks=dsinks[0, :SW_HEADS], conv_w=dcw)
    return dx, gsmall, (dwin, dwmem, dwup, dwout)


def kernel(x, mem, norm_pre, norm_post, norm_mem, w_in, conv_w, a_log, dt_bias, dn_norm, gm_norm, spatial_w, spatial_b, sinks, w_mem_kv, w_up, w_out, loss_target, m_norm_pre, m_norm_post, m_norm_mem, m_w_in, m_conv_w, m_a_log, m_dt_bias, m_dn_norm, m_gm_norm, m_spatial_w, m_spatial_b, m_sinks, m_w_mem_kv, m_w_up, m_w_out, v_norm_pre, v_norm_post, v_norm_mem, v_w_in, v_conv_w, v_a_log, v_dt_bias, v_dn_norm, v_gm_norm, v_spatial_w, v_spatial_b, v_sinks, v_w_mem_kv, v_w_up, v_w_out):
    xi, yi, ci = _my_place()
    my_slot = 4 * xi + 2 * yi + ci
    conv_shard = conv_w.shape[-1]
    x2, mem2, target = x[0], mem[0], loss_target[0]

    w_in_pad = jnp.pad(w_in.astype(BF), ((0, 0), (0, 0), (0, W_IN_SHARD_PAD - W_IN_SHARD)))
    shards = [[w_in_pad[l], w_mem_kv[l].astype(BF), w_up[l].astype(BF), w_out[l].astype(BF)] for l in range(DEPTH)]
    w_in_slots0, conv_slots = _all_gather_slots([shards[0][0], conv_w], "gather_weights_l0")
    ag = list(_spread_start(shards[0][1:] + shards[1], "gather", "gather_weights_rest_start"))
    conv_full = jnp.transpose(conv_slots, (1, 2, 0, 3)).reshape(DEPTH, CONV_W, N_DEV * conv_shard)
    small = dict(norm_pre=norm_pre, norm_post=norm_post, norm_mem=norm_mem, a_log=a_log,
                 dt_bias=dt_bias, dn_norm=dn_norm, gm_norm=gm_norm, spatial_w=spatial_w,
                 spatial_b=spatial_b, sinks=sinks)

    def arrived(which, after, name):
        ag[2], ag[3] = _spread_wait(ag[0], ag[1], ag[2], ag[3], which, after, name)
        return [ag[3][a] for a in which]

    x1, saved0, weights0 = _layer_forward(
        0, x2, mem2, _layer_params(0, small, conv_full, ag[4][0, 0]), _aligned_w_in(w_in_slots0),
        lambda y: _other_weights(*arrived([0, 1, 2], y, "gather_weights_l0_rest_wait")))
    w_in_slots1, = arrived([3], x1, "gather_weights_l1_w_in_wait")
    x_out, saved1, weights1 = _layer_forward(
        1, x1, mem2, _layer_params(1, small, conv_full, 0.0), _aligned_w_in(w_in_slots1),
        lambda y: _other_weights(*arrived([4, 5, 6], y, "gather_weights_l1_rest_wait")))
    loss, dx = loss_head(x_out, target, "loss_head")

    dx, gsmall1, gbig1 = _layer_backward(1, saved1, mem2, weights1, dx, 0.0)
    rs_send, rs_recv, rs_src, rs_land, rs_token = _spread_start(_grad_slots(*gbig1), "scatter", "exchange_grads_l1_start")
    rest0 = []

    def send_rest0(dwmem, dwup, dwout):
        rest0.extend(_spread_start(_grad_slots(None, dwmem, dwup, dwout)[1:], "scatter", "exchange_grads_l0_rest_start"))
        return rest0[4][0, 0]

    dx, gsmall0, gbig0 = _layer_backward(0, saved0, mem2, weights0, dx, rs_token[0, 0], send_rest0)
    _, parts1 = _spread_wait(rs_send, rs_recv, rs_src, rs_land, range(4), dx, "exchange_grads_l1_wait")

    packed_names = SMALL_NAMES + ("conv_w",)
    gs = {n: jnp.stack([gsmall0[n], gsmall1[n]]) for n in packed_names}
    small_parts = [loss[0, :1]] + [gs[n] for n in packed_names]
    tot = _unpack_small(_all_reduce_vmem(_pack_small(small_parts), "all_reduce_small"), [p.shape for p in small_parts])
    loss_tot = tot[0][0]
    grads = dict(zip(packed_names, tot[1:]))
    grads["conv_w"] = lax.dynamic_slice_in_dim(grads["conv_w"], my_slot * conv_shard, conv_shard, axis=2)

    g_win0 = _slots_of_aligned(gbig0[0])
    g_win0 = g_win0.reshape((N_DEV // 2, 2) + g_win0.shape[1:])
    theirs, = _exchange_sibling([g_win0], "exchange_sibling_l0")
    chip_sum = _pair_sum(g_win0, theirs, "pair_sum_l0")
    ch_send, ch_recv, ch_src, ch_land, ch_token = _spread_start([chip_sum], "chips", "exchange_chips_l0_start")

    given = dict(norm_pre=(norm_pre, m_norm_pre, v_norm_pre), norm_post=(norm_post, m_norm_post, v_norm_post),
                 norm_mem=(norm_mem, m_norm_mem, v_norm_mem), a_log=(a_log, m_a_log, v_a_log),
                 dt_bias=(dt_bias, m_dt_bias, v_dt_bias), dn_norm=(dn_norm, m_dn_norm, v_dn_norm),
                 gm_norm=(gm_norm, m_gm_norm, v_gm_norm), spatial_w=(spatial_w, m_spatial_w, v_spatial_w),
                 spatial_b=(spatial_b, m_spatial_b, v_spatial_b), sinks=(sinks, m_sinks, v_sinks),
                 conv_w=(conv_w, m_conv_w, v_conv_w))
    pshapes = [given[n][0].shape for n in packed_names]
    pw, pm, pv = (_pack_small([given[n][i] for n in packed_names]) for i in range(3))
    pd, pnm, pnv = _adamw(pw + ch_token[0, 0], _pack_small([grads[n] for n in packed_names]), pm, pv, "adamw_small")
    upd = {n: t for n, t in zip(packed_names, zip(_unpack_small(pd, pshapes), _unpack_small(pnm, pshapes),
                                                  _unpack_small(pnv, pshapes)))}
    big = (("w_mem_kv", (w_mem_kv, m_w_mem_kv, v_w_mem_kv)), ("w_up", (w_up, m_w_up, v_w_up)),
           ("w_out", (w_out, m_w_out, v_w_out)))
    first = [_sum_adamw(parts1[1 + i], w, m, v, 1, None, ch_token, "adamw_%s_l1" % name)
             for i, (name, (w, m, v)) in enumerate(big)]
    _, parts0_rest = _spread_wait(*rest0[:4], range(3), first[-1][0], "exchange_grads_l0_rest_wait")
    for i, (name, (w, m, v)) in enumerate(big):
        g, d, nm, nv = _sum_adamw(parts0_rest[i], w, m, v, 0, first[i], None, "adamw_%s_l0" % name)
        grads[name], upd[name] = g, (d, nm, nv)
    _, (parts0_w_in,) = _spread_wait(ch_send, ch_recv, ch_src, ch_land, [0], upd["w_out"][0], "exchange_chips_l0_wait")
    w_in_t, m_w_in_t, v_w_in_t = (jnp.transpose(t, (2, 0, 1)) for t in (w_in, m_w_in, v_w_in))
    g, d, nm, nv = (jnp.transpose(t, (1, 2, 0)) for t in
                    _sum_adamw_t([parts0_w_in, parts1[0]], w_in_t, m_w_in_t, v_w_in_t, "adamw_w_in"))
    grads["w_in"], upd["w_in"] = g, (d, nm, nv)

    order = ("norm_pre", "norm_post", "norm_mem", "w_in", "conv_w", "a_log", "dt_bias", "dn_norm",
             "gm_norm", "spatial_w", "spatial_b", "sinks", "w_mem_kv", "w_up", "w_out")
    return (loss_tot, dx[None], *[grads[n] for n in order], *[upd[n][0] for n in order],
            *[upd[n][1] for n in order], *[upd[n][2] for n in order])
```

```python
import functools
import math

import jax
import jax.numpy as jnp
from jax import lax
from jax.experimental import pallas as pl
from jax.experimental.pallas import tpu as pltpu

MESH = pl.DeviceIdType.MESH
N_DEV = 8

D_MODEL = 1024
DEPTH = 2
N_BRANCH = 4
BRANCH_W = 512
DN_HEADS = 4
CONV_W = 4
GM_GROUPS = 4
SW_HEADS = 8
EPS = 1e-6
NEG_INF = -1e30

D_IN = 9992
W_IN_SHARD = D_IN // N_DEV
W_IN_SHARD_PAD = 1280
D_IN_AL = 10752
DN_W, SW_W, GM_W, XM_W = 2560, 1536, 1536, 1024

ADAM_LR = 0.001
ADAM_B1 = 0.9
ADAM_B2 = 0.999
ADAM_EPS = 1e-08
ADAM_WD = 0.01
ADAM_STEP = 10

VMEM_LIMIT = 56 * 1024 * 1024

BF = jnp.bfloat16
F32 = jnp.float32
DN_C = 128
DN_D = 128
HALO = 8
BLK = 128


def _my_place():
    return lax.axis_index("x"), lax.axis_index("y"), lax.axis_index("c")


_ANY = pl.BlockSpec(memory_space=pl.ANY)


def _all_gather_slots(parts, name):
    n = len(parts)

    def body(*refs):
        p_refs, out_refs = refs[:n], refs[n:2 * n]
        send_sems, recv_sems, local_sems = refs[2 * n:]
        x, y, c = _my_place()
        me, sibling = (x, y, c), (x, y, 1 - c)
        chips = [(1 - x, y), (x, 1 - y), (1 - x, 1 - y)]

        def copy(a, k, block, to, src=None):
            px, py, pc = block
            slot = out_refs[a].at[4 * px + 2 * py + pc]
            return pltpu.make_async_remote_copy(
                src_ref=slot if src is None else src, dst_ref=slot,
                send_sem=send_sems.at[7 * a + k], recv_sem=recv_sems.at[7 * a + k],
                device_id=to, device_id_type=MESH)

        mine = [pltpu.make_async_copy(p_refs[a], out_refs[a].at[4 * x + 2 * y + c], local_sems.at[a])
                for a in range(n)]
        for cp in mine:
            cp.start()
        first = []
        for a in range(n):
            first.append(copy(a, 0, me, sibling, src=p_refs[a]))
            first += [copy(a, 1 + j, me, (*chip, c), src=p_refs[a]) for j, chip in enumerate(chips)]
        for cp in first:
            cp.start()
        passed = []
        for j, chip in enumerate(chips):
            for a in range(n):
                copy(a, 1 + j, (*chip, c), me).wait_recv()
                fwd = copy(a, 4 + j, (*chip, c), sibling)
                fwd.start()
                passed.append(fwd)
        for a in range(n):
            copy(a, 0, sibling, me).wait_recv()
            for j, chip in enumerate(chips):
                copy(a, 4 + j, (*chip, 1 - c), me).wait_recv()
        for cp in first + passed:
            cp.wait_send()
        for cp in mine:
            cp.wait()

    return pl.pallas_call(
        body, name=name,
        out_shape=[jax.ShapeDtypeStruct((N_DEV,) + p.shape, p.dtype) for p in parts],
        in_specs=[_ANY] * n, out_specs=[_ANY] * n,
        scratch_shapes=[pltpu.SemaphoreType.DMA((7 * n,)), pltpu.SemaphoreType.DMA((7 * n,)),
                        pltpu.SemaphoreType.DMA((n,))],
    )(*parts)


def _exchange_sibling(parts, name):
    n = len(parts)

    def body(*refs):
        g_refs, out_refs = refs[:n], refs[n:2 * n]
        send_sems, recv_sems = refs[2 * n:]
        x, y, c = _my_place()
        copies = [pltpu.make_async_remote_copy(
            src_ref=g_refs[a].at[:, 1 - c], dst_ref=out_refs[a],
            send_sem=send_sems.at[a], recv_sem=recv_sems.at[a],
            device_id=(x, y, 1 - c), device_id_type=MESH) for a in range(n)]
        for cp in copies:
            cp.start()
        for cp in copies:
            cp.wait()

    return pl.pallas_call(
        body, name=name,
        out_shape=[jax.ShapeDtypeStruct((4,) + g.shape[2:], g.dtype) for g in parts],
        in_specs=[_ANY] * n, out_specs=[_ANY] * n,
        scratch_shapes=[pltpu.SemaphoreType.DMA((n,)), pltpu.SemaphoreType.DMA((n,))],
    )(*parts)


_HBM = pl.BlockSpec(memory_space=pltpu.HBM)
_SEM = pl.BlockSpec(memory_space=pltpu.SEMAPHORE)
_EFFECT = pltpu.SideEffectType.DATAFLOW_SIDE_EFFECTING


def _peer(x, y, c, k):
    return (1 - x if (k >> 2) & 1 else x, 1 - y if (k >> 1) & 1 else y, 1 - c if k & 1 else c)


def _spread_start(srcs, mode, name):
    n = len(srcs)
    lands = [lax.empty((N_DEV,) + s.shape if mode == "gather" else s.shape, s.dtype) for s in srcs]
    peers = range(0, N_DEV, 2) if mode == "chips" else range(N_DEV)

    def body(*refs):
        src_refs, land_refs = refs[:n], refs[n:2 * n]
        send_sems, recv_sems = refs[2 * n:2 * n + 2]
        token = refs[-1]
        x, y, c = _my_place()
        for a in range(n):
            for k in peers:
                px, py, pc = _peer(x, y, c, k)
                if mode == "chips":
                    src, mine = src_refs[a].at[2 * px + py], 2 * x + y
                else:
                    src = src_refs[a].at[4 * px + 2 * py + pc] if mode == "scatter" else src_refs[a]
                    mine = 4 * x + 2 * y + c
                pltpu.make_async_remote_copy(
                    src_ref=src, dst_ref=land_refs[a].at[mine],
                    send_sem=send_sems.at[a], recv_sem=recv_sems.at[a],
                    device_id=(px, py, pc), device_id_type=MESH).start()
        token[...] = jnp.zeros_like(token)

    out = pl.pallas_call(
        body, name=name,
        out_shape=[pltpu.SemaphoreType.DMA((n,)), pltpu.SemaphoreType.DMA((n,))]
        + [pltpu.HBM(s.shape, s.dtype) for s in srcs] + [pltpu.HBM(l.shape, l.dtype) for l in lands]
        + [jax.ShapeDtypeStruct((8, 128), F32)],
        in_specs=[_HBM] * (2 * n),
        out_specs=[_SEM, _SEM] + [_HBM] * (2 * n) + [pl.BlockSpec(memory_space=pltpu.VMEM)],
        input_output_aliases={i: 2 + i for i in range(2 * n)},
        compiler_params=pltpu.CompilerParams(has_side_effects=_EFFECT),
    )(*[pltpu.with_memory_space_constraint(s, pltpu.HBM) for s in srcs],
      *[pltpu.with_memory_space_constraint(l, pltpu.HBM) for l in lands])
    return out[0], out[1], out[2:2 + n], out[2 + n:2 + 2 * n], out[-1]


def _spread_wait(send_sems, recv_sems, srcs, lands, which, after, name):
    n = len(srcs)

    def body(*refs):
        land_refs = refs[n:2 * n]
        send_sems, recv_sems = refs[2 * n:2 * n + 2]
        x, y, c = _my_place()
        for a in which:
            whole = pltpu.make_async_remote_copy(
                src_ref=land_refs[a], dst_ref=land_refs[a],
                send_sem=send_sems.at[a], recv_sem=recv_sems.at[a],
                device_id=(x, y, c), device_id_type=MESH)
            whole.wait_send()
            whole.wait_recv()

    out = pl.pallas_call(
        body, name=name,
        out_shape=[pltpu.HBM(s.shape, s.dtype) for s in srcs] + [pltpu.HBM(l.shape, l.dtype) for l in lands],
        in_specs=[_HBM] * (2 * n) + [_SEM, _SEM, _ANY],
        out_specs=[_HBM] * (2 * n),
        input_output_aliases={i: i for i in range(2 * n)},
        compiler_params=pltpu.CompilerParams(has_side_effects=_EFFECT),
    )(*srcs, *lands, send_sems, recv_sems, after)
    return out[:n], out[n:]


def _all_reduce_vmem(v, name):
    def body(v_ref, out_ref, buf, send_sems, recv_sems):
        x, y, c = _my_place()
        peers = [(x, y, 1 - c), (1 - x, y, c), (x, 1 - y, c)]
        out_ref[...] = v_ref[...]
        for step, peer in enumerate(peers):
            cp = pltpu.make_async_remote_copy(
                src_ref=out_ref, dst_ref=buf.at[step],
                send_sem=send_sems.at[step], recv_sem=recv_sems.at[step],
                device_id=peer, device_id_type=MESH)
            cp.start()
            cp.wait()
            out_ref[...] = out_ref[...] + buf[step]

    return pl.pallas_call(
        body, name=name,
        out_shape=jax.ShapeDtypeStruct(v.shape, v.dtype),
        in_specs=[pl.BlockSpec(memory_space=pltpu.VMEM)],
        out_specs=pl.BlockSpec(memory_space=pltpu.VMEM),
        scratch_shapes=[pltpu.VMEM((3,) + v.shape, v.dtype),
                        pltpu.SemaphoreType.DMA((3,)), pltpu.SemaphoreType.DMA((3,))],
    )(v)


def _pick(n, pref):
    if n <= pref:
        return n
    t = pref - pref % 128
    while t > 0 and n % t:
        t -= 128
    return t if t > 0 else n


_DIMS = {"nn": (((1,), (0,)), ((), ())),
         "nt": (((1,), (1,)), ((), ())),
         "tn": (((0,), (0,)), ((), ()))}


def _matmul(a, b, mode, out_dtype, tiles, name):
    (m, k) = a.shape
    n = b.shape[1] if mode == "nn" else b.shape[0]
    tm, tn, tk = (_pick(d, t) for d, t in zip((m, n, k), tiles))
    nk = k // tk

    def product(a_ref, b_ref):
        return lax.dot_general(a_ref[...].astype(BF), b_ref[...].astype(BF), _DIMS[mode], preferred_element_type=F32)

    def body_whole_k(a_ref, b_ref, o_ref):
        o_ref[...] = product(a_ref, b_ref).astype(o_ref.dtype)

    def body_split_k(a_ref, b_ref, o_ref, acc_ref):
        kk = pl.program_id(2)

        @pl.when(kk == 0)
        def _():
            acc_ref[...] = jnp.zeros_like(acc_ref)

        acc_ref[...] += product(a_ref, b_ref)

        @pl.when(kk == nk - 1)
        def _():
            o_ref[...] = acc_ref[...].astype(o_ref.dtype)

    b_spec = (pl.BlockSpec((tn, tk), lambda i, j, kk: (j, kk)) if mode == "nt"
              else pl.BlockSpec((tk, tn), lambda i, j, kk: (kk, j)))
    return pl.pallas_call(
        body_whole_k if nk == 1 else body_split_k, name=name,
        out_shape=jax.ShapeDtypeStruct((m, n), out_dtype),
        grid=(m // tm, n // tn, nk),
        in_specs=[pl.BlockSpec((tm, tk), lambda i, j, kk: (i, kk)), b_spec],
        out_specs=pl.BlockSpec((tm, tn), lambda i, j, kk: (i, j)),
        scratch_shapes=[] if nk == 1 else [pltpu.VMEM((tm, tn), F32)],
        compiler_params=pltpu.CompilerParams(
            dimension_semantics=("parallel", "parallel", "arbitrary"),
            vmem_limit_bytes=VMEM_LIMIT),
    )(a, b)


def _rows2d(t, lead):
    return t.reshape(t.shape[:lead] + (math.prod(t.shape[lead:-1]), t.shape[-1]))


def _pair_sum(g, theirs, name):
    g3, t3 = _rows2d(g, 2), _rows2d(theirs, 1)
    _, r, w = t3.shape
    tr = _pick(r, 512)

    def body(g_ref, t_ref, o_ref):
        c = lax.axis_index("c")
        mine = jnp.where(c == 0, g_ref[0, 0], g_ref[0, 1])
        o_ref[0] = (mine.astype(F32) + t_ref[0].astype(F32)).astype(o_ref.dtype)

    out = pl.pallas_call(
        body, name=name,
        out_shape=jax.ShapeDtypeStruct(t3.shape, t3.dtype),
        grid=(4, r // tr),
        in_specs=[pl.BlockSpec((1, 2, tr, w), lambda q, i: (q, 0, i, 0)),
                  pl.BlockSpec((1, tr, w), lambda q, i: (q, i, 0))],
        out_specs=pl.BlockSpec((1, tr, w), lambda q, i: (q, i, 0)),
        compiler_params=pltpu.CompilerParams(dimension_semantics=("parallel", "parallel")),
    )(g3, t3)
    return out.reshape(theirs.shape)


def _adam_update(w, g, m, v):
    c1 = 1.0 - ADAM_B1 ** ADAM_STEP
    c2 = 1.0 - ADAM_B2 ** ADAM_STEP
    nm = ADAM_B1 * m + (1.0 - ADAM_B1) * g
    nv = ADAM_B2 * v + (1.0 - ADAM_B2) * (g * g)
    delta = -ADAM_LR * ((nm / c1) / (jnp.sqrt(nv / c2) + ADAM_EPS) + ADAM_WD * w)
    return delta, nm, nv


def _sum_adamw(parts, w, m, v, layer, carry, after, name):
    shape = w.shape
    cols = shape[-1]
    p3 = _rows2d(parts, 1)
    w3, m3, v3 = (_rows2d(t, 1) for t in (w, m, v))
    rows = w3.shape[1]
    tr = _pick(rows, 128)
    n_parts = p3.shape[0]

    def body(p_ref, w_ref, m_ref, v_ref, *rest):
        g_ref, d_ref, nm_ref, nv_ref = rest[-4:]
        g = p_ref[0, :, :cols].astype(F32)
        for q in range(1, n_parts):
            g = g + p_ref[q, :, :cols].astype(F32)
        d, nm, nv = _adam_update(w_ref[0], g, m_ref[0], v_ref[0])
        g_ref[0] = g
        d_ref[0] = d
        nm_ref[0] = nm
        nv_ref[0] = nv

    spec = pl.BlockSpec((1, tr, cols), lambda i: (layer, i, 0))
    extra = [] if carry is None else [_rows2d(t, 1) for t in carry]
    tail = [] if after is None else [after]
    out = pl.pallas_call(
        body, name=name,
        out_shape=[jax.ShapeDtypeStruct(w3.shape, F32)] * 4,
        grid=(rows // tr,),
        in_specs=[pl.BlockSpec((n_parts, tr, p3.shape[-1]), lambda i: (0, i, 0)), spec, spec, spec] + [_ANY] * len(extra + tail),
        out_specs=[spec] * 4,
        input_output_aliases={4 + i: i for i in range(len(extra))},
        compiler_params=pltpu.CompilerParams(dimension_semantics=("parallel",)),
    )(p3, w3, m3, v3, *extra, *tail)
    return tuple(t.reshape(shape) for t in out)


def _sum_adamw_t(parts, w, m, v, name):
    rows = parts[0].shape[2]
    tr = 128
    assert rows % tr == 0 and rows >= w.shape[0]

    def body(*refs):
        p_refs, (w_ref, m_ref, v_ref), (g_ref, d_ref, nm_ref, nv_ref) = refs[:DEPTH], refs[DEPTH:DEPTH + 3], refs[DEPTH + 3:]
        for l in range(DEPTH):
            g = p_refs[l][0].astype(F32)
            for q in range(1, p_refs[l].shape[0]):
                g = g + p_refs[l][q].astype(F32)
            g = g.T
            d, nm, nv = _adam_update(w_ref[:, l, :], g, m_ref[:, l, :], v_ref[:, l, :])
            g_ref[:, l, :] = g
            d_ref[:, l, :] = d
            nm_ref[:, l, :] = nm
            nv_ref[:, l, :] = nv

    spec = pl.BlockSpec((tr,) + w.shape[1:], lambda i: (i, 0, 0))
    return pl.pallas_call(
        body, name=name,
        out_shape=[jax.ShapeDtypeStruct(w.shape, F32)] * 4,
        grid=(rows // tr,),
        in_specs=[pl.BlockSpec((p.shape[0], p.shape[1], tr), lambda i: (0, 0, i)) for p in parts] + [spec] * 3,
        out_specs=[spec] * 4,
        compiler_params=pltpu.CompilerParams(dimension_semantics=("parallel",)),
    )(*parts, w, m, v)


def _adamw(w, g, m, v, name):
    rows, cols = w.shape
    tr = _pick(rows, 128)

    def body(w_ref, g_ref, m_ref, v_ref, d_ref, nm_ref, nv_ref):
        d, nm, nv = _adam_update(w_ref[...], g_ref[...], m_ref[...], v_ref[...])
        d_ref[...] = d
        nm_ref[...] = nm
        nv_ref[...] = nv

    spec = pl.BlockSpec((tr, cols), lambda i: (i, 0))
    return pl.pallas_call(
        body, name=name,
        out_shape=[jax.ShapeDtypeStruct((rows, cols), F32)] * 3,
        grid=(rows // tr,),
        in_specs=[spec] * 4, out_specs=[spec] * 3,
        compiler_params=pltpu.CompilerParams(dimension_semantics=("parallel",)),
    )(w, g, m, v)


_VJP = {"nn": (("nt", "gb"), ("tn", "ag")),
        "nt": (("nn", "gb"), ("tn", "ga")),
        "tn": (("nt", "bg"), ("nn", "ag"))}


def _make_dot(cast, precision):
    def raw(mode, a, b):
        return lax.dot_general(cast(a), cast(b), _DIMS[mode], precision=precision,
                               preferred_element_type=F32)

    @functools.partial(jax.custom_vjp, nondiff_argnums=(0,))
    def dot(mode, a, b):
        return raw(mode, a, b)

    def fwd(mode, a, b):
        return raw(mode, a, b), (a, b)

    def bwd(mode, res, g):
        a, b = res
        pick = {"a": a, "b": b, "g": g}
        (ma, ta), (mb, tb) = _VJP[mode]
        return dot(ma, pick[ta[0]], pick[ta[1]]), dot(mb, pick[tb[0]], pick[tb[1]])

    dot.defvjp(fwd, bwd)
    return dot


bdot = _make_dot(lambda t: t.astype(BF), None)
hdot = _make_dot(lambda t: t, lax.Precision.HIGHEST)


def _xdot(mode, a, b):
    return lax.dot_general(a, b, _DIMS[mode], precision=lax.Precision.HIGH, preferred_element_type=F32)


def _unit_lower_inverse(Ls):
    n = Ls[0].shape[0]
    batched = (((2,), (1,)), ((0,), (0,)))
    mm = lambda a, b: lax.dot_general(a, b, batched, precision=lax.Precision.HIGH, preferred_element_type=F32)
    eye = (lax.broadcasted_iota(jnp.int32, (n, n), 0) == lax.broadcasted_iota(jnp.int32, (n, n), 1)).astype(F32)
    p = jnp.stack(Ls)
    t_inv = eye[None] - p
    for _ in range(6):
        p = mm(p, p)
        t_inv = t_inv + mm(t_inv, p)
    return [t_inv[h] for h in range(len(Ls))]


@jax.custom_vjp
def _tri_solve(L, rhs, t_inv):
    return _xdot("nn", t_inv, rhs)


def _tri_solve_fwd(L, rhs, t_inv):
    sol = _xdot("nn", t_inv, rhs)
    return sol, (t_inv, sol)


def _tri_solve_bwd(res, dsol):
    t_inv, sol = res
    drhs = _xdot("tn", t_inv, dsol)
    return -_xdot("nt", drhs, sol), drhs, jnp.zeros_like(t_inv)


_tri_solve.defvjp(_tri_solve_fwd, _tri_solve_bwd)


def _sigmoid(x):
    return 1.0 / (1.0 + jnp.exp(-x))


def _softplus(x):
    return jnp.maximum(x, 0.0) + jnp.log(1.0 + jnp.exp(-jnp.abs(x)))


def _dn_chunk(S, xs, ba, z, cw, al, dt, dn, t_saved=None):
    C = DN_C
    pre = xs[0] * cw[0] + xs[1] * cw[1] + xs[2] * cw[2] + xs[3] * cw[3]
    qkv = pre * _sigmoid(pre)
    lane = lax.broadcasted_iota(jnp.int32, (1, 128), 1)
    sub = lax.broadcasted_iota(jnp.int32, (C, 1), 0)
    row_i = lax.broadcasted_iota(jnp.int32, (C, C), 0)
    col_i = lax.broadcasted_iota(jnp.int32, (C, C), 1)
    strict = row_i > col_i
    incl = row_i >= col_i
    g_all = jnp.where((lane >= 4) & (lane < 8), -jnp.exp(al) * _softplus(ba + dt), 0.0)
    gc_all = hdot("nn", incl.astype(F32), g_all)
    gc_all_t = gc_all.T
    beta_all = _sigmoid(ba)
    glast_all = jnp.sum(jnp.where(sub == C - 1, gc_all, 0.0), axis=0, keepdims=True)
    heads = []
    for h in range(DN_HEADS):
        q = qkv[:, 128 * h:128 * (h + 1)]
        k = qkv[:, 512 + 128 * h:512 + 128 * (h + 1)]
        v = qkv[:, 1024 + 128 * h:1024 + 128 * (h + 1)]
        q = q * lax.rsqrt(jnp.sum(q * q, axis=1, keepdims=True) + EPS) * (DN_D ** -0.5)
        k = k * lax.rsqrt(jnp.sum(k * k, axis=1, keepdims=True) + EPS)
        beta = jnp.sum(jnp.where(lane == h, beta_all, 0.0), axis=1, keepdims=True)
        gc = jnp.sum(jnp.where(lane == 4 + h, gc_all, 0.0), axis=1, keepdims=True)
        gc_row = jnp.sum(jnp.where(sub == 4 + h, gc_all_t, 0.0), axis=0, keepdims=True)
        g_last = jnp.sum(jnp.where(lane == 4 + h, glast_all, 0.0), axis=1, keepdims=True)
        diff = gc - gc_row
        kb = k * beta
        L = jnp.where(strict, bdot("nt", kb, k) * jnp.exp(jnp.where(strict, diff, 0.0)), 0.0)
        heads.append((q, k, v, beta, gc, g_last, diff, kb, L))
    t_invs = _unit_lower_inverse([hd[-1] for hd in heads]) if t_saved is None else t_saved
    ys, s_new = [], []
    for h, (q, k, v, beta, gc, g_last, diff, kb, L) in enumerate(heads):
        sol = _tri_solve(L, jnp.concatenate([v * beta, kb * jnp.exp(gc)], axis=1), t_invs[h])
        u, w = sol[:, :DN_D], sol[:, DN_D:]
        a_qk = jnp.where(incl, bdot("nt", q, k) * jnp.exp(jnp.where(incl, diff, 0.0)), 0.0)
        qg = q * jnp.exp(gc)
        kd = k * jnp.exp(g_last - gc)
        v_new = u - bdot("nn", w, S[h])
        o = bdot("nn", qg, S[h]) + bdot("nn", a_qk, v_new)
        s_new.append(S[h] * jnp.exp(g_last) + bdot("tn", kd, v_new))
        o = o * lax.rsqrt(jnp.mean(o * o, axis=1, keepdims=True) + EPS) * dn
        zh = z[:, 128 * h:128 * (h + 1)]
        ys.append(o * (zh * _sigmoid(zh)))
    return jnp.concatenate(ys, axis=1), tuple(s_new), tuple(t_invs)


def _load_shifted(xbuf, x_ref, halo_ref, first):
    xbuf[0:HALO, :] = jnp.where(first, 0.0, halo_ref[:, 0:1536])
    xbuf[HALO:HALO + DN_C, :] = x_ref[:, 0:1536]
    return [xbuf[HALO - 3 + k:HALO - 3 + k + DN_C, :] for k in range(4)]


def dn_forward(cols, jblk, cw, al, dt, dn, name):
    T = cols.shape[0]
    n = T // DN_C

    def body(x_ref, halo_ref, cw_ref, al_ref, dt_ref, dn_ref, y_ref, ss_ref, ts_ref, s_scr, xbuf):
        i = pl.program_id(0)

        @pl.when(i == 0)
        def _():
            s_scr[...] = jnp.zeros_like(s_scr)

        xs = _load_shifted(xbuf, x_ref, halo_ref, i == 0)
        ss_ref[0] = s_scr[...]
        S = [s_scr[h] for h in range(DN_HEADS)]
        cws = [cw_ref[k:k + 1, :] for k in range(4)]
        y, s_new, t_invs = _dn_chunk(S, xs, x_ref[:, 2048:2176], x_ref[:, 1536:2048], cws,
                                     al_ref[...], dt_ref[...], dn_ref[...])
        y_ref[...] = y
        for h in range(DN_HEADS):
            s_scr[h] = s_new[h]
            ts_ref[0, h] = t_invs[h]

    per = DN_C // HALO
    full = lambda shape: pl.BlockSpec(shape, lambda i: (0,) * len(shape))
    return pl.pallas_call(
        body, name=name,
        out_shape=[jax.ShapeDtypeStruct((T, 512), F32),
                   jax.ShapeDtypeStruct((n, DN_HEADS, DN_D, DN_D), F32),
                   jax.ShapeDtypeStruct((n, DN_HEADS, DN_D, DN_D), F32)],
        grid=(n,),
        in_specs=[pl.BlockSpec((DN_C, DN_W), lambda i: (i, jblk)),
                  pl.BlockSpec((HALO, DN_W), lambda i: (jnp.maximum(i * per - 1, 0), jblk)),
                  full((4, 1536)), full((1, 128)), full((1, 128)), full((1, 128))],
        out_specs=[pl.BlockSpec((DN_C, 512), lambda i: (i, 0)),
                   pl.BlockSpec((1, DN_HEADS, DN_D, DN_D), lambda i: (i, 0, 0, 0)),
                   pl.BlockSpec((1, DN_HEADS, DN_D, DN_D), lambda i: (i, 0, 0, 0))],
        scratch_shapes=[pltpu.VMEM((DN_HEADS, DN_D, DN_D), F32), pltpu.VMEM((HALO + DN_C, 1536), F32)],
        compiler_params=pltpu.CompilerParams(dimension_semantics=("arbitrary",)),
    )(cols, cols, cw, al, dt, dn)


def dn_backward(cols, jblk, cw, al, dt, dn, ss, ts, dy, dcols, name):
    T = cols.shape[0]
    n = T // DN_C

    def body(x_ref, halo_ref, cw_ref, al_ref, dt_ref, dn_ref, ss_ref, ts_ref, dy_ref, dcols_in,
             dx_ref, dcw_ref, dal_ref, ddt_ref, ddn_ref, ds_scr, xbuf, dbuf, carry):
        i = pl.program_id(0)

        @pl.when(i == 0)
        def _():
            ds_scr[...] = jnp.zeros_like(ds_scr)
            carry[...] = jnp.zeros_like(carry)
            dcw_ref[...] = jnp.zeros_like(dcw_ref)
            dal_ref[...] = jnp.zeros_like(dal_ref)
            ddt_ref[...] = jnp.zeros_like(ddt_ref)
            ddn_ref[...] = jnp.zeros_like(ddn_ref)

        xs = _load_shifted(xbuf, x_ref, halo_ref, i == n - 1)
        S = [ss_ref[0, h] for h in range(DN_HEADS)]
        cws = [cw_ref[k:k + 1, :] for k in range(4)]

        t_saved = [ts_ref[0, h] for h in range(DN_HEADS)]

        def f(S, xs, ba, z, cws, al, dt, dn):
            return _dn_chunk(S, xs, ba, z, cws, al, dt, dn, t_saved)[:2]

        _, vjp = jax.vjp(f, S, xs, x_ref[:, 2048:2176], x_ref[:, 1536:2048], cws, al_ref[...], dt_ref[...], dn_ref[...])
        dS, dxs, dba, dz, dcws, dal, ddt, ddn = vjp((dy_ref[...], tuple(ds_scr[h] for h in range(DN_HEADS))))
        for h in range(DN_HEADS):
            ds_scr[h] = dS[h]
        dbuf[...] = jnp.zeros_like(dbuf)
        for k in range(4):
            lo = HALO - 3 + k
            dbuf[lo:lo + DN_C, :] += dxs[k]
        dbuf[DN_C:DN_C + HALO, :] += carry[...]
        dx_ref[...] = jnp.concatenate([dbuf[HALO:HALO + DN_C, :], dz, dba,
                                       jnp.zeros((DN_C, DN_W - 2176), F32)], axis=1).astype(dx_ref.dtype)
        carry[...] = dbuf[0:HALO, :]
        for k in range(4):
            dcw_ref[k:k + 1, :] += dcws[k]
        dal_ref[...] += dal
        ddt_ref[...] += ddt
        ddn_ref[...] += ddn

    per = DN_C // HALO
    rev = lambda i: n - 1 - i
    full = lambda shape: pl.BlockSpec(shape, lambda i: (0,) * len(shape))
    return pl.pallas_call(
        body, name=name,
        out_shape=[jax.ShapeDtypeStruct(dcols.shape, dcols.dtype),jax.ShapeDtypeStruct((4, 1536), F32),
                   jax.ShapeDtypeStruct((1, 128), F32), jax.ShapeDtypeStruct((1, 128), F32),
                   jax.ShapeDtypeStruct((1, 128), F32)],
        grid=(n,),
        in_specs=[pl.BlockSpec((DN_C, DN_W), lambda i: (rev(i), jblk)),
                  pl.BlockSpec((HALO, DN_W), lambda i: (jnp.maximum(rev(i) * per - 1, 0), jblk)),
                  full((4, 1536)), full((1, 128)), full((1, 128)), full((1, 128)),
                  pl.BlockSpec((1, DN_HEADS, DN_D, DN_D), lambda i: (rev(i), 0, 0, 0)),
                  pl.BlockSpec((1, DN_HEADS, DN_D, DN_D), lambda i: (rev(i), 0, 0, 0)),
                  pl.BlockSpec((DN_C, 512), lambda i: (rev(i), 0)), _ANY],
        out_specs=[pl.BlockSpec((DN_C, DN_W), lambda i: (rev(i), jblk)),
                   full((4, 1536)), full((1, 128)), full((1, 128)), full((1, 128))],
        scratch_shapes=[pltpu.VMEM((DN_HEADS, DN_D, DN_D), F32), pltpu.VMEM((HALO + DN_C, 1536), F32),
                        pltpu.VMEM((HALO + DN_C, 1536), F32), pltpu.VMEM((HALO, 1536), F32)],
        input_output_aliases={9: 0},
        compiler_params=pltpu.CompilerParams(dimension_semantics=("arbitrary",)),
    )(cols, cols, cw, al, dt, dn, ss, ts, dy, dcols)


def _full(shape):
    return pl.BlockSpec(shape, lambda i: (0,) * len(shape))


def _silu(x):
    return x * _sigmoid(x)


def _gelu(x):
    return 0.5 * x * (1.0 + jnp.tanh(0.7978845608028654 * (x + 0.044715 * (x * x * x))))


def _lane_col(mat, idx):
    lane = lax.broadcasted_iota(jnp.int32, (1, mat.shape[1]), 1)
    return jnp.sum(jnp.where(lane == idx, mat, 0.0), axis=1, keepdims=True)


def _gm_chunk(uv, z, gain, ws, bt):
    g = _gelu(uv)
    u, v = g[:, :512], g[:, 512:]
    v = v * lax.rsqrt(jnp.mean(v * v, axis=1, keepdims=True) + EPS) * gain
    row_i = lax.broadcasted_iota(jnp.int32, (BLK, BLK), 0)
    col_i = lax.broadcasted_iota(jnp.int32, (BLK, BLK), 1)
    causal = row_i >= col_i
    ss = []
    for grp in range(4):
        wg = jnp.where(causal, ws[grp], 0.0)
        ss.append(bdot("nn", wg, v[:, BLK * grp:BLK * (grp + 1)]) + _lane_col(bt, grp))
    return u * jnp.concatenate(ss, axis=1) * _silu(z)


def gm_forward(cols, jblk, gain, ws, bt, name):
    T = cols.shape[0]

    def body(x_ref, gain_ref, ws_ref, bt_ref, y_ref):
        y_ref[...] = _gm_chunk(x_ref[:, 0:1024], x_ref[:, 1024:1536], gain_ref[...],
                               [ws_ref[g] for g in range(4)], bt_ref[...])

    return pl.pallas_call(
        body, name=name, out_shape=jax.ShapeDtypeStruct((T, 512), F32), grid=(T // BLK,),
        in_specs=[pl.BlockSpec((BLK, GM_W), lambda i: (i, jblk)),
                  _full((1, 512)), _full((4, BLK, BLK)), _full((BLK, BLK))],
        out_specs=pl.BlockSpec((BLK, 512), lambda i: (i, 0)),
        compiler_params=pltpu.CompilerParams(dimension_semantics=("parallel",)),
    )(cols, gain, ws, bt)


def gm_backward(cols, jblk, gain, ws, bt, dy, dcols, name):
    T = cols.shape[0]

    def body(x_ref, gain_ref, ws_ref, bt_ref, dy_ref, dcols_in, dx_ref, dgain_ref, dws_ref, dbt_ref):
        @pl.when(pl.program_id(0) == 0)
        def _():
            dgain_ref[...] = jnp.zeros_like(dgain_ref)
            dws_ref[...] = jnp.zeros_like(dws_ref)
            dbt_ref[...] = jnp.zeros_like(dbt_ref)

        _, vjp = jax.vjp(_gm_chunk, x_ref[:, 0:1024], x_ref[:, 1024:1536], gain_ref[...],
                         [ws_ref[g] for g in range(4)], bt_ref[...])
        duv, dz, dgain, dws, dbt = vjp(dy_ref[...])
        dx_ref[...] = jnp.concatenate([duv, dz], axis=1).astype(dx_ref.dtype)
        dgain_ref[...] += dgain
        for g in range(4):
            dws_ref[g] += dws[g]
        dbt_ref[...] += dbt

    return pl.pallas_call(
        body, name=name,
        out_shape=[jax.ShapeDtypeStruct(dcols.shape, dcols.dtype),jax.ShapeDtypeStruct((1, 512), F32),
                   jax.ShapeDtypeStruct((4, BLK, BLK), F32), jax.ShapeDtypeStruct((BLK, BLK), F32)],
        grid=(T // BLK,),
        in_specs=[pl.BlockSpec((BLK, GM_W), lambda i: (i, jblk)),
                  _full((1, 512)), _full((4, BLK, BLK)), _full((BLK, BLK)),
                  pl.BlockSpec((BLK, 512), lambda i: (i, 0)), _ANY],
        out_specs=[pl.BlockSpec((BLK, GM_W), lambda i: (i, jblk)),
                   _full((1, 512)), _full((4, BLK, BLK)), _full((BLK, BLK))],
        input_output_aliases={5: 0},
        compiler_params=pltpu.CompilerParams(dimension_semantics=("arbitrary",)),
    )(cols, gain, ws, bt, dy, dcols)


def _sw_block(first, q, kp, kc, vp, vc, z, sinks):
    P = BLK
    lane = lax.broadcasted_iota(jnp.int32, (1, 128), 1)
    r = lax.broadcasted_iota(jnp.int32, (128, 128), 0)
    c = lax.broadcasted_iota(jnp.int32, (128, 128), 1)
    swap = (c == (r + 64) % 128).astype(F32)
    k2 = jnp.concatenate([kp, kc], axis=0)
    v2 = jnp.concatenate([vp, vc], axis=0)
    k2s = bdot("nn", k2, swap)
    v2s = bdot("nn", v2, swap)
    qi = lax.broadcasted_iota(jnp.int32, (P, 2 * P), 0)
    kj = lax.broadcasted_iota(jnp.int32, (P, 2 * P), 1)
    dist = qi + P - kj
    valid = (dist >= 0) & (dist < P) & ((kj >= P) | jnp.logical_not(first))
    outs = []
    for j in range(4):
        acc = jnp.zeros((P, 128), F32)
        for half in range(2):
            h = 2 * j + half
            kv = h // 4
            in_half = (lane >= 64 * half) & (lane < 64 * half + 64)
            qh = jnp.where(in_half, q[:, 128 * j:128 * (j + 1)], 0.0)
            same = (half == kv)
            s = bdot("nt", qh, k2 if same else k2s) * (64 ** -0.5)
            s = jnp.where(valid, s, NEG_INF)
            sink = _lane_col(sinks, h)
            m = lax.stop_gradient(jnp.maximum(jnp.max(s, axis=1, keepdims=True), sink))
            e = jnp.exp(s - m)
            p = e / (jnp.sum(e, axis=1, keepdims=True) + jnp.exp(sink - m))
            o = bdot("nn", p, v2 if same else v2s)
            acc = acc + jnp.where(in_half, o, 0.0)
        outs.append(acc)
    return jnp.concatenate(outs, axis=1) * _silu(z)


def _sw_specs(jblk, idx):
    prev = lambda i: jnp.maximum(idx(i) - 1, 0)
    jk = (jblk * SW_W + 1024) // 128
    return [pl.BlockSpec((BLK, SW_W), lambda i: (idx(i), jblk)),
            pl.BlockSpec((BLK, 128), lambda i: (prev(i), jk)),
            pl.BlockSpec((BLK, 128), lambda i: (prev(i), jk + 1)), _full((1, 128))]


def sw_forward(cols, jblk, sinks, name):
    T = cols.shape[0]

    def body(x_ref, kp_ref, vp_ref, s_ref, y_ref):
        y_ref[...] = _sw_block(pl.program_id(0) == 0, x_ref[:, 0:512], kp_ref[...], x_ref[:, 1024:1152],
                               vp_ref[...], x_ref[:, 1152:1280], x_ref[:, 512:1024], s_ref[...])

    return pl.pallas_call(
        body, name=name, out_shape=jax.ShapeDtypeStruct((T, 512), F32), grid=(T // BLK,),
        in_specs=_sw_specs(jblk, lambda i: i),
        out_specs=pl.BlockSpec((BLK, 512), lambda i: (i, 0)),
        compiler_params=pltpu.CompilerParams(dimension_semantics=("parallel",)),
    )(cols, cols, cols, sinks)


def sw_backward(cols, jblk, sinks, dy, dcols, name):
    T = cols.shape[0]
    n = T // BLK
    rev = lambda i: n - 1 - i

    def body(x_ref, kp_ref, vp_ref, s_ref, dy_ref, dcols_in, dx_ref, ds_ref, kcarry, vcarry):
        i = pl.program_id(0)

        @pl.when(i == 0)
        def _():
            kcarry[...] = jnp.zeros_like(kcarry)
            vcarry[...] = jnp.zeros_like(vcarry)
            ds_ref[...] = jnp.zeros_like(ds_ref)

        f = functools.partial(_sw_block, i == n - 1)
        _, vjp = jax.vjp(f, x_ref[:, 0:512], kp_ref[...], x_ref[:, 1024:1152], vp_ref[...], x_ref[:, 1152:1280],
                         x_ref[:, 512:1024], s_ref[...])
        dq, dkp, dkc, dvp, dvc, dz, dsk = vjp(dy_ref[...])
        dx_ref[...] = jnp.concatenate([dq, dz, dkc + kcarry[...], dvc + vcarry[...],
                                       jnp.zeros((BLK, SW_W - 1280), F32)], axis=1).astype(dx_ref.dtype)
        kcarry[...] = dkp
        vcarry[...] = dvp
        ds_ref[...] += dsk

    return pl.pallas_call(
        body, name=name,
        out_shape=[jax.ShapeDtypeStruct(dcols.shape, dcols.dtype),jax.ShapeDtypeStruct((1, 128), F32)],
        grid=(n,),
        in_specs=_sw_specs(jblk, rev) + [pl.BlockSpec((BLK, 512), lambda i: (rev(i), 0)), _ANY],
        out_specs=[pl.BlockSpec((BLK, SW_W), lambda i: (rev(i), jblk)), _full((1, 128))],
        scratch_shapes=[pltpu.VMEM((BLK, 128), F32), pltpu.VMEM((BLK, 128), F32)],
        input_output_aliases={5: 0},
        compiler_params=pltpu.CompilerParams(dimension_semantics=("arbitrary",)),
    )(cols, cols, cols, sinks, dy, dcols)


XM_TQ = 256


def _xm_block(q, z, mkv):
    outs = []
    for h in range(4):
        s = bdot("nt", q[:, 128 * h:128 * (h + 1)], mkv[:, 128 * h:128 * (h + 1)]) * (128 ** -0.5)
        m = lax.stop_gradient(jnp.max(s, axis=1, keepdims=True))
        e = jnp.exp(s - m)
        p = e / jnp.sum(e, axis=1, keepdims=True)
        outs.append(bdot("nn", p, mkv[:, 512 + 128 * h:512 + 128 * (h + 1)]))
    return jnp.concatenate(outs, axis=1) * _silu(z)


def xm_forward(cols, jblk, mkv, name):
    T = cols.shape[0]

    def body(x_ref, m_ref, y_ref):
        y_ref[...] = _xm_block(x_ref[:, 0:512], x_ref[:, 512:1024], m_ref[...])

    return pl.pallas_call(
        body, name=name, out_shape=jax.ShapeDtypeStruct((T, 512), F32), grid=(T // XM_TQ,),
        in_specs=[pl.BlockSpec((XM_TQ, XM_W), lambda i: (i, jblk)), _full(mkv.shape)],
        out_specs=pl.BlockSpec((XM_TQ, 512), lambda i: (i, 0)),
        compiler_params=pltpu.CompilerParams(dimension_semantics=("parallel",)),
    )(cols, mkv)


def xm_backward(cols, jblk, mkv, dy, dcols, name):
    T = cols.shape[0]

    def body(x_ref, m_ref, dy_ref, dcols_in, dx_ref, dm_ref):
        @pl.when(pl.program_id(0) == 0)
        def _():
            dm_ref[...] = jnp.zeros_like(dm_ref)

        _, vjp = jax.vjp(_xm_block, x_ref[:, 0:512], x_ref[:, 512:1024], m_ref[...])
        dq, dz, dm = vjp(dy_ref[...])
        dx_ref[...] = jnp.concatenate([dq, dz], axis=1).astype(dx_ref.dtype)
        dm_ref[...] += dm

    return pl.pallas_call(
        body, name=name,
        out_shape=[jax.ShapeDtypeStruct(dcols.shape, dcols.dtype),jax.ShapeDtypeStruct(mkv.shape, F32)],
        grid=(T // XM_TQ,),
        in_specs=[pl.BlockSpec((XM_TQ, XM_W), lambda i: (i, jblk)), _full(mkv.shape),
                  pl.BlockSpec((XM_TQ, 512), lambda i: (i, 0)), _ANY],
        out_specs=[pl.BlockSpec((XM_TQ, XM_W), lambda i: (i, jblk)), _full(mkv.shape)],
        input_output_aliases={3: 0},
        compiler_params=pltpu.CompilerParams(dimension_semantics=("arbitrary",)),
    )(cols, mkv, dy, dcols)


def _rms(x, gain):
    return x * lax.rsqrt(jnp.mean(x * x, axis=1, keepdims=True) + EPS) * gain


def memkv_forward(mem, gain, w, name):
    def body(m_ref, g_ref, w_ref, o_ref):
        o_ref[...] = bdot("nn", _rms(m_ref[...], g_ref[...]), w_ref[...])

    return pl.pallas_call(body, name=name, out_shape=jax.ShapeDtypeStruct(mem.shape, F32),
                          compiler_params=pltpu.CompilerParams(vmem_limit_bytes=VMEM_LIMIT))(mem, gain, w)


def memkv_backward(mem, gain, w, dkv, name):
    def body(m_ref, g_ref, w_ref, d_ref, dg_ref, dw_ref):
        mem_v = m_ref[...]
        _, vjp = jax.vjp(lambda g, ww: bdot("nn", _rms(mem_v, g), ww), g_ref[...], w_ref[...].astype(F32))
        dg, dw = vjp(d_ref[...])
        dg_ref[...] = dg
        dw_ref[...] = dw

    return pl.pallas_call(body, name=name,
                          out_shape=[jax.ShapeDtypeStruct(gain.shape, F32), jax.ShapeDtypeStruct(w.shape, F32)],
                          compiler_params=pltpu.CompilerParams(vmem_limit_bytes=VMEM_LIMIT))(mem, gain, w, dkv)


MG_TB = 256


def _merge_block(ys, gl, wup, wout, gpost):
    merged = None
    for n in range(4):
        t = _sigmoid(gl[:, 1024 * n:1024 * (n + 1)]) * bdot("nn", ys[n], wup[n])
        merged = t if merged is None else merged + t
    out = bdot("nn", merged, wout)
    return _rms(out, gpost)


def merge_forward(ys, cols, jgate, x, wup, wout, gpost, name):
    T = x.shape[0]
    TB = 256

    def body(ya, yb, yc, ym, gl_ref, x_ref, wup_ref, wout_ref, gp_ref, o_ref):
        upd = _merge_block([ya[...], yb[...], yc[...], ym[...]], gl_ref[...],
                           [wup_ref[n] for n in range(4)], wout_ref[...], gp_ref[...])
        o_ref[...] = x_ref[...] + upd

    yspec = pl.BlockSpec((TB, 512), lambda i: (i, 0))
    return pl.pallas_call(
        body, name=name, out_shape=jax.ShapeDtypeStruct((T, 1024), F32), grid=(T // TB,),
        in_specs=[yspec] * 4 + [pl.BlockSpec((TB, 4096), lambda i: (i, jgate)),
                                pl.BlockSpec((TB, 1024), lambda i: (i, 0)),
                                _full(wup.shape), _full(wout.shape), _full((1, 1024))],
        out_specs=pl.BlockSpec((TB, 1024), lambda i: (i, 0)),
        compiler_params=pltpu.CompilerParams(dimension_semantics=("parallel",), vmem_limit_bytes=VMEM_LIMIT),
    )(*ys, cols, x, wup, wout, gpost)


def _token_product(a, b, name):
    (T, m), n = a.shape, b.shape[1]

    def body(a_ref, b_ref, o_ref):
        o_ref[...] = lax.dot_general(a_ref[...].astype(BF), b_ref[...].astype(BF), _DIMS["tn"], preferred_element_type=F32)

    return pl.pallas_call(body, name=name, out_shape=jax.ShapeDtypeStruct((m, n), F32),
                          compiler_params=pltpu.CompilerParams(vmem_limit_bytes=VMEM_LIMIT))(a, b)


def merge_backward(ys, cols, jgate, wup, wout, gpost, dx, name):
    T = dx.shape[0]
    TB = MG_TB

    def body(ya, yb, yc, ym, gl_ref, wup_ref, wout_ref, gp_ref, dx_ref,
             dgl_ref, dya, dyb, dyc, dym, dpa, dpb, dpc, dpm, merged_ref, dout_ref, dgp_ref):
        @pl.when(pl.program_id(0) == 0)
        def _():
            dgp_ref[...] = jnp.zeros_like(dgp_ref)

        y_refs = (ya, yb, yc, ym)
        gates = [_sigmoid(gl_ref[:, 1024 * n:1024 * (n + 1)]) for n in range(4)]
        projs = [bdot("nn", y_refs[n][...], wup_ref[n]) for n in range(4)]
        merged = gates[0] * projs[0] + gates[1] * projs[1] + gates[2] * projs[2] + gates[3] * projs[3]
        out = bdot("nn", merged, wout_ref[...])
        _, vjp = jax.vjp(_rms, out, gp_ref[...])
        dout, dgp = vjp(dx_ref[...])
        dmerged = bdot("nt", dout, wout_ref[...])
        for n, (dy_ref, dp_ref) in enumerate(zip((dya, dyb, dyc, dym), (dpa, dpb, dpc, dpm))):
            dproj = dmerged * gates[n]
            dgl_ref[:, 1024 * n:1024 * (n + 1)] = (dmerged * projs[n] * gates[n] * (1.0 - gates[n])).astype(dgl_ref.dtype)
            dy_ref[...] = bdot("nt", dproj, wup_ref[n])
            dp_ref[...] = dproj.astype(BF)
        merged_ref[...] = merged.astype(BF)
        dout_ref[...] = dout.astype(BF)
        dgp_ref[...] += dgp

    yspec = pl.BlockSpec((TB, 512), lambda i: (i, 0))
    dspec = pl.BlockSpec((TB, 1024), lambda i: (i, 0))
    dcols, dya, dyb, dyc, dym, *dproj, merged, dout, dgp = pl.pallas_call(
        body, name=name,
        out_shape=[jax.ShapeDtypeStruct(cols.shape, BF)] + [jax.ShapeDtypeStruct((T, 512), F32)] * 4 + [
            jax.ShapeDtypeStruct((T, 1024), BF)] * 6 + [jax.ShapeDtypeStruct((1, 1024), F32)],
        grid=(T // TB,),
        in_specs=[yspec] * 4 + [pl.BlockSpec((TB, 4096), lambda i: (i, jgate)),
                                _full(wup.shape), _full(wout.shape), _full((1, 1024)), dspec],
        out_specs=[pl.BlockSpec((TB, 4096), lambda i: (i, jgate))] + [yspec] * 4 + [
            dspec] * 6 + [_full((1, 1024))],
        compiler_params=pltpu.CompilerParams(dimension_semantics=("arbitrary",), vmem_limit_bytes=VMEM_LIMIT),
    )(*ys, cols, wup, wout, gpost, dx)
    dwup = jnp.stack([_token_product(ys[n], dproj[n], "%s_w_up%d" % (name, n)) for n in range(4)])
    dwout = _token_product(merged, dout, name + "_w_out")
    return dcols, dya, dyb, dyc, dym, dwup, dwout, dgp


NB = 256


def prenorm_forward(x, gain, name):
    T, D = x.shape

    def body(x_ref, g_ref, o_ref, ot_ref):
        h = _rms(x_ref[...], g_ref[...])
        o_ref[...] = h.astype(BF)
        ot_ref[...] = h.T.astype(BF)

    return pl.pallas_call(
        body, name=name,
        out_shape=[jax.ShapeDtypeStruct((T, D), BF), jax.ShapeDtypeStruct((D, T), BF)], grid=(T // NB,),
        in_specs=[pl.BlockSpec((NB, D), lambda i: (i, 0)), _full((1, D))],
        out_specs=[pl.BlockSpec((NB, D), lambda i: (i, 0)), pl.BlockSpec((D, NB), lambda i: (0, i))],
        compiler_params=pltpu.CompilerParams(dimension_semantics=("parallel",)),
    )(x, gain)


def prenorm_backward(x, gain, dh, dres, name):
    T = x.shape[0]

    def body(x_ref, g_ref, dh_ref, dr_ref, dx_ref, dg_ref):
        @pl.when(pl.program_id(0) == 0)
        def _():
            dg_ref[...] = jnp.zeros_like(dg_ref)

        _, vjp = jax.vjp(_rms, x_ref[...], g_ref[...])
        dxn, dg = vjp(dh_ref[...])
        dx_ref[...] = dr_ref[...] + dxn
        dg_ref[...] += dg

    spec = pl.BlockSpec((NB, 1024), lambda i: (i, 0))
    return pl.pallas_call(
        body, name=name,
        out_shape=[jax.ShapeDtypeStruct(x.shape, F32), jax.ShapeDtypeStruct((1, 1024), F32)], grid=(T // NB,),
        in_specs=[spec, _full((1, 1024)), spec, spec], out_specs=[spec, _full((1, 1024))],
        compiler_params=pltpu.CompilerParams(dimension_semantics=("arbitrary",)),
    )(x, gain, dh, dres)


def loss_head(y, target, name):
    T, D = y.shape

    def body(y_ref, t_ref, l_ref, d_ref):
        @pl.when(pl.program_id(0) == 0)
        def _():
            l_ref[...] = jnp.zeros_like(l_ref)

        err = y_ref[...] - t_ref[...]
        d_ref[...] = err * (1.0 / D)
        l_ref[...] += jnp.full(l_ref.shape, 0.5 * jnp.sum(jnp.mean(err * err, axis=1, keepdims=True)), F32)

    spec = pl.BlockSpec((NB, D), lambda i: (i, 0))
    return pl.pallas_call(
        body, name=name,
        out_shape=[jax.ShapeDtypeStruct((1, 128), F32), jax.ShapeDtypeStruct(y.shape, F32)], grid=(T // NB,),
        in_specs=[spec, spec], out_specs=[_full((1, 128)), spec],
        compiler_params=pltpu.CompilerParams(dimension_semantics=("arbitrary",)),
    )(y, target)


JB_GATE, JB_XM, JB_DN, JB_SW, JB_GM = 0, 4, 2, 5, 6
_ALIGNED_PIECES = ((5896, 4096), (4872, 512), (5384, 512), (0, 2048), (2048, 8), 504, (3592, 512), (4360, 512),
                   (4104, 128), (4232, 128), 256, (2056, 1024), (3080, 512))
_NATURAL_FROM_ALIGNED = ((5120, 2048), (7168, 8), (9216, 1024), (10240, 512), (7680, 512), (8704, 128), (8832, 128),
                         (8192, 512), (4096, 512), (4608, 512), (0, 4096))


def _natural_range(slots, start, width):
    out = []
    while width > 0:
        j, i = divmod(start, W_IN_SHARD)
        take = min(width, W_IN_SHARD - i)
        out.append(slots[j, :, i:i + take])
        start, width = start + take, width - take
    return out


def _aligned_w_in(slots):
    parts = []
    for piece in _ALIGNED_PIECES:
        if isinstance(piece, int):
            parts.append(jnp.zeros(slots.shape[1:2] + (piece,), slots.dtype))
        else:
            parts += _natural_range(slots, *piece)
    return jnp.concatenate(parts, axis=-1)


def _slots_of_aligned(d_al):
    slots = []
    for s in range(N_DEV):
        lo, hi = s * W_IN_SHARD, (s + 1) * W_IN_SHARD
        parts, nat = [], 0
        for a_start, width in _NATURAL_FROM_ALIGNED:
            b, e = max(lo, nat), min(hi, nat + width)
            if b < e:
                parts.append(d_al[..., a_start + b - nat:a_start + e - nat])
            nat += width
        parts.append(jnp.zeros(d_al.shape[:1] + (W_IN_SHARD_PAD - W_IN_SHARD,), d_al.dtype))
        slots.append(jnp.concatenate(parts, axis=-1))
    return jnp.stack(slots)


SMALL_VEC_W = 1024


def _pack_small(parts):
    rows = []
    for p in parts:
        flat = p.reshape(-1).astype(F32)
        r = -(-flat.shape[0] // SMALL_VEC_W)
        rows.append(jnp.pad(flat, (0, r * SMALL_VEC_W - flat.shape[0])).reshape(r, SMALL_VEC_W))
    vec = jnp.concatenate(rows, axis=0)
    return jnp.pad(vec, ((0, -vec.shape[0] % 8), (0, 0)))


def _unpack_small(vec, shapes):
    out, off = [], 0
    for s in shapes:
        n = math.prod(s)
        r = -(-n // SMALL_VEC_W)
        out.append(vec[off:off + r].reshape(-1)[:n].reshape(s))
        off += r
    return out


def _lanes(vec, at):
    return jnp.zeros((1, 128), F32).at[0, at:at + vec.shape[0]].set(vec)


SMALL_NAMES = ("norm_pre", "norm_post", "norm_mem", "a_log", "dt_bias", "dn_norm", "gm_norm",
               "spatial_w", "spatial_b", "sinks")


def _other_weights(s_mem, s_up, s_out):
    return (s_mem.reshape(D_MODEL, 2 * BRANCH_W),
            jnp.transpose(s_up, (1, 2, 0, 3)).reshape(N_BRANCH, BRANCH_W, D_MODEL), s_out.reshape(D_MODEL, D_MODEL))


def _grad_slots(d_in_al, d_mem, d_up, d_out):
    return [None if d_in_al is None else _slots_of_aligned(d_in_al), d_mem.astype(BF).reshape(N_DEV, 128, 2 * BRANCH_W),
            jnp.transpose(d_up.astype(BF).reshape(N_BRANCH, BRANCH_W, N_DEV, 128), (2, 0, 1, 3)),
            d_out.astype(BF).reshape(N_DEV, 128, D_MODEL)]


def _layer_params(l, small, conv_full, token):
    return dict(
        gpre=small["norm_pre"][l][None] + token, gpost=small["norm_post"][l][None], gmem=small["norm_mem"][l][None],
        cw=conv_full[l], al=_lanes(small["a_log"][l], 4), dt=_lanes(small["dt_bias"][l], 4),
        dnn=small["dn_norm"][l][None], gain=small["gm_norm"][l][None], ws=small["spatial_w"][l],
        bt=jnp.zeros((128, 128), F32).at[:, :GM_GROUPS].set(small["spatial_b"][l].T),
        sinks=_lanes(small["sinks"][l], 0))


def _layer_forward(l, xl, mem, p, w_in_al, other_weights):
    t = "l%d_" % l
    h, h_t = prenorm_forward(xl, p["gpre"], t + "prenorm")
    cols = _matmul(h, w_in_al, "nn", F32, (1024, 1536, 1024), t + "w_in")
    ya, ss, ts = dn_forward(cols, JB_DN, p["cw"], p["al"], p["dt"], p["dnn"], t + "deltanet")
    yb = gm_forward(cols, JB_GM, p["gain"], p["ws"], p["bt"], t + "gmlp")
    yc = sw_forward(cols, JB_SW, p["sinks"], t + "swa")
    w_mem, w_up, w_out = other_weights(yc)
    mkv = memkv_forward(mem, p["gmem"], w_mem, t + "memkv")
    ym = xm_forward(cols, JB_XM, mkv, t + "memattn")
    xn = merge_forward([ya, yb, yc, ym], cols, JB_GATE, xl, w_up, w_out, p["gpost"], t + "merge")
    return xn, dict(p, x=xl, h_t=h_t, cols=cols, mkv=mkv, ss=ss, ts=ts, ys=[ya, yb, yc, ym]), (w_in_al, w_mem, w_up, w_out)


def _layer_backward(l, s, mem, weights, dx, token, early=None):
    w_in_al, w_mem, w_up, w_out = weights
    t = "l%d_" % l
    cols = s["cols"]
    dcols, dya, dyb, dyc, dym, dwup, dwout, dgpost = merge_backward(
        s["ys"], cols, JB_GATE, w_up, w_out, s["gpost"] + token, dx, t + "merge_bwd")
    dcols, dmkv = xm_backward(cols, JB_XM, s["mkv"], dym, dcols, t + "memattn_bwd")
    dgmem, dwmem = memkv_backward(mem, s["gmem"], w_mem, dmkv, t + "memkv_bwd")
    sinks = s["sinks"] if early is None else s["sinks"] + early(dwmem, dwup, dwout)
    dcols, dsinks = sw_backward(cols, JB_SW, sinks, dyc, dcols, t + "swa_bwd")
    dcols, dgain, dws, dbt = gm_backward(cols, JB_GM, s["gain"], s["ws"], s["bt"], dyb, dcols, t + "gmlp_bwd")
    dcols, dcw, dal, ddt, ddn = dn_backward(
        cols, JB_DN, s["cw"], s["al"], s["dt"], s["dnn"], s["ss"], s["ts"], dya, dcols, t + "deltanet_bwd")
    dh = _matmul(dcols, w_in_al, "nt", F32, (1024, 1024, 1024), t + "w_in_bwd_x")
    dwin = _matmul(s["h_t"], dcols, "nn", BF, (1024, 1536, 2048), t + "w_in_bwd_w")
    dx, dgpre = prenorm_backward(s["x"], s["gpre"], dh, dx, t + "prenorm_bwd")
    gsmall = dict(norm_pre=dgpre[0], norm_post=dgpost[0], norm_mem=dgmem[0], a_log=dal[0, 4:8], dt_bias=ddt[0, 4:8],
                  dn_norm=ddn[0], gm_norm=dgain[0], spatial_w=dws, spatial_b=dbt[:, :GM_GROUPS].T,
                  sinks=dsinks[0, :SW_HEADS], conv_w=dcw)
    return dx, gsmall, (dwin, dwmem, dwup, dwout)


def kernel(x, mem, norm_pre, norm_post, norm_mem, w_in, conv_w, a_log, dt_bias, dn_norm, gm_norm, spatial_w, spatial_b, sinks, w_mem_kv, w_up, w_out, loss_target, m_norm_pre, m_norm_post, m_norm_mem, m_w_in, m_conv_w, m_a_log, m_dt_bias, m_dn_norm, m_gm_norm, m_spatial_w, m_spatial_b, m_sinks, m_w_mem_kv, m_w_up, m_w_out, v_norm_pre, v_norm_post, v_norm_mem, v_w_in, v_conv_w, v_a_log, v_dt_bias, v_dn_norm, v_gm_norm, v_spatial_w, v_spatial_b, v_sinks, v_w_mem_kv, v_w_up, v_w_out):
    xi, yi, ci = _my_place()
    my_slot = 4 * xi + 2 * yi + ci
    conv_shard = conv_w.shape[-1]
    x2, mem2, target = x[0], mem[0], loss_target[0]

    w_in_pad = jnp.pad(w_in.astype(BF), ((0, 0), (0, 0), (0, W_IN_SHARD_PAD - W_IN_SHARD)))
    shards = [[w_in_pad[l], w_mem_kv[l].astype(BF), w_up[l].astype(BF), w_out[l].astype(BF)] for l in range(DEPTH)]
    w_in_slots0, conv_slots = _all_gather_slots([shards[0][0], conv_w], "gather_weights_l0")
    ag = list(_spread_start(shards[0][1:] + shards[1], "gather", "gather_weights_rest_start"))
    conv_full = jnp.transpose(conv_slots, (1, 2, 0, 3)).reshape(DEPTH, CONV_W, N_DEV * conv_shard)
    small = dict(norm_pre=norm_pre, norm_post=norm_post, norm_mem=norm_mem, a_log=a_log,
                 dt_bias=dt_bias, dn_norm=dn_norm, gm_norm=gm_norm, spatial_w=spatial_w,
                 spatial_b=spatial_b, sinks=sinks)

    def arrived(which, after, name):
        ag[2], ag[3] = _spread_wait(ag[0], ag[1], ag[2], ag[3], which, after, name)
        return [ag[3][a] for a in which]

    x1, saved0, weights0 = _layer_forward(
        0, x2, mem2, _layer_params(0, small, conv_full, ag[4][0, 0]), _aligned_w_in(w_in_slots0),
        lambda y: _other_weights(*arrived([0, 1, 2], y, "gather_weights_l0_rest_wait")))
    w_in_slots1, = arrived([3], x1, "gather_weights_l1_w_in_wait")
    x_out, saved1, weights1 = _layer_forward(
        1, x1, mem2, _layer_params(1, small, conv_full, 0.0), _aligned_w_in(w_in_slots1),
        lambda y: _other_weights(*arrived([4, 5, 6], y, "gather_weights_l1_rest_wait")))
    loss, dx = loss_head(x_out, target, "loss_head")

    dx, gsmall1, gbig1 = _layer_backward(1, saved1, mem2, weights1, dx, 0.0)
    rs_send, rs_recv, rs_src, rs_land, rs_token = _spread_start(_grad_slots(*gbig1), "scatter", "exchange_grads_l1_start")
    rest0 = []

    def send_rest0(dwmem, dwup, dwout):
        rest0.extend(_spread_start(_grad_slots(None, dwmem, dwup, dwout)[1:], "scatter", "exchange_grads_l0_rest_start"))
        return rest0[4][0, 0]

    dx, gsmall0, gbig0 = _layer_backward(0, saved0, mem2, weights0, dx, rs_token[0, 0], send_rest0)
    _, parts1 = _spread_wait(rs_send, rs_recv, rs_src, rs_land, range(4), dx, "exchange_grads_l1_wait")

    packed_names = SMALL_NAMES + ("conv_w",)
    gs = {n: jnp.stack([gsmall0[n], gsmall1[n]]) for n in packed_names}
    small_parts = [loss[0, :1]] + [gs[n] for n in packed_names]
    tot = _unpack_small(_all_reduce_vmem(_pack_small(small_parts), "all_reduce_small"), [p.shape for p in small_parts])
    loss_tot = tot[0][0]
    grads = dict(zip(packed_names, tot[1:]))
    grads["conv_w"] = lax.dynamic_slice_in_dim(grads["conv_w"], my_slot * conv_shard, conv_shard, axis=2)

    g_win0 = _slots_of_aligned(gbig0[0])
    g_win0 = g_win0.reshape((N_DEV // 2, 2) + g_win0.shape[1:])
    theirs, = _exchange_sibling([g_win0], "exchange_sibling_l0")
    chip_sum = _pair_sum(g_win0, theirs, "pair_sum_l0")
    ch_send, ch_recv, ch_src, ch_land, ch_token = _spread_start([chip_sum], "chips", "exchange_chips_l0_start")

    given = dict(norm_pre=(norm_pre, m_norm_pre, v_norm_pre), norm_post=(norm_post, m_norm_post, v_norm_post),
                 norm_mem=(norm_mem, m_norm_mem, v_norm_mem), a_log=(a_log, m_a_log, v_a_log),
                 dt_bias=(dt_bias, m_dt_bias, v_dt_bias), dn_norm=(dn_norm, m_dn_norm, v_dn_norm),
                 gm_norm=(gm_norm, m_gm_norm, v_gm_norm), spatial_w=(spatial_w, m_spatial_w, v_spatial_w),
                 spatial_b=(spatial_b, m_spatial_b, v_spatial_b), sinks=(sinks, m_sinks, v_sinks),
                 conv_w=(conv_w, m_conv_w, v_conv_w))
    pshapes = [given[n][0].shape for n in packed_names]
    pw, pm, pv = (_pack_small([given[n][i] for n in packed_names]) for i in range(3))
    pd, pnm, pnv = _adamw(pw + ch_token[0, 0], _pack_small([grads[n] for n in packed_names]), pm, pv, "adamw_small")
    upd = {n: t for n, t in zip(packed_names, zip(_unpack_small(pd, pshapes), _unpack_small(pnm, pshapes),
                                                  _unpack_small(pnv, pshapes)))}
    big = (("w_mem_kv", (w_mem_kv, m_w_mem_kv, v_w_mem_kv)), ("w_up", (w_up, m_w_up, v_w_up)),
           ("w_out", (w_out, m_w_out, v_w_out)))
    first = [_sum_adamw(parts1[1 + i], w, m, v, 1, None, ch_token, "adamw_%s_l1" % name)
             for i, (name, (w, m, v)) in enumerate(big)]
    _, parts0_rest = _spread_wait(*rest0[:4], range(3), first[-1][0], "exchange_grads_l0_rest_wait")
    for i, (name, (w, m, v)) in enumerate(big):
        g, d, nm, nv = _sum_adamw(parts0_rest[i], w, m, v, 0, first[i], None, "adamw_%s_l0" % name)
        grads[name], upd[name] = g, (d, nm, nv)
    _, (parts0_w_in,) = _spread_wait(ch_send, ch_recv, ch_src, ch_land, [0], upd["w_out"][0], "exchange_chips_l0_wait")
    w_in_t, m_w_in_t, v_w_in_t = (jnp.transpose(t, (2, 0, 1)) for t in (w_in, m_w_in, v_w_in))
    g, d, nm, nv = (jnp.transpose(t, (1, 2, 0)) for t in
                    _sum_adamw_t([parts0_w_in, parts1[0]], w_in_t, m_w_in_t, v_w_in_t, "adamw_w_in"))
    grads["w_in"], upd["w_in"] = g, (d, nm, nv)

    order = ("norm_pre", "norm_post", "norm_mem", "w_in", "conv_w", "a_log", "dt_bias", "dn_norm",
             "gm_norm", "spatial_w", "spatial_b", "sinks", "w_mem_kv", "w_up", "w_out")
    return (loss_tot, dx[None], *[grads[n] for n in order], *[upd[n][0] for n in order],
            *[upd[n][1] for n in order], *[upd[n][2] for n in order])
```

```python
import functools
import math

import jax
import jax.numpy as jnp
from jax import lax
from jax.experimental import pallas as pl
from jax.experimental.pallas import tpu as pltpu

MESH = pl.DeviceIdType.MESH
N_DEV = 8

D_MODEL = 1024
DEPTH = 2
N_BRANCH = 4
BRANCH_W = 512
DN_HEADS = 4
CONV_W = 4
GM_GROUPS = 4
SW_HEADS = 8
EPS = 1e-6
NEG_INF = -1e30

D_IN = 9992
W_IN_SHARD = D_IN // N_DEV
W_IN_SHARD_PAD = 1280
D_IN_AL = 10752
DN_W, SW_W, GM_W, XM_W = 2560, 1536, 1536, 1024

ADAM_LR = 0.001
ADAM_B1 = 0.9
ADAM_B2 = 0.999
ADAM_EPS = 1e-08
ADAM_WD = 0.01
ADAM_STEP = 10

VMEM_LIMIT = 56 * 1024 * 1024

BF = jnp.bfloat16
F32 = jnp.float32
DN_C = 128
DN_D = 128
HALO = 8
BLK = 128


def _my_place():
    return lax.axis_index("x"), lax.axis_index("y"), lax.axis_index("c")


_ANY = pl.BlockSpec(memory_space=pl.ANY)


def _all_gather_slots(parts, name):
    n = len(parts)

    def body(*refs):
        p_refs, out_refs = refs[:n], refs[n:2 * n]
        send_sems, recv_sems, local_sems = refs[2 * n:]
        x, y, c = _my_place()
        me, sibling = (x, y, c), (x, y, 1 - c)
        chips = [(1 - x, y), (x, 1 - y), (1 - x, 1 - y)]

        def copy(a, k, block, to, src=None):
            px, py, pc = block
            slot = out_refs[a].at[4 * px + 2 * py + pc]
            return pltpu.make_async_remote_copy(
                src_ref=slot if src is None else src, dst_ref=slot,
                send_sem=send_sems.at[7 * a + k], recv_sem=recv_sems.at[7 * a + k],
                device_id=to, device_id_type=MESH)

        mine = [pltpu.make_async_copy(p_refs[a], out_refs[a].at[4 * x + 2 * y + c], local_sems.at[a])
                for a in range(n)]
        for cp in mine:
            cp.start()
        first = []
        for a in range(n):
            first.append(copy(a, 0, me, sibling, src=p_refs[a]))
            first += [copy(a, 1 + j, me, (*chip, c), src=p_refs[a]) for j, chip in enumerate(chips)]
        for cp in first:
            cp.start()
        passed = []
        for j, chip in enumerate(chips):
            for a in range(n):
                copy(a, 1 + j, (*chip, c), me).wait_recv()
                fwd = copy(a, 4 + j, (*chip, c), sibling)
                fwd.start()
                passed.append(fwd)
        for a in range(n):
            copy(a, 0, sibling, me).wait_recv()
            for j, chip in enumerate(chips):
                copy(a, 4 + j, (*chip, 1 - c), me).wait_recv()
        for cp in first + passed:
            cp.wait_send()
        for cp in mine:
            cp.wait()

    return pl.pallas_call(
        body, name=name,
        out_shape=[jax.ShapeDtypeStruct((N_DEV,) + p.shape, p.dtype) for p in parts],
        in_specs=[_ANY] * n, out_specs=[_ANY] * n,
        scratch_shapes=[pltpu.SemaphoreType.DMA((7 * n,)), pltpu.SemaphoreType.DMA((7 * n,)),
                        pltpu.SemaphoreType.DMA((n,))],
    )(*parts)


def _exchange_sibling(parts, name):
    n = len(parts)

    def body(*refs):
        g_refs, out_refs = refs[:n], refs[n:2 * n]
        send_sems, recv_sems = refs[2 * n:]
        x, y, c = _my_place()
        copies = [pltpu.make_async_remote_copy(
            src_ref=g_refs[a].at[:, 1 - c], dst_ref=out_refs[a],
            send_sem=send_sems.at[a], recv_sem=recv_sems.at[a],
            device_id=(x, y, 1 - c), device_id_type=MESH) for a in range(n)]
        for cp in copies:
            cp.start()
        for cp in copies:
            cp.wait()

    return pl.pallas_call(
        body, name=name,
        out_shape=[jax.ShapeDtypeStruct((4,) + g.shape[2:], g.dtype) for g in parts],
        in_specs=[_ANY] * n, out_specs=[_ANY] * n,
        scratch_shapes=[pltpu.SemaphoreType.DMA((n,)), pltpu.SemaphoreType.DMA((n,))],
    )(*parts)


_HBM = pl.BlockSpec(memory_space=pltpu.HBM)
_SEM = pl.BlockSpec(memory_space=pltpu.SEMAPHORE)
_EFFECT = pltpu.SideEffectType.DATAFLOW_SIDE_EFFECTING


def _peer(x, y, c, k):
    return (1 - x if (k >> 2) & 1 else x, 1 - y if (k >> 1) & 1 else y, 1 - c if k & 1 else c)


def _spread_start(srcs, mode, name):
    n = len(srcs)
    lands = [lax.empty((N_DEV,) + s.shape if mode == "gather" else s.shape, s.dtype) for s in srcs]
    peers = range(0, N_DEV, 2) if mode == "chips" else range(N_DEV)

    def body(*refs):
        src_refs, land_refs = refs[:n], refs[n:2 * n]
        send_sems, recv_sems = refs[2 * n:2 * n + 2]
        token = refs[-1]
        x, y, c = _my_place()
        for a in range(n):
            for k in peers:
                px, py, pc = _peer(x, y, c, k)
                if mode == "chips":
                    src, mine = src_refs[a].at[2 * px + py], 2 * x + y
                else:
                    src = src_refs[a].at[4 * px + 2 * py + pc] if mode == "scatter" else src_refs[a]
                    mine = 4 * x + 2 * y + c
                pltpu.make_async_remote_copy(
                    src_ref=src, dst_ref=land_refs[a].at[mine],
                    send_sem=send_sems.at[a], recv_sem=recv_sems.at[a],
                    device_id=(px, py, pc), device_id_type=MESH).start()
        token[...] = jnp.zeros_like(token)

    out = pl.pallas_call(
        body, name=name,
        out_shape=[pltpu.SemaphoreType.DMA((n,)), pltpu.SemaphoreType.DMA((n,))]
        + [pltpu.HBM(s.shape, s.dtype) for s in srcs] + [pltpu.HBM(l.shape, l.dtype) for l in lands]
        + [jax.ShapeDtypeStruct((8, 128), F32)],
        in_specs=[_HBM] * (2 * n),
        out_specs=[_SEM, _SEM] + [_HBM] * (2 * n) + [pl.BlockSpec(memory_space=pltpu.VMEM)],
        input_output_aliases={i: 2 + i for i in range(2 * n)},
        compiler_params=pltpu.CompilerParams(has_side_effects=_EFFECT),
    )(*[pltpu.with_memory_space_constraint(s, pltpu.HBM) for s in srcs],
      *[pltpu.with_memory_space_constraint(l, pltpu.HBM) for l in lands])
    return out[0], out[1], out[2:2 + n], out[2 + n:2 + 2 * n], out[-1]


def _spread_wait(send_sems, recv_sems, srcs, lands, which, after, name):
    n = len(srcs)

    def body(*refs):
        land_refs = refs[n:2 * n]
        send_sems, recv_sems = refs[2 * n:2 * n + 2]
        x, y, c = _my_place()
        for a in which:
            whole = pltpu.make_async_remote_copy(
                src_ref=land_refs[a], dst_ref=land_refs[a],
                send_sem=send_sems.at[a], recv_sem=recv_sems.at[a],
                device_id=(x, y, c), device_id_type=MESH)
            whole.wait_send()
            whole.wait_recv()

    out = pl.pallas_call(
        body, name=name,
        out_shape=[pltpu.HBM(s.shape, s.dtype) for s in srcs] + [pltpu.HBM(l.shape, l.dtype) for l in lands],
        in_specs=[_HBM] * (2 * n) + [_SEM, _SEM, _ANY],
        out_specs=[_HBM] * (2 * n),
        input_output_aliases={i: i for i in range(2 * n)},
        compiler_params=pltpu.CompilerParams(has_side_effects=_EFFECT),
    )(*srcs, *lands, send_sems, recv_sems, after)
    return out[:n], out[n:]


def _sum_slots(parts, name):
    def body(p_ref, o_ref):
        acc = p_ref[0]
        for s in range(1, parts.shape[0]):
            acc = acc + p_ref[s]
        o_ref[...] = acc

    return pl.pallas_call(body, name=name, out_shape=jax.ShapeDtypeStruct(parts.shape[1:], parts.dtype))(parts)


def _pick(n, pref):
    if n <= pref:
        return n
    t = pref - pref % 128
    while t > 0 and n % t:
        t -= 128
    return t if t > 0 else n


_DIMS = {"nn": (((1,), (0,)), ((), ())),
         "nt": (((1,), (1,)), ((), ())),
         "tn": (((0,), (0,)), ((), ()))}


def _matmul(a, b, mode, out_dtype, tiles, name):
    (m, k) = a.shape
    n = b.shape[1] if mode == "nn" else b.shape[0]
    tm, tn, tk = (_pick(d, t) for d, t in zip((m, n, k), tiles))
    nk = k // tk

    def product(a_ref, b_ref):
        return lax.dot_general(a_ref[...].astype(BF), b_ref[...].astype(BF), _DIMS[mode], preferred_element_type=F32)

    def body_whole_k(a_ref, b_ref, o_ref):
        o_ref[...] = product(a_ref, b_ref).astype(o_ref.dtype)

    def body_split_k(a_ref, b_ref, o_ref, acc_ref):
        kk = pl.program_id(2)

        @pl.when(kk == 0)
        def _():
            acc_ref[...] = jnp.zeros_like(acc_ref)

        acc_ref[...] += product(a_ref, b_ref)

        @pl.when(kk == nk - 1)
        def _():
            o_ref[...] = acc_ref[...].astype(o_ref.dtype)

    b_spec = (pl.BlockSpec((tn, tk), lambda i, j, kk: (j, kk)) if mode == "nt"
              else pl.BlockSpec((tk, tn), lambda i, j, kk: (kk, j)))
    return pl.pallas_call(
        body_whole_k if nk == 1 else body_split_k, name=name,
        out_shape=jax.ShapeDtypeStruct((m, n), out_dtype),
        grid=(m // tm, n // tn, nk),
        in_specs=[pl.BlockSpec((tm, tk), lambda i, j, kk: (i, kk)), b_spec],
        out_specs=pl.BlockSpec((tm, tn), lambda i, j, kk: (i, j)),
        scratch_shapes=[] if nk == 1 else [pltpu.VMEM((tm, tn), F32)],
        compiler_params=pltpu.CompilerParams(
            dimension_semantics=("parallel", "parallel", "arbitrary"),
            vmem_limit_bytes=VMEM_LIMIT),
    )(a, b)


def _rows2d(t, lead):
    return t.reshape(t.shape[:lead] + (math.prod(t.shape[lead:-1]), t.shape[-1]))


def _pair_sum(g, theirs, name):
    g3, t3 = _rows2d(g, 2), _rows2d(theirs, 1)
    _, r, w = t3.shape
    tr = _pick(r, 512)

    def body(g_ref, t_ref, o_ref):
        c = lax.axis_index("c")
        mine = jnp.where(c == 0, g_ref[0, 0], g_ref[0, 1])
        o_ref[0] = (mine.astype(F32) + t_ref[0].astype(F32)).astype(o_ref.dtype)

    out = pl.pallas_call(
        body, name=name,
        out_shape=jax.ShapeDtypeStruct(t3.shape, t3.dtype),
        grid=(4, r // tr),
        in_specs=[pl.BlockSpec((1, 2, tr, w), lambda q, i: (q, 0, i, 0)),
                  pl.BlockSpec((1, tr, w), lambda q, i: (q, i, 0))],
        out_specs=pl.BlockSpec((1, tr, w), lambda q, i: (q, i, 0)),
        compiler_params=pltpu.CompilerParams(dimension_semantics=("parallel", "parallel")),
    )(g3, t3)
    return out.reshape(theirs.shape)


def _adam_update(w, g, m, v):
    c1 = 1.0 - ADAM_B1 ** ADAM_STEP
    c2 = 1.0 - ADAM_B2 ** ADAM_STEP
    nm = ADAM_B1 * m + (1.0 - ADAM_B1) * g
    nv = ADAM_B2 * v + (1.0 - ADAM_B2) * (g * g)
    delta = -ADAM_LR * ((nm / c1) / (jnp.sqrt(nv / c2) + ADAM_EPS) + ADAM_WD * w)
    return delta, nm, nv


def _sum_adamw(parts, w, m, v, layer, carry, after, name):
    shape = w.shape
    cols = shape[-1]
    p3 = _rows2d(parts, 1)
    w3, m3, v3 = (_rows2d(t, 1) for t in (w, m, v))
    rows = w3.shape[1]
    tr = _pick(rows, 128)
    n_parts = p3.shape[0]

    def body(p_ref, w_ref, m_ref, v_ref, *rest):
        g_ref, d_ref, nm_ref, nv_ref = rest[-4:]
        g = p_ref[0, :, :cols].astype(F32)
        for q in range(1, n_parts):
            g = g + p_ref[q, :, :cols].astype(F32)
        d, nm, nv = _adam_update(w_ref[0], g, m_ref[0], v_ref[0])
        g_ref[0] = g
        d_ref[0] = d
        nm_ref[0] = nm
        nv_ref[0] = nv

    spec = pl.BlockSpec((1, tr, cols), lambda i: (layer, i, 0))
    extra = [] if carry is None else [_rows2d(t, 1) for t in carry]
    tail = [] if after is None else [after]
    out = pl.pallas_call(
        body, name=name,
        out_shape=[jax.ShapeDtypeStruct(w3.shape, F32)] * 4,
        grid=(rows // tr,),
        in_specs=[pl.BlockSpec((n_parts, tr, p3.shape[-1]), lambda i: (0, i, 0)), spec, spec, spec] + [_ANY] * len(extra + tail),
        out_specs=[spec] * 4,
        input_output_aliases={4 + i: i for i in range(len(extra))},
        compiler_params=pltpu.CompilerParams(dimension_semantics=("parallel",)),
    )(p3, w3, m3, v3, *extra, *tail)
    return tuple(t.reshape(shape) for t in out)


def _sum_adamw_t(parts, w, m, v, name):
    rows = parts[0].shape[2]
    tr = 128
    assert rows % tr == 0 and rows >= w.shape[0]

    def body(*refs):
        p_refs, (w_ref, m_ref, v_ref), (g_ref, d_ref, nm_ref, nv_ref) = refs[:DEPTH], refs[DEPTH:DEPTH + 3], refs[DEPTH + 3:]
        for l in range(DEPTH):
            g = p_refs[l][0].astype(F32)
            for q in range(1, p_refs[l].shape[0]):
                g = g + p_refs[l][q].astype(F32)
            g = g.T
            d, nm, nv = _adam_update(w_ref[:, l, :], g, m_ref[:, l, :], v_ref[:, l, :])
            g_ref[:, l, :] = g
            d_ref[:, l, :] = d
            nm_ref[:, l, :] = nm
            nv_ref[:, l, :] = nv

    spec = pl.BlockSpec((tr,) + w.shape[1:], lambda i: (i, 0, 0))
    return pl.pallas_call(
        body, name=name,
        out_shape=[jax.ShapeDtypeStruct(w.shape, F32)] * 4,
        grid=(rows // tr,),
        in_specs=[pl.BlockSpec((p.shape[0], p.shape[1], tr), lambda i: (0, 0, i)) for p in parts] + [spec] * 3,
        out_specs=[spec] * 4,
        compiler_params=pltpu.CompilerParams(dimension_semantics=("parallel",)),
    )(*parts, w, m, v)


def _adamw(w, g, m, v, name):
    rows, cols = w.shape
    tr = _pick(rows, 128)

    def body(w_ref, g_ref, m_ref, v_ref, d_ref, nm_ref, nv_ref):
        d, nm, nv = _adam_update(w_ref[...], g_ref[...], m_ref[...], v_ref[...])
        d_ref[...] = d
        nm_ref[...] = nm
        nv_ref[...] = nv

    spec = pl.BlockSpec((tr, cols), lambda i: (i, 0))
    return pl.pallas_call(
        body, name=name,
        out_shape=[jax.ShapeDtypeStruct((rows, cols), F32)] * 3,
        grid=(rows // tr,),
        in_specs=[spec] * 4, out_specs=[spec] * 3,
        compiler_params=pltpu.CompilerParams(dimension_semantics=("parallel",)),
    )(w, g, m, v)


_VJP = {"nn": (("nt", "gb"), ("tn", "ag")),
        "nt": (("nn", "gb"), ("tn", "ga")),
        "tn": (("nt", "bg"), ("nn", "ag"))}


def _make_dot(cast, precision):
    def raw(mode, a, b):
        return lax.dot_general(cast(a), cast(b), _DIMS[mode], precision=precision,
                               preferred_element_type=F32)

    @functools.partial(jax.custom_vjp, nondiff_argnums=(0,))
    def dot(mode, a, b):
        return raw(mode, a, b)

    def fwd(mode, a, b):
        return raw(mode, a, b), (a, b)

    def bwd(mode, res, g):
        a, b = res
        pick = {"a": a, "b": b, "g": g}
        (ma, ta), (mb, tb) = _VJP[mode]
        return dot(ma, pick[ta[0]], pick[ta[1]]), dot(mb, pick[tb[0]], pick[tb[1]])

    dot.defvjp(fwd, bwd)
    return dot


bdot = _make_dot(lambda t: t.astype(BF), None)
hdot = _make_dot(lambda t: t, lax.Precision.HIGHEST)


def _xdot(mode, a, b):
    return lax.dot_general(a, b, _DIMS[mode], precision=lax.Precision.HIGH, preferred_element_type=F32)


def _unit_lower_inverse(Ls):
    n = Ls[0].shape[0]
    batched = (((2,), (1,)), ((0,), (0,)))
    mm = lambda a, b: lax.dot_general(a, b, batched, precision=lax.Precision.HIGH, preferred_element_type=F32)
    eye = (lax.broadcasted_iota(jnp.int32, (n, n), 0) == lax.broadcasted_iota(jnp.int32, (n, n), 1)).astype(F32)
    p = jnp.stack(Ls)
    t_inv = eye[None] - p
    for _ in range(6):
        p = mm(p, p)
        t_inv = t_inv + mm(t_inv, p)
    return [t_inv[h] for h in range(len(Ls))]


@jax.custom_vjp
def _tri_solve(L, rhs, t_inv):
    return _xdot("nn", t_inv, rhs)


def _tri_solve_fwd(L, rhs, t_inv):
    sol = _xdot("nn", t_inv, rhs)
    return sol, (t_inv, sol)


def _tri_solve_bwd(res, dsol):
    t_inv, sol = res
    drhs = _xdot("tn", t_inv, dsol)
    return -_xdot("nt", drhs, sol), drhs, jnp.zeros_like(t_inv)


_tri_solve.defvjp(_tri_solve_fwd, _tri_solve_bwd)


def _sigmoid(x):
    return 1.0 / (1.0 + jnp.exp(-x))


def _softplus(x):
    return jnp.maximum(x, 0.0) + jnp.log(1.0 + jnp.exp(-jnp.abs(x)))


def _dn_chunk(S, xs, ba, z, cw, al, dt, dn, t_saved=None):
    C = DN_C
    pre = xs[0] * cw[0] + xs[1] * cw[1] + xs[2] * cw[2] + xs[3] * cw[3]
    qkv = pre * _sigmoid(pre)
    lane = lax.broadcasted_iota(jnp.int32, (1, 128), 1)
    sub = lax.broadcasted_iota(jnp.int32, (C, 1), 0)
    row_i = lax.broadcasted_iota(jnp.int32, (C, C), 0)
    col_i = lax.broadcasted_iota(jnp.int32, (C, C), 1)
    strict = row_i > col_i
    incl = row_i >= col_i
    g_all = jnp.where((lane >= 4) & (lane < 8), -jnp.exp(al) * _softplus(ba + dt), 0.0)
    gc_all = hdot("nn", incl.astype(F32), g_all)
    gc_all_t = gc_all.T
    beta_all = _sigmoid(ba)
    glast_all = jnp.sum(jnp.where(sub == C - 1, gc_all, 0.0), axis=0, keepdims=True)
    heads = []
    for h in range(DN_HEADS):
        q = qkv[:, 128 * h:128 * (h + 1)]
        k = qkv[:, 512 + 128 * h:512 + 128 * (h + 1)]
        v = qkv[:, 1024 + 128 * h:1024 + 128 * (h + 1)]
        q = q * lax.rsqrt(jnp.sum(q * q, axis=1, keepdims=True) + EPS) * (DN_D ** -0.5)
        k = k * lax.rsqrt(jnp.sum(k * k, axis=1, keepdims=True) + EPS)
        beta = jnp.sum(jnp.where(lane == h, beta_all, 0.0), axis=1, keepdims=True)
        gc = jnp.sum(jnp.where(lane == 4 + h, gc_all, 0.0), axis=1, keepdims=True)
        gc_row = jnp.sum(jnp.where(sub == 4 + h, gc_all_t, 0.0), axis=0, keepdims=True)
        g_last = jnp.sum(jnp.where(lane == 4 + h, glast_all, 0.0), axis=1, keepdims=True)
        diff = gc - gc_row
        kb = k * beta
        L = jnp.where(strict, bdot("nt", kb, k) * jnp.exp(jnp.where(strict, diff, 0.0)), 0.0)
        heads.append((q, k, v, beta, gc, g_last, diff, kb, L))
    t_invs = _unit_lower_inverse([hd[-1] for hd in heads]) if t_saved is None else t_saved
    ys, s_new = [], []
    for h, (q, k, v, beta, gc, g_last, diff, kb, L) in enumerate(heads):
        sol = _tri_solve(L, jnp.concatenate([v * beta, kb * jnp.exp(gc)], axis=1), t_invs[h])
        u, w = sol[:, :DN_D], sol[:, DN_D:]
        a_qk = jnp.where(incl, bdot("nt", q, k) * jnp.exp(jnp.where(incl, diff, 0.0)), 0.0)
        qg = q * jnp.exp(gc)
        kd = k * jnp.exp(g_last - gc)
        v_new = u - bdot("nn", w, S[h])
        o = bdot("nn", qg, S[h]) + bdot("nn", a_qk, v_new)
        s_new.append(S[h] * jnp.exp(g_last) + bdot("tn", kd, v_new))
        o = o * lax.rsqrt(jnp.mean(o * o, axis=1, keepdims=True) + EPS) * dn
        zh = z[:, 128 * h:128 * (h + 1)]
        ys.append(o * (zh * _sigmoid(zh)))
    return jnp.concatenate(ys, axis=1), tuple(s_new), tuple(t_invs)


def _load_shifted(xbuf, x_ref, halo_ref, first):
    xbuf[0:HALO, :] = jnp.where(first, 0.0, halo_ref[:, 0:1536])
    xbuf[HALO:HALO + DN_C, :] = x_ref[:, 0:1536]
    return [xbuf[HALO - 3 + k:HALO - 3 + k + DN_C, :] for k in range(4)]


def dn_forward(cols, jblk, cw, al, dt, dn, name):
    T = cols.shape[0]
    n = T // DN_C

    def body(x_ref, halo_ref, cw_ref, al_ref, dt_ref, dn_ref, y_ref, ss_ref, ts_ref, s_scr, xbuf):
        i = pl.program_id(0)

        @pl.when(i == 0)
        def _():
            s_scr[...] = jnp.zeros_like(s_scr)

        xs = _load_shifted(xbuf, x_ref, halo_ref, i == 0)
        ss_ref[0] = s_scr[...]
        S = [s_scr[h] for h in range(DN_HEADS)]
        cws = [cw_ref[k:k + 1, :] for k in range(4)]
        y, s_new, t_invs = _dn_chunk(S, xs, x_ref[:, 2048:2176], x_ref[:, 1536:2048], cws,
                                     al_ref[...], dt_ref[...], dn_ref[...])
        y_ref[...] = y
        for h in range(DN_HEADS):
            s_scr[h] = s_new[h]
            ts_ref[0, h] = t_invs[h]

    per = DN_C // HALO
    full = lambda shape: pl.BlockSpec(shape, lambda i: (0,) * len(shape))
    return pl.pallas_call(
        body, name=name,
        out_shape=[jax.ShapeDtypeStruct((T, 512), F32),
                   jax.ShapeDtypeStruct((n, DN_HEADS, DN_D, DN_D), F32),
                   jax.ShapeDtypeStruct((n, DN_HEADS, DN_D, DN_D), F32)],
        grid=(n,),
        in_specs=[pl.BlockSpec((DN_C, DN_W), lambda i: (i, jblk)),
                  pl.BlockSpec((HALO, DN_W), lambda i: (jnp.maximum(i * per - 1, 0), jblk)),
                  full((4, 1536)), full((1, 128)), full((1, 128)), full((1, 128))],
        out_specs=[pl.BlockSpec((DN_C, 512), lambda i: (i, 0)),
                   pl.BlockSpec((1, DN_HEADS, DN_D, DN_D), lambda i: (i, 0, 0, 0)),
                   pl.BlockSpec((1, DN_HEADS, DN_D, DN_D), lambda i: (i, 0, 0, 0))],
        scratch_shapes=[pltpu.VMEM((DN_HEADS, DN_D, DN_D), F32), pltpu.VMEM((HALO + DN_C, 1536), F32)],
        compiler_params=pltpu.CompilerParams(dimension_semantics=("arbitrary",)),
    )(cols, cols, cw, al, dt, dn)


def dn_backward(cols, jblk, cw, al, dt, dn, ss, ts, dy, dcols, name):
    T = cols.shape[0]
    n = T // DN_C

    def body(x_ref, halo_ref, cw_ref, al_ref, dt_ref, dn_ref, ss_ref, ts_ref, dy_ref, dcols_in,
             dx_ref, dcw_ref, dal_ref, ddt_ref, ddn_ref, ds_scr, xbuf, dbuf, carry):
        i = pl.program_id(0)

        @pl.when(i == 0)
        def _():
            ds_scr[...] = jnp.zeros_like(ds_scr)
            carry[...] = jnp.zeros_like(carry)
            dcw_ref[...] = jnp.zeros_like(dcw_ref)
            dal_ref[...] = jnp.zeros_like(dal_ref)
            ddt_ref[...] = jnp.zeros_like(ddt_ref)
            ddn_ref[...] = jnp.zeros_like(ddn_ref)

        xs = _load_shifted(xbuf, x_ref, halo_ref, i == n - 1)
        S = [ss_ref[0, h] for h in range(DN_HEADS)]
        cws = [cw_ref[k:k + 1, :] for k in range(4)]

        t_saved = [ts_ref[0, h] for h in range(DN_HEADS)]

        def f(S, xs, ba, z, cws, al, dt, dn):
            return _dn_chunk(S, xs, ba, z, cws, al, dt, dn, t_saved)[:2]

        _, vjp = jax.vjp(f, S, xs, x_ref[:, 2048:2176], x_ref[:, 1536:2048], cws, al_ref[...], dt_ref[...], dn_ref[...])
        dS, dxs, dba, dz, dcws, dal, ddt, ddn = vjp((dy_ref[...], tuple(ds_scr[h] for h in range(DN_HEADS))))
        for h in range(DN_HEADS):
            ds_scr[h] = dS[h]
        dbuf[...] = jnp.zeros_like(dbuf)
        for k in range(4):
            lo = HALO - 3 + k
            dbuf[lo:lo + DN_C, :] += dxs[k]
        dbuf[DN_C:DN_C + HALO, :] += carry[...]
        dx_ref[...] = jnp.concatenate([dbuf[HALO:HALO + DN_C, :], dz, dba,
                                       jnp.zeros((DN_C, DN_W - 2176), F32)], axis=1).astype(dx_ref.dtype)
        carry[...] = dbuf[0:HALO, :]
        for k in range(4):
            dcw_ref[k:k + 1, :] += dcws[k]
        dal_ref[...] += dal
        ddt_ref[...] += ddt
        ddn_ref[...] += ddn

    per = DN_C // HALO
    rev = lambda i: n - 1 - i
    full = lambda shape: pl.BlockSpec(shape, lambda i: (0,) * len(shape))
    return pl.pallas_call(
        body, name=name,
        out_shape=[jax.ShapeDtypeStruct(dcols.shape, dcols.dtype),jax.ShapeDtypeStruct((4, 1536), F32),
                   jax.ShapeDtypeStruct((1, 128), F32), jax.ShapeDtypeStruct((1, 128), F32),
                   jax.ShapeDtypeStruct((1, 128), F32)],
        grid=(n,),
        in_specs=[pl.BlockSpec((DN_C, DN_W), lambda i: (rev(i), jblk)),
                  pl.BlockSpec((HALO, DN_W), lambda i: (jnp.maximum(rev(i) * per - 1, 0), jblk)),
                  full((4, 1536)), full((1, 128)), full((1, 128)), full((1, 128)),
                  pl.BlockSpec((1, DN_HEADS, DN_D, DN_D), lambda i: (rev(i), 0, 0, 0)),
                  pl.BlockSpec((1, DN_HEADS, DN_D, DN_D), lambda i: (rev(i), 0, 0, 0)),
                  pl.BlockSpec((DN_C, 512), lambda i: (rev(i), 0)), _ANY],
        out_specs=[pl.BlockSpec((DN_C, DN_W), lambda i: (rev(i), jblk)),
                   full((4, 1536)), full((1, 128)), full((1, 128)), full((1, 128))],
        scratch_shapes=[pltpu.VMEM((DN_HEADS, DN_D, DN_D), F32), pltpu.VMEM((HALO + DN_C, 1536), F32),
                        pltpu.VMEM((HALO + DN_C, 1536), F32), pltpu.VMEM((HALO, 1536), F32)],
        input_output_aliases={9: 0},
        compiler_params=pltpu.CompilerParams(dimension_semantics=("arbitrary",)),
    )(cols, cols, cw, al, dt, dn, ss, ts, dy, dcols)


def _full(shape):
    return pl.BlockSpec(shape, lambda i: (0,) * len(shape))


def _silu(x):
    return x * _sigmoid(x)


def _gelu(x):
    return 0.5 * x * (1.0 + jnp.tanh(0.7978845608028654 * (x + 0.044715 * (x * x * x))))


def _lane_col(mat, idx):
    lane = lax.broadcasted_iota(jnp.int32, (1, mat.shape[1]), 1)
    return jnp.sum(jnp.where(lane == idx, mat, 0.0), axis=1, keepdims=True)


def _gm_chunk(uv, z, gain, ws, bt):
    g = _gelu(uv)
    u, v = g[:, :512], g[:, 512:]
    v = v * lax.rsqrt(jnp.mean(v * v, axis=1, keepdims=True) + EPS) * gain
    row_i = lax.broadcasted_iota(jnp.int32, (BLK, BLK), 0)
    col_i = lax.broadcasted_iota(jnp.int32, (BLK, BLK), 1)
    causal = row_i >= col_i
    ss = []
    for grp in range(4):
        wg = jnp.where(causal, ws[grp], 0.0)
        ss.append(bdot("nn", wg, v[:, BLK * grp:BLK * (grp + 1)]) + _lane_col(bt, grp))
    return u * jnp.concatenate(ss, axis=1) * _silu(z)


def gm_forward(cols, jblk, gain, ws, bt, name):
    T = cols.shape[0]

    def body(x_ref, gain_ref, ws_ref, bt_ref, y_ref):
        y_ref[...] = _gm_chunk(x_ref[:, 0:1024], x_ref[:, 1024:1536], gain_ref[...],
                               [ws_ref[g] for g in range(4)], bt_ref[...])

    return pl.pallas_call(
        body, name=name, out_shape=jax.ShapeDtypeStruct((T, 512), F32), grid=(T // BLK,),
        in_specs=[pl.BlockSpec((BLK, GM_W), lambda i: (i, jblk)),
                  _full((1, 512)), _full((4, BLK, BLK)), _full((BLK, BLK))],
        out_specs=pl.BlockSpec((BLK, 512), lambda i: (i, 0)),
        compiler_params=pltpu.CompilerParams(dimension_semantics=("parallel",)),
    )(cols, gain, ws, bt)


def gm_backward(cols, jblk, gain, ws, bt, dy, dcols, name):
    T = cols.shape[0]

    def body(x_ref, gain_ref, ws_ref, bt_ref, dy_ref, dcols_in, dx_ref, dgain_ref, dws_ref, dbt_ref):
        @pl.when(pl.program_id(0) == 0)
        def _():
            dgain_ref[...] = jnp.zeros_like(dgain_ref)
            dws_ref[...] = jnp.zeros_like(dws_ref)
            dbt_ref[...] = jnp.zeros_like(dbt_ref)

        _, vjp = jax.vjp(_gm_chunk, x_ref[:, 0:1024], x_ref[:, 1024:1536], gain_ref[...],
                         [ws_ref[g] for g in range(4)], bt_ref[...])
        duv, dz, dgain, dws, dbt = vjp(dy_ref[...])
        dx_ref[...] = jnp.concatenate([duv, dz], axis=1).astype(dx_ref.dtype)
        dgain_ref[...] += dgain
        for g in range(4):
            dws_ref[g] += dws[g]
        dbt_ref[...] += dbt

    return pl.pallas_call(
        body, name=name,
        out_shape=[jax.ShapeDtypeStruct(dcols.shape, dcols.dtype),jax.ShapeDtypeStruct((1, 512), F32),
                   jax.ShapeDtypeStruct((4, BLK, BLK), F32), jax.ShapeDtypeStruct((BLK, BLK), F32)],
        grid=(T // BLK,),
        in_specs=[pl.BlockSpec((BLK, GM_W), lambda i: (i, jblk)),
                  _full((1, 512)), _full((4, BLK, BLK)), _full((BLK, BLK)),
                  pl.BlockSpec((BLK, 512), lambda i: (i, 0)), _ANY],
        out_specs=[pl.BlockSpec((BLK, GM_W), lambda i: (i, jblk)),
                   _full((1, 512)), _full((4, BLK, BLK)), _full((BLK, BLK))],
        input_output_aliases={5: 0},
        compiler_params=pltpu.CompilerParams(dimension_semantics=("arbitrary",)),
    )(cols, gain, ws, bt, dy, dcols)


def _sw_block(first, q, kp, kc, vp, vc, z, sinks):
    P = BLK
    lane = lax.broadcasted_iota(jnp.int32, (1, 128), 1)
    r = lax.broadcasted_iota(jnp.int32, (128, 128), 0)
    c = lax.broadcasted_iota(jnp.int32, (128, 128), 1)
    swap = (c == (r + 64) % 128).astype(F32)
    k2 = jnp.concatenate([kp, kc], axis=0)
    v2 = jnp.concatenate([vp, vc], axis=0)
    k2s = bdot("nn", k2, swap)
    v2s = bdot("nn", v2, swap)
    qi = lax.broadcasted_iota(jnp.int32, (P, 2 * P), 0)
    kj = lax.broadcasted_iota(jnp.int32, (P, 2 * P), 1)
    dist = qi + P - kj
    valid = (dist >= 0) & (dist < P) & ((kj >= P) | jnp.logical_not(first))
    outs = []
    for j in range(4):
        acc = jnp.zeros((P, 128), F32)
        for half in range(2):
            h = 2 * j + half
            kv = h // 4
            in_half = (lane >= 64 * half) & (lane < 64 * half + 64)
            qh = jnp.where(in_half, q[:, 128 * j:128 * (j + 1)], 0.0)
            same = (half == kv)
            s = bdot("nt", qh, k2 if same else k2s) * (64 ** -0.5)
            s = jnp.where(valid, s, NEG_INF)
            sink = _lane_col(sinks, h)
            m = lax.stop_gradient(jnp.maximum(jnp.max(s, axis=1, keepdims=True), sink))
            e = jnp.exp(s - m)
            p = e / (jnp.sum(e, axis=1, keepdims=True) + jnp.exp(sink - m))
            o = bdot("nn", p, v2 if same else v2s)
            acc = acc + jnp.where(in_half, o, 0.0)
        outs.append(acc)
    return jnp.concatenate(outs, axis=1) * _silu(z)


def _sw_specs(jblk, idx):
    prev = lambda i: jnp.maximum(idx(i) - 1, 0)
    jk = (jblk * SW_W + 1024) // 128
    return [pl.BlockSpec((BLK, SW_W), lambda i: (idx(i), jblk)),
            pl.BlockSpec((BLK, 128), lambda i: (prev(i), jk)),
            pl.BlockSpec((BLK, 128), lambda i: (prev(i), jk + 1)), _full((1, 128))]


def sw_forward(cols, jblk, sinks, name):
    T = cols.shape[0]

    def body(x_ref, kp_ref, vp_ref, s_ref, y_ref):
        y_ref[...] = _sw_block(pl.program_id(0) == 0, x_ref[:, 0:512], kp_ref[...], x_ref[:, 1024:1152],
                               vp_ref[...], x_ref[:, 1152:1280], x_ref[:, 512:1024], s_ref[...])

    return pl.pallas_call(
        body, name=name, out_shape=jax.ShapeDtypeStruct((T, 512), F32), grid=(T // BLK,),
        in_specs=_sw_specs(jblk, lambda i: i),
        out_specs=pl.BlockSpec((BLK, 512), lambda i: (i, 0)),
        compiler_params=pltpu.CompilerParams(dimension_semantics=("parallel",)),
    )(cols, cols, cols, sinks)


def sw_backward(cols, jblk, sinks, dy, dcols, name):
    T = cols.shape[0]
    n = T // BLK
    rev = lambda i: n - 1 - i

    def body(x_ref, kp_ref, vp_ref, s_ref, dy_ref, dcols_in, dx_ref, ds_ref, kcarry, vcarry):
        i = pl.program_id(0)

        @pl.when(i == 0)
        def _():
            kcarry[...] = jnp.zeros_like(kcarry)
            vcarry[...] = jnp.zeros_like(vcarry)
            ds_ref[...] = jnp.zeros_like(ds_ref)

        f = functools.partial(_sw_block, i == n - 1)
        _, vjp = jax.vjp(f, x_ref[:, 0:512], kp_ref[...], x_ref[:, 1024:1152], vp_ref[...], x_ref[:, 1152:1280],
                         x_ref[:, 512:1024], s_ref[...])
        dq, dkp, dkc, dvp, dvc, dz, dsk = vjp(dy_ref[...])
        dx_ref[...] = jnp.concatenate([dq, dz, dkc + kcarry[...], dvc + vcarry[...],
                                       jnp.zeros((BLK, SW_W - 1280), F32)], axis=1).astype(dx_ref.dtype)
        kcarry[...] = dkp
        vcarry[...] = dvp
        ds_ref[...] += dsk

    return pl.pallas_call(
        body, name=name,
        out_shape=[jax.ShapeDtypeStruct(dcols.shape, dcols.dtype),jax.ShapeDtypeStruct((1, 128), F32)],
        grid=(n,),
        in_specs=_sw_specs(jblk, rev) + [pl.BlockSpec((BLK, 512), lambda i: (rev(i), 0)), _ANY],
        out_specs=[pl.BlockSpec((BLK, SW_W), lambda i: (rev(i), jblk)), _full((1, 128))],
        scratch_shapes=[pltpu.VMEM((BLK, 128), F32), pltpu.VMEM((BLK, 128), F32)],
        input_output_aliases={5: 0},
        compiler_params=pltpu.CompilerParams(dimension_semantics=("arbitrary",)),
    )(cols, cols, cols, sinks, dy, dcols)


XM_TQ = 256


def _xm_block(q, z, mkv):
    outs = []
    for h in range(4):
        s = bdot("nt", q[:, 128 * h:128 * (h + 1)], mkv[:, 128 * h:128 * (h + 1)]) * (128 ** -0.5)
        m = lax.stop_gradient(jnp.max(s, axis=1, keepdims=True))
        e = jnp.exp(s - m)
        p = e / jnp.sum(e, axis=1, keepdims=True)
        outs.append(bdot("nn", p, mkv[:, 512 + 128 * h:512 + 128 * (h + 1)]))
    return jnp.concatenate(outs, axis=1) * _silu(z)


def xm_forward(cols, jblk, mkv, name):
    T = cols.shape[0]

    def body(x_ref, m_ref, y_ref):
        y_ref[...] = _xm_block(x_ref[:, 0:512], x_ref[:, 512:1024], m_ref[...])

    return pl.pallas_call(
        body, name=name, out_shape=jax.ShapeDtypeStruct((T, 512), F32), grid=(T // XM_TQ,),
        in_specs=[pl.BlockSpec((XM_TQ, XM_W), lambda i: (i, jblk)), _full(mkv.shape)],
        out_specs=pl.BlockSpec((XM_TQ, 512), lambda i: (i, 0)),
        compiler_params=pltpu.CompilerParams(dimension_semantics=("parallel",)),
    )(cols, mkv)


def xm_backward(cols, jblk, mkv, dy, dcols, name):
    T = cols.shape[0]

    def body(x_ref, m_ref, dy_ref, dcols_in, dx_ref, dm_ref):
        @pl.when(pl.program_id(0) == 0)
        def _():
            dm_ref[...] = jnp.zeros_like(dm_ref)

        _, vjp = jax.vjp(_xm_block, x_ref[:, 0:512], x_ref[:, 512:1024], m_ref[...])
        dq, dz, dm = vjp(dy_ref[...])
        dx_ref[...] = jnp.concatenate([dq, dz], axis=1).astype(dx_ref.dtype)
        dm_ref[...] += dm

    return pl.pallas_call(
        body, name=name,
        out_shape=[jax.ShapeDtypeStruct(dcols.shape, dcols.dtype),jax.ShapeDtypeStruct(mkv.shape, F32)],
        grid=(T // XM_TQ,),
        in_specs=[pl.BlockSpec((XM_TQ, XM_W), lambda i: (i, jblk)), _full(mkv.shape),
                  pl.BlockSpec((XM_TQ, 512), lambda i: (i, 0)), _ANY],
        out_specs=[pl.BlockSpec((XM_TQ, XM_W), lambda i: (i, jblk)), _full(mkv.shape)],
        input_output_aliases={3: 0},
        compiler_params=pltpu.CompilerParams(dimension_semantics=("arbitrary",)),
    )(cols, mkv, dy, dcols)


def _rms(x, gain):
    return x * lax.rsqrt(jnp.mean(x * x, axis=1, keepdims=True) + EPS) * gain


def memkv_forward(mem, gain, w, name):
    def body(m_ref, g_ref, w_ref, o_ref):
        o_ref[...] = bdot("nn", _rms(m_ref[...], g_ref[...]), w_ref[...])

    return pl.pallas_call(body, name=name, out_shape=jax.ShapeDtypeStruct(mem.shape, F32),
                          compiler_params=pltpu.CompilerParams(vmem_limit_bytes=VMEM_LIMIT))(mem, gain, w)


def memkv_backward(mem, gain, w, dkv, name):
    def body(m_ref, g_ref, w_ref, d_ref, dg_ref, dw_ref):
        mem_v = m_ref[...]
        _, vjp = jax.vjp(lambda g, ww: bdot("nn", _rms(mem_v, g), ww), g_ref[...], w_ref[...].astype(F32))
        dg, dw = vjp(d_ref[...])
        dg_ref[...] = dg
        dw_ref[...] = dw

    return pl.pallas_call(body, name=name,
                          out_shape=[jax.ShapeDtypeStruct(gain.shape, F32), jax.ShapeDtypeStruct(w.shape, F32)],
                          compiler_params=pltpu.CompilerParams(vmem_limit_bytes=VMEM_LIMIT))(mem, gain, w, dkv)


MG_TB = 256


def _merge_block(ys, gl, wup, wout, gpost):
    merged = None
    for n in range(4):
        t = _sigmoid(gl[:, 1024 * n:1024 * (n + 1)]) * bdot("nn", ys[n], wup[n])
        merged = t if merged is None else merged + t
    out = bdot("nn", merged, wout)
    return _rms(out, gpost)


def merge_forward(ys, cols, jgate, x, wup, wout, gpost, name):
    T = x.shape[0]
    TB = 256

    def body(ya, yb, yc, ym, gl_ref, x_ref, wup_ref, wout_ref, gp_ref, o_ref):
        upd = _merge_block([ya[...], yb[...], yc[...], ym[...]], gl_ref[...],
                           [wup_ref[n] for n in range(4)], wout_ref[...], gp_ref[...])
        o_ref[...] = x_ref[...] + upd

    yspec = pl.BlockSpec((TB, 512), lambda i: (i, 0))
    return pl.pallas_call(
        body, name=name, out_shape=jax.ShapeDtypeStruct((T, 1024), F32), grid=(T // TB,),
        in_specs=[yspec] * 4 + [pl.BlockSpec((TB, 4096), lambda i: (i, jgate)),
                                pl.BlockSpec((TB, 1024), lambda i: (i, 0)),
                                _full(wup.shape), _full(wout.shape), _full((1, 1024))],
        out_specs=pl.BlockSpec((TB, 1024), lambda i: (i, 0)),
        compiler_params=pltpu.CompilerParams(dimension_semantics=("parallel",), vmem_limit_bytes=VMEM_LIMIT),
    )(*ys, cols, x, wup, wout, gpost)


def _token_product(a, b, name):
    (T, m), n = a.shape, b.shape[1]

    def body(a_ref, b_ref, o_ref):
        o_ref[...] = lax.dot_general(a_ref[...].astype(BF), b_ref[...].astype(BF), _DIMS["tn"], preferred_element_type=F32)

    return pl.pallas_call(body, name=name, out_shape=jax.ShapeDtypeStruct((m, n), F32),
                          compiler_params=pltpu.CompilerParams(vmem_limit_bytes=VMEM_LIMIT))(a, b)


def merge_backward(ys, cols, jgate, wup, wout, gpost, dx, name):
    T = dx.shape[0]
    TB = MG_TB

    def body(ya, yb, yc, ym, gl_ref, wup_ref, wout_ref, gp_ref, dx_ref,
             dgl_ref, dya, dyb, dyc, dym, dpa, dpb, dpc, dpm, merged_ref, dout_ref, dgp_ref):
        @pl.when(pl.program_id(0) == 0)
        def _():
            dgp_ref[...] = jnp.zeros_like(dgp_ref)

        y_refs = (ya, yb, yc, ym)
        gates = [_sigmoid(gl_ref[:, 1024 * n:1024 * (n + 1)]) for n in range(4)]
        projs = [bdot("nn", y_refs[n][...], wup_ref[n]) for n in range(4)]
        merged = gates[0] * projs[0] + gates[1] * projs[1] + gates[2] * projs[2] + gates[3] * projs[3]
        out = bdot("nn", merged, wout_ref[...])
        _, vjp = jax.vjp(_rms, out, gp_ref[...])
        dout, dgp = vjp(dx_ref[...])
        dmerged = bdot("nt", dout, wout_ref[...])
        for n, (dy_ref, dp_ref) in enumerate(zip((dya, dyb, dyc, dym), (dpa, dpb, dpc, dpm))):
            dproj = dmerged * gates[n]
            dgl_ref[:, 1024 * n:1024 * (n + 1)] = (dmerged * projs[n] * gates[n] * (1.0 - gates[n])).astype(dgl_ref.dtype)
            dy_ref[...] = bdot("nt", dproj, wup_ref[n])
            dp_ref[...] = dproj.astype(BF)
        merged_ref[...] = merged.astype(BF)
        dout_ref[...] = dout.astype(BF)
        dgp_ref[...] += dgp

    yspec = pl.BlockSpec((TB, 512), lambda i: (i, 0))
    dspec = pl.BlockSpec((TB, 1024), lambda i: (i, 0))
    dcols, dya, dyb, dyc, dym, *dproj, merged, dout, dgp = pl.pallas_call(
        body, name=name,
        out_shape=[jax.ShapeDtypeStruct(cols.shape, BF)] + [jax.ShapeDtypeStruct((T, 512), F32)] * 4 + [
            jax.ShapeDtypeStruct((T, 1024), BF)] * 6 + [jax.ShapeDtypeStruct((1, 1024), F32)],
        grid=(T // TB,),
        in_specs=[yspec] * 4 + [pl.BlockSpec((TB, 4096), lambda i: (i, jgate)),
                                _full(wup.shape), _full(wout.shape), _full((1, 1024)), dspec],
        out_specs=[pl.BlockSpec((TB, 4096), lambda i: (i, jgate))] + [yspec] * 4 + [
            dspec] * 6 + [_full((1, 1024))],
        compiler_params=pltpu.CompilerParams(dimension_semantics=("arbitrary",), vmem_limit_bytes=VMEM_LIMIT),
    )(*ys, cols, wup, wout, gpost, dx)
    dwup = jnp.stack([_token_product(ys[n], dproj[n], "%s_w_up%d" % (name, n)) for n in range(4)])
    dwout = _token_product(merged, dout, name + "_w_out")
    return dcols, dya, dyb, dyc, dym, dwup, dwout, dgp


NB = 256


def prenorm_forward(x, gain, name):
    T, D = x.shape

    def body(x_ref, g_ref, o_ref, ot_ref):
        h = _rms(x_ref[...], g_ref[...])
        o_ref[...] = h.astype(BF)
        ot_ref[...] = h.T.astype(BF)

    return pl.pallas_call(
        body, name=name,
        out_shape=[jax.ShapeDtypeStruct((T, D), BF), jax.ShapeDtypeStruct((D, T), BF)], grid=(T // NB,),
        in_specs=[pl.BlockSpec((NB, D), lambda i: (i, 0)), _full((1, D))],
        out_specs=[pl.BlockSpec((NB, D), lambda i: (i, 0)), pl.BlockSpec((D, NB), lambda i: (0, i))],
        compiler_params=pltpu.CompilerParams(dimension_semantics=("parallel",)),
    )(x, gain)


def prenorm_backward(x, gain, dh, dres, name):
    T = x.shape[0]

    def body(x_ref, g_ref, dh_ref, dr_ref, dx_ref, dg_ref):
        @pl.when(pl.program_id(0) == 0)
        def _():
            dg_ref[...] = jnp.zeros_like(dg_ref)

        _, vjp = jax.vjp(_rms, x_ref[...], g_ref[...])
        dxn, dg = vjp(dh_ref[...])
        dx_ref[...] = dr_ref[...] + dxn
        dg_ref[...] += dg

    spec = pl.BlockSpec((NB, 1024), lambda i: (i, 0))
    return pl.pallas_call(
        body, name=name,
        out_shape=[jax.ShapeDtypeStruct(x.shape, F32), jax.ShapeDtypeStruct((1, 1024), F32)], grid=(T // NB,),
        in_specs=[spec, _full((1, 1024)), spec, spec], out_specs=[spec, _full((1, 1024))],
        compiler_params=pltpu.CompilerParams(dimension_semantics=("arbitrary",)),
    )(x, gain, dh, dres)


def loss_head(y, target, name):
    T, D = y.shape

    def body(y_ref, t_ref, l_ref, d_ref):
        @pl.when(pl.program_id(0) == 0)
        def _():
            l_ref[...] = jnp.zeros_like(l_ref)

        err = y_ref[...] - t_ref[...]
        d_ref[...] = err * (1.0 / D)
        l_ref[...] += jnp.full(l_ref.shape, 0.5 * jnp.sum(jnp.mean(err * err, axis=1, keepdims=True)), F32)

    spec = pl.BlockSpec((NB, D), lambda i: (i, 0))
    return pl.pallas_call(
        body, name=name,
        out_shape=[jax.ShapeDtypeStruct((1, 128), F32), jax.ShapeDtypeStruct(y.shape, F32)], grid=(T // NB,),
        in_specs=[spec, spec], out_specs=[_full((1, 128)), spec],
        compiler_params=pltpu.CompilerParams(dimension_semantics=("arbitrary",)),
    )(y, target)


JB_GATE, JB_XM, JB_DN, JB_SW, JB_GM = 0, 4, 2, 5, 6
_ALIGNED_PIECES = ((5896, 4096), (4872, 512), (5384, 512), (0, 2048), (2048, 8), 504, (3592, 512), (4360, 512),
                   (4104, 128), (4232, 128), 256, (2056, 1024), (3080, 512))
_NATURAL_FROM_ALIGNED = ((5120, 2048), (7168, 8), (9216, 1024), (10240, 512), (7680, 512), (8704, 128), (8832, 128),
                         (8192, 512), (4096, 512), (4608, 512), (0, 4096))


def _natural_range(slots, start, width):
    out = []
    while width > 0:
        j, i = divmod(start, W_IN_SHARD)
        take = min(width, W_IN_SHARD - i)
        out.append(slots[j, :, i:i + take])
        start, width = start + take, width - take
    return out


def _aligned_w_in(slots):
    parts = []
    for piece in _ALIGNED_PIECES:
        if isinstance(piece, int):
            parts.append(jnp.zeros(slots.shape[1:2] + (piece,), slots.dtype))
        else:
            parts += _natural_range(slots, *piece)
    return jnp.concatenate(parts, axis=-1)


def _slots_of_aligned(d_al):
    slots = []
    for s in range(N_DEV):
        lo, hi = s * W_IN_SHARD, (s + 1) * W_IN_SHARD
        parts, nat = [], 0
        for a_start, width in _NATURAL_FROM_ALIGNED:
            b, e = max(lo, nat), min(hi, nat + width)
            if b < e:
                parts.append(d_al[..., a_start + b - nat:a_start + e - nat])
            nat += width
        parts.append(jnp.zeros(d_al.shape[:1] + (W_IN_SHARD_PAD - W_IN_SHARD,), d_al.dtype))
        slots.append(jnp.concatenate(parts, axis=-1))
    return jnp.stack(slots)


SMALL_VEC_W = 1024


def _pack_small(parts):
    rows = []
    for p in parts:
        flat = p.reshape(-1).astype(F32)
        r = -(-flat.shape[0] // SMALL_VEC_W)
        rows.append(jnp.pad(flat, (0, r * SMALL_VEC_W - flat.shape[0])).reshape(r, SMALL_VEC_W))
    vec = jnp.concatenate(rows, axis=0)
    return jnp.pad(vec, ((0, -vec.shape[0] % 8), (0, 0)))


def _unpack_small(vec, shapes):
    out, off = [], 0
    for s in shapes:
        n = math.prod(s)
        r = -(-n // SMALL_VEC_W)
        out.append(vec[off:off + r].reshape(-1)[:n].reshape(s))
        off += r
    return out


def _lanes(vec, at):
    return jnp.zeros((1, 128), F32).at[0, at:at + vec.shape[0]].set(vec)


SMALL_NAMES = ("norm_pre", "norm_post", "norm_mem", "a_log", "dt_bias", "dn_norm", "gm_norm",
               "spatial_w", "spatial_b", "sinks")


def _other_weights(s_mem, s_up, s_out):
    return (s_mem.reshape(D_MODEL, 2 * BRANCH_W),
            jnp.transpose(s_up, (1, 2, 0, 3)).reshape(N_BRANCH, BRANCH_W, D_MODEL), s_out.reshape(D_MODEL, D_MODEL))


def _grad_slots(d_in_al, d_mem, d_up, d_out):
    return [None if d_in_al is None else _slots_of_aligned(d_in_al), d_mem.astype(BF).reshape(N_DEV, 128, 2 * BRANCH_W),
            jnp.transpose(d_up.astype(BF).reshape(N_BRANCH, BRANCH_W, N_DEV, 128), (2, 0, 1, 3)),
            d_out.astype(BF).reshape(N_DEV, 128, D_MODEL)]


def _layer_params(l, small, conv_full, token):
    return dict(
        gpre=small["norm_pre"][l][None] + token, gpost=small["norm_post"][l][None], gmem=small["norm_mem"][l][None],
        cw=conv_full[l], al=_lanes(small["a_log"][l], 4), dt=_lanes(small["dt_bias"][l], 4),
        dnn=small["dn_norm"][l][None], gain=small["gm_norm"][l][None], ws=small["spatial_w"][l],
        bt=jnp.zeros((128, 128), F32).at[:, :GM_GROUPS].set(small["spatial_b"][l].T),
        sinks=_lanes(small["sinks"][l], 0))


def _layer_forward(l, xl, mem, p, w_in_al, other_weights):
    t = "l%d_" % l
    h, h_t = prenorm_forward(xl, p["gpre"], t + "prenorm")
    cols = _matmul(h, w_in_al, "nn", F32, (1024, 1536, 1024), t + "w_in")
    ya, ss, ts = dn_forward(cols, JB_DN, p["cw"], p["al"], p["dt"], p["dnn"], t + "deltanet")
    yb = gm_forward(cols, JB_GM, p["gain"], p["ws"], p["bt"], t + "gmlp")
    yc = sw_forward(cols, JB_SW, p["sinks"], t + "swa")
    w_mem, w_up, w_out = other_weights(yc)
    mkv = memkv_forward(mem, p["gmem"], w_mem, t + "memkv")
    ym = xm_forward(cols, JB_XM, mkv, t + "memattn")
    xn = merge_forward([ya, yb, yc, ym], cols, JB_GATE, xl, w_up, w_out, p["gpost"], t + "merge")
    return xn, dict(p, x=xl, h_t=h_t, cols=cols, mkv=mkv, ss=ss, ts=ts, ys=[ya, yb, yc, ym]), (w_in_al, w_mem, w_up, w_out)


def _layer_backward(l, s, mem, weights, dx, token, early=None):
    w_in_al, w_mem, w_up, w_out = weights
    t = "l%d_" % l
    cols = s["cols"]
    dcols, dya, dyb, dyc, dym, dwup, dwout, dgpost = merge_backward(
        s["ys"], cols, JB_GATE, w_up, w_out, s["gpost"] + token, dx, t + "merge_bwd")
    dcols, dmkv = xm_backward(cols, JB_XM, s["mkv"], dym, dcols, t + "memattn_bwd")
    dgmem, dwmem = memkv_backward(mem, s["gmem"], w_mem, dmkv, t + "memkv_bwd")
    sinks = s["sinks"] if early is None else s["sinks"] + early(dwmem, dwup, dwout)
    dcols, dsinks = sw_backward(cols, JB_SW, sinks, dyc, dcols, t + "swa_bwd")
    dcols, dgain, dws, dbt = gm_backward(cols, JB_GM, s["gain"], s["ws"], s["bt"], dyb, dcols, t + "gmlp_bwd")
    dcols, dcw, dal, ddt, ddn = dn_backward(
        cols, JB_DN, s["cw"], s["al"], s["dt"], s["dnn"], s["ss"], s["ts"], dya, dcols, t + "deltanet_bwd")
    dh = _matmul(dcols, w_in_al, "nt", F32, (1024, 1024, 1024), t + "w_in_bwd_x")
    dwin = _matmul(s["h_t"], dcols, "nn", BF, (1024, 1536, 2048), t + "w_in_bwd_w")
    dx, dgpre = prenorm_backward(s["x"], s["gpre"], dh, dx, t + "prenorm_bwd")
    gsmall = dict(norm_pre=dgpre[0], norm_post=dgpost[0], norm_mem=dgmem[0], a_log=dal[0, 4:8], dt_bias=ddt[0, 4:8],
                  dn_norm=ddn[0], gm_norm=dgain[0], spatial_w=dws, spatial_b=dbt[:, :GM_GROUPS].T,
                  sinks=dsinks[0, :SW_HEADS], conv_w=dcw)
    return dx, gsmall, (dwin, dwmem, dwup, dwout)


def kernel(x, mem, norm_pre, norm_post, norm_mem, w_in, conv_w, a_log, dt_bias, dn_norm, gm_norm, spatial_w, spatial_b, sinks, w_mem_kv, w_up, w_out, loss_target, m_norm_pre, m_norm_post, m_norm_mem, m_w_in, m_conv_w, m_a_log, m_dt_bias, m_dn_norm, m_gm_norm, m_spatial_w, m_spatial_b, m_sinks, m_w_mem_kv, m_w_up, m_w_out, v_norm_pre, v_norm_post, v_norm_mem, v_w_in, v_conv_w, v_a_log, v_dt_bias, v_dn_norm, v_gm_norm, v_spatial_w, v_spatial_b, v_sinks, v_w_mem_kv, v_w_up, v_w_out):
    xi, yi, ci = _my_place()
    my_slot = 4 * xi + 2 * yi + ci
    conv_shard = conv_w.shape[-1]
    x2, mem2, target = x[0], mem[0], loss_target[0]

    w_in_pad = jnp.pad(w_in.astype(BF), ((0, 0), (0, 0), (0, W_IN_SHARD_PAD - W_IN_SHARD)))
    shards = [[w_in_pad[l], w_mem_kv[l].astype(BF), w_up[l].astype(BF), w_out[l].astype(BF)] for l in range(DEPTH)]
    w_in_slots0, conv_slots = _all_gather_slots([shards[0][0], conv_w], "gather_weights_l0")
    ag = list(_spread_start(shards[0][1:] + shards[1], "gather", "gather_weights_rest_start"))
    conv_full = jnp.transpose(conv_slots, (1, 2, 0, 3)).reshape(DEPTH, CONV_W, N_DEV * conv_shard)
    small = dict(norm_pre=norm_pre, norm_post=norm_post, norm_mem=norm_mem, a_log=a_log,
                 dt_bias=dt_bias, dn_norm=dn_norm, gm_norm=gm_norm, spatial_w=spatial_w,
                 spatial_b=spatial_b, sinks=sinks)

    def arrived(which, after, name):
        ag[2], ag[3] = _spread_wait(ag[0], ag[1], ag[2], ag[3], which, after, name)
        return [ag[3][a] for a in which]

    x1, saved0, weights0 = _layer_forward(
        0, x2, mem2, _layer_params(0, small, conv_full, ag[4][0, 0]), _aligned_w_in(w_in_slots0),
        lambda y: _other_weights(*arrived([0, 1, 2], y, "gather_weights_l0_rest_wait")))
    w_in_slots1, = arrived([3], x1, "gather_weights_l1_w_in_wait")
    x_out, saved1, weights1 = _layer_forward(
        1, x1, mem2, _layer_params(1, small, conv_full, 0.0), _aligned_w_in(w_in_slots1),
        lambda y: _other_weights(*arrived([4, 5, 6], y, "gather_weights_l1_rest_wait")))
    loss, dx = loss_head(x_out, target, "loss_head")

    dx, gsmall1, gbig1 = _layer_backward(1, saved1, mem2, weights1, dx, 0.0)
    rs_send, rs_recv, rs_src, rs_land, rs_token = _spread_start(_grad_slots(*gbig1), "scatter", "exchange_grads_l1_start")
    rest0 = []

    def send_rest0(dwmem, dwup, dwout):
        rest0.extend(_spread_start(_grad_slots(None, dwmem, dwup, dwout)[1:], "scatter", "exchange_grads_l0_rest_start"))
        return rest0[4][0, 0]

    dx, gsmall0, gbig0 = _layer_backward(0, saved0, mem2, weights0, dx, rs_token[0, 0], send_rest0)
    _, parts1 = _spread_wait(rs_send, rs_recv, rs_src, rs_land, range(4), dx, "exchange_grads_l1_wait")

    packed_names = SMALL_NAMES + ("conv_w",)
    gs = {n: jnp.stack([gsmall0[n], gsmall1[n]]) for n in packed_names}
    small_parts = [loss[0, :1]] + [gs[n] for n in packed_names]
    sm = _spread_start([_pack_small(small_parts)], "gather", "gather_small_grads_start")

    g_win0 = _slots_of_aligned(gbig0[0])
    g_win0 = g_win0.reshape((N_DEV // 2, 2) + g_win0.shape[1:])
    theirs, = _exchange_sibling([g_win0], "exchange_sibling_l0")
    chip_sum = _pair_sum(g_win0, theirs, "pair_sum_l0")
    ch_send, ch_recv, ch_src, ch_land, ch_token = _spread_start([chip_sum], "chips", "exchange_chips_l0_start")

    _, (small_land,) = _spread_wait(*sm[:4], [0], ch_token, "gather_small_grads_wait")
    tot = _unpack_small(_sum_slots(small_land, "sum_small_grads"), [p.shape for p in small_parts])
    loss_tot = tot[0][0]
    grads = dict(zip(packed_names, tot[1:]))
    grads["conv_w"] = lax.dynamic_slice_in_dim(grads["conv_w"], my_slot * conv_shard, conv_shard, axis=2)

    given = dict(norm_pre=(norm_pre, m_norm_pre, v_norm_pre), norm_post=(norm_post, m_norm_post, v_norm_post),
                 norm_mem=(norm_mem, m_norm_mem, v_norm_mem), a_log=(a_log, m_a_log, v_a_log),
                 dt_bias=(dt_bias, m_dt_bias, v_dt_bias), dn_norm=(dn_norm, m_dn_norm, v_dn_norm),
                 gm_norm=(gm_norm, m_gm_norm, v_gm_norm), spatial_w=(spatial_w, m_spatial_w, v_spatial_w),
                 spatial_b=(spatial_b, m_spatial_b, v_spatial_b), sinks=(sinks, m_sinks, v_sinks),
                 conv_w=(conv_w, m_conv_w, v_conv_w))
    pshapes = [given[n][0].shape for n in packed_names]
    pw, pm, pv = (_pack_small([given[n][i] for n in packed_names]) for i in range(3))
    pd, pnm, pnv = _adamw(pw + ch_token[0, 0], _pack_small([grads[n] for n in packed_names]), pm, pv, "adamw_small")
    upd = {n: t for n, t in zip(packed_names, zip(_unpack_small(pd, pshapes), _unpack_small(pnm, pshapes),
                                                  _unpack_small(pnv, pshapes)))}
    big = (("w_mem_kv", (w_mem_kv, m_w_mem_kv, v_w_mem_kv)), ("w_up", (w_up, m_w_up, v_w_up)),
           ("w_out", (w_out, m_w_out, v_w_out)))
    first = [_sum_adamw(parts1[1 + i], w, m, v, 1, None, ch_token, "adamw_%s_l1" % name)
             for i, (name, (w, m, v)) in enumerate(big)]
    _, parts0_rest = _spread_wait(*rest0[:4], range(3), first[-1][0], "exchange_grads_l0_rest_wait")
    for i, (name, (w, m, v)) in enumerate(big):
        g, d, nm, nv = _sum_adamw(parts0_rest[i], w, m, v, 0, first[i], None, "adamw_%s_l0" % name)
        grads[name], upd[name] = g, (d, nm, nv)
    _, (parts0_w_in,) = _spread_wait(ch_send, ch_recv, ch_src, ch_land, [0], upd["w_out"][0], "exchange_chips_l0_wait")
    w_in_t, m_w_in_t, v_w_in_t = (jnp.transpose(t, (2, 0, 1)) for t in (w_in, m_w_in, v_w_in))
    g, d, nm, nv = (jnp.transpose(t, (1, 2, 0)) for t in
                    _sum_adamw_t([parts0_w_in, parts1[0]], w_in_t, m_w_in_t, v_w_in_t, "adamw_w_in"))
    grads["w_in"], upd["w_in"] = g, (d, nm, nv)

    order = ("norm_pre", "norm_post", "norm_mem", "w_in", "conv_w", "a_log", "dt_bias", "dn_norm",
             "gm_norm", "spatial_w", "spatial_b", "sinks", "w_mem_kv", "w_up", "w_out")
    return (loss_tot, dx[None], *[grads[n] for n in order], *[upd[n][0] for n in order],
            *[upd[n][1] for n in order], *[upd[n][2] for n in order])
```

```python
import functools
import math

import jax
import jax.numpy as jnp
from jax import lax
from jax.experimental import pallas as pl
from jax.experimental.pallas import tpu as pltpu

MESH = pl.DeviceIdType.MESH
N_DEV = 8

D_MODEL = 1024
DEPTH = 2
N_BRANCH = 4
BRANCH_W = 512
DN_HEADS = 4
CONV_W = 4
GM_GROUPS = 4
SW_HEADS = 8
EPS = 1e-6
NEG_INF = -1e30

D_IN = 9992
W_IN_SHARD = D_IN // N_DEV
W_IN_SHARD_PAD = 1280
D_IN_AL = 10752
DN_W, SW_W, GM_W, XM_W = 2560, 1536, 1536, 1024

ADAM_LR = 0.001
ADAM_B1 = 0.9
ADAM_B2 = 0.999
ADAM_EPS = 1e-08
ADAM_WD = 0.01
ADAM_STEP = 10

VMEM_LIMIT = 56 * 1024 * 1024

BF = jnp.bfloat16
F32 = jnp.float32
DN_C = 128
DN_D = 128
HALO = 8
BLK = 128


def _my_place():
    return lax.axis_index("x"), lax.axis_index("y"), lax.axis_index("c")


_ANY = pl.BlockSpec(memory_space=pl.ANY)


def _all_gather_slots(parts, name):
    n = len(parts)

    def body(*refs):
        p_refs, out_refs = refs[:n], refs[n:2 * n]
        send_sems, recv_sems, local_sems = refs[2 * n:]
        x, y, c = _my_place()
        me, sibling = (x, y, c), (x, y, 1 - c)
        chips = [(1 - x, y), (x, 1 - y), (1 - x, 1 - y)]

        def copy(a, k, block, to, src=None):
            px, py, pc = block
            slot = out_refs[a].at[4 * px + 2 * py + pc]
            return pltpu.make_async_remote_copy(
                src_ref=slot if src is None else src, dst_ref=slot,
                send_sem=send_sems.at[7 * a + k], recv_sem=recv_sems.at[7 * a + k],
                device_id=to, device_id_type=MESH)

        mine = [pltpu.make_async_copy(p_refs[a], out_refs[a].at[4 * x + 2 * y + c], local_sems.at[a])
                for a in range(n)]
        for cp in mine:
            cp.start()
        first = []
        for a in range(n):
            first.append(copy(a, 0, me, sibling, src=p_refs[a]))
            first += [copy(a, 1 + j, me, (*chip, c), src=p_refs[a]) for j, chip in enumerate(chips)]
        for cp in first:
            cp.start()
        passed = []
        for j, chip in enumerate(chips):
            for a in range(n):
                copy(a, 1 + j, (*chip, c), me).wait_recv()
                fwd = copy(a, 4 + j, (*chip, c), sibling)
                fwd.start()
                passed.append(fwd)
        for a in range(n):
            copy(a, 0, sibling, me).wait_recv()
            for j, chip in enumerate(chips):
                copy(a, 4 + j, (*chip, 1 - c), me).wait_recv()
        for cp in first + passed:
            cp.wait_send()
        for cp in mine:
            cp.wait()

    return pl.pallas_call(
        body, name=name,
        out_shape=[jax.ShapeDtypeStruct((N_DEV,) + p.shape, p.dtype) for p in parts],
        in_specs=[_ANY] * n, out_specs=[_ANY] * n,
        scratch_shapes=[pltpu.SemaphoreType.DMA((7 * n,)), pltpu.SemaphoreType.DMA((7 * n,)),
                        pltpu.SemaphoreType.DMA((n,))],
    )(*parts)


def _exchange_sibling(parts, name):
    n = len(parts)

    def body(*refs):
        g_refs, out_refs = refs[:n], refs[n:2 * n]
        send_sems, recv_sems = refs[2 * n:]
        x, y, c = _my_place()
        copies = [pltpu.make_async_remote_copy(
            src_ref=g_refs[a].at[:, 1 - c], dst_ref=out_refs[a],
            send_sem=send_sems.at[a], recv_sem=recv_sems.at[a],
            device_id=(x, y, 1 - c), device_id_type=MESH) for a in range(n)]
        for cp in copies:
            cp.start()
        for cp in copies:
            cp.wait()

    return pl.pallas_call(
        body, name=name,
        out_shape=[jax.ShapeDtypeStruct((4,) + g.shape[2:], g.dtype) for g in parts],
        in_specs=[_ANY] * n, out_specs=[_ANY] * n,
        scratch_shapes=[pltpu.SemaphoreType.DMA((n,)), pltpu.SemaphoreType.DMA((n,))],
    )(*parts)


_HBM = pl.BlockSpec(memory_space=pltpu.HBM)
_SEM = pl.BlockSpec(memory_space=pltpu.SEMAPHORE)
_EFFECT = pltpu.SideEffectType.DATAFLOW_SIDE_EFFECTING


def _peer(x, y, c, k):
    return (1 - x if (k >> 2) & 1 else x, 1 - y if (k >> 1) & 1 else y, 1 - c if k & 1 else c)


def _spread_start(srcs, mode, name):
    n = len(srcs)
    lands = [lax.empty((N_DEV,) + s.shape if mode == "gather" else s.shape, s.dtype) for s in srcs]
    peers = range(0, N_DEV, 2) if mode == "chips" else range(N_DEV)

    def body(*refs):
        src_refs, land_refs = refs[:n], refs[n:2 * n]
        send_sems, recv_sems = refs[2 * n:2 * n + 2]
        token = refs[-1]
        x, y, c = _my_place()
        for a in range(n):
            for k in peers:
                px, py, pc = _peer(x, y, c, k)
                if mode == "chips":
                    src, mine = src_refs[a].at[2 * px + py], 2 * x + y
                else:
                    src = src_refs[a].at[4 * px + 2 * py + pc] if mode == "scatter" else src_refs[a]
                    mine = 4 * x + 2 * y + c
                pltpu.make_async_remote_copy(
                    src_ref=src, dst_ref=land_refs[a].at[mine],
                    send_sem=send_sems.at[a], recv_sem=recv_sems.at[a],
                    device_id=(px, py, pc), device_id_type=MESH).start()
        token[...] = jnp.zeros_like(token)

    out = pl.pallas_call(
        body, name=name,
        out_shape=[pltpu.SemaphoreType.DMA((n,)), pltpu.SemaphoreType.DMA((n,))]
        + [pltpu.HBM(s.shape, s.dtype) for s in srcs] + [pltpu.HBM(l.shape, l.dtype) for l in lands]
        + [jax.ShapeDtypeStruct((8, 128), F32)],
        in_specs=[_HBM] * (2 * n),
        out_specs=[_SEM, _SEM] + [_HBM] * (2 * n) + [pl.BlockSpec(memory_space=pltpu.VMEM)],
        input_output_aliases={i: 2 + i for i in range(2 * n)},
        compiler_params=pltpu.CompilerParams(has_side_effects=_EFFECT),
    )(*[pltpu.with_memory_space_constraint(s, pltpu.HBM) for s in srcs],
      *[pltpu.with_memory_space_constraint(l, pltpu.HBM) for l in lands])
    return out[0], out[1], out[2:2 + n], out[2 + n:2 + 2 * n], out[-1]


def _spread_wait(send_sems, recv_sems, srcs, lands, which, after, name):
    n = len(srcs)

    def body(*refs):
        land_refs = refs[n:2 * n]
        send_sems, recv_sems = refs[2 * n:2 * n + 2]
        x, y, c = _my_place()
        for a in which:
            whole = pltpu.make_async_remote_copy(
                src_ref=land_refs[a], dst_ref=land_refs[a],
                send_sem=send_sems.at[a], recv_sem=recv_sems.at[a],
                device_id=(x, y, c), device_id_type=MESH)
            whole.wait_send()
            whole.wait_recv()

    out = pl.pallas_call(
        body, name=name,
        out_shape=[pltpu.HBM(s.shape, s.dtype) for s in srcs] + [pltpu.HBM(l.shape, l.dtype) for l in lands],
        in_specs=[_HBM] * (2 * n) + [_SEM, _SEM, _ANY],
        out_specs=[_HBM] * (2 * n),
        input_output_aliases={i: i for i in range(2 * n)},
        compiler_params=pltpu.CompilerParams(has_side_effects=_EFFECT),
    )(*srcs, *lands, send_sems, recv_sems, after)
    return out[:n], out[n:]


def _sum_slots(parts, name):
    def body(p_ref, o_ref):
        acc = p_ref[0]
        for s in range(1, parts.shape[0]):
            acc = acc + p_ref[s]
        o_ref[...] = acc

    return pl.pallas_call(body, name=name, out_shape=jax.ShapeDtypeStruct(parts.shape[1:], parts.dtype))(parts)


def _pick(n, pref):
    if n <= pref:
        return n
    t = pref - pref % 128
    while t > 0 and n % t:
        t -= 128
    return t if t > 0 else n


_DIMS = {"nn": (((1,), (0,)), ((), ())),
         "nt": (((1,), (1,)), ((), ())),
         "tn": (((0,), (0,)), ((), ()))}


def _matmul(a, b, mode, out_dtype, tiles, name):
    (m, k) = a.shape
    n = b.shape[1] if mode == "nn" else b.shape[0]
    tm, tn, tk = (_pick(d, t) for d, t in zip((m, n, k), tiles))
    nk = k // tk

    def product(a_ref, b_ref):
        return lax.dot_general(a_ref[...].astype(BF), b_ref[...].astype(BF), _DIMS[mode], preferred_element_type=F32)

    def body_whole_k(a_ref, b_ref, o_ref):
        o_ref[...] = product(a_ref, b_ref).astype(o_ref.dtype)

    def body_split_k(a_ref, b_ref, o_ref, acc_ref):
        kk = pl.program_id(2)

        @pl.when(kk == 0)
        def _():
            acc_ref[...] = jnp.zeros_like(acc_ref)

        acc_ref[...] += product(a_ref, b_ref)

        @pl.when(kk == nk - 1)
        def _():
            o_ref[...] = acc_ref[...].astype(o_ref.dtype)

    b_spec = (pl.BlockSpec((tn, tk), lambda i, j, kk: (j, kk)) if mode == "nt"
              else pl.BlockSpec((tk, tn), lambda i, j, kk: (kk, j)))
    return pl.pallas_call(
        body_whole_k if nk == 1 else body_split_k, name=name,
        out_shape=jax.ShapeDtypeStruct((m, n), out_dtype),
        grid=(m // tm, n // tn, nk),
        in_specs=[pl.BlockSpec((tm, tk), lambda i, j, kk: (i, kk)), b_spec],
        out_specs=pl.BlockSpec((tm, tn), lambda i, j, kk: (i, j)),
        scratch_shapes=[] if nk == 1 else [pltpu.VMEM((tm, tn), F32)],
        compiler_params=pltpu.CompilerParams(
            dimension_semantics=("parallel", "parallel", "arbitrary"),
            vmem_limit_bytes=VMEM_LIMIT),
    )(a, b)


def _rows2d(t, lead):
    return t.reshape(t.shape[:lead] + (math.prod(t.shape[lead:-1]), t.shape[-1]))


def _pair_sum(g, theirs, name):
    g3, t3 = _rows2d(g, 2), _rows2d(theirs, 1)
    _, r, w = t3.shape
    tr = _pick(r, 512)

    def body(g_ref, t_ref, o_ref):
        c = lax.axis_index("c")
        mine = jnp.where(c == 0, g_ref[0, 0], g_ref[0, 1])
        o_ref[0] = (mine.astype(F32) + t_ref[0].astype(F32)).astype(o_ref.dtype)

    out = pl.pallas_call(
        body, name=name,
        out_shape=jax.ShapeDtypeStruct(t3.shape, t3.dtype),
        grid=(4, r // tr),
        in_specs=[pl.BlockSpec((1, 2, tr, w), lambda q, i: (q, 0, i, 0)),
                  pl.BlockSpec((1, tr, w), lambda q, i: (q, i, 0))],
        out_specs=pl.BlockSpec((1, tr, w), lambda q, i: (q, i, 0)),
        compiler_params=pltpu.CompilerParams(dimension_semantics=("parallel", "parallel")),
    )(g3, t3)
    return out.reshape(theirs.shape)


def _adam_update(w, g, m, v):
    c1 = 1.0 - ADAM_B1 ** ADAM_STEP
    c2 = 1.0 - ADAM_B2 ** ADAM_STEP
    nm = ADAM_B1 * m + (1.0 - ADAM_B1) * g
    nv = ADAM_B2 * v + (1.0 - ADAM_B2) * (g * g)
    delta = -ADAM_LR * ((nm / c1) / (jnp.sqrt(nv / c2) + ADAM_EPS) + ADAM_WD * w)
    return delta, nm, nv


def _sum_adamw(parts, w, m, v, layer, carry, after, name):
    shape = w.shape
    cols = shape[-1]
    p3 = _rows2d(parts, 1)
    w3, m3, v3 = (_rows2d(t, 1) for t in (w, m, v))
    rows = w3.shape[1]
    tr = _pick(rows, 128)
    n_parts = p3.shape[0]

    def body(p_ref, w_ref, m_ref, v_ref, *rest):
        g_ref, d_ref, nm_ref, nv_ref = rest[-4:]
        g = p_ref[0, :, :cols].astype(F32)
        for q in range(1, n_parts):
            g = g + p_ref[q, :, :cols].astype(F32)
        d, nm, nv = _adam_update(w_ref[0], g, m_ref[0], v_ref[0])
        g_ref[0] = g
        d_ref[0] = d
        nm_ref[0] = nm
        nv_ref[0] = nv

    spec = pl.BlockSpec((1, tr, cols), lambda i: (layer, i, 0))
    extra = [] if carry is None else [_rows2d(t, 1) for t in carry]
    tail = [] if after is None else [after]
    out = pl.pallas_call(
        body, name=name,
        out_shape=[jax.ShapeDtypeStruct(w3.shape, F32)] * 4,
        grid=(rows // tr,),
        in_specs=[pl.BlockSpec((n_parts, tr, p3.shape[-1]), lambda i: (0, i, 0)), spec, spec, spec] + [_ANY] * len(extra + tail),
        out_specs=[spec] * 4,
        input_output_aliases={4 + i: i for i in range(len(extra))},
        compiler_params=pltpu.CompilerParams(dimension_semantics=("parallel",)),
    )(p3, w3, m3, v3, *extra, *tail)
    return tuple(t.reshape(shape) for t in out)


def _sum_adamw_t(parts, w, m, v, name):
    rows = parts[0].shape[2]
    tr = 128
    assert rows % tr == 0 and rows >= w.shape[0]

    def body(*refs):
        p_refs, (w_ref, m_ref, v_ref), (g_ref, d_ref, nm_ref, nv_ref) = refs[:DEPTH], refs[DEPTH:DEPTH + 3], refs[DEPTH + 3:]
        for l in range(DEPTH):
            g = p_refs[l][0].astype(F32)
            for q in range(1, p_refs[l].shape[0]):
                g = g + p_refs[l][q].astype(F32)
            g = g.T
            d, nm, nv = _adam_update(w_ref[:, l, :], g, m_ref[:, l, :], v_ref[:, l, :])
            g_ref[:, l, :] = g
            d_ref[:, l, :] = d
            nm_ref[:, l, :] = nm
            nv_ref[:, l, :] = nv

    spec = pl.BlockSpec((tr,) + w.shape[1:], lambda i: (i, 0, 0))
    return pl.pallas_call(
        body, name=name,
        out_shape=[jax.ShapeDtypeStruct(w.shape, F32)] * 4,
        grid=(rows // tr,),
        in_specs=[pl.BlockSpec((p.shape[0], p.shape[1], tr), lambda i: (0, 0, i)) for p in parts] + [spec] * 3,
        out_specs=[spec] * 4,
        compiler_params=pltpu.CompilerParams(dimension_semantics=("parallel",)),
    )(*parts, w, m, v)


def _adamw(w, g, m, v, name):
    rows, cols = w.shape
    tr = _pick(rows, 128)

    def body(w_ref, g_ref, m_ref, v_ref, d_ref, nm_ref, nv_ref):
        d, nm, nv = _adam_update(w_ref[...], g_ref[...], m_ref[...], v_ref[...])
        d_ref[...] = d
        nm_ref[...] = nm
        nv_ref[...] = nv

    spec = pl.BlockSpec((tr, cols), lambda i: (i, 0))
    return pl.pallas_call(
        body, name=name,
        out_shape=[jax.ShapeDtypeStruct((rows, cols), F32)] * 3,
        grid=(rows // tr,),
        in_specs=[spec] * 4, out_specs=[spec] * 3,
        compiler_params=pltpu.CompilerParams(dimension_semantics=("parallel",)),
    )(w, g, m, v)


_VJP = {"nn": (("nt", "gb"), ("tn", "ag")),
        "nt": (("nn", "gb"), ("tn", "ga")),
        "tn": (("nt", "bg"), ("nn", "ag"))}


def _make_dot(cast, precision):
    def raw(mode, a, b):
        return lax.dot_general(cast(a), cast(b), _DIMS[mode], precision=precision,
                               preferred_element_type=F32)

    @functools.partial(jax.custom_vjp, nondiff_argnums=(0,))
    def dot(mode, a, b):
        return raw(mode, a, b)

    def fwd(mode, a, b):
        return raw(mode, a, b), (a, b)

    def bwd(mode, res, g):
        a, b = res
        pick = {"a": a, "b": b, "g": g}
        (ma, ta), (mb, tb) = _VJP[mode]
        return dot(ma, pick[ta[0]], pick[ta[1]]), dot(mb, pick[tb[0]], pick[tb[1]])

    dot.defvjp(fwd, bwd)
    return dot


bdot = _make_dot(lambda t: t.astype(BF), None)
hdot = _make_dot(lambda t: t, lax.Precision.HIGHEST)


def _xdot(mode, a, b):
    return lax.dot_general(a, b, _DIMS[mode], precision=lax.Precision.HIGH, preferred_element_type=F32)


def _unit_lower_inverse(Ls):
    n = Ls[0].shape[0]
    batched = (((2,), (1,)), ((0,), (0,)))
    mm = lambda a, b: lax.dot_general(a, b, batched, precision=lax.Precision.HIGH, preferred_element_type=F32)
    eye = (lax.broadcasted_iota(jnp.int32, (n, n), 0) == lax.broadcasted_iota(jnp.int32, (n, n), 1)).astype(F32)
    p = jnp.stack(Ls)
    t_inv = eye[None] - p
    for _ in range(6):
        p = mm(p, p)
        t_inv = t_inv + mm(t_inv, p)
    return [t_inv[h] for h in range(len(Ls))]


@jax.custom_vjp
def _tri_solve(L, rhs, t_inv):
    return _xdot("nn", t_inv, rhs)


def _tri_solve_fwd(L, rhs, t_inv):
    sol = _xdot("nn", t_inv, rhs)
    return sol, (t_inv, sol)


def _tri_solve_bwd(res, dsol):
    t_inv, sol = res
    drhs = _xdot("tn", t_inv, dsol)
    return -_xdot("nt", drhs, sol), drhs, jnp.zeros_like(t_inv)


_tri_solve.defvjp(_tri_solve_fwd, _tri_solve_bwd)


def _sigmoid(x):
    return 1.0 / (1.0 + jnp.exp(-x))


def _softplus(x):
    return jnp.maximum(x, 0.0) + jnp.log(1.0 + jnp.exp(-jnp.abs(x)))


def _dn_chunk(S, xs, ba, z, cw, al, dt, dn, t_saved=None):
    C = DN_C
    pre = xs[0] * cw[0] + xs[1] * cw[1] + xs[2] * cw[2] + xs[3] * cw[3]
    qkv = pre * _sigmoid(pre)
    lane = lax.broadcasted_iota(jnp.int32, (1, 128), 1)
    sub = lax.broadcasted_iota(jnp.int32, (C, 1), 0)
    row_i = lax.broadcasted_iota(jnp.int32, (C, C), 0)
    col_i = lax.broadcasted_iota(jnp.int32, (C, C), 1)
    strict = row_i > col_i
    incl = row_i >= col_i
    g_all = jnp.where((lane >= 4) & (lane < 8), -jnp.exp(al) * _softplus(ba + dt), 0.0)
    gc_all = hdot("nn", incl.astype(F32), g_all)
    gc_all_t = gc_all.T
    beta_all = _sigmoid(ba)
    glast_all = jnp.sum(jnp.where(sub == C - 1, gc_all, 0.0), axis=0, keepdims=True)
    heads = []
    for h in range(DN_HEADS):
        q = qkv[:, 128 * h:128 * (h + 1)]
        k = qkv[:, 512 + 128 * h:512 + 128 * (h + 1)]
        v = qkv[:, 1024 + 128 * h:1024 + 128 * (h + 1)]
        q = q * lax.rsqrt(jnp.sum(q * q, axis=1, keepdims=True) + EPS) * (DN_D ** -0.5)
        k = k * lax.rsqrt(jnp.sum(k * k, axis=1, keepdims=True) + EPS)
        beta = jnp.sum(jnp.where(lane == h, beta_all, 0.0), axis=1, keepdims=True)
        gc = jnp.sum(jnp.where(lane == 4 + h, gc_all, 0.0), axis=1, keepdims=True)
        gc_row = jnp.sum(jnp.where(sub == 4 + h, gc_all_t, 0.0), axis=0, keepdims=True)
        g_last = jnp.sum(jnp.where(lane == 4 + h, glast_all, 0.0), axis=1, keepdims=True)
        diff = gc - gc_row
        kb = k * beta
        L = jnp.where(strict, bdot("nt", kb, k) * jnp.exp(jnp.where(strict, diff, 0.0)), 0.0)
        heads.append((q, k, v, beta, gc, g_last, diff, kb, L))
    t_invs = _unit_lower_inverse([hd[-1] for hd in heads]) if t_saved is None else t_saved
    ys, s_new = [], []
    for h, (q, k, v, beta, gc, g_last, diff, kb, L) in enumerate(heads):
        sol = _tri_solve(L, jnp.concatenate([v * beta, kb * jnp.exp(gc)], axis=1), t_invs[h])
        u, w = sol[:, :DN_D], sol[:, DN_D:]
        a_qk = jnp.where(incl, bdot("nt", q, k) * jnp.exp(jnp.where(incl, diff, 0.0)), 0.0)
        qg = q * jnp.exp(gc)
        kd = k * jnp.exp(g_last - gc)
        v_new = u - bdot("nn", w, S[h])
        o = bdot("nn", qg, S[h]) + bdot("nn", a_qk, v_new)
        s_new.append(S[h] * jnp.exp(g_last) + bdot("tn", kd, v_new))
        o = o * lax.rsqrt(jnp.mean(o * o, axis=1, keepdims=True) + EPS) * dn
        zh = z[:, 128 * h:128 * (h + 1)]
        ys.append(o * (zh * _sigmoid(zh)))
    return jnp.concatenate(ys, axis=1), tuple(s_new), tuple(t_invs)


def _load_shifted(xbuf, x_ref, halo_ref, first):
    xbuf[0:HALO, :] = jnp.where(first, 0.0, halo_ref[:, 0:1536])
    xbuf[HALO:HALO + DN_C, :] = x_ref[:, 0:1536]
    return [xbuf[HALO - 3 + k:HALO - 3 + k + DN_C, :] for k in range(4)]


def dn_forward(cols, jblk, cw, al, dt, dn, name):
    T = cols.shape[0]
    n = T // DN_C

    def body(x_ref, halo_ref, cw_ref, al_ref, dt_ref, dn_ref, y_ref, ss_ref, ts_ref, s_scr, xbuf):
        i = pl.program_id(0)

        @pl.when(i == 0)
        def _():
            s_scr[...] = jnp.zeros_like(s_scr)

        xs = _load_shifted(xbuf, x_ref, halo_ref, i == 0)
        ss_ref[0] = s_scr[...]
        S = [s_scr[h] for h in range(DN_HEADS)]
        cws = [cw_ref[k:k + 1, :] for k in range(4)]
        y, s_new, t_invs = _dn_chunk(S, xs, x_ref[:, 2048:2176], x_ref[:, 1536:2048], cws,
                                     al_ref[...], dt_ref[...], dn_ref[...])
        y_ref[...] = y
        for h in range(DN_HEADS):
            s_scr[h] = s_new[h]
            ts_ref[0, h] = t_invs[h]

    per = DN_C // HALO
    full = lambda shape: pl.BlockSpec(shape, lambda i: (0,) * len(shape))
    return pl.pallas_call(
        body, name=name,
        out_shape=[jax.ShapeDtypeStruct((T, 512), F32),
                   jax.ShapeDtypeStruct((n, DN_HEADS, DN_D, DN_D), F32),
                   jax.ShapeDtypeStruct((n, DN_HEADS, DN_D, DN_D), F32)],
        grid=(n,),
        in_specs=[pl.BlockSpec((DN_C, DN_W), lambda i: (i, jblk)),
                  pl.BlockSpec((HALO, DN_W), lambda i: (jnp.maximum(i * per - 1, 0), jblk)),
                  full((4, 1536)), full((1, 128)), full((1, 128)), full((1, 128))],
        out_specs=[pl.BlockSpec((DN_C, 512), lambda i: (i, 0)),
                   pl.BlockSpec((1, DN_HEADS, DN_D, DN_D), lambda i: (i, 0, 0, 0)),
                   pl.BlockSpec((1, DN_HEADS, DN_D, DN_D), lambda i: (i, 0, 0, 0))],
        scratch_shapes=[pltpu.VMEM((DN_HEADS, DN_D, DN_D), F32), pltpu.VMEM((HALO + DN_C, 1536), F32)],
        compiler_params=pltpu.CompilerParams(dimension_semantics=("arbitrary",)),
    )(cols, cols, cw, al, dt, dn)


def dn_backward(cols, jblk, cw, al, dt, dn, ss, ts, dy, dcols, name):
    T = cols.shape[0]
    n = T // DN_C

    def body(x_ref, halo_ref, cw_ref, al_ref, dt_ref, dn_ref, ss_ref, ts_ref, dy_ref, dcols_in,
             dx_ref, dcw_ref, dal_ref, ddt_ref, ddn_ref, ds_scr, xbuf, dbuf, carry):
        i = pl.program_id(0)

        @pl.when(i == 0)
        def _():
            ds_scr[...] = jnp.zeros_like(ds_scr)
            carry[...] = jnp.zeros_like(carry)
            dcw_ref[...] = jnp.zeros_like(dcw_ref)
            dal_ref[...] = jnp.zeros_like(dal_ref)
            ddt_ref[...] = jnp.zeros_like(ddt_ref)
            ddn_ref[...] = jnp.zeros_like(ddn_ref)

        xs = _load_shifted(xbuf, x_ref, halo_ref, i == n - 1)
        S = [ss_ref[0, h] for h in range(DN_HEADS)]
        cws = [cw_ref[k:k + 1, :] for k in range(4)]

        t_saved = [ts_ref[0, h] for h in range(DN_HEADS)]

        def f(S, xs, ba, z, cws, al, dt, dn):
            return _dn_chunk(S, xs, ba, z, cws, al, dt, dn, t_saved)[:2]

        _, vjp = jax.vjp(f, S, xs, x_ref[:, 2048:2176], x_ref[:, 1536:2048], cws, al_ref[...], dt_ref[...], dn_ref[...])
        dS, dxs, dba, dz, dcws, dal, ddt, ddn = vjp((dy_ref[...], tuple(ds_scr[h] for h in range(DN_HEADS))))
        for h in range(DN_HEADS):
            ds_scr[h] = dS[h]
        dbuf[...] = jnp.zeros_like(dbuf)
        for k in range(4):
            lo = HALO - 3 + k
            dbuf[lo:lo + DN_C, :] += dxs[k]
        dbuf[DN_C:DN_C + HALO, :] += carry[...]
        dx_ref[...] = jnp.concatenate([dbuf[HALO:HALO + DN_C, :], dz, dba,
                                       jnp.zeros((DN_C, DN_W - 2176), F32)], axis=1).astype(dx_ref.dtype)
        carry[...] = dbuf[0:HALO, :]
        for k in range(4):
            dcw_ref[k:k + 1, :] += dcws[k]
        dal_ref[...] += dal
        ddt_ref[...] += ddt
        ddn_ref[...] += ddn

    per = DN_C // HALO
    rev = lambda i: n - 1 - i
    full = lambda shape: pl.BlockSpec(shape, lambda i: (0,) * len(shape))
    return pl.pallas_call(
        body, name=name,
        out_shape=[jax.ShapeDtypeStruct(dcols.shape, dcols.dtype),jax.ShapeDtypeStruct((4, 1536), F32),
                   jax.ShapeDtypeStruct((1, 128), F32), jax.ShapeDtypeStruct((1, 128), F32),
                   jax.ShapeDtypeStruct((1, 128), F32)],
        grid=(n,),
        in_specs=[pl.BlockSpec((DN_C, DN_W), lambda i: (rev(i), jblk)),
                  pl.BlockSpec((HALO, DN_W), lambda i: (jnp.maximum(rev(i) * per - 1, 0), jblk)),
                  full((4, 1536)), full((1, 128)), full((1, 128)), full((1, 128)),
                  pl.BlockSpec((1, DN_HEADS, DN_D, DN_D), lambda i: (rev(i), 0, 0, 0)),
                  pl.BlockSpec((1, DN_HEADS, DN_D, DN_D), lambda i: (rev(i), 0, 0, 0)),
                  pl.BlockSpec((DN_C, 512), lambda i: (rev(i), 0)), _ANY],
        out_specs=[pl.BlockSpec((DN_C, DN_W), lambda i: (rev(i), jblk)),
                   full((4, 1536)), full((1, 128)), full((1, 128)), full((1, 128))],
        scratch_shapes=[pltpu.VMEM((DN_HEADS, DN_D, DN_D), F32), pltpu.VMEM((HALO + DN_C, 1536), F32),
                        pltpu.VMEM((HALO + DN_C, 1536), F32), pltpu.VMEM((HALO, 1536), F32)],
        input_output_aliases={9: 0},
        compiler_params=pltpu.CompilerParams(dimension_semantics=("arbitrary",)),
    )(cols, cols, cw, al, dt, dn, ss, ts, dy, dcols)


def _full(shape):
    return pl.BlockSpec(shape, lambda i: (0,) * len(shape))


def _silu(x):
    return x * _sigmoid(x)


def _gelu(x):
    return 0.5 * x * (1.0 + jnp.tanh(0.7978845608028654 * (x + 0.044715 * (x * x * x))))


def _lane_col(mat, idx):
    lane = lax.broadcasted_iota(jnp.int32, (1, mat.shape[1]), 1)
    return jnp.sum(jnp.where(lane == idx, mat, 0.0), axis=1, keepdims=True)


def _gm_chunk(uv, z, gain, ws, bt):
    g = _gelu(uv)
    u, v = g[:, :512], g[:, 512:]
    v = v * lax.rsqrt(jnp.mean(v * v, axis=1, keepdims=True) + EPS) * gain
    row_i = lax.broadcasted_iota(jnp.int32, (BLK, BLK), 0)
    col_i = lax.broadcasted_iota(jnp.int32, (BLK, BLK), 1)
    causal = row_i >= col_i
    ss = []
    for grp in range(4):
        wg = jnp.where(causal, ws[grp], 0.0)
        ss.append(bdot("nn", wg, v[:, BLK * grp:BLK * (grp + 1)]) + _lane_col(bt, grp))
    return u * jnp.concatenate(ss, axis=1) * _silu(z)


def gm_forward(cols, jblk, gain, ws, bt, name):
    T = cols.shape[0]

    def body(x_ref, gain_ref, ws_ref, bt_ref, y_ref):
        y_ref[...] = _gm_chunk(x_ref[:, 0:1024], x_ref[:, 1024:1536], gain_ref[...],
                               [ws_ref[g] for g in range(4)], bt_ref[...])

    return pl.pallas_call(
        body, name=name, out_shape=jax.ShapeDtypeStruct((T, 512), F32), grid=(T // BLK,),
        in_specs=[pl.BlockSpec((BLK, GM_W), lambda i: (i, jblk)),
                  _full((1, 512)), _full((4, BLK, BLK)), _full((BLK, BLK))],
        out_specs=pl.BlockSpec((BLK, 512), lambda i: (i, 0)),
        compiler_params=pltpu.CompilerParams(dimension_semantics=("parallel",)),
    )(cols, gain, ws, bt)


def gm_backward(cols, jblk, gain, ws, bt, dy, dcols, name):
    T = cols.shape[0]

    def body(x_ref, gain_ref, ws_ref, bt_ref, dy_ref, dcols_in, dx_ref, dgain_ref, dws_ref, dbt_ref):
        @pl.when(pl.program_id(0) == 0)
        def _():
            dgain_ref[...] = jnp.zeros_like(dgain_ref)
            dws_ref[...] = jnp.zeros_like(dws_ref)
            dbt_ref[...] = jnp.zeros_like(dbt_ref)

        _, vjp = jax.vjp(_gm_chunk, x_ref[:, 0:1024], x_ref[:, 1024:1536], gain_ref[...],
                         [ws_ref[g] for g in range(4)], bt_ref[...])
        duv, dz, dgain, dws, dbt = vjp(dy_ref[...])
        dx_ref[...] = jnp.concatenate([duv, dz], axis=1).astype(dx_ref.dtype)
        dgain_ref[...] += dgain
        for g in range(4):
            dws_ref[g] += dws[g]
        dbt_ref[...] += dbt

    return pl.pallas_call(
        body, name=name,
        out_shape=[jax.ShapeDtypeStruct(dcols.shape, dcols.dtype),jax.ShapeDtypeStruct((1, 512), F32),
                   jax.ShapeDtypeStruct((4, BLK, BLK), F32), jax.ShapeDtypeStruct((BLK, BLK), F32)],
        grid=(T // BLK,),
        in_specs=[pl.BlockSpec((BLK, GM_W), lambda i: (i, jblk)),
                  _full((1, 512)), _full((4, BLK, BLK)), _full((BLK, BLK)),
                  pl.BlockSpec((BLK, 512), lambda i: (i, 0)), _ANY],
        out_specs=[pl.BlockSpec((BLK, GM_W), lambda i: (i, jblk)),
                   _full((1, 512)), _full((4, BLK, BLK)), _full((BLK, BLK))],
        input_output_aliases={5: 0},
        compiler_params=pltpu.CompilerParams(dimension_semantics=("arbitrary",)),
    )(cols, gain, ws, bt, dy, dcols)


def _sw_block(first, q, kp, kc, vp, vc, z, sinks):
    P = BLK
    lane = lax.broadcasted_iota(jnp.int32, (1, 128), 1)
    r = lax.broadcasted_iota(jnp.int32, (128, 128), 0)
    c = lax.broadcasted_iota(jnp.int32, (128, 128), 1)
    swap = (c == (r + 64) % 128).astype(F32)
    k2 = jnp.concatenate([kp, kc], axis=0)
    v2 = jnp.concatenate([vp, vc], axis=0)
    k2s = bdot("nn", k2, swap)
    v2s = bdot("nn", v2, swap)
    qi = lax.broadcasted_iota(jnp.int32, (P, 2 * P), 0)
    kj = lax.broadcasted_iota(jnp.int32, (P, 2 * P), 1)
    dist = qi + P - kj
    valid = (dist >= 0) & (dist < P) & ((kj >= P) | jnp.logical_not(first))
    outs = []
    for j in range(4):
        acc = jnp.zeros((P, 128), F32)
        for half in range(2):
            h = 2 * j + half
            kv = h // 4
            in_half = (lane >= 64 * half) & (lane < 64 * half + 64)
            qh = jnp.where(in_half, q[:, 128 * j:128 * (j + 1)], 0.0)
            same = (half == kv)
            s = bdot("nt", qh, k2 if same else k2s) * (64 ** -0.5)
            s = jnp.where(valid, s, NEG_INF)
            sink = _lane_col(sinks, h)
            m = lax.stop_gradient(jnp.maximum(jnp.max(s, axis=1, keepdims=True), sink))
            e = jnp.exp(s - m)
            p = e / (jnp.sum(e, axis=1, keepdims=True) + jnp.exp(sink - m))
            o = bdot("nn", p, v2 if same else v2s)
            acc = acc + jnp.where(in_half, o, 0.0)
        outs.append(acc)
    return jnp.concatenate(outs, axis=1) * _silu(z)


def _sw_specs(jblk, idx):
    prev = lambda i: jnp.maximum(idx(i) - 1, 0)
    jk = (jblk * SW_W + 1024) // 128
    return [pl.BlockSpec((BLK, SW_W), lambda i: (idx(i), jblk)),
            pl.BlockSpec((BLK, 128), lambda i: (prev(i), jk)),
            pl.BlockSpec((BLK, 128), lambda i: (prev(i), jk + 1)), _full((1, 128))]


def sw_forward(cols, jblk, sinks, name):
    T = cols.shape[0]

    def body(x_ref, kp_ref, vp_ref, s_ref, y_ref):
        y_ref[...] = _sw_block(pl.program_id(0) == 0, x_ref[:, 0:512], kp_ref[...], x_ref[:, 1024:1152],
                               vp_ref[...], x_ref[:, 1152:1280], x_ref[:, 512:1024], s_ref[...])

    return pl.pallas_call(
        body, name=name, out_shape=jax.ShapeDtypeStruct((T, 512), F32), grid=(T // BLK,),
        in_specs=_sw_specs(jblk, lambda i: i),
        out_specs=pl.BlockSpec((BLK, 512), lambda i: (i, 0)),
        compiler_params=pltpu.CompilerParams(dimension_semantics=("parallel",)),
    )(cols, cols, cols, sinks)


def sw_backward(cols, jblk, sinks, dy, dcols, name):
    T = cols.shape[0]
    n = T // BLK
    rev = lambda i: n - 1 - i

    def body(x_ref, kp_ref, vp_ref, s_ref, dy_ref, dcols_in, dx_ref, ds_ref, kcarry, vcarry):
        i = pl.program_id(0)

        @pl.when(i == 0)
        def _():
            kcarry[...] = jnp.zeros_like(kcarry)
            vcarry[...] = jnp.zeros_like(vcarry)
            ds_ref[...] = jnp.zeros_like(ds_ref)

        f = functools.partial(_sw_block, i == n - 1)
        _, vjp = jax.vjp(f, x_ref[:, 0:512], kp_ref[...], x_ref[:, 1024:1152], vp_ref[...], x_ref[:, 1152:1280],
                         x_ref[:, 512:1024], s_ref[...])
        dq, dkp, dkc, dvp, dvc, dz, dsk = vjp(dy_ref[...])
        dx_ref[...] = jnp.concatenate([dq, dz, dkc + kcarry[...], dvc + vcarry[...],
                                       jnp.zeros((BLK, SW_W - 1280), F32)], axis=1).astype(dx_ref.dtype)
        kcarry[...] = dkp
        vcarry[...] = dvp
        ds_ref[...] += dsk

    return pl.pallas_call(
        body, name=name,
        out_shape=[jax.ShapeDtypeStruct(dcols.shape, dcols.dtype),jax.ShapeDtypeStruct((1, 128), F32)],
        grid=(n,),
        in_specs=_sw_specs(jblk, rev) + [pl.BlockSpec((BLK, 512), lambda i: (rev(i), 0)), _ANY],
        out_specs=[pl.BlockSpec((BLK, SW_W), lambda i: (rev(i), jblk)), _full((1, 128))],
        scratch_shapes=[pltpu.VMEM((BLK, 128), F32), pltpu.VMEM((BLK, 128), F32)],
        input_output_aliases={5: 0},
        compiler_params=pltpu.CompilerParams(dimension_semantics=("arbitrary",)),
    )(cols, cols, cols, sinks, dy, dcols)


XM_TQ = 256


def _xm_block(q, z, mkv):
    outs = []
    for h in range(4):
        s = bdot("nt", q[:, 128 * h:128 * (h + 1)], mkv[:, 128 * h:128 * (h + 1)]) * (128 ** -0.5)
        m = lax.stop_gradient(jnp.max(s, axis=1, keepdims=True))
        e = jnp.exp(s - m)
        p = e / jnp.sum(e, axis=1, keepdims=True)
        outs.append(bdot("nn", p, mkv[:, 512 + 128 * h:512 + 128 * (h + 1)]))
    return jnp.concatenate(outs, axis=1) * _silu(z)


def xm_forward(cols, jblk, mkv, name):
    T = cols.shape[0]

    def body(x_ref, m_ref, y_ref):
        y_ref[...] = _xm_block(x_ref[:, 0:512], x_ref[:, 512:1024], m_ref[...])

    return pl.pallas_call(
        body, name=name, out_shape=jax.ShapeDtypeStruct((T, 512), F32), grid=(T // XM_TQ,),
        in_specs=[pl.BlockSpec((XM_TQ, XM_W), lambda i: (i, jblk)), _full(mkv.shape)],
        out_specs=pl.BlockSpec((XM_TQ, 512), lambda i: (i, 0)),
        compiler_params=pltpu.CompilerParams(dimension_semantics=("parallel",)),
    )(cols, mkv)


def xm_backward(cols, jblk, mkv, dy, dcols, name):
    T = cols.shape[0]

    def body(x_ref, m_ref, dy_ref, dcols_in, dx_ref, dm_ref):
        @pl.when(pl.program_id(0) == 0)
        def _():
            dm_ref[...] = jnp.zeros_like(dm_ref)

        _, vjp = jax.vjp(_xm_block, x_ref[:, 0:512], x_ref[:, 512:1024], m_ref[...])
        dq, dz, dm = vjp(dy_ref[...])
        dx_ref[...] = jnp.concatenate([dq, dz], axis=1).astype(dx_ref.dtype)
        dm_ref[...] += dm

    return pl.pallas_call(
        body, name=name,
        out_shape=[jax.ShapeDtypeStruct(dcols.shape, dcols.dtype),jax.ShapeDtypeStruct(mkv.shape, F32)],
        grid=(T // XM_TQ,),
        in_specs=[pl.BlockSpec((XM_TQ, XM_W), lambda i: (i, jblk)), _full(mkv.shape),
                  pl.BlockSpec((XM_TQ, 512), lambda i: (i, 0)), _ANY],
        out_specs=[pl.BlockSpec((XM_TQ, XM_W), lambda i: (i, jblk)), _full(mkv.shape)],
        input_output_aliases={3: 0},
        compiler_params=pltpu.CompilerParams(dimension_semantics=("arbitrary",)),
    )(cols, mkv, dy, dcols)


def _rms(x, gain):
    return x * lax.rsqrt(jnp.mean(x * x, axis=1, keepdims=True) + EPS) * gain


def memkv_forward(mem, gain, w, name):
    def body(m_ref, g_ref, w_ref, o_ref):
        o_ref[...] = bdot("nn", _rms(m_ref[...], g_ref[...]), w_ref[...])

    return pl.pallas_call(body, name=name, out_shape=jax.ShapeDtypeStruct(mem.shape, F32),
                          compiler_params=pltpu.CompilerParams(vmem_limit_bytes=VMEM_LIMIT))(mem, gain, w)


def memkv_backward(mem, gain, w, dkv, name):
    def body(m_ref, g_ref, w_ref, d_ref, dg_ref, dw_ref):
        mem_v = m_ref[...]
        _, vjp = jax.vjp(lambda g, ww: bdot("nn", _rms(mem_v, g), ww), g_ref[...], w_ref[...].astype(F32))
        dg, dw = vjp(d_ref[...])
        dg_ref[...] = dg
        dw_ref[...] = dw

    return pl.pallas_call(body, name=name,
                          out_shape=[jax.ShapeDtypeStruct(gain.shape, F32), jax.ShapeDtypeStruct(w.shape, F32)],
                          compiler_params=pltpu.CompilerParams(vmem_limit_bytes=VMEM_LIMIT))(mem, gain, w, dkv)


MG_TB = 256


def _merge_block(ys, gl, wup, wout, gpost):
    merged = None
    for n in range(4):
        t = _sigmoid(gl[:, 1024 * n:1024 * (n + 1)]) * bdot("nn", ys[n], wup[n])
        merged = t if merged is None else merged + t
    out = bdot("nn", merged, wout)
    return _rms(out, gpost)


def merge_forward(ys, cols, jgate, x, wup, wout, gpost, name):
    T = x.shape[0]
    TB = 256

    def body(ya, yb, yc, ym, gl_ref, x_ref, wup_ref, wout_ref, gp_ref, o_ref):
        upd = _merge_block([ya[...], yb[...], yc[...], ym[...]], gl_ref[...],
                           [wup_ref[n] for n in range(4)], wout_ref[...], gp_ref[...])
        o_ref[...] = x_ref[...] + upd

    yspec = pl.BlockSpec((TB, 512), lambda i: (i, 0))
    return pl.pallas_call(
        body, name=name, out_shape=jax.ShapeDtypeStruct((T, 1024), F32), grid=(T // TB,),
        in_specs=[yspec] * 4 + [pl.BlockSpec((TB, 4096), lambda i: (i, jgate)),
                                pl.BlockSpec((TB, 1024), lambda i: (i, 0)),
                                _full(wup.shape), _full(wout.shape), _full((1, 1024))],
        out_specs=pl.BlockSpec((TB, 1024), lambda i: (i, 0)),
        compiler_params=pltpu.CompilerParams(dimension_semantics=("parallel",), vmem_limit_bytes=VMEM_LIMIT),
    )(*ys, cols, x, wup, wout, gpost)


def _token_product(a, b, name):
    (T, m), n = a.shape, b.shape[1]

    def body(a_ref, b_ref, o_ref):
        o_ref[...] = lax.dot_general(a_ref[...].astype(BF), b_ref[...].astype(BF), _DIMS["tn"], preferred_element_type=F32)

    return pl.pallas_call(body, name=name, out_shape=jax.ShapeDtypeStruct((m, n), F32),
                          compiler_params=pltpu.CompilerParams(vmem_limit_bytes=VMEM_LIMIT))(a, b)


def merge_backward(ys, cols, jgate, wup, wout, gpost, dx, name):
    T = dx.shape[0]
    TB = MG_TB

    def body(ya, yb, yc, ym, gl_ref, wup_ref, wout_ref, gp_ref, dx_ref,
             dgl_ref, dya, dyb, dyc, dym, dpa, dpb, dpc, dpm, merged_ref, dout_ref, dgp_ref):
        @pl.when(pl.program_id(0) == 0)
        def _():
            dgp_ref[...] = jnp.zeros_like(dgp_ref)

        y_refs = (ya, yb, yc, ym)
        gates = [_sigmoid(gl_ref[:, 1024 * n:1024 * (n + 1)]) for n in range(4)]
        projs = [bdot("nn", y_refs[n][...], wup_ref[n]) for n in range(4)]
        merged = gates[0] * projs[0] + gates[1] * projs[1] + gates[2] * projs[2] + gates[3] * projs[3]
        out = bdot("nn", merged, wout_ref[...])
        _, vjp = jax.vjp(_rms, out, gp_ref[...])
        dout, dgp = vjp(dx_ref[...])
        dmerged = bdot("nt", dout, wout_ref[...])
        for n, (dy_ref, dp_ref) in enumerate(zip((dya, dyb, dyc, dym), (dpa, dpb, dpc, dpm))):
            dproj = dmerged * gates[n]
            dgl_ref[:, 1024 * n:1024 * (n + 1)] = (dmerged * projs[n] * gates[n] * (1.0 - gates[n])).astype(dgl_ref.dtype)
            dy_ref[...] = bdot("nt", dproj, wup_ref[n])
            dp_ref[...] = dproj.astype(BF)
        merged_ref[...] = merged.astype(BF)
        dout_ref[...] = dout.astype(BF)
        dgp_ref[...] += dgp

    yspec = pl.BlockSpec((TB, 512), lambda i: (i, 0))
    dspec = pl.BlockSpec((TB, 1024), lambda i: (i, 0))
    dcols, dya, dyb, dyc, dym, *dproj, merged, dout, dgp = pl.pallas_call(
        body, name=name,
        out_shape=[jax.ShapeDtypeStruct(cols.shape, BF)] + [jax.ShapeDtypeStruct((T, 512), F32)] * 4 + [
            jax.ShapeDtypeStruct((T, 1024), BF)] * 6 + [jax.ShapeDtypeStruct((1, 1024), F32)],
        grid=(T // TB,),
        in_specs=[yspec] * 4 + [pl.BlockSpec((TB, 4096), lambda i: (i, jgate)),
                                _full(wup.shape), _full(wout.shape), _full((1, 1024)), dspec],
        out_specs=[pl.BlockSpec((TB, 4096), lambda i: (i, jgate))] + [yspec] * 4 + [
            dspec] * 6 + [_full((1, 1024))],
        compiler_params=pltpu.CompilerParams(dimension_semantics=("arbitrary",), vmem_limit_bytes=VMEM_LIMIT),
    )(*ys, cols, wup, wout, gpost, dx)
    dwup = jnp.stack([_token_product(ys[n], dproj[n], "%s_w_up%d" % (name, n)) for n in range(4)])
    dwout = _token_product(merged, dout, name + "_w_out")
    return dcols, dya, dyb, dyc, dym, dwup, dwout, dgp


NB = 256


def prenorm_forward(x, gain, name):
    T, D = x.shape

    def body(x_ref, g_ref, o_ref, ot_ref):
        h = _rms(x_ref[...], g_ref[...])
        o_ref[...] = h.astype(BF)
        ot_ref[...] = h.T.astype(BF)

    return pl.pallas_call(
        body, name=name,
        out_shape=[jax.ShapeDtypeStruct((T, D), BF), jax.ShapeDtypeStruct((D, T), BF)], grid=(T // NB,),
        in_specs=[pl.BlockSpec((NB, D), lambda i: (i, 0)), _full((1, D))],
        out_specs=[pl.BlockSpec((NB, D), lambda i: (i, 0)), pl.BlockSpec((D, NB), lambda i: (0, i))],
        compiler_params=pltpu.CompilerParams(dimension_semantics=("parallel",)),
    )(x, gain)


def prenorm_backward(x, gain, dh, dres, name):
    T = x.shape[0]

    def body(x_ref, g_ref, dh_ref, dr_ref, dx_ref, dg_ref):
        @pl.when(pl.program_id(0) == 0)
        def _():
            dg_ref[...] = jnp.zeros_like(dg_ref)

        _, vjp = jax.vjp(_rms, x_ref[...], g_ref[...])
        dxn, dg = vjp(dh_ref[...])
        dx_ref[...] = dr_ref[...] + dxn
        dg_ref[...] += dg

    spec = pl.BlockSpec((NB, 1024), lambda i: (i, 0))
    return pl.pallas_call(
        body, name=name,
        out_shape=[jax.ShapeDtypeStruct(x.shape, F32), jax.ShapeDtypeStruct((1, 1024), F32)], grid=(T // NB,),
        in_specs=[spec, _full((1, 1024)), spec, spec], out_specs=[spec, _full((1, 1024))],
        compiler_params=pltpu.CompilerParams(dimension_semantics=("arbitrary",)),
    )(x, gain, dh, dres)


def loss_head(y, target, name):
    T, D = y.shape

    def body(y_ref, t_ref, l_ref, d_ref):
        @pl.when(pl.program_id(0) == 0)
        def _():
            l_ref[...] = jnp.zeros_like(l_ref)

        err = y_ref[...] - t_ref[...]
        d_ref[...] = err * (1.0 / D)
        l_ref[...] += jnp.full(l_ref.shape, 0.5 * jnp.sum(jnp.mean(err * err, axis=1, keepdims=True)), F32)

    spec = pl.BlockSpec((NB, D), lambda i: (i, 0))
    return pl.pallas_call(
        body, name=name,
        out_shape=[jax.ShapeDtypeStruct((1, 128), F32), jax.ShapeDtypeStruct(y.shape, F32)], grid=(T // NB,),
        in_specs=[spec, spec], out_specs=[_full((1, 128)), spec],
        compiler_params=pltpu.CompilerParams(dimension_semantics=("arbitrary",)),
    )(y, target)


JB_GATE, JB_XM, JB_DN, JB_SW, JB_GM = 0, 4, 2, 5, 6
_ALIGNED_PIECES = ((5896, 4096), (4872, 512), (5384, 512), (0, 2048), (2048, 8), 504, (3592, 512), (4360, 512),
                   (4104, 128), (4232, 128), 256, (2056, 1024), (3080, 512))
_NATURAL_FROM_ALIGNED = ((5120, 2048), (7168, 8), (9216, 1024), (10240, 512), (7680, 512), (8704, 128), (8832, 128),
                         (8192, 512), (4096, 512), (4608, 512), (0, 4096))


def _natural_range(slots, start, width):
    out = []
    while width > 0:
        j, i = divmod(start, W_IN_SHARD)
        take = min(width, W_IN_SHARD - i)
        out.append(slots[j, :, i:i + take])
        start, width = start + take, width - take
    return out


def _aligned_w_in(slots):
    parts = []
    for piece in _ALIGNED_PIECES:
        if isinstance(piece, int):
            parts.append(jnp.zeros(slots.shape[1:2] + (piece,), slots.dtype))
        else:
            parts += _natural_range(slots, *piece)
    return jnp.concatenate(parts, axis=-1)


def _slots_of_aligned(d_al):
    slots = []
    for s in range(N_DEV):
        lo, hi = s * W_IN_SHARD, (s + 1) * W_IN_SHARD
        parts, nat = [], 0
        for a_start, width in _NATURAL_FROM_ALIGNED:
            b, e = max(lo, nat), min(hi, nat + width)
            if b < e:
                parts.append(d_al[..., a_start + b - nat:a_start + e - nat])
            nat += width
        parts.append(jnp.zeros(d_al.shape[:1] + (W_IN_SHARD_PAD - W_IN_SHARD,), d_al.dtype))
        slots.append(jnp.concatenate(parts, axis=-1))
    return jnp.stack(slots)


SMALL_VEC_W = 1024


def _pack_small(parts):
    rows = []
    for p in parts:
        flat = p.reshape(-1).astype(F32)
        r = -(-flat.shape[0] // SMALL_VEC_W)
        rows.append(jnp.pad(flat, (0, r * SMALL_VEC_W - flat.shape[0])).reshape(r, SMALL_VEC_W))
    vec = jnp.concatenate(rows, axis=0)
    return jnp.pad(vec, ((0, -vec.shape[0] % 8), (0, 0)))


def _unpack_small(vec, shapes):
    out, off = [], 0
    for s in shapes:
        n = math.prod(s)
        r = -(-n // SMALL_VEC_W)
        out.append(vec[off:off + r].reshape(-1)[:n].reshape(s))
        off += r
    return out


def _lanes(vec, at):
    return jnp.zeros((1, 128), F32).at[0, at:at + vec.shape[0]].set(vec)


SMALL_NAMES = ("norm_pre", "norm_post", "norm_mem", "a_log", "dt_bias", "dn_norm", "gm_norm",
               "spatial_w", "spatial_b", "sinks")


def _other_weights(s_mem, s_up, s_out):
    return (s_mem.reshape(D_MODEL, 2 * BRANCH_W),
            jnp.transpose(s_up, (1, 2, 0, 3)).reshape(N_BRANCH, BRANCH_W, D_MODEL), s_out.reshape(D_MODEL, D_MODEL))


def _grad_slots(d_in_al, d_mem, d_up, d_out):
    return [None if d_in_al is None else _slots_of_aligned(d_in_al), d_mem.astype(BF).reshape(N_DEV, 128, 2 * BRANCH_W),
            jnp.transpose(d_up.astype(BF).reshape(N_BRANCH, BRANCH_W, N_DEV, 128), (2, 0, 1, 3)),
            d_out.astype(BF).reshape(N_DEV, 128, D_MODEL)]


def _layer_params(l, small, conv_full, token):
    return dict(
        gpre=small["norm_pre"][l][None] + token, gpost=small["norm_post"][l][None], gmem=small["norm_mem"][l][None],
        cw=conv_full[l], al=_lanes(small["a_log"][l], 4), dt=_lanes(small["dt_bias"][l], 4),
        dnn=small["dn_norm"][l][None], gain=small["gm_norm"][l][None], ws=small["spatial_w"][l],
        bt=jnp.zeros((128, 128), F32).at[:, :GM_GROUPS].set(small["spatial_b"][l].T),
        sinks=_lanes(small["sinks"][l], 0))


def _layer_forward(l, xl, mem, p, w_in_al, other_weights):
    t = "l%d_" % l
    h, h_t = prenorm_forward(xl, p["gpre"], t + "prenorm")
    cols = _matmul(h, w_in_al, "nn", F32, (1024, 1536, 1024), t + "w_in")
    ya, ss, ts = dn_forward(cols, JB_DN, p["cw"], p["al"], p["dt"], p["dnn"], t + "deltanet")
    yb = gm_forward(cols, JB_GM, p["gain"], p["ws"], p["bt"], t + "gmlp")
    yc = sw_forward(cols, JB_SW, p["sinks"], t + "swa")
    w_mem, w_up, w_out = other_weights(yc)
    mkv = memkv_forward(mem, p["gmem"], w_mem, t + "memkv")
    ym = xm_forward(cols, JB_XM, mkv, t + "memattn")
    xn = merge_forward([ya, yb, yc, ym], cols, JB_GATE, xl, w_up, w_out, p["gpost"], t + "merge")
    return xn, dict(p, x=xl, h_t=h_t, cols=cols, mkv=mkv, ss=ss, ts=ts, ys=[ya, yb, yc, ym]), (w_in_al, w_mem, w_up, w_out)


def _layer_backward(l, s, mem, weights, dx, token, early=None):
    w_in_al, w_mem, w_up, w_out = weights
    t = "l%d_" % l
    cols = s["cols"]
    dcols, dya, dyb, dyc, dym, dwup, dwout, dgpost = merge_backward(
        s["ys"], cols, JB_GATE, w_up, w_out, s["gpost"] + token, dx, t + "merge_bwd")
    dcols, dmkv = xm_backward(cols, JB_XM, s["mkv"], dym, dcols, t + "memattn_bwd")
    dgmem, dwmem = memkv_backward(mem, s["gmem"], w_mem, dmkv, t + "memkv_bwd")
    sinks = s["sinks"] if early is None else s["sinks"] + early(dwmem, dwup, dwout)
    dcols, dsinks = sw_backward(cols, JB_SW, sinks, dyc, dcols, t + "swa_bwd")
    dcols, dgain, dws, dbt = gm_backward(cols, JB_GM, s["gain"], s["ws"], s["bt"], dyb, dcols, t + "gmlp_bwd")
    dcols, dcw, dal, ddt, ddn = dn_backward(
        cols, JB_DN, s["cw"], s["al"], s["dt"], s["dnn"], s["ss"], s["ts"], dya, dcols, t + "deltanet_bwd")
    dh = _matmul(dcols, w_in_al, "nt", F32, (1024, 1024, 3584), t + "w_in_bwd_x")
    dwin = _matmul(s["h_t"], dcols, "nn", BF, (1024, 1536, 2048), t + "w_in_bwd_w")
    dx, dgpre = prenorm_backward(s["x"], s["gpre"], dh, dx, t + "prenorm_bwd")
    gsmall = dict(norm_pre=dgpre[0], norm_post=dgpost[0], norm_mem=dgmem[0], a_log=dal[0, 4:8], dt_bias=ddt[0, 4:8],
                  dn_norm=ddn[0], gm_norm=dgain[0], spatial_w=dws, spatial_b=dbt[:, :GM_GROUPS].T,
                  sinks=dsinks[0, :SW_HEADS], conv_w=dcw)
    return dx, gsmall, (dwin, dwmem, dwup, dwout)


def kernel(x, mem, norm_pre, norm_post, norm_mem, w_in, conv_w, a_log, dt_bias, dn_norm, gm_norm, spatial_w, spatial_b, sinks, w_mem_kv, w_up, w_out, loss_target, m_norm_pre, m_norm_post, m_norm_mem, m_w_in, m_conv_w, m_a_log, m_dt_bias, m_dn_norm, m_gm_norm, m_spatial_w, m_spatial_b, m_sinks, m_w_mem_kv, m_w_up, m_w_out, v_norm_pre, v_norm_post, v_norm_mem, v_w_in, v_conv_w, v_a_log, v_dt_bias, v_dn_norm, v_gm_norm, v_spatial_w, v_spatial_b, v_sinks, v_w_mem_kv, v_w_up, v_w_out):
    xi, yi, ci = _my_place()
    my_slot = 4 * xi + 2 * yi + ci
    conv_shard = conv_w.shape[-1]
    x2, mem2, target = x[0], mem[0], loss_target[0]

    w_in_pad = jnp.pad(w_in.astype(BF), ((0, 0), (0, 0), (0, W_IN_SHARD_PAD - W_IN_SHARD)))
    shards = [[w_in_pad[l], w_mem_kv[l].astype(BF), w_up[l].astype(BF), w_out[l].astype(BF)] for l in range(DEPTH)]
    w_in_slots0, w_up_slots0, w_out_slots0, conv_slots = _all_gather_slots(
        [shards[0][0], shards[0][2], shards[0][3], conv_w], "gather_weights_l0")
    ag = list(_spread_start([shards[0][1]] + shards[1], "gather", "gather_weights_rest_start"))
    conv_full = jnp.transpose(conv_slots, (1, 2, 0, 3)).reshape(DEPTH, CONV_W, N_DEV * conv_shard)
    small = dict(norm_pre=norm_pre, norm_post=norm_post, norm_mem=norm_mem, a_log=a_log,
                 dt_bias=dt_bias, dn_norm=dn_norm, gm_norm=gm_norm, spatial_w=spatial_w,
                 spatial_b=spatial_b, sinks=sinks)

    def arrived(which, after, name):
        ag[2], ag[3] = _spread_wait(ag[0], ag[1], ag[2], ag[3], which, after, name)
        return [ag[3][a] for a in which]

    x1, saved0, weights0 = _layer_forward(
        0, x2, mem2, _layer_params(0, small, conv_full, ag[4][0, 0]), _aligned_w_in(w_in_slots0),
        lambda y: _other_weights(*arrived([0], y, "gather_weights_l0_w_mem_wait"), w_up_slots0, w_out_slots0))
    w_in_slots1, = arrived([1], x1, "gather_weights_l1_w_in_wait")
    x_out, saved1, weights1 = _layer_forward(
        1, x1, mem2, _layer_params(1, small, conv_full, 0.0), _aligned_w_in(w_in_slots1),
        lambda y: _other_weights(*arrived([2, 3, 4], y, "gather_weights_l1_rest_wait")))
    loss, dx = loss_head(x_out, target, "loss_head")

    dx, gsmall1, gbig1 = _layer_backward(1, saved1, mem2, weights1, dx, 0.0)
    rs_send, rs_recv, rs_src, rs_land, rs_token = _spread_start(_grad_slots(*gbig1), "scatter", "exchange_grads_l1_start")
    rest0 = []

    def send_rest0(dwmem, dwup, dwout):
        rest0.extend(_spread_start(_grad_slots(None, dwmem, dwup, dwout)[1:], "scatter", "exchange_grads_l0_rest_start"))
        return rest0[4][0, 0]

    dx, gsmall0, gbig0 = _layer_backward(0, saved0, mem2, weights0, dx, rs_token[0, 0], send_rest0)
    _, parts1 = _spread_wait(rs_send, rs_recv, rs_src, rs_land, range(4), dx, "exchange_grads_l1_wait")

    packed_names = SMALL_NAMES + ("conv_w",)
    gs = {n: jnp.stack([gsmall0[n], gsmall1[n]]) for n in packed_names}
    small_parts = [loss[0, :1]] + [gs[n] for n in packed_names]
    sm = _spread_start([_pack_small(small_parts)], "gather", "gather_small_grads_start")

    g_win0 = _slots_of_aligned(gbig0[0])
    g_win0 = g_win0.reshape((N_DEV // 2, 2) + g_win0.shape[1:])
    theirs, = _exchange_sibling([g_win0], "exchange_sibling_l0")
    chip_sum = _pair_sum(g_win0, theirs, "pair_sum_l0")
    ch_send, ch_recv, ch_src, ch_land, ch_token = _spread_start([chip_sum], "chips", "exchange_chips_l0_start")

    _, (small_land,) = _spread_wait(*sm[:4], [0], ch_token, "gather_small_grads_wait")
    tot = _unpack_small(_sum_slots(small_land, "sum_small_grads"), [p.shape for p in small_parts])
    loss_tot = tot[0][0]
    grads = dict(zip(packed_names, tot[1:]))
    grads["conv_w"] = lax.dynamic_slice_in_dim(grads["conv_w"], my_slot * conv_shard, conv_shard, axis=2)

    given = dict(norm_pre=(norm_pre, m_norm_pre, v_norm_pre), norm_post=(norm_post, m_norm_post, v_norm_post),
                 norm_mem=(norm_mem, m_norm_mem, v_norm_mem), a_log=(a_log, m_a_log, v_a_log),
                 dt_bias=(dt_bias, m_dt_bias, v_dt_bias), dn_norm=(dn_norm, m_dn_norm, v_dn_norm),
                 gm_norm=(gm_norm, m_gm_norm, v_gm_norm), spatial_w=(spatial_w, m_spatial_w, v_spatial_w),
                 spatial_b=(spatial_b, m_spatial_b, v_spatial_b), sinks=(sinks, m_sinks, v_sinks),
                 conv_w=(conv_w, m_conv_w, v_conv_w))
    pshapes = [given[n][0].shape for n in packed_names]
    pw, pm, pv = (_pack_small([given[n][i] for n in packed_names]) for i in range(3))
    pd, pnm, pnv = _adamw(pw + ch_token[0, 0], _pack_small([grads[n] for n in packed_names]), pm, pv, "adamw_small")
    upd = {n: t for n, t in zip(packed_names, zip(_unpack_small(pd, pshapes), _unpack_small(pnm, pshapes),
                                                  _unpack_small(pnv, pshapes)))}
    big = (("w_mem_kv", (w_mem_kv, m_w_mem_kv, v_w_mem_kv)), ("w_up", (w_up, m_w_up, v_w_up)),
           ("w_out", (w_out, m_w_out, v_w_out)))
    first = [_sum_adamw(parts1[1 + i], w, m, v, 1, None, ch_token, "adamw_%s_l1" % name)
             for i, (name, (w, m, v)) in enumerate(big)]
    _, parts0_rest = _spread_wait(*rest0[:4], range(3), first[-1][0], "exchange_grads_l0_rest_wait")
    for i, (name, (w, m, v)) in enumerate(big):
        g, d, nm, nv = _sum_adamw(parts0_rest[i], w, m, v, 0, first[i], None, "adamw_%s_l0" % name)
        grads[name], upd[name] = g, (d, nm, nv)
    _, (parts0_w_in,) = _spread_wait(ch_send, ch_recv, ch_src, ch_land, [0], upd["w_out"][0], "exchange_chips_l0_wait")
    w_in_t, m_w_in_t, v_w_in_t = (jnp.transpose(t, (2, 0, 1)) for t in (w_in, m_w_in, v_w_in))
    g, d, nm, nv = (jnp.transpose(t, (1, 2, 0)) for t in
                    _sum_adamw_t([parts0_w_in, parts1[0]], w_in_t, m_w_in_t, v_w_in_t, "adamw_w_in"))
    grads["w_in"], upd["w_in"] = g, (d, nm, nv)

    order = ("norm_pre", "norm_post", "norm_mem", "w_in", "conv_w", "a_log", "dt_bias", "dn_norm",
             "gm_norm", "spatial_w", "spatial_b", "sinks", "w_mem_kv", "w_up", "w_out")
    return (loss_tot, dx[None], *[grads[n] for n in order], *[upd[n][0] for n in order],
            *[upd[n][1] for n in order], *[upd[n][2] for n in order])
```

```python
import functools
import math

import jax
import jax.numpy as jnp
from jax import lax
from jax.experimental import pallas as pl
from jax.experimental.pallas import tpu as pltpu

MESH = pl.DeviceIdType.MESH
N_DEV = 8

D_MODEL = 1024
DEPTH = 2
N_BRANCH = 4
BRANCH_W = 512
DN_HEADS = 4
CONV_W = 4
GM_GROUPS = 4
SW_HEADS = 8
EPS = 1e-6
NEG_INF = -1e30

D_IN = 9992
W_IN_SHARD = D_IN // N_DEV
W_IN_SHARD_PAD = 1280
D_IN_AL = 10752
DN_W, SW_W, GM_W, XM_W = 2560, 1536, 1536, 1024

ADAM_LR = 0.001
ADAM_B1 = 0.9
ADAM_B2 = 0.999
ADAM_EPS = 1e-08
ADAM_WD = 0.01
ADAM_STEP = 10

VMEM_LIMIT = 56 * 1024 * 1024

BF = jnp.bfloat16
F32 = jnp.float32
DN_C = 128
DN_D = 128
HALO = 8
BLK = 128


def _my_place():
    return lax.axis_index("x"), lax.axis_index("y"), lax.axis_index("c")


_ANY = pl.BlockSpec(memory_space=pl.ANY)


def _all_gather_slots(parts, name):
    n = len(parts)

    def body(*refs):
        p_refs, out_refs = refs[:n], refs[n:2 * n]
        send_sems, recv_sems, local_sems = refs[2 * n:]
        x, y, c = _my_place()
        me, sibling = (x, y, c), (x, y, 1 - c)
        chips = [(1 - x, y), (x, 1 - y), (1 - x, 1 - y)]

        def copy(a, k, block, to, src=None):
            px, py, pc = block
            slot = out_refs[a].at[4 * px + 2 * py + pc]
            return pltpu.make_async_remote_copy(
                src_ref=slot if src is None else src, dst_ref=slot,
                send_sem=send_sems.at[7 * a + k], recv_sem=recv_sems.at[7 * a + k],
                device_id=to, device_id_type=MESH)

        mine = [pltpu.make_async_copy(p_refs[a], out_refs[a].at[4 * x + 2 * y + c], local_sems.at[a])
                for a in range(n)]
        for cp in mine:
            cp.start()
        first = []
        for a in range(n):
            first.append(copy(a, 0, me, sibling, src=p_refs[a]))
            first += [copy(a, 1 + j, me, (*chip, c), src=p_refs[a]) for j, chip in enumerate(chips)]
        for cp in first:
            cp.start()
        passed = []
        for j, chip in enumerate(chips):
            for a in range(n):
                copy(a, 1 + j, (*chip, c), me).wait_recv()
                fwd = copy(a, 4 + j, (*chip, c), sibling)
                fwd.start()
                passed.append(fwd)
        for a in range(n):
            copy(a, 0, sibling, me).wait_recv()
            for j, chip in enumerate(chips):
                copy(a, 4 + j, (*chip, 1 - c), me).wait_recv()
        for cp in first + passed:
            cp.wait_send()
        for cp in mine:
            cp.wait()

    return pl.pallas_call(
        body, name=name,
        out_shape=[jax.ShapeDtypeStruct((N_DEV,) + p.shape, p.dtype) for p in parts],
        in_specs=[_ANY] * n, out_specs=[_ANY] * n,
        scratch_shapes=[pltpu.SemaphoreType.DMA((7 * n,)), pltpu.SemaphoreType.DMA((7 * n,)),
                        pltpu.SemaphoreType.DMA((n,))],
    )(*parts)


def _exchange_sibling(parts, name):
    n = len(parts)

    def body(*refs):
        g_refs, out_refs = refs[:n], refs[n:2 * n]
        send_sems, recv_sems = refs[2 * n:]
        x, y, c = _my_place()
        copies = [pltpu.make_async_remote_copy(
            src_ref=g_refs[a].at[:, 1 - c], dst_ref=out_refs[a],
            send_sem=send_sems.at[a], recv_sem=recv_sems.at[a],
            device_id=(x, y, 1 - c), device_id_type=MESH) for a in range(n)]
        for cp in copies:
            cp.start()
        for cp in copies:
            cp.wait()

    return pl.pallas_call(
        body, name=name,
        out_shape=[jax.ShapeDtypeStruct((4,) + g.shape[2:], g.dtype) for g in parts],
        in_specs=[_ANY] * n, out_specs=[_ANY] * n,
        scratch_shapes=[pltpu.SemaphoreType.DMA((n,)), pltpu.SemaphoreType.DMA((n,))],
    )(*parts)


_HBM = pl.BlockSpec(memory_space=pltpu.HBM)
_SEM = pl.BlockSpec(memory_space=pltpu.SEMAPHORE)
_EFFECT = pltpu.SideEffectType.DATAFLOW_SIDE_EFFECTING


def _peer(x, y, c, k):
    return (1 - x if (k >> 2) & 1 else x, 1 - y if (k >> 1) & 1 else y, 1 - c if k & 1 else c)


def _spread_start(srcs, mode, name):
    n = len(srcs)
    lands = [lax.empty((N_DEV,) + s.shape if mode == "gather" else s.shape, s.dtype) for s in srcs]
    peers = range(0, N_DEV, 2) if mode == "chips" else range(N_DEV)

    def body(*refs):
        src_refs, land_refs = refs[:n], refs[n:2 * n]
        send_sems, recv_sems = refs[2 * n:2 * n + 2]
        token = refs[-1]
        x, y, c = _my_place()
        for a in range(n):
            for k in peers:
                px, py, pc = _peer(x, y, c, k)
                if mode == "chips":
                    src, mine = src_refs[a].at[2 * px + py], 2 * x + y
                else:
                    src = src_refs[a].at[4 * px + 2 * py + pc] if mode == "scatter" else src_refs[a]
                    mine = 4 * x + 2 * y + c
                pltpu.make_async_remote_copy(
                    src_ref=src, dst_ref=land_refs[a].at[mine],
                    send_sem=send_sems.at[a], recv_sem=recv_sems.at[a],
                    device_id=(px, py, pc), device_id_type=MESH).start()
        token[...] = jnp.zeros_like(token)

    out = pl.pallas_call(
        body, name=name,
        out_shape=[pltpu.SemaphoreType.DMA((n,)), pltpu.SemaphoreType.DMA((n,))]
        + [pltpu.HBM(s.shape, s.dtype) for s in srcs] + [pltpu.HBM(l.shape, l.dtype) for l in lands]
        + [jax.ShapeDtypeStruct((8, 128), F32)],
        in_specs=[_HBM] * (2 * n),
        out_specs=[_SEM, _SEM] + [_HBM] * (2 * n) + [pl.BlockSpec(memory_space=pltpu.VMEM)],
        input_output_aliases={i: 2 + i for i in range(2 * n)},
        compiler_params=pltpu.CompilerParams(has_side_effects=_EFFECT),
    )(*[pltpu.with_memory_space_constraint(s, pltpu.HBM) for s in srcs],
      *[pltpu.with_memory_space_constraint(l, pltpu.HBM) for l in lands])
    return out[0], out[1], out[2:2 + n], out[2 + n:2 + 2 * n], out[-1]


def _spread_wait(send_sems, recv_sems, srcs, lands, which, after, name):
    n = len(srcs)

    def body(*refs):
        land_refs = refs[n:2 * n]
        send_sems, recv_sems = refs[2 * n:2 * n + 2]
        x, y, c = _my_place()
        for a in which:
            whole = pltpu.make_async_remote_copy(
                src_ref=land_refs[a], dst_ref=land_refs[a],
                send_sem=send_sems.at[a], recv_sem=recv_sems.at[a],
                device_id=(x, y, c), device_id_type=MESH)
            whole.wait_send()
            whole.wait_recv()

    out = pl.pallas_call(
        body, name=name,
        out_shape=[pltpu.HBM(s.shape, s.dtype) for s in srcs] + [pltpu.HBM(l.shape, l.dtype) for l in lands],
        in_specs=[_HBM] * (2 * n) + [_SEM, _SEM, _ANY],
        out_specs=[_HBM] * (2 * n),
        input_output_aliases={i: i for i in range(2 * n)},
        compiler_params=pltpu.CompilerParams(has_side_effects=_EFFECT),
    )(*srcs, *lands, send_sems, recv_sems, after)
    return out[:n], out[n:]


def _sum_slots(parts, name):
    def body(p_ref, o_ref):
        acc = p_ref[0]
        for s in range(1, parts.shape[0]):
            acc = acc + p_ref[s]
        o_ref[...] = acc

    return pl.pallas_call(body, name=name, out_shape=jax.ShapeDtypeStruct(parts.shape[1:], parts.dtype))(parts)


def _pick(n, pref):
    if n <= pref:
        return n
    t = pref - pref % 128
    while t > 0 and n % t:
        t -= 128
    return t if t > 0 else n


_DIMS = {"nn": (((1,), (0,)), ((), ())),
         "nt": (((1,), (1,)), ((), ())),
         "tn": (((0,), (0,)), ((), ()))}


def _matmul(a, b, mode, out_dtype, tiles, name):
    (m, k) = a.shape
    n = b.shape[1] if mode == "nn" else b.shape[0]
    tm, tn, tk = (_pick(d, t) for d, t in zip((m, n, k), tiles))
    nk = k // tk

    def product(a_ref, b_ref):
        return lax.dot_general(a_ref[...].astype(BF), b_ref[...].astype(BF), _DIMS[mode], preferred_element_type=F32)

    def body_whole_k(a_ref, b_ref, o_ref):
        o_ref[...] = product(a_ref, b_ref).astype(o_ref.dtype)

    def body_split_k(a_ref, b_ref, o_ref, acc_ref):
        kk = pl.program_id(2)

        @pl.when(kk == 0)
        def _():
            acc_ref[...] = jnp.zeros_like(acc_ref)

        acc_ref[...] += product(a_ref, b_ref)

        @pl.when(kk == nk - 1)
        def _():
            o_ref[...] = acc_ref[...].astype(o_ref.dtype)

    b_spec = (pl.BlockSpec((tn, tk), lambda i, j, kk: (j, kk)) if mode == "nt"
              else pl.BlockSpec((tk, tn), lambda i, j, kk: (kk, j)))
    return pl.pallas_call(
        body_whole_k if nk == 1 else body_split_k, name=name,
        out_shape=jax.ShapeDtypeStruct((m, n), out_dtype),
        grid=(m // tm, n // tn, nk),
        in_specs=[pl.BlockSpec((tm, tk), lambda i, j, kk: (i, kk)), b_spec],
        out_specs=pl.BlockSpec((tm, tn), lambda i, j, kk: (i, j)),
        scratch_shapes=[] if nk == 1 else [pltpu.VMEM((tm, tn), F32)],
        compiler_params=pltpu.CompilerParams(
            dimension_semantics=("parallel", "parallel", "arbitrary"),
            vmem_limit_bytes=VMEM_LIMIT),
    )(a, b)


def _rows2d(t, lead):
    return t.reshape(t.shape[:lead] + (math.prod(t.shape[lead:-1]), t.shape[-1]))


def _pair_sum(g, theirs, name):
    g3, t3 = _rows2d(g, 2), _rows2d(theirs, 1)
    _, r, w = t3.shape
    tr = _pick(r, 512)

    def body(g_ref, t_ref, o_ref):
        c = lax.axis_index("c")
        mine = jnp.where(c == 0, g_ref[0, 0], g_ref[0, 1])
        o_ref[0] = (mine.astype(F32) + t_ref[0].astype(F32)).astype(o_ref.dtype)

    out = pl.pallas_call(
        body, name=name,
        out_shape=jax.ShapeDtypeStruct(t3.shape, t3.dtype),
        grid=(4, r // tr),
        in_specs=[pl.BlockSpec((1, 2, tr, w), lambda q, i: (q, 0, i, 0)),
                  pl.BlockSpec((1, tr, w), lambda q, i: (q, i, 0))],
        out_specs=pl.BlockSpec((1, tr, w), lambda q, i: (q, i, 0)),
        compiler_params=pltpu.CompilerParams(dimension_semantics=("parallel", "parallel")),
    )(g3, t3)
    return out.reshape(theirs.shape)


def _adam_update(w, g, m, v):
    c1 = 1.0 - ADAM_B1 ** ADAM_STEP
    c2 = 1.0 - ADAM_B2 ** ADAM_STEP
    nm = ADAM_B1 * m + (1.0 - ADAM_B1) * g
    nv = ADAM_B2 * v + (1.0 - ADAM_B2) * (g * g)
    delta = -ADAM_LR * ((nm / c1) / (jnp.sqrt(nv / c2) + ADAM_EPS) + ADAM_WD * w)
    return delta, nm, nv


def _sum_adamw(parts, w, m, v, layer, carry, after, name):
    shape = w.shape
    cols = shape[-1]
    p3 = _rows2d(parts, 1)
    w3, m3, v3 = (_rows2d(t, 1) for t in (w, m, v))
    rows = w3.shape[1]
    tr = _pick(rows, 128)
    n_parts = p3.shape[0]

    def body(p_ref, w_ref, m_ref, v_ref, *rest):
        g_ref, d_ref, nm_ref, nv_ref = rest[-4:]
        g = p_ref[0, :, :cols].astype(F32)
        for q in range(1, n_parts):
            g = g + p_ref[q, :, :cols].astype(F32)
        d, nm, nv = _adam_update(w_ref[0], g, m_ref[0], v_ref[0])
        g_ref[0] = g
        d_ref[0] = d
        nm_ref[0] = nm
        nv_ref[0] = nv

    spec = pl.BlockSpec((1, tr, cols), lambda i: (layer, i, 0))
    extra = [] if carry is None else [_rows2d(t, 1) for t in carry]
    tail = [] if after is None else [after]
    out = pl.pallas_call(
        body, name=name,
        out_shape=[jax.ShapeDtypeStruct(w3.shape, F32)] * 4,
        grid=(rows // tr,),
        in_specs=[pl.BlockSpec((n_parts, tr, p3.shape[-1]), lambda i: (0, i, 0)), spec, spec, spec] + [_ANY] * len(extra + tail),
        out_specs=[spec] * 4,
        input_output_aliases={4 + i: i for i in range(len(extra))},
        compiler_params=pltpu.CompilerParams(dimension_semantics=("parallel",)),
    )(p3, w3, m3, v3, *extra, *tail)
    return tuple(t.reshape(shape) for t in out)


def _sum_adamw_t(parts, w, m, v, name):
    rows = parts[0].shape[2]
    tr = 128
    assert rows % tr == 0 and rows >= w.shape[0]

    def body(*refs):
        p_refs, (w_ref, m_ref, v_ref), (g_ref, d_ref, nm_ref, nv_ref) = refs[:DEPTH], refs[DEPTH:DEPTH + 3], refs[DEPTH + 3:]
        for l in range(DEPTH):
            g = p_refs[l][0].astype(F32)
            for q in range(1, p_refs[l].shape[0]):
                g = g + p_refs[l][q].astype(F32)
            g = g.T
            d, nm, nv = _adam_update(w_ref[:, l, :], g, m_ref[:, l, :], v_ref[:, l, :])
            g_ref[:, l, :] = g
            d_ref[:, l, :] = d
            nm_ref[:, l, :] = nm
            nv_ref[:, l, :] = nv

    spec = pl.BlockSpec((tr,) + w.shape[1:], lambda i: (i, 0, 0))
    return pl.pallas_call(
        body, name=name,
        out_shape=[jax.ShapeDtypeStruct(w.shape, F32)] * 4,
        grid=(rows // tr,),
        in_specs=[pl.BlockSpec((p.shape[0], p.shape[1], tr), lambda i: (0, 0, i)) for p in parts] + [spec] * 3,
        out_specs=[spec] * 4,
        compiler_params=pltpu.CompilerParams(dimension_semantics=("parallel",)),
    )(*parts, w, m, v)


def _adamw(w, g, m, v, name):
    rows, cols = w.shape
    tr = _pick(rows, 128)

    def body(w_ref, g_ref, m_ref, v_ref, d_ref, nm_ref, nv_ref):
        d, nm, nv = _adam_update(w_ref[...], g_ref[...], m_ref[...], v_ref[...])
        d_ref[...] = d
        nm_ref[...] = nm
        nv_ref[...] = nv

    spec = pl.BlockSpec((tr, cols), lambda i: (i, 0))
    return pl.pallas_call(
        body, name=name,
        out_shape=[jax.ShapeDtypeStruct((rows, cols), F32)] * 3,
        grid=(rows // tr,),
        in_specs=[spec] * 4, out_specs=[spec] * 3,
        compiler_params=pltpu.CompilerParams(dimension_semantics=("parallel",)),
    )(w, g, m, v)


_VJP = {"nn": (("nt", "gb"), ("tn", "ag")),
        "nt": (("nn", "gb"), ("tn", "ga")),
        "tn": (("nt", "bg"), ("nn", "ag"))}


def _make_dot(cast, precision):
    def raw(mode, a, b):
        return lax.dot_general(cast(a), cast(b), _DIMS[mode], precision=precision,
                               preferred_element_type=F32)

    @functools.partial(jax.custom_vjp, nondiff_argnums=(0,))
    def dot(mode, a, b):
        return raw(mode, a, b)

    def fwd(mode, a, b):
        return raw(mode, a, b), (a, b)

    def bwd(mode, res, g):
        a, b = res
        pick = {"a": a, "b": b, "g": g}
        (ma, ta), (mb, tb) = _VJP[mode]
        return dot(ma, pick[ta[0]], pick[ta[1]]), dot(mb, pick[tb[0]], pick[tb[1]])

    dot.defvjp(fwd, bwd)
    return dot


bdot = _make_dot(lambda t: t.astype(BF), None)
hdot = _make_dot(lambda t: t, lax.Precision.HIGHEST)


def _xdot(mode, a, b):
    return lax.dot_general(a, b, _DIMS[mode], precision=lax.Precision.HIGH, preferred_element_type=F32)


def _unit_lower_inverse(Ls):
    n = Ls[0].shape[0]
    batched = (((2,), (1,)), ((0,), (0,)))
    mm = lambda a, b: lax.dot_general(a, b, batched, precision=lax.Precision.HIGH, preferred_element_type=F32)
    eye = (lax.broadcasted_iota(jnp.int32, (n, n), 0) == lax.broadcasted_iota(jnp.int32, (n, n), 1)).astype(F32)
    p = jnp.stack(Ls)
    t_inv = eye[None] - p
    for _ in range(6):
        p = mm(p, p)
        t_inv = t_inv + mm(t_inv, p)
    return [t_inv[h] for h in range(len(Ls))]


@jax.custom_vjp
def _tri_solve(L, rhs, t_inv):
    return _xdot("nn", t_inv, rhs)


def _tri_solve_fwd(L, rhs, t_inv):
    sol = _xdot("nn", t_inv, rhs)
    return sol, (t_inv, sol)


def _tri_solve_bwd(res, dsol):
    t_inv, sol = res
    drhs = _xdot("tn", t_inv, dsol)
    return -_xdot("nt", drhs, sol), drhs, jnp.zeros_like(t_inv)


_tri_solve.defvjp(_tri_solve_fwd, _tri_solve_bwd)


def _sigmoid(x):
    return 1.0 / (1.0 + jnp.exp(-x))


def _softplus(x):
    return jnp.maximum(x, 0.0) + jnp.log(1.0 + jnp.exp(-jnp.abs(x)))


def _dn_chunk(S, xs, ba, z, cw, al, dt, dn, t_saved=None):
    C = DN_C
    pre = xs[0] * cw[0] + xs[1] * cw[1] + xs[2] * cw[2] + xs[3] * cw[3]
    qkv = pre * _sigmoid(pre)
    lane = lax.broadcasted_iota(jnp.int32, (1, 128), 1)
    sub = lax.broadcasted_iota(jnp.int32, (C, 1), 0)
    row_i = lax.broadcasted_iota(jnp.int32, (C, C), 0)
    col_i = lax.broadcasted_iota(jnp.int32, (C, C), 1)
    strict = row_i > col_i
    incl = row_i >= col_i
    g_all = jnp.where((lane >= 4) & (lane < 8), -jnp.exp(al) * _softplus(ba + dt), 0.0)
    gc_all = hdot("nn", incl.astype(F32), g_all)
    gc_all_t = gc_all.T
    beta_all = _sigmoid(ba)
    glast_all = jnp.sum(jnp.where(sub == C - 1, gc_all, 0.0), axis=0, keepdims=True)
    heads = []
    for h in range(DN_HEADS):
        q = qkv[:, 128 * h:128 * (h + 1)]
        k = qkv[:, 512 + 128 * h:512 + 128 * (h + 1)]
        v = qkv[:, 1024 + 128 * h:1024 + 128 * (h + 1)]
        q = q * lax.rsqrt(jnp.sum(q * q, axis=1, keepdims=True) + EPS) * (DN_D ** -0.5)
        k = k * lax.rsqrt(jnp.sum(k * k, axis=1, keepdims=True) + EPS)
        beta = jnp.sum(jnp.where(lane == h, beta_all, 0.0), axis=1, keepdims=True)
        gc = jnp.sum(jnp.where(lane == 4 + h, gc_all, 0.0), axis=1, keepdims=True)
        gc_row = jnp.sum(jnp.where(sub == 4 + h, gc_all_t, 0.0), axis=0, keepdims=True)
        g_last = jnp.sum(jnp.where(lane == 4 + h, glast_all, 0.0), axis=1, keepdims=True)
        diff = gc - gc_row
        kb = k * beta
        L = jnp.where(strict, bdot("nt", kb, k) * jnp.exp(jnp.where(strict, diff, 0.0)), 0.0)
        heads.append((q, k, v, beta, gc, g_last, diff, kb, L))
    t_invs = _unit_lower_inverse([hd[-1] for hd in heads]) if t_saved is None else t_saved
    ys, s_new = [], []
    for h, (q, k, v, beta, gc, g_last, diff, kb, L) in enumerate(heads):
        sol = _tri_solve(L, jnp.concatenate([v * beta, kb * jnp.exp(gc)], axis=1), t_invs[h])
        u, w = sol[:, :DN_D], sol[:, DN_D:]
        a_qk = jnp.where(incl, bdot("nt", q, k) * jnp.exp(jnp.where(incl, diff, 0.0)), 0.0)
        qg = q * jnp.exp(gc)
        kd = k * jnp.exp(g_last - gc)
        v_new = u - bdot("nn", w, S[h])
        o = bdot("nn", qg, S[h]) + bdot("nn", a_qk, v_new)
        s_new.append(S[h] * jnp.exp(g_last) + bdot("tn", kd, v_new))
        o = o * lax.rsqrt(jnp.mean(o * o, axis=1, keepdims=True) + EPS) * dn
        zh = z[:, 128 * h:128 * (h + 1)]
        ys.append(o * (zh * _sigmoid(zh)))
    return jnp.concatenate(ys, axis=1), tuple(s_new), tuple(t_invs)


def _load_shifted(xbuf, x_ref, halo_ref, first):
    xbuf[0:HALO, :] = jnp.where(first, 0.0, halo_ref[:, 0:1536])
    xbuf[HALO:HALO + DN_C, :] = x_ref[:, 0:1536]
    return [xbuf[HALO - 3 + k:HALO - 3 + k + DN_C, :] for k in range(4)]


def dn_forward(cols, jblk, cw, al, dt, dn, name):
    T = cols.shape[0]
    n = T // DN_C

    def body(x_ref, halo_ref, cw_ref, al_ref, dt_ref, dn_ref, y_ref, ss_ref, ts_ref, s_scr, xbuf):
        i = pl.program_id(0)

        @pl.when(i == 0)
        def _():
            s_scr[...] = jnp.zeros_like(s_scr)

        xs = _load_shifted(xbuf, x_ref, halo_ref, i == 0)
        ss_ref[0] = s_scr[...]
        S = [s_scr[h] for h in range(DN_HEADS)]
        cws = [cw_ref[k:k + 1, :] for k in range(4)]
        y, s_new, t_invs = _dn_chunk(S, xs, x_ref[:, 2048:2176], x_ref[:, 1536:2048], cws,
                                     al_ref[...], dt_ref[...], dn_ref[...])
        y_ref[...] = y
        for h in range(DN_HEADS):
            s_scr[h] = s_new[h]
            ts_ref[0, h] = t_invs[h]

    per = DN_C // HALO
    full = lambda shape: pl.BlockSpec(shape, lambda i: (0,) * len(shape))
    return pl.pallas_call(
        body, name=name,
        out_shape=[jax.ShapeDtypeStruct((T, 512), F32),
                   jax.ShapeDtypeStruct((n, DN_HEADS, DN_D, DN_D), F32),
                   jax.ShapeDtypeStruct((n, DN_HEADS, DN_D, DN_D), F32)],
        grid=(n,),
        in_specs=[pl.BlockSpec((DN_C, DN_W), lambda i: (i, jblk)),
                  pl.BlockSpec((HALO, DN_W), lambda i: (jnp.maximum(i * per - 1, 0), jblk)),
                  full((4, 1536)), full((1, 128)), full((1, 128)), full((1, 128))],
        out_specs=[pl.BlockSpec((DN_C, 512), lambda i: (i, 0)),
                   pl.BlockSpec((1, DN_HEADS, DN_D, DN_D), lambda i: (i, 0, 0, 0)),
                   pl.BlockSpec((1, DN_HEADS, DN_D, DN_D), lambda i: (i, 0, 0, 0))],
        scratch_shapes=[pltpu.VMEM((DN_HEADS, DN_D, DN_D), F32), pltpu.VMEM((HALO + DN_C, 1536), F32)],
        compiler_params=pltpu.CompilerParams(dimension_semantics=("arbitrary",)),
    )(cols, cols, cw, al, dt, dn)


def dn_backward(cols, jblk, cw, al, dt, dn, ss, ts, dy, dcols, name):
    T = cols.shape[0]
    n = T // DN_C

    def body(x_ref, halo_ref, cw_ref, al_ref, dt_ref, dn_ref, ss_ref, ts_ref, dy_ref, dcols_in,
             dx_ref, dcw_ref, dal_ref, ddt_ref, ddn_ref, ds_scr, xbuf, dbuf, carry):
        i = pl.program_id(0)

        @pl.when(i == 0)
        def _():
            ds_scr[...] = jnp.zeros_like(ds_scr)
            carry[...] = jnp.zeros_like(carry)
            dcw_ref[...] = jnp.zeros_like(dcw_ref)
            dal_ref[...] = jnp.zeros_like(dal_ref)
            ddt_ref[...] = jnp.zeros_like(ddt_ref)
            ddn_ref[...] = jnp.zeros_like(ddn_ref)

        xs = _load_shifted(xbuf, x_ref, halo_ref, i == n - 1)
        S = [ss_ref[0, h] for h in range(DN_HEADS)]
        cws = [cw_ref[k:k + 1, :] for k in range(4)]

        t_saved = [ts_ref[0, h] for h in range(DN_HEADS)]

        def f(S, xs, ba, z, cws, al, dt, dn):
            return _dn_chunk(S, xs, ba, z, cws, al, dt, dn, t_saved)[:2]

        _, vjp = jax.vjp(f, S, xs, x_ref[:, 2048:2176], x_ref[:, 1536:2048], cws, al_ref[...], dt_ref[...], dn_ref[...])
        dS, dxs, dba, dz, dcws, dal, ddt, ddn = vjp((dy_ref[...], tuple(ds_scr[h] for h in range(DN_HEADS))))
        for h in range(DN_HEADS):
            ds_scr[h] = dS[h]
        dbuf[...] = jnp.zeros_like(dbuf)
        for k in range(4):
            lo = HALO - 3 + k
            dbuf[lo:lo + DN_C, :] += dxs[k]
        dbuf[DN_C:DN_C + HALO, :] += carry[...]
        dx_ref[...] = jnp.concatenate([dbuf[HALO:HALO + DN_C, :], dz, dba,
                                       jnp.zeros((DN_C, DN_W - 2176), F32)], axis=1).astype(dx_ref.dtype)
        carry[...] = dbuf[0:HALO, :]
        for k in range(4):
            dcw_ref[k:k + 1, :] += dcws[k]
        dal_ref[...] += dal
        ddt_ref[...] += ddt
        ddn_ref[...] += ddn

    per = DN_C // HALO
    rev = lambda i: n - 1 - i
    full = lambda shape: pl.BlockSpec(shape, lambda i: (0,) * len(shape))
    return pl.pallas_call(
        body, name=name,
        out_shape=[jax.ShapeDtypeStruct(dcols.shape, dcols.dtype),jax.ShapeDtypeStruct((4, 1536), F32),
                   jax.ShapeDtypeStruct((1, 128), F32), jax.ShapeDtypeStruct((1, 128), F32),
                   jax.ShapeDtypeStruct((1, 128), F32)],
        grid=(n,),
        in_specs=[pl.BlockSpec((DN_C, DN_W), lambda i: (rev(i), jblk)),
                  pl.BlockSpec((HALO, DN_W), lambda i: (jnp.maximum(rev(i) * per - 1, 0), jblk)),
                  full((4, 1536)), full((1, 128)), full((1, 128)), full((1, 128)),
                  pl.BlockSpec((1, DN_HEADS, DN_D, DN_D), lambda i: (rev(i), 0, 0, 0)),
                  pl.BlockSpec((1, DN_HEADS, DN_D, DN_D), lambda i: (rev(i), 0, 0, 0)),
                  pl.BlockSpec((DN_C, 512), lambda i: (rev(i), 0)), _ANY],
        out_specs=[pl.BlockSpec((DN_C, DN_W), lambda i: (rev(i), jblk)),
                   full((4, 1536)), full((1, 128)), full((1, 128)), full((1, 128))],
        scratch_shapes=[pltpu.VMEM((DN_HEADS, DN_D, DN_D), F32), pltpu.VMEM((HALO + DN_C, 1536), F32),
                        pltpu.VMEM((HALO + DN_C, 1536), F32), pltpu.VMEM((HALO, 1536), F32)],
        input_output_aliases={9: 0},
        compiler_params=pltpu.CompilerParams(dimension_semantics=("arbitrary",)),
    )(cols, cols, cw, al, dt, dn, ss, ts, dy, dcols)


def _full(shape):
    return pl.BlockSpec(shape, lambda i: (0,) * len(shape))


def _silu(x):
    return x * _sigmoid(x)


def _gelu(x):
    return 0.5 * x * (1.0 + jnp.tanh(0.7978845608028654 * (x + 0.044715 * (x * x * x))))


def _lane_col(mat, idx):
    lane = lax.broadcasted_iota(jnp.int32, (1, mat.shape[1]), 1)
    return jnp.sum(jnp.where(lane == idx, mat, 0.0), axis=1, keepdims=True)


def _gm_chunk(uv, z, gain, ws, bt):
    g = _gelu(uv)
    u, v = g[:, :512], g[:, 512:]
    v = v * lax.rsqrt(jnp.mean(v * v, axis=1, keepdims=True) + EPS) * gain
    row_i = lax.broadcasted_iota(jnp.int32, (BLK, BLK), 0)
    col_i = lax.broadcasted_iota(jnp.int32, (BLK, BLK), 1)
    causal = row_i >= col_i
    ss = []
    for grp in range(4):
        wg = jnp.where(causal, ws[grp], 0.0)
        ss.append(bdot("nn", wg, v[:, BLK * grp:BLK * (grp + 1)]) + _lane_col(bt, grp))
    return u * jnp.concatenate(ss, axis=1) * _silu(z)


def gm_forward(cols, jblk, gain, ws, bt, name):
    T = cols.shape[0]

    def body(x_ref, gain_ref, ws_ref, bt_ref, y_ref):
        y_ref[...] = _gm_chunk(x_ref[:, 0:1024], x_ref[:, 1024:1536], gain_ref[...],
                               [ws_ref[g] for g in range(4)], bt_ref[...])

    return pl.pallas_call(
        body, name=name, out_shape=jax.ShapeDtypeStruct((T, 512), F32), grid=(T // BLK,),
        in_specs=[pl.BlockSpec((BLK, GM_W), lambda i: (i, jblk)),
                  _full((1, 512)), _full((4, BLK, BLK)), _full((BLK, BLK))],
        out_specs=pl.BlockSpec((BLK, 512), lambda i: (i, 0)),
        compiler_params=pltpu.CompilerParams(dimension_semantics=("parallel",)),
    )(cols, gain, ws, bt)


def gm_backward(cols, jblk, gain, ws, bt, dy, dcols, name):
    T = cols.shape[0]

    def body(x_ref, gain_ref, ws_ref, bt_ref, dy_ref, dcols_in, dx_ref, dgain_ref, dws_ref, dbt_ref):
        @pl.when(pl.program_id(0) == 0)
        def _():
            dgain_ref[...] = jnp.zeros_like(dgain_ref)
            dws_ref[...] = jnp.zeros_like(dws_ref)
            dbt_ref[...] = jnp.zeros_like(dbt_ref)

        _, vjp = jax.vjp(_gm_chunk, x_ref[:, 0:1024], x_ref[:, 1024:1536], gain_ref[...],
                         [ws_ref[g] for g in range(4)], bt_ref[...])
        duv, dz, dgain, dws, dbt = vjp(dy_ref[...])
        dx_ref[...] = jnp.concatenate([duv, dz], axis=1).astype(dx_ref.dtype)
        dgain_ref[...] += dgain
        for g in range(4):
            dws_ref[g] += dws[g]
        dbt_ref[...] += dbt

    return pl.pallas_call(
        body, name=name,
        out_shape=[jax.ShapeDtypeStruct(dcols.shape, dcols.dtype),jax.ShapeDtypeStruct((1, 512), F32),
                   jax.ShapeDtypeStruct((4, BLK, BLK), F32), jax.ShapeDtypeStruct((BLK, BLK), F32)],
        grid=(T // BLK,),
        in_specs=[pl.BlockSpec((BLK, GM_W), lambda i: (i, jblk)),
                  _full((1, 512)), _full((4, BLK, BLK)), _full((BLK, BLK)),
                  pl.BlockSpec((BLK, 512), lambda i: (i, 0)), _ANY],
        out_specs=[pl.BlockSpec((BLK, GM_W), lambda i: (i, jblk)),
                   _full((1, 512)), _full((4, BLK, BLK)), _full((BLK, BLK))],
        input_output_aliases={5: 0},
        compiler_params=pltpu.CompilerParams(dimension_semantics=("arbitrary",)),
    )(cols, gain, ws, bt, dy, dcols)


def _sw_block(first, q, kp, kc, vp, vc, z, sinks):
    P = BLK
    lane = lax.broadcasted_iota(jnp.int32, (1, 128), 1)
    r = lax.broadcasted_iota(jnp.int32, (128, 128), 0)
    c = lax.broadcasted_iota(jnp.int32, (128, 128), 1)
    swap = (c == (r + 64) % 128).astype(F32)
    k2 = jnp.concatenate([kp, kc], axis=0)
    v2 = jnp.concatenate([vp, vc], axis=0)
    k2s = bdot("nn", k2, swap)
    v2s = bdot("nn", v2, swap)
    qi = lax.broadcasted_iota(jnp.int32, (P, 2 * P), 0)
    kj = lax.broadcasted_iota(jnp.int32, (P, 2 * P), 1)
    dist = qi + P - kj
    valid = (dist >= 0) & (dist < P) & ((kj >= P) | jnp.logical_not(first))
    outs = []
    for j in range(4):
        acc = jnp.zeros((P, 128), F32)
        for half in range(2):
            h = 2 * j + half
            kv = h // 4
            in_half = (lane >= 64 * half) & (lane < 64 * half + 64)
            qh = jnp.where(in_half, q[:, 128 * j:128 * (j + 1)], 0.0)
            same = (half == kv)
            s = bdot("nt", qh, k2 if same else k2s) * (64 ** -0.5)
            s = jnp.where(valid, s, NEG_INF)
            sink = _lane_col(sinks, h)
            m = lax.stop_gradient(jnp.maximum(jnp.max(s, axis=1, keepdims=True), sink))
            e = jnp.exp(s - m)
            p = e / (jnp.sum(e, axis=1, keepdims=True) + jnp.exp(sink - m))
            o = bdot("nn", p, v2 if same else v2s)
            acc = acc + jnp.where(in_half, o, 0.0)
        outs.append(acc)
    return jnp.concatenate(outs, axis=1) * _silu(z)


def _sw_specs(jblk, idx):
    prev = lambda i: jnp.maximum(idx(i) - 1, 0)
    jk = (jblk * SW_W + 1024) // 128
    return [pl.BlockSpec((BLK, SW_W), lambda i: (idx(i), jblk)),
            pl.BlockSpec((BLK, 128), lambda i: (prev(i), jk)),
            pl.BlockSpec((BLK, 128), lambda i: (prev(i), jk + 1)), _full((1, 128))]


def sw_forward(cols, jblk, sinks, name):
    T = cols.shape[0]

    def body(x_ref, kp_ref, vp_ref, s_ref, y_ref):
        y_ref[...] = _sw_block(pl.program_id(0) == 0, x_ref[:, 0:512], kp_ref[...], x_ref[:, 1024:1152],
                               vp_ref[...], x_ref[:, 1152:1280], x_ref[:, 512:1024], s_ref[...])

    return pl.pallas_call(
        body, name=name, out_shape=jax.ShapeDtypeStruct((T, 512), F32), grid=(T // BLK,),
        in_specs=_sw_specs(jblk, lambda i: i),
        out_specs=pl.BlockSpec((BLK, 512), lambda i: (i, 0)),
        compiler_params=pltpu.CompilerParams(dimension_semantics=("parallel",)),
    )(cols, cols, cols, sinks)


def sw_backward(cols, jblk, sinks, dy, dcols, name):
    T = cols.shape[0]
    n = T // BLK
    rev = lambda i: n - 1 - i

    def body(x_ref, kp_ref, vp_ref, s_ref, dy_ref, dcols_in, dx_ref, ds_ref, kcarry, vcarry):
        i = pl.program_id(0)

        @pl.when(i == 0)
        def _():
            kcarry[...] = jnp.zeros_like(kcarry)
            vcarry[...] = jnp.zeros_like(vcarry)
            ds_ref[...] = jnp.zeros_like(ds_ref)

        f = functools.partial(_sw_block, i == n - 1)
        _, vjp = jax.vjp(f, x_ref[:, 0:512], kp_ref[...], x_ref[:, 1024:1152], vp_ref[...], x_ref[:, 1152:1280],
                         x_ref[:, 512:1024], s_ref[...])
        dq, dkp, dkc, dvp, dvc, dz, dsk = vjp(dy_ref[...])
        dx_ref[...] = jnp.concatenate([dq, dz, dkc + kcarry[...], dvc + vcarry[...],
                                       jnp.zeros((BLK, SW_W - 1280), F32)], axis=1).astype(dx_ref.dtype)
        kcarry[...] = dkp
        vcarry[...] = dvp
        ds_ref[...] += dsk

    return pl.pallas_call(
        body, name=name,
        out_shape=[jax.ShapeDtypeStruct(dcols.shape, dcols.dtype),jax.ShapeDtypeStruct((1, 128), F32)],
        grid=(n,),
        in_specs=_sw_specs(jblk, rev) + [pl.BlockSpec((BLK, 512), lambda i: (rev(i), 0)), _ANY],
        out_specs=[pl.BlockSpec((BLK, SW_W), lambda i: (rev(i), jblk)), _full((1, 128))],
        scratch_shapes=[pltpu.VMEM((BLK, 128), F32), pltpu.VMEM((BLK, 128), F32)],
        input_output_aliases={5: 0},
        compiler_params=pltpu.CompilerParams(dimension_semantics=("arbitrary",)),
    )(cols, cols, cols, sinks, dy, dcols)


XM_TQ = 256


def _xm_block(q, z, mkv):
    outs = []
    for h in range(4):
        s = bdot("nt", q[:, 128 * h:128 * (h + 1)], mkv[:, 128 * h:128 * (h + 1)]) * (128 ** -0.5)
        m = lax.stop_gradient(jnp.max(s, axis=1, keepdims=True))
        e = jnp.exp(s - m)
        p = e / jnp.sum(e, axis=1, keepdims=True)
        outs.append(bdot("nn", p, mkv[:, 512 + 128 * h:512 + 128 * (h + 1)]))
    return jnp.concatenate(outs, axis=1) * _silu(z)


def xm_forward(cols, jblk, mkv, name):
    T = cols.shape[0]

    def body(x_ref, m_ref, y_ref):
        y_ref[...] = _xm_block(x_ref[:, 0:512], x_ref[:, 512:1024], m_ref[...])

    return pl.pallas_call(
        body, name=name, out_shape=jax.ShapeDtypeStruct((T, 512), F32), grid=(T // XM_TQ,),
        in_specs=[pl.BlockSpec((XM_TQ, XM_W), lambda i: (i, jblk)), _full(mkv.shape)],
        out_specs=pl.BlockSpec((XM_TQ, 512), lambda i: (i, 0)),
        compiler_params=pltpu.CompilerParams(dimension_semantics=("parallel",)),
    )(cols, mkv)


def xm_backward(cols, jblk, mkv, dy, dcols, name):
    T = cols.shape[0]

    def body(x_ref, m_ref, dy_ref, dcols_in, dx_ref, dm_ref):
        @pl.when(pl.program_id(0) == 0)
        def _():
            dm_ref[...] = jnp.zeros_like(dm_ref)

        _, vjp = jax.vjp(_xm_block, x_ref[:, 0:512], x_ref[:, 512:1024], m_ref[...])
        dq, dz, dm = vjp(dy_ref[...])
        dx_ref[...] = jnp.concatenate([dq, dz], axis=1).astype(dx_ref.dtype)
        dm_ref[...] += dm

    return pl.pallas_call(
        body, name=name,
        out_shape=[jax.ShapeDtypeStruct(dcols.shape, dcols.dtype),jax.ShapeDtypeStruct(mkv.shape, F32)],
        grid=(T // XM_TQ,),
        in_specs=[pl.BlockSpec((XM_TQ, XM_W), lambda i: (i, jblk)), _full(mkv.shape),
                  pl.BlockSpec((XM_TQ, 512), lambda i: (i, 0)), _ANY],
        out_specs=[pl.BlockSpec((XM_TQ, XM_W), lambda i: (i, jblk)), _full(mkv.shape)],
        input_output_aliases={3: 0},
        compiler_params=pltpu.CompilerParams(dimension_semantics=("arbitrary",)),
    )(cols, mkv, dy, dcols)


def _rms(x, gain):
    return x * lax.rsqrt(jnp.mean(x * x, axis=1, keepdims=True) + EPS) * gain


def memkv_forward(mem, gain, w, name):
    def body(m_ref, g_ref, w_ref, o_ref):
        o_ref[...] = bdot("nn", _rms(m_ref[...], g_ref[...]), w_ref[...])

    return pl.pallas_call(body, name=name, out_shape=jax.ShapeDtypeStruct(mem.shape, F32),
                          compiler_params=pltpu.CompilerParams(vmem_limit_bytes=VMEM_LIMIT))(mem, gain, w)


def memkv_backward(mem, gain, w, dkv, name):
    def body(m_ref, g_ref, w_ref, d_ref, dg_ref, dw_ref):
        mem_v = m_ref[...]
        _, vjp = jax.vjp(lambda g, ww: bdot("nn", _rms(mem_v, g), ww), g_ref[...], w_ref[...].astype(F32))
        dg, dw = vjp(d_ref[...])
        dg_ref[...] = dg
        dw_ref[...] = dw

    return pl.pallas_call(body, name=name,
                          out_shape=[jax.ShapeDtypeStruct(gain.shape, F32), jax.ShapeDtypeStruct(w.shape, F32)],
                          compiler_params=pltpu.CompilerParams(vmem_limit_bytes=VMEM_LIMIT))(mem, gain, w, dkv)


MG_TB = 256


def _merge_block(ys, gl, wup, wout, gpost):
    merged = None
    for n in range(4):
        t = _sigmoid(gl[:, 1024 * n:1024 * (n + 1)]) * bdot("nn", ys[n], wup[n])
        merged = t if merged is None else merged + t
    out = bdot("nn", merged, wout)
    return _rms(out, gpost)


def merge_forward(ys, cols, jgate, x, wup, wout, gpost, name):
    T = x.shape[0]
    TB = 256

    def body(ya, yb, yc, ym, gl_ref, x_ref, wup_ref, wout_ref, gp_ref, o_ref):
        upd = _merge_block([ya[...], yb[...], yc[...], ym[...]], gl_ref[...],
                           [wup_ref[n] for n in range(4)], wout_ref[...], gp_ref[...])
        o_ref[...] = x_ref[...] + upd

    yspec = pl.BlockSpec((TB, 512), lambda i: (i, 0))
    return pl.pallas_call(
        body, name=name, out_shape=jax.ShapeDtypeStruct((T, 1024), F32), grid=(T // TB,),
        in_specs=[yspec] * 4 + [pl.BlockSpec((TB, 4096), lambda i: (i, jgate)),
                                pl.BlockSpec((TB, 1024), lambda i: (i, 0)),
                                _full(wup.shape), _full(wout.shape), _full((1, 1024))],
        out_specs=pl.BlockSpec((TB, 1024), lambda i: (i, 0)),
        compiler_params=pltpu.CompilerParams(dimension_semantics=("parallel",), vmem_limit_bytes=VMEM_LIMIT),
    )(*ys, cols, x, wup, wout, gpost)


def _token_product(a, b, name):
    (T, m), n = a.shape, b.shape[1]

    def body(a_ref, b_ref, o_ref):
        o_ref[...] = lax.dot_general(a_ref[...].astype(BF), b_ref[...].astype(BF), _DIMS["tn"], preferred_element_type=F32)

    return pl.pallas_call(body, name=name, out_shape=jax.ShapeDtypeStruct((m, n), F32),
                          compiler_params=pltpu.CompilerParams(vmem_limit_bytes=VMEM_LIMIT))(a, b)


def merge_backward(ys, cols, jgate, wup, wout, gpost, dx, name):
    T = dx.shape[0]
    TB = MG_TB

    def body(ya, yb, yc, ym, gl_ref, wup_ref, wout_ref, gp_ref, dx_ref,
             dgl_ref, dya, dyb, dyc, dym, dpa, dpb, dpc, dpm, merged_ref, dout_ref, dgp_ref):
        @pl.when(pl.program_id(0) == 0)
        def _():
            dgp_ref[...] = jnp.zeros_like(dgp_ref)

        y_refs = (ya, yb, yc, ym)
        gates = [_sigmoid(gl_ref[:, 1024 * n:1024 * (n + 1)]) for n in range(4)]
        projs = [bdot("nn", y_refs[n][...], wup_ref[n]) for n in range(4)]
        merged = gates[0] * projs[0] + gates[1] * projs[1] + gates[2] * projs[2] + gates[3] * projs[3]
        out = bdot("nn", merged, wout_ref[...])
        _, vjp = jax.vjp(_rms, out, gp_ref[...])
        dout, dgp = vjp(dx_ref[...])
        dmerged = bdot("nt", dout, wout_ref[...])
        for n, (dy_ref, dp_ref) in enumerate(zip((dya, dyb, dyc, dym), (dpa, dpb, dpc, dpm))):
            dproj = dmerged * gates[n]
            dgl_ref[:, 1024 * n:1024 * (n + 1)] = (dmerged * projs[n] * gates[n] * (1.0 - gates[n])).astype(dgl_ref.dtype)
            dy_ref[...] = bdot("nt", dproj, wup_ref[n])
            dp_ref[...] = dproj.astype(BF)
        merged_ref[...] = merged.astype(BF)
        dout_ref[...] = dout.astype(BF)
        dgp_ref[...] += dgp

    yspec = pl.BlockSpec((TB, 512), lambda i: (i, 0))
    dspec = pl.BlockSpec((TB, 1024), lambda i: (i, 0))
    dcols, dya, dyb, dyc, dym, *dproj, merged, dout, dgp = pl.pallas_call(
        body, name=name,
        out_shape=[jax.ShapeDtypeStruct(cols.shape, BF)] + [jax.ShapeDtypeStruct((T, 512), F32)] * 4 + [
            jax.ShapeDtypeStruct((T, 1024), BF)] * 6 + [jax.ShapeDtypeStruct((1, 1024), F32)],
        grid=(T // TB,),
        in_specs=[yspec] * 4 + [pl.BlockSpec((TB, 4096), lambda i: (i, jgate)),
                                _full(wup.shape), _full(wout.shape), _full((1, 1024)), dspec],
        out_specs=[pl.BlockSpec((TB, 4096), lambda i: (i, jgate))] + [yspec] * 4 + [
            dspec] * 6 + [_full((1, 1024))],
        compiler_params=pltpu.CompilerParams(dimension_semantics=("arbitrary",), vmem_limit_bytes=VMEM_LIMIT),
    )(*ys, cols, wup, wout, gpost, dx)
    dwup = jnp.stack([_token_product(ys[n], dproj[n], "%s_w_up%d" % (name, n)) for n in range(4)])
    dwout = _token_product(merged, dout, name + "_w_out")
    return dcols, dya, dyb, dyc, dym, dwup, dwout, dgp


NB = 256


def prenorm_forward(x, gain, name):
    T, D = x.shape

    def body(x_ref, g_ref, o_ref, ot_ref):
        h = _rms(x_ref[...], g_ref[...])
        o_ref[...] = h.astype(BF)
        ot_ref[...] = h.T.astype(BF)

    return pl.pallas_call(
        body, name=name,
        out_shape=[jax.ShapeDtypeStruct((T, D), BF), jax.ShapeDtypeStruct((D, T), BF)], grid=(T // NB,),
        in_specs=[pl.BlockSpec((NB, D), lambda i: (i, 0)), _full((1, D))],
        out_specs=[pl.BlockSpec((NB, D), lambda i: (i, 0)), pl.BlockSpec((D, NB), lambda i: (0, i))],
        compiler_params=pltpu.CompilerParams(dimension_semantics=("parallel",)),
    )(x, gain)


def prenorm_backward(x, gain, dh, dres, name):
    T = x.shape[0]

    def body(x_ref, g_ref, dh_ref, dr_ref, dx_ref, dg_ref):
        @pl.when(pl.program_id(0) == 0)
        def _():
            dg_ref[...] = jnp.zeros_like(dg_ref)

        _, vjp = jax.vjp(_rms, x_ref[...], g_ref[...])
        dxn, dg = vjp(dh_ref[...])
        dx_ref[...] = dr_ref[...] + dxn
        dg_ref[...] += dg

    spec = pl.BlockSpec((NB, 1024), lambda i: (i, 0))
    return pl.pallas_call(
        body, name=name,
        out_shape=[jax.ShapeDtypeStruct(x.shape, F32), jax.ShapeDtypeStruct((1, 1024), F32)], grid=(T // NB,),
        in_specs=[spec, _full((1, 1024)), spec, spec], out_specs=[spec, _full((1, 1024))],
        compiler_params=pltpu.CompilerParams(dimension_semantics=("arbitrary",)),
    )(x, gain, dh, dres)


def loss_head(y, target, name):
    T, D = y.shape

    def body(y_ref, t_ref, l_ref, d_ref):
        @pl.when(pl.program_id(0) == 0)
        def _():
            l_ref[...] = jnp.zeros_like(l_ref)

        err = y_ref[...] - t_ref[...]
        d_ref[...] = err * (1.0 / D)
        l_ref[...] += jnp.full(l_ref.shape, 0.5 * jnp.sum(jnp.mean(err * err, axis=1, keepdims=True)), F32)

    spec = pl.BlockSpec((NB, D), lambda i: (i, 0))
    return pl.pallas_call(
        body, name=name,
        out_shape=[jax.ShapeDtypeStruct((1, 128), F32), jax.ShapeDtypeStruct(y.shape, F32)], grid=(T // NB,),
        in_specs=[spec, spec], out_specs=[_full((1, 128)), spec],
        compiler_params=pltpu.CompilerParams(dimension_semantics=("arbitrary",)),
    )(y, target)


JB_GATE, JB_XM, JB_DN, JB_SW, JB_GM = 0, 4, 2, 5, 6
_ALIGNED_PIECES = ((5896, 4096), (4872, 512), (5384, 512), (0, 2048), (2048, 8), 504, (3592, 512), (4360, 512),
                   (4104, 128), (4232, 128), 256, (2056, 1024), (3080, 512))
_NATURAL_FROM_ALIGNED = ((5120, 2048), (7168, 8), (9216, 1024), (10240, 512), (7680, 512), (8704, 128), (8832, 128),
                         (8192, 512), (4096, 512), (4608, 512), (0, 4096))


def _natural_range(slots, start, width):
    out = []
    while width > 0:
        j, i = divmod(start, W_IN_SHARD)
        take = min(width, W_IN_SHARD - i)
        out.append(slots[j, :, i:i + take])
        start, width = start + take, width - take
    return out


def _aligned_w_in(slots):
    parts = []
    for piece in _ALIGNED_PIECES:
        if isinstance(piece, int):
            parts.append(jnp.zeros(slots.shape[1:2] + (piece,), slots.dtype))
        else:
            parts += _natural_range(slots, *piece)
    return jnp.concatenate(parts, axis=-1)


def _slots_of_aligned(d_al):
    slots = []
    for s in range(N_DEV):
        lo, hi = s * W_IN_SHARD, (s + 1) * W_IN_SHARD
        parts, nat = [], 0
        for a_start, width in _NATURAL_FROM_ALIGNED:
            b, e = max(lo, nat), min(hi, nat + width)
            if b < e:
                parts.append(d_al[..., a_start + b - nat:a_start + e - nat])
            nat += width
        parts.append(jnp.zeros(d_al.shape[:1] + (W_IN_SHARD_PAD - W_IN_SHARD,), d_al.dtype))
        slots.append(jnp.concatenate(parts, axis=-1))
    return jnp.stack(slots)


SMALL_VEC_W = 1024


def _pack_small(parts):
    rows = []
    for p in parts:
        flat = p.reshape(-1).astype(F32)
        r = -(-flat.shape[0] // SMALL_VEC_W)
        rows.append(jnp.pad(flat, (0, r * SMALL_VEC_W - flat.shape[0])).reshape(r, SMALL_VEC_W))
    vec = jnp.concatenate(rows, axis=0)
    return jnp.pad(vec, ((0, -vec.shape[0] % 8), (0, 0)))


def _unpack_small(vec, shapes):
    out, off = [], 0
    for s in shapes:
        n = math.prod(s)
        r = -(-n // SMALL_VEC_W)
        out.append(vec[off:off + r].reshape(-1)[:n].reshape(s))
        off += r
    return out


def _lanes(vec, at):
    return jnp.zeros((1, 128), F32).at[0, at:at + vec.shape[0]].set(vec)


SMALL_NAMES = ("norm_pre", "norm_post", "norm_mem", "a_log", "dt_bias", "dn_norm", "gm_norm",
               "spatial_w", "spatial_b", "sinks")


def _other_weights(s_mem, s_up, s_out):
    return (s_mem.reshape(D_MODEL, 2 * BRANCH_W),
            jnp.transpose(s_up, (1, 2, 0, 3)).reshape(N_BRANCH, BRANCH_W, D_MODEL), s_out.reshape(D_MODEL, D_MODEL))


def _grad_slots(d_in_al, d_mem, d_up, d_out):
    return [None if d_in_al is None else _slots_of_aligned(d_in_al), d_mem.astype(BF).reshape(N_DEV, 128, 2 * BRANCH_W),
            jnp.transpose(d_up.astype(BF).reshape(N_BRANCH, BRANCH_W, N_DEV, 128), (2, 0, 1, 3)),
            d_out.astype(BF).reshape(N_DEV, 128, D_MODEL)]


def _layer_params(l, small, conv_full, token):
    return dict(
        gpre=small["norm_pre"][l][None] + token, gpost=small["norm_post"][l][None], gmem=small["norm_mem"][l][None],
        cw=conv_full[l], al=_lanes(small["a_log"][l], 4), dt=_lanes(small["dt_bias"][l], 4),
        dnn=small["dn_norm"][l][None], gain=small["gm_norm"][l][None], ws=small["spatial_w"][l],
        bt=jnp.zeros((128, 128), F32).at[:, :GM_GROUPS].set(small["spatial_b"][l].T),
        sinks=_lanes(small["sinks"][l], 0))


def _layer_forward(l, xl, mem, p, w_in_al, other_weights):
    t = "l%d_" % l
    h, h_t = prenorm_forward(xl, p["gpre"], t + "prenorm")
    cols = _matmul(h, w_in_al, "nn", F32, (1024, 1536, 1024), t + "w_in")
    ya, ss, ts = dn_forward(cols, JB_DN, p["cw"], p["al"], p["dt"], p["dnn"], t + "deltanet")
    yb = gm_forward(cols, JB_GM, p["gain"], p["ws"], p["bt"], t + "gmlp")
    yc = sw_forward(cols, JB_SW, p["sinks"], t + "swa")
    w_mem, w_up, w_out = other_weights(yc)
    mkv = memkv_forward(mem, p["gmem"], w_mem, t + "memkv")
    ym = xm_forward(cols, JB_XM, mkv, t + "memattn")
    xn = merge_forward([ya, yb, yc, ym], cols, JB_GATE, xl, w_up, w_out, p["gpost"], t + "merge")
    return xn, dict(p, x=xl, h_t=h_t, cols=cols, mkv=mkv, ss=ss, ts=ts, ys=[ya, yb, yc, ym]), (w_in_al, w_mem, w_up, w_out)


def _layer_backward(l, s, mem, weights, dx, token, early=None):
    w_in_al, w_mem, w_up, w_out = weights
    t = "l%d_" % l
    cols = s["cols"]
    dcols, dya, dyb, dyc, dym, dwup, dwout, dgpost = merge_backward(
        s["ys"], cols, JB_GATE, w_up, w_out, s["gpost"] + token, dx, t + "merge_bwd")
    dcols, dmkv = xm_backward(cols, JB_XM, s["mkv"], dym, dcols, t + "memattn_bwd")
    dgmem, dwmem = memkv_backward(mem, s["gmem"], w_mem, dmkv, t + "memkv_bwd")
    sinks = s["sinks"] if early is None else s["sinks"] + early(dwmem, dwup, dwout)
    dcols, dsinks = sw_backward(cols, JB_SW, sinks, dyc, dcols, t + "swa_bwd")
    dcols, dgain, dws, dbt = gm_backward(cols, JB_GM, s["gain"], s["ws"], s["bt"], dyb, dcols, t + "gmlp_bwd")
    dcols, dcw, dal, ddt, ddn = dn_backward(
        cols, JB_DN, s["cw"], s["al"], s["dt"], s["dnn"], s["ss"], s["ts"], dya, dcols, t + "deltanet_bwd")
    dh = _matmul(dcols, w_in_al, "nt", F32, (1024, 1024, 3584), t + "w_in_bwd_x")
    dwin = _matmul(s["h_t"], dcols, "nn", BF, (1024, 1536, 2048), t + "w_in_bwd_w")
    dx, dgpre = prenorm_backward(s["x"], s["gpre"], dh, dx, t + "prenorm_bwd")
    gsmall = dict(norm_pre=dgpre[0], norm_post=dgpost[0], norm_mem=dgmem[0], a_log=dal[0, 4:8], dt_bias=ddt[0, 4:8],
                  dn_norm=ddn[0], gm_norm=dgain[0], spatial_w=dws, spatial_b=dbt[:, :GM_GROUPS].T,
                  sinks=dsinks[0, :SW_HEADS], conv_w=dcw)
    return dx, gsmall, (dwin, dwmem, dwup, dwout)


def kernel(x, mem, norm_pre, norm_post, norm_mem, w_in, conv_w, a_log, dt_bias, dn_norm, gm_norm, spatial_w, spatial_b, sinks, w_mem_kv, w_up, w_out, loss_target, m_norm_pre, m_norm_post, m_norm_mem, m_w_in, m_conv_w, m_a_log, m_dt_bias, m_dn_norm, m_gm_norm, m_spatial_w, m_spatial_b, m_sinks, m_w_mem_kv, m_w_up, m_w_out, v_norm_pre, v_norm_post, v_norm_mem, v_w_in, v_conv_w, v_a_log, v_dt_bias, v_dn_norm, v_gm_norm, v_spatial_w, v_spatial_b, v_sinks, v_w_mem_kv, v_w_up, v_w_out):
    xi, yi, ci = _my_place()
    my_slot = 4 * xi + 2 * yi + ci
    conv_shard = conv_w.shape[-1]
    x2, mem2, target = x[0], mem[0], loss_target[0]

    w_in_pad = jnp.pad(w_in.astype(BF), ((0, 0), (0, 0), (0, W_IN_SHARD_PAD - W_IN_SHARD)))
    shards = [[w_in_pad[l], w_mem_kv[l].astype(BF), w_up[l].astype(BF), w_out[l].astype(BF)] for l in range(DEPTH)]
    w_in_slots0, w_up_slots0, w_out_slots0, conv_slots = _all_gather_slots(
        [shards[0][0], shards[0][2], shards[0][3], conv_w], "gather_weights_l0")
    ag = list(_spread_start([shards[0][1]] + shards[1], "gather", "gather_weights_rest_start"))
    conv_full = jnp.transpose(conv_slots, (1, 2, 0, 3)).reshape(DEPTH, CONV_W, N_DEV * conv_shard)
    small = dict(norm_pre=norm_pre, norm_post=norm_post, norm_mem=norm_mem, a_log=a_log,
                 dt_bias=dt_bias, dn_norm=dn_norm, gm_norm=gm_norm, spatial_w=spatial_w,
                 spatial_b=spatial_b, sinks=sinks)

    def arrived(which, after, name):
        ag[2], ag[3] = _spread_wait(ag[0], ag[1], ag[2], ag[3], which, after, name)
        return [ag[3][a] for a in which]

    x1, saved0, weights0 = _layer_forward(
        0, x2, mem2, _layer_params(0, small, conv_full, ag[4][0, 0]), _aligned_w_in(w_in_slots0),
        lambda y: _other_weights(*arrived([0], y, "gather_weights_l0_w_mem_wait"), w_up_slots0, w_out_slots0))
    w_in_slots1, = arrived([1], x1, "gather_weights_l1_w_in_wait")
    x_out, saved1, weights1 = _layer_forward(
        1, x1, mem2, _layer_params(1, small, conv_full, 0.0), _aligned_w_in(w_in_slots1),
        lambda y: _other_weights(*arrived([2, 3, 4], y, "gather_weights_l1_rest_wait")))
    loss, dx = loss_head(x_out, target, "loss_head")

    rest = [[], []]

    def send_rest(l):
        def send(dwmem, dwup, dwout):
            rest[l].extend(_spread_start(_grad_slots(None, dwmem, dwup, dwout)[1:], "scatter",
                                         "exchange_grads_l%d_rest_start" % l))
            return rest[l][4][0, 0]
        return send

    def send_w_in(d_in_al, l):
        g = _slots_of_aligned(d_in_al)
        g = g.reshape((N_DEV // 2, 2) + g.shape[1:])
        theirs, = _exchange_sibling([g], "exchange_sibling_l%d" % l)
        return _spread_start([_pair_sum(g, theirs, "pair_sum_l%d" % l)], "chips", "exchange_chips_l%d_start" % l)

    dx, gsmall1, gbig1 = _layer_backward(1, saved1, mem2, weights1, dx, 0.0, send_rest(1))
    ch1 = send_w_in(gbig1[0], 1)
    dx, gsmall0, gbig0 = _layer_backward(0, saved0, mem2, weights0, dx, ch1[4][0, 0], send_rest(0))

    packed_names = SMALL_NAMES + ("conv_w",)
    gs = {n: jnp.stack([gsmall0[n], gsmall1[n]]) for n in packed_names}
    small_parts = [loss[0, :1]] + [gs[n] for n in packed_names]
    sm = _spread_start([_pack_small(small_parts)], "gather", "gather_small_grads_start")

    ch0 = send_w_in(gbig0[0], 0)
    ch_token = ch0[4]
    _, (small_land,) = _spread_wait(*sm[:4], [0], ch_token, "gather_small_grads_wait")
    tot = _unpack_small(_sum_slots(small_land, "sum_small_grads"), [p.shape for p in small_parts])
    loss_tot = tot[0][0]
    grads = dict(zip(packed_names, tot[1:]))
    grads["conv_w"] = lax.dynamic_slice_in_dim(grads["conv_w"], my_slot * conv_shard, conv_shard, axis=2)

    given = dict(norm_pre=(norm_pre, m_norm_pre, v_norm_pre), norm_post=(norm_post, m_norm_post, v_norm_post),
                 norm_mem=(norm_mem, m_norm_mem, v_norm_mem), a_log=(a_log, m_a_log, v_a_log),
                 dt_bias=(dt_bias, m_dt_bias, v_dt_bias), dn_norm=(dn_norm, m_dn_norm, v_dn_norm),
                 gm_norm=(gm_norm, m_gm_norm, v_gm_norm), spatial_w=(spatial_w, m_spatial_w, v_spatial_w),
                 spatial_b=(spatial_b, m_spatial_b, v_spatial_b), sinks=(sinks, m_sinks, v_sinks),
                 conv_w=(conv_w, m_conv_w, v_conv_w))
    pshapes = [given[n][0].shape for n in packed_names]
    pw, pm, pv = (_pack_small([given[n][i] for n in packed_names]) for i in range(3))
    pd, pnm, pnv = _adamw(pw + ch_token[0, 0], _pack_small([grads[n] for n in packed_names]), pm, pv, "adamw_small")
    upd = {n: t for n, t in zip(packed_names, zip(_unpack_small(pd, pshapes), _unpack_small(pnm, pshapes),
                                                  _unpack_small(pnv, pshapes)))}
    big = (("w_mem_kv", (w_mem_kv, m_w_mem_kv, v_w_mem_kv)), ("w_up", (w_up, m_w_up, v_w_up)),
           ("w_out", (w_out, m_w_out, v_w_out)))
    _, parts1_rest = _spread_wait(*rest[1][:4], range(3), ch_token, "exchange_grads_l1_rest_wait")
    first = [_sum_adamw(parts1_rest[i], w, m, v, 1, None, None, "adamw_%s_l1" % name)
             for i, (name, (w, m, v)) in enumerate(big)]
    _, parts0_rest = _spread_wait(*rest[0][:4], range(3), first[-1][0], "exchange_grads_l0_rest_wait")
    for i, (name, (w, m, v)) in enumerate(big):
        g, d, nm, nv = _sum_adamw(parts0_rest[i], w, m, v, 0, first[i], None, "adamw_%s_l0" % name)
        grads[name], upd[name] = g, (d, nm, nv)
    _, (parts1_w_in,) = _spread_wait(*ch1[:4], [0], upd["w_out"][0], "exchange_chips_l1_wait")
    _, (parts0_w_in,) = _spread_wait(*ch0[:4], [0], parts1_w_in, "exchange_chips_l0_wait")
    w_in_t, m_w_in_t, v_w_in_t = (jnp.transpose(t, (2, 0, 1)) for t in (w_in, m_w_in, v_w_in))
    g, d, nm, nv = (jnp.transpose(t, (1, 2, 0)) for t in
                    _sum_adamw_t([parts0_w_in, parts1_w_in], w_in_t, m_w_in_t, v_w_in_t, "adamw_w_in"))
    grads["w_in"], upd["w_in"] = g, (d, nm, nv)

    order = ("norm_pre", "norm_post", "norm_mem", "w_in", "conv_w", "a_log", "dt_bias", "dn_norm",
             "gm_norm", "spatial_w", "spatial_b", "sinks", "w_mem_kv", "w_up", "w_out")
    return (loss_tot, dx[None], *[grads[n] for n in order], *[upd[n][0] for n in order],
            *[upd[n][1] for n in order], *[upd[n][2] for n in order])
```

```python
import functools
import math

import jax
import jax.numpy as jnp
from jax import lax
from jax.experimental import pallas as pl
from jax.experimental.pallas import tpu as pltpu

MESH = pl.DeviceIdType.MESH
N_DEV = 8

D_MODEL = 1024
DEPTH = 2
N_BRANCH = 4
BRANCH_W = 512
DN_HEADS = 4
CONV_W = 4
GM_GROUPS = 4
SW_HEADS = 8
EPS = 1e-6
NEG_INF = -1e30

D_IN = 9992
W_IN_SHARD = D_IN // N_DEV
W_IN_SHARD_PAD = 1280
D_IN_AL = 10752
DN_W, SW_W, GM_W, XM_W = 2560, 1536, 1536, 1024

ADAM_LR = 0.001
ADAM_B1 = 0.9
ADAM_B2 = 0.999
ADAM_EPS = 1e-08
ADAM_WD = 0.01
ADAM_STEP = 10

VMEM_LIMIT = 56 * 1024 * 1024

BF = jnp.bfloat16
F32 = jnp.float32
DN_C = 128
DN_D = 128
HALO = 8
BLK = 128


def _my_place():
    return lax.axis_index("x"), lax.axis_index("y"), lax.axis_index("c")


_ANY = pl.BlockSpec(memory_space=pl.ANY)


def _all_gather_slots(parts, name):
    n = len(parts)

    def body(*refs):
        p_refs, out_refs = refs[:n], refs[n:2 * n]
        send_sems, recv_sems, local_sems = refs[2 * n:]
        x, y, c = _my_place()
        me, sibling = (x, y, c), (x, y, 1 - c)
        chips = [(1 - x, y), (x, 1 - y), (1 - x, 1 - y)]

        def copy(a, k, block, to, src=None):
            px, py, pc = block
            slot = out_refs[a].at[4 * px + 2 * py + pc]
            return pltpu.make_async_remote_copy(
                src_ref=slot if src is None else src, dst_ref=slot,
                send_sem=send_sems.at[7 * a + k], recv_sem=recv_sems.at[7 * a + k],
                device_id=to, device_id_type=MESH)

        mine = [pltpu.make_async_copy(p_refs[a], out_refs[a].at[4 * x + 2 * y + c], local_sems.at[a])
                for a in range(n)]
        for cp in mine:
            cp.start()
        first = []
        for a in range(n):
            first.append(copy(a, 0, me, sibling, src=p_refs[a]))
            first += [copy(a, 1 + j, me, (*chip, c), src=p_refs[a]) for j, chip in enumerate(chips)]
        for cp in first:
            cp.start()
        passed = []
        for j, chip in enumerate(chips):
            for a in range(n):
                copy(a, 1 + j, (*chip, c), me).wait_recv()
                fwd = copy(a, 4 + j, (*chip, c), sibling)
                fwd.start()
                passed.append(fwd)
        for a in range(n):
            copy(a, 0, sibling, me).wait_recv()
            for j, chip in enumerate(chips):
                copy(a, 4 + j, (*chip, 1 - c), me).wait_recv()
        for cp in first + passed:
            cp.wait_send()
        for cp in mine:
            cp.wait()

    return pl.pallas_call(
        body, name=name,
        out_shape=[jax.ShapeDtypeStruct((N_DEV,) + p.shape, p.dtype) for p in parts],
        in_specs=[_ANY] * n, out_specs=[_ANY] * n,
        scratch_shapes=[pltpu.SemaphoreType.DMA((7 * n,)), pltpu.SemaphoreType.DMA((7 * n,)),
                        pltpu.SemaphoreType.DMA((n,))],
    )(*parts)


def _exchange_sibling(parts, name):
    n = len(parts)

    def body(*refs):
        g_refs, out_refs = refs[:n], refs[n:2 * n]
        send_sems, recv_sems = refs[2 * n:]
        x, y, c = _my_place()
        copies = [pltpu.make_async_remote_copy(
            src_ref=g_refs[a].at[:, 1 - c], dst_ref=out_refs[a],
            send_sem=send_sems.at[a], recv_sem=recv_sems.at[a],
            device_id=(x, y, 1 - c), device_id_type=MESH) for a in range(n)]
        for cp in copies:
            cp.start()
        for cp in copies:
            cp.wait()

    return pl.pallas_call(
        body, name=name,
        out_shape=[jax.ShapeDtypeStruct((4,) + g.shape[2:], g.dtype) for g in parts],
        in_specs=[_ANY] * n, out_specs=[_ANY] * n,
        scratch_shapes=[pltpu.SemaphoreType.DMA((n,)), pltpu.SemaphoreType.DMA((n,))],
    )(*parts)


_HBM = pl.BlockSpec(memory_space=pltpu.HBM)
_SEM = pl.BlockSpec(memory_space=pltpu.SEMAPHORE)
_EFFECT = pltpu.SideEffectType.DATAFLOW_SIDE_EFFECTING


def _peer(x, y, c, k):
    return (1 - x if (k >> 2) & 1 else x, 1 - y if (k >> 1) & 1 else y, 1 - c if k & 1 else c)


def _spread_start(srcs, mode, name):
    n = len(srcs)
    lands = [lax.empty((N_DEV,) + s.shape if mode == "gather" else s.shape, s.dtype) for s in srcs]
    peers = range(0, N_DEV, 2) if mode == "chips" else range(N_DEV)

    def body(*refs):
        src_refs, land_refs = refs[:n], refs[n:2 * n]
        send_sems, recv_sems = refs[2 * n:2 * n + 2]
        token = refs[-1]
        x, y, c = _my_place()
        for a in range(n):
            for k in peers:
                px, py, pc = _peer(x, y, c, k)
                if mode == "chips":
                    src, mine = src_refs[a].at[2 * px + py], 2 * x + y
                else:
                    src = src_refs[a].at[4 * px + 2 * py + pc] if mode == "scatter" else src_refs[a]
                    mine = 4 * x + 2 * y + c
                pltpu.make_async_remote_copy(
                    src_ref=src, dst_ref=land_refs[a].at[mine],
                    send_sem=send_sems.at[a], recv_sem=recv_sems.at[a],
                    device_id=(px, py, pc), device_id_type=MESH).start()
        token[...] = jnp.zeros_like(token)

    out = pl.pallas_call(
        body, name=name,
        out_shape=[pltpu.SemaphoreType.DMA((n,)), pltpu.SemaphoreType.DMA((n,))]
        + [pltpu.HBM(s.shape, s.dtype) for s in srcs] + [pltpu.HBM(l.shape, l.dtype) for l in lands]
        + [jax.ShapeDtypeStruct((8, 128), F32)],
        in_specs=[_HBM] * (2 * n),
        out_specs=[_SEM, _SEM] + [_HBM] * (2 * n) + [pl.BlockSpec(memory_space=pltpu.VMEM)],
        input_output_aliases={i: 2 + i for i in range(2 * n)},
        compiler_params=pltpu.CompilerParams(has_side_effects=_EFFECT),
    )(*[pltpu.with_memory_space_constraint(s, pltpu.HBM) for s in srcs],
      *[pltpu.with_memory_space_constraint(l, pltpu.HBM) for l in lands])
    return out[0], out[1], out[2:2 + n], out[2 + n:2 + 2 * n], out[-1]


def _spread_wait(send_sems, recv_sems, srcs, lands, which, after, name):
    n = len(srcs)

    def body(*refs):
        land_refs = refs[n:2 * n]
        send_sems, recv_sems = refs[2 * n:2 * n + 2]
        x, y, c = _my_place()
        for a in which:
            whole = pltpu.make_async_remote_copy(
                src_ref=land_refs[a], dst_ref=land_refs[a],
                send_sem=send_sems.at[a], recv_sem=recv_sems.at[a],
                device_id=(x, y, c), device_id_type=MESH)
            whole.wait_send()
            whole.wait_recv()

    out = pl.pallas_call(
        body, name=name,
        out_shape=[pltpu.HBM(s.shape, s.dtype) for s in srcs] + [pltpu.HBM(l.shape, l.dtype) for l in lands],
        in_specs=[_HBM] * (2 * n) + [_SEM, _SEM, _ANY],
        out_specs=[_HBM] * (2 * n),
        input_output_aliases={i: i for i in range(2 * n)},
        compiler_params=pltpu.CompilerParams(has_side_effects=_EFFECT),
    )(*srcs, *lands, send_sems, recv_sems, after)
    return out[:n], out[n:]


def _sum_slots(parts, name):
    def body(p_ref, o_ref):
        acc = p_ref[0]
        for s in range(1, parts.shape[0]):
            acc = acc + p_ref[s]
        o_ref[...] = acc

    return pl.pallas_call(body, name=name, out_shape=jax.ShapeDtypeStruct(parts.shape[1:], parts.dtype))(parts)


def _pick(n, pref):
    if n <= pref:
        return n
    t = pref - pref % 128
    while t > 0 and n % t:
        t -= 128
    return t if t > 0 else n


_DIMS = {"nn": (((1,), (0,)), ((), ())),
         "nt": (((1,), (1,)), ((), ())),
         "tn": (((0,), (0,)), ((), ()))}


def _matmul(a, b, mode, out_dtype, tiles, name):
    (m, k) = a.shape
    n = b.shape[1] if mode == "nn" else b.shape[0]
    tm, tn, tk = (_pick(d, t) for d, t in zip((m, n, k), tiles))
    nk = k // tk

    def product(a_ref, b_ref):
        return lax.dot_general(a_ref[...].astype(BF), b_ref[...].astype(BF), _DIMS[mode], preferred_element_type=F32)

    def body_whole_k(a_ref, b_ref, o_ref):
        o_ref[...] = product(a_ref, b_ref).astype(o_ref.dtype)

    def body_split_k(a_ref, b_ref, o_ref, acc_ref):
        kk = pl.program_id(2)

        @pl.when(kk == 0)
        def _():
            acc_ref[...] = jnp.zeros_like(acc_ref)

        acc_ref[...] += product(a_ref, b_ref)

        @pl.when(kk == nk - 1)
        def _():
            o_ref[...] = acc_ref[...].astype(o_ref.dtype)

    b_spec = (pl.BlockSpec((tn, tk), lambda i, j, kk: (j, kk)) if mode == "nt"
              else pl.BlockSpec((tk, tn), lambda i, j, kk: (kk, j)))
    return pl.pallas_call(
        body_whole_k if nk == 1 else body_split_k, name=name,
        out_shape=jax.ShapeDtypeStruct((m, n), out_dtype),
        grid=(m // tm, n // tn, nk),
        in_specs=[pl.BlockSpec((tm, tk), lambda i, j, kk: (i, kk)), b_spec],
        out_specs=pl.BlockSpec((tm, tn), lambda i, j, kk: (i, j)),
        scratch_shapes=[] if nk == 1 else [pltpu.VMEM((tm, tn), F32)],
        compiler_params=pltpu.CompilerParams(
            dimension_semantics=("parallel", "parallel", "arbitrary"),
            vmem_limit_bytes=VMEM_LIMIT),
    )(a, b)


def _rows2d(t, lead):
    return t.reshape(t.shape[:lead] + (math.prod(t.shape[lead:-1]), t.shape[-1]))


def _pair_sum(g, theirs, name):
    g3, t3 = _rows2d(g, 2), _rows2d(theirs, 1)
    _, r, w = t3.shape
    tr = _pick(r, 512)

    def body(g_ref, t_ref, o_ref):
        c = lax.axis_index("c")
        mine = jnp.where(c == 0, g_ref[0, 0], g_ref[0, 1])
        o_ref[0] = (mine.astype(F32) + t_ref[0].astype(F32)).astype(o_ref.dtype)

    out = pl.pallas_call(
        body, name=name,
        out_shape=jax.ShapeDtypeStruct(t3.shape, t3.dtype),
        grid=(4, r // tr),
        in_specs=[pl.BlockSpec((1, 2, tr, w), lambda q, i: (q, 0, i, 0)),
                  pl.BlockSpec((1, tr, w), lambda q, i: (q, i, 0))],
        out_specs=pl.BlockSpec((1, tr, w), lambda q, i: (q, i, 0)),
        compiler_params=pltpu.CompilerParams(dimension_semantics=("parallel", "parallel")),
    )(g3, t3)
    return out.reshape(theirs.shape)


def _adam_update(w, g, m, v):
    c1 = 1.0 - ADAM_B1 ** ADAM_STEP
    c2 = 1.0 - ADAM_B2 ** ADAM_STEP
    nm = ADAM_B1 * m + (1.0 - ADAM_B1) * g
    nv = ADAM_B2 * v + (1.0 - ADAM_B2) * (g * g)
    delta = -ADAM_LR * ((nm / c1) / (jnp.sqrt(nv / c2) + ADAM_EPS) + ADAM_WD * w)
    return delta, nm, nv


def _sum_adamw(parts, w, m, v, layer, carry, after, name):
    shape = w.shape
    cols = shape[-1]
    p3 = _rows2d(parts, 1)
    w3, m3, v3 = (_rows2d(t, 1) for t in (w, m, v))
    rows = w3.shape[1]
    tr = _pick(rows, 128)
    n_parts = p3.shape[0]

    def body(p_ref, w_ref, m_ref, v_ref, *rest):
        g_ref, d_ref, nm_ref, nv_ref = rest[-4:]
        g = p_ref[0, :, :cols].astype(F32)
        for q in range(1, n_parts):
            g = g + p_ref[q, :, :cols].astype(F32)
        d, nm, nv = _adam_update(w_ref[0], g, m_ref[0], v_ref[0])
        g_ref[0] = g
        d_ref[0] = d
        nm_ref[0] = nm
        nv_ref[0] = nv

    spec = pl.BlockSpec((1, tr, cols), lambda i: (layer, i, 0))
    extra = [] if carry is None else [_rows2d(t, 1) for t in carry]
    tail = [] if after is None else [after]
    out = pl.pallas_call(
        body, name=name,
        out_shape=[jax.ShapeDtypeStruct(w3.shape, F32)] * 4,
        grid=(rows // tr,),
        in_specs=[pl.BlockSpec((n_parts, tr, p3.shape[-1]), lambda i: (0, i, 0)), spec, spec, spec] + [_ANY] * len(extra + tail),
        out_specs=[spec] * 4,
        input_output_aliases={4 + i: i for i in range(len(extra))},
        compiler_params=pltpu.CompilerParams(dimension_semantics=("parallel",)),
    )(p3, w3, m3, v3, *extra, *tail)
    return tuple(t.reshape(shape) for t in out)


def _sum_adamw_t(parts, w, m, v, name):
    rows = parts[0].shape[2]
    tr = 128
    assert rows % tr == 0 and rows >= w.shape[0]

    def body(*refs):
        p_refs, (w_ref, m_ref, v_ref), (g_ref, d_ref, nm_ref, nv_ref) = refs[:DEPTH], refs[DEPTH:DEPTH + 3], refs[DEPTH + 3:]
        for l in range(DEPTH):
            g = p_refs[l][0].astype(F32)
            for q in range(1, p_refs[l].shape[0]):
                g = g + p_refs[l][q].astype(F32)
            g = g.T
            d, nm, nv = _adam_update(w_ref[:, l, :], g, m_ref[:, l, :], v_ref[:, l, :])
            g_ref[:, l, :] = g
            d_ref[:, l, :] = d
            nm_ref[:, l, :] = nm
            nv_ref[:, l, :] = nv

    spec = pl.BlockSpec((tr,) + w.shape[1:], lambda i: (i, 0, 0))
    return pl.pallas_call(
        body, name=name,
        out_shape=[jax.ShapeDtypeStruct(w.shape, F32)] * 4,
        grid=(rows // tr,),
        in_specs=[pl.BlockSpec((p.shape[0], p.shape[1], tr), lambda i: (0, 0, i)) for p in parts] + [spec] * 3,
        out_specs=[spec] * 4,
        compiler_params=pltpu.CompilerParams(dimension_semantics=("parallel",)),
    )(*parts, w, m, v)


def _adamw(w, g, m, v, name):
    rows, cols = w.shape
    tr = _pick(rows, 128)

    def body(w_ref, g_ref, m_ref, v_ref, d_ref, nm_ref, nv_ref):
        d, nm, nv = _adam_update(w_ref[...], g_ref[...], m_ref[...], v_ref[...])
        d_ref[...] = d
        nm_ref[...] = nm
        nv_ref[...] = nv

    spec = pl.BlockSpec((tr, cols), lambda i: (i, 0))
    return pl.pallas_call(
        body, name=name,
        out_shape=[jax.ShapeDtypeStruct((rows, cols), F32)] * 3,
        grid=(rows // tr,),
        in_specs=[spec] * 4, out_specs=[spec] * 3,
        compiler_params=pltpu.CompilerParams(dimension_semantics=("parallel",)),
    )(w, g, m, v)


_VJP = {"nn": (("nt", "gb"), ("tn", "ag")),
        "nt": (("nn", "gb"), ("tn", "ga")),
        "tn": (("nt", "bg"), ("nn", "ag"))}


def _make_dot(cast, precision):
    def raw(mode, a, b):
        return lax.dot_general(cast(a), cast(b), _DIMS[mode], precision=precision,
                               preferred_element_type=F32)

    @functools.partial(jax.custom_vjp, nondiff_argnums=(0,))
    def dot(mode, a, b):
        return raw(mode, a, b)

    def fwd(mode, a, b):
        return raw(mode, a, b), (a, b)

    def bwd(mode, res, g):
        a, b = res
        pick = {"a": a, "b": b, "g": g}
        (ma, ta), (mb, tb) = _VJP[mode]
        return dot(ma, pick[ta[0]], pick[ta[1]]), dot(mb, pick[tb[0]], pick[tb[1]])

    dot.defvjp(fwd, bwd)
    return dot


bdot = _make_dot(lambda t: t.astype(BF), None)
hdot = _make_dot(lambda t: t, lax.Precision.HIGHEST)


def _xdot(mode, a, b):
    return lax.dot_general(a, b, _DIMS[mode], precision=lax.Precision.HIGH, preferred_element_type=F32)


def _unit_lower_inverse(Ls):
    n = Ls[0].shape[0]
    batched = (((2,), (1,)), ((0,), (0,)))
    mm = lambda a, b: lax.dot_general(a, b, batched, precision=lax.Precision.HIGH, preferred_element_type=F32)
    eye = (lax.broadcasted_iota(jnp.int32, (n, n), 0) == lax.broadcasted_iota(jnp.int32, (n, n), 1)).astype(F32)
    p = jnp.stack(Ls)
    t_inv = eye[None] - p
    for _ in range(6):
        p = mm(p, p)
        t_inv = t_inv + mm(t_inv, p)
    return [t_inv[h] for h in range(len(Ls))]


@jax.custom_vjp
def _tri_solve(L, rhs, t_inv):
    return _xdot("nn", t_inv, rhs)


def _tri_solve_fwd(L, rhs, t_inv):
    sol = _xdot("nn", t_inv, rhs)
    return sol, (t_inv, sol)


def _tri_solve_bwd(res, dsol):
    t_inv, sol = res
    drhs = _xdot("tn", t_inv, dsol)
    return -_xdot("nt", drhs, sol), drhs, jnp.zeros_like(t_inv)


_tri_solve.defvjp(_tri_solve_fwd, _tri_solve_bwd)


def _sigmoid(x):
    return 1.0 / (1.0 + jnp.exp(-x))


def _softplus(x):
    return jnp.maximum(x, 0.0) + jnp.log(1.0 + jnp.exp(-jnp.abs(x)))


def _dn_chunk(S, xs, ba, z, cw, al, dt, dn, t_saved=None):
    C = DN_C
    pre = xs[0] * cw[0] + xs[1] * cw[1] + xs[2] * cw[2] + xs[3] * cw[3]
    qkv = pre * _sigmoid(pre)
    lane = lax.broadcasted_iota(jnp.int32, (1, 128), 1)
    sub = lax.broadcasted_iota(jnp.int32, (C, 1), 0)
    row_i = lax.broadcasted_iota(jnp.int32, (C, C), 0)
    col_i = lax.broadcasted_iota(jnp.int32, (C, C), 1)
    strict = row_i > col_i
    incl = row_i >= col_i
    g_all = jnp.where((lane >= 4) & (lane < 8), -jnp.exp(al) * _softplus(ba + dt), 0.0)
    gc_all = hdot("nn", incl.astype(F32), g_all)
    gc_all_t = gc_all.T
    beta_all = _sigmoid(ba)
    glast_all = jnp.sum(jnp.where(sub == C - 1, gc_all, 0.0), axis=0, keepdims=True)
    heads = []
    for h in range(DN_HEADS):
        q = qkv[:, 128 * h:128 * (h + 1)]
        k = qkv[:, 512 + 128 * h:512 + 128 * (h + 1)]
        v = qkv[:, 1024 + 128 * h:1024 + 128 * (h + 1)]
        q = q * lax.rsqrt(jnp.sum(q * q, axis=1, keepdims=True) + EPS) * (DN_D ** -0.5)
        k = k * lax.rsqrt(jnp.sum(k * k, axis=1, keepdims=True) + EPS)
        beta = jnp.sum(jnp.where(lane == h, beta_all, 0.0), axis=1, keepdims=True)
        gc = jnp.sum(jnp.where(lane == 4 + h, gc_all, 0.0), axis=1, keepdims=True)
        gc_row = jnp.sum(jnp.where(sub == 4 + h, gc_all_t, 0.0), axis=0, keepdims=True)
        g_last = jnp.sum(jnp.where(lane == 4 + h, glast_all, 0.0), axis=1, keepdims=True)
        diff = gc - gc_row
        kb = k * beta
        L = jnp.where(strict, bdot("nt", kb, k) * jnp.exp(jnp.where(strict, diff, 0.0)), 0.0)
        heads.append((q, k, v, beta, gc, g_last, diff, kb, L))
    t_invs = _unit_lower_inverse([hd[-1] for hd in heads]) if t_saved is None else t_saved
    ys, s_new = [], []
    for h, (q, k, v, beta, gc, g_last, diff, kb, L) in enumerate(heads):
        sol = _tri_solve(L, jnp.concatenate([v * beta, kb * jnp.exp(gc)], axis=1), t_invs[h])
        u, w = sol[:, :DN_D], sol[:, DN_D:]
        a_qk = jnp.where(incl, bdot("nt", q, k) * jnp.exp(jnp.where(incl, diff, 0.0)), 0.0)
        qg = q * jnp.exp(gc)
        kd = k * jnp.exp(g_last - gc)
        v_new = u - bdot("nn", w, S[h])
        o = bdot("nn", qg, S[h]) + bdot("nn", a_qk, v_new)
        s_new.append(S[h] * jnp.exp(g_last) + bdot("tn", kd, v_new))
        o = o * lax.rsqrt(jnp.mean(o * o, axis=1, keepdims=True) + EPS) * dn
        zh = z[:, 128 * h:128 * (h + 1)]
        ys.append(o * (zh * _sigmoid(zh)))
    return jnp.concatenate(ys, axis=1), tuple(s_new), tuple(t_invs)


def _load_shifted(xbuf, x_ref, halo_ref, first):
    xbuf[0:HALO, :] = jnp.where(first, 0.0, halo_ref[:, 0:1536])
    xbuf[HALO:HALO + DN_C, :] = x_ref[:, 0:1536]
    return [xbuf[HALO - 3 + k:HALO - 3 + k + DN_C, :] for k in range(4)]


def dn_forward(cols, jblk, cw, al, dt, dn, name):
    T = cols.shape[0]
    n = T // DN_C

    def body(x_ref, halo_ref, cw_ref, al_ref, dt_ref, dn_ref, y_ref, ss_ref, ts_ref, s_scr, xbuf):
        i = pl.program_id(0)

        @pl.when(i == 0)
        def _():
            s_scr[...] = jnp.zeros_like(s_scr)

        xs = _load_shifted(xbuf, x_ref, halo_ref, i == 0)
        ss_ref[0] = s_scr[...]
        S = [s_scr[h] for h in range(DN_HEADS)]
        cws = [cw_ref[k:k + 1, :] for k in range(4)]
        y, s_new, t_invs = _dn_chunk(S, xs, x_ref[:, 2048:2176], x_ref[:, 1536:2048], cws,
                                     al_ref[...], dt_ref[...], dn_ref[...])
        y_ref[...] = y
        for h in range(DN_HEADS):
            s_scr[h] = s_new[h]
            ts_ref[0, h] = t_invs[h]

    per = DN_C // HALO
    full = lambda shape: pl.BlockSpec(shape, lambda i: (0,) * len(shape))
    return pl.pallas_call(
        body, name=name,
        out_shape=[jax.ShapeDtypeStruct((T, 512), F32),
                   jax.ShapeDtypeStruct((n, DN_HEADS, DN_D, DN_D), F32),
                   jax.ShapeDtypeStruct((n, DN_HEADS, DN_D, DN_D), F32)],
        grid=(n,),
        in_specs=[pl.BlockSpec((DN_C, DN_W), lambda i: (i, jblk)),
                  pl.BlockSpec((HALO, DN_W), lambda i: (jnp.maximum(i * per - 1, 0), jblk)),
                  full((4, 1536)), full((1, 128)), full((1, 128)), full((1, 128))],
        out_specs=[pl.BlockSpec((DN_C, 512), lambda i: (i, 0)),
                   pl.BlockSpec((1, DN_HEADS, DN_D, DN_D), lambda i: (i, 0, 0, 0)),
                   pl.BlockSpec((1, DN_HEADS, DN_D, DN_D), lambda i: (i, 0, 0, 0))],
        scratch_shapes=[pltpu.VMEM((DN_HEADS, DN_D, DN_D), F32), pltpu.VMEM((HALO + DN_C, 1536), F32)],
        compiler_params=pltpu.CompilerParams(dimension_semantics=("arbitrary",)),
    )(cols, cols, cw, al, dt, dn)


def dn_backward(cols, jblk, cw, al, dt, dn, ss, ts, dy, dcols, name):
    T = cols.shape[0]
    n = T // DN_C

    def body(x_ref, halo_ref, cw_ref, al_ref, dt_ref, dn_ref, ss_ref, ts_ref, dy_ref, dcols_in,
             dx_ref, dcw_ref, dal_ref, ddt_ref, ddn_ref, ds_scr, xbuf, dbuf, carry):
        i = pl.program_id(0)

        @pl.when(i == 0)
        def _():
            ds_scr[...] = jnp.zeros_like(ds_scr)
            carry[...] = jnp.zeros_like(carry)
            dcw_ref[...] = jnp.zeros_like(dcw_ref)
            dal_ref[...] = jnp.zeros_like(dal_ref)
            ddt_ref[...] = jnp.zeros_like(ddt_ref)
            ddn_ref[...] = jnp.zeros_like(ddn_ref)

        xs = _load_shifted(xbuf, x_ref, halo_ref, i == n - 1)
        S = [ss_ref[0, h] for h in range(DN_HEADS)]
        cws = [cw_ref[k:k + 1, :] for k in range(4)]

        t_saved = [ts_ref[0, h] for h in range(DN_HEADS)]

        def f(S, xs, ba, z, cws, al, dt, dn):
            return _dn_chunk(S, xs, ba, z, cws, al, dt, dn, t_saved)[:2]

        _, vjp = jax.vjp(f, S, xs, x_ref[:, 2048:2176], x_ref[:, 1536:2048], cws, al_ref[...], dt_ref[...], dn_ref[...])
        dS, dxs, dba, dz, dcws, dal, ddt, ddn = vjp((dy_ref[...], tuple(ds_scr[h] for h in range(DN_HEADS))))
        for h in range(DN_HEADS):
            ds_scr[h] = dS[h]
        dbuf[...] = jnp.zeros_like(dbuf)
        for k in range(4):
            lo = HALO - 3 + k
            dbuf[lo:lo + DN_C, :] += dxs[k]
        dbuf[DN_C:DN_C + HALO, :] += carry[...]
        dx_ref[...] = jnp.concatenate([dbuf[HALO:HALO + DN_C, :], dz, dba,
                                       jnp.zeros((DN_C, DN_W - 2176), F32)], axis=1).astype(dx_ref.dtype)
        carry[...] = dbuf[0:HALO, :]
        for k in range(4):
            dcw_ref[k:k + 1, :] += dcws[k]
        dal_ref[...] += dal
        ddt_ref[...] += ddt
        ddn_ref[...] += ddn

    per = DN_C // HALO
    rev = lambda i: n - 1 - i
    full = lambda shape: pl.BlockSpec(shape, lambda i: (0,) * len(shape))
    return pl.pallas_call(
        body, name=name,
        out_shape=[jax.ShapeDtypeStruct(dcols.shape, dcols.dtype),jax.ShapeDtypeStruct((4, 1536), F32),
                   jax.ShapeDtypeStruct((1, 128), F32), jax.ShapeDtypeStruct((1, 128), F32),
                   jax.ShapeDtypeStruct((1, 128), F32)],
        grid=(n,),
        in_specs=[pl.BlockSpec((DN_C, DN_W), lambda i: (rev(i), jblk)),
                  pl.BlockSpec((HALO, DN_W), lambda i: (jnp.maximum(rev(i) * per - 1, 0), jblk)),
                  full((4, 1536)), full((1, 128)), full((1, 128)), full((1, 128)),
                  pl.BlockSpec((1, DN_HEADS, DN_D, DN_D), lambda i: (rev(i), 0, 0, 0)),
                  pl.BlockSpec((1, DN_HEADS, DN_D, DN_D), lambda i: (rev(i), 0, 0, 0)),
                  pl.BlockSpec((DN_C, 512), lambda i: (rev(i), 0)), _ANY],
        out_specs=[pl.BlockSpec((DN_C, DN_W), lambda i: (rev(i), jblk)),
                   full((4, 1536)), full((1, 128)), full((1, 128)), full((1, 128))],
        scratch_shapes=[pltpu.VMEM((DN_HEADS, DN_D, DN_D), F32), pltpu.VMEM((HALO + DN_C, 1536), F32),
                        pltpu.VMEM((HALO + DN_C, 1536), F32), pltpu.VMEM((HALO, 1536), F32)],
        input_output_aliases={9: 0},
        compiler_params=pltpu.CompilerParams(dimension_semantics=("arbitrary",)),
    )(cols, cols, cw, al, dt, dn, ss, ts, dy, dcols)


def _full(shape):
    return pl.BlockSpec(shape, lambda i: (0,) * len(shape))


def _silu(x):
    return x * _sigmoid(x)


def _gelu(x):
    return 0.5 * x * (1.0 + jnp.tanh(0.7978845608028654 * (x + 0.044715 * (x * x * x))))


def _lane_col(mat, idx):
    lane = lax.broadcasted_iota(jnp.int32, (1, mat.shape[1]), 1)
    return jnp.sum(jnp.where(lane == idx, mat, 0.0), axis=1, keepdims=True)


def _gm_chunk(uv, z, gain, ws, bt):
    g = _gelu(uv)
    u, v = g[:, :512], g[:, 512:]
    v = v * lax.rsqrt(jnp.mean(v * v, axis=1, keepdims=True) + EPS) * gain
    row_i = lax.broadcasted_iota(jnp.int32, (BLK, BLK), 0)
    col_i = lax.broadcasted_iota(jnp.int32, (BLK, BLK), 1)
    causal = row_i >= col_i
    ss = []
    for grp in range(4):
        wg = jnp.where(causal, ws[grp], 0.0)
        ss.append(bdot("nn", wg, v[:, BLK * grp:BLK * (grp + 1)]) + _lane_col(bt, grp))
    return u * jnp.concatenate(ss, axis=1) * _silu(z)


def gm_forward(cols, jblk, gain, ws, bt, name):
    T = cols.shape[0]

    def body(x_ref, gain_ref, ws_ref, bt_ref, y_ref):
        y_ref[...] = _gm_chunk(x_ref[:, 0:1024], x_ref[:, 1024:1536], gain_ref[...],
                               [ws_ref[g] for g in range(4)], bt_ref[...])

    return pl.pallas_call(
        body, name=name, out_shape=jax.ShapeDtypeStruct((T, 512), F32), grid=(T // BLK,),
        in_specs=[pl.BlockSpec((BLK, GM_W), lambda i: (i, jblk)),
                  _full((1, 512)), _full((4, BLK, BLK)), _full((BLK, BLK))],
        out_specs=pl.BlockSpec((BLK, 512), lambda i: (i, 0)),
        compiler_params=pltpu.CompilerParams(dimension_semantics=("parallel",)),
    )(cols, gain, ws, bt)


def gm_backward(cols, jblk, gain, ws, bt, dy, dcols, name):
    T = cols.shape[0]

    def body(x_ref, gain_ref, ws_ref, bt_ref, dy_ref, dcols_in, dx_ref, dgain_ref, dws_ref, dbt_ref):
        @pl.when(pl.program_id(0) == 0)
        def _():
            dgain_ref[...] = jnp.zeros_like(dgain_ref)
            dws_ref[...] = jnp.zeros_like(dws_ref)
            dbt_ref[...] = jnp.zeros_like(dbt_ref)

        _, vjp = jax.vjp(_gm_chunk, x_ref[:, 0:1024], x_ref[:, 1024:1536], gain_ref[...],
                         [ws_ref[g] for g in range(4)], bt_ref[...])
        duv, dz, dgain, dws, dbt = vjp(dy_ref[...])
        dx_ref[...] = jnp.concatenate([duv, dz], axis=1).astype(dx_ref.dtype)
        dgain_ref[...] += dgain
        for g in range(4):
            dws_ref[g] += dws[g]
        dbt_ref[...] += dbt

    return pl.pallas_call(
        body, name=name,
        out_shape=[jax.ShapeDtypeStruct(dcols.shape, dcols.dtype),jax.ShapeDtypeStruct((1, 512), F32),
                   jax.ShapeDtypeStruct((4, BLK, BLK), F32), jax.ShapeDtypeStruct((BLK, BLK), F32)],
        grid=(T // BLK,),
        in_specs=[pl.BlockSpec((BLK, GM_W), lambda i: (i, jblk)),
                  _full((1, 512)), _full((4, BLK, BLK)), _full((BLK, BLK)),
                  pl.BlockSpec((BLK, 512), lambda i: (i, 0)), _ANY],
        out_specs=[pl.BlockSpec((BLK, GM_W), lambda i: (i, jblk)),
                   _full((1, 512)), _full((4, BLK, BLK)), _full((BLK, BLK))],
        input_output_aliases={5: 0},
        compiler_params=pltpu.CompilerParams(dimension_semantics=("arbitrary",)),
    )(cols, gain, ws, bt, dy, dcols)


def _sw_block(first, q, kp, kc, vp, vc, z, sinks):
    P = BLK
    lane = lax.broadcasted_iota(jnp.int32, (1, 128), 1)
    r = lax.broadcasted_iota(jnp.int32, (128, 128), 0)
    c = lax.broadcasted_iota(jnp.int32, (128, 128), 1)
    swap = (c == (r + 64) % 128).astype(F32)
    k2 = jnp.concatenate([kp, kc], axis=0)
    v2 = jnp.concatenate([vp, vc], axis=0)
    k2s = bdot("nn", k2, swap)
    v2s = bdot("nn", v2, swap)
    qi = lax.broadcasted_iota(jnp.int32, (P, 2 * P), 0)
    kj = lax.broadcasted_iota(jnp.int32, (P, 2 * P), 1)
    dist = qi + P - kj
    valid = (dist >= 0) & (dist < P) & ((kj >= P) | jnp.logical_not(first))
    outs = []
    for j in range(4):
        acc = jnp.zeros((P, 128), F32)
        for half in range(2):
            h = 2 * j + half
            kv = h // 4
            in_half = (lane >= 64 * half) & (lane < 64 * half + 64)
            qh = jnp.where(in_half, q[:, 128 * j:128 * (j + 1)], 0.0)
            same = (half == kv)
            s = bdot("nt", qh, k2 if same else k2s) * (64 ** -0.5)
            s = jnp.where(valid, s, NEG_INF)
            sink = _lane_col(sinks, h)
            m = lax.stop_gradient(jnp.maximum(jnp.max(s, axis=1, keepdims=True), sink))
            e = jnp.exp(s - m)
            p = e / (jnp.sum(e, axis=1, keepdims=True) + jnp.exp(sink - m))
            o = bdot("nn", p, v2 if same else v2s)
            acc = acc + jnp.where(in_half, o, 0.0)
        outs.append(acc)
    return jnp.concatenate(outs, axis=1) * _silu(z)


def _sw_specs(jblk, idx):
    prev = lambda i: jnp.maximum(idx(i) - 1, 0)
    jk = (jblk * SW_W + 1024) // 128
    return [pl.BlockSpec((BLK, SW_W), lambda i: (idx(i), jblk)),
            pl.BlockSpec((BLK, 128), lambda i: (prev(i), jk)),
            pl.BlockSpec((BLK, 128), lambda i: (prev(i), jk + 1)), _full((1, 128))]


def sw_forward(cols, jblk, sinks, name):
    T = cols.shape[0]

    def body(x_ref, kp_ref, vp_ref, s_ref, y_ref):
        y_ref[...] = _sw_block(pl.program_id(0) == 0, x_ref[:, 0:512], kp_ref[...], x_ref[:, 1024:1152],
                               vp_ref[...], x_ref[:, 1152:1280], x_ref[:, 512:1024], s_ref[...])

    return pl.pallas_call(
        body, name=name, out_shape=jax.ShapeDtypeStruct((T, 512), F32), grid=(T // BLK,),
        in_specs=_sw_specs(jblk, lambda i: i),
        out_specs=pl.BlockSpec((BLK, 512), lambda i: (i, 0)),
        compiler_params=pltpu.CompilerParams(dimension_semantics=("parallel",)),
    )(cols, cols, cols, sinks)


def sw_backward(cols, jblk, sinks, dy, dcols, name):
    T = cols.shape[0]
    n = T // BLK
    rev = lambda i: n - 1 - i

    def body(x_ref, kp_ref, vp_ref, s_ref, dy_ref, dcols_in, dx_ref, ds_ref, kcarry, vcarry):
        i = pl.program_id(0)

        @pl.when(i == 0)
        def _():
            kcarry[...] = jnp.zeros_like(kcarry)
            vcarry[...] = jnp.zeros_like(vcarry)
            ds_ref[...] = jnp.zeros_like(ds_ref)

        f = functools.partial(_sw_block, i == n - 1)
        _, vjp = jax.vjp(f, x_ref[:, 0:512], kp_ref[...], x_ref[:, 1024:1152], vp_ref[...], x_ref[:, 1152:1280],
                         x_ref[:, 512:1024], s_ref[...])
        dq, dkp, dkc, dvp, dvc, dz, dsk = vjp(dy_ref[...])
        dx_ref[...] = jnp.concatenate([dq, dz, dkc + kcarry[...], dvc + vcarry[...],
                                       jnp.zeros((BLK, SW_W - 1280), F32)], axis=1).astype(dx_ref.dtype)
        kcarry[...] = dkp
        vcarry[...] = dvp
        ds_ref[...] += dsk

    return pl.pallas_call(
        body, name=name,
        out_shape=[jax.ShapeDtypeStruct(dcols.shape, dcols.dtype),jax.ShapeDtypeStruct((1, 128), F32)],
        grid=(n,),
        in_specs=_sw_specs(jblk, rev) + [pl.BlockSpec((BLK, 512), lambda i: (rev(i), 0)), _ANY],
        out_specs=[pl.BlockSpec((BLK, SW_W), lambda i: (rev(i), jblk)), _full((1, 128))],
        scratch_shapes=[pltpu.VMEM((BLK, 128), F32), pltpu.VMEM((BLK, 128), F32)],
        input_output_aliases={5: 0},
        compiler_params=pltpu.CompilerParams(dimension_semantics=("arbitrary",)),
    )(cols, cols, cols, sinks, dy, dcols)


XM_TQ = 256


def _xm_block(q, z, mkv):
    outs = []
    for h in range(4):
        s = bdot("nt", q[:, 128 * h:128 * (h + 1)], mkv[:, 128 * h:128 * (h + 1)]) * (128 ** -0.5)
        m = lax.stop_gradient(jnp.max(s, axis=1, keepdims=True))
        e = jnp.exp(s - m)
        p = e / jnp.sum(e, axis=1, keepdims=True)
        outs.append(bdot("nn", p, mkv[:, 512 + 128 * h:512 + 128 * (h + 1)]))
    return jnp.concatenate(outs, axis=1) * _silu(z)


def xm_forward(cols, jblk, mkv, name):
    T = cols.shape[0]

    def body(x_ref, m_ref, y_ref):
        y_ref[...] = _xm_block(x_ref[:, 0:512], x_ref[:, 512:1024], m_ref[...])

    return pl.pallas_call(
        body, name=name, out_shape=jax.ShapeDtypeStruct((T, 512), F32), grid=(T // XM_TQ,),
        in_specs=[pl.BlockSpec((XM_TQ, XM_W), lambda i: (i, jblk)), _full(mkv.shape)],
        out_specs=pl.BlockSpec((XM_TQ, 512), lambda i: (i, 0)),
        compiler_params=pltpu.CompilerParams(dimension_semantics=("parallel",)),
    )(cols, mkv)


def xm_backward(cols, jblk, mkv, dy, dcols, name):
    T = cols.shape[0]

    def body(x_ref, m_ref, dy_ref, dcols_in, dx_ref, dm_ref):
        @pl.when(pl.program_id(0) == 0)
        def _():
            dm_ref[...] = jnp.zeros_like(dm_ref)

        _, vjp = jax.vjp(_xm_block, x_ref[:, 0:512], x_ref[:, 512:1024], m_ref[...])
        dq, dz, dm = vjp(dy_ref[...])
        dx_ref[...] = jnp.concatenate([dq, dz], axis=1).astype(dx_ref.dtype)
        dm_ref[...] += dm

    return pl.pallas_call(
        body, name=name,
        out_shape=[jax.ShapeDtypeStruct(dcols.shape, dcols.dtype),jax.ShapeDtypeStruct(mkv.shape, F32)],
        grid=(T // XM_TQ,),
        in_specs=[pl.BlockSpec((XM_TQ, XM_W), lambda i: (i, jblk)), _full(mkv.shape),
                  pl.BlockSpec((XM_TQ, 512), lambda i: (i, 0)), _ANY],
        out_specs=[pl.BlockSpec((XM_TQ, XM_W), lambda i: (i, jblk)), _full(mkv.shape)],
        input_output_aliases={3: 0},
        compiler_params=pltpu.CompilerParams(dimension_semantics=("arbitrary",)),
    )(cols, mkv, dy, dcols)


def _rms(x, gain):
    return x * lax.rsqrt(jnp.mean(x * x, axis=1, keepdims=True) + EPS) * gain


def memkv_forward(mem, gain, w, name):
    def body(m_ref, g_ref, w_ref, o_ref):
        o_ref[...] = bdot("nn", _rms(m_ref[...], g_ref[...]), w_ref[...])

    return pl.pallas_call(body, name=name, out_shape=jax.ShapeDtypeStruct(mem.shape, F32),
                          compiler_params=pltpu.CompilerParams(vmem_limit_bytes=VMEM_LIMIT))(mem, gain, w)


def memkv_backward(mem, gain, w, dkv, name):
    def body(m_ref, g_ref, w_ref, d_ref, dg_ref, dw_ref):
        mem_v = m_ref[...]
        _, vjp = jax.vjp(lambda g, ww: bdot("nn", _rms(mem_v, g), ww), g_ref[...], w_ref[...].astype(F32))
        dg, dw = vjp(d_ref[...])
        dg_ref[...] = dg
        dw_ref[...] = dw

    return pl.pallas_call(body, name=name,
                          out_shape=[jax.ShapeDtypeStruct(gain.shape, F32), jax.ShapeDtypeStruct(w.shape, F32)],
                          compiler_params=pltpu.CompilerParams(vmem_limit_bytes=VMEM_LIMIT))(mem, gain, w, dkv)


MG_TB = 256


def _merge_block(ys, gl, wup, wout, gpost):
    merged = None
    for n in range(4):
        t = _sigmoid(gl[:, 1024 * n:1024 * (n + 1)]) * bdot("nn", ys[n], wup[n])
        merged = t if merged is None else merged + t
    out = bdot("nn", merged, wout)
    return _rms(out, gpost)


def merge_forward(ys, cols, jgate, x, wup, wout, gpost, name):
    T = x.shape[0]
    TB = 256

    def body(ya, yb, yc, ym, gl_ref, x_ref, wup_ref, wout_ref, gp_ref, o_ref):
        upd = _merge_block([ya[...], yb[...], yc[...], ym[...]], gl_ref[...],
                           [wup_ref[n] for n in range(4)], wout_ref[...], gp_ref[...])
        o_ref[...] = x_ref[...] + upd

    yspec = pl.BlockSpec((TB, 512), lambda i: (i, 0))
    return pl.pallas_call(
        body, name=name, out_shape=jax.ShapeDtypeStruct((T, 1024), F32), grid=(T // TB,),
        in_specs=[yspec] * 4 + [pl.BlockSpec((TB, 4096), lambda i: (i, jgate)),
                                pl.BlockSpec((TB, 1024), lambda i: (i, 0)),
                                _full(wup.shape), _full(wout.shape), _full((1, 1024))],
        out_specs=pl.BlockSpec((TB, 1024), lambda i: (i, 0)),
        compiler_params=pltpu.CompilerParams(dimension_semantics=("parallel",), vmem_limit_bytes=VMEM_LIMIT),
    )(*ys, cols, x, wup, wout, gpost)


def _token_product(a, b, name):
    (T, m), n = a.shape, b.shape[1]

    def body(a_ref, b_ref, o_ref):
        o_ref[...] = lax.dot_general(a_ref[...].astype(BF), b_ref[...].astype(BF), _DIMS["tn"], preferred_element_type=F32)

    return pl.pallas_call(body, name=name, out_shape=jax.ShapeDtypeStruct((m, n), F32),
                          compiler_params=pltpu.CompilerParams(vmem_limit_bytes=VMEM_LIMIT))(a, b)


def merge_backward(ys, cols, jgate, wup, wout, gpost, dx, name):
    T = dx.shape[0]
    TB = MG_TB

    def body(ya, yb, yc, ym, gl_ref, wup_ref, wout_ref, gp_ref, dx_ref,
             dgl_ref, dya, dyb, dyc, dym, dpa, dpb, dpc, dpm, merged_ref, dout_ref, dgp_ref):
        @pl.when(pl.program_id(0) == 0)
        def _():
            dgp_ref[...] = jnp.zeros_like(dgp_ref)

        y_refs = (ya, yb, yc, ym)
        gates = [_sigmoid(gl_ref[:, 1024 * n:1024 * (n + 1)]) for n in range(4)]
        projs = [bdot("nn", y_refs[n][...], wup_ref[n]) for n in range(4)]
        merged = gates[0] * projs[0] + gates[1] * projs[1] + gates[2] * projs[2] + gates[3] * projs[3]
        out = bdot("nn", merged, wout_ref[...])
        _, vjp = jax.vjp(_rms, out, gp_ref[...])
        dout, dgp = vjp(dx_ref[...])
        dmerged = bdot("nt", dout, wout_ref[...])
        for n, (dy_ref, dp_ref) in enumerate(zip((dya, dyb, dyc, dym), (dpa, dpb, dpc, dpm))):
            dproj = dmerged * gates[n]
            dgl_ref[:, 1024 * n:1024 * (n + 1)] = (dmerged * projs[n] * gates[n] * (1.0 - gates[n])).astype(dgl_ref.dtype)
            dy_ref[...] = bdot("nt", dproj, wup_ref[n])
            dp_ref[...] = dproj.astype(BF)
        merged_ref[...] = merged.astype(BF)
        dout_ref[...] = dout.astype(BF)
        dgp_ref[...] += dgp

    yspec = pl.BlockSpec((TB, 512), lambda i: (i, 0))
    dspec = pl.BlockSpec((TB, 1024), lambda i: (i, 0))
    dcols, dya, dyb, dyc, dym, *dproj, merged, dout, dgp = pl.pallas_call(
        body, name=name,
        out_shape=[jax.ShapeDtypeStruct(cols.shape, BF)] + [jax.ShapeDtypeStruct((T, 512), F32)] * 4 + [
            jax.ShapeDtypeStruct((T, 1024), BF)] * 6 + [jax.ShapeDtypeStruct((1, 1024), F32)],
        grid=(T // TB,),
        in_specs=[yspec] * 4 + [pl.BlockSpec((TB, 4096), lambda i: (i, jgate)),
                                _full(wup.shape), _full(wout.shape), _full((1, 1024)), dspec],
        out_specs=[pl.BlockSpec((TB, 4096), lambda i: (i, jgate))] + [yspec] * 4 + [
            dspec] * 6 + [_full((1, 1024))],
        compiler_params=pltpu.CompilerParams(dimension_semantics=("arbitrary",), vmem_limit_bytes=VMEM_LIMIT),
    )(*ys, cols, wup, wout, gpost, dx)
    dwup = jnp.stack([_token_product(ys[n], dproj[n], "%s_w_up%d" % (name, n)) for n in range(4)])
    dwout = _token_product(merged, dout, name + "_w_out")
    return dcols, dya, dyb, dyc, dym, dwup, dwout, dgp


NB = 256


def prenorm_forward(x, gain, name):
    T, D = x.shape

    def body(x_ref, g_ref, o_ref, ot_ref):
        h = _rms(x_ref[...], g_ref[...])
        o_ref[...] = h.astype(BF)
        ot_ref[...] = h.T.astype(BF)

    return pl.pallas_call(
        body, name=name,
        out_shape=[jax.ShapeDtypeStruct((T, D), BF), jax.ShapeDtypeStruct((D, T), BF)], grid=(T // NB,),
        in_specs=[pl.BlockSpec((NB, D), lambda i: (i, 0)), _full((1, D))],
        out_specs=[pl.BlockSpec((NB, D), lambda i: (i, 0)), pl.BlockSpec((D, NB), lambda i: (0, i))],
        compiler_params=pltpu.CompilerParams(dimension_semantics=("parallel",)),
    )(x, gain)


def prenorm_backward(x, gain, dh, dres, name):
    T = x.shape[0]

    def body(x_ref, g_ref, dh_ref, dr_ref, dx_ref, dg_ref):
        @pl.when(pl.program_id(0) == 0)
        def _():
            dg_ref[...] = jnp.zeros_like(dg_ref)

        _, vjp = jax.vjp(_rms, x_ref[...], g_ref[...])
        dxn, dg = vjp(dh_ref[...])
        dx_ref[...] = dr_ref[...] + dxn
        dg_ref[...] += dg

    spec = pl.BlockSpec((NB, 1024), lambda i: (i, 0))
    return pl.pallas_call(
        body, name=name,
        out_shape=[jax.ShapeDtypeStruct(x.shape, F32), jax.ShapeDtypeStruct((1, 1024), F32)], grid=(T // NB,),
        in_specs=[spec, _full((1, 1024)), spec, spec], out_specs=[spec, _full((1, 1024))],
        compiler_params=pltpu.CompilerParams(dimension_semantics=("arbitrary",)),
    )(x, gain, dh, dres)


def loss_head(y, target, name):
    T, D = y.shape

    def body(y_ref, t_ref, l_ref, d_ref):
        @pl.when(pl.program_id(0) == 0)
        def _():
            l_ref[...] = jnp.zeros_like(l_ref)

        err = y_ref[...] - t_ref[...]
        d_ref[...] = err * (1.0 / D)
        l_ref[...] += jnp.full(l_ref.shape, 0.5 * jnp.sum(jnp.mean(err * err, axis=1, keepdims=True)), F32)

    spec = pl.BlockSpec((NB, D), lambda i: (i, 0))
    return pl.pallas_call(
        body, name=name,
        out_shape=[jax.ShapeDtypeStruct((1, 128), F32), jax.ShapeDtypeStruct(y.shape, F32)], grid=(T // NB,),
        in_specs=[spec, spec], out_specs=[_full((1, 128)), spec],
        compiler_params=pltpu.CompilerParams(dimension_semantics=("arbitrary",)),
    )(y, target)


JB_GATE, JB_XM, JB_DN, JB_SW, JB_GM = 0, 4, 2, 5, 6
_ALIGNED_PIECES = ((5896, 4096), (4872, 512), (5384, 512), (0, 2048), (2048, 8), 504, (3592, 512), (4360, 512),
                   (4104, 128), (4232, 128), 256, (2056, 1024), (3080, 512))
_NATURAL_FROM_ALIGNED = ((5120, 2048), (7168, 8), (9216, 1024), (10240, 512), (7680, 512), (8704, 128), (8832, 128),
                         (8192, 512), (4096, 512), (4608, 512), (0, 4096))


def _natural_range(slots, start, width):
    out = []
    while width > 0:
        j, i = divmod(start, W_IN_SHARD)
        take = min(width, W_IN_SHARD - i)
        out.append(slots[j, :, i:i + take])
        start, width = start + take, width - take
    return out


def _aligned_w_in(slots):
    parts = []
    for piece in _ALIGNED_PIECES:
        if isinstance(piece, int):
            parts.append(jnp.zeros(slots.shape[1:2] + (piece,), slots.dtype))
        else:
            parts += _natural_range(slots, *piece)
    return jnp.concatenate(parts, axis=-1)


def _slots_of_aligned(d_al):
    slots = []
    for s in range(N_DEV):
        lo, hi = s * W_IN_SHARD, (s + 1) * W_IN_SHARD
        parts, nat = [], 0
        for a_start, width in _NATURAL_FROM_ALIGNED:
            b, e = max(lo, nat), min(hi, nat + width)
            if b < e:
                parts.append(d_al[..., a_start + b - nat:a_start + e - nat])
            nat += width
        parts.append(jnp.zeros(d_al.shape[:1] + (W_IN_SHARD_PAD - W_IN_SHARD,), d_al.dtype))
        slots.append(jnp.concatenate(parts, axis=-1))
    return jnp.stack(slots)


SMALL_VEC_W = 1024


def _pack_small(parts):
    rows = []
    for p in parts:
        flat = p.reshape(-1).astype(F32)
        r = -(-flat.shape[0] // SMALL_VEC_W)
        rows.append(jnp.pad(flat, (0, r * SMALL_VEC_W - flat.shape[0])).reshape(r, SMALL_VEC_W))
    vec = jnp.concatenate(rows, axis=0)
    return jnp.pad(vec, ((0, -vec.shape[0] % 8), (0, 0)))


def _unpack_small(vec, shapes):
    out, off = [], 0
    for s in shapes:
        n = math.prod(s)
        r = -(-n // SMALL_VEC_W)
        out.append(vec[off:off + r].reshape(-1)[:n].reshape(s))
        off += r
    return out


def _lanes(vec, at):
    return jnp.zeros((1, 128), F32).at[0, at:at + vec.shape[0]].set(vec)


SMALL_NAMES = ("norm_pre", "norm_post", "norm_mem", "a_log", "dt_bias", "dn_norm", "gm_norm",
               "spatial_w", "spatial_b", "sinks")


def _other_weights(s_mem, s_up, s_out):
    return (s_mem.reshape(D_MODEL, 2 * BRANCH_W),
            jnp.transpose(s_up, (1, 2, 0, 3)).reshape(N_BRANCH, BRANCH_W, D_MODEL), s_out.reshape(D_MODEL, D_MODEL))


def _grad_slots(d_in_al, d_mem, d_up, d_out):
    return [None if d_in_al is None else _slots_of_aligned(d_in_al), d_mem.astype(BF).reshape(N_DEV, 128, 2 * BRANCH_W),
            jnp.transpose(d_up.astype(BF).reshape(N_BRANCH, BRANCH_W, N_DEV, 128), (2, 0, 1, 3)),
            d_out.astype(BF).reshape(N_DEV, 128, D_MODEL)]


def _layer_params(l, small, conv_full, token):
    return dict(
        gpre=small["norm_pre"][l][None] + token, gpost=small["norm_post"][l][None], gmem=small["norm_mem"][l][None],
        cw=conv_full[l], al=_lanes(small["a_log"][l], 4), dt=_lanes(small["dt_bias"][l], 4),
        dnn=small["dn_norm"][l][None], gain=small["gm_norm"][l][None], ws=small["spatial_w"][l],
        bt=jnp.zeros((128, 128), F32).at[:, :GM_GROUPS].set(small["spatial_b"][l].T),
        sinks=_lanes(small["sinks"][l], 0))


def _layer_forward(l, xl, mem, p, w_in_al, other_weights):
    t = "l%d_" % l
    h, h_t = prenorm_forward(xl, p["gpre"], t + "prenorm")
    cols = _matmul(h, w_in_al, "nn", F32, (1024, 1536, 1024), t + "w_in")
    ya, ss, ts = dn_forward(cols, JB_DN, p["cw"], p["al"], p["dt"], p["dnn"], t + "deltanet")
    yb = gm_forward(cols, JB_GM, p["gain"], p["ws"], p["bt"], t + "gmlp")
    yc = sw_forward(cols, JB_SW, p["sinks"], t + "swa")
    w_mem, w_up, w_out = other_weights(yc)
    mkv = memkv_forward(mem, p["gmem"], w_mem, t + "memkv")
    ym = xm_forward(cols, JB_XM, mkv, t + "memattn")
    xn = merge_forward([ya, yb, yc, ym], cols, JB_GATE, xl, w_up, w_out, p["gpost"], t + "merge")
    return xn, dict(p, x=xl, h_t=h_t, cols=cols, mkv=mkv, ss=ss, ts=ts, ys=[ya, yb, yc, ym]), (w_in_al, w_mem, w_up, w_out)


def _layer_backward(l, s, mem, weights, dx, token, early=None):
    w_in_al, w_mem, w_up, w_out = weights
    t = "l%d_" % l
    cols = s["cols"]
    dcols, dya, dyb, dyc, dym, dwup, dwout, dgpost = merge_backward(
        s["ys"], cols, JB_GATE, w_up, w_out, s["gpost"] + token, dx, t + "merge_bwd")
    dcols, dmkv = xm_backward(cols, JB_XM, s["mkv"], dym, dcols, t + "memattn_bwd")
    dgmem, dwmem = memkv_backward(mem, s["gmem"], w_mem, dmkv, t + "memkv_bwd")
    sinks = s["sinks"] if early is None else s["sinks"] + early(dwmem, dwup, dwout)
    dcols, dsinks = sw_backward(cols, JB_SW, sinks, dyc, dcols, t + "swa_bwd")
    dcols, dgain, dws, dbt = gm_backward(cols, JB_GM, s["gain"], s["ws"], s["bt"], dyb, dcols, t + "gmlp_bwd")
    dcols, dcw, dal, ddt, ddn = dn_backward(
        cols, JB_DN, s["cw"], s["al"], s["dt"], s["dnn"], s["ss"], s["ts"], dya, dcols, t + "deltanet_bwd")
    dh = _matmul(dcols, w_in_al, "nt", F32, (1024, 1024, 3584), t + "w_in_bwd_x")
    dwin = _matmul(s["h_t"], dcols, "nn", BF, (1024, 1536, 2048), t + "w_in_bwd_w")
    dx, dgpre = prenorm_backward(s["x"], s["gpre"], dh, dx, t + "prenorm_bwd")
    gsmall = dict(norm_pre=dgpre[0], norm_post=dgpost[0], norm_mem=dgmem[0], a_log=dal[0, 4:8], dt_bias=ddt[0, 4:8],
                  dn_norm=ddn[0], gm_norm=dgain[0], spatial_w=dws, spatial_b=dbt[:, :GM_GROUPS].T,
                  sinks=dsinks[0, :SW_HEADS], conv_w=dcw)
    return dx, gsmall, (dwin, dwmem, dwup, dwout)


def kernel(x, mem, norm_pre, norm_post, norm_mem, w_in, conv_w, a_log, dt_bias, dn_norm, gm_norm, spatial_w, spatial_b, sinks, w_mem_kv, w_up, w_out, loss_target, m_norm_pre, m_norm_post, m_norm_mem, m_w_in, m_conv_w, m_a_log, m_dt_bias, m_dn_norm, m_gm_norm, m_spatial_w, m_spatial_b, m_sinks, m_w_mem_kv, m_w_up, m_w_out, v_norm_pre, v_norm_post, v_norm_mem, v_w_in, v_conv_w, v_a_log, v_dt_bias, v_dn_norm, v_gm_norm, v_spatial_w, v_spatial_b, v_sinks, v_w_mem_kv, v_w_up, v_w_out):
    xi, yi, ci = _my_place()
    my_slot = 4 * xi + 2 * yi + ci
    conv_shard = conv_w.shape[-1]
    x2, mem2, target = x[0], mem[0], loss_target[0]

    w_in_pad = jnp.pad(w_in.astype(BF), ((0, 0), (0, 0), (0, W_IN_SHARD_PAD - W_IN_SHARD)))
    shards = [[w_in_pad[l], w_mem_kv[l].astype(BF), w_up[l].astype(BF), w_out[l].astype(BF)] for l in range(DEPTH)]
    w_in_slots0, w_up_slots0, w_out_slots0, conv_slots = _all_gather_slots(
        [shards[0][0], shards[0][2], shards[0][3], conv_w], "gather_weights_l0")
    ag = list(_spread_start([shards[0][1]] + shards[1], "gather", "gather_weights_rest_start"))
    conv_full = jnp.transpose(conv_slots, (1, 2, 0, 3)).reshape(DEPTH, CONV_W, N_DEV * conv_shard)
    small = dict(norm_pre=norm_pre, norm_post=norm_post, norm_mem=norm_mem, a_log=a_log,
                 dt_bias=dt_bias, dn_norm=dn_norm, gm_norm=gm_norm, spatial_w=spatial_w,
                 spatial_b=spatial_b, sinks=sinks)

    def arrived(which, after, name):
        ag[2], ag[3] = _spread_wait(ag[0], ag[1], ag[2], ag[3], which, after, name)
        return [ag[3][a] for a in which]

    x1, saved0, weights0 = _layer_forward(
        0, x2, mem2, _layer_params(0, small, conv_full, ag[4][0, 0]), _aligned_w_in(w_in_slots0),
        lambda y: _other_weights(*arrived([0], y, "gather_weights_l0_w_mem_wait"), w_up_slots0, w_out_slots0))
    w_in_slots1, = arrived([1], x1, "gather_weights_l1_w_in_wait")
    x_out, saved1, weights1 = _layer_forward(
        1, x1, mem2, _layer_params(1, small, conv_full, 0.0), _aligned_w_in(w_in_slots1),
        lambda y: _other_weights(*arrived([2, 3, 4], y, "gather_weights_l1_rest_wait")))
    loss, dx = loss_head(x_out, target, "loss_head")

    packed_names = SMALL_NAMES + ("conv_w",)
    dx, gsmall1, gbig1 = _layer_backward(1, saved1, mem2, weights1, dx, 0.0)
    small1 = [loss[0, :1]] + [gsmall1[n] for n in packed_names]
    sm1 = _spread_start([_pack_small(small1)], "gather", "gather_small_grads_l1_start")
    rs_send, rs_recv, rs_src, rs_land, rs_token = _spread_start(_grad_slots(*gbig1), "scatter", "exchange_grads_l1_start")
    rest0 = []

    def send_rest0(dwmem, dwup, dwout):
        rest0.extend(_spread_start(_grad_slots(None, dwmem, dwup, dwout)[1:], "scatter", "exchange_grads_l0_rest_start"))
        return rest0[4][0, 0]

    dx, gsmall0, gbig0 = _layer_backward(0, saved0, mem2, weights0, dx, rs_token[0, 0] + sm1[4][0, 0], send_rest0)
    _, parts1 = _spread_wait(rs_send, rs_recv, rs_src, rs_land, range(4), dx, "exchange_grads_l1_wait")

    small0 = [gsmall0[n] for n in packed_names]
    sm0 = _spread_start([_pack_small(small0)], "gather", "gather_small_grads_l0_start")

    g_win0 = _slots_of_aligned(gbig0[0])
    g_win0 = g_win0.reshape((N_DEV // 2, 2) + g_win0.shape[1:])
    theirs, = _exchange_sibling([g_win0], "exchange_sibling_l0")
    chip_sum = _pair_sum(g_win0, theirs, "pair_sum_l0")
    ch_send, ch_recv, ch_src, ch_land, ch_token = _spread_start([chip_sum], "chips", "exchange_chips_l0_start")

    _, (land1,) = _spread_wait(*sm1[:4], [0], ch_token, "gather_small_grads_l1_wait")
    _, (land0,) = _spread_wait(*sm0[:4], [0], land1, "gather_small_grads_l0_wait")
    tot1 = _unpack_small(_sum_slots(land1, "sum_small_grads_l1"), [p.shape for p in small1])
    tot0 = _unpack_small(_sum_slots(land0, "sum_small_grads_l0"), [p.shape for p in small0])
    loss_tot = tot1[0][0]
    grads = {n: jnp.stack([g0, g1]) for n, g0, g1 in zip(packed_names, tot0, tot1[1:])}
    grads["conv_w"] = lax.dynamic_slice_in_dim(grads["conv_w"], my_slot * conv_shard, conv_shard, axis=2)

    given = dict(norm_pre=(norm_pre, m_norm_pre, v_norm_pre), norm_post=(norm_post, m_norm_post, v_norm_post),
                 norm_mem=(norm_mem, m_norm_mem, v_norm_mem), a_log=(a_log, m_a_log, v_a_log),
                 dt_bias=(dt_bias, m_dt_bias, v_dt_bias), dn_norm=(dn_norm, m_dn_norm, v_dn_norm),
                 gm_norm=(gm_norm, m_gm_norm, v_gm_norm), spatial_w=(spatial_w, m_spatial_w, v_spatial_w),
                 spatial_b=(spatial_b, m_spatial_b, v_spatial_b), sinks=(sinks, m_sinks, v_sinks),
                 conv_w=(conv_w, m_conv_w, v_conv_w))
    pshapes = [given[n][0].shape for n in packed_names]
    pw, pm, pv = (_pack_small([given[n][i] for n in packed_names]) for i in range(3))
    pd, pnm, pnv = _adamw(pw + ch_token[0, 0], _pack_small([grads[n] for n in packed_names]), pm, pv, "adamw_small")
    upd = {n: t for n, t in zip(packed_names, zip(_unpack_small(pd, pshapes), _unpack_small(pnm, pshapes),
                                                  _unpack_small(pnv, pshapes)))}
    big = (("w_mem_kv", (w_mem_kv, m_w_mem_kv, v_w_mem_kv)), ("w_up", (w_up, m_w_up, v_w_up)),
           ("w_out", (w_out, m_w_out, v_w_out)))
    first = [_sum_adamw(parts1[1 + i], w, m, v, 1, None, ch_token, "adamw_%s_l1" % name)
             for i, (name, (w, m, v)) in enumerate(big)]
    _, parts0_rest = _spread_wait(*rest0[:4], range(3), first[-1][0], "exchange_grads_l0_rest_wait")
    for i, (name, (w, m, v)) in enumerate(big):
        g, d, nm, nv = _sum_adamw(parts0_rest[i], w, m, v, 0, first[i], None, "adamw_%s_l0" % name)
        grads[name], upd[name] = g, (d, nm, nv)
    _, (parts0_w_in,) = _spread_wait(ch_send, ch_recv, ch_src, ch_land, [0], upd["w_out"][0], "exchange_chips_l0_wait")
    w_in_t, m_w_in_t, v_w_in_t = (jnp.transpose(t, (2, 0, 1)) for t in (w_in, m_w_in, v_w_in))
    g, d, nm, nv = (jnp.transpose(t, (1, 2, 0)) for t in
                    _sum_adamw_t([parts0_w_in, parts1[0]], w_in_t, m_w_in_t, v_w_in_t, "adamw_w_in"))
    grads["w_in"], upd["w_in"] = g, (d, nm, nv)

    order = ("norm_pre", "norm_post", "norm_mem", "w_in", "conv_w", "a_log", "dt_bias", "dn_norm",
             "gm_norm", "spatial_w", "spatial_b", "sinks", "w_mem_kv", "w_up", "w_out")
    return (loss_tot, dx[None], *[grads[n] for n in order], *[upd[n][0] for n in order],
            *[upd[n][1] for n in order], *[upd[n][2] for n in order])
```

```python
import functools
import math

import jax
import jax.numpy as jnp
from jax import lax
from jax.experimental import pallas as pl
from jax.experimental.pallas import tpu as pltpu

MESH = pl.DeviceIdType.MESH
N_DEV = 8

D_MODEL = 1024
DEPTH = 2
N_BRANCH = 4
BRANCH_W = 512
DN_HEADS = 4
CONV_W = 4
GM_GROUPS = 4
SW_HEADS = 8
EPS = 1e-6
NEG_INF = -1e30

D_IN = 9992
W_IN_SHARD = D_IN // N_DEV
W_IN_SHARD_PAD = 1280
D_IN_AL = 10752
DN_W, SW_W, GM_W, XM_W = 2560, 1536, 1536, 1024

ADAM_LR = 0.001
ADAM_B1 = 0.9
ADAM_B2 = 0.999
ADAM_EPS = 1e-08
ADAM_WD = 0.01
ADAM_STEP = 10

VMEM_LIMIT = 56 * 1024 * 1024

BF = jnp.bfloat16
F32 = jnp.float32
DN_C = 128
DN_D = 128
HALO = 8
BLK = 128


def _my_place():
    return lax.axis_index("x"), lax.axis_index("y"), lax.axis_index("c")


_ANY = pl.BlockSpec(memory_space=pl.ANY)


def _all_gather_slots(parts, name):
    n = len(parts)

    def body(*refs):
        p_refs, out_refs = refs[:n], refs[n:2 * n]
        send_sems, recv_sems, local_sems = refs[2 * n:]
        x, y, c = _my_place()
        me, sibling = (x, y, c), (x, y, 1 - c)
        chips = [(1 - x, y), (x, 1 - y), (1 - x, 1 - y)]

        def copy(a, k, block, to, src=None):
            px, py, pc = block
            slot = out_refs[a].at[4 * px + 2 * py + pc]
            return pltpu.make_async_remote_copy(
                src_ref=slot if src is None else src, dst_ref=slot,
                send_sem=send_sems.at[7 * a + k], recv_sem=recv_sems.at[7 * a + k],
                device_id=to, device_id_type=MESH)

        mine = [pltpu.make_async_copy(p_refs[a], out_refs[a].at[4 * x + 2 * y + c], local_sems.at[a])
                for a in range(n)]
        for cp in mine:
            cp.start()
        first = []
        for a in range(n):
            first.append(copy(a, 0, me, sibling, src=p_refs[a]))
            first += [copy(a, 1 + j, me, (*chip, c), src=p_refs[a]) for j, chip in enumerate(chips)]
        for cp in first:
            cp.start()
        passed = []
        for j, chip in enumerate(chips):
            for a in range(n):
                copy(a, 1 + j, (*chip, c), me).wait_recv()
                fwd = copy(a, 4 + j, (*chip, c), sibling)
                fwd.start()
                passed.append(fwd)
        for a in range(n):
            copy(a, 0, sibling, me).wait_recv()
            for j, chip in enumerate(chips):
                copy(a, 4 + j, (*chip, 1 - c), me).wait_recv()
        for cp in first + passed:
            cp.wait_send()
        for cp in mine:
            cp.wait()

    return pl.pallas_call(
        body, name=name,
        out_shape=[jax.ShapeDtypeStruct((N_DEV,) + p.shape, p.dtype) for p in parts],
        in_specs=[_ANY] * n, out_specs=[_ANY] * n,
        scratch_shapes=[pltpu.SemaphoreType.DMA((7 * n,)), pltpu.SemaphoreType.DMA((7 * n,)),
                        pltpu.SemaphoreType.DMA((n,))],
    )(*parts)


def _exchange_sibling(parts, name):
    n = len(parts)

    def body(*refs):
        g_refs, out_refs = refs[:n], refs[n:2 * n]
        send_sems, recv_sems = refs[2 * n:]
        x, y, c = _my_place()
        copies = [pltpu.make_async_remote_copy(
            src_ref=g_refs[a].at[:, 1 - c], dst_ref=out_refs[a],
            send_sem=send_sems.at[a], recv_sem=recv_sems.at[a],
            device_id=(x, y, 1 - c), device_id_type=MESH) for a in range(n)]
        for cp in copies:
            cp.start()
        for cp in copies:
            cp.wait()

    return pl.pallas_call(
        body, name=name,
        out_shape=[jax.ShapeDtypeStruct((4,) + g.shape[2:], g.dtype) for g in parts],
        in_specs=[_ANY] * n, out_specs=[_ANY] * n,
        scratch_shapes=[pltpu.SemaphoreType.DMA((n,)), pltpu.SemaphoreType.DMA((n,))],
    )(*parts)


_HBM = pl.BlockSpec(memory_space=pltpu.HBM)
_SEM = pl.BlockSpec(memory_space=pltpu.SEMAPHORE)
_EFFECT = pltpu.SideEffectType.DATAFLOW_SIDE_EFFECTING


def _peer(x, y, c, k):
    return (1 - x if (k >> 2) & 1 else x, 1 - y if (k >> 1) & 1 else y, 1 - c if k & 1 else c)


def _spread_start(srcs, mode, name):
    n = len(srcs)
    lands = [lax.empty((N_DEV,) + s.shape if mode == "gather" else s.shape, s.dtype) for s in srcs]
    peers = range(0, N_DEV, 2) if mode == "chips" else range(N_DEV)

    def body(*refs):
        src_refs, land_refs = refs[:n], refs[n:2 * n]
        send_sems, recv_sems = refs[2 * n:2 * n + 2]
        token = refs[-1]
        x, y, c = _my_place()
        for a in range(n):
            for k in peers:
                px, py, pc = _peer(x, y, c, k)
                if mode == "chips":
                    src, mine = src_refs[a].at[2 * px + py], 2 * x + y
                else:
                    src = src_refs[a].at[4 * px + 2 * py + pc] if mode == "scatter" else src_refs[a]
                    mine = 4 * x + 2 * y + c
                pltpu.make_async_remote_copy(
                    src_ref=src, dst_ref=land_refs[a].at[mine],
                    send_sem=send_sems.at[a], recv_sem=recv_sems.at[a],
                    device_id=(px, py, pc), device_id_type=MESH).start()
        token[...] = jnp.zeros_like(token)

    out = pl.pallas_call(
        body, name=name,
        out_shape=[pltpu.SemaphoreType.DMA((n,)), pltpu.SemaphoreType.DMA((n,))]
        + [pltpu.HBM(s.shape, s.dtype) for s in srcs] + [pltpu.HBM(l.shape, l.dtype) for l in lands]
        + [jax.ShapeDtypeStruct((8, 128), F32)],
        in_specs=[_HBM] * (2 * n),
        out_specs=[_SEM, _SEM] + [_HBM] * (2 * n) + [pl.BlockSpec(memory_space=pltpu.VMEM)],
        input_output_aliases={i: 2 + i for i in range(2 * n)},
        compiler_params=pltpu.CompilerParams(has_side_effects=_EFFECT),
    )(*[pltpu.with_memory_space_constraint(s, pltpu.HBM) for s in srcs],
      *[pltpu.with_memory_space_constraint(l, pltpu.HBM) for l in lands])
    return out[0], out[1], out[2:2 + n], out[2 + n:2 + 2 * n], out[-1]


def _spread_wait(send_sems, recv_sems, srcs, lands, which, after, name):
    n = len(srcs)

    def body(*refs):
        land_refs = refs[n:2 * n]
        send_sems, recv_sems = refs[2 * n:2 * n + 2]
        x, y, c = _my_place()
        for a in which:
            whole = pltpu.make_async_remote_copy(
                src_ref=land_refs[a], dst_ref=land_refs[a],
                send_sem=send_sems.at[a], recv_sem=recv_sems.at[a],
                device_id=(x, y, c), device_id_type=MESH)
            whole.wait_send()
            whole.wait_recv()

    out = pl.pallas_call(
        body, name=name,
        out_shape=[pltpu.HBM(s.shape, s.dtype) for s in srcs] + [pltpu.HBM(l.shape, l.dtype) for l in lands],
        in_specs=[_HBM] * (2 * n) + [_SEM, _SEM, _ANY],
        out_specs=[_HBM] * (2 * n),
        input_output_aliases={i: i for i in range(2 * n)},
        compiler_params=pltpu.CompilerParams(has_side_effects=_EFFECT),
    )(*srcs, *lands, send_sems, recv_sems, after)
    return out[:n], out[n:]


def _sum_slots(parts, name):
    def body(p_ref, o_ref):
        acc = p_ref[0]
        for s in range(1, parts.shape[0]):
            acc = acc + p_ref[s]
        o_ref[...] = acc

    return pl.pallas_call(body, name=name, out_shape=jax.ShapeDtypeStruct(parts.shape[1:], parts.dtype))(parts)


def _pick(n, pref):
    if n <= pref:
        return n
    t = pref - pref % 128
    while t > 0 and n % t:
        t -= 128
    return t if t > 0 else n


_DIMS = {"nn": (((1,), (0,)), ((), ())),
         "nt": (((1,), (1,)), ((), ())),
         "tn": (((0,), (0,)), ((), ()))}


def _matmul(a, b, mode, out_dtype, tiles, name):
    (m, k) = a.shape
    n = b.shape[1] if mode == "nn" else b.shape[0]
    tm, tn, tk = (_pick(d, t) for d, t in zip((m, n, k), tiles))
    nk = k // tk

    def product(a_ref, b_ref):
        return lax.dot_general(a_ref[...].astype(BF), b_ref[...].astype(BF), _DIMS[mode], preferred_element_type=F32)

    def body_whole_k(a_ref, b_ref, o_ref):
        o_ref[...] = product(a_ref, b_ref).astype(o_ref.dtype)

    def body_split_k(a_ref, b_ref, o_ref, acc_ref):
        kk = pl.program_id(2)

        @pl.when(kk == 0)
        def _():
            acc_ref[...] = jnp.zeros_like(acc_ref)

        acc_ref[...] += product(a_ref, b_ref)

        @pl.when(kk == nk - 1)
        def _():
            o_ref[...] = acc_ref[...].astype(o_ref.dtype)

    b_spec = (pl.BlockSpec((tn, tk), lambda i, j, kk: (j, kk)) if mode == "nt"
              else pl.BlockSpec((tk, tn), lambda i, j, kk: (kk, j)))
    return pl.pallas_call(
        body_whole_k if nk == 1 else body_split_k, name=name,
        out_shape=jax.ShapeDtypeStruct((m, n), out_dtype),
        grid=(m // tm, n // tn, nk),
        in_specs=[pl.BlockSpec((tm, tk), lambda i, j, kk: (i, kk)), b_spec],
        out_specs=pl.BlockSpec((tm, tn), lambda i, j, kk: (i, j)),
        scratch_shapes=[] if nk == 1 else [pltpu.VMEM((tm, tn), F32)],
        compiler_params=pltpu.CompilerParams(
            dimension_semantics=("parallel", "parallel", "arbitrary"),
            vmem_limit_bytes=VMEM_LIMIT),
    )(a, b)


def _rows2d(t, lead):
    return t.reshape(t.shape[:lead] + (math.prod(t.shape[lead:-1]), t.shape[-1]))


def _pair_sum(g, theirs, name):
    g3, t3 = _rows2d(g, 2), _rows2d(theirs, 1)
    _, r, w = t3.shape
    tr = _pick(r, 512)

    def body(g_ref, t_ref, o_ref):
        c = lax.axis_index("c")
        mine = jnp.where(c == 0, g_ref[0, 0], g_ref[0, 1])
        o_ref[0] = (mine.astype(F32) + t_ref[0].astype(F32)).astype(o_ref.dtype)

    out = pl.pallas_call(
        body, name=name,
        out_shape=jax.ShapeDtypeStruct(t3.shape, t3.dtype),
        grid=(4, r // tr),
        in_specs=[pl.BlockSpec((1, 2, tr, w), lambda q, i: (q, 0, i, 0)),
                  pl.BlockSpec((1, tr, w), lambda q, i: (q, i, 0))],
        out_specs=pl.BlockSpec((1, tr, w), lambda q, i: (q, i, 0)),
        compiler_params=pltpu.CompilerParams(dimension_semantics=("parallel", "parallel")),
    )(g3, t3)
    return out.reshape(theirs.shape)


def _adam_update(w, g, m, v):
    c1 = 1.0 - ADAM_B1 ** ADAM_STEP
    c2 = 1.0 - ADAM_B2 ** ADAM_STEP
    nm = ADAM_B1 * m + (1.0 - ADAM_B1) * g
    nv = ADAM_B2 * v + (1.0 - ADAM_B2) * (g * g)
    delta = -ADAM_LR * ((nm / c1) / (jnp.sqrt(nv / c2) + ADAM_EPS) + ADAM_WD * w)
    return delta, nm, nv


def _sum_adamw(parts, w, m, v, layer, carry, after, name):
    shape = w.shape
    cols = shape[-1]
    p3 = _rows2d(parts, 1)
    w3, m3, v3 = (_rows2d(t, 1) for t in (w, m, v))
    rows = w3.shape[1]
    tr = _pick(rows, 128)
    n_parts = p3.shape[0]

    def body(p_ref, w_ref, m_ref, v_ref, *rest):
        g_ref, d_ref, nm_ref, nv_ref = rest[-4:]
        g = p_ref[0, :, :cols].astype(F32)
        for q in range(1, n_parts):
            g = g + p_ref[q, :, :cols].astype(F32)
        d, nm, nv = _adam_update(w_ref[0], g, m_ref[0], v_ref[0])
        g_ref[0] = g
        d_ref[0] = d
        nm_ref[0] = nm
        nv_ref[0] = nv

    spec = pl.BlockSpec((1, tr, cols), lambda i: (layer, i, 0))
    extra = [] if carry is None else [_rows2d(t, 1) for t in carry]
    tail = [] if after is None else [after]
    out = pl.pallas_call(
        body, name=name,
        out_shape=[jax.ShapeDtypeStruct(w3.shape, F32)] * 4,
        grid=(rows // tr,),
        in_specs=[pl.BlockSpec((n_parts, tr, p3.shape[-1]), lambda i: (0, i, 0)), spec, spec, spec] + [_ANY] * len(extra + tail),
        out_specs=[spec] * 4,
        input_output_aliases={4 + i: i for i in range(len(extra))},
        compiler_params=pltpu.CompilerParams(dimension_semantics=("parallel",)),
    )(p3, w3, m3, v3, *extra, *tail)
    return tuple(t.reshape(shape) for t in out)


def _sum_adamw_t(parts, w, m, v, name):
    rows = parts[0].shape[2]
    tr = 128
    assert rows % tr == 0 and rows >= w.shape[0]

    def body(*refs):
        p_refs, (w_ref, m_ref, v_ref), (g_ref, d_ref, nm_ref, nv_ref) = refs[:DEPTH], refs[DEPTH:DEPTH + 3], refs[DEPTH + 3:]
        for l in range(DEPTH):
            g = p_refs[l][0].astype(F32)
            for q in range(1, p_refs[l].shape[0]):
                g = g + p_refs[l][q].astype(F32)
            g = g.T
            d, nm, nv = _adam_update(w_ref[:, l, :], g, m_ref[:, l, :], v_ref[:, l, :])
            g_ref[:, l, :] = g
            d_ref[:, l, :] = d
            nm_ref[:, l, :] = nm
            nv_ref[:, l, :] = nv

    spec = pl.BlockSpec((tr,) + w.shape[1:], lambda i: (i, 0, 0))
    return pl.pallas_call(
        body, name=name,
        out_shape=[jax.ShapeDtypeStruct(w.shape, F32)] * 4,
        grid=(rows // tr,),
        in_specs=[pl.BlockSpec((p.shape[0], p.shape[1], tr), lambda i: (0, 0, i)) for p in parts] + [spec] * 3,
        out_specs=[spec] * 4,
        compiler_params=pltpu.CompilerParams(dimension_semantics=("parallel",)),
    )(*parts, w, m, v)


def _adamw(w, g, m, v, name):
    rows, cols = w.shape
    tr = _pick(rows, 128)

    def body(w_ref, g_ref, m_ref, v_ref, d_ref, nm_ref, nv_ref):
        d, nm, nv = _adam_update(w_ref[...], g_ref[...], m_ref[...], v_ref[...])
        d_ref[...] = d
        nm_ref[...] = nm
        nv_ref[...] = nv

    spec = pl.BlockSpec((tr, cols), lambda i: (i, 0))
    return pl.pallas_call(
        body, name=name,
        out_shape=[jax.ShapeDtypeStruct((rows, cols), F32)] * 3,
        grid=(rows // tr,),
        in_specs=[spec] * 4, out_specs=[spec] * 3,
        compiler_params=pltpu.CompilerParams(dimension_semantics=("parallel",)),
    )(w, g, m, v)


_VJP = {"nn": (("nt", "gb"), ("tn", "ag")),
        "nt": (("nn", "gb"), ("tn", "ga")),
        "tn": (("nt", "bg"), ("nn", "ag"))}


def _make_dot(cast, precision):
    def raw(mode, a, b):
        return lax.dot_general(cast(a), cast(b), _DIMS[mode], precision=precision,
                               preferred_element_type=F32)

    @functools.partial(jax.custom_vjp, nondiff_argnums=(0,))
    def dot(mode, a, b):
        return raw(mode, a, b)

    def fwd(mode, a, b):
        return raw(mode, a, b), (a, b)

    def bwd(mode, res, g):
        a, b = res
        pick = {"a": a, "b": b, "g": g}
        (ma, ta), (mb, tb) = _VJP[mode]
        return dot(ma, pick[ta[0]], pick[ta[1]]), dot(mb, pick[tb[0]], pick[tb[1]])

    dot.defvjp(fwd, bwd)
    return dot


bdot = _make_dot(lambda t: t.astype(BF), None)
hdot = _make_dot(lambda t: t, lax.Precision.HIGHEST)


def _xdot(mode, a, b):
    return lax.dot_general(a, b, _DIMS[mode], precision=lax.Precision.HIGH, preferred_element_type=F32)


def _unit_lower_inverse(Ls):
    n = Ls[0].shape[0]
    batched = (((2,), (1,)), ((0,), (0,)))
    mm = lambda a, b: lax.dot_general(a, b, batched, precision=lax.Precision.HIGH, preferred_element_type=F32)
    eye = (lax.broadcasted_iota(jnp.int32, (n, n), 0) == lax.broadcasted_iota(jnp.int32, (n, n), 1)).astype(F32)
    p = jnp.stack(Ls)
    t_inv = eye[None] - p
    for _ in range(6):
        p = mm(p, p)
        t_inv = t_inv + mm(t_inv, p)
    return [t_inv[h] for h in range(len(Ls))]


@jax.custom_vjp
def _tri_solve(L, rhs, t_inv):
    return _xdot("nn", t_inv, rhs)


def _tri_solve_fwd(L, rhs, t_inv):
    sol = _xdot("nn", t_inv, rhs)
    return sol, (t_inv, sol)


def _tri_solve_bwd(res, dsol):
    t_inv, sol = res
    drhs = _xdot("tn", t_inv, dsol)
    return -_xdot("nt", drhs, sol), drhs, jnp.zeros_like(t_inv)


_tri_solve.defvjp(_tri_solve_fwd, _tri_solve_bwd)


def _sigmoid(x):
    return 1.0 / (1.0 + jnp.exp(-x))


def _softplus(x):
    return jnp.maximum(x, 0.0) + jnp.log(1.0 + jnp.exp(-jnp.abs(x)))


def _dn_chunk(S, xs, ba, z, cw, al, dt, dn, t_saved=None):
    C = DN_C
    pre = xs[0] * cw[0] + xs[1] * cw[1] + xs[2] * cw[2] + xs[3] * cw[3]
    qkv = pre * _sigmoid(pre)
    lane = lax.broadcasted_iota(jnp.int32, (1, 128), 1)
    sub = lax.broadcasted_iota(jnp.int32, (C, 1), 0)
    row_i = lax.broadcasted_iota(jnp.int32, (C, C), 0)
    col_i = lax.broadcasted_iota(jnp.int32, (C, C), 1)
    strict = row_i > col_i
    incl = row_i >= col_i
    g_all = jnp.where((lane >= 4) & (lane < 8), -jnp.exp(al) * _softplus(ba + dt), 0.0)
    gc_all = hdot("nn", incl.astype(F32), g_all)
    gc_all_t = gc_all.T
    beta_all = _sigmoid(ba)
    glast_all = jnp.sum(jnp.where(sub == C - 1, gc_all, 0.0), axis=0, keepdims=True)
    heads = []
    for h in range(DN_HEADS):
        q = qkv[:, 128 * h:128 * (h + 1)]
        k = qkv[:, 512 + 128 * h:512 + 128 * (h + 1)]
        v = qkv[:, 1024 + 128 * h:1024 + 128 * (h + 1)]
        q = q * lax.rsqrt(jnp.sum(q * q, axis=1, keepdims=True) + EPS) * (DN_D ** -0.5)
        k = k * lax.rsqrt(jnp.sum(k * k, axis=1, keepdims=True) + EPS)
        beta = jnp.sum(jnp.where(lane == h, beta_all, 0.0), axis=1, keepdims=True)
        gc = jnp.sum(jnp.where(lane == 4 + h, gc_all, 0.0), axis=1, keepdims=True)
        gc_row = jnp.sum(jnp.where(sub == 4 + h, gc_all_t, 0.0), axis=0, keepdims=True)
        g_last = jnp.sum(jnp.where(lane == 4 + h, glast_all, 0.0), axis=1, keepdims=True)
        diff = gc - gc_row
        kb = k * beta
        L = jnp.where(strict, bdot("nt", kb, k) * jnp.exp(jnp.where(strict, diff, 0.0)), 0.0)
        heads.append((q, k, v, beta, gc, g_last, diff, kb, L))
    t_invs = _unit_lower_inverse([hd[-1] for hd in heads]) if t_saved is None else t_saved
    ys, s_new = [], []
    for h, (q, k, v, beta, gc, g_last, diff, kb, L) in enumerate(heads):
        sol = _tri_solve(L, jnp.concatenate([v * beta, kb * jnp.exp(gc)], axis=1), t_invs[h])
        u, w = sol[:, :DN_D], sol[:, DN_D:]
        a_qk = jnp.where(incl, bdot("nt", q, k) * jnp.exp(jnp.where(incl, diff, 0.0)), 0.0)
        qg = q * jnp.exp(gc)
        kd = k * jnp.exp(g_last - gc)
        v_new = u - bdot("nn", w, S[h])
        o = bdot("nn", qg, S[h]) + bdot("nn", a_qk, v_new)
        s_new.append(S[h] * jnp.exp(g_last) + bdot("tn", kd, v_new))
        o = o * lax.rsqrt(jnp.mean(o * o, axis=1, keepdims=True) + EPS) * dn
        zh = z[:, 128 * h:128 * (h + 1)]
        ys.append(o * (zh * _sigmoid(zh)))
    return jnp.concatenate(ys, axis=1), tuple(s_new), tuple(t_invs)


def _load_shifted(xbuf, x_ref, halo_ref, first):
    xbuf[0:HALO, :] = jnp.where(first, 0.0, halo_ref[:, 0:1536])
    xbuf[HALO:HALO + DN_C, :] = x_ref[:, 0:1536]
    return [xbuf[HALO - 3 + k:HALO - 3 + k + DN_C, :] for k in range(4)]


def dn_forward(cols, jblk, cw, al, dt, dn, name):
    T = cols.shape[0]
    n = T // DN_C

    def body(x_ref, halo_ref, cw_ref, al_ref, dt_ref, dn_ref, y_ref, ss_ref, ts_ref, s_scr, xbuf):
        i = pl.program_id(0)

        @pl.when(i == 0)
        def _():
            s_scr[...] = jnp.zeros_like(s_scr)

        xs = _load_shifted(xbuf, x_ref, halo_ref, i == 0)
        ss_ref[0] = s_scr[...]
        S = [s_scr[h] for h in range(DN_HEADS)]
        cws = [cw_ref[k:k + 1, :] for k in range(4)]
        y, s_new, t_invs = _dn_chunk(S, xs, x_ref[:, 2048:2176], x_ref[:, 1536:2048], cws,
                                     al_ref[...], dt_ref[...], dn_ref[...])
        y_ref[...] = y
        for h in range(DN_HEADS):
            s_scr[h] = s_new[h]
            ts_ref[0, h] = t_invs[h]

    per = DN_C // HALO
    full = lambda shape: pl.BlockSpec(shape, lambda i: (0,) * len(shape))
    return pl.pallas_call(
        body, name=name,
        out_shape=[jax.ShapeDtypeStruct((T, 512), F32),
                   jax.ShapeDtypeStruct((n, DN_HEADS, DN_D, DN_D), F32),
                   jax.ShapeDtypeStruct((n, DN_HEADS, DN_D, DN_D), F32)],
        grid=(n,),
        in_specs=[pl.BlockSpec((DN_C, DN_W), lambda i: (i, jblk)),
                  pl.BlockSpec((HALO, DN_W), lambda i: (jnp.maximum(i * per - 1, 0), jblk)),
                  full((4, 1536)), full((1, 128)), full((1, 128)), full((1, 128))],
        out_specs=[pl.BlockSpec((DN_C, 512), lambda i: (i, 0)),
                   pl.BlockSpec((1, DN_HEADS, DN_D, DN_D), lambda i: (i, 0, 0, 0)),
                   pl.BlockSpec((1, DN_HEADS, DN_D, DN_D), lambda i: (i, 0, 0, 0))],
        scratch_shapes=[pltpu.VMEM((DN_HEADS, DN_D, DN_D), F32), pltpu.VMEM((HALO + DN_C, 1536), F32)],
        compiler_params=pltpu.CompilerParams(dimension_semantics=("arbitrary",)),
    )(cols, cols, cw, al, dt, dn)


def dn_backward(cols, jblk, cw, al, dt, dn, ss, ts, dy, dcols, name):
    T = cols.shape[0]
    n = T // DN_C

    def body(x_ref, halo_ref, cw_ref, al_ref, dt_ref, dn_ref, ss_ref, ts_ref, dy_ref, dcols_in,
             dx_ref, dcw_ref, dal_ref, ddt_ref, ddn_ref, ds_scr, xbuf, dbuf, carry):
        i = pl.program_id(0)

        @pl.when(i == 0)
        def _():
            ds_scr[...] = jnp.zeros_like(ds_scr)
            carry[...] = jnp.zeros_like(carry)
            dcw_ref[...] = jnp.zeros_like(dcw_ref)
            dal_ref[...] = jnp.zeros_like(dal_ref)
            ddt_ref[...] = jnp.zeros_like(ddt_ref)
            ddn_ref[...] = jnp.zeros_like(ddn_ref)

        xs = _load_shifted(xbuf, x_ref, halo_ref, i == n - 1)
        S = [ss_ref[0, h] for h in range(DN_HEADS)]
        cws = [cw_ref[k:k + 1, :] for k in range(4)]

        t_saved = [ts_ref[0, h] for h in range(DN_HEADS)]

        def f(S, xs, ba, z, cws, al, dt, dn):
            return _dn_chunk(S, xs, ba, z, cws, al, dt, dn, t_saved)[:2]

        _, vjp = jax.vjp(f, S, xs, x_ref[:, 2048:2176], x_ref[:, 1536:2048], cws, al_ref[...], dt_ref[...], dn_ref[...])
        dS, dxs, dba, dz, dcws, dal, ddt, ddn = vjp((dy_ref[...], tuple(ds_scr[h] for h in range(DN_HEADS))))
        for h in range(DN_HEADS):
            ds_scr[h] = dS[h]
        dbuf[...] = jnp.zeros_like(dbuf)
        for k in range(4):
            lo = HALO - 3 + k
            dbuf[lo:lo + DN_C, :] += dxs[k]
        dbuf[DN_C:DN_C + HALO, :] += carry[...]
        dx_ref[...] = jnp.concatenate([dbuf[HALO:HALO + DN_C, :], dz, dba,
                                       jnp.zeros((DN_C, DN_W - 2176), F32)], axis=1).astype(dx_ref.dtype)
        carry[...] = dbuf[0:HALO, :]
        for k in range(4):
            dcw_ref[k:k + 1, :] += dcws[k]
        dal_ref[...] += dal
        ddt_ref[...] += ddt
        ddn_ref[...] += ddn

    per = DN_C // HALO
    rev = lambda i: n - 1 - i
    full = lambda shape: pl.BlockSpec(shape, lambda i: (0,) * len(shape))
    return pl.pallas_call(
        body, name=name,
        out_shape=[jax.ShapeDtypeStruct(dcols.shape, dcols.dtype),jax.ShapeDtypeStruct((4, 1536), F32),
                   jax.ShapeDtypeStruct((1, 128), F32), jax.ShapeDtypeStruct((1, 128), F32),
                   jax.ShapeDtypeStruct((1, 128), F32)],
        grid=(n,),
        in_specs=[pl.BlockSpec((DN_C, DN_W), lambda i: (rev(i), jblk)),
                  pl.BlockSpec((HALO, DN_W), lambda i: (jnp.maximum(rev(i) * per - 1, 0), jblk)),
                  full((4, 1536)), full((1, 128)), full((1, 128)), full((1, 128)),
                  pl.BlockSpec((1, DN_HEADS, DN_D, DN_D), lambda i: (rev(i), 0, 0, 0)),
                  pl.BlockSpec((1, DN_HEADS, DN_D, DN_D), lambda i: (rev(i), 0, 0, 0)),
                  pl.BlockSpec((DN_C, 512), lambda i: (rev(i), 0)), _ANY],
        out_specs=[pl.BlockSpec((DN_C, DN_W), lambda i: (rev(i), jblk)),
                   full((4, 1536)), full((1, 128)), full((1, 128)), full((1, 128))],
        scratch_shapes=[pltpu.VMEM((DN_HEADS, DN_D, DN_D), F32), pltpu.VMEM((HALO + DN_C, 1536), F32),
                        pltpu.VMEM((HALO + DN_C, 1536), F32), pltpu.VMEM((HALO, 1536), F32)],
        input_output_aliases={9: 0},
        compiler_params=pltpu.CompilerParams(dimension_semantics=("arbitrary",)),
    )(cols, cols, cw, al, dt, dn, ss, ts, dy, dcols)


def _full(shape):
    return pl.BlockSpec(shape, lambda i: (0,) * len(shape))


def _silu(x):
    return x * _sigmoid(x)


def _gelu(x):
    return 0.5 * x * (1.0 + jnp.tanh(0.7978845608028654 * (x + 0.044715 * (x * x * x))))


def _lane_col(mat, idx):
    lane = lax.broadcasted_iota(jnp.int32, (1, mat.shape[1]), 1)
    return jnp.sum(jnp.where(lane == idx, mat, 0.0), axis=1, keepdims=True)


def _gm_chunk(uv, z, gain, ws, bt):
    g = _gelu(uv)
    u, v = g[:, :512], g[:, 512:]
    v = v * lax.rsqrt(jnp.mean(v * v, axis=1, keepdims=True) + EPS) * gain
    row_i = lax.broadcasted_iota(jnp.int32, (BLK, BLK), 0)
    col_i = lax.broadcasted_iota(jnp.int32, (BLK, BLK), 1)
    causal = row_i >= col_i
    ss = []
    for grp in range(4):
        wg = jnp.where(causal, ws[grp], 0.0)
        ss.append(bdot("nn", wg, v[:, BLK * grp:BLK * (grp + 1)]) + _lane_col(bt, grp))
    return u * jnp.concatenate(ss, axis=1) * _silu(z)


def gm_forward(cols, jblk, gain, ws, bt, name):
    T = cols.shape[0]

    def body(x_ref, gain_ref, ws_ref, bt_ref, y_ref):
        y_ref[...] = _gm_chunk(x_ref[:, 0:1024], x_ref[:, 1024:1536], gain_ref[...],
                               [ws_ref[g] for g in range(4)], bt_ref[...])

    return pl.pallas_call(
        body, name=name, out_shape=jax.ShapeDtypeStruct((T, 512), F32), grid=(T // BLK,),
        in_specs=[pl.BlockSpec((BLK, GM_W), lambda i: (i, jblk)),
                  _full((1, 512)), _full((4, BLK, BLK)), _full((BLK, BLK))],
        out_specs=pl.BlockSpec((BLK, 512), lambda i: (i, 0)),
        compiler_params=pltpu.CompilerParams(dimension_semantics=("parallel",)),
    )(cols, gain, ws, bt)


def gm_backward(cols, jblk, gain, ws, bt, dy, dcols, name):
    T = cols.shape[0]

    def body(x_ref, gain_ref, ws_ref, bt_ref, dy_ref, dcols_in, dx_ref, dgain_ref, dws_ref, dbt_ref):
        @pl.when(pl.program_id(0) == 0)
        def _():
            dgain_ref[...] = jnp.zeros_like(dgain_ref)
            dws_ref[...] = jnp.zeros_like(dws_ref)
            dbt_ref[...] = jnp.zeros_like(dbt_ref)

        _, vjp = jax.vjp(_gm_chunk, x_ref[:, 0:1024], x_ref[:, 1024:1536], gain_ref[...],
                         [ws_ref[g] for g in range(4)], bt_ref[...])
        duv, dz, dgain, dws, dbt = vjp(dy_ref[...])
        dx_ref[...] = jnp.concatenate([duv, dz], axis=1).astype(dx_ref.dtype)
        dgain_ref[...] += dgain
        for g in range(4):
            dws_ref[g] += dws[g]
        dbt_ref[...] += dbt

    return pl.pallas_call(
        body, name=name,
        out_shape=[jax.ShapeDtypeStruct(dcols.shape, dcols.dtype),jax.ShapeDtypeStruct((1, 512), F32),
                   jax.ShapeDtypeStruct((4, BLK, BLK), F32), jax.ShapeDtypeStruct((BLK, BLK), F32)],
        grid=(T // BLK,),
        in_specs=[pl.BlockSpec((BLK, GM_W), lambda i: (i, jblk)),
                  _full((1, 512)), _full((4, BLK, BLK)), _full((BLK, BLK)),
                  pl.BlockSpec((BLK, 512), lambda i: (i, 0)), _ANY],
        out_specs=[pl.BlockSpec((BLK, GM_W), lambda i: (i, jblk)),
                   _full((1, 512)), _full((4, BLK, BLK)), _full((BLK, BLK))],
        input_output_aliases={5: 0},
        compiler_params=pltpu.CompilerParams(dimension_semantics=("arbitrary",)),
    )(cols, gain, ws, bt, dy, dcols)


def _sw_block(first, q, kp, kc, vp, vc, z, sinks):
    P = BLK
    lane = lax.broadcasted_iota(jnp.int32, (1, 128), 1)
    r = lax.broadcasted_iota(jnp.int32, (128, 128), 0)
    c = lax.broadcasted_iota(jnp.int32, (128, 128), 1)
    swap = (c == (r + 64) % 128).astype(F32)
    k2 = jnp.concatenate([kp, kc], axis=0)
    v2 = jnp.concatenate([vp, vc], axis=0)
    k2s = bdot("nn", k2, swap)
    v2s = bdot("nn", v2, swap)
    qi = lax.broadcasted_iota(jnp.int32, (P, 2 * P), 0)
    kj = lax.broadcasted_iota(jnp.int32, (P, 2 * P), 1)
    dist = qi + P - kj
    valid = (dist >= 0) & (dist < P) & ((kj >= P) | jnp.logical_not(first))
    outs = []
    for j in range(4):
        acc = jnp.zeros((P, 128), F32)
        for half in range(2):
            h = 2 * j + half
            kv = h // 4
            in_half = (lane >= 64 * half) & (lane < 64 * half + 64)
            qh = jnp.where(in_half, q[:, 128 * j:128 * (j + 1)], 0.0)
            same = (half == kv)
            s = bdot("nt", qh, k2 if same else k2s) * (64 ** -0.5)
            s = jnp.where(valid, s, NEG_INF)
            sink = _lane_col(sinks, h)
            m = lax.stop_gradient(jnp.maximum(jnp.max(s, axis=1, keepdims=True), sink))
            e = jnp.exp(s - m)
            p = e / (jnp.sum(e, axis=1, keepdims=True) + jnp.exp(sink - m))
            o = bdot("nn", p, v2 if same else v2s)
            acc = acc + jnp.where(in_half, o, 0.0)
        outs.append(acc)
    return jnp.concatenate(outs, axis=1) * _silu(z)


def _sw_specs(jblk, idx):
    prev = lambda i: jnp.maximum(idx(i) - 1, 0)
    jk = (jblk * SW_W + 1024) // 128
    return [pl.BlockSpec((BLK, SW_W), lambda i: (idx(i), jblk)),
            pl.BlockSpec((BLK, 128), lambda i: (prev(i), jk)),
            pl.BlockSpec((BLK, 128), lambda i: (prev(i), jk + 1)), _full((1, 128))]


def sw_forward(cols, jblk, sinks, name):
    T = cols.shape[0]

    def body(x_ref, kp_ref, vp_ref, s_ref, y_ref):
        y_ref[...] = _sw_block(pl.program_id(0) == 0, x_ref[:, 0:512], kp_ref[...], x_ref[:, 1024:1152],
                               vp_ref[...], x_ref[:, 1152:1280], x_ref[:, 512:1024], s_ref[...])

    return pl.pallas_call(
        body, name=name, out_shape=jax.ShapeDtypeStruct((T, 512), F32), grid=(T // BLK,),
        in_specs=_sw_specs(jblk, lambda i: i),
        out_specs=pl.BlockSpec((BLK, 512), lambda i: (i, 0)),
        compiler_params=pltpu.CompilerParams(dimension_semantics=("parallel",)),
    )(cols, cols, cols, sinks)


def sw_backward(cols, jblk, sinks, dy, dcols, name):
    T = cols.shape[0]
    n = T // BLK
    rev = lambda i: n - 1 - i

    def body(x_ref, kp_ref, vp_ref, s_ref, dy_ref, dcols_in, dx_ref, ds_ref, kcarry, vcarry):
        i = pl.program_id(0)

        @pl.when(i == 0)
        def _():
            kcarry[...] = jnp.zeros_like(kcarry)
            vcarry[...] = jnp.zeros_like(vcarry)
            ds_ref[...] = jnp.zeros_like(ds_ref)

        f = functools.partial(_sw_block, i == n - 1)
        _, vjp = jax.vjp(f, x_ref[:, 0:512], kp_ref[...], x_ref[:, 1024:1152], vp_ref[...], x_ref[:, 1152:1280],
                         x_ref[:, 512:1024], s_ref[...])
        dq, dkp, dkc, dvp, dvc, dz, dsk = vjp(dy_ref[...])
        dx_ref[...] = jnp.concatenate([dq, dz, dkc + kcarry[...], dvc + vcarry[...],
                                       jnp.zeros((BLK, SW_W - 1280), F32)], axis=1).astype(dx_ref.dtype)
        kcarry[...] = dkp
        vcarry[...] = dvp
        ds_ref[...] += dsk

    return pl.pallas_call(
        body, name=name,
        out_shape=[jax.ShapeDtypeStruct(dcols.shape, dcols.dtype),jax.ShapeDtypeStruct((1, 128), F32)],
        grid=(n,),
        in_specs=_sw_specs(jblk, rev) + [pl.BlockSpec((BLK, 512), lambda i: (rev(i), 0)), _ANY],
        out_specs=[pl.BlockSpec((BLK, SW_W), lambda i: (rev(i), jblk)), _full((1, 128))],
        scratch_shapes=[pltpu.VMEM((BLK, 128), F32), pltpu.VMEM((BLK, 128), F32)],
        input_output_aliases={5: 0},
        compiler_params=pltpu.CompilerParams(dimension_semantics=("arbitrary",)),
    )(cols, cols, cols, sinks, dy, dcols)


XM_TQ = 256


def _xm_block(q, z, mkv):
    outs = []
    for h in range(4):
        s = bdot("nt", q[:, 128 * h:128 * (h + 1)], mkv[:, 128 * h:128 * (h + 1)]) * (128 ** -0.5)
        m = lax.stop_gradient(jnp.max(s, axis=1, keepdims=True))
        e = jnp.exp(s - m)
        p = e / jnp.sum(e, axis=1, keepdims=True)
        outs.append(bdot("nn", p, mkv[:, 512 + 128 * h:512 + 128 * (h + 1)]))
    return jnp.concatenate(outs, axis=1) * _silu(z)


def xm_forward(cols, jblk, mkv, name):
    T = cols.shape[0]

    def body(x_ref, m_ref, y_ref):
        y_ref[...] = _xm_block(x_ref[:, 0:512], x_ref[:, 512:1024], m_ref[...])

    return pl.pallas_call(
        body, name=name, out_shape=jax.ShapeDtypeStruct((T, 512), F32), grid=(T // XM_TQ,),
        in_specs=[pl.BlockSpec((XM_TQ, XM_W), lambda i: (i, jblk)), _full(mkv.shape)],
        out_specs=pl.BlockSpec((XM_TQ, 512), lambda i: (i, 0)),
        compiler_params=pltpu.CompilerParams(dimension_semantics=("parallel",)),
    )(cols, mkv)


def xm_backward(cols, jblk, mkv, dy, dcols, name):
    T = cols.shape[0]

    def body(x_ref, m_ref, dy_ref, dcols_in, dx_ref, dm_ref):
        @pl.when(pl.program_id(0) == 0)
        def _():
            dm_ref[...] = jnp.zeros_like(dm_ref)

        _, vjp = jax.vjp(_xm_block, x_ref[:, 0:512], x_ref[:, 512:1024], m_ref[...])
        dq, dz, dm = vjp(dy_ref[...])
        dx_ref[...] = jnp.concatenate([dq, dz], axis=1).astype(dx_ref.dtype)
        dm_ref[...] += dm

    return pl.pallas_call(
        body, name=name,
        out_shape=[jax.ShapeDtypeStruct(dcols.shape, dcols.dtype),jax.ShapeDtypeStruct(mkv.shape, F32)],
        grid=(T // XM_TQ,),
        in_specs=[pl.BlockSpec((XM_TQ, XM_W), lambda i: (i, jblk)), _full(mkv.shape),
                  pl.BlockSpec((XM_TQ, 512), lambda i: (i, 0)), _ANY],
        out_specs=[pl.BlockSpec((XM_TQ, XM_W), lambda i: (i, jblk)), _full(mkv.shape)],
        input_output_aliases={3: 0},
        compiler_params=pltpu.CompilerParams(dimension_semantics=("arbitrary",)),
    )(cols, mkv, dy, dcols)


def _rms(x, gain):
    return x * lax.rsqrt(jnp.mean(x * x, axis=1, keepdims=True) + EPS) * gain


def memkv_forward(mem, gain, w, name):
    def body(m_ref, g_ref, w_ref, o_ref):
        o_ref[...] = bdot("nn", _rms(m_ref[...], g_ref[...]), w_ref[...])

    return pl.pallas_call(body, name=name, out_shape=jax.ShapeDtypeStruct(mem.shape, F32),
                          compiler_params=pltpu.CompilerParams(vmem_limit_bytes=VMEM_LIMIT))(mem, gain, w)


def memkv_backward(mem, gain, w, dkv, name):
    def body(m_ref, g_ref, w_ref, d_ref, dg_ref, dw_ref):
        mem_v = m_ref[...]
        _, vjp = jax.vjp(lambda g, ww: bdot("nn", _rms(mem_v, g), ww), g_ref[...], w_ref[...].astype(F32))
        dg, dw = vjp(d_ref[...])
        dg_ref[...] = dg
        dw_ref[...] = dw

    return pl.pallas_call(body, name=name,
                          out_shape=[jax.ShapeDtypeStruct(gain.shape, F32), jax.ShapeDtypeStruct(w.shape, F32)],
                          compiler_params=pltpu.CompilerParams(vmem_limit_bytes=VMEM_LIMIT))(mem, gain, w, dkv)


MG_TB = 256


def _merge_block(ys, gl, wup, wout, gpost):
    merged = None
    for n in range(4):
        t = _sigmoid(gl[:, 1024 * n:1024 * (n + 1)]) * bdot("nn", ys[n], wup[n])
        merged = t if merged is None else merged + t
    out = bdot("nn", merged, wout)
    return _rms(out, gpost)


def merge_forward(ys, cols, jgate, x, wup, wout, gpost, name, next_gain=None, target=None):
    T, D = x.shape
    TB = 256
    n_extra = (next_gain is not None) + (target is not None)

    def body(ya, yb, yc, ym, gl_ref, x_ref, wup_ref, wout_ref, gp_ref, *rest):
        extra, outs = rest[:n_extra], rest[n_extra:]
        upd = _merge_block([ya[...], yb[...], yc[...], ym[...]], gl_ref[...],
                           [wup_ref[n] for n in range(4)], wout_ref[...], gp_ref[...])
        y = x_ref[...] + upd
        outs[0][...] = y
        outs = outs[1:]
        if next_gain is not None:
            h = _rms(y, extra[0][...])
            outs[0][...] = h.astype(BF)
            outs[1][...] = h.T.astype(BF)
            outs = outs[2:]
        if target is not None:
            l_ref, d_ref = outs

            @pl.when(pl.program_id(0) == 0)
            def _():
                l_ref[...] = jnp.zeros_like(l_ref)

            err = y - extra[-1][...]
            d_ref[...] = err * (1.0 / D)
            l_ref[...] += jnp.full(l_ref.shape, 0.5 * jnp.sum(jnp.mean(err * err, axis=1, keepdims=True)), F32)

    yspec = pl.BlockSpec((TB, 512), lambda i: (i, 0))
    xspec = pl.BlockSpec((TB, D), lambda i: (i, 0))
    extra_in, extra_specs = [], []
    out_shape, out_specs = [jax.ShapeDtypeStruct((T, D), F32)], [xspec]
    if next_gain is not None:
        extra_in, extra_specs = extra_in + [next_gain], extra_specs + [_full((1, D))]
        out_shape += [jax.ShapeDtypeStruct((T, D), BF), jax.ShapeDtypeStruct((D, T), BF)]
        out_specs += [xspec, pl.BlockSpec((D, TB), lambda i: (0, i))]
    if target is not None:
        extra_in, extra_specs = extra_in + [target], extra_specs + [xspec]
        out_shape += [jax.ShapeDtypeStruct((1, 128), F32), jax.ShapeDtypeStruct((T, D), F32)]
        out_specs += [_full((1, 128)), xspec]
    return pl.pallas_call(
        body, name=name, out_shape=out_shape, grid=(T // TB,),
        in_specs=[yspec] * 4 + [pl.BlockSpec((TB, 4096), lambda i: (i, jgate)), xspec,
                                _full(wup.shape), _full(wout.shape), _full((1, D))] + extra_specs,
        out_specs=out_specs,
        compiler_params=pltpu.CompilerParams(
            dimension_semantics=("parallel" if target is None else "arbitrary",), vmem_limit_bytes=VMEM_LIMIT),
    )(*ys, cols, x, wup, wout, gpost, *extra_in)


def _token_product(a, b, name):
    (T, m), n = a.shape, b.shape[1]

    def body(a_ref, b_ref, o_ref):
        o_ref[...] = lax.dot_general(a_ref[...].astype(BF), b_ref[...].astype(BF), _DIMS["tn"], preferred_element_type=F32)

    return pl.pallas_call(body, name=name, out_shape=jax.ShapeDtypeStruct((m, n), F32),
                          compiler_params=pltpu.CompilerParams(vmem_limit_bytes=VMEM_LIMIT))(a, b)


def merge_backward(ys, cols, jgate, wup, wout, gpost, dx, name):
    T = dx.shape[0]
    TB = MG_TB

    def body(ya, yb, yc, ym, gl_ref, wup_ref, wout_ref, gp_ref, dx_ref,
             dgl_ref, dya, dyb, dyc, dym, dpa, dpb, dpc, dpm, merged_ref, dout_ref, dgp_ref):
        @pl.when(pl.program_id(0) == 0)
        def _():
            dgp_ref[...] = jnp.zeros_like(dgp_ref)

        y_refs = (ya, yb, yc, ym)
        gates = [_sigmoid(gl_ref[:, 1024 * n:1024 * (n + 1)]) for n in range(4)]
        projs = [bdot("nn", y_refs[n][...], wup_ref[n]) for n in range(4)]
        merged = gates[0] * projs[0] + gates[1] * projs[1] + gates[2] * projs[2] + gates[3] * projs[3]
        out = bdot("nn", merged, wout_ref[...])
        _, vjp = jax.vjp(_rms, out, gp_ref[...])
        dout, dgp = vjp(dx_ref[...])
        dmerged = bdot("nt", dout, wout_ref[...])
        for n, (dy_ref, dp_ref) in enumerate(zip((dya, dyb, dyc, dym), (dpa, dpb, dpc, dpm))):
            dproj = dmerged * gates[n]
            dgl_ref[:, 1024 * n:1024 * (n + 1)] = (dmerged * projs[n] * gates[n] * (1.0 - gates[n])).astype(dgl_ref.dtype)
            dy_ref[...] = bdot("nt", dproj, wup_ref[n])
            dp_ref[...] = dproj.astype(BF)
        merged_ref[...] = merged.astype(BF)
        dout_ref[...] = dout.astype(BF)
        dgp_ref[...] += dgp

    yspec = pl.BlockSpec((TB, 512), lambda i: (i, 0))
    dspec = pl.BlockSpec((TB, 1024), lambda i: (i, 0))
    dcols, dya, dyb, dyc, dym, *dproj, merged, dout, dgp = pl.pallas_call(
        body, name=name,
        out_shape=[jax.ShapeDtypeStruct(cols.shape, BF)] + [jax.ShapeDtypeStruct((T, 512), F32)] * 4 + [
            jax.ShapeDtypeStruct((T, 1024), BF)] * 6 + [jax.ShapeDtypeStruct((1, 1024), F32)],
        grid=(T // TB,),
        in_specs=[yspec] * 4 + [pl.BlockSpec((TB, 4096), lambda i: (i, jgate)),
                                _full(wup.shape), _full(wout.shape), _full((1, 1024)), dspec],
        out_specs=[pl.BlockSpec((TB, 4096), lambda i: (i, jgate))] + [yspec] * 4 + [
            dspec] * 6 + [_full((1, 1024))],
        compiler_params=pltpu.CompilerParams(dimension_semantics=("arbitrary",), vmem_limit_bytes=VMEM_LIMIT),
    )(*ys, cols, wup, wout, gpost, dx)
    dwup = jnp.stack([_token_product(ys[n], dproj[n], "%s_w_up%d" % (name, n)) for n in range(4)])
    dwout = _token_product(merged, dout, name + "_w_out")
    return dcols, dya, dyb, dyc, dym, dwup, dwout, dgp


NB = 256


def prenorm_forward(x, gain, name):
    T, D = x.shape

    def body(x_ref, g_ref, o_ref, ot_ref):
        h = _rms(x_ref[...], g_ref[...])
        o_ref[...] = h.astype(BF)
        ot_ref[...] = h.T.astype(BF)

    return pl.pallas_call(
        body, name=name,
        out_shape=[jax.ShapeDtypeStruct((T, D), BF), jax.ShapeDtypeStruct((D, T), BF)], grid=(T // NB,),
        in_specs=[pl.BlockSpec((NB, D), lambda i: (i, 0)), _full((1, D))],
        out_specs=[pl.BlockSpec((NB, D), lambda i: (i, 0)), pl.BlockSpec((D, NB), lambda i: (0, i))],
        compiler_params=pltpu.CompilerParams(dimension_semantics=("parallel",)),
    )(x, gain)


def prenorm_backward(x, gain, dh, dres, name):
    T = x.shape[0]

    def body(x_ref, g_ref, dh_ref, dr_ref, dx_ref, dg_ref):
        @pl.when(pl.program_id(0) == 0)
        def _():
            dg_ref[...] = jnp.zeros_like(dg_ref)

        _, vjp = jax.vjp(_rms, x_ref[...], g_ref[...])
        dxn, dg = vjp(dh_ref[...])
        dx_ref[...] = dr_ref[...] + dxn
        dg_ref[...] += dg

    spec = pl.BlockSpec((NB, 1024), lambda i: (i, 0))
    return pl.pallas_call(
        body, name=name,
        out_shape=[jax.ShapeDtypeStruct(x.shape, F32), jax.ShapeDtypeStruct((1, 1024), F32)], grid=(T // NB,),
        in_specs=[spec, _full((1, 1024)), spec, spec], out_specs=[spec, _full((1, 1024))],
        compiler_params=pltpu.CompilerParams(dimension_semantics=("arbitrary",)),
    )(x, gain, dh, dres)


JB_GATE, JB_XM, JB_DN, JB_SW, JB_GM = 0, 4, 2, 5, 6
_ALIGNED_PIECES = ((5896, 4096), (4872, 512), (5384, 512), (0, 2048), (2048, 8), 504, (3592, 512), (4360, 512),
                   (4104, 128), (4232, 128), 256, (2056, 1024), (3080, 512))
_NATURAL_FROM_ALIGNED = ((5120, 2048), (7168, 8), (9216, 1024), (10240, 512), (7680, 512), (8704, 128), (8832, 128),
                         (8192, 512), (4096, 512), (4608, 512), (0, 4096))


def _natural_range(slots, start, width):
    out = []
    while width > 0:
        j, i = divmod(start, W_IN_SHARD)
        take = min(width, W_IN_SHARD - i)
        out.append(slots[j, :, i:i + take])
        start, width = start + take, width - take
    return out


def _aligned_w_in(slots):
    parts = []
    for piece in _ALIGNED_PIECES:
        if isinstance(piece, int):
            parts.append(jnp.zeros(slots.shape[1:2] + (piece,), slots.dtype))
        else:
            parts += _natural_range(slots, *piece)
    return jnp.concatenate(parts, axis=-1)


def _slots_of_aligned(d_al):
    slots = []
    for s in range(N_DEV):
        lo, hi = s * W_IN_SHARD, (s + 1) * W_IN_SHARD
        parts, nat = [], 0
        for a_start, width in _NATURAL_FROM_ALIGNED:
            b, e = max(lo, nat), min(hi, nat + width)
            if b < e:
                parts.append(d_al[..., a_start + b - nat:a_start + e - nat])
            nat += width
        parts.append(jnp.zeros(d_al.shape[:1] + (W_IN_SHARD_PAD - W_IN_SHARD,), d_al.dtype))
        slots.append(jnp.concatenate(parts, axis=-1))
    return jnp.stack(slots)


SMALL_VEC_W = 1024


def _pack_small(parts):
    rows = []
    for p in parts:
        flat = p.reshape(-1).astype(F32)
        r = -(-flat.shape[0] // SMALL_VEC_W)
        rows.append(jnp.pad(flat, (0, r * SMALL_VEC_W - flat.shape[0])).reshape(r, SMALL_VEC_W))
    vec = jnp.concatenate(rows, axis=0)
    return jnp.pad(vec, ((0, -vec.shape[0] % 8), (0, 0)))


def _unpack_small(vec, shapes):
    out, off = [], 0
    for s in shapes:
        n = math.prod(s)
        r = -(-n // SMALL_VEC_W)
        out.append(vec[off:off + r].reshape(-1)[:n].reshape(s))
        off += r
    return out


def _lanes(vec, at):
    return jnp.zeros((1, 128), F32).at[0, at:at + vec.shape[0]].set(vec)


SMALL_NAMES = ("norm_pre", "norm_post", "norm_mem", "a_log", "dt_bias", "dn_norm", "gm_norm",
               "spatial_w", "spatial_b", "sinks")


def _other_weights(s_mem, s_up, s_out):
    return (s_mem.reshape(D_MODEL, 2 * BRANCH_W),
            jnp.transpose(s_up, (1, 2, 0, 3)).reshape(N_BRANCH, BRANCH_W, D_MODEL), s_out.reshape(D_MODEL, D_MODEL))


def _grad_slots(d_in_al, d_mem, d_up, d_out):
    return [None if d_in_al is None else _slots_of_aligned(d_in_al), d_mem.astype(BF).reshape(N_DEV, 128, 2 * BRANCH_W),
            jnp.transpose(d_up.astype(BF).reshape(N_BRANCH, BRANCH_W, N_DEV, 128), (2, 0, 1, 3)),
            d_out.astype(BF).reshape(N_DEV, 128, D_MODEL)]


def _layer_params(l, small, conv_full, token):
    return dict(
        gpre=small["norm_pre"][l][None] + token, gpost=small["norm_post"][l][None], gmem=small["norm_mem"][l][None],
        cw=conv_full[l], al=_lanes(small["a_log"][l], 4), dt=_lanes(small["dt_bias"][l], 4),
        dnn=small["dn_norm"][l][None], gain=small["gm_norm"][l][None], ws=small["spatial_w"][l],
        bt=jnp.zeros((128, 128), F32).at[:, :GM_GROUPS].set(small["spatial_b"][l].T),
        sinks=_lanes(small["sinks"][l], 0))


def _layer_forward(l, xl, hs, mem, p, w_in_al, other_weights, **tail):
    t = "l%d_" % l
    h, h_t = hs
    cols = _matmul(h, w_in_al, "nn", F32, (1024, 1536, 1024), t + "w_in")
    ya, ss, ts = dn_forward(cols, JB_DN, p["cw"], p["al"], p["dt"], p["dnn"], t + "deltanet")
    yb = gm_forward(cols, JB_GM, p["gain"], p["ws"], p["bt"], t + "gmlp")
    yc = sw_forward(cols, JB_SW, p["sinks"], t + "swa")
    w_mem, w_up, w_out = other_weights(yc)
    mkv = memkv_forward(mem, p["gmem"], w_mem, t + "memkv")
    ym = xm_forward(cols, JB_XM, mkv, t + "memattn")
    outs = merge_forward([ya, yb, yc, ym], cols, JB_GATE, xl, w_up, w_out, p["gpost"], t + "merge", **tail)
    return outs, dict(p, x=xl, h_t=h_t, cols=cols, mkv=mkv, ss=ss, ts=ts, ys=[ya, yb, yc, ym]), (w_in_al, w_mem, w_up, w_out)


def _layer_backward(l, s, mem, weights, dx, token, early=None):
    w_in_al, w_mem, w_up, w_out = weights
    t = "l%d_" % l
    cols = s["cols"]
    dcols, dya, dyb, dyc, dym, dwup, dwout, dgpost = merge_backward(
        s["ys"], cols, JB_GATE, w_up, w_out, s["gpost"] + token, dx, t + "merge_bwd")
    dcols, dmkv = xm_backward(cols, JB_XM, s["mkv"], dym, dcols, t + "memattn_bwd")
    dgmem, dwmem = memkv_backward(mem, s["gmem"], w_mem, dmkv, t + "memkv_bwd")
    sinks = s["sinks"] if early is None else s["sinks"] + early(dwmem, dwup, dwout)
    dcols, dsinks = sw_backward(cols, JB_SW, sinks, dyc, dcols, t + "swa_bwd")
    dcols, dgain, dws, dbt = gm_backward(cols, JB_GM, s["gain"], s["ws"], s["bt"], dyb, dcols, t + "gmlp_bwd")
    dcols, dcw, dal, ddt, ddn = dn_backward(
        cols, JB_DN, s["cw"], s["al"], s["dt"], s["dnn"], s["ss"], s["ts"], dya, dcols, t + "deltanet_bwd")
    dh = _matmul(dcols, w_in_al, "nt", F32, (1024, 1024, 3584), t + "w_in_bwd_x")
    dwin = _matmul(s["h_t"], dcols, "nn", BF, (1024, 1536, 2048), t + "w_in_bwd_w")
    dx, dgpre = prenorm_backward(s["x"], s["gpre"], dh, dx, t + "prenorm_bwd")
    gsmall = dict(norm_pre=dgpre[0], norm_post=dgpost[0], norm_mem=dgmem[0], a_log=dal[0, 4:8], dt_bias=ddt[0, 4:8],
                  dn_norm=ddn[0], gm_norm=dgain[0], spatial_w=dws, spatial_b=dbt[:, :GM_GROUPS].T,
                  sinks=dsinks[0, :SW_HEADS], conv_w=dcw)
    return dx, gsmall, (dwin, dwmem, dwup, dwout)


def kernel(x, mem, norm_pre, norm_post, norm_mem, w_in, conv_w, a_log, dt_bias, dn_norm, gm_norm, spatial_w, spatial_b, sinks, w_mem_kv, w_up, w_out, loss_target, m_norm_pre, m_norm_post, m_norm_mem, m_w_in, m_conv_w, m_a_log, m_dt_bias, m_dn_norm, m_gm_norm, m_spatial_w, m_spatial_b, m_sinks, m_w_mem_kv, m_w_up, m_w_out, v_norm_pre, v_norm_post, v_norm_mem, v_w_in, v_conv_w, v_a_log, v_dt_bias, v_dn_norm, v_gm_norm, v_spatial_w, v_spatial_b, v_sinks, v_w_mem_kv, v_w_up, v_w_out):
    xi, yi, ci = _my_place()
    my_slot = 4 * xi + 2 * yi + ci
    conv_shard = conv_w.shape[-1]
    x2, mem2, target = x[0], mem[0], loss_target[0]

    w_in_pad = jnp.pad(w_in.astype(BF), ((0, 0), (0, 0), (0, W_IN_SHARD_PAD - W_IN_SHARD)))
    shards = [[w_in_pad[l], w_mem_kv[l].astype(BF), w_up[l].astype(BF), w_out[l].astype(BF)] for l in range(DEPTH)]
    w_in_slots0, w_up_slots0, w_out_slots0, conv_slots = _all_gather_slots(
        [shards[0][0], shards[0][2], shards[0][3], conv_w], "gather_weights_l0")
    ag = list(_spread_start([shards[0][1]] + shards[1], "gather", "gather_weights_rest_start"))
    conv_full = jnp.transpose(conv_slots, (1, 2, 0, 3)).reshape(DEPTH, CONV_W, N_DEV * conv_shard)
    small = dict(norm_pre=norm_pre, norm_post=norm_post, norm_mem=norm_mem, a_log=a_log,
                 dt_bias=dt_bias, dn_norm=dn_norm, gm_norm=gm_norm, spatial_w=spatial_w,
                 spatial_b=spatial_b, sinks=sinks)

    def arrived(which, after, name):
        ag[2], ag[3] = _spread_wait(ag[0], ag[1], ag[2], ag[3], which, after, name)
        return [ag[3][a] for a in which]

    p0, p1 = _layer_params(0, small, conv_full, ag[4][0, 0]), _layer_params(1, small, conv_full, 0.0)
    (x1, h1, h1_t), saved0, weights0 = _layer_forward(
        0, x2, prenorm_forward(x2, p0["gpre"], "l0_prenorm"), mem2, p0, _aligned_w_in(w_in_slots0),
        lambda y: _other_weights(*arrived([0], y, "gather_weights_l0_w_mem_wait"), w_up_slots0, w_out_slots0),
        next_gain=p1["gpre"])
    w_in_slots1, = arrived([1], x1, "gather_weights_l1_w_in_wait")
    (x_out, loss, dx), saved1, weights1 = _layer_forward(
        1, x1, (h1, h1_t), mem2, p1, _aligned_w_in(w_in_slots1),
        lambda y: _other_weights(*arrived([2, 3, 4], y, "gather_weights_l1_rest_wait")), target=target)

    packed_names = SMALL_NAMES + ("conv_w",)
    dx, gsmall1, gbig1 = _layer_backward(1, saved1, mem2, weights1, dx, 0.0)
    small1 = [loss[0, :1]] + [gsmall1[n] for n in packed_names]
    sm1 = _spread_start([_pack_small(small1)], "gather", "gather_small_grads_l1_start")
    rs_send, rs_recv, rs_src, rs_land, rs_token = _spread_start(_grad_slots(*gbig1), "scatter", "exchange_grads_l1_start")
    rest0 = []

    def send_rest0(dwmem, dwup, dwout):
        rest0.extend(_spread_start(_grad_slots(None, dwmem, dwup, dwout)[1:], "scatter", "exchange_grads_l0_rest_start"))
        return rest0[4][0, 0]

    dx, gsmall0, gbig0 = _layer_backward(0, saved0, mem2, weights0, dx, rs_token[0, 0] + sm1[4][0, 0], send_rest0)
    _, parts1 = _spread_wait(rs_send, rs_recv, rs_src, rs_land, range(4), dx, "exchange_grads_l1_wait")

    small0 = [gsmall0[n] for n in packed_names]
    sm0 = _spread_start([_pack_small(small0)], "gather", "gather_small_grads_l0_start")

    g_win0 = _slots_of_aligned(gbig0[0])
    g_win0 = g_win0.reshape((N_DEV // 2, 2) + g_win0.shape[1:])
    theirs, = _exchange_sibling([g_win0], "exchange_sibling_l0")
    chip_sum = _pair_sum(g_win0, theirs, "pair_sum_l0")
    ch_send, ch_recv, ch_src, ch_land, ch_token = _spread_start([chip_sum], "chips", "exchange_chips_l0_start")

    _, (land1,) = _spread_wait(*sm1[:4], [0], ch_token, "gather_small_grads_l1_wait")
    _, (land0,) = _spread_wait(*sm0[:4], [0], land1, "gather_small_grads_l0_wait")
    tot1 = _unpack_small(_sum_slots(land1, "sum_small_grads_l1"), [p.shape for p in small1])
    tot0 = _unpack_small(_sum_slots(land0, "sum_small_grads_l0"), [p.shape for p in small0])
    loss_tot = tot1[0][0]
    grads = {n: jnp.stack([g0, g1]) for n, g0, g1 in zip(packed_names, tot0, tot1[1:])}
    grads["conv_w"] = lax.dynamic_slice_in_dim(grads["conv_w"], my_slot * conv_shard, conv_shard, axis=2)

    given = dict(norm_pre=(norm_pre, m_norm_pre, v_norm_pre), norm_post=(norm_post, m_norm_post, v_norm_post),
                 norm_mem=(norm_mem, m_norm_mem, v_norm_mem), a_log=(a_log, m_a_log, v_a_log),
                 dt_bias=(dt_bias, m_dt_bias, v_dt_bias), dn_norm=(dn_norm, m_dn_norm, v_dn_norm),
                 gm_norm=(gm_norm, m_gm_norm, v_gm_norm), spatial_w=(spatial_w, m_spatial_w, v_spatial_w),
                 spatial_b=(spatial_b, m_spatial_b, v_spatial_b), sinks=(sinks, m_sinks, v_sinks),
                 conv_w=(conv_w, m_conv_w, v_conv_w))
    pshapes = [given[n][0].shape for n in packed_names]
    pw, pm, pv = (_pack_small([given[n][i] for n in packed_names]) for i in range(3))
    pd, pnm, pnv = _adamw(pw + ch_token[0, 0], _pack_small([grads[n] for n in packed_names]), pm, pv, "adamw_small")
    upd = {n: t for n, t in zip(packed_names, zip(_unpack_small(pd, pshapes), _unpack_small(pnm, pshapes),
                                                  _unpack_small(pnv, pshapes)))}
    big = (("w_mem_kv", (w_mem_kv, m_w_mem_kv, v_w_mem_kv)), ("w_up", (w_up, m_w_up, v_w_up)),
           ("w_out", (w_out, m_w_out, v_w_out)))
    first = [_sum_adamw(parts1[1 + i], w, m, v, 1, None, ch_token, "adamw_%s_l1" % name)
             for i, (name, (w, m, v)) in enumerate(big)]
    _, parts0_rest = _spread_wait(*rest0[:4], range(3), first[-1][0], "exchange_grads_l0_rest_wait")
    for i, (name, (w, m, v)) in enumerate(big):
        g, d, nm, nv = _sum_adamw(parts0_rest[i], w, m, v, 0, first[i], None, "adamw_%s_l0" % name)
        grads[name], upd[name] = g, (d, nm, nv)
    _, (parts0_w_in,) = _spread_wait(ch_send, ch_recv, ch_src, ch_land, [0], upd["w_out"][0], "exchange_chips_l0_wait")
    w_in_t, m_w_in_t, v_w_in_t = (jnp.transpose(t, (2, 0, 1)) for t in (w_in, m_w_in, v_w_in))
    g, d, nm, nv = (jnp.transpose(t, (1, 2, 0)) for t in
                    _sum_adamw_t([parts0_w_in, parts1[0]], w_in_t, m_w_in_t, v_w_in_t, "adamw_w_in"))
    grads["w_in"], upd["w_in"] = g, (d, nm, nv)

    order = ("norm_pre", "norm_post", "norm_mem", "w_in", "conv_w", "a_log", "dt_bias", "dn_norm",
             "gm_norm", "spatial_w", "spatial_b", "sinks", "w_mem_kv", "w_up", "w_out")
    return (loss_tot, dx[None], *[grads[n] for n in order], *[upd[n][0] for n in order],
            *[upd[n][1] for n in order], *[upd[n][2] for n in order])
```

```python
import functools
import math

import jax
import jax.numpy as jnp
from jax import lax
from jax.experimental import pallas as pl
from jax.experimental.pallas import tpu as pltpu

MESH = pl.DeviceIdType.MESH
N_DEV = 8

D_MODEL = 1024
DEPTH = 2
N_BRANCH = 4
BRANCH_W = 512
DN_HEADS = 4
CONV_W = 4
GM_GROUPS = 4
SW_HEADS = 8
EPS = 1e-6
NEG_INF = -1e30

D_IN = 9992
W_IN_SHARD = D_IN // N_DEV
W_IN_SHARD_PAD = 1280
D_IN_AL = 10752
DN_W, SW_W, GM_W, XM_W = 2560, 1536, 1536, 1024

ADAM_LR = 0.001
ADAM_B1 = 0.9
ADAM_B2 = 0.999
ADAM_EPS = 1e-08
ADAM_WD = 0.01
ADAM_STEP = 10

VMEM_LIMIT = 56 * 1024 * 1024

BF = jnp.bfloat16
F32 = jnp.float32
DN_C = 128
DN_D = 128
HALO = 8
BLK = 128
Y_DTYPE = BF


def _my_place():
    return lax.axis_index("x"), lax.axis_index("y"), lax.axis_index("c")


_ANY = pl.BlockSpec(memory_space=pl.ANY)


def _all_gather_slots(parts, name):
    n = len(parts)

    def body(*refs):
        p_refs, out_refs = refs[:n], refs[n:2 * n]
        send_sems, recv_sems, local_sems = refs[2 * n:]
        x, y, c = _my_place()
        me, sibling = (x, y, c), (x, y, 1 - c)
        chips = [(1 - x, y), (x, 1 - y), (1 - x, 1 - y)]

        def copy(a, k, block, to, src=None):
            px, py, pc = block
            slot = out_refs[a].at[4 * px + 2 * py + pc]
            return pltpu.make_async_remote_copy(
                src_ref=slot if src is None else src, dst_ref=slot,
                send_sem=send_sems.at[7 * a + k], recv_sem=recv_sems.at[7 * a + k],
                device_id=to, device_id_type=MESH)

        mine = [pltpu.make_async_copy(p_refs[a], out_refs[a].at[4 * x + 2 * y + c], local_sems.at[a])
                for a in range(n)]
        for cp in mine:
            cp.start()
        first = []
        for a in range(n):
            first.append(copy(a, 0, me, sibling, src=p_refs[a]))
            first += [copy(a, 1 + j, me, (*chip, c), src=p_refs[a]) for j, chip in enumerate(chips)]
        for cp in first:
            cp.start()
        passed = []
        for j, chip in enumerate(chips):
            for a in range(n):
                copy(a, 1 + j, (*chip, c), me).wait_recv()
                fwd = copy(a, 4 + j, (*chip, c), sibling)
                fwd.start()
                passed.append(fwd)
        for a in range(n):
            copy(a, 0, sibling, me).wait_recv()
            for j, chip in enumerate(chips):
                copy(a, 4 + j, (*chip, 1 - c), me).wait_recv()
        for cp in first + passed:
            cp.wait_send()
        for cp in mine:
            cp.wait()

    return pl.pallas_call(
        body, name=name,
        out_shape=[jax.ShapeDtypeStruct((N_DEV,) + p.shape, p.dtype) for p in parts],
        in_specs=[_ANY] * n, out_specs=[_ANY] * n,
        scratch_shapes=[pltpu.SemaphoreType.DMA((7 * n,)), pltpu.SemaphoreType.DMA((7 * n,)),
                        pltpu.SemaphoreType.DMA((n,))],
    )(*parts)


def _exchange_sibling(parts, name):
    n = len(parts)

    def body(*refs):
        g_refs, out_refs = refs[:n], refs[n:2 * n]
        send_sems, recv_sems = refs[2 * n:]
        x, y, c = _my_place()
        copies = [pltpu.make_async_remote_copy(
            src_ref=g_refs[a].at[:, 1 - c], dst_ref=out_refs[a],
            send_sem=send_sems.at[a], recv_sem=recv_sems.at[a],
            device_id=(x, y, 1 - c), device_id_type=MESH) for a in range(n)]
        for cp in copies:
            cp.start()
        for cp in copies:
            cp.wait()

    return pl.pallas_call(
        body, name=name,
        out_shape=[jax.ShapeDtypeStruct((4,) + g.shape[2:], g.dtype) for g in parts],
        in_specs=[_ANY] * n, out_specs=[_ANY] * n,
        scratch_shapes=[pltpu.SemaphoreType.DMA((n,)), pltpu.SemaphoreType.DMA((n,))],
    )(*parts)


_HBM = pl.BlockSpec(memory_space=pltpu.HBM)
_SEM = pl.BlockSpec(memory_space=pltpu.SEMAPHORE)
_EFFECT = pltpu.SideEffectType.DATAFLOW_SIDE_EFFECTING


def _peer(x, y, c, k):
    return (1 - x if (k >> 2) & 1 else x, 1 - y if (k >> 1) & 1 else y, 1 - c if k & 1 else c)


def _spread_start(srcs, mode, name):
    n = len(srcs)
    lands = [lax.empty((N_DEV,) + s.shape if mode == "gather" else s.shape, s.dtype) for s in srcs]
    peers = range(0, N_DEV, 2) if mode == "chips" else range(N_DEV)

    def body(*refs):
        src_refs, land_refs = refs[:n], refs[n:2 * n]
        send_sems, recv_sems = refs[2 * n:2 * n + 2]
        token = refs[-1]
        x, y, c = _my_place()
        for a in range(n):
            for k in peers:
                px, py, pc = _peer(x, y, c, k)
                if mode == "chips":
                    src, mine = src_refs[a].at[2 * px + py], 2 * x + y
                else:
                    src = src_refs[a].at[4 * px + 2 * py + pc] if mode == "scatter" else src_refs[a]
                    mine = 4 * x + 2 * y + c
                pltpu.make_async_remote_copy(
                    src_ref=src, dst_ref=land_refs[a].at[mine],
                    send_sem=send_sems.at[a], recv_sem=recv_sems.at[a],
                    device_id=(px, py, pc), device_id_type=MESH).start()
        token[...] = jnp.zeros_like(token)

    out = pl.pallas_call(
        body, name=name,
        out_shape=[pltpu.SemaphoreType.DMA((n,)), pltpu.SemaphoreType.DMA((n,))]
        + [pltpu.HBM(s.shape, s.dtype) for s in srcs] + [pltpu.HBM(l.shape, l.dtype) for l in lands]
        + [jax.ShapeDtypeStruct((8, 128), F32)],
        in_specs=[_HBM] * (2 * n),
        out_specs=[_SEM, _SEM] + [_HBM] * (2 * n) + [pl.BlockSpec(memory_space=pltpu.VMEM)],
        input_output_aliases={i: 2 + i for i in range(2 * n)},
        compiler_params=pltpu.CompilerParams(has_side_effects=_EFFECT),
    )(*[pltpu.with_memory_space_constraint(s, pltpu.HBM) for s in srcs],
      *[pltpu.with_memory_space_constraint(l, pltpu.HBM) for l in lands])
    return out[0], out[1], out[2:2 + n], out[2 + n:2 + 2 * n], out[-1]


def _spread_wait(send_sems, recv_sems, srcs, lands, which, after, name):
    n = len(srcs)

    def body(*refs):
        land_refs = refs[n:2 * n]
        send_sems, recv_sems = refs[2 * n:2 * n + 2]
        x, y, c = _my_place()
        for a in which:
            whole = pltpu.make_async_remote_copy(
                src_ref=land_refs[a], dst_ref=land_refs[a],
                send_sem=send_sems.at[a], recv_sem=recv_sems.at[a],
                device_id=(x, y, c), device_id_type=MESH)
            whole.wait_send()
            whole.wait_recv()

    out = pl.pallas_call(
        body, name=name,
        out_shape=[pltpu.HBM(s.shape, s.dtype) for s in srcs] + [pltpu.HBM(l.shape, l.dtype) for l in lands],
        in_specs=[_HBM] * (2 * n) + [_SEM, _SEM, _ANY],
        out_specs=[_HBM] * (2 * n),
        input_output_aliases={i: i for i in range(2 * n)},
        compiler_params=pltpu.CompilerParams(has_side_effects=_EFFECT),
    )(*srcs, *lands, send_sems, recv_sems, after)
    return out[:n], out[n:]


def _sum_slots(parts, name):
    def body(p_ref, o_ref):
        acc = p_ref[0]
        for s in range(1, parts.shape[0]):
            acc = acc + p_ref[s]
        o_ref[...] = acc

    return pl.pallas_call(body, name=name, out_shape=jax.ShapeDtypeStruct(parts.shape[1:], parts.dtype))(parts)


def _pick(n, pref):
    if n <= pref:
        return n
    t = pref - pref % 128
    while t > 0 and n % t:
        t -= 128
    return t if t > 0 else n


_DIMS = {"nn": (((1,), (0,)), ((), ())),
         "nt": (((1,), (1,)), ((), ())),
         "tn": (((0,), (0,)), ((), ()))}


def _matmul(a, b, mode, out_dtype, tiles, name):
    (m, k) = a.shape
    n = b.shape[1] if mode == "nn" else b.shape[0]
    tm, tn, tk = (_pick(d, t) for d, t in zip((m, n, k), tiles))
    nk = k // tk

    def product(a_ref, b_ref):
        return lax.dot_general(a_ref[...].astype(BF), b_ref[...].astype(BF), _DIMS[mode], preferred_element_type=F32)

    def body_whole_k(a_ref, b_ref, o_ref):
        o_ref[...] = product(a_ref, b_ref).astype(o_ref.dtype)

    def body_split_k(a_ref, b_ref, o_ref, acc_ref):
        kk = pl.program_id(2)

        @pl.when(kk == 0)
        def _():
            acc_ref[...] = jnp.zeros_like(acc_ref)

        acc_ref[...] += product(a_ref, b_ref)

        @pl.when(kk == nk - 1)
        def _():
            o_ref[...] = acc_ref[...].astype(o_ref.dtype)

    b_spec = (pl.BlockSpec((tn, tk), lambda i, j, kk: (j, kk)) if mode == "nt"
              else pl.BlockSpec((tk, tn), lambda i, j, kk: (kk, j)))
    return pl.pallas_call(
        body_whole_k if nk == 1 else body_split_k, name=name,
        out_shape=jax.ShapeDtypeStruct((m, n), out_dtype),
        grid=(m // tm, n // tn, nk),
        in_specs=[pl.BlockSpec((tm, tk), lambda i, j, kk: (i, kk)), b_spec],
        out_specs=pl.BlockSpec((tm, tn), lambda i, j, kk: (i, j)),
        scratch_shapes=[] if nk == 1 else [pltpu.VMEM((tm, tn), F32)],
        compiler_params=pltpu.CompilerParams(
            dimension_semantics=("parallel", "parallel", "arbitrary"),
            vmem_limit_bytes=VMEM_LIMIT),
    )(a, b)


def _rows2d(t, lead):
    return t.reshape(t.shape[:lead] + (math.prod(t.shape[lead:-1]), t.shape[-1]))


def _pair_sum(g, theirs, name):
    g3, t3 = _rows2d(g, 2), _rows2d(theirs, 1)
    _, r, w = t3.shape
    tr = _pick(r, 512)

    def body(g_ref, t_ref, o_ref):
        c = lax.axis_index("c")
        mine = jnp.where(c == 0, g_ref[0, 0], g_ref[0, 1])
        o_ref[0] = (mine.astype(F32) + t_ref[0].astype(F32)).astype(o_ref.dtype)

    out = pl.pallas_call(
        body, name=name,
        out_shape=jax.ShapeDtypeStruct(t3.shape, t3.dtype),
        grid=(4, r // tr),
        in_specs=[pl.BlockSpec((1, 2, tr, w), lambda q, i: (q, 0, i, 0)),
                  pl.BlockSpec((1, tr, w), lambda q, i: (q, i, 0))],
        out_specs=pl.BlockSpec((1, tr, w), lambda q, i: (q, i, 0)),
        compiler_params=pltpu.CompilerParams(dimension_semantics=("parallel", "parallel")),
    )(g3, t3)
    return out.reshape(theirs.shape)


def _adam_update(w, g, m, v):
    c1 = 1.0 - ADAM_B1 ** ADAM_STEP
    c2 = 1.0 - ADAM_B2 ** ADAM_STEP
    nm = ADAM_B1 * m + (1.0 - ADAM_B1) * g
    nv = ADAM_B2 * v + (1.0 - ADAM_B2) * (g * g)
    delta = -ADAM_LR * ((nm / c1) / (jnp.sqrt(nv / c2) + ADAM_EPS) + ADAM_WD * w)
    return delta, nm, nv


def _sum_adamw(parts, w, m, v, layer, carry, after, name):
    shape = w.shape
    cols = shape[-1]
    p3 = _rows2d(parts, 1)
    w3, m3, v3 = (_rows2d(t, 1) for t in (w, m, v))
    rows = w3.shape[1]
    tr = _pick(rows, 128)
    n_parts = p3.shape[0]

    def body(p_ref, w_ref, m_ref, v_ref, *rest):
        g_ref, d_ref, nm_ref, nv_ref = rest[-4:]
        g = p_ref[0, :, :cols].astype(F32)
        for q in range(1, n_parts):
            g = g + p_ref[q, :, :cols].astype(F32)
        d, nm, nv = _adam_update(w_ref[0], g, m_ref[0], v_ref[0])
        g_ref[0] = g
        d_ref[0] = d
        nm_ref[0] = nm
        nv_ref[0] = nv

    spec = pl.BlockSpec((1, tr, cols), lambda i: (layer, i, 0))
    extra = [] if carry is None else [_rows2d(t, 1) for t in carry]
    tail = [] if after is None else [after]
    out = pl.pallas_call(
        body, name=name,
        out_shape=[jax.ShapeDtypeStruct(w3.shape, F32)] * 4,
        grid=(rows // tr,),
        in_specs=[pl.BlockSpec((n_parts, tr, p3.shape[-1]), lambda i: (0, i, 0)), spec, spec, spec] + [_ANY] * len(extra + tail),
        out_specs=[spec] * 4,
        input_output_aliases={4 + i: i for i in range(len(extra))},
        compiler_params=pltpu.CompilerParams(dimension_semantics=("parallel",)),
    )(p3, w3, m3, v3, *extra, *tail)
    return tuple(t.reshape(shape) for t in out)


def _sum_adamw_t(parts, w, m, v, name):
    rows = parts[0].shape[2]
    tr = 128
    assert rows % tr == 0 and rows >= w.shape[0]

    def body(*refs):
        p_refs, (w_ref, m_ref, v_ref), (g_ref, d_ref, nm_ref, nv_ref) = refs[:DEPTH], refs[DEPTH:DEPTH + 3], refs[DEPTH + 3:]
        for l in range(DEPTH):
            g = p_refs[l][0].astype(F32)
            for q in range(1, p_refs[l].shape[0]):
                g = g + p_refs[l][q].astype(F32)
            g = g.T
            d, nm, nv = _adam_update(w_ref[:, l, :], g, m_ref[:, l, :], v_ref[:, l, :])
            g_ref[:, l, :] = g
            d_ref[:, l, :] = d
            nm_ref[:, l, :] = nm
            nv_ref[:, l, :] = nv

    spec = pl.BlockSpec((tr,) + w.shape[1:], lambda i: (i, 0, 0))
    return pl.pallas_call(
        body, name=name,
        out_shape=[jax.ShapeDtypeStruct(w.shape, F32)] * 4,
        grid=(rows // tr,),
        in_specs=[pl.BlockSpec((p.shape[0], p.shape[1], tr), lambda i: (0, 0, i)) for p in parts] + [spec] * 3,
        out_specs=[spec] * 4,
        compiler_params=pltpu.CompilerParams(dimension_semantics=("parallel",)),
    )(*parts, w, m, v)


def _adamw(w, g, m, v, name):
    rows, cols = w.shape
    tr = _pick(rows, 128)

    def body(w_ref, g_ref, m_ref, v_ref, d_ref, nm_ref, nv_ref):
        d, nm, nv = _adam_update(w_ref[...], g_ref[...], m_ref[...], v_ref[...])
        d_ref[...] = d
        nm_ref[...] = nm
        nv_ref[...] = nv

    spec = pl.BlockSpec((tr, cols), lambda i: (i, 0))
    return pl.pallas_call(
        body, name=name,
        out_shape=[jax.ShapeDtypeStruct((rows, cols), F32)] * 3,
        grid=(rows // tr,),
        in_specs=[spec] * 4, out_specs=[spec] * 3,
        compiler_params=pltpu.CompilerParams(dimension_semantics=("parallel",)),
    )(w, g, m, v)


_VJP = {"nn": (("nt", "gb"), ("tn", "ag")),
        "nt": (("nn", "gb"), ("tn", "ga")),
        "tn": (("nt", "bg"), ("nn", "ag"))}


def _make_dot(cast, precision):
    def raw(mode, a, b):
        return lax.dot_general(cast(a), cast(b), _DIMS[mode], precision=precision,
                               preferred_element_type=F32)

    @functools.partial(jax.custom_vjp, nondiff_argnums=(0,))
    def dot(mode, a, b):
        return raw(mode, a, b)

    def fwd(mode, a, b):
        return raw(mode, a, b), (a, b)

    def bwd(mode, res, g):
        a, b = res
        pick = {"a": a, "b": b, "g": g}
        (ma, ta), (mb, tb) = _VJP[mode]
        return dot(ma, pick[ta[0]], pick[ta[1]]), dot(mb, pick[tb[0]], pick[tb[1]])

    dot.defvjp(fwd, bwd)
    return dot


bdot = _make_dot(lambda t: t.astype(BF), None)
hdot = _make_dot(lambda t: t, lax.Precision.HIGHEST)


def _xdot(mode, a, b):
    return lax.dot_general(a, b, _DIMS[mode], precision=lax.Precision.HIGH, preferred_element_type=F32)


def _unit_lower_inverse(Ls):
    n = Ls[0].shape[0]
    batched = (((2,), (1,)), ((0,), (0,)))
    mm = lambda a, b: lax.dot_general(a, b, batched, precision=lax.Precision.HIGH, preferred_element_type=F32)
    eye = (lax.broadcasted_iota(jnp.int32, (n, n), 0) == lax.broadcasted_iota(jnp.int32, (n, n), 1)).astype(F32)
    p = jnp.stack(Ls)
    t_inv = eye[None] - p
    for _ in range(6):
        p = mm(p, p)
        t_inv = t_inv + mm(t_inv, p)
    return [t_inv[h] for h in range(len(Ls))]


@jax.custom_vjp
def _tri_solve(L, rhs, t_inv):
    return _xdot("nn", t_inv, rhs)


def _tri_solve_fwd(L, rhs, t_inv):
    sol = _xdot("nn", t_inv, rhs)
    return sol, (t_inv, sol)


def _tri_solve_bwd(res, dsol):
    t_inv, sol = res
    drhs = _xdot("tn", t_inv, dsol)
    return -_xdot("nt", drhs, sol), drhs, jnp.zeros_like(t_inv)


_tri_solve.defvjp(_tri_solve_fwd, _tri_solve_bwd)


def _sigmoid(x):
    return 1.0 / (1.0 + jnp.exp(-x))


def _softplus(x):
    return jnp.maximum(x, 0.0) + jnp.log(1.0 + jnp.exp(-jnp.abs(x)))


def _dn_chunk(S, xs, ba, z, cw, al, dt, dn, t_saved=None):
    C = DN_C
    pre = xs[0] * cw[0] + xs[1] * cw[1] + xs[2] * cw[2] + xs[3] * cw[3]
    qkv = pre * _sigmoid(pre)
    lane = lax.broadcasted_iota(jnp.int32, (1, 128), 1)
    sub = lax.broadcasted_iota(jnp.int32, (C, 1), 0)
    row_i = lax.broadcasted_iota(jnp.int32, (C, C), 0)
    col_i = lax.broadcasted_iota(jnp.int32, (C, C), 1)
    strict = row_i > col_i
    incl = row_i >= col_i
    g_all = jnp.where((lane >= 4) & (lane < 8), -jnp.exp(al) * _softplus(ba + dt), 0.0)
    gc_all = hdot("nn", incl.astype(F32), g_all)
    gc_all_t = gc_all.T
    beta_all = _sigmoid(ba)
    glast_all = jnp.sum(jnp.where(sub == C - 1, gc_all, 0.0), axis=0, keepdims=True)
    heads = []
    for h in range(DN_HEADS):
        q = qkv[:, 128 * h:128 * (h + 1)]
        k = qkv[:, 512 + 128 * h:512 + 128 * (h + 1)]
        v = qkv[:, 1024 + 128 * h:1024 + 128 * (h + 1)]
        q = q * lax.rsqrt(jnp.sum(q * q, axis=1, keepdims=True) + EPS) * (DN_D ** -0.5)
        k = k * lax.rsqrt(jnp.sum(k * k, axis=1, keepdims=True) + EPS)
        beta = jnp.sum(jnp.where(lane == h, beta_all, 0.0), axis=1, keepdims=True)
        gc = jnp.sum(jnp.where(lane == 4 + h, gc_all, 0.0), axis=1, keepdims=True)
        gc_row = jnp.sum(jnp.where(sub == 4 + h, gc_all_t, 0.0), axis=0, keepdims=True)
        g_last = jnp.sum(jnp.where(lane == 4 + h, glast_all, 0.0), axis=1, keepdims=True)
        diff = gc - gc_row
        kb = k * beta
        L = jnp.where(strict, bdot("nt", kb, k) * jnp.exp(jnp.where(strict, diff, 0.0)), 0.0)
        heads.append((q, k, v, beta, gc, g_last, diff, kb, L))
    t_invs = _unit_lower_inverse([hd[-1] for hd in heads]) if t_saved is None else t_saved
    ys, s_new = [], []
    for h, (q, k, v, beta, gc, g_last, diff, kb, L) in enumerate(heads):
        sol = _tri_solve(L, jnp.concatenate([v * beta, kb * jnp.exp(gc)], axis=1), t_invs[h])
        u, w = sol[:, :DN_D], sol[:, DN_D:]
        a_qk = jnp.where(incl, bdot("nt", q, k) * jnp.exp(jnp.where(incl, diff, 0.0)), 0.0)
        qg = q * jnp.exp(gc)
        kd = k * jnp.exp(g_last - gc)
        v_new = u - bdot("nn", w, S[h])
        o = bdot("nn", qg, S[h]) + bdot("nn", a_qk, v_new)
        s_new.append(S[h] * jnp.exp(g_last) + bdot("tn", kd, v_new))
        o = o * lax.rsqrt(jnp.mean(o * o, axis=1, keepdims=True) + EPS) * dn
        zh = z[:, 128 * h:128 * (h + 1)]
        ys.append(o * (zh * _sigmoid(zh)))
    return jnp.concatenate(ys, axis=1), tuple(s_new), tuple(t_invs)


def _load_shifted(xbuf, x_ref, halo_ref, first):
    xbuf[0:HALO, :] = jnp.where(first, 0.0, halo_ref[:, 0:1536])
    xbuf[HALO:HALO + DN_C, :] = x_ref[:, 0:1536]
    return [xbuf[HALO - 3 + k:HALO - 3 + k + DN_C, :] for k in range(4)]


def dn_forward(cols, jblk, cw, al, dt, dn, name):
    T = cols.shape[0]
    n = T // DN_C

    def body(x_ref, halo_ref, cw_ref, al_ref, dt_ref, dn_ref, y_ref, ss_ref, ts_ref, s_scr, xbuf):
        i = pl.program_id(0)

        @pl.when(i == 0)
        def _():
            s_scr[...] = jnp.zeros_like(s_scr)

        xs = _load_shifted(xbuf, x_ref, halo_ref, i == 0)
        ss_ref[0] = s_scr[...]
        S = [s_scr[h] for h in range(DN_HEADS)]
        cws = [cw_ref[k:k + 1, :] for k in range(4)]
        y, s_new, t_invs = _dn_chunk(S, xs, x_ref[:, 2048:2176], x_ref[:, 1536:2048], cws,
                                     al_ref[...], dt_ref[...], dn_ref[...])
        y_ref[...] = y.astype(y_ref.dtype)
        for h in range(DN_HEADS):
            s_scr[h] = s_new[h]
            ts_ref[0, h] = t_invs[h]

    per = DN_C // HALO
    full = lambda shape: pl.BlockSpec(shape, lambda i: (0,) * len(shape))
    return pl.pallas_call(
        body, name=name,
        out_shape=[jax.ShapeDtypeStruct((T, 512), Y_DTYPE),
                   jax.ShapeDtypeStruct((n, DN_HEADS, DN_D, DN_D), F32),
                   jax.ShapeDtypeStruct((n, DN_HEADS, DN_D, DN_D), F32)],
        grid=(n,),
        in_specs=[pl.BlockSpec((DN_C, DN_W), lambda i: (i, jblk)),
                  pl.BlockSpec((HALO, DN_W), lambda i: (jnp.maximum(i * per - 1, 0), jblk)),
                  full((4, 1536)), full((1, 128)), full((1, 128)), full((1, 128))],
        out_specs=[pl.BlockSpec((DN_C, 512), lambda i: (i, 0)),
                   pl.BlockSpec((1, DN_HEADS, DN_D, DN_D), lambda i: (i, 0, 0, 0)),
                   pl.BlockSpec((1, DN_HEADS, DN_D, DN_D), lambda i: (i, 0, 0, 0))],
        scratch_shapes=[pltpu.VMEM((DN_HEADS, DN_D, DN_D), F32), pltpu.VMEM((HALO + DN_C, 1536), F32)],
        compiler_params=pltpu.CompilerParams(dimension_semantics=("arbitrary",)),
    )(cols, cols, cw, al, dt, dn)


def dn_backward(cols, jblk, cw, al, dt, dn, ss, ts, dy, dcols, name):
    T = cols.shape[0]
    n = T // DN_C

    def body(x_ref, halo_ref, cw_ref, al_ref, dt_ref, dn_ref, ss_ref, ts_ref, dy_ref, dcols_in,
             dx_ref, dcw_ref, dal_ref, ddt_ref, ddn_ref, ds_scr, xbuf, dbuf, carry):
        i = pl.program_id(0)

        @pl.when(i == 0)
        def _():
            ds_scr[...] = jnp.zeros_like(ds_scr)
            carry[...] = jnp.zeros_like(carry)
            dcw_ref[...] = jnp.zeros_like(dcw_ref)
            dal_ref[...] = jnp.zeros_like(dal_ref)
            ddt_ref[...] = jnp.zeros_like(ddt_ref)
            ddn_ref[...] = jnp.zeros_like(ddn_ref)

        xs = _load_shifted(xbuf, x_ref, halo_ref, i == n - 1)
        S = [ss_ref[0, h] for h in range(DN_HEADS)]
        cws = [cw_ref[k:k + 1, :] for k in range(4)]

        t_saved = [ts_ref[0, h] for h in range(DN_HEADS)]

        def f(S, xs, ba, z, cws, al, dt, dn):
            return _dn_chunk(S, xs, ba, z, cws, al, dt, dn, t_saved)[:2]

        _, vjp = jax.vjp(f, S, xs, x_ref[:, 2048:2176], x_ref[:, 1536:2048], cws, al_ref[...], dt_ref[...], dn_ref[...])
        dS, dxs, dba, dz, dcws, dal, ddt, ddn = vjp((dy_ref[...], tuple(ds_scr[h] for h in range(DN_HEADS))))
        for h in range(DN_HEADS):
            ds_scr[h] = dS[h]
        dbuf[...] = jnp.zeros_like(dbuf)
        for k in range(4):
            lo = HALO - 3 + k
            dbuf[lo:lo + DN_C, :] += dxs[k]
        dbuf[DN_C:DN_C + HALO, :] += carry[...]
        dx_ref[...] = jnp.concatenate([dbuf[HALO:HALO + DN_C, :], dz, dba,
                                       jnp.zeros((DN_C, DN_W - 2176), F32)], axis=1).astype(dx_ref.dtype)
        carry[...] = dbuf[0:HALO, :]
        for k in range(4):
            dcw_ref[k:k + 1, :] += dcws[k]
        dal_ref[...] += dal
        ddt_ref[...] += ddt
        ddn_ref[...] += ddn

    per = DN_C // HALO
    rev = lambda i: n - 1 - i
    full = lambda shape: pl.BlockSpec(shape, lambda i: (0,) * len(shape))
    return pl.pallas_call(
        body, name=name,
        out_shape=[jax.ShapeDtypeStruct(dcols.shape, dcols.dtype),jax.ShapeDtypeStruct((4, 1536), F32),
                   jax.ShapeDtypeStruct((1, 128), F32), jax.ShapeDtypeStruct((1, 128), F32),
                   jax.ShapeDtypeStruct((1, 128), F32)],
        grid=(n,),
        in_specs=[pl.BlockSpec((DN_C, DN_W), lambda i: (rev(i), jblk)),
                  pl.BlockSpec((HALO, DN_W), lambda i: (jnp.maximum(rev(i) * per - 1, 0), jblk)),
                  full((4, 1536)), full((1, 128)), full((1, 128)), full((1, 128)),
                  pl.BlockSpec((1, DN_HEADS, DN_D, DN_D), lambda i: (rev(i), 0, 0, 0)),
                  pl.BlockSpec((1, DN_HEADS, DN_D, DN_D), lambda i: (rev(i), 0, 0, 0)),
                  pl.BlockSpec((DN_C, 512), lambda i: (rev(i), 0)), _ANY],
        out_specs=[pl.BlockSpec((DN_C, DN_W), lambda i: (rev(i), jblk)),
                   full((4, 1536)), full((1, 128)), full((1, 128)), full((1, 128))],
        scratch_shapes=[pltpu.VMEM((DN_HEADS, DN_D, DN_D), F32), pltpu.VMEM((HALO + DN_C, 1536), F32),
                        pltpu.VMEM((HALO + DN_C, 1536), F32), pltpu.VMEM((HALO, 1536), F32)],
        input_output_aliases={9: 0},
        compiler_params=pltpu.CompilerParams(dimension_semantics=("arbitrary",)),
    )(cols, cols, cw, al, dt, dn, ss, ts, dy, dcols)


def _full(shape):
    return pl.BlockSpec(shape, lambda i: (0,) * len(shape))


def _silu(x):
    return x * _sigmoid(x)


def _gelu(x):
    return 0.5 * x * (1.0 + jnp.tanh(0.7978845608028654 * (x + 0.044715 * (x * x * x))))


def _lane_col(mat, idx):
    lane = lax.broadcasted_iota(jnp.int32, (1, mat.shape[1]), 1)
    return jnp.sum(jnp.where(lane == idx, mat, 0.0), axis=1, keepdims=True)


def _gm_chunk(uv, z, gain, ws, bt):
    g = _gelu(uv)
    u, v = g[:, :512], g[:, 512:]
    v = v * lax.rsqrt(jnp.mean(v * v, axis=1, keepdims=True) + EPS) * gain
    row_i = lax.broadcasted_iota(jnp.int32, (BLK, BLK), 0)
    col_i = lax.broadcasted_iota(jnp.int32, (BLK, BLK), 1)
    causal = row_i >= col_i
    ss = []
    for grp in range(4):
        wg = jnp.where(causal, ws[grp], 0.0)
        ss.append(bdot("nn", wg, v[:, BLK * grp:BLK * (grp + 1)]) + _lane_col(bt, grp))
    return u * jnp.concatenate(ss, axis=1) * _silu(z)


def gm_forward(cols, jblk, gain, ws, bt, name):
    T = cols.shape[0]

    def body(x_ref, gain_ref, ws_ref, bt_ref, y_ref):
        y_ref[...] = _gm_chunk(x_ref[:, 0:1024], x_ref[:, 1024:1536], gain_ref[...],
                               [ws_ref[g] for g in range(4)], bt_ref[...]).astype(y_ref.dtype)

    return pl.pallas_call(
        body, name=name, out_shape=jax.ShapeDtypeStruct((T, 512), Y_DTYPE), grid=(T // BLK,),
        in_specs=[pl.BlockSpec((BLK, GM_W), lambda i: (i, jblk)),
                  _full((1, 512)), _full((4, BLK, BLK)), _full((BLK, BLK))],
        out_specs=pl.BlockSpec((BLK, 512), lambda i: (i, 0)),
        compiler_params=pltpu.CompilerParams(dimension_semantics=("parallel",)),
    )(cols, gain, ws, bt)


def gm_backward(cols, jblk, gain, ws, bt, dy, dcols, name):
    T = cols.shape[0]

    def body(x_ref, gain_ref, ws_ref, bt_ref, dy_ref, dcols_in, dx_ref, dgain_ref, dws_ref, dbt_ref):
        @pl.when(pl.program_id(0) == 0)
        def _():
            dgain_ref[...] = jnp.zeros_like(dgain_ref)
            dws_ref[...] = jnp.zeros_like(dws_ref)
            dbt_ref[...] = jnp.zeros_like(dbt_ref)

        _, vjp = jax.vjp(_gm_chunk, x_ref[:, 0:1024], x_ref[:, 1024:1536], gain_ref[...],
                         [ws_ref[g] for g in range(4)], bt_ref[...])
        duv, dz, dgain, dws, dbt = vjp(dy_ref[...])
        dx_ref[...] = jnp.concatenate([duv, dz], axis=1).astype(dx_ref.dtype)
        dgain_ref[...] += dgain
        for g in range(4):
            dws_ref[g] += dws[g]
        dbt_ref[...] += dbt

    return pl.pallas_call(
        body, name=name,
        out_shape=[jax.ShapeDtypeStruct(dcols.shape, dcols.dtype),jax.ShapeDtypeStruct((1, 512), F32),
                   jax.ShapeDtypeStruct((4, BLK, BLK), F32), jax.ShapeDtypeStruct((BLK, BLK), F32)],
        grid=(T // BLK,),
        in_specs=[pl.BlockSpec((BLK, GM_W), lambda i: (i, jblk)),
                  _full((1, 512)), _full((4, BLK, BLK)), _full((BLK, BLK)),
                  pl.BlockSpec((BLK, 512), lambda i: (i, 0)), _ANY],
        out_specs=[pl.BlockSpec((BLK, GM_W), lambda i: (i, jblk)),
                   _full((1, 512)), _full((4, BLK, BLK)), _full((BLK, BLK))],
        input_output_aliases={5: 0},
        compiler_params=pltpu.CompilerParams(dimension_semantics=("arbitrary",)),
    )(cols, gain, ws, bt, dy, dcols)


def _sw_block(first, q, kp, kc, vp, vc, z, sinks):
    P = BLK
    lane = lax.broadcasted_iota(jnp.int32, (1, 128), 1)
    r = lax.broadcasted_iota(jnp.int32, (128, 128), 0)
    c = lax.broadcasted_iota(jnp.int32, (128, 128), 1)
    swap = (c == (r + 64) % 128).astype(F32)
    k2 = jnp.concatenate([kp, kc], axis=0)
    v2 = jnp.concatenate([vp, vc], axis=0)
    k2s = bdot("nn", k2, swap)
    v2s = bdot("nn", v2, swap)
    qi = lax.broadcasted_iota(jnp.int32, (P, 2 * P), 0)
    kj = lax.broadcasted_iota(jnp.int32, (P, 2 * P), 1)
    dist = qi + P - kj
    valid = (dist >= 0) & (dist < P) & ((kj >= P) | jnp.logical_not(first))
    outs = []
    for j in range(4):
        acc = jnp.zeros((P, 128), F32)
        for half in range(2):
            h = 2 * j + half
            kv = h // 4
            in_half = (lane >= 64 * half) & (lane < 64 * half + 64)
            qh = jnp.where(in_half, q[:, 128 * j:128 * (j + 1)], 0.0)
            same = (half == kv)
            s = bdot("nt", qh, k2 if same else k2s) * (64 ** -0.5)
            s = jnp.where(valid, s, NEG_INF)
            sink = _lane_col(sinks, h)
            m = lax.stop_gradient(jnp.maximum(jnp.max(s, axis=1, keepdims=True), sink))
            e = jnp.exp(s - m)
            p = e / (jnp.sum(e, axis=1, keepdims=True) + jnp.exp(sink - m))
            o = bdot("nn", p, v2 if same else v2s)
            acc = acc + jnp.where(in_half, o, 0.0)
        outs.append(acc)
    return jnp.concatenate(outs, axis=1) * _silu(z)


def _sw_specs(jblk, idx):
    prev = lambda i: jnp.maximum(idx(i) - 1, 0)
    jk = (jblk * SW_W + 1024) // 128
    return [pl.BlockSpec((BLK, SW_W), lambda i: (idx(i), jblk)),
            pl.BlockSpec((BLK, 128), lambda i: (prev(i), jk)),
            pl.BlockSpec((BLK, 128), lambda i: (prev(i), jk + 1)), _full((1, 128))]


def sw_forward(cols, jblk, sinks, name):
    T = cols.shape[0]

    def body(x_ref, kp_ref, vp_ref, s_ref, y_ref):
        y_ref[...] = _sw_block(pl.program_id(0) == 0, x_ref[:, 0:512], kp_ref[...], x_ref[:, 1024:1152],
                               vp_ref[...], x_ref[:, 1152:1280], x_ref[:, 512:1024], s_ref[...]).astype(y_ref.dtype)

    return pl.pallas_call(
        body, name=name, out_shape=jax.ShapeDtypeStruct((T, 512), Y_DTYPE), grid=(T // BLK,),
        in_specs=_sw_specs(jblk, lambda i: i),
        out_specs=pl.BlockSpec((BLK, 512), lambda i: (i, 0)),
        compiler_params=pltpu.CompilerParams(dimension_semantics=("parallel",)),
    )(cols, cols, cols, sinks)


def sw_backward(cols, jblk, sinks, dy, dcols, name):
    T = cols.shape[0]
    n = T // BLK
    rev = lambda i: n - 1 - i

    def body(x_ref, kp_ref, vp_ref, s_ref, dy_ref, dcols_in, dx_ref, ds_ref, kcarry, vcarry):
        i = pl.program_id(0)

        @pl.when(i == 0)
        def _():
            kcarry[...] = jnp.zeros_like(kcarry)
            vcarry[...] = jnp.zeros_like(vcarry)
            ds_ref[...] = jnp.zeros_like(ds_ref)

        f = functools.partial(_sw_block, i == n - 1)
        _, vjp = jax.vjp(f, x_ref[:, 0:512], kp_ref[...], x_ref[:, 1024:1152], vp_ref[...], x_ref[:, 1152:1280],
                         x_ref[:, 512:1024], s_ref[...])
        dq, dkp, dkc, dvp, dvc, dz, dsk = vjp(dy_ref[...])
        dx_ref[...] = jnp.concatenate([dq, dz, dkc + kcarry[...], dvc + vcarry[...],
                                       jnp.zeros((BLK, SW_W - 1280), F32)], axis=1).astype(dx_ref.dtype)
        kcarry[...] = dkp
        vcarry[...] = dvp
        ds_ref[...] += dsk

    return pl.pallas_call(
        body, name=name,
        out_shape=[jax.ShapeDtypeStruct(dcols.shape, dcols.dtype),jax.ShapeDtypeStruct((1, 128), F32)],
        grid=(n,),
        in_specs=_sw_specs(jblk, rev) + [pl.BlockSpec((BLK, 512), lambda i: (rev(i), 0)), _ANY],
        out_specs=[pl.BlockSpec((BLK, SW_W), lambda i: (rev(i), jblk)), _full((1, 128))],
        scratch_shapes=[pltpu.VMEM((BLK, 128), F32), pltpu.VMEM((BLK, 128), F32)],
        input_output_aliases={5: 0},
        compiler_params=pltpu.CompilerParams(dimension_semantics=("arbitrary",)),
    )(cols, cols, cols, sinks, dy, dcols)


XM_TQ = 512


def _xm_block(q, z, mkv):
    outs = []
    for h in range(4):
        s = bdot("nt", q[:, 128 * h:128 * (h + 1)], mkv[:, 128 * h:128 * (h + 1)]) * (128 ** -0.5)
        m = lax.stop_gradient(jnp.max(s, axis=1, keepdims=True))
        e = jnp.exp(s - m)
        p = e / jnp.sum(e, axis=1, keepdims=True)
        outs.append(bdot("nn", p, mkv[:, 512 + 128 * h:512 + 128 * (h + 1)]))
    return jnp.concatenate(outs, axis=1) * _silu(z)


def xm_forward(cols, jblk, mkv, name):
    T = cols.shape[0]

    def body(x_ref, m_ref, y_ref):
        y_ref[...] = _xm_block(x_ref[:, 0:512], x_ref[:, 512:1024], m_ref[...]).astype(y_ref.dtype)

    return pl.pallas_call(
        body, name=name, out_shape=jax.ShapeDtypeStruct((T, 512), Y_DTYPE), grid=(T // XM_TQ,),
        in_specs=[pl.BlockSpec((XM_TQ, XM_W), lambda i: (i, jblk)), _full(mkv.shape)],
        out_specs=pl.BlockSpec((XM_TQ, 512), lambda i: (i, 0)),
        compiler_params=pltpu.CompilerParams(dimension_semantics=("parallel",)),
    )(cols, mkv)


def xm_backward(cols, jblk, mkv, dy, dcols, name):
    T = cols.shape[0]

    def body(x_ref, m_ref, dy_ref, dcols_in, dx_ref, dm_ref):
        @pl.when(pl.program_id(0) == 0)
        def _():
            dm_ref[...] = jnp.zeros_like(dm_ref)

        _, vjp = jax.vjp(_xm_block, x_ref[:, 0:512], x_ref[:, 512:1024], m_ref[...])
        dq, dz, dm = vjp(dy_ref[...])
        dx_ref[...] = jnp.concatenate([dq, dz], axis=1).astype(dx_ref.dtype)
        dm_ref[...] += dm

    return pl.pallas_call(
        body, name=name,
        out_shape=[jax.ShapeDtypeStruct(dcols.shape, dcols.dtype),jax.ShapeDtypeStruct(mkv.shape, F32)],
        grid=(T // XM_TQ,),
        in_specs=[pl.BlockSpec((XM_TQ, XM_W), lambda i: (i, jblk)), _full(mkv.shape),
                  pl.BlockSpec((XM_TQ, 512), lambda i: (i, 0)), _ANY],
        out_specs=[pl.BlockSpec((XM_TQ, XM_W), lambda i: (i, jblk)), _full(mkv.shape)],
        input_output_aliases={3: 0},
        compiler_params=pltpu.CompilerParams(dimension_semantics=("arbitrary",)),
    )(cols, mkv, dy, dcols)


def _rms(x, gain):
    return x * lax.rsqrt(jnp.mean(x * x, axis=1, keepdims=True) + EPS) * gain


def memkv_forward(mem, gain, w, name):
    def body(m_ref, g_ref, w_ref, o_ref):
        o_ref[...] = bdot("nn", _rms(m_ref[...], g_ref[...]), w_ref[...])

    return pl.pallas_call(body, name=name, out_shape=jax.ShapeDtypeStruct(mem.shape, F32),
                          compiler_params=pltpu.CompilerParams(vmem_limit_bytes=VMEM_LIMIT))(mem, gain, w)


def memkv_backward(mem, gain, w, dkv, name):
    def body(m_ref, g_ref, w_ref, d_ref, dg_ref, dw_ref):
        mem_v = m_ref[...]
        _, vjp = jax.vjp(lambda g, ww: bdot("nn", _rms(mem_v, g), ww), g_ref[...], w_ref[...].astype(F32))
        dg, dw = vjp(d_ref[...])
        dg_ref[...] = dg
        dw_ref[...] = dw

    return pl.pallas_call(body, name=name,
                          out_shape=[jax.ShapeDtypeStruct(gain.shape, F32), jax.ShapeDtypeStruct(w.shape, F32)],
                          compiler_params=pltpu.CompilerParams(vmem_limit_bytes=VMEM_LIMIT))(mem, gain, w, dkv)


MG_TB = 256


def _merge_block(ys, gl, wup, wout, gpost):
    merged = None
    for n in range(4):
        t = _sigmoid(gl[:, 1024 * n:1024 * (n + 1)]) * bdot("nn", ys[n], wup[n])
        merged = t if merged is None else merged + t
    out = bdot("nn", merged, wout)
    return _rms(out, gpost)


def merge_forward(ys, cols, jgate, x, wup, wout, gpost, name, next_gain=None, target=None):
    T, D = x.shape
    TB = 256
    n_extra = (next_gain is not None) + (target is not None)

    def body(ya, yb, yc, ym, gl_ref, x_ref, wup_ref, wout_ref, gp_ref, *rest):
        extra, outs = rest[:n_extra], rest[n_extra:]
        upd = _merge_block([ya[...], yb[...], yc[...], ym[...]], gl_ref[...],
                           [wup_ref[n] for n in range(4)], wout_ref[...], gp_ref[...])
        y = x_ref[...] + upd
        outs[0][...] = y
        outs = outs[1:]
        if next_gain is not None:
            h = _rms(y, extra[0][...])
            outs[0][...] = h.astype(BF)
            outs[1][...] = h.T.astype(BF)
            outs = outs[2:]
        if target is not None:
            l_ref, d_ref = outs

            @pl.when(pl.program_id(0) == 0)
            def _():
                l_ref[...] = jnp.zeros_like(l_ref)

            err = y - extra[-1][...]
            d_ref[...] = err * (1.0 / D)
            l_ref[...] += jnp.full(l_ref.shape, 0.5 * jnp.sum(jnp.mean(err * err, axis=1, keepdims=True)), F32)

    yspec = pl.BlockSpec((TB, 512), lambda i: (i, 0))
    xspec = pl.BlockSpec((TB, D), lambda i: (i, 0))
    extra_in, extra_specs = [], []
    out_shape, out_specs = [jax.ShapeDtypeStruct((T, D), F32)], [xspec]
    if next_gain is not None:
        extra_in, extra_specs = extra_in + [next_gain], extra_specs + [_full((1, D))]
        out_shape += [jax.ShapeDtypeStruct((T, D), BF), jax.ShapeDtypeStruct((D, T), BF)]
        out_specs += [xspec, pl.BlockSpec((D, TB), lambda i: (0, i))]
    if target is not None:
        extra_in, extra_specs = extra_in + [target], extra_specs + [xspec]
        out_shape += [jax.ShapeDtypeStruct((1, 128), F32), jax.ShapeDtypeStruct((T, D), F32)]
        out_specs += [_full((1, 128)), xspec]
    return pl.pallas_call(
        body, name=name, out_shape=out_shape, grid=(T // TB,),
        in_specs=[yspec] * 4 + [pl.BlockSpec((TB, 4096), lambda i: (i, jgate)), xspec,
                                _full(wup.shape), _full(wout.shape), _full((1, D))] + extra_specs,
        out_specs=out_specs,
        compiler_params=pltpu.CompilerParams(
            dimension_semantics=("parallel" if target is None else "arbitrary",), vmem_limit_bytes=VMEM_LIMIT),
    )(*ys, cols, x, wup, wout, gpost, *extra_in)


def _token_product(a, b, name):
    (T, m), n = a.shape, b.shape[1]

    def body(a_ref, b_ref, o_ref):
        o_ref[...] = lax.dot_general(a_ref[...].astype(BF), b_ref[...].astype(BF), _DIMS["tn"], preferred_element_type=F32)

    return pl.pallas_call(body, name=name, out_shape=jax.ShapeDtypeStruct((m, n), F32),
                          compiler_params=pltpu.CompilerParams(vmem_limit_bytes=VMEM_LIMIT))(a, b)


def merge_backward(ys, cols, jgate, wup, wout, gpost, dx, name):
    T = dx.shape[0]
    TB = MG_TB

    def body(ya, yb, yc, ym, gl_ref, wup_ref, wout_ref, gp_ref, dx_ref,
             dgl_ref, dya, dyb, dyc, dym, dpa, dpb, dpc, dpm, merged_ref, dout_ref, dgp_ref):
        @pl.when(pl.program_id(0) == 0)
        def _():
            dgp_ref[...] = jnp.zeros_like(dgp_ref)

        y_refs = (ya, yb, yc, ym)
        gates = [_sigmoid(gl_ref[:, 1024 * n:1024 * (n + 1)]) for n in range(4)]
        projs = [bdot("nn", y_refs[n][...], wup_ref[n]) for n in range(4)]
        merged = gates[0] * projs[0] + gates[1] * projs[1] + gates[2] * projs[2] + gates[3] * projs[3]
        out = bdot("nn", merged, wout_ref[...])
        _, vjp = jax.vjp(_rms, out, gp_ref[...])
        dout, dgp = vjp(dx_ref[...])
        dmerged = bdot("nt", dout, wout_ref[...])
        for n, (dy_ref, dp_ref) in enumerate(zip((dya, dyb, dyc, dym), (dpa, dpb, dpc, dpm))):
            dproj = dmerged * gates[n]
            dgl_ref[:, 1024 * n:1024 * (n + 1)] = (dmerged * projs[n] * gates[n] * (1.0 - gates[n])).astype(dgl_ref.dtype)
            dy_ref[...] = bdot("nt", dproj, wup_ref[n])
            dp_ref[...] = dproj.astype(BF)
        merged_ref[...] = merged.astype(BF)
        dout_ref[...] = dout.astype(BF)
        dgp_ref[...] += dgp

    yspec = pl.BlockSpec((TB, 512), lambda i: (i, 0))
    dspec = pl.BlockSpec((TB, 1024), lambda i: (i, 0))
    dcols, dya, dyb, dyc, dym, *dproj, merged, dout, dgp = pl.pallas_call(
        body, name=name,
        out_shape=[jax.ShapeDtypeStruct(cols.shape, BF)] + [jax.ShapeDtypeStruct((T, 512), F32)] * 4 + [
            jax.ShapeDtypeStruct((T, 1024), BF)] * 6 + [jax.ShapeDtypeStruct((1, 1024), F32)],
        grid=(T // TB,),
        in_specs=[yspec] * 4 + [pl.BlockSpec((TB, 4096), lambda i: (i, jgate)),
                                _full(wup.shape), _full(wout.shape), _full((1, 1024)), dspec],
        out_specs=[pl.BlockSpec((TB, 4096), lambda i: (i, jgate))] + [yspec] * 4 + [
            dspec] * 6 + [_full((1, 1024))],
        compiler_params=pltpu.CompilerParams(dimension_semantics=("arbitrary",), vmem_limit_bytes=VMEM_LIMIT),
    )(*ys, cols, wup, wout, gpost, dx)
    dwup = jnp.stack([_token_product(ys[n], dproj[n], "%s_w_up%d" % (name, n)) for n in range(4)])
    dwout = _token_product(merged, dout, name + "_w_out")
    return dcols, dya, dyb, dyc, dym, dwup, dwout, dgp


NB = 256


def prenorm_forward(x, gain, name):
    T, D = x.shape

    def body(x_ref, g_ref, o_ref, ot_ref):
        h = _rms(x_ref[...], g_ref[...])
        o_ref[...] = h.astype(BF)
        ot_ref[...] = h.T.astype(BF)

    return pl.pallas_call(
        body, name=name,
        out_shape=[jax.ShapeDtypeStruct((T, D), BF), jax.ShapeDtypeStruct((D, T), BF)], grid=(T // NB,),
        in_specs=[pl.BlockSpec((NB, D), lambda i: (i, 0)), _full((1, D))],
        out_specs=[pl.BlockSpec((NB, D), lambda i: (i, 0)), pl.BlockSpec((D, NB), lambda i: (0, i))],
        compiler_params=pltpu.CompilerParams(dimension_semantics=("parallel",)),
    )(x, gain)


def prenorm_backward(x, gain, dh, dres, name):
    T = x.shape[0]

    def body(x_ref, g_ref, dh_ref, dr_ref, dx_ref, dg_ref):
        @pl.when(pl.program_id(0) == 0)
        def _():
            dg_ref[...] = jnp.zeros_like(dg_ref)

        _, vjp = jax.vjp(_rms, x_ref[...], g_ref[...])
        dxn, dg = vjp(dh_ref[...])
        dx_ref[...] = dr_ref[...] + dxn
        dg_ref[...] += dg

    spec = pl.BlockSpec((NB, 1024), lambda i: (i, 0))
    return pl.pallas_call(
        body, name=name,
        out_shape=[jax.ShapeDtypeStruct(x.shape, F32), jax.ShapeDtypeStruct((1, 1024), F32)], grid=(T // NB,),
        in_specs=[spec, _full((1, 1024)), spec, spec], out_specs=[spec, _full((1, 1024))],
        compiler_params=pltpu.CompilerParams(dimension_semantics=("arbitrary",)),
    )(x, gain, dh, dres)


JB_GATE, JB_XM, JB_DN, JB_SW, JB_GM = 0, 4, 2, 5, 6
_ALIGNED_PIECES = ((5896, 4096), (4872, 512), (5384, 512), (0, 2048), (2048, 8), 504, (3592, 512), (4360, 512),
                   (4104, 128), (4232, 128), 256, (2056, 1024), (3080, 512))
_NATURAL_FROM_ALIGNED = ((5120, 2048), (7168, 8), (9216, 1024), (10240, 512), (7680, 512), (8704, 128), (8832, 128),
                         (8192, 512), (4096, 512), (4608, 512), (0, 4096))


def _natural_range(slots, start, width):
    out = []
    while width > 0:
        j, i = divmod(start, W_IN_SHARD)
        take = min(width, W_IN_SHARD - i)
        out.append(slots[j, :, i:i + take])
        start, width = start + take, width - take
    return out


def _aligned_w_in(slots):
    parts = []
    for piece in _ALIGNED_PIECES:
        if isinstance(piece, int):
            parts.append(jnp.zeros(slots.shape[1:2] + (piece,), slots.dtype))
        else:
            parts += _natural_range(slots, *piece)
    return jnp.concatenate(parts, axis=-1)


def _slots_of_aligned(d_al):
    slots = []
    for s in range(N_DEV):
        lo, hi = s * W_IN_SHARD, (s + 1) * W_IN_SHARD
        parts, nat = [], 0
        for a_start, width in _NATURAL_FROM_ALIGNED:
            b, e = max(lo, nat), min(hi, nat + width)
            if b < e:
                parts.append(d_al[..., a_start + b - nat:a_start + e - nat])
            nat += width
        parts.append(jnp.zeros(d_al.shape[:1] + (W_IN_SHARD_PAD - W_IN_SHARD,), d_al.dtype))
        slots.append(jnp.concatenate(parts, axis=-1))
    return jnp.stack(slots)


SMALL_VEC_W = 1024


def _pack_small(parts):
    rows = []
    for p in parts:
        flat = p.reshape(-1).astype(F32)
        r = -(-flat.shape[0] // SMALL_VEC_W)
        rows.append(jnp.pad(flat, (0, r * SMALL_VEC_W - flat.shape[0])).reshape(r, SMALL_VEC_W))
    vec = jnp.concatenate(rows, axis=0)
    return jnp.pad(vec, ((0, -vec.shape[0] % 8), (0, 0)))


def _unpack_small(vec, shapes):
    out, off = [], 0
    for s in shapes:
        n = math.prod(s)
        r = -(-n // SMALL_VEC_W)
        out.append(vec[off:off + r].reshape(-1)[:n].reshape(s))
        off += r
    return out


def _lanes(vec, at):
    return jnp.zeros((1, 128), F32).at[0, at:at + vec.shape[0]].set(vec)


SMALL_NAMES = ("norm_pre", "norm_post", "norm_mem", "a_log", "dt_bias", "dn_norm", "gm_norm",
               "spatial_w", "spatial_b", "sinks")


def _other_weights(s_mem, s_up, s_out):
    return (s_mem.reshape(D_MODEL, 2 * BRANCH_W),
            jnp.transpose(s_up, (1, 2, 0, 3)).reshape(N_BRANCH, BRANCH_W, D_MODEL), s_out.reshape(D_MODEL, D_MODEL))


def _grad_slots(d_in_al, d_mem, d_up, d_out):
    return [None if d_in_al is None else _slots_of_aligned(d_in_al), d_mem.astype(BF).reshape(N_DEV, 128, 2 * BRANCH_W),
            jnp.transpose(d_up.astype(BF).reshape(N_BRANCH, BRANCH_W, N_DEV, 128), (2, 0, 1, 3)),
            d_out.astype(BF).reshape(N_DEV, 128, D_MODEL)]


def _layer_params(l, small, conv_full, token):
    return dict(
        gpre=small["norm_pre"][l][None] + token, gpost=small["norm_post"][l][None], gmem=small["norm_mem"][l][None],
        cw=conv_full[l], al=_lanes(small["a_log"][l], 4), dt=_lanes(small["dt_bias"][l], 4),
        dnn=small["dn_norm"][l][None], gain=small["gm_norm"][l][None], ws=small["spatial_w"][l],
        bt=jnp.zeros((128, 128), F32).at[:, :GM_GROUPS].set(small["spatial_b"][l].T),
        sinks=_lanes(small["sinks"][l], 0))


def _layer_forward(l, xl, hs, mem, p, w_in_al, other_weights, **tail):
    t = "l%d_" % l
    h, h_t = hs
    cols = _matmul(h, w_in_al, "nn", F32, (1024, 1536, 1024), t + "w_in")
    ya, ss, ts = dn_forward(cols, JB_DN, p["cw"], p["al"], p["dt"], p["dnn"], t + "deltanet")
    yb = gm_forward(cols, JB_GM, p["gain"], p["ws"], p["bt"], t + "gmlp")
    yc = sw_forward(cols, JB_SW, p["sinks"], t + "swa")
    w_mem, w_up, w_out = other_weights(yc)
    mkv = memkv_forward(mem, p["gmem"], w_mem, t + "memkv")
    ym = xm_forward(cols, JB_XM, mkv, t + "memattn")
    outs = merge_forward([ya, yb, yc, ym], cols, JB_GATE, xl, w_up, w_out, p["gpost"], t + "merge", **tail)
    return outs, dict(p, x=xl, h_t=h_t, cols=cols, mkv=mkv, ss=ss, ts=ts, ys=[ya, yb, yc, ym]), (w_in_al, w_mem, w_up, w_out)


def _layer_backward(l, s, mem, weights, dx, token, early=None):
    w_in_al, w_mem, w_up, w_out = weights
    t = "l%d_" % l
    cols = s["cols"]
    dcols, dya, dyb, dyc, dym, dwup, dwout, dgpost = merge_backward(
        s["ys"], cols, JB_GATE, w_up, w_out, s["gpost"] + token, dx, t + "merge_bwd")
    dcols, dmkv = xm_backward(cols, JB_XM, s["mkv"], dym, dcols, t + "memattn_bwd")
    dgmem, dwmem = memkv_backward(mem, s["gmem"], w_mem, dmkv, t + "memkv_bwd")
    sinks = s["sinks"] if early is None else s["sinks"] + early(dwmem, dwup, dwout)
    dcols, dsinks = sw_backward(cols, JB_SW, sinks, dyc, dcols, t + "swa_bwd")
    dcols, dgain, dws, dbt = gm_backward(cols, JB_GM, s["gain"], s["ws"], s["bt"], dyb, dcols, t + "gmlp_bwd")
    dcols, dcw, dal, ddt, ddn = dn_backward(
        cols, JB_DN, s["cw"], s["al"], s["dt"], s["dnn"], s["ss"], s["ts"], dya, dcols, t + "deltanet_bwd")
    dh = _matmul(dcols, w_in_al, "nt", F32, (1024, 1024, 3584), t + "w_in_bwd_x")
    dwin = _matmul(s["h_t"], dcols, "nn", BF, (1024, 1536, 2048), t + "w_in_bwd_w")
    dx, dgpre = prenorm_backward(s["x"], s["gpre"], dh, dx, t + "prenorm_bwd")
    gsmall = dict(norm_pre=dgpre[0], norm_post=dgpost[0], norm_mem=dgmem[0], a_log=dal[0, 4:8], dt_bias=ddt[0, 4:8],
                  dn_norm=ddn[0], gm_norm=dgain[0], spatial_w=dws, spatial_b=dbt[:, :GM_GROUPS].T,
                  sinks=dsinks[0, :SW_HEADS], conv_w=dcw)
    return dx, gsmall, (dwin, dwmem, dwup, dwout)


def kernel(x, mem, norm_pre, norm_post, norm_mem, w_in, conv_w, a_log, dt_bias, dn_norm, gm_norm, spatial_w, spatial_b, sinks, w_mem_kv, w_up, w_out, loss_target, m_norm_pre, m_norm_post, m_norm_mem, m_w_in, m_conv_w, m_a_log, m_dt_bias, m_dn_norm, m_gm_norm, m_spatial_w, m_spatial_b, m_sinks, m_w_mem_kv, m_w_up, m_w_out, v_norm_pre, v_norm_post, v_norm_mem, v_w_in, v_conv_w, v_a_log, v_dt_bias, v_dn_norm, v_gm_norm, v_spatial_w, v_spatial_b, v_sinks, v_w_mem_kv, v_w_up, v_w_out):
    xi, yi, ci = _my_place()
    my_slot = 4 * xi + 2 * yi + ci
    conv_shard = conv_w.shape[-1]
    x2, mem2, target = x[0], mem[0], loss_target[0]

    w_in_pad = jnp.pad(w_in.astype(BF), ((0, 0), (0, 0), (0, W_IN_SHARD_PAD - W_IN_SHARD)))
    shards = [[w_in_pad[l], w_mem_kv[l].astype(BF), w_up[l].astype(BF), w_out[l].astype(BF)] for l in range(DEPTH)]
    w_in_slots0, w_up_slots0, w_out_slots0, conv_slots = _all_gather_slots(
        [shards[0][0], shards[0][2], shards[0][3], conv_w], "gather_weights_l0")
    ag = list(_spread_start([shards[0][1]] + shards[1], "gather", "gather_weights_rest_start"))
    conv_full = jnp.transpose(conv_slots, (1, 2, 0, 3)).reshape(DEPTH, CONV_W, N_DEV * conv_shard)
    small = dict(norm_pre=norm_pre, norm_post=norm_post, norm_mem=norm_mem, a_log=a_log,
                 dt_bias=dt_bias, dn_norm=dn_norm, gm_norm=gm_norm, spatial_w=spatial_w,
                 spatial_b=spatial_b, sinks=sinks)

    def arrived(which, after, name):
        ag[2], ag[3] = _spread_wait(ag[0], ag[1], ag[2], ag[3], which, after, name)
        return [ag[3][a] for a in which]

    p0, p1 = _layer_params(0, small, conv_full, ag[4][0, 0]), _layer_params(1, small, conv_full, 0.0)
    (x1, h1, h1_t), saved0, weights0 = _layer_forward(
        0, x2, prenorm_forward(x2, p0["gpre"], "l0_prenorm"), mem2, p0, _aligned_w_in(w_in_slots0),
        lambda y: _other_weights(*arrived([0], y, "gather_weights_l0_w_mem_wait"), w_up_slots0, w_out_slots0),
        next_gain=p1["gpre"])
    w_in_slots1, = arrived([1], x1, "gather_weights_l1_w_in_wait")
    (x_out, loss, dx), saved1, weights1 = _layer_forward(
        1, x1, (h1, h1_t), mem2, p1, _aligned_w_in(w_in_slots1),
        lambda y: _other_weights(*arrived([2, 3, 4], y, "gather_weights_l1_rest_wait")), target=target)

    packed_names = SMALL_NAMES + ("conv_w",)
    dx, gsmall1, gbig1 = _layer_backward(1, saved1, mem2, weights1, dx, 0.0)
    small1 = [loss[0, :1]] + [gsmall1[n] for n in packed_names]
    sm1 = _spread_start([_pack_small(small1)], "gather", "gather_small_grads_l1_start")
    rs_send, rs_recv, rs_src, rs_land, rs_token = _spread_start(_grad_slots(*gbig1), "scatter", "exchange_grads_l1_start")
    rest0 = []

    def send_rest0(dwmem, dwup, dwout):
        rest0.extend(_spread_start(_grad_slots(None, dwmem, dwup, dwout)[1:], "scatter", "exchange_grads_l0_rest_start"))
        return rest0[4][0, 0]

    dx, gsmall0, gbig0 = _layer_backward(0, saved0, mem2, weights0, dx, rs_token[0, 0] + sm1[4][0, 0], send_rest0)
    _, parts1 = _spread_wait(rs_send, rs_recv, rs_src, rs_land, range(4), dx, "exchange_grads_l1_wait")

    small0 = [gsmall0[n] for n in packed_names]
    sm0 = _spread_start([_pack_small(small0)], "gather", "gather_small_grads_l0_start")

    g_win0 = _slots_of_aligned(gbig0[0])
    g_win0 = g_win0.reshape((N_DEV // 2, 2) + g_win0.shape[1:])
    theirs, = _exchange_sibling([g_win0], "exchange_sibling_l0")
    chip_sum = _pair_sum(g_win0, theirs, "pair_sum_l0")
    ch_send, ch_recv, ch_src, ch_land, ch_token = _spread_start([chip_sum], "chips", "exchange_chips_l0_start")

    _, (land1,) = _spread_wait(*sm1[:4], [0], ch_token, "gather_small_grads_l1_wait")
    _, (land0,) = _spread_wait(*sm0[:4], [0], land1, "gather_small_grads_l0_wait")
    tot1 = _unpack_small(_sum_slots(land1, "sum_small_grads_l1"), [p.shape for p in small1])
    tot0 = _unpack_small(_sum_slots(land0, "sum_small_grads_l0"), [p.shape for p in small0])
    loss_tot = tot1[0][0]
    grads = {n: jnp.stack([g0, g1]) for n, g0, g1 in zip(packed_names, tot0, tot1[1:])}
    grads["conv_w"] = lax.dynamic_slice_in_dim(grads["conv_w"], my_slot * conv_shard, conv_shard, axis=2)

    given = dict(norm_pre=(norm_pre, m_norm_pre, v_norm_pre), norm_post=(norm_post, m_norm_post, v_norm_post),
                 norm_mem=(norm_mem, m_norm_mem, v_norm_mem), a_log=(a_log, m_a_log, v_a_log),
                 dt_bias=(dt_bias, m_dt_bias, v_dt_bias), dn_norm=(dn_norm, m_dn_norm, v_dn_norm),
                 gm_norm=(gm_norm, m_gm_norm, v_gm_norm), spatial_w=(spatial_w, m_spatial_w, v_spatial_w),
                 spatial_b=(spatial_b, m_spatial_b, v_spatial_b), sinks=(sinks, m_sinks, v_sinks),
                 conv_w=(conv_w, m_conv_w, v_conv_w))
    pshapes = [given[n][0].shape for n in packed_names]
    pw, pm, pv = (_pack_small([given[n][i] for n in packed_names]) for i in range(3))
    pd, pnm, pnv = _adamw(pw + ch_token[0, 0], _pack_small([grads[n] for n in packed_names]), pm, pv, "adamw_small")
    upd = {n: t for n, t in zip(packed_names, zip(_unpack_small(pd, pshapes), _unpack_small(pnm, pshapes),
                                                  _unpack_small(pnv, pshapes)))}
    big = (("w_mem_kv", (w_mem_kv, m_w_mem_kv, v_w_mem_kv)), ("w_up", (w_up, m_w_up, v_w_up)),
           ("w_out", (w_out, m_w_out, v_w_out)))
    first = [_sum_adamw(parts1[1 + i], w, m, v, 1, None, ch_token, "adamw_%s_l1" % name)
             for i, (name, (w, m, v)) in enumerate(big)]
    _, parts0_rest = _spread_wait(*rest0[:4], range(3), first[-1][0], "exchange_grads_l0_rest_wait")
    for i, (name, (w, m, v)) in enumerate(big):
        g, d, nm, nv = _sum_adamw(parts0_rest[i], w, m, v, 0, first[i], None, "adamw_%s_l0" % name)
        grads[name], upd[name] = g, (d, nm, nv)
    _, (parts0_w_in,) = _spread_wait(ch_send, ch_recv, ch_src, ch_land, [0], upd["w_out"][0], "exchange_chips_l0_wait")
    w_in_t, m_w_in_t, v_w_in_t = (jnp.transpose(t, (2, 0, 1)) for t in (w_in, m_w_in, v_w_in))
    g, d, nm, nv = (jnp.transpose(t, (1, 2, 0)) for t in
                    _sum_adamw_t([parts0_w_in, parts1[0]], w_in_t, m_w_in_t, v_w_in_t, "adamw_w_in"))
    grads["w_in"], upd["w_in"] = g, (d, nm, nv)

    order = ("norm_pre", "norm_post", "norm_mem", "w_in", "conv_w", "a_log", "dt_bias", "dn_norm",
             "gm_norm", "spatial_w", "spatial_b", "sinks", "w_mem_kv", "w_up", "w_out")
    return (loss_tot, dx[None], *[grads[n] for n in order], *[upd[n][0] for n in order],
            *[upd[n][1] for n in order], *[upd[n][2] for n in order])
```

```python
import functools
import math

import jax
import jax.numpy as jnp
from jax import lax
from jax.experimental import pallas as pl
from jax.experimental.pallas import tpu as pltpu

MESH = pl.DeviceIdType.MESH
N_DEV = 8

D_MODEL = 1024
DEPTH = 2
N_BRANCH = 4
BRANCH_W = 512
DN_HEADS = 4
CONV_W = 4
GM_GROUPS = 4
SW_HEADS = 8
EPS = 1e-6
NEG_INF = -1e30

D_IN = 9992
W_IN_SHARD = D_IN // N_DEV
W_IN_SHARD_PAD = 1280
D_IN_AL = 10752
DN_W, SW_W, GM_W, XM_W = 2560, 1536, 1536, 1024

ADAM_LR = 0.001
ADAM_B1 = 0.9
ADAM_B2 = 0.999
ADAM_EPS = 1e-08
ADAM_WD = 0.01
ADAM_STEP = 10

VMEM_LIMIT = 56 * 1024 * 1024

BF = jnp.bfloat16
F32 = jnp.float32
DN_C = 128
DN_D = 128
HALO = 8
BLK = 128
Y_DTYPE = BF


def _my_place():
    return lax.axis_index("x"), lax.axis_index("y"), lax.axis_index("c")


_ANY = pl.BlockSpec(memory_space=pl.ANY)


def _all_gather_slots(parts, name):
    n = len(parts)

    def body(*refs):
        p_refs, out_refs = refs[:n], refs[n:2 * n]
        send_sems, recv_sems, local_sems = refs[2 * n:]
        x, y, c = _my_place()
        me, sibling = (x, y, c), (x, y, 1 - c)
        chips = [(1 - x, y), (x, 1 - y), (1 - x, 1 - y)]

        def copy(a, k, block, to, src=None):
            px, py, pc = block
            slot = out_refs[a].at[4 * px + 2 * py + pc]
            return pltpu.make_async_remote_copy(
                src_ref=slot if src is None else src, dst_ref=slot,
                send_sem=send_sems.at[7 * a + k], recv_sem=recv_sems.at[7 * a + k],
                device_id=to, device_id_type=MESH)

        mine = [pltpu.make_async_copy(p_refs[a], out_refs[a].at[4 * x + 2 * y + c], local_sems.at[a])
                for a in range(n)]
        for cp in mine:
            cp.start()
        first = []
        for a in range(n):
            first.append(copy(a, 0, me, sibling, src=p_refs[a]))
            first += [copy(a, 1 + j, me, (*chip, c), src=p_refs[a]) for j, chip in enumerate(chips)]
        for cp in first:
            cp.start()
        passed = []
        for j, chip in enumerate(chips):
            for a in range(n):
                copy(a, 1 + j, (*chip, c), me).wait_recv()
                fwd = copy(a, 4 + j, (*chip, c), sibling)
                fwd.start()
                passed.append(fwd)
        for a in range(n):
            copy(a, 0, sibling, me).wait_recv()
            for j, chip in enumerate(chips):
                copy(a, 4 + j, (*chip, 1 - c), me).wait_recv()
        for cp in first + passed:
            cp.wait_send()
        for cp in mine:
            cp.wait()

    return pl.pallas_call(
        body, name=name,
        out_shape=[jax.ShapeDtypeStruct((N_DEV,) + p.shape, p.dtype) for p in parts],
        in_specs=[_ANY] * n, out_specs=[_ANY] * n,
        scratch_shapes=[pltpu.SemaphoreType.DMA((7 * n,)), pltpu.SemaphoreType.DMA((7 * n,)),
                        pltpu.SemaphoreType.DMA((n,))],
    )(*parts)


def _exchange_sibling(parts, name):
    n = len(parts)

    def body(*refs):
        g_refs, out_refs = refs[:n], refs[n:2 * n]
        send_sems, recv_sems = refs[2 * n:]
        x, y, c = _my_place()
        copies = [pltpu.make_async_remote_copy(
            src_ref=g_refs[a].at[:, 1 - c], dst_ref=out_refs[a],
            send_sem=send_sems.at[a], recv_sem=recv_sems.at[a],
            device_id=(x, y, 1 - c), device_id_type=MESH) for a in range(n)]
        for cp in copies:
            cp.start()
        for cp in copies:
            cp.wait()

    return pl.pallas_call(
        body, name=name,
        out_shape=[jax.ShapeDtypeStruct((4,) + g.shape[2:], g.dtype) for g in parts],
        in_specs=[_ANY] * n, out_specs=[_ANY] * n,
        scratch_shapes=[pltpu.SemaphoreType.DMA((n,)), pltpu.SemaphoreType.DMA((n,))],
    )(*parts)


_HBM = pl.BlockSpec(memory_space=pltpu.HBM)
_SEM = pl.BlockSpec(memory_space=pltpu.SEMAPHORE)
_EFFECT = pltpu.SideEffectType.DATAFLOW_SIDE_EFFECTING


def _peer(x, y, c, k):
    return (1 - x if (k >> 2) & 1 else x, 1 - y if (k >> 1) & 1 else y, 1 - c if k & 1 else c)


def _spread_start(srcs, mode, name):
    n = len(srcs)
    lands = [lax.empty((N_DEV,) + s.shape if mode == "gather" else s.shape, s.dtype) for s in srcs]
    peers = range(0, N_DEV, 2) if mode == "chips" else range(N_DEV)

    def body(*refs):
        src_refs, land_refs = refs[:n], refs[n:2 * n]
        send_sems, recv_sems = refs[2 * n:2 * n + 2]
        token = refs[-1]
        x, y, c = _my_place()
        for a in range(n):
            for k in peers:
                px, py, pc = _peer(x, y, c, k)
                if mode == "chips":
                    src, mine = src_refs[a].at[2 * px + py], 2 * x + y
                else:
                    src = src_refs[a].at[4 * px + 2 * py + pc] if mode == "scatter" else src_refs[a]
                    mine = 4 * x + 2 * y + c
                pltpu.make_async_remote_copy(
                    src_ref=src, dst_ref=land_refs[a].at[mine],
                    send_sem=send_sems.at[a], recv_sem=recv_sems.at[a],
                    device_id=(px, py, pc), device_id_type=MESH).start()
        token[...] = jnp.zeros_like(token)

    out = pl.pallas_call(
        body, name=name,
        out_shape=[pltpu.SemaphoreType.DMA((n,)), pltpu.SemaphoreType.DMA((n,))]
        + [pltpu.HBM(s.shape, s.dtype) for s in srcs] + [pltpu.HBM(l.shape, l.dtype) for l in lands]
        + [jax.ShapeDtypeStruct((8, 128), F32)],
        in_specs=[_HBM] * (2 * n),
        out_specs=[_SEM, _SEM] + [_HBM] * (2 * n) + [pl.BlockSpec(memory_space=pltpu.VMEM)],
        input_output_aliases={i: 2 + i for i in range(2 * n)},
        compiler_params=pltpu.CompilerParams(has_side_effects=_EFFECT),
    )(*[pltpu.with_memory_space_constraint(s, pltpu.HBM) for s in srcs],
      *[pltpu.with_memory_space_constraint(l, pltpu.HBM) for l in lands])
    return out[0], out[1], out[2:2 + n], out[2 + n:2 + 2 * n], out[-1]


def _spread_wait(send_sems, recv_sems, srcs, lands, which, after, name):
    n = len(srcs)

    def body(*refs):
        land_refs = refs[n:2 * n]
        send_sems, recv_sems = refs[2 * n:2 * n + 2]
        x, y, c = _my_place()
        for a in which:
            whole = pltpu.make_async_remote_copy(
                src_ref=land_refs[a], dst_ref=land_refs[a],
                send_sem=send_sems.at[a], recv_sem=recv_sems.at[a],
                device_id=(x, y, c), device_id_type=MESH)
            whole.wait_send()
            whole.wait_recv()

    out = pl.pallas_call(
        body, name=name,
        out_shape=[pltpu.HBM(s.shape, s.dtype) for s in srcs] + [pltpu.HBM(l.shape, l.dtype) for l in lands],
        in_specs=[_HBM] * (2 * n) + [_SEM, _SEM, _ANY],
        out_specs=[_HBM] * (2 * n),
        input_output_aliases={i: i for i in range(2 * n)},
        compiler_params=pltpu.CompilerParams(has_side_effects=_EFFECT),
    )(*srcs, *lands, send_sems, recv_sems, after)
    return out[:n], out[n:]


def _sum_slots(parts, name):
    def body(p_ref, o_ref):
        acc = p_ref[0]
        for s in range(1, parts.shape[0]):
            acc = acc + p_ref[s]
        o_ref[...] = acc

    return pl.pallas_call(body, name=name, out_shape=jax.ShapeDtypeStruct(parts.shape[1:], parts.dtype))(parts)


def _pick(n, pref):
    if n <= pref:
        return n
    t = pref - pref % 128
    while t > 0 and n % t:
        t -= 128
    return t if t > 0 else n


_DIMS = {"nn": (((1,), (0,)), ((), ())),
         "nt": (((1,), (1,)), ((), ())),
         "tn": (((0,), (0,)), ((), ()))}


def _matmul(a, b, mode, out_dtype, tiles, name):
    (m, k) = a.shape
    n = b.shape[1] if mode == "nn" else b.shape[0]
    tm, tn, tk = (_pick(d, t) for d, t in zip((m, n, k), tiles))
    nk = k // tk

    def product(a_ref, b_ref):
        return lax.dot_general(a_ref[...].astype(BF), b_ref[...].astype(BF), _DIMS[mode], preferred_element_type=F32)

    def body_whole_k(a_ref, b_ref, o_ref):
        o_ref[...] = product(a_ref, b_ref).astype(o_ref.dtype)

    def body_split_k(a_ref, b_ref, o_ref, acc_ref):
        kk = pl.program_id(2)

        @pl.when(kk == 0)
        def _():
            acc_ref[...] = jnp.zeros_like(acc_ref)

        acc_ref[...] += product(a_ref, b_ref)

        @pl.when(kk == nk - 1)
        def _():
            o_ref[...] = acc_ref[...].astype(o_ref.dtype)

    b_spec = (pl.BlockSpec((tn, tk), lambda i, j, kk: (j, kk)) if mode == "nt"
              else pl.BlockSpec((tk, tn), lambda i, j, kk: (kk, j)))
    return pl.pallas_call(
        body_whole_k if nk == 1 else body_split_k, name=name,
        out_shape=jax.ShapeDtypeStruct((m, n), out_dtype),
        grid=(m // tm, n // tn, nk),
        in_specs=[pl.BlockSpec((tm, tk), lambda i, j, kk: (i, kk)), b_spec],
        out_specs=pl.BlockSpec((tm, tn), lambda i, j, kk: (i, j)),
        scratch_shapes=[] if nk == 1 else [pltpu.VMEM((tm, tn), F32)],
        compiler_params=pltpu.CompilerParams(
            dimension_semantics=("parallel", "parallel", "arbitrary"),
            vmem_limit_bytes=VMEM_LIMIT),
    )(a, b)


def _rows2d(t, lead):
    return t.reshape(t.shape[:lead] + (math.prod(t.shape[lead:-1]), t.shape[-1]))


def _pair_sum(g, theirs, name):
    g3, t3 = _rows2d(g, 2), _rows2d(theirs, 1)
    _, r, w = t3.shape
    tr = _pick(r, 512)

    def body(g_ref, t_ref, o_ref):
        c = lax.axis_index("c")
        mine = jnp.where(c == 0, g_ref[0, 0], g_ref[0, 1])
        o_ref[0] = (mine.astype(F32) + t_ref[0].astype(F32)).astype(o_ref.dtype)

    out = pl.pallas_call(
        body, name=name,
        out_shape=jax.ShapeDtypeStruct(t3.shape, t3.dtype),
        grid=(4, r // tr),
        in_specs=[pl.BlockSpec((1, 2, tr, w), lambda q, i: (q, 0, i, 0)),
                  pl.BlockSpec((1, tr, w), lambda q, i: (q, i, 0))],
        out_specs=pl.BlockSpec((1, tr, w), lambda q, i: (q, i, 0)),
        compiler_params=pltpu.CompilerParams(dimension_semantics=("parallel", "parallel")),
    )(g3, t3)
    return out.reshape(theirs.shape)


def _adam_update(w, g, m, v):
    c1 = 1.0 - ADAM_B1 ** ADAM_STEP
    c2 = 1.0 - ADAM_B2 ** ADAM_STEP
    nm = ADAM_B1 * m + (1.0 - ADAM_B1) * g
    nv = ADAM_B2 * v + (1.0 - ADAM_B2) * (g * g)
    delta = -ADAM_LR * ((nm / c1) / (jnp.sqrt(nv / c2) + ADAM_EPS) + ADAM_WD * w)
    return delta, nm, nv


def _sum_adamw(parts, w, m, v, layer, carry, after, name):
    shape = w.shape
    cols = shape[-1]
    p3 = _rows2d(parts, 1)
    w3, m3, v3 = (_rows2d(t, 1) for t in (w, m, v))
    rows = w3.shape[1]
    tr = _pick(rows, 128)
    n_parts = p3.shape[0]

    def body(p_ref, w_ref, m_ref, v_ref, *rest):
        g_ref, d_ref, nm_ref, nv_ref = rest[-4:]
        g = p_ref[0, :, :cols].astype(F32)
        for q in range(1, n_parts):
            g = g + p_ref[q, :, :cols].astype(F32)
        d, nm, nv = _adam_update(w_ref[0], g, m_ref[0], v_ref[0])
        g_ref[0] = g
        d_ref[0] = d
        nm_ref[0] = nm
        nv_ref[0] = nv

    spec = pl.BlockSpec((1, tr, cols), lambda i: (layer, i, 0))
    extra = [] if carry is None else [_rows2d(t, 1) for t in carry]
    tail = [] if after is None else [after]
    out = pl.pallas_call(
        body, name=name,
        out_shape=[jax.ShapeDtypeStruct(w3.shape, F32)] * 4,
        grid=(rows // tr,),
        in_specs=[pl.BlockSpec((n_parts, tr, p3.shape[-1]), lambda i: (0, i, 0)), spec, spec, spec] + [_ANY] * len(extra + tail),
        out_specs=[spec] * 4,
        input_output_aliases={4 + i: i for i in range(len(extra))},
        compiler_params=pltpu.CompilerParams(dimension_semantics=("parallel",)),
    )(p3, w3, m3, v3, *extra, *tail)
    return tuple(t.reshape(shape) for t in out)


def _sum_adamw_t(parts, w, m, v, name):
    rows = parts[0].shape[2]
    tr = 256
    assert rows % tr == 0 and rows >= w.shape[0]

    def body(*refs):
        p_refs, (w_ref, m_ref, v_ref), (g_ref, d_ref, nm_ref, nv_ref) = refs[:DEPTH], refs[DEPTH:DEPTH + 3], refs[DEPTH + 3:]
        for l in range(DEPTH):
            g = p_refs[l][0].astype(F32)
            for q in range(1, p_refs[l].shape[0]):
                g = g + p_refs[l][q].astype(F32)
            g = g.T
            d, nm, nv = _adam_update(w_ref[:, l, :], g, m_ref[:, l, :], v_ref[:, l, :])
            g_ref[:, l, :] = g
            d_ref[:, l, :] = d
            nm_ref[:, l, :] = nm
            nv_ref[:, l, :] = nv

    spec = pl.BlockSpec((tr,) + w.shape[1:], lambda i: (i, 0, 0))
    return pl.pallas_call(
        body, name=name,
        out_shape=[jax.ShapeDtypeStruct(w.shape, F32)] * 4,
        grid=(rows // tr,),
        in_specs=[pl.BlockSpec((p.shape[0], p.shape[1], tr), lambda i: (0, 0, i)) for p in parts] + [spec] * 3,
        out_specs=[spec] * 4,
        compiler_params=pltpu.CompilerParams(dimension_semantics=("parallel",)),
    )(*parts, w, m, v)


def _adamw(w, g, m, v, name):
    rows, cols = w.shape
    tr = _pick(rows, 128)

    def body(w_ref, g_ref, m_ref, v_ref, d_ref, nm_ref, nv_ref):
        d, nm, nv = _adam_update(w_ref[...], g_ref[...], m_ref[...], v_ref[...])
        d_ref[...] = d
        nm_ref[...] = nm
        nv_ref[...] = nv

    spec = pl.BlockSpec((tr, cols), lambda i: (i, 0))
    return pl.pallas_call(
        body, name=name,
        out_shape=[jax.ShapeDtypeStruct((rows, cols), F32)] * 3,
        grid=(rows // tr,),
        in_specs=[spec] * 4, out_specs=[spec] * 3,
        compiler_params=pltpu.CompilerParams(dimension_semantics=("parallel",)),
    )(w, g, m, v)


_VJP = {"nn": (("nt", "gb"), ("tn", "ag")),
        "nt": (("nn", "gb"), ("tn", "ga")),
        "tn": (("nt", "bg"), ("nn", "ag"))}


def _make_dot(cast, precision):
    def raw(mode, a, b):
        return lax.dot_general(cast(a), cast(b), _DIMS[mode], precision=precision,
                               preferred_element_type=F32)

    @functools.partial(jax.custom_vjp, nondiff_argnums=(0,))
    def dot(mode, a, b):
        return raw(mode, a, b)

    def fwd(mode, a, b):
        return raw(mode, a, b), (a, b)

    def bwd(mode, res, g):
        a, b = res
        pick = {"a": a, "b": b, "g": g}
        (ma, ta), (mb, tb) = _VJP[mode]
        return dot(ma, pick[ta[0]], pick[ta[1]]), dot(mb, pick[tb[0]], pick[tb[1]])

    dot.defvjp(fwd, bwd)
    return dot


bdot = _make_dot(lambda t: t.astype(BF), None)
hdot = _make_dot(lambda t: t, lax.Precision.HIGHEST)


def _xdot(mode, a, b):
    return lax.dot_general(a, b, _DIMS[mode], precision=lax.Precision.HIGH, preferred_element_type=F32)


def _unit_lower_inverse(Ls):
    n = Ls[0].shape[0]
    batched = (((2,), (1,)), ((0,), (0,)))
    mm = lambda a, b: lax.dot_general(a, b, batched, precision=lax.Precision.HIGH, preferred_element_type=F32)
    eye = (lax.broadcasted_iota(jnp.int32, (n, n), 0) == lax.broadcasted_iota(jnp.int32, (n, n), 1)).astype(F32)
    p = jnp.stack(Ls)
    t_inv = eye[None] - p
    for _ in range(6):
        p = mm(p, p)
        t_inv = t_inv + mm(t_inv, p)
    return [t_inv[h] for h in range(len(Ls))]


@jax.custom_vjp
def _tri_solve(L, rhs, t_inv):
    return _xdot("nn", t_inv, rhs)


def _tri_solve_fwd(L, rhs, t_inv):
    sol = _xdot("nn", t_inv, rhs)
    return sol, (t_inv, sol)


def _tri_solve_bwd(res, dsol):
    t_inv, sol = res
    drhs = _xdot("tn", t_inv, dsol)
    return -_xdot("nt", drhs, sol), drhs, jnp.zeros_like(t_inv)


_tri_solve.defvjp(_tri_solve_fwd, _tri_solve_bwd)


def _sigmoid(x):
    return 1.0 / (1.0 + jnp.exp(-x))


def _softplus(x):
    return jnp.maximum(x, 0.0) + jnp.log(1.0 + jnp.exp(-jnp.abs(x)))


def _dn_chunk(S, xs, ba, z, cw, al, dt, dn, t_saved=None):
    C = DN_C
    pre = xs[0] * cw[0] + xs[1] * cw[1] + xs[2] * cw[2] + xs[3] * cw[3]
    qkv = pre * _sigmoid(pre)
    lane = lax.broadcasted_iota(jnp.int32, (1, 128), 1)
    sub = lax.broadcasted_iota(jnp.int32, (C, 1), 0)
    row_i = lax.broadcasted_iota(jnp.int32, (C, C), 0)
    col_i = lax.broadcasted_iota(jnp.int32, (C, C), 1)
    strict = row_i > col_i
    incl = row_i >= col_i
    g_all = jnp.where((lane >= 4) & (lane < 8), -jnp.exp(al) * _softplus(ba + dt), 0.0)
    gc_all = hdot("nn", incl.astype(F32), g_all)
    gc_all_t = gc_all.T
    beta_all = _sigmoid(ba)
    glast_all = jnp.sum(jnp.where(sub == C - 1, gc_all, 0.0), axis=0, keepdims=True)
    heads = []
    for h in range(DN_HEADS):
        q = qkv[:, 128 * h:128 * (h + 1)]
        k = qkv[:, 512 + 128 * h:512 + 128 * (h + 1)]
        v = qkv[:, 1024 + 128 * h:1024 + 128 * (h + 1)]
        q = q * lax.rsqrt(jnp.sum(q * q, axis=1, keepdims=True) + EPS) * (DN_D ** -0.5)
        k = k * lax.rsqrt(jnp.sum(k * k, axis=1, keepdims=True) + EPS)
        beta = jnp.sum(jnp.where(lane == h, beta_all, 0.0), axis=1, keepdims=True)
        gc = jnp.sum(jnp.where(lane == 4 + h, gc_all, 0.0), axis=1, keepdims=True)
        gc_row = jnp.sum(jnp.where(sub == 4 + h, gc_all_t, 0.0), axis=0, keepdims=True)
        g_last = jnp.sum(jnp.where(lane == 4 + h, glast_all, 0.0), axis=1, keepdims=True)
        diff = gc - gc_row
        kb = k * beta
        L = jnp.where(strict, bdot("nt", kb, k) * jnp.exp(jnp.where(strict, diff, 0.0)), 0.0)
        heads.append((q, k, v, beta, gc, g_last, diff, kb, L))
    t_invs = _unit_lower_inverse([hd[-1] for hd in heads]) if t_saved is None else t_saved
    ys, s_new = [], []
    for h, (q, k, v, beta, gc, g_last, diff, kb, L) in enumerate(heads):
        sol = _tri_solve(L, jnp.concatenate([v * beta, kb * jnp.exp(gc)], axis=1), t_invs[h])
        u, w = sol[:, :DN_D], sol[:, DN_D:]
        a_qk = jnp.where(incl, bdot("nt", q, k) * jnp.exp(jnp.where(incl, diff, 0.0)), 0.0)
        qg = q * jnp.exp(gc)
        kd = k * jnp.exp(g_last - gc)
        v_new = u - bdot("nn", w, S[h])
        o = bdot("nn", qg, S[h]) + bdot("nn", a_qk, v_new)
        s_new.append(S[h] * jnp.exp(g_last) + bdot("tn", kd, v_new))
        o = o * lax.rsqrt(jnp.mean(o * o, axis=1, keepdims=True) + EPS) * dn
        zh = z[:, 128 * h:128 * (h + 1)]
        ys.append(o * (zh * _sigmoid(zh)))
    return jnp.concatenate(ys, axis=1), tuple(s_new), tuple(t_invs)


def _load_shifted(xbuf, x_ref, halo_ref, first):
    xbuf[0:HALO, :] = jnp.where(first, 0.0, halo_ref[:, 0:1536])
    xbuf[HALO:HALO + DN_C, :] = x_ref[:, 0:1536]
    return [xbuf[HALO - 3 + k:HALO - 3 + k + DN_C, :] for k in range(4)]


def dn_forward(cols, jblk, cw, al, dt, dn, name):
    T = cols.shape[0]
    n = T // DN_C

    def body(x_ref, halo_ref, cw_ref, al_ref, dt_ref, dn_ref, y_ref, ss_ref, ts_ref, s_scr, xbuf):
        i = pl.program_id(0)

        @pl.when(i == 0)
        def _():
            s_scr[...] = jnp.zeros_like(s_scr)

        xs = _load_shifted(xbuf, x_ref, halo_ref, i == 0)
        ss_ref[0] = s_scr[...]
        S = [s_scr[h] for h in range(DN_HEADS)]
        cws = [cw_ref[k:k + 1, :] for k in range(4)]
        y, s_new, t_invs = _dn_chunk(S, xs, x_ref[:, 2048:2176], x_ref[:, 1536:2048], cws,
                                     al_ref[...], dt_ref[...], dn_ref[...])
        y_ref[...] = y.astype(y_ref.dtype)
        for h in range(DN_HEADS):
            s_scr[h] = s_new[h]
            ts_ref[0, h] = t_invs[h]

    per = DN_C // HALO
    full = lambda shape: pl.BlockSpec(shape, lambda i: (0,) * len(shape))
    return pl.pallas_call(
        body, name=name,
        out_shape=[jax.ShapeDtypeStruct((T, 512), Y_DTYPE),
                   jax.ShapeDtypeStruct((n, DN_HEADS, DN_D, DN_D), F32),
                   jax.ShapeDtypeStruct((n, DN_HEADS, DN_D, DN_D), F32)],
        grid=(n,),
        in_specs=[pl.BlockSpec((DN_C, DN_W), lambda i: (i, jblk)),
                  pl.BlockSpec((HALO, DN_W), lambda i: (jnp.maximum(i * per - 1, 0), jblk)),
                  full((4, 1536)), full((1, 128)), full((1, 128)), full((1, 128))],
        out_specs=[pl.BlockSpec((DN_C, 512), lambda i: (i, 0)),
                   pl.BlockSpec((1, DN_HEADS, DN_D, DN_D), lambda i: (i, 0, 0, 0)),
                   pl.BlockSpec((1, DN_HEADS, DN_D, DN_D), lambda i: (i, 0, 0, 0))],
        scratch_shapes=[pltpu.VMEM((DN_HEADS, DN_D, DN_D), F32), pltpu.VMEM((HALO + DN_C, 1536), F32)],
        compiler_params=pltpu.CompilerParams(dimension_semantics=("arbitrary",)),
    )(cols, cols, cw, al, dt, dn)


def dn_backward(cols, jblk, cw, al, dt, dn, ss, ts, dy, dcols, name):
    T = cols.shape[0]
    n = T // DN_C

    def body(x_ref, halo_ref, cw_ref, al_ref, dt_ref, dn_ref, ss_ref, ts_ref, dy_ref, dcols_in,
             dx_ref, dcw_ref, dal_ref, ddt_ref, ddn_ref, ds_scr, xbuf, dbuf, carry):
        i = pl.program_id(0)

        @pl.when(i == 0)
        def _():
            ds_scr[...] = jnp.zeros_like(ds_scr)
            carry[...] = jnp.zeros_like(carry)
            dcw_ref[...] = jnp.zeros_like(dcw_ref)
            dal_ref[...] = jnp.zeros_like(dal_ref)
            ddt_ref[...] = jnp.zeros_like(ddt_ref)
            ddn_ref[...] = jnp.zeros_like(ddn_ref)

        xs = _load_shifted(xbuf, x_ref, halo_ref, i == n - 1)
        S = [ss_ref[0, h] for h in range(DN_HEADS)]
        cws = [cw_ref[k:k + 1, :] for k in range(4)]

        t_saved = [ts_ref[0, h] for h in range(DN_HEADS)]

        def f(S, xs, ba, z, cws, al, dt, dn):
            return _dn_chunk(S, xs, ba, z, cws, al, dt, dn, t_saved)[:2]

        _, vjp = jax.vjp(f, S, xs, x_ref[:, 2048:2176], x_ref[:, 1536:2048], cws, al_ref[...], dt_ref[...], dn_ref[...])
        dS, dxs, dba, dz, dcws, dal, ddt, ddn = vjp((dy_ref[...], tuple(ds_scr[h] for h in range(DN_HEADS))))
        for h in range(DN_HEADS):
            ds_scr[h] = dS[h]
        dbuf[...] = jnp.zeros_like(dbuf)
        for k in range(4):
            lo = HALO - 3 + k
            dbuf[lo:lo + DN_C, :] += dxs[k]
        dbuf[DN_C:DN_C + HALO, :] += carry[...]
        dx_ref[...] = jnp.concatenate([dbuf[HALO:HALO + DN_C, :], dz, dba,
                                       jnp.zeros((DN_C, DN_W - 2176), F32)], axis=1).astype(dx_ref.dtype)
        carry[...] = dbuf[0:HALO, :]
        for k in range(4):
            dcw_ref[k:k + 1, :] += dcws[k]
        dal_ref[...] += dal
        ddt_ref[...] += ddt
        ddn_ref[...] += ddn

    per = DN_C // HALO
    rev = lambda i: n - 1 - i
    full = lambda shape: pl.BlockSpec(shape, lambda i: (0,) * len(shape))
    return pl.pallas_call(
        body, name=name,
        out_shape=[jax.ShapeDtypeStruct(dcols.shape, dcols.dtype),jax.ShapeDtypeStruct((4, 1536), F32),
                   jax.ShapeDtypeStruct((1, 128), F32), jax.ShapeDtypeStruct((1, 128), F32),
                   jax.ShapeDtypeStruct((1, 128), F32)],
        grid=(n,),
        in_specs=[pl.BlockSpec((DN_C, DN_W), lambda i: (rev(i), jblk)),
                  pl.BlockSpec((HALO, DN_W), lambda i: (jnp.maximum(rev(i) * per - 1, 0), jblk)),
                  full((4, 1536)), full((1, 128)), full((1, 128)), full((1, 128)),
                  pl.BlockSpec((1, DN_HEADS, DN_D, DN_D), lambda i: (rev(i), 0, 0, 0)),
                  pl.BlockSpec((1, DN_HEADS, DN_D, DN_D), lambda i: (rev(i), 0, 0, 0)),
                  pl.BlockSpec((DN_C, 512), lambda i: (rev(i), 0)), _ANY],
        out_specs=[pl.BlockSpec((DN_C, DN_W), lambda i: (rev(i), jblk)),
                   full((4, 1536)), full((1, 128)), full((1, 128)), full((1, 128))],
        scratch_shapes=[pltpu.VMEM((DN_HEADS, DN_D, DN_D), F32), pltpu.VMEM((HALO + DN_C, 1536), F32),
                        pltpu.VMEM((HALO + DN_C, 1536), F32), pltpu.VMEM((HALO, 1536), F32)],
        input_output_aliases={9: 0},
        compiler_params=pltpu.CompilerParams(dimension_semantics=("arbitrary",)),
    )(cols, cols, cw, al, dt, dn, ss, ts, dy, dcols)


def _full(shape):
    return pl.BlockSpec(shape, lambda i: (0,) * len(shape))


def _silu(x):
    return x * _sigmoid(x)


def _gelu(x):
    return 0.5 * x * (1.0 + jnp.tanh(0.7978845608028654 * (x + 0.044715 * (x * x * x))))


def _lane_col(mat, idx):
    lane = lax.broadcasted_iota(jnp.int32, (1, mat.shape[1]), 1)
    return jnp.sum(jnp.where(lane == idx, mat, 0.0), axis=1, keepdims=True)


def _gm_chunk(uv, z, gain, ws, bt):
    g = _gelu(uv)
    u, v = g[:, :512], g[:, 512:]
    v = v * lax.rsqrt(jnp.mean(v * v, axis=1, keepdims=True) + EPS) * gain
    row_i = lax.broadcasted_iota(jnp.int32, (BLK, BLK), 0)
    col_i = lax.broadcasted_iota(jnp.int32, (BLK, BLK), 1)
    causal = row_i >= col_i
    ss = []
    for grp in range(4):
        wg = jnp.where(causal, ws[grp], 0.0)
        ss.append(bdot("nn", wg, v[:, BLK * grp:BLK * (grp + 1)]) + _lane_col(bt, grp))
    return u * jnp.concatenate(ss, axis=1) * _silu(z)


def gm_forward(cols, jblk, gain, ws, bt, name):
    T = cols.shape[0]

    def body(x_ref, gain_ref, ws_ref, bt_ref, y_ref):
        y_ref[...] = _gm_chunk(x_ref[:, 0:1024], x_ref[:, 1024:1536], gain_ref[...],
                               [ws_ref[g] for g in range(4)], bt_ref[...]).astype(y_ref.dtype)

    return pl.pallas_call(
        body, name=name, out_shape=jax.ShapeDtypeStruct((T, 512), Y_DTYPE), grid=(T // BLK,),
        in_specs=[pl.BlockSpec((BLK, GM_W), lambda i: (i, jblk)),
                  _full((1, 512)), _full((4, BLK, BLK)), _full((BLK, BLK))],
        out_specs=pl.BlockSpec((BLK, 512), lambda i: (i, 0)),
        compiler_params=pltpu.CompilerParams(dimension_semantics=("parallel",)),
    )(cols, gain, ws, bt)


def gm_backward(cols, jblk, gain, ws, bt, dy, dcols, name):
    T = cols.shape[0]

    def body(x_ref, gain_ref, ws_ref, bt_ref, dy_ref, dcols_in, dx_ref, dgain_ref, dws_ref, dbt_ref):
        @pl.when(pl.program_id(0) == 0)
        def _():
            dgain_ref[...] = jnp.zeros_like(dgain_ref)
            dws_ref[...] = jnp.zeros_like(dws_ref)
            dbt_ref[...] = jnp.zeros_like(dbt_ref)

        _, vjp = jax.vjp(_gm_chunk, x_ref[:, 0:1024], x_ref[:, 1024:1536], gain_ref[...],
                         [ws_ref[g] for g in range(4)], bt_ref[...])
        duv, dz, dgain, dws, dbt = vjp(dy_ref[...])
        dx_ref[...] = jnp.concatenate([duv, dz], axis=1).astype(dx_ref.dtype)
        dgain_ref[...] += dgain
        for g in range(4):
            dws_ref[g] += dws[g]
        dbt_ref[...] += dbt

    return pl.pallas_call(
        body, name=name,
        out_shape=[jax.ShapeDtypeStruct(dcols.shape, dcols.dtype),jax.ShapeDtypeStruct((1, 512), F32),
                   jax.ShapeDtypeStruct((4, BLK, BLK), F32), jax.ShapeDtypeStruct((BLK, BLK), F32)],
        grid=(T // BLK,),
        in_specs=[pl.BlockSpec((BLK, GM_W), lambda i: (i, jblk)),
                  _full((1, 512)), _full((4, BLK, BLK)), _full((BLK, BLK)),
                  pl.BlockSpec((BLK, 512), lambda i: (i, 0)), _ANY],
        out_specs=[pl.BlockSpec((BLK, GM_W), lambda i: (i, jblk)),
                   _full((1, 512)), _full((4, BLK, BLK)), _full((BLK, BLK))],
        input_output_aliases={5: 0},
        compiler_params=pltpu.CompilerParams(dimension_semantics=("arbitrary",)),
    )(cols, gain, ws, bt, dy, dcols)


def _sw_block(first, q, kp, kc, vp, vc, z, sinks):
    P = BLK
    lane = lax.broadcasted_iota(jnp.int32, (1, 128), 1)
    r = lax.broadcasted_iota(jnp.int32, (128, 128), 0)
    c = lax.broadcasted_iota(jnp.int32, (128, 128), 1)
    swap = (c == (r + 64) % 128).astype(F32)
    k2 = jnp.concatenate([kp, kc], axis=0)
    v2 = jnp.concatenate([vp, vc], axis=0)
    k2s = bdot("nn", k2, swap)
    v2s = bdot("nn", v2, swap)
    qi = lax.broadcasted_iota(jnp.int32, (P, 2 * P), 0)
    kj = lax.broadcasted_iota(jnp.int32, (P, 2 * P), 1)
    dist = qi + P - kj
    valid = (dist >= 0) & (dist < P) & ((kj >= P) | jnp.logical_not(first))
    outs = []
    for j in range(4):
        acc = jnp.zeros((P, 128), F32)
        for half in range(2):
            h = 2 * j + half
            kv = h // 4
            in_half = (lane >= 64 * half) & (lane < 64 * half + 64)
            qh = jnp.where(in_half, q[:, 128 * j:128 * (j + 1)], 0.0)
            same = (half == kv)
            s = bdot("nt", qh, k2 if same else k2s) * (64 ** -0.5)
            s = jnp.where(valid, s, NEG_INF)
            sink = _lane_col(sinks, h)
            m = lax.stop_gradient(jnp.maximum(jnp.max(s, axis=1, keepdims=True), sink))
            e = jnp.exp(s - m)
            p = e / (jnp.sum(e, axis=1, keepdims=True) + jnp.exp(sink - m))
            o = bdot("nn", p, v2 if same else v2s)
            acc = acc + jnp.where(in_half, o, 0.0)
        outs.append(acc)
    return jnp.concatenate(outs, axis=1) * _silu(z)


def _sw_specs(jblk, idx):
    prev = lambda i: jnp.maximum(idx(i) - 1, 0)
    jk = (jblk * SW_W + 1024) // 128
    return [pl.BlockSpec((BLK, SW_W), lambda i: (idx(i), jblk)),
            pl.BlockSpec((BLK, 128), lambda i: (prev(i), jk)),
            pl.BlockSpec((BLK, 128), lambda i: (prev(i), jk + 1)), _full((1, 128))]


def sw_forward(cols, jblk, sinks, name):
    T = cols.shape[0]

    def body(x_ref, kp_ref, vp_ref, s_ref, y_ref):
        y_ref[...] = _sw_block(pl.program_id(0) == 0, x_ref[:, 0:512], kp_ref[...], x_ref[:, 1024:1152],
                               vp_ref[...], x_ref[:, 1152:1280], x_ref[:, 512:1024], s_ref[...]).astype(y_ref.dtype)

    return pl.pallas_call(
        body, name=name, out_shape=jax.ShapeDtypeStruct((T, 512), Y_DTYPE), grid=(T // BLK,),
        in_specs=_sw_specs(jblk, lambda i: i),
        out_specs=pl.BlockSpec((BLK, 512), lambda i: (i, 0)),
        compiler_params=pltpu.CompilerParams(dimension_semantics=("parallel",)),
    )(cols, cols, cols, sinks)


def sw_backward(cols, jblk, sinks, dy, dcols, name):
    T = cols.shape[0]
    n = T // BLK
    rev = lambda i: n - 1 - i

    def body(x_ref, kp_ref, vp_ref, s_ref, dy_ref, dcols_in, dx_ref, ds_ref, kcarry, vcarry):
        i = pl.program_id(0)

        @pl.when(i == 0)
        def _():
            kcarry[...] = jnp.zeros_like(kcarry)
            vcarry[...] = jnp.zeros_like(vcarry)
            ds_ref[...] = jnp.zeros_like(ds_ref)

        f = functools.partial(_sw_block, i == n - 1)
        _, vjp = jax.vjp(f, x_ref[:, 0:512], kp_ref[...], x_ref[:, 1024:1152], vp_ref[...], x_ref[:, 1152:1280],
                         x_ref[:, 512:1024], s_ref[...])
        dq, dkp, dkc, dvp, dvc, dz, dsk = vjp(dy_ref[...])
        dx_ref[...] = jnp.concatenate([dq, dz, dkc + kcarry[...], dvc + vcarry[...],
                                       jnp.zeros((BLK, SW_W - 1280), F32)], axis=1).astype(dx_ref.dtype)
        kcarry[...] = dkp
        vcarry[...] = dvp
        ds_ref[...] += dsk

    return pl.pallas_call(
        body, name=name,
        out_shape=[jax.ShapeDtypeStruct(dcols.shape, dcols.dtype),jax.ShapeDtypeStruct((1, 128), F32)],
        grid=(n,),
        in_specs=_sw_specs(jblk, rev) + [pl.BlockSpec((BLK, 512), lambda i: (rev(i), 0)), _ANY],
        out_specs=[pl.BlockSpec((BLK, SW_W), lambda i: (rev(i), jblk)), _full((1, 128))],
        scratch_shapes=[pltpu.VMEM((BLK, 128), F32), pltpu.VMEM((BLK, 128), F32)],
        input_output_aliases={5: 0},
        compiler_params=pltpu.CompilerParams(dimension_semantics=("arbitrary",)),
    )(cols, cols, cols, sinks, dy, dcols)


XM_TQ = 512


def _xm_block(q, z, mkv):
    outs = []
    for h in range(4):
        s = bdot("nt", q[:, 128 * h:128 * (h + 1)], mkv[:, 128 * h:128 * (h + 1)]) * (128 ** -0.5)
        m = lax.stop_gradient(jnp.max(s, axis=1, keepdims=True))
        e = jnp.exp(s - m)
        p = e / jnp.sum(e, axis=1, keepdims=True)
        outs.append(bdot("nn", p, mkv[:, 512 + 128 * h:512 + 128 * (h + 1)]))
    return jnp.concatenate(outs, axis=1) * _silu(z)


def xm_forward(cols, jblk, mkv, name):
    T = cols.shape[0]

    def body(x_ref, m_ref, y_ref):
        y_ref[...] = _xm_block(x_ref[:, 0:512], x_ref[:, 512:1024], m_ref[...]).astype(y_ref.dtype)

    return pl.pallas_call(
        body, name=name, out_shape=jax.ShapeDtypeStruct((T, 512), Y_DTYPE), grid=(T // XM_TQ,),
        in_specs=[pl.BlockSpec((XM_TQ, XM_W), lambda i: (i, jblk)), _full(mkv.shape)],
        out_specs=pl.BlockSpec((XM_TQ, 512), lambda i: (i, 0)),
        compiler_params=pltpu.CompilerParams(dimension_semantics=("parallel",)),
    )(cols, mkv)


def xm_backward(cols, jblk, mkv, dy, dcols, name):
    T = cols.shape[0]

    def body(x_ref, m_ref, dy_ref, dcols_in, dx_ref, dm_ref):
        @pl.when(pl.program_id(0) == 0)
        def _():
            dm_ref[...] = jnp.zeros_like(dm_ref)

        _, vjp = jax.vjp(_xm_block, x_ref[:, 0:512], x_ref[:, 512:1024], m_ref[...])
        dq, dz, dm = vjp(dy_ref[...])
        dx_ref[...] = jnp.concatenate([dq, dz], axis=1).astype(dx_ref.dtype)
        dm_ref[...] += dm

    return pl.pallas_call(
        body, name=name,
        out_shape=[jax.ShapeDtypeStruct(dcols.shape, dcols.dtype),jax.ShapeDtypeStruct(mkv.shape, F32)],
        grid=(T // XM_TQ,),
        in_specs=[pl.BlockSpec((XM_TQ, XM_W), lambda i: (i, jblk)), _full(mkv.shape),
                  pl.BlockSpec((XM_TQ, 512), lambda i: (i, 0)), _ANY],
        out_specs=[pl.BlockSpec((XM_TQ, XM_W), lambda i: (i, jblk)), _full(mkv.shape)],
        input_output_aliases={3: 0},
        compiler_params=pltpu.CompilerParams(dimension_semantics=("arbitrary",)),
    )(cols, mkv, dy, dcols)


def _rms(x, gain):
    return x * lax.rsqrt(jnp.mean(x * x, axis=1, keepdims=True) + EPS) * gain


def memkv_forward(mem, gain, w, name):
    def body(m_ref, g_ref, w_ref, o_ref):
        o_ref[...] = bdot("nn", _rms(m_ref[...], g_ref[...]), w_ref[...])

    return pl.pallas_call(body, name=name, out_shape=jax.ShapeDtypeStruct(mem.shape, F32),
                          compiler_params=pltpu.CompilerParams(vmem_limit_bytes=VMEM_LIMIT))(mem, gain, w)


def memkv_backward(mem, gain, w, dkv, name):
    def body(m_ref, g_ref, w_ref, d_ref, dg_ref, dw_ref):
        mem_v = m_ref[...]
        _, vjp = jax.vjp(lambda g, ww: bdot("nn", _rms(mem_v, g), ww), g_ref[...], w_ref[...].astype(F32))
        dg, dw = vjp(d_ref[...])
        dg_ref[...] = dg
        dw_ref[...] = dw

    return pl.pallas_call(body, name=name,
                          out_shape=[jax.ShapeDtypeStruct(gain.shape, F32), jax.ShapeDtypeStruct(w.shape, F32)],
                          compiler_params=pltpu.CompilerParams(vmem_limit_bytes=VMEM_LIMIT))(mem, gain, w, dkv)


MG_TB = 256


def _merge_block(ys, gl, wup, wout, gpost):
    merged = None
    for n in range(4):
        t = _sigmoid(gl[:, 1024 * n:1024 * (n + 1)]) * bdot("nn", ys[n], wup[n])
        merged = t if merged is None else merged + t
    out = bdot("nn", merged, wout)
    return _rms(out, gpost)


def merge_forward(ys, cols, jgate, x, wup, wout, gpost, name, next_gain=None, target=None):
    T, D = x.shape
    TB = 512
    n_extra = (next_gain is not None) + (target is not None)

    def body(ya, yb, yc, ym, gl_ref, x_ref, wup_ref, wout_ref, gp_ref, *rest):
        extra, outs = rest[:n_extra], rest[n_extra:]
        upd = _merge_block([ya[...], yb[...], yc[...], ym[...]], gl_ref[...],
                           [wup_ref[n] for n in range(4)], wout_ref[...], gp_ref[...])
        y = x_ref[...] + upd
        outs[0][...] = y
        outs = outs[1:]
        if next_gain is not None:
            h = _rms(y, extra[0][...])
            outs[0][...] = h.astype(BF)
            outs[1][...] = h.T.astype(BF)
            outs = outs[2:]
        if target is not None:
            l_ref, d_ref = outs

            @pl.when(pl.program_id(0) == 0)
            def _():
                l_ref[...] = jnp.zeros_like(l_ref)

            err = y - extra[-1][...]
            d_ref[...] = err * (1.0 / D)
            l_ref[...] += jnp.full(l_ref.shape, 0.5 * jnp.sum(jnp.mean(err * err, axis=1, keepdims=True)), F32)

    yspec = pl.BlockSpec((TB, 512), lambda i: (i, 0))
    xspec = pl.BlockSpec((TB, D), lambda i: (i, 0))
    extra_in, extra_specs = [], []
    out_shape, out_specs = [jax.ShapeDtypeStruct((T, D), F32)], [xspec]
    if next_gain is not None:
        extra_in, extra_specs = extra_in + [next_gain], extra_specs + [_full((1, D))]
        out_shape += [jax.ShapeDtypeStruct((T, D), BF), jax.ShapeDtypeStruct((D, T), BF)]
        out_specs += [xspec, pl.BlockSpec((D, TB), lambda i: (0, i))]
    if target is not None:
        extra_in, extra_specs = extra_in + [target], extra_specs + [xspec]
        out_shape += [jax.ShapeDtypeStruct((1, 128), F32), jax.ShapeDtypeStruct((T, D), F32)]
        out_specs += [_full((1, 128)), xspec]
    return pl.pallas_call(
        body, name=name, out_shape=out_shape, grid=(T // TB,),
        in_specs=[yspec] * 4 + [pl.BlockSpec((TB, 4096), lambda i: (i, jgate)), xspec,
                                _full(wup.shape), _full(wout.shape), _full((1, D))] + extra_specs,
        out_specs=out_specs,
        compiler_params=pltpu.CompilerParams(
            dimension_semantics=("parallel" if target is None else "arbitrary",), vmem_limit_bytes=VMEM_LIMIT),
    )(*ys, cols, x, wup, wout, gpost, *extra_in)


def _token_product(a, b, name):
    (T, m), n = a.shape, b.shape[1]

    def body(a_ref, b_ref, o_ref):
        o_ref[...] = lax.dot_general(a_ref[...].astype(BF), b_ref[...].astype(BF), _DIMS["tn"], preferred_element_type=F32)

    return pl.pallas_call(body, name=name, out_shape=jax.ShapeDtypeStruct((m, n), F32),
                          compiler_params=pltpu.CompilerParams(vmem_limit_bytes=VMEM_LIMIT))(a, b)


def merge_backward(ys, cols, jgate, wup, wout, gpost, dx, name):
    T = dx.shape[0]
    TB = MG_TB

    def body(ya, yb, yc, ym, gl_ref, wup_ref, wout_ref, gp_ref, dx_ref,
             dgl_ref, dya, dyb, dyc, dym, dpa, dpb, dpc, dpm, merged_ref, dout_ref, dgp_ref):
        @pl.when(pl.program_id(0) == 0)
        def _():
            dgp_ref[...] = jnp.zeros_like(dgp_ref)

        y_refs = (ya, yb, yc, ym)
        gates = [_sigmoid(gl_ref[:, 1024 * n:1024 * (n + 1)]) for n in range(4)]
        projs = [bdot("nn", y_refs[n][...], wup_ref[n]) for n in range(4)]
        merged = gates[0] * projs[0] + gates[1] * projs[1] + gates[2] * projs[2] + gates[3] * projs[3]
        out = bdot("nn", merged, wout_ref[...])
        _, vjp = jax.vjp(_rms, out, gp_ref[...])
        dout, dgp = vjp(dx_ref[...])
        dmerged = bdot("nt", dout, wout_ref[...])
        for n, (dy_ref, dp_ref) in enumerate(zip((dya, dyb, dyc, dym), (dpa, dpb, dpc, dpm))):
            dproj = dmerged * gates[n]
            dgl_ref[:, 1024 * n:1024 * (n + 1)] = (dmerged * projs[n] * gates[n] * (1.0 - gates[n])).astype(dgl_ref.dtype)
            dy_ref[...] = bdot("nt", dproj, wup_ref[n])
            dp_ref[...] = dproj.astype(BF)
        merged_ref[...] = merged.astype(BF)
        dout_ref[...] = dout.astype(BF)
        dgp_ref[...] += dgp

    yspec = pl.BlockSpec((TB, 512), lambda i: (i, 0))
    dspec = pl.BlockSpec((TB, 1024), lambda i: (i, 0))
    dcols, dya, dyb, dyc, dym, *dproj, merged, dout, dgp = pl.pallas_call(
        body, name=name,
        out_shape=[jax.ShapeDtypeStruct(cols.shape, BF)] + [jax.ShapeDtypeStruct((T, 512), F32)] * 4 + [
            jax.ShapeDtypeStruct((T, 1024), BF)] * 6 + [jax.ShapeDtypeStruct((1, 1024), F32)],
        grid=(T // TB,),
        in_specs=[yspec] * 4 + [pl.BlockSpec((TB, 4096), lambda i: (i, jgate)),
                                _full(wup.shape), _full(wout.shape), _full((1, 1024)), dspec],
        out_specs=[pl.BlockSpec((TB, 4096), lambda i: (i, jgate))] + [yspec] * 4 + [
            dspec] * 6 + [_full((1, 1024))],
        compiler_params=pltpu.CompilerParams(dimension_semantics=("arbitrary",), vmem_limit_bytes=VMEM_LIMIT),
    )(*ys, cols, wup, wout, gpost, dx)
    dwup = jnp.stack([_token_product(ys[n], dproj[n], "%s_w_up%d" % (name, n)) for n in range(4)])
    dwout = _token_product(merged, dout, name + "_w_out")
    return dcols, dya, dyb, dyc, dym, dwup, dwout, dgp


NB = 256


def prenorm_forward(x, gain, name):
    T, D = x.shape

    def body(x_ref, g_ref, o_ref, ot_ref):
        h = _rms(x_ref[...], g_ref[...])
        o_ref[...] = h.astype(BF)
        ot_ref[...] = h.T.astype(BF)

    return pl.pallas_call(
        body, name=name,
        out_shape=[jax.ShapeDtypeStruct((T, D), BF), jax.ShapeDtypeStruct((D, T), BF)], grid=(T // NB,),
        in_specs=[pl.BlockSpec((NB, D), lambda i: (i, 0)), _full((1, D))],
        out_specs=[pl.BlockSpec((NB, D), lambda i: (i, 0)), pl.BlockSpec((D, NB), lambda i: (0, i))],
        compiler_params=pltpu.CompilerParams(dimension_semantics=("parallel",)),
    )(x, gain)


def prenorm_backward(x, gain, dh, dres, name):
    T = x.shape[0]

    def body(x_ref, g_ref, dh_ref, dr_ref, dx_ref, dg_ref):
        @pl.when(pl.program_id(0) == 0)
        def _():
            dg_ref[...] = jnp.zeros_like(dg_ref)

        _, vjp = jax.vjp(_rms, x_ref[...], g_ref[...])
        dxn, dg = vjp(dh_ref[...])
        dx_ref[...] = dr_ref[...] + dxn
        dg_ref[...] += dg

    spec = pl.BlockSpec((NB, 1024), lambda i: (i, 0))
    return pl.pallas_call(
        body, name=name,
        out_shape=[jax.ShapeDtypeStruct(x.shape, F32), jax.ShapeDtypeStruct((1, 1024), F32)], grid=(T // NB,),
        in_specs=[spec, _full((1, 1024)), spec, spec], out_specs=[spec, _full((1, 1024))],
        compiler_params=pltpu.CompilerParams(dimension_semantics=("arbitrary",)),
    )(x, gain, dh, dres)


JB_GATE, JB_XM, JB_DN, JB_SW, JB_GM = 0, 4, 2, 5, 6
_ALIGNED_PIECES = ((5896, 4096), (4872, 512), (5384, 512), (0, 2048), (2048, 8), 504, (3592, 512), (4360, 512),
                   (4104, 128), (4232, 128), 256, (2056, 1024), (3080, 512))
_NATURAL_FROM_ALIGNED = ((5120, 2048), (7168, 8), (9216, 1024), (10240, 512), (7680, 512), (8704, 128), (8832, 128),
                         (8192, 512), (4096, 512), (4608, 512), (0, 4096))


def _natural_range(slots, start, width):
    out = []
    while width > 0:
        j, i = divmod(start, W_IN_SHARD)
        take = min(width, W_IN_SHARD - i)
        out.append(slots[j, :, i:i + take])
        start, width = start + take, width - take
    return out


def _aligned_w_in(slots):
    parts = []
    for piece in _ALIGNED_PIECES:
        if isinstance(piece, int):
            parts.append(jnp.zeros(slots.shape[1:2] + (piece,), slots.dtype))
        else:
            parts += _natural_range(slots, *piece)
    return jnp.concatenate(parts, axis=-1)


def _slots_of_aligned(d_al):
    slots = []
    for s in range(N_DEV):
        lo, hi = s * W_IN_SHARD, (s + 1) * W_IN_SHARD
        parts, nat = [], 0
        for a_start, width in _NATURAL_FROM_ALIGNED:
            b, e = max(lo, nat), min(hi, nat + width)
            if b < e:
                parts.append(d_al[..., a_start + b - nat:a_start + e - nat])
            nat += width
        parts.append(jnp.zeros(d_al.shape[:1] + (W_IN_SHARD_PAD - W_IN_SHARD,), d_al.dtype))
        slots.append(jnp.concatenate(parts, axis=-1))
    return jnp.stack(slots)


SMALL_VEC_W = 1024


def _pack_small(parts):
    rows = []
    for p in parts:
        flat = p.reshape(-1).astype(F32)
        r = -(-flat.shape[0] // SMALL_VEC_W)
        rows.append(jnp.pad(flat, (0, r * SMALL_VEC_W - flat.shape[0])).reshape(r, SMALL_VEC_W))
    vec = jnp.concatenate(rows, axis=0)
    return jnp.pad(vec, ((0, -vec.shape[0] % 8), (0, 0)))


def _unpack_small(vec, shapes):
    out, off = [], 0
    for s in shapes:
        n = math.prod(s)
        r = -(-n // SMALL_VEC_W)
        out.append(vec[off:off + r].reshape(-1)[:n].reshape(s))
        off += r
    return out


def _lanes(vec, at):
    return jnp.zeros((1, 128), F32).at[0, at:at + vec.shape[0]].set(vec)


SMALL_NAMES = ("norm_pre", "norm_post", "norm_mem", "a_log", "dt_bias", "dn_norm", "gm_norm",
               "spatial_w", "spatial_b", "sinks")


def _other_weights(s_mem, s_up, s_out):
    return (s_mem.reshape(D_MODEL, 2 * BRANCH_W),
            jnp.transpose(s_up, (1, 2, 0, 3)).reshape(N_BRANCH, BRANCH_W, D_MODEL), s_out.reshape(D_MODEL, D_MODEL))


def _grad_slots(d_in_al, d_mem, d_up, d_out):
    return [None if d_in_al is None else _slots_of_aligned(d_in_al), d_mem.astype(BF).reshape(N_DEV, 128, 2 * BRANCH_W),
            jnp.transpose(d_up.astype(BF).reshape(N_BRANCH, BRANCH_W, N_DEV, 128), (2, 0, 1, 3)),
            d_out.astype(BF).reshape(N_DEV, 128, D_MODEL)]


def _layer_params(l, small, conv_full, token):
    return dict(
        gpre=small["norm_pre"][l][None] + token, gpost=small["norm_post"][l][None], gmem=small["norm_mem"][l][None],
        cw=conv_full[l], al=_lanes(small["a_log"][l], 4), dt=_lanes(small["dt_bias"][l], 4),
        dnn=small["dn_norm"][l][None], gain=small["gm_norm"][l][None], ws=small["spatial_w"][l],
        bt=jnp.zeros((128, 128), F32).at[:, :GM_GROUPS].set(small["spatial_b"][l].T),
        sinks=_lanes(small["sinks"][l], 0))


def _layer_forward(l, xl, hs, mem, p, w_in_al, other_weights, **tail):
    t = "l%d_" % l
    h, h_t = hs
    cols = _matmul(h, w_in_al, "nn", F32, (1024, 1536, 1024), t + "w_in")
    ya, ss, ts = dn_forward(cols, JB_DN, p["cw"], p["al"], p["dt"], p["dnn"], t + "deltanet")
    yb = gm_forward(cols, JB_GM, p["gain"], p["ws"], p["bt"], t + "gmlp")
    yc = sw_forward(cols, JB_SW, p["sinks"], t + "swa")
    w_mem, w_up, w_out = other_weights(yc)
    mkv = memkv_forward(mem, p["gmem"], w_mem, t + "memkv")
    ym = xm_forward(cols, JB_XM, mkv, t + "memattn")
    outs = merge_forward([ya, yb, yc, ym], cols, JB_GATE, xl, w_up, w_out, p["gpost"], t + "merge", **tail)
    return outs, dict(p, x=xl, h_t=h_t, cols=cols, mkv=mkv, ss=ss, ts=ts, ys=[ya, yb, yc, ym]), (w_in_al, w_mem, w_up, w_out)


def _layer_backward(l, s, mem, weights, dx, token, early=None):
    w_in_al, w_mem, w_up, w_out = weights
    t = "l%d_" % l
    cols = s["cols"]
    dcols, dya, dyb, dyc, dym, dwup, dwout, dgpost = merge_backward(
        s["ys"], cols, JB_GATE, w_up, w_out, s["gpost"] + token, dx, t + "merge_bwd")
    dcols, dmkv = xm_backward(cols, JB_XM, s["mkv"], dym, dcols, t + "memattn_bwd")
    dgmem, dwmem = memkv_backward(mem, s["gmem"], w_mem, dmkv, t + "memkv_bwd")
    sinks = s["sinks"] if early is None else s["sinks"] + early(dwmem, dwup, dwout)
    dcols, dsinks = sw_backward(cols, JB_SW, sinks, dyc, dcols, t + "swa_bwd")
    dcols, dgain, dws, dbt = gm_backward(cols, JB_GM, s["gain"], s["ws"], s["bt"], dyb, dcols, t + "gmlp_bwd")
    dcols, dcw, dal, ddt, ddn = dn_backward(
        cols, JB_DN, s["cw"], s["al"], s["dt"], s["dnn"], s["ss"], s["ts"], dya, dcols, t + "deltanet_bwd")
    dh = _matmul(dcols, w_in_al, "nt", F32, (1024, 1024, 3584), t + "w_in_bwd_x")
    dwin = _matmul(s["h_t"], dcols, "nn", BF, (1024, 1536, 2048), t + "w_in_bwd_w")
    dx, dgpre = prenorm_backward(s["x"], s["gpre"], dh, dx, t + "prenorm_bwd")
    gsmall = dict(norm_pre=dgpre[0], norm_post=dgpost[0], norm_mem=dgmem[0], a_log=dal[0, 4:8], dt_bias=ddt[0, 4:8],
                  dn_norm=ddn[0], gm_norm=dgain[0], spatial_w=dws, spatial_b=dbt[:, :GM_GROUPS].T,
                  sinks=dsinks[0, :SW_HEADS], conv_w=dcw)
    return dx, gsmall, (dwin, dwmem, dwup, dwout)


def kernel(x, mem, norm_pre, norm_post, norm_mem, w_in, conv_w, a_log, dt_bias, dn_norm, gm_norm, spatial_w, spatial_b, sinks, w_mem_kv, w_up, w_out, loss_target, m_norm_pre, m_norm_post, m_norm_mem, m_w_in, m_conv_w, m_a_log, m_dt_bias, m_dn_norm, m_gm_norm, m_spatial_w, m_spatial_b, m_sinks, m_w_mem_kv, m_w_up, m_w_out, v_norm_pre, v_norm_post, v_norm_mem, v_w_in, v_conv_w, v_a_log, v_dt_bias, v_dn_norm, v_gm_norm, v_spatial_w, v_spatial_b, v_sinks, v_w_mem_kv, v_w_up, v_w_out):
    xi, yi, ci = _my_place()
    my_slot = 4 * xi + 2 * yi + ci
    conv_shard = conv_w.shape[-1]
    x2, mem2, target = x[0], mem[0], loss_target[0]

    w_in_pad = jnp.pad(w_in.astype(BF), ((0, 0), (0, 0), (0, W_IN_SHARD_PAD - W_IN_SHARD)))
    shards = [[w_in_pad[l], w_mem_kv[l].astype(BF), w_up[l].astype(BF), w_out[l].astype(BF)] for l in range(DEPTH)]
    w_in_slots0, w_up_slots0, w_out_slots0, conv_slots = _all_gather_slots(
        [shards[0][0], shards[0][2], shards[0][3], conv_w], "gather_weights_l0")
    ag = list(_spread_start([shards[0][1]] + shards[1], "gather", "gather_weights_rest_start"))
    conv_full = jnp.transpose(conv_slots, (1, 2, 0, 3)).reshape(DEPTH, CONV_W, N_DEV * conv_shard)
    small = dict(norm_pre=norm_pre, norm_post=norm_post, norm_mem=norm_mem, a_log=a_log,
                 dt_bias=dt_bias, dn_norm=dn_norm, gm_norm=gm_norm, spatial_w=spatial_w,
                 spatial_b=spatial_b, sinks=sinks)

    def arrived(which, after, name):
        ag[2], ag[3] = _spread_wait(ag[0], ag[1], ag[2], ag[3], which, after, name)
        return [ag[3][a] for a in which]

    p0, p1 = _layer_params(0, small, conv_full, ag[4][0, 0]), _layer_params(1, small, conv_full, 0.0)
    (x1, h1, h1_t), saved0, weights0 = _layer_forward(
        0, x2, prenorm_forward(x2, p0["gpre"], "l0_prenorm"), mem2, p0, _aligned_w_in(w_in_slots0),
        lambda y: _other_weights(*arrived([0], y, "gather_weights_l0_w_mem_wait"), w_up_slots0, w_out_slots0),
        next_gain=p1["gpre"])
    w_in_slots1, = arrived([1], x1, "gather_weights_l1_w_in_wait")
    (x_out, loss, dx), saved1, weights1 = _layer_forward(
        1, x1, (h1, h1_t), mem2, p1, _aligned_w_in(w_in_slots1),
        lambda y: _other_weights(*arrived([2, 3, 4], y, "gather_weights_l1_rest_wait")), target=target)

    packed_names = SMALL_NAMES + ("conv_w",)
    dx, gsmall1, gbig1 = _layer_backward(1, saved1, mem2, weights1, dx, 0.0)
    small1 = [loss[0, :1]] + [gsmall1[n] for n in packed_names]
    sm1 = _spread_start([_pack_small(small1)], "gather", "gather_small_grads_l1_start")
    rs_send, rs_recv, rs_src, rs_land, rs_token = _spread_start(_grad_slots(*gbig1), "scatter", "exchange_grads_l1_start")
    rest0 = []

    def send_rest0(dwmem, dwup, dwout):
        rest0.extend(_spread_start(_grad_slots(None, dwmem, dwup, dwout)[1:], "scatter", "exchange_grads_l0_rest_start"))
        return rest0[4][0, 0]

    dx, gsmall0, gbig0 = _layer_backward(0, saved0, mem2, weights0, dx, rs_token[0, 0] + sm1[4][0, 0], send_rest0)
    _, parts1 = _spread_wait(rs_send, rs_recv, rs_src, rs_land, range(4), dx, "exchange_grads_l1_wait")

    small0 = [gsmall0[n] for n in packed_names]
    sm0 = _spread_start([_pack_small(small0)], "gather", "gather_small_grads_l0_start")

    g_win0 = _slots_of_aligned(gbig0[0])
    g_win0 = g_win0.reshape((N_DEV // 2, 2) + g_win0.shape[1:])
    theirs, = _exchange_sibling([g_win0], "exchange_sibling_l0")
    chip_sum = _pair_sum(g_win0, theirs, "pair_sum_l0")
    ch_send, ch_recv, ch_src, ch_land, ch_token = _spread_start([chip_sum], "chips", "exchange_chips_l0_start")

    _, (land1,) = _spread_wait(*sm1[:4], [0], ch_token, "gather_small_grads_l1_wait")
    _, (land0,) = _spread_wait(*sm0[:4], [0], land1, "gather_small_grads_l0_wait")
    tot1 = _unpack_small(_sum_slots(land1, "sum_small_grads_l1"), [p.shape for p in small1])
    tot0 = _unpack_small(_sum_slots(land0, "sum_small_grads_l0"), [p.shape for p in small0])
    loss_tot = tot1[0][0]
    grads = {n: jnp.stack([g0, g1]) for n, g0, g1 in zip(packed_names, tot0, tot1[1:])}
    grads["conv_w"] = lax.dynamic_slice_in_dim(grads["conv_w"], my_slot * conv_shard, conv_shard, axis=2)

    given = dict(norm_pre=(norm_pre, m_norm_pre, v_norm_pre), norm_post=(norm_post, m_norm_post, v_norm_post),
                 norm_mem=(norm_mem, m_norm_mem, v_norm_mem), a_log=(a_log, m_a_log, v_a_log),
                 dt_bias=(dt_bias, m_dt_bias, v_dt_bias), dn_norm=(dn_norm, m_dn_norm, v_dn_norm),
                 gm_norm=(gm_norm, m_gm_norm, v_gm_norm), spatial_w=(spatial_w, m_spatial_w, v_spatial_w),
                 spatial_b=(spatial_b, m_spatial_b, v_spatial_b), sinks=(sinks, m_sinks, v_sinks),
                 conv_w=(conv_w, m_conv_w, v_conv_w))
    pshapes = [given[n][0].shape for n in packed_names]
    pw, pm, pv = (_pack_small([given[n][i] for n in packed_names]) for i in range(3))
    pd, pnm, pnv = _adamw(pw + ch_token[0, 0], _pack_small([grads[n] for n in packed_names]), pm, pv, "adamw_small")
    upd = {n: t for n, t in zip(packed_names, zip(_unpack_small(pd, pshapes), _unpack_small(pnm, pshapes),
                                                  _unpack_small(pnv, pshapes)))}
    big = (("w_mem_kv", (w_mem_kv, m_w_mem_kv, v_w_mem_kv)), ("w_up", (w_up, m_w_up, v_w_up)),
           ("w_out", (w_out, m_w_out, v_w_out)))
    first = [_sum_adamw(parts1[1 + i], w, m, v, 1, None, ch_token, "adamw_%s_l1" % name)
             for i, (name, (w, m, v)) in enumerate(big)]
    _, parts0_rest = _spread_wait(*rest0[:4], range(3), first[-1][0], "exchange_grads_l0_rest_wait")
    for i, (name, (w, m, v)) in enumerate(big):
        g, d, nm, nv = _sum_adamw(parts0_rest[i], w, m, v, 0, first[i], None, "adamw_%s_l0" % name)
        grads[name], upd[name] = g, (d, nm, nv)
    _, (parts0_w_in,) = _spread_wait(ch_send, ch_recv, ch_src, ch_land, [0], upd["w_out"][0], "exchange_chips_l0_wait")
    w_in_t, m_w_in_t, v_w_in_t = (jnp.transpose(t, (2, 0, 1)) for t in (w_in, m_w_in, v_w_in))
    g, d, nm, nv = (jnp.transpose(t, (1, 2, 0)) for t in
                    _sum_adamw_t([parts0_w_in, parts1[0]], w_in_t, m_w_in_t, v_w_in_t, "adamw_w_in"))
    grads["w_in"], upd["w_in"] = g, (d, nm, nv)

    order = ("norm_pre", "norm_post", "norm_mem", "w_in", "conv_w", "a_log", "dt_bias", "dn_norm",
             "gm_norm", "spatial_w", "spatial_b", "sinks", "w_mem_kv", "w_up", "w_out")
    return (loss_tot, dx[None], *[grads[n] for n in order], *[upd[n][0] for n in order],
            *[upd[n][1] for n in order], *[upd[n][2] for n in order])
```

```python
import functools
import math

import jax
import jax.numpy as jnp
from jax import lax
from jax.experimental import pallas as pl
from jax.experimental.pallas import tpu as pltpu

MESH = pl.DeviceIdType.MESH
N_DEV = 8

D_MODEL = 1024
DEPTH = 2
N_BRANCH = 4
BRANCH_W = 512
DN_HEADS = 4
CONV_W = 4
GM_GROUPS = 4
SW_HEADS = 8
EPS = 1e-6
NEG_INF = -1e30

D_IN = 9992
W_IN_SHARD = D_IN // N_DEV
W_IN_SHARD_PAD = 1280
D_IN_AL = 10752
DN_W, SW_W, GM_W, XM_W = 2560, 1536, 1536, 1024

ADAM_LR = 0.001
ADAM_B1 = 0.9
ADAM_B2 = 0.999
ADAM_EPS = 1e-08
ADAM_WD = 0.01
ADAM_STEP = 10

VMEM_LIMIT = 56 * 1024 * 1024

BF = jnp.bfloat16
F32 = jnp.float32
DN_C = 128
DN_D = 128
HALO = 8
BLK = 128
Y_DTYPE = BF


def _my_place():
    return lax.axis_index("x"), lax.axis_index("y"), lax.axis_index("c")


_ANY = pl.BlockSpec(memory_space=pl.ANY)


def _all_gather_slots(parts, name):
    n = len(parts)

    def body(*refs):
        p_refs, out_refs = refs[:n], refs[n:2 * n]
        send_sems, recv_sems, local_sems = refs[2 * n:]
        x, y, c = _my_place()
        me, sibling = (x, y, c), (x, y, 1 - c)
        chips = [(1 - x, y), (x, 1 - y), (1 - x, 1 - y)]

        def copy(a, k, block, to, src=None):
            px, py, pc = block
            slot = out_refs[a].at[4 * px + 2 * py + pc]
            return pltpu.make_async_remote_copy(
                src_ref=slot if src is None else src, dst_ref=slot,
                send_sem=send_sems.at[7 * a + k], recv_sem=recv_sems.at[7 * a + k],
                device_id=to, device_id_type=MESH)

        mine = [pltpu.make_async_copy(p_refs[a], out_refs[a].at[4 * x + 2 * y + c], local_sems.at[a])
                for a in range(n)]
        for cp in mine:
            cp.start()
        first = []
        for a in range(n):
            first.append(copy(a, 0, me, sibling, src=p_refs[a]))
            first += [copy(a, 1 + j, me, (*chip, c), src=p_refs[a]) for j, chip in enumerate(chips)]
        for cp in first:
            cp.start()
        passed = []
        for j, chip in enumerate(chips):
            for a in range(n):
                copy(a, 1 + j, (*chip, c), me).wait_recv()
                fwd = copy(a, 4 + j, (*chip, c), sibling)
                fwd.start()
                passed.append(fwd)
        for a in range(n):
            copy(a, 0, sibling, me).wait_recv()
            for j, chip in enumerate(chips):
                copy(a, 4 + j, (*chip, 1 - c), me).wait_recv()
        for cp in first + passed:
            cp.wait_send()
        for cp in mine:
            cp.wait()

    return pl.pallas_call(
        body, name=name,
        out_shape=[jax.ShapeDtypeStruct((N_DEV,) + p.shape, p.dtype) for p in parts],
        in_specs=[_ANY] * n, out_specs=[_ANY] * n,
        scratch_shapes=[pltpu.SemaphoreType.DMA((7 * n,)), pltpu.SemaphoreType.DMA((7 * n,)),
                        pltpu.SemaphoreType.DMA((n,))],
    )(*parts)


def _exchange_sibling(parts, name):
    n = len(parts)

    def body(*refs):
        g_refs, out_refs = refs[:n], refs[n:2 * n]
        send_sems, recv_sems = refs[2 * n:]
        x, y, c = _my_place()
        copies = [pltpu.make_async_remote_copy(
            src_ref=g_refs[a].at[:, 1 - c], dst_ref=out_refs[a],
            send_sem=send_sems.at[a], recv_sem=recv_sems.at[a],
            device_id=(x, y, 1 - c), device_id_type=MESH) for a in range(n)]
        for cp in copies:
            cp.start()
        for cp in copies:
            cp.wait()

    return pl.pallas_call(
        body, name=name,
        out_shape=[jax.ShapeDtypeStruct((4,) + g.shape[2:], g.dtype) for g in parts],
        in_specs=[_ANY] * n, out_specs=[_ANY] * n,
        scratch_shapes=[pltpu.SemaphoreType.DMA((n,)), pltpu.SemaphoreType.DMA((n,))],
    )(*parts)


_HBM = pl.BlockSpec(memory_space=pltpu.HBM)
_SEM = pl.BlockSpec(memory_space=pltpu.SEMAPHORE)
_EFFECT = pltpu.SideEffectType.DATAFLOW_SIDE_EFFECTING


def _peer(x, y, c, k):
    return (1 - x if (k >> 2) & 1 else x, 1 - y if (k >> 1) & 1 else y, 1 - c if k & 1 else c)


def _spread_start(srcs, mode, name, slot_axis=None):
    n = len(srcs)
    slot_axis = slot_axis or [0] * n
    lands = [lax.empty((N_DEV,) + s.shape if mode == "gather"
                       else ((N_DEV, s.shape[0], s.shape[1] // N_DEV) if ax else s.shape), s.dtype)
             for s, ax in zip(srcs, slot_axis)]
    peers = range(0, N_DEV, 2) if mode == "chips" else range(N_DEV)

    def body(*refs):
        src_refs, land_refs = refs[:n], refs[n:2 * n]
        send_sems, recv_sems = refs[2 * n:2 * n + 2]
        token = refs[-1]
        x, y, c = _my_place()
        for a in range(n):
            for k in peers:
                px, py, pc = _peer(x, y, c, k)
                if mode == "chips":
                    src, mine = src_refs[a].at[2 * px + py], 2 * x + y
                else:
                    to_slot = 4 * px + 2 * py + pc
                    src = (src_refs[a] if mode == "gather" else
                           src_refs[a].at[:, pl.ds(pl.multiple_of(to_slot * (src_refs[a].shape[1] // N_DEV), 128),
                                                   src_refs[a].shape[1] // N_DEV)]
                           if slot_axis[a] else src_refs[a].at[to_slot])
                    mine = 4 * x + 2 * y + c
                pltpu.make_async_remote_copy(
                    src_ref=src, dst_ref=land_refs[a].at[mine],
                    send_sem=send_sems.at[a], recv_sem=recv_sems.at[a],
                    device_id=(px, py, pc), device_id_type=MESH).start()
        token[...] = jnp.zeros_like(token)

    out = pl.pallas_call(
        body, name=name,
        out_shape=[pltpu.SemaphoreType.DMA((n,)), pltpu.SemaphoreType.DMA((n,))]
        + [pltpu.HBM(s.shape, s.dtype) for s in srcs] + [pltpu.HBM(l.shape, l.dtype) for l in lands]
        + [jax.ShapeDtypeStruct((8, 128), F32)],
        in_specs=[_HBM] * (2 * n),
        out_specs=[_SEM, _SEM] + [_HBM] * (2 * n) + [pl.BlockSpec(memory_space=pltpu.VMEM)],
        input_output_aliases={i: 2 + i for i in range(2 * n)},
        compiler_params=pltpu.CompilerParams(has_side_effects=_EFFECT),
    )(*[pltpu.with_memory_space_constraint(s, pltpu.HBM) for s in srcs],
      *[pltpu.with_memory_space_constraint(l, pltpu.HBM) for l in lands])
    return out[0], out[1], out[2:2 + n], out[2 + n:2 + 2 * n], out[-1]


def _spread_wait(send_sems, recv_sems, srcs, lands, which, after, name):
    n = len(srcs)

    def body(*refs):
        land_refs = refs[n:2 * n]
        send_sems, recv_sems = refs[2 * n:2 * n + 2]
        x, y, c = _my_place()
        for a in which:
            whole = pltpu.make_async_remote_copy(
                src_ref=land_refs[a], dst_ref=land_refs[a],
                send_sem=send_sems.at[a], recv_sem=recv_sems.at[a],
                device_id=(x, y, c), device_id_type=MESH)
            whole.wait_send()
            whole.wait_recv()

    out = pl.pallas_call(
        body, name=name,
        out_shape=[pltpu.HBM(s.shape, s.dtype) for s in srcs] + [pltpu.HBM(l.shape, l.dtype) for l in lands],
        in_specs=[_HBM] * (2 * n) + [_SEM, _SEM, _ANY],
        out_specs=[_HBM] * (2 * n),
        input_output_aliases={i: i for i in range(2 * n)},
        compiler_params=pltpu.CompilerParams(has_side_effects=_EFFECT),
    )(*srcs, *lands, send_sems, recv_sems, after)
    return out[:n], out[n:]


def _sum_slots(parts, name):
    def body(p_ref, o_ref):
        acc = p_ref[0]
        for s in range(1, parts.shape[0]):
            acc = acc + p_ref[s]
        o_ref[...] = acc

    return pl.pallas_call(body, name=name, out_shape=jax.ShapeDtypeStruct(parts.shape[1:], parts.dtype))(parts)


def _pick(n, pref):
    if n <= pref:
        return n
    t = pref - pref % 128
    while t > 0 and n % t:
        t -= 128
    return t if t > 0 else n


_DIMS = {"nn": (((1,), (0,)), ((), ())),
         "nt": (((1,), (1,)), ((), ())),
         "tn": (((0,), (0,)), ((), ()))}


def _matmul(a, b, mode, out_dtype, tiles, name):
    (m, k) = a.shape
    n = b.shape[1] if mode == "nn" else b.shape[0]
    tm, tn, tk = (_pick(d, t) for d, t in zip((m, n, k), tiles))
    nk = k // tk

    def product(a_ref, b_ref):
        return lax.dot_general(a_ref[...].astype(BF), b_ref[...].astype(BF), _DIMS[mode], preferred_element_type=F32)

    def body_whole_k(a_ref, b_ref, o_ref):
        o_ref[...] = product(a_ref, b_ref).astype(o_ref.dtype)

    def body_split_k(a_ref, b_ref, o_ref, acc_ref):
        kk = pl.program_id(2)

        @pl.when(kk == 0)
        def _():
            acc_ref[...] = jnp.zeros_like(acc_ref)

        acc_ref[...] += product(a_ref, b_ref)

        @pl.when(kk == nk - 1)
        def _():
            o_ref[...] = acc_ref[...].astype(o_ref.dtype)

    b_spec = (pl.BlockSpec((tn, tk), lambda i, j, kk: (j, kk)) if mode == "nt"
              else pl.BlockSpec((tk, tn), lambda i, j, kk: (kk, j)))
    return pl.pallas_call(
        body_whole_k if nk == 1 else body_split_k, name=name,
        out_shape=jax.ShapeDtypeStruct((m, n), out_dtype),
        grid=(m // tm, n // tn, nk),
        in_specs=[pl.BlockSpec((tm, tk), lambda i, j, kk: (i, kk)), b_spec],
        out_specs=pl.BlockSpec((tm, tn), lambda i, j, kk: (i, j)),
        scratch_shapes=[] if nk == 1 else [pltpu.VMEM((tm, tn), F32)],
        compiler_params=pltpu.CompilerParams(
            dimension_semantics=("parallel", "parallel", "arbitrary"),
            vmem_limit_bytes=VMEM_LIMIT),
    )(a, b)


def _rows2d(t, lead):
    return t.reshape(t.shape[:lead] + (math.prod(t.shape[lead:-1]), t.shape[-1]))


def _pair_sum(g, theirs, name):
    g3, t3 = _rows2d(g, 2), _rows2d(theirs, 1)
    _, r, w = t3.shape
    tr = _pick(r, 512)

    def body(g_ref, t_ref, o_ref):
        c = lax.axis_index("c")
        mine = jnp.where(c == 0, g_ref[0, 0], g_ref[0, 1])
        o_ref[0] = (mine.astype(F32) + t_ref[0].astype(F32)).astype(o_ref.dtype)

    out = pl.pallas_call(
        body, name=name,
        out_shape=jax.ShapeDtypeStruct(t3.shape, t3.dtype),
        grid=(4, r // tr),
        in_specs=[pl.BlockSpec((1, 2, tr, w), lambda q, i: (q, 0, i, 0)),
                  pl.BlockSpec((1, tr, w), lambda q, i: (q, i, 0))],
        out_specs=pl.BlockSpec((1, tr, w), lambda q, i: (q, i, 0)),
        compiler_params=pltpu.CompilerParams(dimension_semantics=("parallel", "parallel")),
    )(g3, t3)
    return out.reshape(theirs.shape)


def _adam_update(w, g, m, v):
    c1 = 1.0 - ADAM_B1 ** ADAM_STEP
    c2 = 1.0 - ADAM_B2 ** ADAM_STEP
    nm = ADAM_B1 * m + (1.0 - ADAM_B1) * g
    nv = ADAM_B2 * v + (1.0 - ADAM_B2) * (g * g)
    delta = -ADAM_LR * ((nm / c1) / (jnp.sqrt(nv / c2) + ADAM_EPS) + ADAM_WD * w)
    return delta, nm, nv


def _sum_adamw(parts, w, m, v, layer, carry, after, name):
    shape = w.shape
    cols = shape[-1]
    p3 = _rows2d(parts, 1)
    w3, m3, v3 = (_rows2d(t, 1) for t in (w, m, v))
    rows = w3.shape[1]
    tr = _pick(rows, 128)
    n_parts = p3.shape[0]

    def body(p_ref, w_ref, m_ref, v_ref, *rest):
        g_ref, d_ref, nm_ref, nv_ref = rest[-4:]
        g = p_ref[0, :, :cols].astype(F32)
        for q in range(1, n_parts):
            g = g + p_ref[q, :, :cols].astype(F32)
        d, nm, nv = _adam_update(w_ref[0], g, m_ref[0], v_ref[0])
        g_ref[0] = g
        d_ref[0] = d
        nm_ref[0] = nm
        nv_ref[0] = nv

    spec = pl.BlockSpec((1, tr, cols), lambda i: (layer, i, 0))
    extra = [] if carry is None else [_rows2d(t, 1) for t in carry]
    tail = [] if after is None else [after]
    out = pl.pallas_call(
        body, name=name,
        out_shape=[jax.ShapeDtypeStruct(w3.shape, F32)] * 4,
        grid=(rows // tr,),
        in_specs=[pl.BlockSpec((n_parts, tr, p3.shape[-1]), lambda i: (0, i, 0)), spec, spec, spec] + [_ANY] * len(extra + tail),
        out_specs=[spec] * 4,
        input_output_aliases={4 + i: i for i in range(len(extra))},
        compiler_params=pltpu.CompilerParams(dimension_semantics=("parallel",)),
    )(p3, w3, m3, v3, *extra, *tail)
    return tuple(t.reshape(shape) for t in out)


def _sum_adamw_t(parts, w, m, v, name):
    rows = parts[0].shape[2]
    tr = 128
    assert rows % tr == 0 and rows >= w.shape[0]

    def body(*refs):
        p_refs, (w_ref, m_ref, v_ref), (g_ref, d_ref, nm_ref, nv_ref) = refs[:DEPTH], refs[DEPTH:DEPTH + 3], refs[DEPTH + 3:]
        for l in range(DEPTH):
            g = p_refs[l][0].astype(F32)
            for q in range(1, p_refs[l].shape[0]):
                g = g + p_refs[l][q].astype(F32)
            g = g.T
            d, nm, nv = _adam_update(w_ref[:, l, :], g, m_ref[:, l, :], v_ref[:, l, :])
            g_ref[:, l, :] = g
            d_ref[:, l, :] = d
            nm_ref[:, l, :] = nm
            nv_ref[:, l, :] = nv

    spec = pl.BlockSpec((tr,) + w.shape[1:], lambda i: (i, 0, 0))
    return pl.pallas_call(
        body, name=name,
        out_shape=[jax.ShapeDtypeStruct(w.shape, F32)] * 4,
        grid=(rows // tr,),
        in_specs=[pl.BlockSpec((p.shape[0], p.shape[1], tr), lambda i: (0, 0, i)) for p in parts] + [spec] * 3,
        out_specs=[spec] * 4,
        compiler_params=pltpu.CompilerParams(dimension_semantics=("parallel",)),
    )(*parts, w, m, v)


def _adamw(w, g, m, v, name):
    rows, cols = w.shape
    tr = _pick(rows, 128)

    def body(w_ref, g_ref, m_ref, v_ref, d_ref, nm_ref, nv_ref):
        d, nm, nv = _adam_update(w_ref[...], g_ref[...], m_ref[...], v_ref[...])
        d_ref[...] = d
        nm_ref[...] = nm
        nv_ref[...] = nv

    spec = pl.BlockSpec((tr, cols), lambda i: (i, 0))
    return pl.pallas_call(
        body, name=name,
        out_shape=[jax.ShapeDtypeStruct((rows, cols), F32)] * 3,
        grid=(rows // tr,),
        in_specs=[spec] * 4, out_specs=[spec] * 3,
        compiler_params=pltpu.CompilerParams(dimension_semantics=("parallel",)),
    )(w, g, m, v)


_VJP = {"nn": (("nt", "gb"), ("tn", "ag")),
        "nt": (("nn", "gb"), ("tn", "ga")),
        "tn": (("nt", "bg"), ("nn", "ag"))}


def _make_dot(cast, precision):
    def raw(mode, a, b):
        return lax.dot_general(cast(a), cast(b), _DIMS[mode], precision=precision,
                               preferred_element_type=F32)

    @functools.partial(jax.custom_vjp, nondiff_argnums=(0,))
    def dot(mode, a, b):
        return raw(mode, a, b)

    def fwd(mode, a, b):
        return raw(mode, a, b), (a, b)

    def bwd(mode, res, g):
        a, b = res
        pick = {"a": a, "b": b, "g": g}
        (ma, ta), (mb, tb) = _VJP[mode]
        return dot(ma, pick[ta[0]], pick[ta[1]]), dot(mb, pick[tb[0]], pick[tb[1]])

    dot.defvjp(fwd, bwd)
    return dot


bdot = _make_dot(lambda t: t.astype(BF), None)
hdot = _make_dot(lambda t: t, lax.Precision.HIGHEST)


def _xdot(mode, a, b):
    return lax.dot_general(a, b, _DIMS[mode], precision=lax.Precision.HIGH, preferred_element_type=F32)


def _unit_lower_inverse(Ls):
    n = Ls[0].shape[0]
    batched = (((2,), (1,)), ((0,), (0,)))
    mm = lambda a, b: lax.dot_general(a, b, batched, precision=lax.Precision.HIGH, preferred_element_type=F32)
    eye = (lax.broadcasted_iota(jnp.int32, (n, n), 0) == lax.broadcasted_iota(jnp.int32, (n, n), 1)).astype(F32)
    p = jnp.stack(Ls)
    t_inv = eye[None] - p
    for _ in range(6):
        p = mm(p, p)
        t_inv = t_inv + mm(t_inv, p)
    return [t_inv[h] for h in range(len(Ls))]


@jax.custom_vjp
def _tri_solve(L, rhs, t_inv):
    return _xdot("nn", t_inv, rhs)


def _tri_solve_fwd(L, rhs, t_inv):
    sol = _xdot("nn", t_inv, rhs)
    return sol, (t_inv, sol)


def _tri_solve_bwd(res, dsol):
    t_inv, sol = res
    drhs = _xdot("tn", t_inv, dsol)
    return -_xdot("nt", drhs, sol), drhs, jnp.zeros_like(t_inv)


_tri_solve.defvjp(_tri_solve_fwd, _tri_solve_bwd)


def _sigmoid(x):
    return 1.0 / (1.0 + jnp.exp(-x))


def _softplus(x):
    return jnp.maximum(x, 0.0) + jnp.log(1.0 + jnp.exp(-jnp.abs(x)))


def _dn_chunk(S, xs, ba, z, cw, al, dt, dn, t_saved=None):
    C = DN_C
    pre = xs[0] * cw[0] + xs[1] * cw[1] + xs[2] * cw[2] + xs[3] * cw[3]
    qkv = pre * _sigmoid(pre)
    lane = lax.broadcasted_iota(jnp.int32, (1, 128), 1)
    sub = lax.broadcasted_iota(jnp.int32, (C, 1), 0)
    row_i = lax.broadcasted_iota(jnp.int32, (C, C), 0)
    col_i = lax.broadcasted_iota(jnp.int32, (C, C), 1)
    strict = row_i > col_i
    incl = row_i >= col_i
    g_all = jnp.where((lane >= 4) & (lane < 8), -jnp.exp(al) * _softplus(ba + dt), 0.0)
    gc_all = hdot("nn", incl.astype(F32), g_all)
    gc_all_t = gc_all.T
    beta_all = _sigmoid(ba)
    glast_all = jnp.sum(jnp.where(sub == C - 1, gc_all, 0.0), axis=0, keepdims=True)
    heads = []
    for h in range(DN_HEADS):
        q = qkv[:, 128 * h:128 * (h + 1)]
        k = qkv[:, 512 + 128 * h:512 + 128 * (h + 1)]
        v = qkv[:, 1024 + 128 * h:1024 + 128 * (h + 1)]
        q = q * lax.rsqrt(jnp.sum(q * q, axis=1, keepdims=True) + EPS) * (DN_D ** -0.5)
        k = k * lax.rsqrt(jnp.sum(k * k, axis=1, keepdims=True) + EPS)
        beta = jnp.sum(jnp.where(lane == h, beta_all, 0.0), axis=1, keepdims=True)
        gc = jnp.sum(jnp.where(lane == 4 + h, gc_all, 0.0), axis=1, keepdims=True)
        gc_row = jnp.sum(jnp.where(sub == 4 + h, gc_all_t, 0.0), axis=0, keepdims=True)
        g_last = jnp.sum(jnp.where(lane == 4 + h, glast_all, 0.0), axis=1, keepdims=True)
        diff = gc - gc_row
        kb = k * beta
        L = jnp.where(strict, bdot("nt", kb, k) * jnp.exp(jnp.where(strict, diff, 0.0)), 0.0)
        heads.append((q, k, v, beta, gc, g_last, diff, kb, L))
    t_invs = _unit_lower_inverse([hd[-1] for hd in heads]) if t_saved is None else t_saved
    ys, s_new = [], []
    for h, (q, k, v, beta, gc, g_last, diff, kb, L) in enumerate(heads):
        sol = _tri_solve(L, jnp.concatenate([v * beta, kb * jnp.exp(gc)], axis=1), t_invs[h])
        u, w = sol[:, :DN_D], sol[:, DN_D:]
        a_qk = jnp.where(incl, bdot("nt", q, k) * jnp.exp(jnp.where(incl, diff, 0.0)), 0.0)
        qg = q * jnp.exp(gc)
        kd = k * jnp.exp(g_last - gc)
        v_new = u - bdot("nn", w, S[h])
        o = bdot("nn", qg, S[h]) + bdot("nn", a_qk, v_new)
        s_new.append(S[h] * jnp.exp(g_last) + bdot("tn", kd, v_new))
        o = o * lax.rsqrt(jnp.mean(o * o, axis=1, keepdims=True) + EPS) * dn
        zh = z[:, 128 * h:128 * (h + 1)]
        ys.append(o * (zh * _sigmoid(zh)))
    return jnp.concatenate(ys, axis=1), tuple(s_new), tuple(t_invs)


def _load_shifted(xbuf, x_ref, halo_ref, first):
    xbuf[0:HALO, :] = jnp.where(first, 0.0, halo_ref[:, 0:1536])
    xbuf[HALO:HALO + DN_C, :] = x_ref[:, 0:1536]
    return [xbuf[HALO - 3 + k:HALO - 3 + k + DN_C, :] for k in range(4)]


def dn_forward(cols, jblk, cw, al, dt, dn, name):
    T = cols.shape[0]
    n = T // DN_C

    def body(x_ref, halo_ref, cw_ref, al_ref, dt_ref, dn_ref, y_ref, ss_ref, ts_ref, s_scr, xbuf):
        i = pl.program_id(0)

        @pl.when(i == 0)
        def _():
            s_scr[...] = jnp.zeros_like(s_scr)

        xs = _load_shifted(xbuf, x_ref, halo_ref, i == 0)
        ss_ref[0] = s_scr[...]
        S = [s_scr[h] for h in range(DN_HEADS)]
        cws = [cw_ref[k:k + 1, :] for k in range(4)]
        y, s_new, t_invs = _dn_chunk(S, xs, x_ref[:, 2048:2176], x_ref[:, 1536:2048], cws,
                                     al_ref[...], dt_ref[...], dn_ref[...])
        y_ref[...] = y.astype(y_ref.dtype)
        for h in range(DN_HEADS):
            s_scr[h] = s_new[h]
            ts_ref[0, h] = t_invs[h]

    per = DN_C // HALO
    full = lambda shape: pl.BlockSpec(shape, lambda i: (0,) * len(shape))
    return pl.pallas_call(
        body, name=name,
        out_shape=[jax.ShapeDtypeStruct((T, 512), Y_DTYPE),
                   jax.ShapeDtypeStruct((n, DN_HEADS, DN_D, DN_D), F32),
                   jax.ShapeDtypeStruct((n, DN_HEADS, DN_D, DN_D), F32)],
        grid=(n,),
        in_specs=[pl.BlockSpec((DN_C, DN_W), lambda i: (i, jblk)),
                  pl.BlockSpec((HALO, DN_W), lambda i: (jnp.maximum(i * per - 1, 0), jblk)),
                  full((4, 1536)), full((1, 128)), full((1, 128)), full((1, 128))],
        out_specs=[pl.BlockSpec((DN_C, 512), lambda i: (i, 0)),
                   pl.BlockSpec((1, DN_HEADS, DN_D, DN_D), lambda i: (i, 0, 0, 0)),
                   pl.BlockSpec((1, DN_HEADS, DN_D, DN_D), lambda i: (i, 0, 0, 0))],
        scratch_shapes=[pltpu.VMEM((DN_HEADS, DN_D, DN_D), F32), pltpu.VMEM((HALO + DN_C, 1536), F32)],
        compiler_params=pltpu.CompilerParams(dimension_semantics=("arbitrary",)),
    )(cols, cols, cw, al, dt, dn)


def dn_backward(cols, jblk, cw, al, dt, dn, ss, ts, dy, dcols, name):
    T = cols.shape[0]
    n = T // DN_C

    def body(x_ref, halo_ref, cw_ref, al_ref, dt_ref, dn_ref, ss_ref, ts_ref, dy_ref, dcols_in,
             dx_ref, dcw_ref, dal_ref, ddt_ref, ddn_ref, ds_scr, xbuf, dbuf, carry):
        i = pl.program_id(0)

        @pl.when(i == 0)
        def _():
            ds_scr[...] = jnp.zeros_like(ds_scr)
            carry[...] = jnp.zeros_like(carry)
            dcw_ref[...] = jnp.zeros_like(dcw_ref)
            dal_ref[...] = jnp.zeros_like(dal_ref)
            ddt_ref[...] = jnp.zeros_like(ddt_ref)
            ddn_ref[...] = jnp.zeros_like(ddn_ref)

        xs = _load_shifted(xbuf, x_ref, halo_ref, i == n - 1)
        S = [ss_ref[0, h] for h in range(DN_HEADS)]
        cws = [cw_ref[k:k + 1, :] for k in range(4)]

        t_saved = [ts_ref[0, h] for h in range(DN_HEADS)]

        def f(S, xs, ba, z, cws, al, dt, dn):
            return _dn_chunk(S, xs, ba, z, cws, al, dt, dn, t_saved)[:2]

        _, vjp = jax.vjp(f, S, xs, x_ref[:, 2048:2176], x_ref[:, 1536:2048], cws, al_ref[...], dt_ref[...], dn_ref[...])
        dS, dxs, dba, dz, dcws, dal, ddt, ddn = vjp((dy_ref[...], tuple(ds_scr[h] for h in range(DN_HEADS))))
        for h in range(DN_HEADS):
            ds_scr[h] = dS[h]
        dbuf[...] = jnp.zeros_like(dbuf)
        for k in range(4):
            lo = HALO - 3 + k
            dbuf[lo:lo + DN_C, :] += dxs[k]
        dbuf[DN_C:DN_C + HALO, :] += carry[...]
        dx_ref[...] = jnp.concatenate([dbuf[HALO:HALO + DN_C, :], dz, dba,
                                       jnp.zeros((DN_C, DN_W - 2176), F32)], axis=1).astype(dx_ref.dtype)
        carry[...] = dbuf[0:HALO, :]
        for k in range(4):
            dcw_ref[k:k + 1, :] += dcws[k]
        dal_ref[...] += dal
        ddt_ref[...] += ddt
        ddn_ref[...] += ddn

    per = DN_C // HALO
    rev = lambda i: n - 1 - i
    full = lambda shape: pl.BlockSpec(shape, lambda i: (0,) * len(shape))
    return pl.pallas_call(
        body, name=name,
        out_shape=[jax.ShapeDtypeStruct(dcols.shape, dcols.dtype),jax.ShapeDtypeStruct((4, 1536), F32),
                   jax.ShapeDtypeStruct((1, 128), F32), jax.ShapeDtypeStruct((1, 128), F32),
                   jax.ShapeDtypeStruct((1, 128), F32)],
        grid=(n,),
        in_specs=[pl.BlockSpec((DN_C, DN_W), lambda i: (rev(i), jblk)),
                  pl.BlockSpec((HALO, DN_W), lambda i: (jnp.maximum(rev(i) * per - 1, 0), jblk)),
                  full((4, 1536)), full((1, 128)), full((1, 128)), full((1, 128)),
                  pl.BlockSpec((1, DN_HEADS, DN_D, DN_D), lambda i: (rev(i), 0, 0, 0)),
                  pl.BlockSpec((1, DN_HEADS, DN_D, DN_D), lambda i: (rev(i), 0, 0, 0)),
                  pl.BlockSpec((DN_C, 512), lambda i: (rev(i), 0)), _ANY],
        out_specs=[pl.BlockSpec((DN_C, DN_W), lambda i: (rev(i), jblk)),
                   full((4, 1536)), full((1, 128)), full((1, 128)), full((1, 128))],
        scratch_shapes=[pltpu.VMEM((DN_HEADS, DN_D, DN_D), F32), pltpu.VMEM((HALO + DN_C, 1536), F32),
                        pltpu.VMEM((HALO + DN_C, 1536), F32), pltpu.VMEM((HALO, 1536), F32)],
        input_output_aliases={9: 0},
        compiler_params=pltpu.CompilerParams(dimension_semantics=("arbitrary",)),
    )(cols, cols, cw, al, dt, dn, ss, ts, dy, dcols)


def _full(shape):
    return pl.BlockSpec(shape, lambda i: (0,) * len(shape))


def _silu(x):
    return x * _sigmoid(x)


def _gelu(x):
    return 0.5 * x * (1.0 + jnp.tanh(0.7978845608028654 * (x + 0.044715 * (x * x * x))))


def _lane_col(mat, idx):
    lane = lax.broadcasted_iota(jnp.int32, (1, mat.shape[1]), 1)
    return jnp.sum(jnp.where(lane == idx, mat, 0.0), axis=1, keepdims=True)


def _gm_chunk(uv, z, gain, ws, bt):
    g = _gelu(uv)
    u, v = g[:, :512], g[:, 512:]
    v = v * lax.rsqrt(jnp.mean(v * v, axis=1, keepdims=True) + EPS) * gain
    row_i = lax.broadcasted_iota(jnp.int32, (BLK, BLK), 0)
    col_i = lax.broadcasted_iota(jnp.int32, (BLK, BLK), 1)
    causal = row_i >= col_i
    ss = []
    for grp in range(4):
        wg = jnp.where(causal, ws[grp], 0.0)
        ss.append(bdot("nn", wg, v[:, BLK * grp:BLK * (grp + 1)]) + _lane_col(bt, grp))
    return u * jnp.concatenate(ss, axis=1) * _silu(z)


def gm_forward(cols, jblk, gain, ws, bt, name):
    T = cols.shape[0]

    def body(x_ref, gain_ref, ws_ref, bt_ref, y_ref):
        y_ref[...] = _gm_chunk(x_ref[:, 0:1024], x_ref[:, 1024:1536], gain_ref[...],
                               [ws_ref[g] for g in range(4)], bt_ref[...]).astype(y_ref.dtype)

    return pl.pallas_call(
        body, name=name, out_shape=jax.ShapeDtypeStruct((T, 512), Y_DTYPE), grid=(T // BLK,),
        in_specs=[pl.BlockSpec((BLK, GM_W), lambda i: (i, jblk)),
                  _full((1, 512)), _full((4, BLK, BLK)), _full((BLK, BLK))],
        out_specs=pl.BlockSpec((BLK, 512), lambda i: (i, 0)),
        compiler_params=pltpu.CompilerParams(dimension_semantics=("parallel",)),
    )(cols, gain, ws, bt)


def gm_backward(cols, jblk, gain, ws, bt, dy, dcols, name):
    T = cols.shape[0]

    def body(x_ref, gain_ref, ws_ref, bt_ref, dy_ref, dcols_in, dx_ref, dgain_ref, dws_ref, dbt_ref):
        @pl.when(pl.program_id(0) == 0)
        def _():
            dgain_ref[...] = jnp.zeros_like(dgain_ref)
            dws_ref[...] = jnp.zeros_like(dws_ref)
            dbt_ref[...] = jnp.zeros_like(dbt_ref)

        _, vjp = jax.vjp(_gm_chunk, x_ref[:, 0:1024], x_ref[:, 1024:1536], gain_ref[...],
                         [ws_ref[g] for g in range(4)], bt_ref[...])
        duv, dz, dgain, dws, dbt = vjp(dy_ref[...])
        dx_ref[...] = jnp.concatenate([duv, dz], axis=1).astype(dx_ref.dtype)
        dgain_ref[...] += dgain
        for g in range(4):
            dws_ref[g] += dws[g]
        dbt_ref[...] += dbt

    return pl.pallas_call(
        body, name=name,
        out_shape=[jax.ShapeDtypeStruct(dcols.shape, dcols.dtype),jax.ShapeDtypeStruct((1, 512), F32),
                   jax.ShapeDtypeStruct((4, BLK, BLK), F32), jax.ShapeDtypeStruct((BLK, BLK), F32)],
        grid=(T // BLK,),
        in_specs=[pl.BlockSpec((BLK, GM_W), lambda i: (i, jblk)),
                  _full((1, 512)), _full((4, BLK, BLK)), _full((BLK, BLK)),
                  pl.BlockSpec((BLK, 512), lambda i: (i, 0)), _ANY],
        out_specs=[pl.BlockSpec((BLK, GM_W), lambda i: (i, jblk)),
                   _full((1, 512)), _full((4, BLK, BLK)), _full((BLK, BLK))],
        input_output_aliases={5: 0},
        compiler_params=pltpu.CompilerParams(dimension_semantics=("arbitrary",)),
    )(cols, gain, ws, bt, dy, dcols)


def _sw_block(first, q, kp, kc, vp, vc, z, sinks):
    P = BLK
    lane = lax.broadcasted_iota(jnp.int32, (1, 128), 1)
    r = lax.broadcasted_iota(jnp.int32, (128, 128), 0)
    c = lax.broadcasted_iota(jnp.int32, (128, 128), 1)
    swap = (c == (r + 64) % 128).astype(F32)
    k2 = jnp.concatenate([kp, kc], axis=0)
    v2 = jnp.concatenate([vp, vc], axis=0)
    k2s = bdot("nn", k2, swap)
    v2s = bdot("nn", v2, swap)
    qi = lax.broadcasted_iota(jnp.int32, (P, 2 * P), 0)
    kj = lax.broadcasted_iota(jnp.int32, (P, 2 * P), 1)
    dist = qi + P - kj
    valid = (dist >= 0) & (dist < P) & ((kj >= P) | jnp.logical_not(first))
    outs = []
    for j in range(4):
        acc = jnp.zeros((P, 128), F32)
        for half in range(2):
            h = 2 * j + half
            kv = h // 4
            in_half = (lane >= 64 * half) & (lane < 64 * half + 64)
            qh = jnp.where(in_half, q[:, 128 * j:128 * (j + 1)], 0.0)
            same = (half == kv)
            s = bdot("nt", qh, k2 if same else k2s) * (64 ** -0.5)
            s = jnp.where(valid, s, NEG_INF)
            sink = _lane_col(sinks, h)
            m = lax.stop_gradient(jnp.maximum(jnp.max(s, axis=1, keepdims=True), sink))
            e = jnp.exp(s - m)
            p = e / (jnp.sum(e, axis=1, keepdims=True) + jnp.exp(sink - m))
            o = bdot("nn", p, v2 if same else v2s)
            acc = acc + jnp.where(in_half, o, 0.0)
        outs.append(acc)
    return jnp.concatenate(outs, axis=1) * _silu(z)


def _sw_specs(jblk, idx):
    prev = lambda i: jnp.maximum(idx(i) - 1, 0)
    jk = (jblk * SW_W + 1024) // 128
    return [pl.BlockSpec((BLK, SW_W), lambda i: (idx(i), jblk)),
            pl.BlockSpec((BLK, 128), lambda i: (prev(i), jk)),
            pl.BlockSpec((BLK, 128), lambda i: (prev(i), jk + 1)), _full((1, 128))]


def sw_forward(cols, jblk, sinks, name):
    T = cols.shape[0]

    def body(x_ref, kp_ref, vp_ref, s_ref, y_ref):
        y_ref[...] = _sw_block(pl.program_id(0) == 0, x_ref[:, 0:512], kp_ref[...], x_ref[:, 1024:1152],
                               vp_ref[...], x_ref[:, 1152:1280], x_ref[:, 512:1024], s_ref[...]).astype(y_ref.dtype)

    return pl.pallas_call(
        body, name=name, out_shape=jax.ShapeDtypeStruct((T, 512), Y_DTYPE), grid=(T // BLK,),
        in_specs=_sw_specs(jblk, lambda i: i),
        out_specs=pl.BlockSpec((BLK, 512), lambda i: (i, 0)),
        compiler_params=pltpu.CompilerParams(dimension_semantics=("parallel",)),
    )(cols, cols, cols, sinks)


def sw_backward(cols, jblk, sinks, dy, dcols, name):
    T = cols.shape[0]
    n = T // BLK
    rev = lambda i: n - 1 - i

    def body(x_ref, kp_ref, vp_ref, s_ref, dy_ref, dcols_in, dx_ref, ds_ref, kcarry, vcarry):
        i = pl.program_id(0)

        @pl.when(i == 0)
        def _():
            kcarry[...] = jnp.zeros_like(kcarry)
            vcarry[...] = jnp.zeros_like(vcarry)
            ds_ref[...] = jnp.zeros_like(ds_ref)

        f = functools.partial(_sw_block, i == n - 1)
        _, vjp = jax.vjp(f, x_ref[:, 0:512], kp_ref[...], x_ref[:, 1024:1152], vp_ref[...], x_ref[:, 1152:1280],
                         x_ref[:, 512:1024], s_ref[...])
        dq, dkp, dkc, dvp, dvc, dz, dsk = vjp(dy_ref[...])
        dx_ref[...] = jnp.concatenate([dq, dz, dkc + kcarry[...], dvc + vcarry[...],
                                       jnp.zeros((BLK, SW_W - 1280), F32)], axis=1).astype(dx_ref.dtype)
        kcarry[...] = dkp
        vcarry[...] = dvp
        ds_ref[...] += dsk

    return pl.pallas_call(
        body, name=name,
        out_shape=[jax.ShapeDtypeStruct(dcols.shape, dcols.dtype),jax.ShapeDtypeStruct((1, 128), F32)],
        grid=(n,),
        in_specs=_sw_specs(jblk, rev) + [pl.BlockSpec((BLK, 512), lambda i: (rev(i), 0)), _ANY],
        out_specs=[pl.BlockSpec((BLK, SW_W), lambda i: (rev(i), jblk)), _full((1, 128))],
        scratch_shapes=[pltpu.VMEM((BLK, 128), F32), pltpu.VMEM((BLK, 128), F32)],
        input_output_aliases={5: 0},
        compiler_params=pltpu.CompilerParams(dimension_semantics=("arbitrary",)),
    )(cols, cols, cols, sinks, dy, dcols)


XM_TQ = 512


def _xm_block(q, z, mkv):
    outs = []
    for h in range(4):
        s = bdot("nt", q[:, 128 * h:128 * (h + 1)], mkv[:, 128 * h:128 * (h + 1)]) * (128 ** -0.5)
        m = lax.stop_gradient(jnp.max(s, axis=1, keepdims=True))
        e = jnp.exp(s - m)
        p = e / jnp.sum(e, axis=1, keepdims=True)
        outs.append(bdot("nn", p, mkv[:, 512 + 128 * h:512 + 128 * (h + 1)]))
    return jnp.concatenate(outs, axis=1) * _silu(z)


def xm_forward(cols, jblk, mkv, name):
    T = cols.shape[0]

    def body(x_ref, m_ref, y_ref):
        y_ref[...] = _xm_block(x_ref[:, 0:512], x_ref[:, 512:1024], m_ref[...]).astype(y_ref.dtype)

    return pl.pallas_call(
        body, name=name, out_shape=jax.ShapeDtypeStruct((T, 512), Y_DTYPE), grid=(T // XM_TQ,),
        in_specs=[pl.BlockSpec((XM_TQ, XM_W), lambda i: (i, jblk)), _full(mkv.shape)],
        out_specs=pl.BlockSpec((XM_TQ, 512), lambda i: (i, 0)),
        compiler_params=pltpu.CompilerParams(dimension_semantics=("parallel",)),
    )(cols, mkv)


def xm_backward(cols, jblk, mkv, dy, dcols, name):
    T = cols.shape[0]

    def body(x_ref, m_ref, dy_ref, dcols_in, dx_ref, dm_ref):
        @pl.when(pl.program_id(0) == 0)
        def _():
            dm_ref[...] = jnp.zeros_like(dm_ref)

        _, vjp = jax.vjp(_xm_block, x_ref[:, 0:512], x_ref[:, 512:1024], m_ref[...])
        dq, dz, dm = vjp(dy_ref[...])
        dx_ref[...] = jnp.concatenate([dq, dz], axis=1).astype(dx_ref.dtype)
        dm_ref[...] += dm

    return pl.pallas_call(
        body, name=name,
        out_shape=[jax.ShapeDtypeStruct(dcols.shape, dcols.dtype),jax.ShapeDtypeStruct(mkv.shape, F32)],
        grid=(T // XM_TQ,),
        in_specs=[pl.BlockSpec((XM_TQ, XM_W), lambda i: (i, jblk)), _full(mkv.shape),
                  pl.BlockSpec((XM_TQ, 512), lambda i: (i, 0)), _ANY],
        out_specs=[pl.BlockSpec((XM_TQ, XM_W), lambda i: (i, jblk)), _full(mkv.shape)],
        input_output_aliases={3: 0},
        compiler_params=pltpu.CompilerParams(dimension_semantics=("arbitrary",)),
    )(cols, mkv, dy, dcols)


def _rms(x, gain):
    return x * lax.rsqrt(jnp.mean(x * x, axis=1, keepdims=True) + EPS) * gain


def memkv_forward(mem, gain, w, name):
    def body(m_ref, g_ref, w_ref, o_ref):
        o_ref[...] = bdot("nn", _rms(m_ref[...], g_ref[...]), w_ref[...])

    return pl.pallas_call(body, name=name, out_shape=jax.ShapeDtypeStruct(mem.shape, F32),
                          compiler_params=pltpu.CompilerParams(vmem_limit_bytes=VMEM_LIMIT))(mem, gain, w)


def memkv_backward(mem, gain, w, dkv, name):
    def body(m_ref, g_ref, w_ref, d_ref, dg_ref, dw_ref):
        mem_v = m_ref[...]
        _, vjp = jax.vjp(lambda g, ww: bdot("nn", _rms(mem_v, g), ww), g_ref[...], w_ref[...].astype(F32))
        dg, dw = vjp(d_ref[...])
        dg_ref[...] = dg
        dw_ref[...] = dw

    return pl.pallas_call(body, name=name,
                          out_shape=[jax.ShapeDtypeStruct(gain.shape, F32), jax.ShapeDtypeStruct(w.shape, F32)],
                          compiler_params=pltpu.CompilerParams(vmem_limit_bytes=VMEM_LIMIT))(mem, gain, w, dkv)


MG_TB = 256


def _merge_block(ys, gl, wup, wout, gpost):
    merged = None
    for n in range(4):
        t = _sigmoid(gl[:, 1024 * n:1024 * (n + 1)]) * bdot("nn", ys[n], wup[n])
        merged = t if merged is None else merged + t
    out = bdot("nn", merged, wout)
    return _rms(out, gpost)


def merge_forward(ys, cols, jgate, x, wup, wout, gpost, name, next_gain=None, target=None):
    T, D = x.shape
    TB = 256
    n_extra = (next_gain is not None) + (target is not None)

    def body(ya, yb, yc, ym, gl_ref, x_ref, wup_ref, wout_ref, gp_ref, *rest):
        extra, outs = rest[:n_extra], rest[n_extra:]
        upd = _merge_block([ya[...], yb[...], yc[...], ym[...]], gl_ref[...],
                           [wup_ref[n] for n in range(4)], wout_ref[...], gp_ref[...])
        y = x_ref[...] + upd
        outs[0][...] = y
        outs = outs[1:]
        if next_gain is not None:
            h = _rms(y, extra[0][...])
            outs[0][...] = h.astype(BF)
            outs[1][...] = h.T.astype(BF)
            outs = outs[2:]
        if target is not None:
            l_ref, d_ref = outs

            @pl.when(pl.program_id(0) == 0)
            def _():
                l_ref[...] = jnp.zeros_like(l_ref)

            err = y - extra[-1][...]
            d_ref[...] = err * (1.0 / D)
            l_ref[...] += jnp.full(l_ref.shape, 0.5 * jnp.sum(jnp.mean(err * err, axis=1, keepdims=True)), F32)

    yspec = pl.BlockSpec((TB, 512), lambda i: (i, 0))
    xspec = pl.BlockSpec((TB, D), lambda i: (i, 0))
    extra_in, extra_specs = [], []
    out_shape, out_specs = [jax.ShapeDtypeStruct((T, D), F32)], [xspec]
    if next_gain is not None:
        extra_in, extra_specs = extra_in + [next_gain], extra_specs + [_full((1, D))]
        out_shape += [jax.ShapeDtypeStruct((T, D), BF), jax.ShapeDtypeStruct((D, T), BF)]
        out_specs += [xspec, pl.BlockSpec((D, TB), lambda i: (0, i))]
    if target is not None:
        extra_in, extra_specs = extra_in + [target], extra_specs + [xspec]
        out_shape += [jax.ShapeDtypeStruct((1, 128), F32), jax.ShapeDtypeStruct((T, D), F32)]
        out_specs += [_full((1, 128)), xspec]
    return pl.pallas_call(
        body, name=name, out_shape=out_shape, grid=(T // TB,),
        in_specs=[yspec] * 4 + [pl.BlockSpec((TB, 4096), lambda i: (i, jgate)), xspec,
                                _full(wup.shape), _full(wout.shape), _full((1, D))] + extra_specs,
        out_specs=out_specs,
        compiler_params=pltpu.CompilerParams(
            dimension_semantics=("parallel" if target is None else "arbitrary",), vmem_limit_bytes=VMEM_LIMIT),
    )(*ys, cols, x, wup, wout, gpost, *extra_in)


def _token_product(a, b, name):
    (T, m), n = a.shape, b.shape[1]

    def body(a_ref, b_ref, o_ref):
        o_ref[...] = lax.dot_general(a_ref[...].astype(BF), b_ref[...].astype(BF), _DIMS["tn"], preferred_element_type=F32)

    return pl.pallas_call(body, name=name, out_shape=jax.ShapeDtypeStruct((m, n), F32),
                          compiler_params=pltpu.CompilerParams(vmem_limit_bytes=VMEM_LIMIT))(a, b)


def merge_backward(ys, cols, jgate, wup, wout, gpost, dx, name):
    T = dx.shape[0]
    TB = MG_TB

    def body(ya, yb, yc, ym, gl_ref, wup_ref, wout_ref, gp_ref, dx_ref,
             dgl_ref, dya, dyb, dyc, dym, dpa, dpb, dpc, dpm, merged_ref, dout_ref, dgp_ref):
        @pl.when(pl.program_id(0) == 0)
        def _():
            dgp_ref[...] = jnp.zeros_like(dgp_ref)

        y_refs = (ya, yb, yc, ym)
        gates = [_sigmoid(gl_ref[:, 1024 * n:1024 * (n + 1)]) for n in range(4)]
        projs = [bdot("nn", y_refs[n][...], wup_ref[n]) for n in range(4)]
        merged = gates[0] * projs[0] + gates[1] * projs[1] + gates[2] * projs[2] + gates[3] * projs[3]
        out = bdot("nn", merged, wout_ref[...])
        _, vjp = jax.vjp(_rms, out, gp_ref[...])
        dout, dgp = vjp(dx_ref[...])
        dmerged = bdot("nt", dout, wout_ref[...])
        for n, (dy_ref, dp_ref) in enumerate(zip((dya, dyb, dyc, dym), (dpa, dpb, dpc, dpm))):
            dproj = dmerged * gates[n]
            dgl_ref[:, 1024 * n:1024 * (n + 1)] = (dmerged * projs[n] * gates[n] * (1.0 - gates[n])).astype(dgl_ref.dtype)
            dy_ref[...] = bdot("nt", dproj, wup_ref[n])
            dp_ref[...] = dproj.astype(BF)
        merged_ref[...] = merged.astype(BF)
        dout_ref[...] = dout.astype(BF)
        dgp_ref[...] += dgp

    yspec = pl.BlockSpec((TB, 512), lambda i: (i, 0))
    dspec = pl.BlockSpec((TB, 1024), lambda i: (i, 0))
    dcols, dya, dyb, dyc, dym, *dproj, merged, dout, dgp = pl.pallas_call(
        body, name=name,
        out_shape=[jax.ShapeDtypeStruct(cols.shape, BF)] + [jax.ShapeDtypeStruct((T, 512), F32)] * 4 + [
            jax.ShapeDtypeStruct((T, 1024), BF)] * 6 + [jax.ShapeDtypeStruct((1, 1024), F32)],
        grid=(T // TB,),
        in_specs=[yspec] * 4 + [pl.BlockSpec((TB, 4096), lambda i: (i, jgate)),
                                _full(wup.shape), _full(wout.shape), _full((1, 1024)), dspec],
        out_specs=[pl.BlockSpec((TB, 4096), lambda i: (i, jgate))] + [yspec] * 4 + [
            dspec] * 6 + [_full((1, 1024))],
        compiler_params=pltpu.CompilerParams(dimension_semantics=("arbitrary",), vmem_limit_bytes=VMEM_LIMIT),
    )(*ys, cols, wup, wout, gpost, dx)
    dwup = jnp.stack([_token_product(ys[n], dproj[n], "%s_w_up%d" % (name, n)) for n in range(4)])
    dwout = _token_product(merged, dout, name + "_w_out")
    return dcols, dya, dyb, dyc, dym, dwup, dwout, dgp


NB = 256


def prenorm_forward(x, gain, name):
    T, D = x.shape

    def body(x_ref, g_ref, o_ref, ot_ref):
        h = _rms(x_ref[...], g_ref[...])
        o_ref[...] = h.astype(BF)
        ot_ref[...] = h.T.astype(BF)

    return pl.pallas_call(
        body, name=name,
        out_shape=[jax.ShapeDtypeStruct((T, D), BF), jax.ShapeDtypeStruct((D, T), BF)], grid=(T // NB,),
        in_specs=[pl.BlockSpec((NB, D), lambda i: (i, 0)), _full((1, D))],
        out_specs=[pl.BlockSpec((NB, D), lambda i: (i, 0)), pl.BlockSpec((D, NB), lambda i: (0, i))],
        compiler_params=pltpu.CompilerParams(dimension_semantics=("parallel",)),
    )(x, gain)


def prenorm_backward(x, gain, dh, dres, name):
    T = x.shape[0]

    def body(x_ref, g_ref, dh_ref, dr_ref, dx_ref, dg_ref):
        @pl.when(pl.program_id(0) == 0)
        def _():
            dg_ref[...] = jnp.zeros_like(dg_ref)

        _, vjp = jax.vjp(_rms, x_ref[...], g_ref[...])
        dxn, dg = vjp(dh_ref[...])
        dx_ref[...] = dr_ref[...] + dxn
        dg_ref[...] += dg

    spec = pl.BlockSpec((NB, 1024), lambda i: (i, 0))
    return pl.pallas_call(
        body, name=name,
        out_shape=[jax.ShapeDtypeStruct(x.shape, F32), jax.ShapeDtypeStruct((1, 1024), F32)], grid=(T // NB,),
        in_specs=[spec, _full((1, 1024)), spec, spec], out_specs=[spec, _full((1, 1024))],
        compiler_params=pltpu.CompilerParams(dimension_semantics=("arbitrary",)),
    )(x, gain, dh, dres)


JB_GATE, JB_XM, JB_DN, JB_SW, JB_GM = 0, 4, 2, 5, 6
_ALIGNED_PIECES = ((5896, 4096), (4872, 512), (5384, 512), (0, 2048), (2048, 8), 504, (3592, 512), (4360, 512),
                   (4104, 128), (4232, 128), 256, (2056, 1024), (3080, 512))
_NATURAL_FROM_ALIGNED = ((5120, 2048), (7168, 8), (9216, 1024), (10240, 512), (7680, 512), (8704, 128), (8832, 128),
                         (8192, 512), (4096, 512), (4608, 512), (0, 4096))


def _natural_range(slots, start, width):
    out = []
    while width > 0:
        j, i = divmod(start, W_IN_SHARD)
        take = min(width, W_IN_SHARD - i)
        out.append(slots[j, :, i:i + take])
        start, width = start + take, width - take
    return out


def _aligned_w_in(slots):
    parts = []
    for piece in _ALIGNED_PIECES:
        if isinstance(piece, int):
            parts.append(jnp.zeros(slots.shape[1:2] + (piece,), slots.dtype))
        else:
            parts += _natural_range(slots, *piece)
    return jnp.concatenate(parts, axis=-1)


def _slots_of_aligned(d_al, stacked=True):
    slots = []
    for s in range(N_DEV):
        lo, hi = s * W_IN_SHARD, (s + 1) * W_IN_SHARD
        parts, nat = [], 0
        for a_start, width in _NATURAL_FROM_ALIGNED:
            b, e = max(lo, nat), min(hi, nat + width)
            if b < e:
                parts.append(d_al[..., a_start + b - nat:a_start + e - nat])
            nat += width
        parts.append(jnp.zeros(d_al.shape[:1] + (W_IN_SHARD_PAD - W_IN_SHARD,), d_al.dtype))
        slots.append(jnp.concatenate(parts, axis=-1) if stacked else parts)
    if stacked:
        return jnp.stack(slots)
    return jnp.concatenate([p for parts in slots for p in parts], axis=-1)


SMALL_VEC_W = 1024


def _pack_small(parts):
    rows = []
    for p in parts:
        flat = p.reshape(-1).astype(F32)
        r = -(-flat.shape[0] // SMALL_VEC_W)
        rows.append(jnp.pad(flat, (0, r * SMALL_VEC_W - flat.shape[0])).reshape(r, SMALL_VEC_W))
    vec = jnp.concatenate(rows, axis=0)
    return jnp.pad(vec, ((0, -vec.shape[0] % 8), (0, 0)))


def _unpack_small(vec, shapes):
    out, off = [], 0
    for s in shapes:
        n = math.prod(s)
        r = -(-n // SMALL_VEC_W)
        out.append(vec[off:off + r].reshape(-1)[:n].reshape(s))
        off += r
    return out


def _lanes(vec, at):
    return jnp.zeros((1, 128), F32).at[0, at:at + vec.shape[0]].set(vec)


SMALL_NAMES = ("norm_pre", "norm_post", "norm_mem", "a_log", "dt_bias", "dn_norm", "gm_norm",
               "spatial_w", "spatial_b", "sinks")


def _other_weights(s_mem, s_up, s_out):
    return (s_mem.reshape(D_MODEL, 2 * BRANCH_W),
            jnp.transpose(s_up, (1, 2, 0, 3)).reshape(N_BRANCH, BRANCH_W, D_MODEL), s_out.reshape(D_MODEL, D_MODEL))


def _grad_slots(d_in_al, d_mem, d_up, d_out, stacked=True):
    return [None if d_in_al is None else _slots_of_aligned(d_in_al, stacked), d_mem.astype(BF).reshape(N_DEV, 128, 2 * BRANCH_W),
            jnp.transpose(d_up.astype(BF).reshape(N_BRANCH, BRANCH_W, N_DEV, 128), (2, 0, 1, 3)),
            d_out.astype(BF).reshape(N_DEV, 128, D_MODEL)]


def _layer_params(l, small, conv_full, token):
    return dict(
        gpre=small["norm_pre"][l][None] + token, gpost=small["norm_post"][l][None], gmem=small["norm_mem"][l][None],
        cw=conv_full[l], al=_lanes(small["a_log"][l], 4), dt=_lanes(small["dt_bias"][l], 4),
        dnn=small["dn_norm"][l][None], gain=small["gm_norm"][l][None], ws=small["spatial_w"][l],
        bt=jnp.zeros((128, 128), F32).at[:, :GM_GROUPS].set(small["spatial_b"][l].T),
        sinks=_lanes(small["sinks"][l], 0))


def _layer_forward(l, xl, hs, mem, p, w_in_al, other_weights, **tail):
    t = "l%d_" % l
    h, h_t = hs
    cols = _matmul(h, w_in_al, "nn", F32, (1024, 1536, 1024), t + "w_in")
    ya, ss, ts = dn_forward(cols, JB_DN, p["cw"], p["al"], p["dt"], p["dnn"], t + "deltanet")
    yb = gm_forward(cols, JB_GM, p["gain"], p["ws"], p["bt"], t + "gmlp")
    yc = sw_forward(cols, JB_SW, p["sinks"], t + "swa")
    w_mem, w_up, w_out = other_weights(yc)
    mkv = memkv_forward(mem, p["gmem"], w_mem, t + "memkv")
    ym = xm_forward(cols, JB_XM, mkv, t + "memattn")
    outs = merge_forward([ya, yb, yc, ym], cols, JB_GATE, xl, w_up, w_out, p["gpost"], t + "merge", **tail)
    return outs, dict(p, x=xl, h_t=h_t, cols=cols, mkv=mkv, ss=ss, ts=ts, ys=[ya, yb, yc, ym]), (w_in_al, w_mem, w_up, w_out)


def _layer_backward(l, s, mem, weights, dx, token, early=None):
    w_in_al, w_mem, w_up, w_out = weights
    t = "l%d_" % l
    cols = s["cols"]
    dcols, dya, dyb, dyc, dym, dwup, dwout, dgpost = merge_backward(
        s["ys"], cols, JB_GATE, w_up, w_out, s["gpost"] + token, dx, t + "merge_bwd")
    dcols, dmkv = xm_backward(cols, JB_XM, s["mkv"], dym, dcols, t + "memattn_bwd")
    dgmem, dwmem = memkv_backward(mem, s["gmem"], w_mem, dmkv, t + "memkv_bwd")
    sinks = s["sinks"] if early is None else s["sinks"] + early(dwmem, dwup, dwout)
    dcols, dsinks = sw_backward(cols, JB_SW, sinks, dyc, dcols, t + "swa_bwd")
    dcols, dgain, dws, dbt = gm_backward(cols, JB_GM, s["gain"], s["ws"], s["bt"], dyb, dcols, t + "gmlp_bwd")
    dcols, dcw, dal, ddt, ddn = dn_backward(
        cols, JB_DN, s["cw"], s["al"], s["dt"], s["dnn"], s["ss"], s["ts"], dya, dcols, t + "deltanet_bwd")
    dh = _matmul(dcols, w_in_al, "nt", F32, (1024, 1024, 3584), t + "w_in_bwd_x")
    dwin = _matmul(s["h_t"], dcols, "nn", BF, (1024, 1536, 2048), t + "w_in_bwd_w")
    dx, dgpre = prenorm_backward(s["x"], s["gpre"], dh, dx, t + "prenorm_bwd")
    gsmall = dict(norm_pre=dgpre[0], norm_post=dgpost[0], norm_mem=dgmem[0], a_log=dal[0, 4:8], dt_bias=ddt[0, 4:8],
                  dn_norm=ddn[0], gm_norm=dgain[0], spatial_w=dws, spatial_b=dbt[:, :GM_GROUPS].T,
                  sinks=dsinks[0, :SW_HEADS], conv_w=dcw)
    return dx, gsmall, (dwin, dwmem, dwup, dwout)


def kernel(x, mem, norm_pre, norm_post, norm_mem, w_in, conv_w, a_log, dt_bias, dn_norm, gm_norm, spatial_w, spatial_b, sinks, w_mem_kv, w_up, w_out, loss_target, m_norm_pre, m_norm_post, m_norm_mem, m_w_in, m_conv_w, m_a_log, m_dt_bias, m_dn_norm, m_gm_norm, m_spatial_w, m_spatial_b, m_sinks, m_w_mem_kv, m_w_up, m_w_out, v_norm_pre, v_norm_post, v_norm_mem, v_w_in, v_conv_w, v_a_log, v_dt_bias, v_dn_norm, v_gm_norm, v_spatial_w, v_spatial_b, v_sinks, v_w_mem_kv, v_w_up, v_w_out):
    xi, yi, ci = _my_place()
    my_slot = 4 * xi + 2 * yi + ci
    conv_shard = conv_w.shape[-1]
    x2, mem2, target = x[0], mem[0], loss_target[0]

    w_in_pad = jnp.pad(w_in.astype(BF), ((0, 0), (0, 0), (0, W_IN_SHARD_PAD - W_IN_SHARD)))
    shards = [[w_in_pad[l], w_mem_kv[l].astype(BF), w_up[l].astype(BF), w_out[l].astype(BF)] for l in range(DEPTH)]
    w_in_slots0, w_up_slots0, w_out_slots0, conv_slots = _all_gather_slots(
        [shards[0][0], shards[0][2], shards[0][3], conv_w], "gather_weights_l0")
    ag = list(_spread_start([shards[0][1]] + shards[1], "gather", "gather_weights_rest_start"))
    conv_full = jnp.transpose(conv_slots, (1, 2, 0, 3)).reshape(DEPTH, CONV_W, N_DEV * conv_shard)
    small = dict(norm_pre=norm_pre, norm_post=norm_post, norm_mem=norm_mem, a_log=a_log,
                 dt_bias=dt_bias, dn_norm=dn_norm, gm_norm=gm_norm, spatial_w=spatial_w,
                 spatial_b=spatial_b, sinks=sinks)

    def arrived(which, after, name):
        ag[2], ag[3] = _spread_wait(ag[0], ag[1], ag[2], ag[3], which, after, name)
        return [ag[3][a] for a in which]

    p0, p1 = _layer_params(0, small, conv_full, ag[4][0, 0]), _layer_params(1, small, conv_full, 0.0)
    (x1, h1, h1_t), saved0, weights0 = _layer_forward(
        0, x2, prenorm_forward(x2, p0["gpre"], "l0_prenorm"), mem2, p0, _aligned_w_in(w_in_slots0),
        lambda y: _other_weights(*arrived([0], y, "gather_weights_l0_w_mem_wait"), w_up_slots0, w_out_slots0),
        next_gain=p1["gpre"])
    w_in_slots1, = arrived([1], x1, "gather_weights_l1_w_in_wait")
    (x_out, loss, dx), saved1, weights1 = _layer_forward(
        1, x1, (h1, h1_t), mem2, p1, _aligned_w_in(w_in_slots1),
        lambda y: _other_weights(*arrived([2, 3, 4], y, "gather_weights_l1_rest_wait")), target=target)

    packed_names = SMALL_NAMES + ("conv_w",)
    dx, gsmall1, gbig1 = _layer_backward(1, saved1, mem2, weights1, dx, 0.0)
    small1 = [loss[0, :1]] + [gsmall1[n] for n in packed_names]
    sm1 = _spread_start([_pack_small(small1)], "gather", "gather_small_grads_l1_start")
    rs_send, rs_recv, rs_src, rs_land, rs_token = _spread_start(
        _grad_slots(*gbig1, stacked=False), "scatter", "exchange_grads_l1_start", slot_axis=[1, 0, 0, 0])
    rest0 = []

    def send_rest0(dwmem, dwup, dwout):
        rest0.extend(_spread_start(_grad_slots(None, dwmem, dwup, dwout)[1:], "scatter", "exchange_grads_l0_rest_start"))
        return rest0[4][0, 0]

    dx, gsmall0, gbig0 = _layer_backward(0, saved0, mem2, weights0, dx, rs_token[0, 0] + sm1[4][0, 0], send_rest0)
    _, parts1 = _spread_wait(rs_send, rs_recv, rs_src, rs_land, range(4), dx, "exchange_grads_l1_wait")

    small0 = [gsmall0[n] for n in packed_names]
    sm0 = _spread_start([_pack_small(small0)], "gather", "gather_small_grads_l0_start")

    g_win0 = _slots_of_aligned(gbig0[0])
    g_win0 = g_win0.reshape((N_DEV // 2, 2) + g_win0.shape[1:])
    theirs, = _exchange_sibling([g_win0], "exchange_sibling_l0")
    chip_sum = _pair_sum(g_win0, theirs, "pair_sum_l0")
    ch_send, ch_recv, ch_src, ch_land, ch_token = _spread_start([chip_sum], "chips", "exchange_chips_l0_start")

    _, (land1,) = _spread_wait(*sm1[:4], [0], ch_token, "gather_small_grads_l1_wait")
    _, (land0,) = _spread_wait(*sm0[:4], [0], land1, "gather_small_grads_l0_wait")
    tot1 = _unpack_small(_sum_slots(land1, "sum_small_grads_l1"), [p.shape for p in small1])
    tot0 = _unpack_small(_sum_slots(land0, "sum_small_grads_l0"), [p.shape for p in small0])
    loss_tot = tot1[0][0]
    grads = {n: jnp.stack([g0, g1]) for n, g0, g1 in zip(packed_names, tot0, tot1[1:])}
    grads["conv_w"] = lax.dynamic_slice_in_dim(grads["conv_w"], my_slot * conv_shard, conv_shard, axis=2)

    given = dict(norm_pre=(norm_pre, m_norm_pre, v_norm_pre), norm_post=(norm_post, m_norm_post, v_norm_post),
                 norm_mem=(norm_mem, m_norm_mem, v_norm_mem), a_log=(a_log, m_a_log, v_a_log),
                 dt_bias=(dt_bias, m_dt_bias, v_dt_bias), dn_norm=(dn_norm, m_dn_norm, v_dn_norm),
                 gm_norm=(gm_norm, m_gm_norm, v_gm_norm), spatial_w=(spatial_w, m_spatial_w, v_spatial_w),
                 spatial_b=(spatial_b, m_spatial_b, v_spatial_b), sinks=(sinks, m_sinks, v_sinks),
                 conv_w=(conv_w, m_conv_w, v_conv_w))
    pshapes = [given[n][0].shape for n in packed_names]
    pw, pm, pv = (_pack_small([given[n][i] for n in packed_names]) for i in range(3))
    pd, pnm, pnv = _adamw(pw + ch_token[0, 0], _pack_small([grads[n] for n in packed_names]), pm, pv, "adamw_small")
    upd = {n: t for n, t in zip(packed_names, zip(_unpack_small(pd, pshapes), _unpack_small(pnm, pshapes),
                                                  _unpack_small(pnv, pshapes)))}
    big = (("w_mem_kv", (w_mem_kv, m_w_mem_kv, v_w_mem_kv)), ("w_up", (w_up, m_w_up, v_w_up)),
           ("w_out", (w_out, m_w_out, v_w_out)))
    first = [_sum_adamw(parts1[1 + i], w, m, v, 1, None, ch_token, "adamw_%s_l1" % name)
             for i, (name, (w, m, v)) in enumerate(big)]
    _, parts0_rest = _spread_wait(*rest0[:4], range(3), first[-1][0], "exchange_grads_l0_rest_wait")
    for i, (name, (w, m, v)) in enumerate(big):
        g, d, nm, nv = _sum_adamw(parts0_rest[i], w, m, v, 0, first[i], None, "adamw_%s_l0" % name)
        grads[name], upd[name] = g, (d, nm, nv)
    _, (parts0_w_in,) = _spread_wait(ch_send, ch_recv, ch_src, ch_land, [0], upd["w_out"][0], "exchange_chips_l0_wait")
    w_in_t, m_w_in_t, v_w_in_t = (jnp.transpose(t, (2, 0, 1)) for t in (w_in, m_w_in, v_w_in))
    g, d, nm, nv = (jnp.transpose(t, (1, 2, 0)) for t in
                    _sum_adamw_t([parts0_w_in, parts1[0]], w_in_t, m_w_in_t, v_w_in_t, "adamw_w_in"))
    grads["w_in"], upd["w_in"] = g, (d, nm, nv)

    order = ("norm_pre", "norm_post", "norm_mem", "w_in", "conv_w", "a_log", "dt_bias", "dn_norm",
             "gm_norm", "spatial_w", "spatial_b", "sinks", "w_mem_kv", "w_up", "w_out")
    return (loss_tot, dx[None], *[grads[n] for n in order], *[upd[n][0] for n in order],
            *[upd[n][1] for n in order], *[upd[n][2] for n in order])
```

```python
import functools
import math

import jax
import jax.numpy as jnp
from jax import lax
from jax.experimental import pallas as pl
from jax.experimental.pallas import tpu as pltpu

MESH = pl.DeviceIdType.MESH
N_DEV = 8

D_MODEL = 1024
DEPTH = 2
N_BRANCH = 4
BRANCH_W = 512
DN_HEADS = 4
CONV_W = 4
GM_GROUPS = 4
SW_HEADS = 8
EPS = 1e-6
NEG_INF = -1e30

D_IN = 9992
W_IN_SHARD = D_IN // N_DEV
W_IN_SHARD_PAD = 1280
D_IN_AL = 10752
DN_W, SW_W, GM_W, XM_W = 2560, 1536, 1536, 1024

ADAM_LR = 0.001
ADAM_B1 = 0.9
ADAM_B2 = 0.999
ADAM_EPS = 1e-08
ADAM_WD = 0.01
ADAM_STEP = 10

VMEM_LIMIT = 56 * 1024 * 1024

BF = jnp.bfloat16
F32 = jnp.float32
DN_C = 128
DN_D = 128
HALO = 8
BLK = 128
Y_DTYPE = BF


def _my_place():
    return lax.axis_index("x"), lax.axis_index("y"), lax.axis_index("c")


_ANY = pl.BlockSpec(memory_space=pl.ANY)


def _all_gather_slots(parts, name):
    n = len(parts)

    def body(*refs):
        p_refs, out_refs = refs[:n], refs[n:2 * n]
        send_sems, recv_sems, local_sems = refs[2 * n:]
        x, y, c = _my_place()
        me, sibling = (x, y, c), (x, y, 1 - c)
        chips = [(1 - x, y), (x, 1 - y), (1 - x, 1 - y)]

        def copy(a, k, block, to, src=None):
            px, py, pc = block
            slot = out_refs[a].at[4 * px + 2 * py + pc]
            return pltpu.make_async_remote_copy(
                src_ref=slot if src is None else src, dst_ref=slot,
                send_sem=send_sems.at[7 * a + k], recv_sem=recv_sems.at[7 * a + k],
                device_id=to, device_id_type=MESH)

        mine = [pltpu.make_async_copy(p_refs[a], out_refs[a].at[4 * x + 2 * y + c], local_sems.at[a])
                for a in range(n)]
        for cp in mine:
            cp.start()
        first = []
        for a in range(n):
            first.append(copy(a, 0, me, sibling, src=p_refs[a]))
            first += [copy(a, 1 + j, me, (*chip, c), src=p_refs[a]) for j, chip in enumerate(chips)]
        for cp in first:
            cp.start()
        passed = []
        for j, chip in enumerate(chips):
            for a in range(n):
                copy(a, 1 + j, (*chip, c), me).wait_recv()
                fwd = copy(a, 4 + j, (*chip, c), sibling)
                fwd.start()
                passed.append(fwd)
        for a in range(n):
            copy(a, 0, sibling, me).wait_recv()
            for j, chip in enumerate(chips):
                copy(a, 4 + j, (*chip, 1 - c), me).wait_recv()
        for cp in first + passed:
            cp.wait_send()
        for cp in mine:
            cp.wait()

    return pl.pallas_call(
        body, name=name,
        out_shape=[jax.ShapeDtypeStruct((N_DEV,) + p.shape, p.dtype) for p in parts],
        in_specs=[_ANY] * n, out_specs=[_ANY] * n,
        scratch_shapes=[pltpu.SemaphoreType.DMA((7 * n,)), pltpu.SemaphoreType.DMA((7 * n,)),
                        pltpu.SemaphoreType.DMA((n,))],
    )(*parts)


def _exchange_sibling(parts, name):
    n = len(parts)

    def body(*refs):
        g_refs, out_refs = refs[:n], refs[n:2 * n]
        send_sems, recv_sems = refs[2 * n:]
        x, y, c = _my_place()
        copies = [pltpu.make_async_remote_copy(
            src_ref=g_refs[a].at[:, 1 - c], dst_ref=out_refs[a],
            send_sem=send_sems.at[a], recv_sem=recv_sems.at[a],
            device_id=(x, y, 1 - c), device_id_type=MESH) for a in range(n)]
        for cp in copies:
            cp.start()
        for cp in copies:
            cp.wait()

    return pl.pallas_call(
        body, name=name,
        out_shape=[jax.ShapeDtypeStruct((4,) + g.shape[2:], g.dtype) for g in parts],
        in_specs=[_ANY] * n, out_specs=[_ANY] * n,
        scratch_shapes=[pltpu.SemaphoreType.DMA((n,)), pltpu.SemaphoreType.DMA((n,))],
    )(*parts)


_HBM = pl.BlockSpec(memory_space=pltpu.HBM)
_SEM = pl.BlockSpec(memory_space=pltpu.SEMAPHORE)
_EFFECT = pltpu.SideEffectType.DATAFLOW_SIDE_EFFECTING


def _peer(x, y, c, k):
    return (1 - x if (k >> 2) & 1 else x, 1 - y if (k >> 1) & 1 else y, 1 - c if k & 1 else c)


def _spread_start(srcs, mode, name):
    n = len(srcs)
    lands = [lax.empty((N_DEV,) + s.shape if mode == "gather" else s.shape, s.dtype) for s in srcs]
    peers = range(0, N_DEV, 2) if mode == "chips" else range(N_DEV)

    def body(*refs):
        src_refs, land_refs = refs[:n], refs[n:2 * n]
        send_sems, recv_sems = refs[2 * n:2 * n + 2]
        token = refs[-1]
        x, y, c = _my_place()
        for a in range(n):
            for k in peers:
                px, py, pc = _peer(x, y, c, k)
                if mode == "chips":
                    src, mine = src_refs[a].at[2 * px + py], 2 * x + y
                else:
                    src = src_refs[a].at[4 * px + 2 * py + pc] if mode == "scatter" else src_refs[a]
                    mine = 4 * x + 2 * y + c
                pltpu.make_async_remote_copy(
                    src_ref=src, dst_ref=land_refs[a].at[mine],
                    send_sem=send_sems.at[a], recv_sem=recv_sems.at[a],
                    device_id=(px, py, pc), device_id_type=MESH).start()
        token[...] = jnp.zeros_like(token)

    out = pl.pallas_call(
        body, name=name,
        out_shape=[pltpu.SemaphoreType.DMA((n,)), pltpu.SemaphoreType.DMA((n,))]
        + [pltpu.HBM(s.shape, s.dtype) for s in srcs] + [pltpu.HBM(l.shape, l.dtype) for l in lands]
        + [jax.ShapeDtypeStruct((8, 128), F32)],
        in_specs=[_HBM] * (2 * n),
        out_specs=[_SEM, _SEM] + [_HBM] * (2 * n) + [pl.BlockSpec(memory_space=pltpu.VMEM)],
        input_output_aliases={i: 2 + i for i in range(2 * n)},
        compiler_params=pltpu.CompilerParams(has_side_effects=_EFFECT),
    )(*[pltpu.with_memory_space_constraint(s, pltpu.HBM) for s in srcs],
      *[pltpu.with_memory_space_constraint(l, pltpu.HBM) for l in lands])
    return out[0], out[1], out[2:2 + n], out[2 + n:2 + 2 * n], out[-1]


def _spread_wait(send_sems, recv_sems, srcs, lands, which, after, name):
    n = len(srcs)

    def body(*refs):
        land_refs = refs[n:2 * n]
        send_sems, recv_sems = refs[2 * n:2 * n + 2]
        x, y, c = _my_place()
        for a in which:
            whole = pltpu.make_async_remote_copy(
                src_ref=land_refs[a], dst_ref=land_refs[a],
                send_sem=send_sems.at[a], recv_sem=recv_sems.at[a],
                device_id=(x, y, c), device_id_type=MESH)
            whole.wait_send()
            whole.wait_recv()

    out = pl.pallas_call(
        body, name=name,
        out_shape=[pltpu.HBM(s.shape, s.dtype) for s in srcs] + [pltpu.HBM(l.shape, l.dtype) for l in lands],
        in_specs=[_HBM] * (2 * n) + [_SEM, _SEM, _ANY],
        out_specs=[_HBM] * (2 * n),
        input_output_aliases={i: i for i in range(2 * n)},
        compiler_params=pltpu.CompilerParams(has_side_effects=_EFFECT),
    )(*srcs, *lands, send_sems, recv_sems, after)
    return out[:n], out[n:]


def _sum_slots(parts, name):
    def body(p_ref, o_ref):
        acc = p_ref[0]
        for s in range(1, parts.shape[0]):
            acc = acc + p_ref[s]
        o_ref[...] = acc

    return pl.pallas_call(body, name=name, out_shape=jax.ShapeDtypeStruct(parts.shape[1:], parts.dtype))(parts)


def _pick(n, pref):
    if n <= pref:
        return n
    t = pref - pref % 128
    while t > 0 and n % t:
        t -= 128
    return t if t > 0 else n


_DIMS = {"nn": (((1,), (0,)), ((), ())),
         "nt": (((1,), (1,)), ((), ())),
         "tn": (((0,), (0,)), ((), ()))}


def _matmul(a, b, mode, out_dtype, tiles, name):
    (m, k) = a.shape
    n = b.shape[1] if mode == "nn" else b.shape[0]
    tm, tn, tk = (_pick(d, t) for d, t in zip((m, n, k), tiles))
    nk = k // tk

    def product(a_ref, b_ref):
        return lax.dot_general(a_ref[...].astype(BF), b_ref[...].astype(BF), _DIMS[mode], preferred_element_type=F32)

    def body_whole_k(a_ref, b_ref, o_ref):
        o_ref[...] = product(a_ref, b_ref).astype(o_ref.dtype)

    def body_split_k(a_ref, b_ref, o_ref, acc_ref):
        kk = pl.program_id(2)

        @pl.when(kk == 0)
        def _():
            acc_ref[...] = jnp.zeros_like(acc_ref)

        acc_ref[...] += product(a_ref, b_ref)

        @pl.when(kk == nk - 1)
        def _():
            o_ref[...] = acc_ref[...].astype(o_ref.dtype)

    b_spec = (pl.BlockSpec((tn, tk), lambda i, j, kk: (j, kk)) if mode == "nt"
              else pl.BlockSpec((tk, tn), lambda i, j, kk: (kk, j)))
    return pl.pallas_call(
        body_whole_k if nk == 1 else body_split_k, name=name,
        out_shape=jax.ShapeDtypeStruct((m, n), out_dtype),
        grid=(m // tm, n // tn, nk),
        in_specs=[pl.BlockSpec((tm, tk), lambda i, j, kk: (i, kk)), b_spec],
        out_specs=pl.BlockSpec((tm, tn), lambda i, j, kk: (i, j)),
        scratch_shapes=[] if nk == 1 else [pltpu.VMEM((tm, tn), F32)],
        compiler_params=pltpu.CompilerParams(
            dimension_semantics=("parallel", "parallel", "arbitrary"),
            vmem_limit_bytes=VMEM_LIMIT),
    )(a, b)


def _rows2d(t, lead):
    return t.reshape(t.shape[:lead] + (math.prod(t.shape[lead:-1]), t.shape[-1]))


def _pair_sum(g, theirs, name):
    g3, t3 = _rows2d(g, 2), _rows2d(theirs, 1)
    _, r, w = t3.shape
    tr = _pick(r, 512)

    def body(g_ref, t_ref, o_ref):
        c = lax.axis_index("c")
        mine = jnp.where(c == 0, g_ref[0, 0], g_ref[0, 1])
        o_ref[0] = (mine.astype(F32) + t_ref[0].astype(F32)).astype(o_ref.dtype)

    out = pl.pallas_call(
        body, name=name,
        out_shape=jax.ShapeDtypeStruct(t3.shape, t3.dtype),
        grid=(4, r // tr),
        in_specs=[pl.BlockSpec((1, 2, tr, w), lambda q, i: (q, 0, i, 0)),
                  pl.BlockSpec((1, tr, w), lambda q, i: (q, i, 0))],
        out_specs=pl.BlockSpec((1, tr, w), lambda q, i: (q, i, 0)),
        compiler_params=pltpu.CompilerParams(dimension_semantics=("parallel", "parallel")),
    )(g3, t3)
    return out.reshape(theirs.shape)


def _adam_update(w, g, m, v):
    c1 = 1.0 - ADAM_B1 ** ADAM_STEP
    c2 = 1.0 - ADAM_B2 ** ADAM_STEP
    nm = ADAM_B1 * m + (1.0 - ADAM_B1) * g
    nv = ADAM_B2 * v + (1.0 - ADAM_B2) * (g * g)
    delta = -ADAM_LR * ((nm / c1) / (jnp.sqrt(nv / c2) + ADAM_EPS) + ADAM_WD * w)
    return delta, nm, nv


def _sum_adamw(parts, w, m, v, layer, carry, after, name):
    shape = w.shape
    cols = shape[-1]
    p3 = _rows2d(parts, 1)
    w3, m3, v3 = (_rows2d(t, 1) for t in (w, m, v))
    rows = w3.shape[1]
    tr = _pick(rows, 128)
    n_parts = p3.shape[0]

    def body(p_ref, w_ref, m_ref, v_ref, *rest):
        g_ref, d_ref, nm_ref, nv_ref = rest[-4:]
        g = p_ref[0, :, :cols].astype(F32)
        for q in range(1, n_parts):
            g = g + p_ref[q, :, :cols].astype(F32)
        d, nm, nv = _adam_update(w_ref[0], g, m_ref[0], v_ref[0])
        g_ref[0] = g
        d_ref[0] = d
        nm_ref[0] = nm
        nv_ref[0] = nv

    spec = pl.BlockSpec((1, tr, cols), lambda i: (layer, i, 0))
    extra = [] if carry is None else [_rows2d(t, 1) for t in carry]
    tail = [] if after is None else [after]
    out = pl.pallas_call(
        body, name=name,
        out_shape=[jax.ShapeDtypeStruct(w3.shape, F32)] * 4,
        grid=(rows // tr,),
        in_specs=[pl.BlockSpec((n_parts, tr, p3.shape[-1]), lambda i: (0, i, 0)), spec, spec, spec] + [_ANY] * len(extra + tail),
        out_specs=[spec] * 4,
        input_output_aliases={4 + i: i for i in range(len(extra))},
        compiler_params=pltpu.CompilerParams(dimension_semantics=("parallel",)),
    )(p3, w3, m3, v3, *extra, *tail)
    return tuple(t.reshape(shape) for t in out)


def _sum_adamw_t(parts, w, m, v, name):
    rows = parts[0].shape[2]
    tr = 128
    assert rows % tr == 0 and rows >= w.shape[0]

    def body(*refs):
        p_refs, (w_ref, m_ref, v_ref), (g_ref, d_ref, nm_ref, nv_ref) = refs[:DEPTH], refs[DEPTH:DEPTH + 3], refs[DEPTH + 3:]
        for l in range(DEPTH):
            g = p_refs[l][0].astype(F32)
            for q in range(1, p_refs[l].shape[0]):
                g = g + p_refs[l][q].astype(F32)
            g = g.T
            d, nm, nv = _adam_update(w_ref[:, l, :], g, m_ref[:, l, :], v_ref[:, l, :])
            g_ref[:, l, :] = g
            d_ref[:, l, :] = d
            nm_ref[:, l, :] = nm
            nv_ref[:, l, :] = nv

    spec = pl.BlockSpec((tr,) + w.shape[1:], lambda i: (i, 0, 0))
    return pl.pallas_call(
        body, name=name,
        out_shape=[jax.ShapeDtypeStruct(w.shape, F32)] * 4,
        grid=(rows // tr,),
        in_specs=[pl.BlockSpec((p.shape[0], p.shape[1], tr), lambda i: (0, 0, i)) for p in parts] + [spec] * 3,
        out_specs=[spec] * 4,
        compiler_params=pltpu.CompilerParams(dimension_semantics=("parallel",)),
    )(*parts, w, m, v)


def _adamw(w, g, m, v, name):
    rows, cols = w.shape
    tr = _pick(rows, 128)

    def body(w_ref, g_ref, m_ref, v_ref, d_ref, nm_ref, nv_ref):
        d, nm, nv = _adam_update(w_ref[...], g_ref[...], m_ref[...], v_ref[...])
        d_ref[...] = d
        nm_ref[...] = nm
        nv_ref[...] = nv

    spec = pl.BlockSpec((tr, cols), lambda i: (i, 0))
    return pl.pallas_call(
        body, name=name,
        out_shape=[jax.ShapeDtypeStruct((rows, cols), F32)] * 3,
        grid=(rows // tr,),
        in_specs=[spec] * 4, out_specs=[spec] * 3,
        compiler_params=pltpu.CompilerParams(dimension_semantics=("parallel",)),
    )(w, g, m, v)


_VJP = {"nn": (("nt", "gb"), ("tn", "ag")),
        "nt": (("nn", "gb"), ("tn", "ga")),
        "tn": (("nt", "bg"), ("nn", "ag"))}


def _make_dot(cast, precision):
    def raw(mode, a, b):
        return lax.dot_general(cast(a), cast(b), _DIMS[mode], precision=precision,
                               preferred_element_type=F32)

    @functools.partial(jax.custom_vjp, nondiff_argnums=(0,))
    def dot(mode, a, b):
        return raw(mode, a, b)

    def fwd(mode, a, b):
        return raw(mode, a, b), (a, b)

    def bwd(mode, res, g):
        a, b = res
        pick = {"a": a, "b": b, "g": g}
        (ma, ta), (mb, tb) = _VJP[mode]
        return dot(ma, pick[ta[0]], pick[ta[1]]), dot(mb, pick[tb[0]], pick[tb[1]])

    dot.defvjp(fwd, bwd)
    return dot


bdot = _make_dot(lambda t: t.astype(BF), None)
hdot = _make_dot(lambda t: t, lax.Precision.HIGHEST)


def _xdot(mode, a, b):
    return lax.dot_general(a, b, _DIMS[mode], precision=lax.Precision.HIGH, preferred_element_type=F32)


def _unit_lower_inverse(Ls):
    n = Ls[0].shape[0]
    batched = (((2,), (1,)), ((0,), (0,)))
    mm = lambda a, b: lax.dot_general(a, b, batched, precision=lax.Precision.HIGH, preferred_element_type=F32)
    eye = (lax.broadcasted_iota(jnp.int32, (n, n), 0) == lax.broadcasted_iota(jnp.int32, (n, n), 1)).astype(F32)
    p = jnp.stack(Ls)
    t_inv = eye[None] - p
    for _ in range(6):
        p = mm(p, p)
        t_inv = t_inv + mm(t_inv, p)
    return [t_inv[h] for h in range(len(Ls))]


@jax.custom_vjp
def _tri_solve(L, rhs, t_inv):
    return _xdot("nn", t_inv, rhs)


def _tri_solve_fwd(L, rhs, t_inv):
    sol = _xdot("nn", t_inv, rhs)
    return sol, (t_inv, sol)


def _tri_solve_bwd(res, dsol):
    t_inv, sol = res
    drhs = _xdot("tn", t_inv, dsol)
    return -_xdot("nt", drhs, sol), drhs, jnp.zeros_like(t_inv)


_tri_solve.defvjp(_tri_solve_fwd, _tri_solve_bwd)


def _sigmoid(x):
    return 1.0 / (1.0 + jnp.exp(-x))


def _softplus(x):
    return jnp.maximum(x, 0.0) + jnp.log(1.0 + jnp.exp(-jnp.abs(x)))


def _dn_chunk(S, xs, ba, z, cw, al, dt, dn, t_saved=None):
    C = DN_C
    pre = xs[0] * cw[0] + xs[1] * cw[1] + xs[2] * cw[2] + xs[3] * cw[3]
    qkv = pre * _sigmoid(pre)
    lane = lax.broadcasted_iota(jnp.int32, (1, 128), 1)
    sub = lax.broadcasted_iota(jnp.int32, (C, 1), 0)
    row_i = lax.broadcasted_iota(jnp.int32, (C, C), 0)
    col_i = lax.broadcasted_iota(jnp.int32, (C, C), 1)
    strict = row_i > col_i
    incl = row_i >= col_i
    g_all = jnp.where((lane >= 4) & (lane < 8), -jnp.exp(al) * _softplus(ba + dt), 0.0)
    gc_all = hdot("nn", incl.astype(F32), g_all)
    gc_all_t = gc_all.T
    beta_all = _sigmoid(ba)
    glast_all = jnp.sum(jnp.where(sub == C - 1, gc_all, 0.0), axis=0, keepdims=True)
    heads = []
    for h in range(DN_HEADS):
        q = qkv[:, 128 * h:128 * (h + 1)]
        k = qkv[:, 512 + 128 * h:512 + 128 * (h + 1)]
        v = qkv[:, 1024 + 128 * h:1024 + 128 * (h + 1)]
        q = q * lax.rsqrt(jnp.sum(q * q, axis=1, keepdims=True) + EPS) * (DN_D ** -0.5)
        k = k * lax.rsqrt(jnp.sum(k * k, axis=1, keepdims=True) + EPS)
        beta = jnp.sum(jnp.where(lane == h, beta_all, 0.0), axis=1, keepdims=True)
        gc = jnp.sum(jnp.where(lane == 4 + h, gc_all, 0.0), axis=1, keepdims=True)
        gc_row = jnp.sum(jnp.where(sub == 4 + h, gc_all_t, 0.0), axis=0, keepdims=True)
        g_last = jnp.sum(jnp.where(lane == 4 + h, glast_all, 0.0), axis=1, keepdims=True)
        diff = gc - gc_row
        kb = k * beta
        L = jnp.where(strict, bdot("nt", kb, k) * jnp.exp(jnp.where(strict, diff, 0.0)), 0.0)
        heads.append((q, k, v, beta, gc, g_last, diff, kb, L))
    t_invs = _unit_lower_inverse([hd[-1] for hd in heads]) if t_saved is None else t_saved
    ys, s_new = [], []
    for h, (q, k, v, beta, gc, g_last, diff, kb, L) in enumerate(heads):
        sol = _tri_solve(L, jnp.concatenate([v * beta, kb * jnp.exp(gc)], axis=1), t_invs[h])
        u, w = sol[:, :DN_D], sol[:, DN_D:]
        a_qk = jnp.where(incl, bdot("nt", q, k) * jnp.exp(jnp.where(incl, diff, 0.0)), 0.0)
        qg = q * jnp.exp(gc)
        kd = k * jnp.exp(g_last - gc)
        v_new = u - bdot("nn", w, S[h])
        o = bdot("nn", qg, S[h]) + bdot("nn", a_qk, v_new)
        s_new.append(S[h] * jnp.exp(g_last) + bdot("tn", kd, v_new))
        o = o * lax.rsqrt(jnp.mean(o * o, axis=1, keepdims=True) + EPS) * dn
        zh = z[:, 128 * h:128 * (h + 1)]
        ys.append(o * (zh * _sigmoid(zh)))
    return jnp.concatenate(ys, axis=1), tuple(s_new), tuple(t_invs)


def _load_shifted(xbuf, x_ref, halo_ref, first):
    xbuf[0:HALO, :] = jnp.where(first, 0.0, halo_ref[:, 0:1536])
    xbuf[HALO:HALO + DN_C, :] = x_ref[:, 0:1536]
    return [xbuf[HALO - 3 + k:HALO - 3 + k + DN_C, :] for k in range(4)]


def dn_forward(cols, jblk, cw, al, dt, dn, name):
    T = cols.shape[0]
    n = T // DN_C

    def body(x_ref, halo_ref, cw_ref, al_ref, dt_ref, dn_ref, y_ref, ss_ref, ts_ref, s_scr, xbuf):
        i = pl.program_id(0)

        @pl.when(i == 0)
        def _():
            s_scr[...] = jnp.zeros_like(s_scr)

        xs = _load_shifted(xbuf, x_ref, halo_ref, i == 0)
        ss_ref[0] = s_scr[...]
        S = [s_scr[h] for h in range(DN_HEADS)]
        cws = [cw_ref[k:k + 1, :] for k in range(4)]
        y, s_new, t_invs = _dn_chunk(S, xs, x_ref[:, 2048:2176], x_ref[:, 1536:2048], cws,
                                     al_ref[...], dt_ref[...], dn_ref[...])
        y_ref[...] = y.astype(y_ref.dtype)
        for h in range(DN_HEADS):
            s_scr[h] = s_new[h]
            ts_ref[0, h] = t_invs[h]

    per = DN_C // HALO
    full = lambda shape: pl.BlockSpec(shape, lambda i: (0,) * len(shape))
    return pl.pallas_call(
        body, name=name,
        out_shape=[jax.ShapeDtypeStruct((T, 512), Y_DTYPE),
                   jax.ShapeDtypeStruct((n, DN_HEADS, DN_D, DN_D), F32),
                   jax.ShapeDtypeStruct((n, DN_HEADS, DN_D, DN_D), F32)],
        grid=(n,),
        in_specs=[pl.BlockSpec((DN_C, DN_W), lambda i: (i, jblk)),
                  pl.BlockSpec((HALO, DN_W), lambda i: (jnp.maximum(i * per - 1, 0), jblk)),
                  full((4, 1536)), full((1, 128)), full((1, 128)), full((1, 128))],
        out_specs=[pl.BlockSpec((DN_C, 512), lambda i: (i, 0)),
                   pl.BlockSpec((1, DN_HEADS, DN_D, DN_D), lambda i: (i, 0, 0, 0)),
                   pl.BlockSpec((1, DN_HEADS, DN_D, DN_D), lambda i: (i, 0, 0, 0))],
        scratch_shapes=[pltpu.VMEM((DN_HEADS, DN_D, DN_D), F32), pltpu.VMEM((HALO + DN_C, 1536), F32)],
        compiler_params=pltpu.CompilerParams(dimension_semantics=("arbitrary",)),
    )(cols, cols, cw, al, dt, dn)


def dn_backward(cols, jblk, cw, al, dt, dn, ss, ts, dy, dcols, name):
    T = cols.shape[0]
    n = T // DN_C

    def body(x_ref, halo_ref, cw_ref, al_ref, dt_ref, dn_ref, ss_ref, ts_ref, dy_ref, dcols_in,
             dx_ref, dcw_ref, dal_ref, ddt_ref, ddn_ref, ds_scr, xbuf, dbuf, carry):
        i = pl.program_id(0)

        @pl.when(i == 0)
        def _():
            ds_scr[...] = jnp.zeros_like(ds_scr)
            carry[...] = jnp.zeros_like(carry)
            dcw_ref[...] = jnp.zeros_like(dcw_ref)
            dal_ref[...] = jnp.zeros_like(dal_ref)
            ddt_ref[...] = jnp.zeros_like(ddt_ref)
            ddn_ref[...] = jnp.zeros_like(ddn_ref)

        xs = _load_shifted(xbuf, x_ref, halo_ref, i == n - 1)
        S = [ss_ref[0, h] for h in range(DN_HEADS)]
        cws = [cw_ref[k:k + 1, :] for k in range(4)]

        t_saved = [ts_ref[0, h] for h in range(DN_HEADS)]

        def f(S, xs, ba, z, cws, al, dt, dn):
            return _dn_chunk(S, xs, ba, z, cws, al, dt, dn, t_saved)[:2]

        _, vjp = jax.vjp(f, S, xs, x_ref[:, 2048:2176], x_ref[:, 1536:2048], cws, al_ref[...], dt_ref[...], dn_ref[...])
        dS, dxs, dba, dz, dcws, dal, ddt, ddn = vjp((dy_ref[...], tuple(ds_scr[h] for h in range(DN_HEADS))))
        for h in range(DN_HEADS):
            ds_scr[h] = dS[h]
        dbuf[...] = jnp.zeros_like(dbuf)
        for k in range(4):
            lo = HALO - 3 + k
            dbuf[lo:lo + DN_C, :] += dxs[k]
        dbuf[DN_C:DN_C + HALO, :] += carry[...]
        dx_ref[...] = jnp.concatenate([dbuf[HALO:HALO + DN_C, :], dz, dba,
                                       jnp.zeros((DN_C, DN_W - 2176), F32)], axis=1).astype(dx_ref.dtype)
        carry[...] = dbuf[0:HALO, :]
        for k in range(4):
            dcw_ref[k:k + 1, :] += dcws[k]
        dal_ref[...] += dal
        ddt_ref[...] += ddt
        ddn_ref[...] += ddn

    per = DN_C // HALO
    rev = lambda i: n - 1 - i
    full = lambda shape: pl.BlockSpec(shape, lambda i: (0,) * len(shape))
    return pl.pallas_call(
        body, name=name,
        out_shape=[jax.ShapeDtypeStruct(dcols.shape, dcols.dtype),jax.ShapeDtypeStruct((4, 1536), F32),
                   jax.ShapeDtypeStruct((1, 128), F32), jax.ShapeDtypeStruct((1, 128), F32),
                   jax.ShapeDtypeStruct((1, 128), F32)],
        grid=(n,),
        in_specs=[pl.BlockSpec((DN_C, DN_W), lambda i: (rev(i), jblk)),
                  pl.BlockSpec((HALO, DN_W), lambda i: (jnp.maximum(rev(i) * per - 1, 0), jblk)),
                  full((4, 1536)), full((1, 128)), full((1, 128)), full((1, 128)),
                  pl.BlockSpec((1, DN_HEADS, DN_D, DN_D), lambda i: (rev(i), 0, 0, 0)),
                  pl.BlockSpec((1, DN_HEADS, DN_D, DN_D), lambda i: (rev(i), 0, 0, 0)),
                  pl.BlockSpec((DN_C, 512), lambda i: (rev(i), 0)), _ANY],
        out_specs=[pl.BlockSpec((DN_C, DN_W), lambda i: (rev(i), jblk)),
                   full((4, 1536)), full((1, 128)), full((1, 128)), full((1, 128))],
        scratch_shapes=[pltpu.VMEM((DN_HEADS, DN_D, DN_D), F32), pltpu.VMEM((HALO + DN_C, 1536), F32),
                        pltpu.VMEM((HALO + DN_C, 1536), F32), pltpu.VMEM((HALO, 1536), F32)],
        input_output_aliases={9: 0},
        compiler_params=pltpu.CompilerParams(dimension_semantics=("arbitrary",)),
    )(cols, cols, cw, al, dt, dn, ss, ts, dy, dcols)


def _full(shape):
    return pl.BlockSpec(shape, lambda i: (0,) * len(shape))


def _silu(x):
    return x * _sigmoid(x)


def _gelu(x):
    return 0.5 * x * (1.0 + jnp.tanh(0.7978845608028654 * (x + 0.044715 * (x * x * x))))


def _lane_col(mat, idx):
    lane = lax.broadcasted_iota(jnp.int32, (1, mat.shape[1]), 1)
    return jnp.sum(jnp.where(lane == idx, mat, 0.0), axis=1, keepdims=True)


def _gm_chunk(uv, z, gain, ws, bt):
    g = _gelu(uv)
    u, v = g[:, :512], g[:, 512:]
    v = v * lax.rsqrt(jnp.mean(v * v, axis=1, keepdims=True) + EPS) * gain
    row_i = lax.broadcasted_iota(jnp.int32, (BLK, BLK), 0)
    col_i = lax.broadcasted_iota(jnp.int32, (BLK, BLK), 1)
    causal = row_i >= col_i
    ss = []
    for grp in range(4):
        wg = jnp.where(causal, ws[grp], 0.0)
        ss.append(bdot("nn", wg, v[:, BLK * grp:BLK * (grp + 1)]) + _lane_col(bt, grp))
    return u * jnp.concatenate(ss, axis=1) * _silu(z)


def gm_forward(cols, jblk, gain, ws, bt, name):
    T = cols.shape[0]

    def body(x_ref, gain_ref, ws_ref, bt_ref, y_ref):
        y_ref[...] = _gm_chunk(x_ref[:, 0:1024], x_ref[:, 1024:1536], gain_ref[...],
                               [ws_ref[g] for g in range(4)], bt_ref[...]).astype(y_ref.dtype)

    return pl.pallas_call(
        body, name=name, out_shape=jax.ShapeDtypeStruct((T, 512), Y_DTYPE), grid=(T // BLK,),
        in_specs=[pl.BlockSpec((BLK, GM_W), lambda i: (i, jblk)),
                  _full((1, 512)), _full((4, BLK, BLK)), _full((BLK, BLK))],
        out_specs=pl.BlockSpec((BLK, 512), lambda i: (i, 0)),
        compiler_params=pltpu.CompilerParams(dimension_semantics=("parallel",)),
    )(cols, gain, ws, bt)


def gm_backward(cols, jblk, gain, ws, bt, dy, dcols, name):
    T = cols.shape[0]

    def body(x_ref, gain_ref, ws_ref, bt_ref, dy_ref, dcols_in, dx_ref, dgain_ref, dws_ref, dbt_ref):
        @pl.when(pl.program_id(0) == 0)
        def _():
            dgain_ref[...] = jnp.zeros_like(dgain_ref)
            dws_ref[...] = jnp.zeros_like(dws_ref)
            dbt_ref[...] = jnp.zeros_like(dbt_ref)

        _, vjp = jax.vjp(_gm_chunk, x_ref[:, 0:1024], x_ref[:, 1024:1536], gain_ref[...],
                         [ws_ref[g] for g in range(4)], bt_ref[...])
        duv, dz, dgain, dws, dbt = vjp(dy_ref[...])
        dx_ref[...] = jnp.concatenate([duv, dz], axis=1).astype(dx_ref.dtype)
        dgain_ref[...] += dgain
        for g in range(4):
            dws_ref[g] += dws[g]
        dbt_ref[...] += dbt

    return pl.pallas_call(
        body, name=name,
        out_shape=[jax.ShapeDtypeStruct(dcols.shape, dcols.dtype),jax.ShapeDtypeStruct((1, 512), F32),
                   jax.ShapeDtypeStruct((4, BLK, BLK), F32), jax.ShapeDtypeStruct((BLK, BLK), F32)],
        grid=(T // BLK,),
        in_specs=[pl.BlockSpec((BLK, GM_W), lambda i: (i, jblk)),
                  _full((1, 512)), _full((4, BLK, BLK)), _full((BLK, BLK)),
                  pl.BlockSpec((BLK, 512), lambda i: (i, 0)), _ANY],
        out_specs=[pl.BlockSpec((BLK, GM_W), lambda i: (i, jblk)),
                   _full((1, 512)), _full((4, BLK, BLK)), _full((BLK, BLK))],
        input_output_aliases={5: 0},
        compiler_params=pltpu.CompilerParams(dimension_semantics=("arbitrary",)),
    )(cols, gain, ws, bt, dy, dcols)


def _sw_block(first, q, kp, kc, vp, vc, z, sinks):
    P = BLK
    lane = lax.broadcasted_iota(jnp.int32, (1, 128), 1)
    r = lax.broadcasted_iota(jnp.int32, (128, 128), 0)
    c = lax.broadcasted_iota(jnp.int32, (128, 128), 1)
    swap = (c == (r + 64) % 128).astype(F32)
    k2 = jnp.concatenate([kp, kc], axis=0)
    v2 = jnp.concatenate([vp, vc], axis=0)
    k2s = bdot("nn", k2, swap)
    v2s = bdot("nn", v2, swap)
    qi = lax.broadcasted_iota(jnp.int32, (P, 2 * P), 0)
    kj = lax.broadcasted_iota(jnp.int32, (P, 2 * P), 1)
    dist = qi + P - kj
    valid = (dist >= 0) & (dist < P) & ((kj >= P) | jnp.logical_not(first))
    outs = []
    for j in range(4):
        acc = jnp.zeros((P, 128), F32)
        for half in range(2):
            h = 2 * j + half
            kv = h // 4
            in_half = (lane >= 64 * half) & (lane < 64 * half + 64)
            qh = jnp.where(in_half, q[:, 128 * j:128 * (j + 1)], 0.0)
            same = (half == kv)
            s = bdot("nt", qh, k2 if same else k2s) * (64 ** -0.5)
            s = jnp.where(valid, s, NEG_INF)
            sink = _lane_col(sinks, h)
            m = lax.stop_gradient(jnp.maximum(jnp.max(s, axis=1, keepdims=True), sink))
            e = jnp.exp(s - m)
            p = e / (jnp.sum(e, axis=1, keepdims=True) + jnp.exp(sink - m))
            o = bdot("nn", p, v2 if same else v2s)
            acc = acc + jnp.where(in_half, o, 0.0)
        outs.append(acc)
    return jnp.concatenate(outs, axis=1) * _silu(z)


def _sw_specs(jblk, idx):
    prev = lambda i: jnp.maximum(idx(i) - 1, 0)
    jk = (jblk * SW_W + 1024) // 128
    return [pl.BlockSpec((BLK, SW_W), lambda i: (idx(i), jblk)),
            pl.BlockSpec((BLK, 128), lambda i: (prev(i), jk)),
            pl.BlockSpec((BLK, 128), lambda i: (prev(i), jk + 1)), _full((1, 128))]


def sw_forward(cols, jblk, sinks, name):
    T = cols.shape[0]

    def body(x_ref, kp_ref, vp_ref, s_ref, y_ref):
        y_ref[...] = _sw_block(pl.program_id(0) == 0, x_ref[:, 0:512], kp_ref[...], x_ref[:, 1024:1152],
                               vp_ref[...], x_ref[:, 1152:1280], x_ref[:, 512:1024], s_ref[...]).astype(y_ref.dtype)

    return pl.pallas_call(
        body, name=name, out_shape=jax.ShapeDtypeStruct((T, 512), Y_DTYPE), grid=(T // BLK,),
        in_specs=_sw_specs(jblk, lambda i: i),
        out_specs=pl.BlockSpec((BLK, 512), lambda i: (i, 0)),
        compiler_params=pltpu.CompilerParams(dimension_semantics=("parallel",)),
    )(cols, cols, cols, sinks)


def sw_backward(cols, jblk, sinks, dy, dcols, name):
    T = cols.shape[0]
    n = T // BLK
    rev = lambda i: n - 1 - i

    def body(x_ref, kp_ref, vp_ref, s_ref, dy_ref, dcols_in, dx_ref, ds_ref, kcarry, vcarry):
        i = pl.program_id(0)

        @pl.when(i == 0)
        def _():
            kcarry[...] = jnp.zeros_like(kcarry)
            vcarry[...] = jnp.zeros_like(vcarry)
            ds_ref[...] = jnp.zeros_like(ds_ref)

        f = functools.partial(_sw_block, i == n - 1)
        _, vjp = jax.vjp(f, x_ref[:, 0:512], kp_ref[...], x_ref[:, 1024:1152], vp_ref[...], x_ref[:, 1152:1280],
                         x_ref[:, 512:1024], s_ref[...])
        dq, dkp, dkc, dvp, dvc, dz, dsk = vjp(dy_ref[...])
        dx_ref[...] = jnp.concatenate([dq, dz, dkc + kcarry[...], dvc + vcarry[...],
                                       jnp.zeros((BLK, SW_W - 1280), F32)], axis=1).astype(dx_ref.dtype)
        kcarry[...] = dkp
        vcarry[...] = dvp
        ds_ref[...] += dsk

    return pl.pallas_call(
        body, name=name,
        out_shape=[jax.ShapeDtypeStruct(dcols.shape, dcols.dtype),jax.ShapeDtypeStruct((1, 128), F32)],
        grid=(n,),
        in_specs=_sw_specs(jblk, rev) + [pl.BlockSpec((BLK, 512), lambda i: (rev(i), 0)), _ANY],
        out_specs=[pl.BlockSpec((BLK, SW_W), lambda i: (rev(i), jblk)), _full((1, 128))],
        scratch_shapes=[pltpu.VMEM((BLK, 128), F32), pltpu.VMEM((BLK, 128), F32)],
        input_output_aliases={5: 0},
        compiler_params=pltpu.CompilerParams(dimension_semantics=("arbitrary",)),
    )(cols, cols, cols, sinks, dy, dcols)


XM_TQ = 512


def _xm_block(q, z, mkv):
    outs = []
    for h in range(4):
        s = bdot("nt", q[:, 128 * h:128 * (h + 1)], mkv[:, 128 * h:128 * (h + 1)]) * (128 ** -0.5)
        m = lax.stop_gradient(jnp.max(s, axis=1, keepdims=True))
        e = jnp.exp(s - m)
        p = e / jnp.sum(e, axis=1, keepdims=True)
        outs.append(bdot("nn", p, mkv[:, 512 + 128 * h:512 + 128 * (h + 1)]))
    return jnp.concatenate(outs, axis=1) * _silu(z)


def xm_forward(cols, jblk, mkv, name):
    T = cols.shape[0]

    def body(x_ref, m_ref, y_ref):
        y_ref[...] = _xm_block(x_ref[:, 0:512], x_ref[:, 512:1024], m_ref[...]).astype(y_ref.dtype)

    return pl.pallas_call(
        body, name=name, out_shape=jax.ShapeDtypeStruct((T, 512), Y_DTYPE), grid=(T // XM_TQ,),
        in_specs=[pl.BlockSpec((XM_TQ, XM_W), lambda i: (i, jblk)), _full(mkv.shape)],
        out_specs=pl.BlockSpec((XM_TQ, 512), lambda i: (i, 0)),
        compiler_params=pltpu.CompilerParams(dimension_semantics=("parallel",)),
    )(cols, mkv)


def xm_backward(cols, jblk, mkv, dy, dcols, name):
    T = cols.shape[0]

    def body(x_ref, m_ref, dy_ref, dcols_in, dx_ref, dm_ref):
        @pl.when(pl.program_id(0) == 0)
        def _():
            dm_ref[...] = jnp.zeros_like(dm_ref)

        _, vjp = jax.vjp(_xm_block, x_ref[:, 0:512], x_ref[:, 512:1024], m_ref[...])
        dq, dz, dm = vjp(dy_ref[...])
        dx_ref[...] = jnp.concatenate([dq, dz], axis=1).astype(dx_ref.dtype)
        dm_ref[...] += dm

    return pl.pallas_call(
        body, name=name,
        out_shape=[jax.ShapeDtypeStruct(dcols.shape, dcols.dtype),jax.ShapeDtypeStruct(mkv.shape, F32)],
        grid=(T // XM_TQ,),
        in_specs=[pl.BlockSpec((XM_TQ, XM_W), lambda i: (i, jblk)), _full(mkv.shape),
                  pl.BlockSpec((XM_TQ, 512), lambda i: (i, 0)), _ANY],
        out_specs=[pl.BlockSpec((XM_TQ, XM_W), lambda i: (i, jblk)), _full(mkv.shape)],
        input_output_aliases={3: 0},
        compiler_params=pltpu.CompilerParams(dimension_semantics=("arbitrary",)),
    )(cols, mkv, dy, dcols)


def _rms(x, gain):
    return x * lax.rsqrt(jnp.mean(x * x, axis=1, keepdims=True) + EPS) * gain


def memkv_forward(mem, gain, w, name):
    def body(m_ref, g_ref, w_ref, o_ref):
        o_ref[...] = bdot("nn", _rms(m_ref[...], g_ref[...]), w_ref[...])

    return pl.pallas_call(body, name=name, out_shape=jax.ShapeDtypeStruct(mem.shape, F32),
                          compiler_params=pltpu.CompilerParams(vmem_limit_bytes=VMEM_LIMIT))(mem, gain, w)


def memkv_backward(mem, gain, w, dkv, name):
    def body(m_ref, g_ref, w_ref, d_ref, dg_ref, dw_ref):
        mem_v = m_ref[...]
        _, vjp = jax.vjp(lambda g, ww: bdot("nn", _rms(mem_v, g), ww), g_ref[...], w_ref[...].astype(F32))
        dg, dw = vjp(d_ref[...])
        dg_ref[...] = dg
        dw_ref[...] = dw

    return pl.pallas_call(body, name=name,
                          out_shape=[jax.ShapeDtypeStruct(gain.shape, F32), jax.ShapeDtypeStruct(w.shape, F32)],
                          compiler_params=pltpu.CompilerParams(vmem_limit_bytes=VMEM_LIMIT))(mem, gain, w, dkv)


MG_TB = 256


def _merge_block(ys, gl, wup, wout, gpost):
    merged = None
    for n in range(4):
        t = _sigmoid(gl[:, 1024 * n:1024 * (n + 1)]) * bdot("nn", ys[n], wup[n])
        merged = t if merged is None else merged + t
    out = bdot("nn", merged, wout)
    return _rms(out, gpost)


def merge_forward(ys, cols, jgate, x, wup, wout, gpost, name, next_gain=None, target=None):
    T, D = x.shape
    TB = 256
    n_extra = (next_gain is not None) + (target is not None)

    def body(ya, yb, yc, ym, gl_ref, x_ref, wup_ref, wout_ref, gp_ref, *rest):
        extra, outs = rest[:n_extra], rest[n_extra:]
        upd = _merge_block([ya[...], yb[...], yc[...], ym[...]], gl_ref[...],
                           [wup_ref[n] for n in range(4)], wout_ref[...], gp_ref[...])
        y = x_ref[...] + upd
        outs[0][...] = y
        outs = outs[1:]
        if next_gain is not None:
            h = _rms(y, extra[0][...])
            outs[0][...] = h.astype(BF)
            outs[1][...] = h.T.astype(BF)
            outs = outs[2:]
        if target is not None:
            l_ref, d_ref = outs

            @pl.when(pl.program_id(0) == 0)
            def _():
                l_ref[...] = jnp.zeros_like(l_ref)

            err = y - extra[-1][...]
            d_ref[...] = err * (1.0 / D)
            l_ref[...] += jnp.full(l_ref.shape, 0.5 * jnp.sum(jnp.mean(err * err, axis=1, keepdims=True)), F32)

    yspec = pl.BlockSpec((TB, 512), lambda i: (i, 0))
    xspec = pl.BlockSpec((TB, D), lambda i: (i, 0))
    extra_in, extra_specs = [], []
    out_shape, out_specs = [jax.ShapeDtypeStruct((T, D), F32)], [xspec]
    if next_gain is not None:
        extra_in, extra_specs = extra_in + [next_gain], extra_specs + [_full((1, D))]
        out_shape += [jax.ShapeDtypeStruct((T, D), BF), jax.ShapeDtypeStruct((D, T), BF)]
        out_specs += [xspec, pl.BlockSpec((D, TB), lambda i: (0, i))]
    if target is not None:
        extra_in, extra_specs = extra_in + [target], extra_specs + [xspec]
        out_shape += [jax.ShapeDtypeStruct((1, 128), F32), jax.ShapeDtypeStruct((T, D), F32)]
        out_specs += [_full((1, 128)), xspec]
    return pl.pallas_call(
        body, name=name, out_shape=out_shape, grid=(T // TB,),
        in_specs=[yspec] * 4 + [pl.BlockSpec((TB, 4096), lambda i: (i, jgate)), xspec,
                                _full(wup.shape), _full(wout.shape), _full((1, D))] + extra_specs,
        out_specs=out_specs,
        compiler_params=pltpu.CompilerParams(
            dimension_semantics=("parallel" if target is None else "arbitrary",), vmem_limit_bytes=VMEM_LIMIT),
    )(*ys, cols, x, wup, wout, gpost, *extra_in)


def _token_product(a, b, name):
    (T, m), n = a.shape, b.shape[1]

    def body(a_ref, b_ref, o_ref):
        o_ref[...] = lax.dot_general(a_ref[...].astype(BF), b_ref[...].astype(BF), _DIMS["tn"], preferred_element_type=F32)

    return pl.pallas_call(body, name=name, out_shape=jax.ShapeDtypeStruct((m, n), F32),
                          compiler_params=pltpu.CompilerParams(vmem_limit_bytes=VMEM_LIMIT))(a, b)


def merge_backward(ys, cols, jgate, wup, wout, gpost, dx, name):
    T = dx.shape[0]
    TB = MG_TB

    def body(ya, yb, yc, ym, gl_ref, wup_ref, wout_ref, gp_ref, dx_ref,
             dgl_ref, dya, dyb, dyc, dym, dpa, dpb, dpc, dpm, merged_ref, dout_ref, dgp_ref):
        @pl.when(pl.program_id(0) == 0)
        def _():
            dgp_ref[...] = jnp.zeros_like(dgp_ref)

        y_refs = (ya, yb, yc, ym)
        gates = [_sigmoid(gl_ref[:, 1024 * n:1024 * (n + 1)]) for n in range(4)]
        projs = [bdot("nn", y_refs[n][...], wup_ref[n]) for n in range(4)]
        merged = gates[0] * projs[0] + gates[1] * projs[1] + gates[2] * projs[2] + gates[3] * projs[3]
        out = bdot("nn", merged, wout_ref[...])
        _, vjp = jax.vjp(_rms, out, gp_ref[...])
        dout, dgp = vjp(dx_ref[...])
        dmerged = bdot("nt", dout, wout_ref[...])
        for n, (dy_ref, dp_ref) in enumerate(zip((dya, dyb, dyc, dym), (dpa, dpb, dpc, dpm))):
            dproj = dmerged * gates[n]
            dgl_ref[:, 1024 * n:1024 * (n + 1)] = (dmerged * projs[n] * gates[n] * (1.0 - gates[n])).astype(dgl_ref.dtype)
            dy_ref[...] = bdot("nt", dproj, wup_ref[n])
            dp_ref[...] = dproj.astype(BF)
        merged_ref[...] = merged.astype(BF)
        dout_ref[...] = dout.astype(BF)
        dgp_ref[...] += dgp

    yspec = pl.BlockSpec((TB, 512), lambda i: (i, 0))
    dspec = pl.BlockSpec((TB, 1024), lambda i: (i, 0))
    dcols, dya, dyb, dyc, dym, *dproj, merged, dout, dgp = pl.pallas_call(
        body, name=name,
        out_shape=[jax.ShapeDtypeStruct(cols.shape, BF)] + [jax.ShapeDtypeStruct((T, 512), F32)] * 4 + [
            jax.ShapeDtypeStruct((T, 1024), BF)] * 6 + [jax.ShapeDtypeStruct((1, 1024), F32)],
        grid=(T // TB,),
        in_specs=[yspec] * 4 + [pl.BlockSpec((TB, 4096), lambda i: (i, jgate)),
                                _full(wup.shape), _full(wout.shape), _full((1, 1024)), dspec],
        out_specs=[pl.BlockSpec((TB, 4096), lambda i: (i, jgate))] + [yspec] * 4 + [
            dspec] * 6 + [_full((1, 1024))],
        compiler_params=pltpu.CompilerParams(dimension_semantics=("arbitrary",), vmem_limit_bytes=VMEM_LIMIT),
    )(*ys, cols, wup, wout, gpost, dx)
    dwup = jnp.stack([_token_product(ys[n], dproj[n], "%s_w_up%d" % (name, n)) for n in range(4)])
    dwout = _token_product(merged, dout, name + "_w_out")
    return dcols, dya, dyb, dyc, dym, dwup, dwout, dgp


NB = 256


def prenorm_forward(x, gain, name):
    T, D = x.shape

    def body(x_ref, g_ref, o_ref, ot_ref):
        h = _rms(x_ref[...], g_ref[...])
        o_ref[...] = h.astype(BF)
        ot_ref[...] = h.T.astype(BF)

    return pl.pallas_call(
        body, name=name,
        out_shape=[jax.ShapeDtypeStruct((T, D), BF), jax.ShapeDtypeStruct((D, T), BF)], grid=(T // NB,),
        in_specs=[pl.BlockSpec((NB, D), lambda i: (i, 0)), _full((1, D))],
        out_specs=[pl.BlockSpec((NB, D), lambda i: (i, 0)), pl.BlockSpec((D, NB), lambda i: (0, i))],
        compiler_params=pltpu.CompilerParams(dimension_semantics=("parallel",)),
    )(x, gain)


def prenorm_backward(x, gain, dh, dres, name):
    T = x.shape[0]

    def body(x_ref, g_ref, dh_ref, dr_ref, dx_ref, dg_ref):
        @pl.when(pl.program_id(0) == 0)
        def _():
            dg_ref[...] = jnp.zeros_like(dg_ref)

        _, vjp = jax.vjp(_rms, x_ref[...], g_ref[...])
        dxn, dg = vjp(dh_ref[...])
        dx_ref[...] = dr_ref[...] + dxn
        dg_ref[...] += dg

    spec = pl.BlockSpec((NB, 1024), lambda i: (i, 0))
    return pl.pallas_call(
        body, name=name,
        out_shape=[jax.ShapeDtypeStruct(x.shape, F32), jax.ShapeDtypeStruct((1, 1024), F32)], grid=(T // NB,),
        in_specs=[spec, _full((1, 1024)), spec, spec], out_specs=[spec, _full((1, 1024))],
        compiler_params=pltpu.CompilerParams(dimension_semantics=("arbitrary",)),
    )(x, gain, dh, dres)


JB_GATE, JB_XM, JB_DN, JB_SW, JB_GM = 0, 4, 2, 5, 6
_ALIGNED_PIECES = ((5896, 4096), (4872, 512), (5384, 512), (0, 2048), (2048, 8), 504, (3592, 512), (4360, 512),
                   (4104, 128), (4232, 128), 256, (2056, 1024), (3080, 512))
_NATURAL_FROM_ALIGNED = ((5120, 2048), (7168, 8), (9216, 1024), (10240, 512), (7680, 512), (8704, 128), (8832, 128),
                         (8192, 512), (4096, 512), (4608, 512), (0, 4096))


def _natural_range(slots, start, width):
    out = []
    while width > 0:
        j, i = divmod(start, W_IN_SHARD)
        take = min(width, W_IN_SHARD - i)
        out.append(slots[j, :, i:i + take])
        start, width = start + take, width - take
    return out


def _aligned_w_in(slots, zero=0.0):
    parts = []
    for piece in _ALIGNED_PIECES:
        if isinstance(piece, int):
            parts.append(jnp.full(slots.shape[1:2] + (piece,), zero, slots.dtype))
        else:
            parts += _natural_range(slots, *piece)
    return jnp.concatenate(parts, axis=-1)


def _slots_of_aligned(d_al):
    slots = []
    for s in range(N_DEV):
        lo, hi = s * W_IN_SHARD, (s + 1) * W_IN_SHARD
        parts, nat = [], 0
        for a_start, width in _NATURAL_FROM_ALIGNED:
            b, e = max(lo, nat), min(hi, nat + width)
            if b < e:
                parts.append(d_al[..., a_start + b - nat:a_start + e - nat])
            nat += width
        parts.append(jnp.zeros(d_al.shape[:1] + (W_IN_SHARD_PAD - W_IN_SHARD,), d_al.dtype))
        slots.append(jnp.concatenate(parts, axis=-1))
    return jnp.stack(slots)


SMALL_VEC_W = 1024


def _pack_small(parts):
    rows = []
    for p in parts:
        flat = p.reshape(-1).astype(F32)
        r = -(-flat.shape[0] // SMALL_VEC_W)
        rows.append(jnp.pad(flat, (0, r * SMALL_VEC_W - flat.shape[0])).reshape(r, SMALL_VEC_W))
    vec = jnp.concatenate(rows, axis=0)
    return jnp.pad(vec, ((0, -vec.shape[0] % 8), (0, 0)))


def _unpack_small(vec, shapes):
    out, off = [], 0
    for s in shapes:
        n = math.prod(s)
        r = -(-n // SMALL_VEC_W)
        out.append(vec[off:off + r].reshape(-1)[:n].reshape(s))
        off += r
    return out


def _lanes(vec, at):
    return jnp.zeros((1, 128), F32).at[0, at:at + vec.shape[0]].set(vec)


SMALL_NAMES = ("norm_pre", "norm_post", "norm_mem", "a_log", "dt_bias", "dn_norm", "gm_norm",
               "spatial_w", "spatial_b", "sinks")


def _other_weights(s_mem, s_up, s_out):
    return (s_mem.reshape(D_MODEL, 2 * BRANCH_W),
            jnp.transpose(s_up, (1, 2, 0, 3)).reshape(N_BRANCH, BRANCH_W, D_MODEL), s_out.reshape(D_MODEL, D_MODEL))


def _grad_slots(d_in_al, d_mem, d_up, d_out):
    return [None if d_in_al is None else _slots_of_aligned(d_in_al), d_mem.astype(BF).reshape(N_DEV, 128, 2 * BRANCH_W),
            jnp.transpose(d_up.astype(BF).reshape(N_BRANCH, BRANCH_W, N_DEV, 128), (2, 0, 1, 3)),
            d_out.astype(BF).reshape(N_DEV, 128, D_MODEL)]


def _layer_params(l, small, conv_full, token):
    return dict(
        gpre=small["norm_pre"][l][None] + token, gpost=small["norm_post"][l][None], gmem=small["norm_mem"][l][None],
        cw=conv_full[l], al=_lanes(small["a_log"][l], 4), dt=_lanes(small["dt_bias"][l], 4),
        dnn=small["dn_norm"][l][None], gain=small["gm_norm"][l][None], ws=small["spatial_w"][l],
        bt=jnp.zeros((128, 128), F32).at[:, :GM_GROUPS].set(small["spatial_b"][l].T),
        sinks=_lanes(small["sinks"][l], 0))


def _layer_forward(l, xl, hs, mem, p, w_in_al, other_weights, **tail):
    t = "l%d_" % l
    h, h_t = hs
    cols = _matmul(h, w_in_al, "nn", F32, (1024, 1536, 1024), t + "w_in")
    ya, ss, ts = dn_forward(cols, JB_DN, p["cw"], p["al"], p["dt"], p["dnn"], t + "deltanet")
    yb = gm_forward(cols, JB_GM, p["gain"], p["ws"], p["bt"], t + "gmlp")
    yc = sw_forward(cols, JB_SW, p["sinks"], t + "swa")
    w_mem, w_up, w_out = other_weights(yc)
    mkv = memkv_forward(mem, p["gmem"], w_mem, t + "memkv")
    ym = xm_forward(cols, JB_XM, mkv, t + "memattn")
    outs = merge_forward([ya, yb, yc, ym], cols, JB_GATE, xl, w_up, w_out, p["gpost"], t + "merge", **tail)
    return outs, dict(p, x=xl, h_t=h_t, cols=cols, mkv=mkv, ss=ss, ts=ts, ys=[ya, yb, yc, ym]), (w_in_al, w_mem, w_up, w_out)


def _layer_backward(l, s, mem, weights, dx, token, early=None):
    w_in_al, w_mem, w_up, w_out = weights
    t = "l%d_" % l
    cols = s["cols"]
    dcols, dya, dyb, dyc, dym, dwup, dwout, dgpost = merge_backward(
        s["ys"], cols, JB_GATE, w_up, w_out, s["gpost"] + token, dx, t + "merge_bwd")
    dcols, dmkv = xm_backward(cols, JB_XM, s["mkv"], dym, dcols, t + "memattn_bwd")
    dgmem, dwmem = memkv_backward(mem, s["gmem"], w_mem, dmkv, t + "memkv_bwd")
    sinks = s["sinks"] if early is None else s["sinks"] + early(dwmem, dwup, dwout)
    dcols, dsinks = sw_backward(cols, JB_SW, sinks, dyc, dcols, t + "swa_bwd")
    dcols, dgain, dws, dbt = gm_backward(cols, JB_GM, s["gain"], s["ws"], s["bt"], dyb, dcols, t + "gmlp_bwd")
    dcols, dcw, dal, ddt, ddn = dn_backward(
        cols, JB_DN, s["cw"], s["al"], s["dt"], s["dnn"], s["ss"], s["ts"], dya, dcols, t + "deltanet_bwd")
    dh = _matmul(dcols, w_in_al, "nt", F32, (1024, 1024, 3584), t + "w_in_bwd_x")
    dwin = _matmul(s["h_t"], dcols, "nn", BF, (1024, 1536, 2048), t + "w_in_bwd_w")
    dx, dgpre = prenorm_backward(s["x"], s["gpre"], dh, dx, t + "prenorm_bwd")
    gsmall = dict(norm_pre=dgpre[0], norm_post=dgpost[0], norm_mem=dgmem[0], a_log=dal[0, 4:8], dt_bias=ddt[0, 4:8],
                  dn_norm=ddn[0], gm_norm=dgain[0], spatial_w=dws, spatial_b=dbt[:, :GM_GROUPS].T,
                  sinks=dsinks[0, :SW_HEADS], conv_w=dcw)
    return dx, gsmall, (dwin, dwmem, dwup, dwout)


def kernel(x, mem, norm_pre, norm_post, norm_mem, w_in, conv_w, a_log, dt_bias, dn_norm, gm_norm, spatial_w, spatial_b, sinks, w_mem_kv, w_up, w_out, loss_target, m_norm_pre, m_norm_post, m_norm_mem, m_w_in, m_conv_w, m_a_log, m_dt_bias, m_dn_norm, m_gm_norm, m_spatial_w, m_spatial_b, m_sinks, m_w_mem_kv, m_w_up, m_w_out, v_norm_pre, v_norm_post, v_norm_mem, v_w_in, v_conv_w, v_a_log, v_dt_bias, v_dn_norm, v_gm_norm, v_spatial_w, v_spatial_b, v_sinks, v_w_mem_kv, v_w_up, v_w_out):
    xi, yi, ci = _my_place()
    my_slot = 4 * xi + 2 * yi + ci
    conv_shard = conv_w.shape[-1]
    x2, mem2, target = x[0], mem[0], loss_target[0]

    w_in_pad = jnp.pad(w_in.astype(BF), ((0, 0), (0, 0), (0, W_IN_SHARD_PAD - W_IN_SHARD)))
    shards = [[w_in_pad[l], w_mem_kv[l].astype(BF), w_up[l].astype(BF), w_out[l].astype(BF)] for l in range(DEPTH)]
    w_in_slots0, conv_slots = _all_gather_slots([shards[0][0], conv_w], "gather_weights_l0")
    ag = list(_spread_start(shards[0][1:] + shards[1], "gather", "gather_weights_rest_start"))
    conv_full = jnp.transpose(conv_slots, (1, 2, 0, 3)).reshape(DEPTH, CONV_W, N_DEV * conv_shard)
    small = dict(norm_pre=norm_pre, norm_post=norm_post, norm_mem=norm_mem, a_log=a_log,
                 dt_bias=dt_bias, dn_norm=dn_norm, gm_norm=gm_norm, spatial_w=spatial_w,
                 spatial_b=spatial_b, sinks=sinks)

    def arrived(which, after, name):
        ag[2], ag[3] = _spread_wait(ag[0], ag[1], ag[2], ag[3], which, after, name)
        return [ag[3][a] for a in which]

    p0, p1 = _layer_params(0, small, conv_full, ag[4][0, 0]), _layer_params(1, small, conv_full, 0.0)
    (x1, h1, h1_t), saved0, weights0 = _layer_forward(
        0, x2, prenorm_forward(x2, p0["gpre"], "l0_prenorm"), mem2, p0, _aligned_w_in(w_in_slots0, ag[4][0, 0]),
        lambda y: _other_weights(*arrived([0, 1, 2], y, "gather_weights_l0_rest_wait")),
        next_gain=p1["gpre"])
    w_in_slots1, = arrived([3], x1, "gather_weights_l1_w_in_wait")
    (x_out, loss, dx), saved1, weights1 = _layer_forward(
        1, x1, (h1, h1_t), mem2, p1, _aligned_w_in(w_in_slots1),
        lambda y: _other_weights(*arrived([4, 5, 6], y, "gather_weights_l1_rest_wait")), target=target)

    packed_names = SMALL_NAMES + ("conv_w",)
    dx, gsmall1, gbig1 = _layer_backward(1, saved1, mem2, weights1, dx, 0.0)
    small1 = [loss[0, :1]] + [gsmall1[n] for n in packed_names]
    sm1 = _spread_start([_pack_small(small1)], "gather", "gather_small_grads_l1_start")
    rs_send, rs_recv, rs_src, rs_land, rs_token = _spread_start(_grad_slots(*gbig1), "scatter", "exchange_grads_l1_start")
    rest0 = []

    def send_rest0(dwmem, dwup, dwout):
        rest0.extend(_spread_start(_grad_slots(None, dwmem, dwup, dwout)[1:], "scatter", "exchange_grads_l0_rest_start"))
        return rest0[4][0, 0]

    dx, gsmall0, gbig0 = _layer_backward(0, saved0, mem2, weights0, dx, rs_token[0, 0] + sm1[4][0, 0], send_rest0)
    _, parts1 = _spread_wait(rs_send, rs_recv, rs_src, rs_land, range(4), dx, "exchange_grads_l1_wait")

    small0 = [gsmall0[n] for n in packed_names]
    sm0 = _spread_start([_pack_small(small0)], "gather", "gather_small_grads_l0_start")

    g_win0 = _slots_of_aligned(gbig0[0])
    g_win0 = g_win0.reshape((N_DEV // 2, 2) + g_win0.shape[1:])
    theirs, = _exchange_sibling([g_win0], "exchange_sibling_l0")
    chip_sum = _pair_sum(g_win0, theirs, "pair_sum_l0")
    ch_send, ch_recv, ch_src, ch_land, ch_token = _spread_start([chip_sum], "chips", "exchange_chips_l0_start")

    _, (land1,) = _spread_wait(*sm1[:4], [0], ch_token, "gather_small_grads_l1_wait")
    _, (land0,) = _spread_wait(*sm0[:4], [0], land1, "gather_small_grads_l0_wait")
    tot1 = _unpack_small(_sum_slots(land1, "sum_small_grads_l1"), [p.shape for p in small1])
    tot0 = _unpack_small(_sum_slots(land0, "sum_small_grads_l0"), [p.shape for p in small0])
    loss_tot = tot1[0][0]
    grads = {n: jnp.stack([g0, g1]) for n, g0, g1 in zip(packed_names, tot0, tot1[1:])}
    grads["conv_w"] = lax.dynamic_slice_in_dim(grads["conv_w"], my_slot * conv_shard, conv_shard, axis=2)

    given = dict(norm_pre=(norm_pre, m_norm_pre, v_norm_pre), norm_post=(norm_post, m_norm_post, v_norm_post),
                 norm_mem=(norm_mem, m_norm_mem, v_norm_mem), a_log=(a_log, m_a_log, v_a_log),
                 dt_bias=(dt_bias, m_dt_bias, v_dt_bias), dn_norm=(dn_norm, m_dn_norm, v_dn_norm),
                 gm_norm=(gm_norm, m_gm_norm, v_gm_norm), spatial_w=(spatial_w, m_spatial_w, v_spatial_w),
                 spatial_b=(spatial_b, m_spatial_b, v_spatial_b), sinks=(sinks, m_sinks, v_sinks),
                 conv_w=(conv_w, m_conv_w, v_conv_w))
    pshapes = [given[n][0].shape for n in packed_names]
    pw, pm, pv = (_pack_small([given[n][i] for n in packed_names]) for i in range(3))
    pd, pnm, pnv = _adamw(pw + ch_token[0, 0], _pack_small([grads[n] for n in packed_names]), pm, pv, "adamw_small")
    upd = {n: t for n, t in zip(packed_names, zip(_unpack_small(pd, pshapes), _unpack_small(pnm, pshapes),
                                                  _unpack_small(pnv, pshapes)))}
    big = (("w_mem_kv", (w_mem_kv, m_w_mem_kv, v_w_mem_kv)), ("w_up", (w_up, m_w_up, v_w_up)),
           ("w_out", (w_out, m_w_out, v_w_out)))
    first = [_sum_adamw(parts1[1 + i], w, m, v, 1, None, ch_token, "adamw_%s_l1" % name)
             for i, (name, (w, m, v)) in enumerate(big)]
    _, parts0_rest = _spread_wait(*rest0[:4], range(3), first[-1][0], "exchange_grads_l0_rest_wait")
    for i, (name, (w, m, v)) in enumerate(big):
        g, d, nm, nv = _sum_adamw(parts0_rest[i], w, m, v, 0, first[i], None, "adamw_%s_l0" % name)
        grads[name], upd[name] = g, (d, nm, nv)
    _, (parts0_w_in,) = _spread_wait(ch_send, ch_recv, ch_src, ch_land, [0], upd["w_out"][0], "exchange_chips_l0_wait")
    w_in_t, m_w_in_t, v_w_in_t = (jnp.transpose(t, (2, 0, 1)) for t in (w_in, m_w_in, v_w_in))
    g, d, nm, nv = (jnp.transpose(t, (1, 2, 0)) for t in
                    _sum_adamw_t([parts0_w_in, parts1[0]], w_in_t, m_w_in_t, v_w_in_t, "adamw_w_in"))
    grads["w_in"], upd["w_in"] = g, (d, nm, nv)

    order = ("norm_pre", "norm_post", "norm_mem", "w_in", "conv_w", "a_log", "dt_bias", "dn_norm",
             "gm_norm", "spatial_w", "spatial_b", "sinks", "w_mem_kv", "w_up", "w_out")
    return (loss_tot, dx[None], *[grads[n] for n in order], *[upd[n][0] for n in order],
            *[upd[n][1] for n in order], *[upd[n][2] for n in order])
```

```python
import functools
import math

import jax
import jax.numpy as jnp
from jax import lax
from jax.experimental import pallas as pl
from jax.experimental.pallas import tpu as pltpu

MESH = pl.DeviceIdType.MESH
N_DEV = 8

D_MODEL = 1024
DEPTH = 2
N_BRANCH = 4
BRANCH_W = 512
DN_HEADS = 4
CONV_W = 4
GM_GROUPS = 4
SW_HEADS = 8
EPS = 1e-6
NEG_INF = -1e30

D_IN = 9992
W_IN_SHARD = D_IN // N_DEV
W_IN_SHARD_PAD = 1280
D_IN_AL = 10752
DN_W, SW_W, GM_W, XM_W = 2560, 1536, 1536, 1024

ADAM_LR = 0.001
ADAM_B1 = 0.9
ADAM_B2 = 0.999
ADAM_EPS = 1e-08
ADAM_WD = 0.01
ADAM_STEP = 10

VMEM_LIMIT = 56 * 1024 * 1024

BF = jnp.bfloat16
F32 = jnp.float32
DN_C = 128
DN_D = 128
HALO = 8
BLK = 128
Y_DTYPE = BF


def _my_place():
    return lax.axis_index("x"), lax.axis_index("y"), lax.axis_index("c")


_ANY = pl.BlockSpec(memory_space=pl.ANY)


def _all_gather_slots(parts, name):
    n = len(parts)

    def body(*refs):
        p_refs, out_refs = refs[:n], refs[n:2 * n]
        send_sems, recv_sems, local_sems = refs[2 * n:]
        x, y, c = _my_place()
        me, sibling = (x, y, c), (x, y, 1 - c)
        chips = [(1 - x, y), (x, 1 - y), (1 - x, 1 - y)]

        def copy(a, k, block, to, src=None):
            px, py, pc = block
            slot = out_refs[a].at[4 * px + 2 * py + pc]
            return pltpu.make_async_remote_copy(
                src_ref=slot if src is None else src, dst_ref=slot,
                send_sem=send_sems.at[7 * a + k], recv_sem=recv_sems.at[7 * a + k],
                device_id=to, device_id_type=MESH)

        mine = [pltpu.make_async_copy(p_refs[a], out_refs[a].at[4 * x + 2 * y + c], local_sems.at[a])
                for a in range(n)]
        for cp in mine:
            cp.start()
        first = []
        for a in range(n):
            first.append(copy(a, 0, me, sibling, src=p_refs[a]))
            first += [copy(a, 1 + j, me, (*chip, c), src=p_refs[a]) for j, chip in enumerate(chips)]
        for cp in first:
            cp.start()
        passed = []
        for j, chip in enumerate(chips):
            for a in range(n):
                copy(a, 1 + j, (*chip, c), me).wait_recv()
                fwd = copy(a, 4 + j, (*chip, c), sibling)
                fwd.start()
                passed.append(fwd)
        for a in range(n):
            copy(a, 0, sibling, me).wait_recv()
            for j, chip in enumerate(chips):
                copy(a, 4 + j, (*chip, 1 - c), me).wait_recv()
        for cp in first + passed:
            cp.wait_send()
        for cp in mine:
            cp.wait()

    return pl.pallas_call(
        body, name=name,
        out_shape=[jax.ShapeDtypeStruct((N_DEV,) + p.shape, p.dtype) for p in parts],
        in_specs=[_ANY] * n, out_specs=[_ANY] * n,
        scratch_shapes=[pltpu.SemaphoreType.DMA((7 * n,)), pltpu.SemaphoreType.DMA((7 * n,)),
                        pltpu.SemaphoreType.DMA((n,))],
    )(*parts)


def _exchange_sibling(parts, name):
    n = len(parts)

    def body(*refs):
        g_refs, out_refs = refs[:n], refs[n:2 * n]
        send_sems, recv_sems = refs[2 * n:]
        x, y, c = _my_place()
        copies = [pltpu.make_async_remote_copy(
            src_ref=g_refs[a].at[:, 1 - c], dst_ref=out_refs[a],
            send_sem=send_sems.at[a], recv_sem=recv_sems.at[a],
            device_id=(x, y, 1 - c), device_id_type=MESH) for a in range(n)]
        for cp in copies:
            cp.start()
        for cp in copies:
            cp.wait()

    return pl.pallas_call(
        body, name=name,
        out_shape=[jax.ShapeDtypeStruct((4,) + g.shape[2:], g.dtype) for g in parts],
        in_specs=[_ANY] * n, out_specs=[_ANY] * n,
        scratch_shapes=[pltpu.SemaphoreType.DMA((n,)), pltpu.SemaphoreType.DMA((n,))],
    )(*parts)


_HBM = pl.BlockSpec(memory_space=pltpu.HBM)
_SEM = pl.BlockSpec(memory_space=pltpu.SEMAPHORE)
_EFFECT = pltpu.SideEffectType.DATAFLOW_SIDE_EFFECTING


def _peer(x, y, c, k):
    return (1 - x if (k >> 2) & 1 else x, 1 - y if (k >> 1) & 1 else y, 1 - c if k & 1 else c)


def _spread_start(srcs, mode, name, after=None):
    n = len(srcs)
    tail = [] if after is None else [after]
    lands = [lax.empty((N_DEV,) + s.shape if mode == "gather" else s.shape, s.dtype) for s in srcs]
    peers = range(0, N_DEV, 2) if mode == "chips" else range(N_DEV)

    def body(*refs):
        src_refs, land_refs = refs[:n], refs[n:2 * n]
        send_sems, recv_sems = refs[2 * n + len(tail):2 * n + len(tail) + 2]
        token = refs[-1]
        x, y, c = _my_place()
        for a in range(n):
            for k in peers:
                px, py, pc = _peer(x, y, c, k)
                if mode == "chips":
                    src, mine = src_refs[a].at[2 * px + py], 2 * x + y
                else:
                    src = src_refs[a].at[4 * px + 2 * py + pc] if mode == "scatter" else src_refs[a]
                    mine = 4 * x + 2 * y + c
                pltpu.make_async_remote_copy(
                    src_ref=src, dst_ref=land_refs[a].at[mine],
                    send_sem=send_sems.at[a], recv_sem=recv_sems.at[a],
                    device_id=(px, py, pc), device_id_type=MESH).start()
        token[...] = jnp.zeros_like(token)

    out = pl.pallas_call(
        body, name=name,
        out_shape=[pltpu.SemaphoreType.DMA((n,)), pltpu.SemaphoreType.DMA((n,))]
        + [pltpu.HBM(s.shape, s.dtype) for s in srcs] + [pltpu.HBM(l.shape, l.dtype) for l in lands]
        + [jax.ShapeDtypeStruct((8, 128), F32)],
        in_specs=[_HBM] * (2 * n) + [_ANY] * len(tail),
        out_specs=[_SEM, _SEM] + [_HBM] * (2 * n) + [pl.BlockSpec(memory_space=pltpu.VMEM)],
        input_output_aliases={i: 2 + i for i in range(2 * n)},
        compiler_params=pltpu.CompilerParams(has_side_effects=_EFFECT),
    )(*[pltpu.with_memory_space_constraint(s, pltpu.HBM) for s in srcs],
      *[pltpu.with_memory_space_constraint(l, pltpu.HBM) for l in lands], *tail)
    return out[0], out[1], out[2:2 + n], out[2 + n:2 + 2 * n], out[-1]


def _spread_wait(send_sems, recv_sems, srcs, lands, which, after, name):
    n = len(srcs)

    def body(*refs):
        land_refs = refs[n:2 * n]
        send_sems, recv_sems = refs[2 * n:2 * n + 2]
        x, y, c = _my_place()
        for a in which:
            whole = pltpu.make_async_remote_copy(
                src_ref=land_refs[a], dst_ref=land_refs[a],
                send_sem=send_sems.at[a], recv_sem=recv_sems.at[a],
                device_id=(x, y, c), device_id_type=MESH)
            whole.wait_send()
            whole.wait_recv()

    out = pl.pallas_call(
        body, name=name,
        out_shape=[pltpu.HBM(s.shape, s.dtype) for s in srcs] + [pltpu.HBM(l.shape, l.dtype) for l in lands],
        in_specs=[_HBM] * (2 * n) + [_SEM, _SEM, _ANY],
        out_specs=[_HBM] * (2 * n),
        input_output_aliases={i: i for i in range(2 * n)},
        compiler_params=pltpu.CompilerParams(has_side_effects=_EFFECT),
    )(*srcs, *lands, send_sems, recv_sems, after)
    return out[:n], out[n:]


def _sum_slots(parts, name):
    def body(p_ref, o_ref):
        acc = p_ref[0]
        for s in range(1, parts.shape[0]):
            acc = acc + p_ref[s]
        o_ref[...] = acc

    return pl.pallas_call(body, name=name, out_shape=jax.ShapeDtypeStruct(parts.shape[1:], parts.dtype))(parts)


def _pick(n, pref):
    if n <= pref:
        return n
    t = pref - pref % 128
    while t > 0 and n % t:
        t -= 128
    return t if t > 0 else n


_DIMS = {"nn": (((1,), (0,)), ((), ())),
         "nt": (((1,), (1,)), ((), ())),
         "tn": (((0,), (0,)), ((), ()))}


def _matmul(a, b, mode, out_dtype, tiles, name):
    (m, k) = a.shape
    n = b.shape[1] if mode == "nn" else b.shape[0]
    tm, tn, tk = (_pick(d, t) for d, t in zip((m, n, k), tiles))
    nk = k // tk

    def product(a_ref, b_ref):
        return lax.dot_general(a_ref[...].astype(BF), b_ref[...].astype(BF), _DIMS[mode], preferred_element_type=F32)

    def body_whole_k(a_ref, b_ref, o_ref):
        o_ref[...] = product(a_ref, b_ref).astype(o_ref.dtype)

    def body_split_k(a_ref, b_ref, o_ref, acc_ref):
        kk = pl.program_id(2)

        @pl.when(kk == 0)
        def _():
            acc_ref[...] = jnp.zeros_like(acc_ref)

        acc_ref[...] += product(a_ref, b_ref)

        @pl.when(kk == nk - 1)
        def _():
            o_ref[...] = acc_ref[...].astype(o_ref.dtype)

    b_spec = (pl.BlockSpec((tn, tk), lambda i, j, kk: (j, kk)) if mode == "nt"
              else pl.BlockSpec((tk, tn), lambda i, j, kk: (kk, j)))
    return pl.pallas_call(
        body_whole_k if nk == 1 else body_split_k, name=name,
        out_shape=jax.ShapeDtypeStruct((m, n), out_dtype),
        grid=(m // tm, n // tn, nk),
        in_specs=[pl.BlockSpec((tm, tk), lambda i, j, kk: (i, kk)), b_spec],
        out_specs=pl.BlockSpec((tm, tn), lambda i, j, kk: (i, j)),
        scratch_shapes=[] if nk == 1 else [pltpu.VMEM((tm, tn), F32)],
        compiler_params=pltpu.CompilerParams(
            dimension_semantics=("parallel", "parallel", "arbitrary"),
            vmem_limit_bytes=VMEM_LIMIT),
    )(a, b)


def _rows2d(t, lead):
    return t.reshape(t.shape[:lead] + (math.prod(t.shape[lead:-1]), t.shape[-1]))


def _pair_sum(g, theirs, name):
    g3, t3 = _rows2d(g, 2), _rows2d(theirs, 1)
    _, r, w = t3.shape
    tr = _pick(r, 512)

    def body(g_ref, t_ref, o_ref):
        c = lax.axis_index("c")
        mine = jnp.where(c == 0, g_ref[0, 0], g_ref[0, 1])
        o_ref[0] = (mine.astype(F32) + t_ref[0].astype(F32)).astype(o_ref.dtype)

    out = pl.pallas_call(
        body, name=name,
        out_shape=jax.ShapeDtypeStruct(t3.shape, t3.dtype),
        grid=(4, r // tr),
        in_specs=[pl.BlockSpec((1, 2, tr, w), lambda q, i: (q, 0, i, 0)),
                  pl.BlockSpec((1, tr, w), lambda q, i: (q, i, 0))],
        out_specs=pl.BlockSpec((1, tr, w), lambda q, i: (q, i, 0)),
        compiler_params=pltpu.CompilerParams(dimension_semantics=("parallel", "parallel")),
    )(g3, t3)
    return out.reshape(theirs.shape)


def _adam_update(w, g, m, v):
    c1 = 1.0 - ADAM_B1 ** ADAM_STEP
    c2 = 1.0 - ADAM_B2 ** ADAM_STEP
    nm = ADAM_B1 * m + (1.0 - ADAM_B1) * g
    nv = ADAM_B2 * v + (1.0 - ADAM_B2) * (g * g)
    delta = -ADAM_LR * ((nm / c1) / (jnp.sqrt(nv / c2) + ADAM_EPS) + ADAM_WD * w)
    return delta, nm, nv


def _sum_adamw(parts, w, m, v, layer, carry, after, name):
    shape = w.shape
    cols = shape[-1]
    p3 = _rows2d(parts, 1)
    w3, m3, v3 = (_rows2d(t, 1) for t in (w, m, v))
    rows = w3.shape[1]
    tr = _pick(rows, 128)
    n_parts = p3.shape[0]

    def body(p_ref, w_ref, m_ref, v_ref, *rest):
        g_ref, d_ref, nm_ref, nv_ref = rest[-4:]
        g = p_ref[0, :, :cols].astype(F32)
        for q in range(1, n_parts):
            g = g + p_ref[q, :, :cols].astype(F32)
        d, nm, nv = _adam_update(w_ref[0], g, m_ref[0], v_ref[0])
        g_ref[0] = g
        d_ref[0] = d
        nm_ref[0] = nm
        nv_ref[0] = nv

    spec = pl.BlockSpec((1, tr, cols), lambda i: (layer, i, 0))
    extra = [] if carry is None else [_rows2d(t, 1) for t in carry]
    tail = [] if after is None else [after]
    out = pl.pallas_call(
        body, name=name,
        out_shape=[jax.ShapeDtypeStruct(w3.shape, F32)] * 4,
        grid=(rows // tr,),
        in_specs=[pl.BlockSpec((n_parts, tr, p3.shape[-1]), lambda i: (0, i, 0)), spec, spec, spec] + [_ANY] * len(extra + tail),
        out_specs=[spec] * 4,
        input_output_aliases={4 + i: i for i in range(len(extra))},
        compiler_params=pltpu.CompilerParams(dimension_semantics=("parallel",)),
    )(p3, w3, m3, v3, *extra, *tail)
    return tuple(t.reshape(shape) for t in out)


def _sum_adamw_t(parts, w, m, v, name):
    rows = parts[0].shape[2]
    tr = 128
    assert rows % tr == 0 and rows >= w.shape[0]

    def body(*refs):
        p_refs, (w_ref, m_ref, v_ref), (g_ref, d_ref, nm_ref, nv_ref) = refs[:DEPTH], refs[DEPTH:DEPTH + 3], refs[DEPTH + 3:]
        for l in range(DEPTH):
            g = p_refs[l][0].astype(F32)
            for q in range(1, p_refs[l].shape[0]):
                g = g + p_refs[l][q].astype(F32)
            g = g.T
            d, nm, nv = _adam_update(w_ref[:, l, :], g, m_ref[:, l, :], v_ref[:, l, :])
            g_ref[:, l, :] = g
            d_ref[:, l, :] = d
            nm_ref[:, l, :] = nm
            nv_ref[:, l, :] = nv

    spec = pl.BlockSpec((tr,) + w.shape[1:], lambda i: (i, 0, 0))
    return pl.pallas_call(
        body, name=name,
        out_shape=[jax.ShapeDtypeStruct(w.shape, F32)] * 4,
        grid=(rows // tr,),
        in_specs=[pl.BlockSpec((p.shape[0], p.shape[1], tr), lambda i: (0, 0, i)) for p in parts] + [spec] * 3,
        out_specs=[spec] * 4,
        compiler_params=pltpu.CompilerParams(dimension_semantics=("parallel",)),
    )(*parts, w, m, v)


def _adamw(w, g, m, v, name):
    rows, cols = w.shape
    tr = _pick(rows, 128)

    def body(w_ref, g_ref, m_ref, v_ref, d_ref, nm_ref, nv_ref):
        d, nm, nv = _adam_update(w_ref[...], g_ref[...], m_ref[...], v_ref[...])
        d_ref[...] = d
        nm_ref[...] = nm
        nv_ref[...] = nv

    spec = pl.BlockSpec((tr, cols), lambda i: (i, 0))
    return pl.pallas_call(
        body, name=name,
        out_shape=[jax.ShapeDtypeStruct((rows, cols), F32)] * 3,
        grid=(rows // tr,),
        in_specs=[spec] * 4, out_specs=[spec] * 3,
        compiler_params=pltpu.CompilerParams(dimension_semantics=("parallel",)),
    )(w, g, m, v)


_VJP = {"nn": (("nt", "gb"), ("tn", "ag")),
        "nt": (("nn", "gb"), ("tn", "ga")),
        "tn": (("nt", "bg"), ("nn", "ag"))}


def _make_dot(cast, precision):
    def raw(mode, a, b):
        return lax.dot_general(cast(a), cast(b), _DIMS[mode], precision=precision,
                               preferred_element_type=F32)

    @functools.partial(jax.custom_vjp, nondiff_argnums=(0,))
    def dot(mode, a, b):
        return raw(mode, a, b)

    def fwd(mode, a, b):
        return raw(mode, a, b), (a, b)

    def bwd(mode, res, g):
        a, b = res
        pick = {"a": a, "b": b, "g": g}
        (ma, ta), (mb, tb) = _VJP[mode]
        return dot(ma, pick[ta[0]], pick[ta[1]]), dot(mb, pick[tb[0]], pick[tb[1]])

    dot.defvjp(fwd, bwd)
    return dot


bdot = _make_dot(lambda t: t.astype(BF), None)
hdot = _make_dot(lambda t: t, lax.Precision.HIGHEST)


def _xdot(mode, a, b):
    return lax.dot_general(a, b, _DIMS[mode], precision=lax.Precision.HIGH, preferred_element_type=F32)


def _unit_lower_inverse(Ls):
    n = Ls[0].shape[0]
    batched = (((2,), (1,)), ((0,), (0,)))
    mm = lambda a, b: lax.dot_general(a, b, batched, precision=lax.Precision.HIGH, preferred_element_type=F32)
    eye = (lax.broadcasted_iota(jnp.int32, (n, n), 0) == lax.broadcasted_iota(jnp.int32, (n, n), 1)).astype(F32)
    p = jnp.stack(Ls)
    t_inv = eye[None] - p
    for _ in range(6):
        p = mm(p, p)
        t_inv = t_inv + mm(t_inv, p)
    return [t_inv[h] for h in range(len(Ls))]


@jax.custom_vjp
def _tri_solve(L, rhs, t_inv):
    return _xdot("nn", t_inv, rhs)


def _tri_solve_fwd(L, rhs, t_inv):
    sol = _xdot("nn", t_inv, rhs)
    return sol, (t_inv, sol)


def _tri_solve_bwd(res, dsol):
    t_inv, sol = res
    drhs = _xdot("tn", t_inv, dsol)
    return -_xdot("nt", drhs, sol), drhs, jnp.zeros_like(t_inv)


_tri_solve.defvjp(_tri_solve_fwd, _tri_solve_bwd)


def _sigmoid(x):
    return 1.0 / (1.0 + jnp.exp(-x))


def _softplus(x):
    return jnp.maximum(x, 0.0) + jnp.log(1.0 + jnp.exp(-jnp.abs(x)))


def _dn_chunk(S, xs, ba, z, cw, al, dt, dn, t_saved=None):
    C = DN_C
    pre = xs[0] * cw[0] + xs[1] * cw[1] + xs[2] * cw[2] + xs[3] * cw[3]
    qkv = pre * _sigmoid(pre)
    lane = lax.broadcasted_iota(jnp.int32, (1, 128), 1)
    sub = lax.broadcasted_iota(jnp.int32, (C, 1), 0)
    row_i = lax.broadcasted_iota(jnp.int32, (C, C), 0)
    col_i = lax.broadcasted_iota(jnp.int32, (C, C), 1)
    strict = row_i > col_i
    incl = row_i >= col_i
    g_all = jnp.where((lane >= 4) & (lane < 8), -jnp.exp(al) * _softplus(ba + dt), 0.0)
    gc_all = hdot("nn", incl.astype(F32), g_all)
    gc_all_t = gc_all.T
    beta_all = _sigmoid(ba)
    glast_all = jnp.sum(jnp.where(sub == C - 1, gc_all, 0.0), axis=0, keepdims=True)
    heads = []
    for h in range(DN_HEADS):
        q = qkv[:, 128 * h:128 * (h + 1)]
        k = qkv[:, 512 + 128 * h:512 + 128 * (h + 1)]
        v = qkv[:, 1024 + 128 * h:1024 + 128 * (h + 1)]
        q = q * lax.rsqrt(jnp.sum(q * q, axis=1, keepdims=True) + EPS) * (DN_D ** -0.5)
        k = k * lax.rsqrt(jnp.sum(k * k, axis=1, keepdims=True) + EPS)
        beta = jnp.sum(jnp.where(lane == h, beta_all, 0.0), axis=1, keepdims=True)
        gc = jnp.sum(jnp.where(lane == 4 + h, gc_all, 0.0), axis=1, keepdims=True)
        gc_row = jnp.sum(jnp.where(sub == 4 + h, gc_all_t, 0.0), axis=0, keepdims=True)
        g_last = jnp.sum(jnp.where(lane == 4 + h, glast_all, 0.0), axis=1, keepdims=True)
        diff = gc - gc_row
        kb = k * beta
        L = jnp.where(strict, bdot("nt", kb, k) * jnp.exp(jnp.where(strict, diff, 0.0)), 0.0)
        heads.append((q, k, v, beta, gc, g_last, diff, kb, L))
    t_invs = _unit_lower_inverse([hd[-1] for hd in heads]) if t_saved is None else t_saved
    ys, s_new = [], []
    for h, (q, k, v, beta, gc, g_last, diff, kb, L) in enumerate(heads):
        sol = _tri_solve(L, jnp.concatenate([v * beta, kb * jnp.exp(gc)], axis=1), t_invs[h])
        u, w = sol[:, :DN_D], sol[:, DN_D:]
        a_qk = jnp.where(incl, bdot("nt", q, k) * jnp.exp(jnp.where(incl, diff, 0.0)), 0.0)
        qg = q * jnp.exp(gc)
        kd = k * jnp.exp(g_last - gc)
        v_new = u - bdot("nn", w, S[h])
        o = bdot("nn", qg, S[h]) + bdot("nn", a_qk, v_new)
        s_new.append(S[h] * jnp.exp(g_last) + bdot("tn", kd, v_new))
        o = o * lax.rsqrt(jnp.mean(o * o, axis=1, keepdims=True) + EPS) * dn
        zh = z[:, 128 * h:128 * (h + 1)]
        ys.append(o * (zh * _sigmoid(zh)))
    return jnp.concatenate(ys, axis=1), tuple(s_new), tuple(t_invs)


def _load_shifted(xbuf, x_ref, halo_ref, first):
    xbuf[0:HALO, :] = jnp.where(first, 0.0, halo_ref[:, 0:1536])
    xbuf[HALO:HALO + DN_C, :] = x_ref[:, 0:1536]
    return [xbuf[HALO - 3 + k:HALO - 3 + k + DN_C, :] for k in range(4)]


def dn_forward(cols, jblk, cw, al, dt, dn, name):
    T = cols.shape[0]
    n = T // DN_C

    def body(x_ref, halo_ref, cw_ref, al_ref, dt_ref, dn_ref, y_ref, ss_ref, ts_ref, s_scr, xbuf):
        i = pl.program_id(0)

        @pl.when(i == 0)
        def _():
            s_scr[...] = jnp.zeros_like(s_scr)

        xs = _load_shifted(xbuf, x_ref, halo_ref, i == 0)
        ss_ref[0] = s_scr[...]
        S = [s_scr[h] for h in range(DN_HEADS)]
        cws = [cw_ref[k:k + 1, :] for k in range(4)]
        y, s_new, t_invs = _dn_chunk(S, xs, x_ref[:, 2048:2176], x_ref[:, 1536:2048], cws,
                                     al_ref[...], dt_ref[...], dn_ref[...])
        y_ref[...] = y.astype(y_ref.dtype)
        for h in range(DN_HEADS):
            s_scr[h] = s_new[h]
            ts_ref[0, h] = t_invs[h]

    per = DN_C // HALO
    full = lambda shape: pl.BlockSpec(shape, lambda i: (0,) * len(shape))
    return pl.pallas_call(
        body, name=name,
        out_shape=[jax.ShapeDtypeStruct((T, 512), Y_DTYPE),
                   jax.ShapeDtypeStruct((n, DN_HEADS, DN_D, DN_D), F32),
                   jax.ShapeDtypeStruct((n, DN_HEADS, DN_D, DN_D), F32)],
        grid=(n,),
        in_specs=[pl.BlockSpec((DN_C, DN_W), lambda i: (i, jblk)),
                  pl.BlockSpec((HALO, DN_W), lambda i: (jnp.maximum(i * per - 1, 0), jblk)),
                  full((4, 1536)), full((1, 128)), full((1, 128)), full((1, 128))],
        out_specs=[pl.BlockSpec((DN_C, 512), lambda i: (i, 0)),
                   pl.BlockSpec((1, DN_HEADS, DN_D, DN_D), lambda i: (i, 0, 0, 0)),
                   pl.BlockSpec((1, DN_HEADS, DN_D, DN_D), lambda i: (i, 0, 0, 0))],
        scratch_shapes=[pltpu.VMEM((DN_HEADS, DN_D, DN_D), F32), pltpu.VMEM((HALO + DN_C, 1536), F32)],
        compiler_params=pltpu.CompilerParams(dimension_semantics=("arbitrary",)),
    )(cols, cols, cw, al, dt, dn)


def dn_backward(cols, jblk, cw, al, dt, dn, ss, ts, dy, dcols, name):
    T = cols.shape[0]
    n = T // DN_C

    def body(x_ref, halo_ref, cw_ref, al_ref, dt_ref, dn_ref, ss_ref, ts_ref, dy_ref, dcols_in,
             dx_ref, dcw_ref, dal_ref, ddt_ref, ddn_ref, ds_scr, xbuf, dbuf, carry):
        i = pl.program_id(0)

        @pl.when(i == 0)
        def _():
            ds_scr[...] = jnp.zeros_like(ds_scr)
            carry[...] = jnp.zeros_like(carry)
            dcw_ref[...] = jnp.zeros_like(dcw_ref)
            dal_ref[...] = jnp.zeros_like(dal_ref)
            ddt_ref[...] = jnp.zeros_like(ddt_ref)
            ddn_ref[...] = jnp.zeros_like(ddn_ref)

        xs = _load_shifted(xbuf, x_ref, halo_ref, i == n - 1)
        S = [ss_ref[0, h] for h in range(DN_HEADS)]
        cws = [cw_ref[k:k + 1, :] for k in range(4)]

        t_saved = [ts_ref[0, h] for h in range(DN_HEADS)]

        def f(S, xs, ba, z, cws, al, dt, dn):
            return _dn_chunk(S, xs, ba, z, cws, al, dt, dn, t_saved)[:2]

        _, vjp = jax.vjp(f, S, xs, x_ref[:, 2048:2176], x_ref[:, 1536:2048], cws, al_ref[...], dt_ref[...], dn_ref[...])
        dS, dxs, dba, dz, dcws, dal, ddt, ddn = vjp((dy_ref[...], tuple(ds_scr[h] for h in range(DN_HEADS))))
        for h in range(DN_HEADS):
            ds_scr[h] = dS[h]
        dbuf[...] = jnp.zeros_like(dbuf)
        for k in range(4):
            lo = HALO - 3 + k
            dbuf[lo:lo + DN_C, :] += dxs[k]
        dbuf[DN_C:DN_C + HALO, :] += carry[...]
        dx_ref[...] = jnp.concatenate([dbuf[HALO:HALO + DN_C, :], dz, dba,
                                       jnp.zeros((DN_C, DN_W - 2176), F32)], axis=1).astype(dx_ref.dtype)
        carry[...] = dbuf[0:HALO, :]
        for k in range(4):
            dcw_ref[k:k + 1, :] += dcws[k]
        dal_ref[...] += dal
        ddt_ref[...] += ddt
        ddn_ref[...] += ddn

    per = DN_C // HALO
    rev = lambda i: n - 1 - i
    full = lambda shape: pl.BlockSpec(shape, lambda i: (0,) * len(shape))
    return pl.pallas_call(
        body, name=name,
        out_shape=[jax.ShapeDtypeStruct(dcols.shape, dcols.dtype),jax.ShapeDtypeStruct((4, 1536), F32),
                   jax.ShapeDtypeStruct((1, 128), F32), jax.ShapeDtypeStruct((1, 128), F32),
                   jax.ShapeDtypeStruct((1, 128), F32)],
        grid=(n,),
        in_specs=[pl.BlockSpec((DN_C, DN_W), lambda i: (rev(i), jblk)),
                  pl.BlockSpec((HALO, DN_W), lambda i: (jnp.maximum(rev(i) * per - 1, 0), jblk)),
                  full((4, 1536)), full((1, 128)), full((1, 128)), full((1, 128)),
                  pl.BlockSpec((1, DN_HEADS, DN_D, DN_D), lambda i: (rev(i), 0, 0, 0)),
                  pl.BlockSpec((1, DN_HEADS, DN_D, DN_D), lambda i: (rev(i), 0, 0, 0)),
                  pl.BlockSpec((DN_C, 512), lambda i: (rev(i), 0)), _ANY],
        out_specs=[pl.BlockSpec((DN_C, DN_W), lambda i: (rev(i), jblk)),
                   full((4, 1536)), full((1, 128)), full((1, 128)), full((1, 128))],
        scratch_shapes=[pltpu.VMEM((DN_HEADS, DN_D, DN_D), F32), pltpu.VMEM((HALO + DN_C, 1536), F32),
                        pltpu.VMEM((HALO + DN_C, 1536), F32), pltpu.VMEM((HALO, 1536), F32)],
        input_output_aliases={9: 0},
        compiler_params=pltpu.CompilerParams(dimension_semantics=("arbitrary",)),
    )(cols, cols, cw, al, dt, dn, ss, ts, dy, dcols)


def _full(shape):
    return pl.BlockSpec(shape, lambda i: (0,) * len(shape))


def _silu(x):
    return x * _sigmoid(x)


def _gelu(x):
    return 0.5 * x * (1.0 + jnp.tanh(0.7978845608028654 * (x + 0.044715 * (x * x * x))))


def _lane_col(mat, idx):
    lane = lax.broadcasted_iota(jnp.int32, (1, mat.shape[1]), 1)
    return jnp.sum(jnp.where(lane == idx, mat, 0.0), axis=1, keepdims=True)


def _gm_chunk(uv, z, gain, ws, bt):
    g = _gelu(uv)
    u, v = g[:, :512], g[:, 512:]
    v = v * lax.rsqrt(jnp.mean(v * v, axis=1, keepdims=True) + EPS) * gain
    row_i = lax.broadcasted_iota(jnp.int32, (BLK, BLK), 0)
    col_i = lax.broadcasted_iota(jnp.int32, (BLK, BLK), 1)
    causal = row_i >= col_i
    ss = []
    for grp in range(4):
        wg = jnp.where(causal, ws[grp], 0.0)
        ss.append(bdot("nn", wg, v[:, BLK * grp:BLK * (grp + 1)]) + _lane_col(bt, grp))
    return u * jnp.concatenate(ss, axis=1) * _silu(z)


def gm_forward(cols, jblk, gain, ws, bt, name):
    T = cols.shape[0]

    def body(x_ref, gain_ref, ws_ref, bt_ref, y_ref):
        y_ref[...] = _gm_chunk(x_ref[:, 0:1024], x_ref[:, 1024:1536], gain_ref[...],
                               [ws_ref[g] for g in range(4)], bt_ref[...]).astype(y_ref.dtype)

    return pl.pallas_call(
        body, name=name, out_shape=jax.ShapeDtypeStruct((T, 512), Y_DTYPE), grid=(T // BLK,),
        in_specs=[pl.BlockSpec((BLK, GM_W), lambda i: (i, jblk)),
                  _full((1, 512)), _full((4, BLK, BLK)), _full((BLK, BLK))],
        out_specs=pl.BlockSpec((BLK, 512), lambda i: (i, 0)),
        compiler_params=pltpu.CompilerParams(dimension_semantics=("parallel",)),
    )(cols, gain, ws, bt)


def gm_backward(cols, jblk, gain, ws, bt, dy, dcols, name):
    T = cols.shape[0]

    def body(x_ref, gain_ref, ws_ref, bt_ref, dy_ref, dcols_in, dx_ref, dgain_ref, dws_ref, dbt_ref):
        @pl.when(pl.program_id(0) == 0)
        def _():
            dgain_ref[...] = jnp.zeros_like(dgain_ref)
            dws_ref[...] = jnp.zeros_like(dws_ref)
            dbt_ref[...] = jnp.zeros_like(dbt_ref)

        _, vjp = jax.vjp(_gm_chunk, x_ref[:, 0:1024], x_ref[:, 1024:1536], gain_ref[...],
                         [ws_ref[g] for g in range(4)], bt_ref[...])
        duv, dz, dgain, dws, dbt = vjp(dy_ref[...])
        dx_ref[...] = jnp.concatenate([duv, dz], axis=1).astype(dx_ref.dtype)
        dgain_ref[...] += dgain
        for g in range(4):
            dws_ref[g] += dws[g]
        dbt_ref[...] += dbt

    return pl.pallas_call(
        body, name=name,
        out_shape=[jax.ShapeDtypeStruct(dcols.shape, dcols.dtype),jax.ShapeDtypeStruct((1, 512), F32),
                   jax.ShapeDtypeStruct((4, BLK, BLK), F32), jax.ShapeDtypeStruct((BLK, BLK), F32)],
        grid=(T // BLK,),
        in_specs=[pl.BlockSpec((BLK, GM_W), lambda i: (i, jblk)),
                  _full((1, 512)), _full((4, BLK, BLK)), _full((BLK, BLK)),
                  pl.BlockSpec((BLK, 512), lambda i: (i, 0)), _ANY],
        out_specs=[pl.BlockSpec((BLK, GM_W), lambda i: (i, jblk)),
                   _full((1, 512)), _full((4, BLK, BLK)), _full((BLK, BLK))],
        input_output_aliases={5: 0},
        compiler_params=pltpu.CompilerParams(dimension_semantics=("arbitrary",)),
    )(cols, gain, ws, bt, dy, dcols)


def _sw_block(first, q, kp, kc, vp, vc, z, sinks):
    P = BLK
    lane = lax.broadcasted_iota(jnp.int32, (1, 128), 1)
    r = lax.broadcasted_iota(jnp.int32, (128, 128), 0)
    c = lax.broadcasted_iota(jnp.int32, (128, 128), 1)
    swap = (c == (r + 64) % 128).astype(F32)
    k2 = jnp.concatenate([kp, kc], axis=0)
    v2 = jnp.concatenate([vp, vc], axis=0)
    k2s = bdot("nn", k2, swap)
    v2s = bdot("nn", v2, swap)
    qi = lax.broadcasted_iota(jnp.int32, (P, 2 * P), 0)
    kj = lax.broadcasted_iota(jnp.int32, (P, 2 * P), 1)
    dist = qi + P - kj
    valid = (dist >= 0) & (dist < P) & ((kj >= P) | jnp.logical_not(first))
    outs = []
    for j in range(4):
        acc = jnp.zeros((P, 128), F32)
        for half in range(2):
            h = 2 * j + half
            kv = h // 4
            in_half = (lane >= 64 * half) & (lane < 64 * half + 64)
            qh = jnp.where(in_half, q[:, 128 * j:128 * (j + 1)], 0.0)
            same = (half == kv)
            s = bdot("nt", qh, k2 if same else k2s) * (64 ** -0.5)
            s = jnp.where(valid, s, NEG_INF)
            sink = _lane_col(sinks, h)
            m = lax.stop_gradient(jnp.maximum(jnp.max(s, axis=1, keepdims=True), sink))
            e = jnp.exp(s - m)
            p = e / (jnp.sum(e, axis=1, keepdims=True) + jnp.exp(sink - m))
            o = bdot("nn", p, v2 if same else v2s)
            acc = acc + jnp.where(in_half, o, 0.0)
        outs.append(acc)
    return jnp.concatenate(outs, axis=1) * _silu(z)


def _sw_specs(jblk, idx):
    prev = lambda i: jnp.maximum(idx(i) - 1, 0)
    jk = (jblk * SW_W + 1024) // 128
    return [pl.BlockSpec((BLK, SW_W), lambda i: (idx(i), jblk)),
            pl.BlockSpec((BLK, 128), lambda i: (prev(i), jk)),
            pl.BlockSpec((BLK, 128), lambda i: (prev(i), jk + 1)), _full((1, 128))]


def sw_forward(cols, jblk, sinks, name):
    T = cols.shape[0]

    def body(x_ref, kp_ref, vp_ref, s_ref, y_ref):
        y_ref[...] = _sw_block(pl.program_id(0) == 0, x_ref[:, 0:512], kp_ref[...], x_ref[:, 1024:1152],
                               vp_ref[...], x_ref[:, 1152:1280], x_ref[:, 512:1024], s_ref[...]).astype(y_ref.dtype)

    return pl.pallas_call(
        body, name=name, out_shape=jax.ShapeDtypeStruct((T, 512), Y_DTYPE), grid=(T // BLK,),
        in_specs=_sw_specs(jblk, lambda i: i),
        out_specs=pl.BlockSpec((BLK, 512), lambda i: (i, 0)),
        compiler_params=pltpu.CompilerParams(dimension_semantics=("parallel",)),
    )(cols, cols, cols, sinks)


def sw_backward(cols, jblk, sinks, dy, dcols, name):
    T = cols.shape[0]
    n = T // BLK
    rev = lambda i: n - 1 - i

    def body(x_ref, kp_ref, vp_ref, s_ref, dy_ref, dcols_in, dx_ref, ds_ref, kcarry, vcarry):
        i = pl.program_id(0)

        @pl.when(i == 0)
        def _():
            kcarry[...] = jnp.zeros_like(kcarry)
            vcarry[...] = jnp.zeros_like(vcarry)
            ds_ref[...] = jnp.zeros_like(ds_ref)

        f = functools.partial(_sw_block, i == n - 1)
        _, vjp = jax.vjp(f, x_ref[:, 0:512], kp_ref[...], x_ref[:, 1024:1152], vp_ref[...], x_ref[:, 1152:1280],
                         x_ref[:, 512:1024], s_ref[...])
        dq, dkp, dkc, dvp, dvc, dz, dsk = vjp(dy_ref[...])
        dx_ref[...] = jnp.concatenate([dq, dz, dkc + kcarry[...], dvc + vcarry[...],
                                       jnp.zeros((BLK, SW_W - 1280), F32)], axis=1).astype(dx_ref.dtype)
        kcarry[...] = dkp
        vcarry[...] = dvp
        ds_ref[...] += dsk

    return pl.pallas_call(
        body, name=name,
        out_shape=[jax.ShapeDtypeStruct(dcols.shape, dcols.dtype),jax.ShapeDtypeStruct((1, 128), F32)],
        grid=(n,),
        in_specs=_sw_specs(jblk, rev) + [pl.BlockSpec((BLK, 512), lambda i: (rev(i), 0)), _ANY],
        out_specs=[pl.BlockSpec((BLK, SW_W), lambda i: (rev(i), jblk)), _full((1, 128))],
        scratch_shapes=[pltpu.VMEM((BLK, 128), F32), pltpu.VMEM((BLK, 128), F32)],
        input_output_aliases={5: 0},
        compiler_params=pltpu.CompilerParams(dimension_semantics=("arbitrary",)),
    )(cols, cols, cols, sinks, dy, dcols)


XM_TQ = 512


def _xm_block(q, z, mkv):
    outs = []
    for h in range(4):
        s = bdot("nt", q[:, 128 * h:128 * (h + 1)], mkv[:, 128 * h:128 * (h + 1)]) * (128 ** -0.5)
        m = lax.stop_gradient(jnp.max(s, axis=1, keepdims=True))
        e = jnp.exp(s - m)
        p = e / jnp.sum(e, axis=1, keepdims=True)
        outs.append(bdot("nn", p, mkv[:, 512 + 128 * h:512 + 128 * (h + 1)]))
    return jnp.concatenate(outs, axis=1) * _silu(z)


def xm_forward(cols, jblk, mkv, name):
    T = cols.shape[0]

    def body(x_ref, m_ref, y_ref):
        y_ref[...] = _xm_block(x_ref[:, 0:512], x_ref[:, 512:1024], m_ref[...]).astype(y_ref.dtype)

    return pl.pallas_call(
        body, name=name, out_shape=jax.ShapeDtypeStruct((T, 512), Y_DTYPE), grid=(T // XM_TQ,),
        in_specs=[pl.BlockSpec((XM_TQ, XM_W), lambda i: (i, jblk)), _full(mkv.shape)],
        out_specs=pl.BlockSpec((XM_TQ, 512), lambda i: (i, 0)),
        compiler_params=pltpu.CompilerParams(dimension_semantics=("parallel",)),
    )(cols, mkv)


def xm_backward(cols, jblk, mkv, dy, dcols, name):
    T = cols.shape[0]

    def body(x_ref, m_ref, dy_ref, dcols_in, dx_ref, dm_ref):
        @pl.when(pl.program_id(0) == 0)
        def _():
            dm_ref[...] = jnp.zeros_like(dm_ref)

        _, vjp = jax.vjp(_xm_block, x_ref[:, 0:512], x_ref[:, 512:1024], m_ref[...])
        dq, dz, dm = vjp(dy_ref[...])
        dx_ref[...] = jnp.concatenate([dq, dz], axis=1).astype(dx_ref.dtype)
        dm_ref[...] += dm

    return pl.pallas_call(
        body, name=name,
        out_shape=[jax.ShapeDtypeStruct(dcols.shape, dcols.dtype),jax.ShapeDtypeStruct(mkv.shape, F32)],
        grid=(T // XM_TQ,),
        in_specs=[pl.BlockSpec((XM_TQ, XM_W), lambda i: (i, jblk)), _full(mkv.shape),
                  pl.BlockSpec((XM_TQ, 512), lambda i: (i, 0)), _ANY],
        out_specs=[pl.BlockSpec((XM_TQ, XM_W), lambda i: (i, jblk)), _full(mkv.shape)],
        input_output_aliases={3: 0},
        compiler_params=pltpu.CompilerParams(dimension_semantics=("arbitrary",)),
    )(cols, mkv, dy, dcols)


def _rms(x, gain):
    return x * lax.rsqrt(jnp.mean(x * x, axis=1, keepdims=True) + EPS) * gain


def memkv_forward(mem, gain, w, name):
    def body(m_ref, g_ref, w_ref, o_ref):
        o_ref[...] = bdot("nn", _rms(m_ref[...], g_ref[...]), w_ref[...])

    return pl.pallas_call(body, name=name, out_shape=jax.ShapeDtypeStruct(mem.shape, F32),
                          compiler_params=pltpu.CompilerParams(vmem_limit_bytes=VMEM_LIMIT))(mem, gain, w)


def memkv_backward(mem, gain, w, dkv, name):
    def body(m_ref, g_ref, w_ref, d_ref, dg_ref, dw_ref):
        mem_v = m_ref[...]
        _, vjp = jax.vjp(lambda g, ww: bdot("nn", _rms(mem_v, g), ww), g_ref[...], w_ref[...].astype(F32))
        dg, dw = vjp(d_ref[...])
        dg_ref[...] = dg
        dw_ref[...] = dw

    return pl.pallas_call(body, name=name,
                          out_shape=[jax.ShapeDtypeStruct(gain.shape, F32), jax.ShapeDtypeStruct(w.shape, F32)],
                          compiler_params=pltpu.CompilerParams(vmem_limit_bytes=VMEM_LIMIT))(mem, gain, w, dkv)


MG_TB = 256


def _merge_block(ys, gl, wup, wout, gpost):
    merged = None
    for n in range(4):
        t = _sigmoid(gl[:, 1024 * n:1024 * (n + 1)]) * bdot("nn", ys[n], wup[n])
        merged = t if merged is None else merged + t
    out = bdot("nn", merged, wout)
    return _rms(out, gpost)


def merge_forward(ys, cols, jgate, x, wup, wout, gpost, name, next_gain=None, target=None):
    T, D = x.shape
    TB = 256
    n_extra = (next_gain is not None) + (target is not None)

    def body(ya, yb, yc, ym, gl_ref, x_ref, wup_ref, wout_ref, gp_ref, *rest):
        extra, outs = rest[:n_extra], rest[n_extra:]
        upd = _merge_block([ya[...], yb[...], yc[...], ym[...]], gl_ref[...],
                           [wup_ref[n] for n in range(4)], wout_ref[...], gp_ref[...])
        y = x_ref[...] + upd
        outs[0][...] = y
        outs = outs[1:]
        if next_gain is not None:
            h = _rms(y, extra[0][...])
            outs[0][...] = h.astype(BF)
            outs[1][...] = h.T.astype(BF)
            outs = outs[2:]
        if target is not None:
            l_ref, d_ref = outs

            @pl.when(pl.program_id(0) == 0)
            def _():
                l_ref[...] = jnp.zeros_like(l_ref)

            err = y - extra[-1][...]
            d_ref[...] = err * (1.0 / D)
            l_ref[...] += jnp.full(l_ref.shape, 0.5 * jnp.sum(jnp.mean(err * err, axis=1, keepdims=True)), F32)

    yspec = pl.BlockSpec((TB, 512), lambda i: (i, 0))
    xspec = pl.BlockSpec((TB, D), lambda i: (i, 0))
    extra_in, extra_specs = [], []
    out_shape, out_specs = [jax.ShapeDtypeStruct((T, D), F32)], [xspec]
    if next_gain is not None:
        extra_in, extra_specs = extra_in + [next_gain], extra_specs + [_full((1, D))]
        out_shape += [jax.ShapeDtypeStruct((T, D), BF), jax.ShapeDtypeStruct((D, T), BF)]
        out_specs += [xspec, pl.BlockSpec((D, TB), lambda i: (0, i))]
    if target is not None:
        extra_in, extra_specs = extra_in + [target], extra_specs + [xspec]
        out_shape += [jax.ShapeDtypeStruct((1, 128), F32), jax.ShapeDtypeStruct((T, D), F32)]
        out_specs += [_full((1, 128)), xspec]
    return pl.pallas_call(
        body, name=name, out_shape=out_shape, grid=(T // TB,),
        in_specs=[yspec] * 4 + [pl.BlockSpec((TB, 4096), lambda i: (i, jgate)), xspec,
                                _full(wup.shape), _full(wout.shape), _full((1, D))] + extra_specs,
        out_specs=out_specs,
        compiler_params=pltpu.CompilerParams(
            dimension_semantics=("parallel" if target is None else "arbitrary",), vmem_limit_bytes=VMEM_LIMIT),
    )(*ys, cols, x, wup, wout, gpost, *extra_in)


def _token_product(a, b, name):
    (T, m), n = a.shape, b.shape[1]

    def body(a_ref, b_ref, o_ref):
        o_ref[...] = lax.dot_general(a_ref[...].astype(BF), b_ref[...].astype(BF), _DIMS["tn"], preferred_element_type=F32)

    return pl.pallas_call(body, name=name, out_shape=jax.ShapeDtypeStruct((m, n), F32),
                          compiler_params=pltpu.CompilerParams(vmem_limit_bytes=VMEM_LIMIT))(a, b)


def merge_backward(ys, cols, jgate, wup, wout, gpost, dx, name):
    T = dx.shape[0]
    TB = MG_TB

    def body(ya, yb, yc, ym, gl_ref, wup_ref, wout_ref, gp_ref, dx_ref,
             dgl_ref, dya, dyb, dyc, dym, dpa, dpb, dpc, dpm, merged_ref, dout_ref, dgp_ref):
        @pl.when(pl.program_id(0) == 0)
        def _():
            dgp_ref[...] = jnp.zeros_like(dgp_ref)

        y_refs = (ya, yb, yc, ym)
        gates = [_sigmoid(gl_ref[:, 1024 * n:1024 * (n + 1)]) for n in range(4)]
        projs = [bdot("nn", y_refs[n][...], wup_ref[n]) for n in range(4)]
        merged = gates[0] * projs[0] + gates[1] * projs[1] + gates[2] * projs[2] + gates[3] * projs[3]
        out = bdot("nn", merged, wout_ref[...])
        _, vjp = jax.vjp(_rms, out, gp_ref[...])
        dout, dgp = vjp(dx_ref[...])
        dmerged = bdot("nt", dout, wout_ref[...])
        for n, (dy_ref, dp_ref) in enumerate(zip((dya, dyb, dyc, dym), (dpa, dpb, dpc, dpm))):
            dproj = dmerged * gates[n]
            dgl_ref[:, 1024 * n:1024 * (n + 1)] = (dmerged * projs[n] * gates[n] * (1.0 - gates[n])).astype(dgl_ref.dtype)
            dy_ref[...] = bdot("nt", dproj, wup_ref[n])
            dp_ref[...] = dproj.astype(BF)
        merged_ref[...] = merged.astype(BF)
        dout_ref[...] = dout.astype(BF)
        dgp_ref[...] += dgp

    yspec = pl.BlockSpec((TB, 512), lambda i: (i, 0))
    dspec = pl.BlockSpec((TB, 1024), lambda i: (i, 0))
    dcols, dya, dyb, dyc, dym, *dproj, merged, dout, dgp = pl.pallas_call(
        body, name=name,
        out_shape=[jax.ShapeDtypeStruct(cols.shape, BF)] + [jax.ShapeDtypeStruct((T, 512), F32)] * 4 + [
            jax.ShapeDtypeStruct((T, 1024), BF)] * 6 + [jax.ShapeDtypeStruct((1, 1024), F32)],
        grid=(T // TB,),
        in_specs=[yspec] * 4 + [pl.BlockSpec((TB, 4096), lambda i: (i, jgate)),
                                _full(wup.shape), _full(wout.shape), _full((1, 1024)), dspec],
        out_specs=[pl.BlockSpec((TB, 4096), lambda i: (i, jgate))] + [yspec] * 4 + [
            dspec] * 6 + [_full((1, 1024))],
        compiler_params=pltpu.CompilerParams(dimension_semantics=("arbitrary",), vmem_limit_bytes=VMEM_LIMIT),
    )(*ys, cols, wup, wout, gpost, dx)
    dwup = jnp.stack([_token_product(ys[n], dproj[n], "%s_w_up%d" % (name, n)) for n in range(4)])
    dwout = _token_product(merged, dout, name + "_w_out")
    return dcols, dya, dyb, dyc, dym, dwup, dwout, dgp


NB = 256


def prenorm_forward(x, gain, name):
    T, D = x.shape

    def body(x_ref, g_ref, o_ref, ot_ref):
        h = _rms(x_ref[...], g_ref[...])
        o_ref[...] = h.astype(BF)
        ot_ref[...] = h.T.astype(BF)

    return pl.pallas_call(
        body, name=name,
        out_shape=[jax.ShapeDtypeStruct((T, D), BF), jax.ShapeDtypeStruct((D, T), BF)], grid=(T // NB,),
        in_specs=[pl.BlockSpec((NB, D), lambda i: (i, 0)), _full((1, D))],
        out_specs=[pl.BlockSpec((NB, D), lambda i: (i, 0)), pl.BlockSpec((D, NB), lambda i: (0, i))],
        compiler_params=pltpu.CompilerParams(dimension_semantics=("parallel",)),
    )(x, gain)


def prenorm_backward(x, gain, dh, dres, name):
    T = x.shape[0]

    def body(x_ref, g_ref, dh_ref, dr_ref, dx_ref, dg_ref):
        @pl.when(pl.program_id(0) == 0)
        def _():
            dg_ref[...] = jnp.zeros_like(dg_ref)

        _, vjp = jax.vjp(_rms, x_ref[...], g_ref[...])
        dxn, dg = vjp(dh_ref[...])
        dx_ref[...] = dr_ref[...] + dxn
        dg_ref[...] += dg

    spec = pl.BlockSpec((NB, 1024), lambda i: (i, 0))
    return pl.pallas_call(
        body, name=name,
        out_shape=[jax.ShapeDtypeStruct(x.shape, F32), jax.ShapeDtypeStruct((1, 1024), F32)], grid=(T // NB,),
        in_specs=[spec, _full((1, 1024)), spec, spec], out_specs=[spec, _full((1, 1024))],
        compiler_params=pltpu.CompilerParams(dimension_semantics=("arbitrary",)),
    )(x, gain, dh, dres)


JB_GATE, JB_XM, JB_DN, JB_SW, JB_GM = 0, 4, 2, 5, 6
_ALIGNED_PIECES = ((5896, 4096), (4872, 512), (5384, 512), (0, 2048), (2048, 8), 504, (3592, 512), (4360, 512),
                   (4104, 128), (4232, 128), 256, (2056, 1024), (3080, 512))
_NATURAL_FROM_ALIGNED = ((5120, 2048), (7168, 8), (9216, 1024), (10240, 512), (7680, 512), (8704, 128), (8832, 128),
                         (8192, 512), (4096, 512), (4608, 512), (0, 4096))


def _natural_range(slots, start, width):
    out = []
    while width > 0:
        j, i = divmod(start, W_IN_SHARD)
        take = min(width, W_IN_SHARD - i)
        out.append(slots[j, :, i:i + take])
        start, width = start + take, width - take
    return out


def _aligned_w_in(slots, zero=0.0):
    parts = []
    for piece in _ALIGNED_PIECES:
        if isinstance(piece, int):
            parts.append(jnp.full(slots.shape[1:2] + (piece,), zero, slots.dtype))
        else:
            parts += _natural_range(slots, *piece)
    return jnp.concatenate(parts, axis=-1)


def _slots_of_aligned(d_al):
    slots = []
    for s in range(N_DEV):
        lo, hi = s * W_IN_SHARD, (s + 1) * W_IN_SHARD
        parts, nat = [], 0
        for a_start, width in _NATURAL_FROM_ALIGNED:
            b, e = max(lo, nat), min(hi, nat + width)
            if b < e:
                parts.append(d_al[..., a_start + b - nat:a_start + e - nat])
            nat += width
        parts.append(jnp.zeros(d_al.shape[:1] + (W_IN_SHARD_PAD - W_IN_SHARD,), d_al.dtype))
        slots.append(jnp.concatenate(parts, axis=-1))
    return jnp.stack(slots)


SMALL_VEC_W = 1024


def _pack_small(parts):
    rows = []
    for p in parts:
        flat = p.reshape(-1).astype(F32)
        r = -(-flat.shape[0] // SMALL_VEC_W)
        rows.append(jnp.pad(flat, (0, r * SMALL_VEC_W - flat.shape[0])).reshape(r, SMALL_VEC_W))
    vec = jnp.concatenate(rows, axis=0)
    return jnp.pad(vec, ((0, -vec.shape[0] % 8), (0, 0)))


def _unpack_small(vec, shapes):
    out, off = [], 0
    for s in shapes:
        n = math.prod(s)
        r = -(-n // SMALL_VEC_W)
        out.append(vec[off:off + r].reshape(-1)[:n].reshape(s))
        off += r
    return out


def _lanes(vec, at):
    return jnp.zeros((1, 128), F32).at[0, at:at + vec.shape[0]].set(vec)


SMALL_NAMES = ("norm_pre", "norm_post", "norm_mem", "a_log", "dt_bias", "dn_norm", "gm_norm",
               "spatial_w", "spatial_b", "sinks")


def _other_weights(s_mem, s_up, s_out):
    return (s_mem.reshape(D_MODEL, 2 * BRANCH_W),
            jnp.transpose(s_up, (1, 2, 0, 3)).reshape(N_BRANCH, BRANCH_W, D_MODEL), s_out.reshape(D_MODEL, D_MODEL))


def _grad_slots(d_in_al, d_mem, d_up, d_out):
    return [None if d_in_al is None else _slots_of_aligned(d_in_al), d_mem.astype(BF).reshape(N_DEV, 128, 2 * BRANCH_W),
            jnp.transpose(d_up.astype(BF).reshape(N_BRANCH, BRANCH_W, N_DEV, 128), (2, 0, 1, 3)),
            d_out.astype(BF).reshape(N_DEV, 128, D_MODEL)]


def _layer_params(l, small, conv_full, token):
    return dict(
        gpre=small["norm_pre"][l][None] + token, gpost=small["norm_post"][l][None], gmem=small["norm_mem"][l][None],
        cw=conv_full[l], al=_lanes(small["a_log"][l], 4), dt=_lanes(small["dt_bias"][l], 4),
        dnn=small["dn_norm"][l][None], gain=small["gm_norm"][l][None], ws=small["spatial_w"][l],
        bt=jnp.zeros((128, 128), F32).at[:, :GM_GROUPS].set(small["spatial_b"][l].T),
        sinks=_lanes(small["sinks"][l], 0))


def _layer_forward(l, xl, hs, mem, p, w_in_al, other_weights, **tail):
    t = "l%d_" % l
    h, h_t = hs
    cols = _matmul(h, w_in_al, "nn", F32, (1024, 1536, 1024), t + "w_in")
    ya, ss, ts = dn_forward(cols, JB_DN, p["cw"], p["al"], p["dt"], p["dnn"], t + "deltanet")
    yb = gm_forward(cols, JB_GM, p["gain"], p["ws"], p["bt"], t + "gmlp")
    yc = sw_forward(cols, JB_SW, p["sinks"], t + "swa")
    w_mem, w_up, w_out = other_weights(yc)
    mkv = memkv_forward(mem, p["gmem"], w_mem, t + "memkv")
    ym = xm_forward(cols, JB_XM, mkv, t + "memattn")
    outs = merge_forward([ya, yb, yc, ym], cols, JB_GATE, xl, w_up, w_out, p["gpost"], t + "merge", **tail)
    return outs, dict(p, x=xl, h_t=h_t, cols=cols, mkv=mkv, ss=ss, ts=ts, ys=[ya, yb, yc, ym]), (w_in_al, w_mem, w_up, w_out)


def _layer_backward(l, s, mem, weights, dx, token, early=None):
    w_in_al, w_mem, w_up, w_out = weights
    t = "l%d_" % l
    cols = s["cols"]
    dcols, dya, dyb, dyc, dym, dwup, dwout, dgpost = merge_backward(
        s["ys"], cols, JB_GATE, w_up, w_out, s["gpost"] + token, dx, t + "merge_bwd")
    dcols, dmkv = xm_backward(cols, JB_XM, s["mkv"], dym, dcols, t + "memattn_bwd")
    dgmem, dwmem = memkv_backward(mem, s["gmem"], w_mem, dmkv, t + "memkv_bwd")
    sinks = s["sinks"] if early is None else s["sinks"] + early(dwmem, dwup, dwout)
    dcols, dsinks = sw_backward(cols, JB_SW, sinks, dyc, dcols, t + "swa_bwd")
    dcols, dgain, dws, dbt = gm_backward(cols, JB_GM, s["gain"], s["ws"], s["bt"], dyb, dcols, t + "gmlp_bwd")
    dcols, dcw, dal, ddt, ddn = dn_backward(
        cols, JB_DN, s["cw"], s["al"], s["dt"], s["dnn"], s["ss"], s["ts"], dya, dcols, t + "deltanet_bwd")
    dh = _matmul(dcols, w_in_al, "nt", F32, (1024, 1024, 3584), t + "w_in_bwd_x")
    dwin = _matmul(s["h_t"], dcols, "nn", BF, (1024, 1536, 2048), t + "w_in_bwd_w")
    dx, dgpre = prenorm_backward(s["x"], s["gpre"], dh, dx, t + "prenorm_bwd")
    gsmall = dict(norm_pre=dgpre[0], norm_post=dgpost[0], norm_mem=dgmem[0], a_log=dal[0, 4:8], dt_bias=ddt[0, 4:8],
                  dn_norm=ddn[0], gm_norm=dgain[0], spatial_w=dws, spatial_b=dbt[:, :GM_GROUPS].T,
                  sinks=dsinks[0, :SW_HEADS], conv_w=dcw)
    return dx, gsmall, (dwin, dwmem, dwup, dwout)


def kernel(x, mem, norm_pre, norm_post, norm_mem, w_in, conv_w, a_log, dt_bias, dn_norm, gm_norm, spatial_w, spatial_b, sinks, w_mem_kv, w_up, w_out, loss_target, m_norm_pre, m_norm_post, m_norm_mem, m_w_in, m_conv_w, m_a_log, m_dt_bias, m_dn_norm, m_gm_norm, m_spatial_w, m_spatial_b, m_sinks, m_w_mem_kv, m_w_up, m_w_out, v_norm_pre, v_norm_post, v_norm_mem, v_w_in, v_conv_w, v_a_log, v_dt_bias, v_dn_norm, v_gm_norm, v_spatial_w, v_spatial_b, v_sinks, v_w_mem_kv, v_w_up, v_w_out):
    xi, yi, ci = _my_place()
    my_slot = 4 * xi + 2 * yi + ci
    conv_shard = conv_w.shape[-1]
    x2, mem2, target = x[0], mem[0], loss_target[0]

    w_in_pad = jnp.pad(w_in.astype(BF), ((0, 0), (0, 0), (0, W_IN_SHARD_PAD - W_IN_SHARD)))
    shards = [[w_in_pad[l], w_mem_kv[l].astype(BF), w_up[l].astype(BF), w_out[l].astype(BF)] for l in range(DEPTH)]
    w_in_slots0, conv_slots = _all_gather_slots([shards[0][0], conv_w], "gather_weights_l0")
    ag = list(_spread_start(shards[0][1:] + shards[1], "gather", "gather_weights_rest_start", after=w_in_slots0))
    conv_full = jnp.transpose(conv_slots, (1, 2, 0, 3)).reshape(DEPTH, CONV_W, N_DEV * conv_shard)
    small = dict(norm_pre=norm_pre, norm_post=norm_post, norm_mem=norm_mem, a_log=a_log,
                 dt_bias=dt_bias, dn_norm=dn_norm, gm_norm=gm_norm, spatial_w=spatial_w,
                 spatial_b=spatial_b, sinks=sinks)

    def arrived(which, after, name):
        ag[2], ag[3] = _spread_wait(ag[0], ag[1], ag[2], ag[3], which, after, name)
        return [ag[3][a] for a in which]

    p0, p1 = _layer_params(0, small, conv_full, ag[4][0, 0]), _layer_params(1, small, conv_full, 0.0)
    (x1, h1, h1_t), saved0, weights0 = _layer_forward(
        0, x2, prenorm_forward(x2, p0["gpre"], "l0_prenorm"), mem2, p0, _aligned_w_in(w_in_slots0, ag[4][0, 0]),
        lambda y: _other_weights(*arrived([0, 1, 2], y, "gather_weights_l0_rest_wait")),
        next_gain=p1["gpre"])
    w_in_slots1, = arrived([3], x1, "gather_weights_l1_w_in_wait")
    (x_out, loss, dx), saved1, weights1 = _layer_forward(
        1, x1, (h1, h1_t), mem2, p1, _aligned_w_in(w_in_slots1),
        lambda y: _other_weights(*arrived([4, 5, 6], y, "gather_weights_l1_rest_wait")), target=target)

    packed_names = SMALL_NAMES + ("conv_w",)
    dx, gsmall1, gbig1 = _layer_backward(1, saved1, mem2, weights1, dx, 0.0)
    small1 = [loss[0, :1]] + [gsmall1[n] for n in packed_names]
    sm1 = _spread_start([_pack_small(small1)], "gather", "gather_small_grads_l1_start")
    rs_send, rs_recv, rs_src, rs_land, rs_token = _spread_start(_grad_slots(*gbig1), "scatter", "exchange_grads_l1_start")
    rest0 = []

    def send_rest0(dwmem, dwup, dwout):
        rest0.extend(_spread_start(_grad_slots(None, dwmem, dwup, dwout)[1:], "scatter", "exchange_grads_l0_rest_start"))
        return rest0[4][0, 0]

    dx, gsmall0, gbig0 = _layer_backward(0, saved0, mem2, weights0, dx, rs_token[0, 0] + sm1[4][0, 0], send_rest0)
    _, parts1 = _spread_wait(rs_send, rs_recv, rs_src, rs_land, range(4), dx, "exchange_grads_l1_wait")

    small0 = [gsmall0[n] for n in packed_names]
    sm0 = _spread_start([_pack_small(small0)], "gather", "gather_small_grads_l0_start")

    g_win0 = _slots_of_aligned(gbig0[0])
    g_win0 = g_win0.reshape((N_DEV // 2, 2) + g_win0.shape[1:])
    theirs, = _exchange_sibling([g_win0], "exchange_sibling_l0")
    chip_sum = _pair_sum(g_win0, theirs, "pair_sum_l0")
    ch_send, ch_recv, ch_src, ch_land, ch_token = _spread_start([chip_sum], "chips", "exchange_chips_l0_start")

    _, (land1,) = _spread_wait(*sm1[:4], [0], ch_token, "gather_small_grads_l1_wait")
    _, (land0,) = _spread_wait(*sm0[:4], [0], land1, "gather_small_grads_l0_wait")
    tot1 = _unpack_small(_sum_slots(land1, "sum_small_grads_l1"), [p.shape for p in small1])
    tot0 = _unpack_small(_sum_slots(land0, "sum_small_grads_l0"), [p.shape for p in small0])
    loss_tot = tot1[0][0]
    grads = {n: jnp.stack([g0, g1]) for n, g0, g1 in zip(packed_names, tot0, tot1[1:])}
    grads["conv_w"] = lax.dynamic_slice_in_dim(grads["conv_w"], my_slot * conv_shard, conv_shard, axis=2)

    given = dict(norm_pre=(norm_pre, m_norm_pre, v_norm_pre), norm_post=(norm_post, m_norm_post, v_norm_post),
                 norm_mem=(norm_mem, m_norm_mem, v_norm_mem), a_log=(a_log, m_a_log, v_a_log),
                 dt_bias=(dt_bias, m_dt_bias, v_dt_bias), dn_norm=(dn_norm, m_dn_norm, v_dn_norm),
                 gm_norm=(gm_norm, m_gm_norm, v_gm_norm), spatial_w=(spatial_w, m_spatial_w, v_spatial_w),
                 spatial_b=(spatial_b, m_spatial_b, v_spatial_b), sinks=(sinks, m_sinks, v_sinks),
                 conv_w=(conv_w, m_conv_w, v_conv_w))
    pshapes = [given[n][0].shape for n in packed_names]
    pw, pm, pv = (_pack_small([given[n][i] for n in packed_names]) for i in range(3))
    pd, pnm, pnv = _adamw(pw + ch_token[0, 0], _pack_small([grads[n] for n in packed_names]), pm, pv, "adamw_small")
    upd = {n: t for n, t in zip(packed_names, zip(_unpack_small(pd, pshapes), _unpack_small(pnm, pshapes),
                                                  _unpack_small(pnv, pshapes)))}
    big = (("w_mem_kv", (w_mem_kv, m_w_mem_kv, v_w_mem_kv)), ("w_up", (w_up, m_w_up, v_w_up)),
           ("w_out", (w_out, m_w_out, v_w_out)))
    first = [_sum_adamw(parts1[1 + i], w, m, v, 1, None, ch_token, "adamw_%s_l1" % name)
             for i, (name, (w, m, v)) in enumerate(big)]
    _, parts0_rest = _spread_wait(*rest0[:4], range(3), first[-1][0], "exchange_grads_l0_rest_wait")
    for i, (name, (w, m, v)) in enumerate(big):
        g, d, nm, nv = _sum_adamw(parts0_rest[i], w, m, v, 0, first[i], None, "adamw_%s_l0" % name)
        grads[name], upd[name] = g, (d, nm, nv)
    _, (parts0_w_in,) = _spread_wait(ch_send, ch_recv, ch_src, ch_land, [0], upd["w_out"][0], "exchange_chips_l0_wait")
    w_in_t, m_w_in_t, v_w_in_t = (jnp.transpose(t, (2, 0, 1)) for t in (w_in, m_w_in, v_w_in))
    g, d, nm, nv = (jnp.transpose(t, (1, 2, 0)) for t in
                    _sum_adamw_t([parts0_w_in, parts1[0]], w_in_t, m_w_in_t, v_w_in_t, "adamw_w_in"))
    grads["w_in"], upd["w_in"] = g, (d, nm, nv)

    order = ("norm_pre", "norm_post", "norm_mem", "w_in", "conv_w", "a_log", "dt_bias", "dn_norm",
             "gm_norm", "spatial_w", "spatial_b", "sinks", "w_mem_kv", "w_up", "w_out")
    return (loss_tot, dx[None], *[grads[n] for n in order], *[upd[n][0] for n in order],
            *[upd[n][1] for n in order], *[upd[n][2] for n in order])
```

```python
import functools
import math

import jax
import jax.numpy as jnp
from jax import lax
from jax.experimental import pallas as pl
from jax.experimental.pallas import tpu as pltpu

MESH = pl.DeviceIdType.MESH
N_DEV = 8

D_MODEL = 1024
DEPTH = 2
N_BRANCH = 4
BRANCH_W = 512
DN_HEADS = 4
CONV_W = 4
GM_GROUPS = 4
SW_HEADS = 8
EPS = 1e-6
NEG_INF = -1e30

D_IN = 9992
W_IN_SHARD = D_IN // N_DEV
W_IN_SHARD_PAD = 1280
D_IN_AL = 10752
DN_W, SW_W, GM_W, XM_W = 2560, 1536, 1536, 1024

ADAM_LR = 0.001
ADAM_B1 = 0.9
ADAM_B2 = 0.999
ADAM_EPS = 1e-08
ADAM_WD = 0.01
ADAM_STEP = 10

VMEM_LIMIT = 56 * 1024 * 1024

BF = jnp.bfloat16
F32 = jnp.float32
DN_C = 128
DN_D = 128
HALO = 8
BLK = 128
Y_DTYPE = BF


def _my_place():
    return lax.axis_index("x"), lax.axis_index("y"), lax.axis_index("c")


_ANY = pl.BlockSpec(memory_space=pl.ANY)


def _all_gather_slots(parts, name):
    n = len(parts)

    def body(*refs):
        p_refs, out_refs = refs[:n], refs[n:2 * n]
        send_sems, recv_sems, local_sems = refs[2 * n:]
        x, y, c = _my_place()
        me, sibling = (x, y, c), (x, y, 1 - c)
        chips = [(1 - x, y), (x, 1 - y), (1 - x, 1 - y)]

        def copy(a, k, block, to, src=None):
            px, py, pc = block
            slot = out_refs[a].at[4 * px + 2 * py + pc]
            return pltpu.make_async_remote_copy(
                src_ref=slot if src is None else src, dst_ref=slot,
                send_sem=send_sems.at[7 * a + k], recv_sem=recv_sems.at[7 * a + k],
                device_id=to, device_id_type=MESH)

        mine = [pltpu.make_async_copy(p_refs[a], out_refs[a].at[4 * x + 2 * y + c], local_sems.at[a])
                for a in range(n)]
        for cp in mine:
            cp.start()
        first = []
        for a in range(n):
            first.append(copy(a, 0, me, sibling, src=p_refs[a]))
            first += [copy(a, 1 + j, me, (*chip, c), src=p_refs[a]) for j, chip in enumerate(chips)]
        for cp in first:
            cp.start()
        passed = []
        for j, chip in enumerate(chips):
            for a in range(n):
                copy(a, 1 + j, (*chip, c), me).wait_recv()
                fwd = copy(a, 4 + j, (*chip, c), sibling)
                fwd.start()
                passed.append(fwd)
        for a in range(n):
            copy(a, 0, sibling, me).wait_recv()
            for j, chip in enumerate(chips):
                copy(a, 4 + j, (*chip, 1 - c), me).wait_recv()
        for cp in first + passed:
            cp.wait_send()
        for cp in mine:
            cp.wait()

    return pl.pallas_call(
        body, name=name,
        out_shape=[jax.ShapeDtypeStruct((N_DEV,) + p.shape, p.dtype) for p in parts],
        in_specs=[_ANY] * n, out_specs=[_ANY] * n,
        scratch_shapes=[pltpu.SemaphoreType.DMA((7 * n,)), pltpu.SemaphoreType.DMA((7 * n,)),
                        pltpu.SemaphoreType.DMA((n,))],
    )(*parts)


def _exchange_sibling(parts, name):
    n = len(parts)

    def body(*refs):
        g_refs, out_refs = refs[:n], refs[n:2 * n]
        send_sems, recv_sems = refs[2 * n:]
        x, y, c = _my_place()
        copies = [pltpu.make_async_remote_copy(
            src_ref=g_refs[a].at[:, 1 - c], dst_ref=out_refs[a],
            send_sem=send_sems.at[a], recv_sem=recv_sems.at[a],
            device_id=(x, y, 1 - c), device_id_type=MESH) for a in range(n)]
        for cp in copies:
            cp.start()
        for cp in copies:
            cp.wait()

    return pl.pallas_call(
        body, name=name,
        out_shape=[jax.ShapeDtypeStruct((4,) + g.shape[2:], g.dtype) for g in parts],
        in_specs=[_ANY] * n, out_specs=[_ANY] * n,
        scratch_shapes=[pltpu.SemaphoreType.DMA((n,)), pltpu.SemaphoreType.DMA((n,))],
    )(*parts)


_HBM = pl.BlockSpec(memory_space=pltpu.HBM)
_SEM = pl.BlockSpec(memory_space=pltpu.SEMAPHORE)
_EFFECT = pltpu.SideEffectType.DATAFLOW_SIDE_EFFECTING


def _peer(x, y, c, k):
    return (1 - x if (k >> 2) & 1 else x, 1 - y if (k >> 1) & 1 else y, 1 - c if k & 1 else c)


def _spread_start(srcs, mode, name, after=None):
    n = len(srcs)
    tail = [] if after is None else [after]
    lands = [lax.empty((N_DEV,) + s.shape if mode == "gather" else s.shape, s.dtype) for s in srcs]
    peers = range(0, N_DEV, 2) if mode == "chips" else range(N_DEV)

    def body(*refs):
        src_refs, land_refs = refs[:n], refs[n:2 * n]
        send_sems, recv_sems = refs[2 * n + len(tail):2 * n + len(tail) + 2]
        token = refs[-1]
        x, y, c = _my_place()
        for a in range(n):
            for k in peers:
                px, py, pc = _peer(x, y, c, k)
                if mode == "chips":
                    src, mine = src_refs[a].at[2 * px + py], 2 * x + y
                else:
                    src = src_refs[a].at[4 * px + 2 * py + pc] if mode == "scatter" else src_refs[a]
                    mine = 4 * x + 2 * y + c
                pltpu.make_async_remote_copy(
                    src_ref=src, dst_ref=land_refs[a].at[mine],
                    send_sem=send_sems.at[a], recv_sem=recv_sems.at[a],
                    device_id=(px, py, pc), device_id_type=MESH).start()
        token[...] = jnp.zeros_like(token)

    out = pl.pallas_call(
        body, name=name,
        out_shape=[pltpu.SemaphoreType.DMA((n,)), pltpu.SemaphoreType.DMA((n,))]
        + [pltpu.HBM(s.shape, s.dtype) for s in srcs] + [pltpu.HBM(l.shape, l.dtype) for l in lands]
        + [jax.ShapeDtypeStruct((8, 128), F32)],
        in_specs=[_HBM] * (2 * n) + [_ANY] * len(tail),
        out_specs=[_SEM, _SEM] + [_HBM] * (2 * n) + [pl.BlockSpec(memory_space=pltpu.VMEM)],
        input_output_aliases={i: 2 + i for i in range(2 * n)},
        compiler_params=pltpu.CompilerParams(has_side_effects=_EFFECT),
    )(*[pltpu.with_memory_space_constraint(s, pltpu.HBM) for s in srcs],
      *[pltpu.with_memory_space_constraint(l, pltpu.HBM) for l in lands], *tail)
    return out[0], out[1], out[2:2 + n], out[2 + n:2 + 2 * n], out[-1]


def _spread_wait(send_sems, recv_sems, srcs, lands, which, after, name):
    n = len(srcs)

    def body(*refs):
        land_refs = refs[n:2 * n]
        send_sems, recv_sems = refs[2 * n:2 * n + 2]
        x, y, c = _my_place()
        for a in which:
            whole = pltpu.make_async_remote_copy(
                src_ref=land_refs[a], dst_ref=land_refs[a],
                send_sem=send_sems.at[a], recv_sem=recv_sems.at[a],
                device_id=(x, y, c), device_id_type=MESH)
            whole.wait_send()
            whole.wait_recv()

    out = pl.pallas_call(
        body, name=name,
        out_shape=[pltpu.HBM(s.shape, s.dtype) for s in srcs] + [pltpu.HBM(l.shape, l.dtype) for l in lands],
        in_specs=[_HBM] * (2 * n) + [_SEM, _SEM, _ANY],
        out_specs=[_HBM] * (2 * n),
        input_output_aliases={i: i for i in range(2 * n)},
        compiler_params=pltpu.CompilerParams(has_side_effects=_EFFECT),
    )(*srcs, *lands, send_sems, recv_sems, after)
    return out[:n], out[n:]


def _sum_slots(parts, name):
    def body(p_ref, o_ref):
        acc = p_ref[0]
        for s in range(1, parts.shape[0]):
            acc = acc + p_ref[s]
        o_ref[...] = acc

    return pl.pallas_call(body, name=name, out_shape=jax.ShapeDtypeStruct(parts.shape[1:], parts.dtype))(parts)


def _pick(n, pref):
    if n <= pref:
        return n
    t = pref - pref % 128
    while t > 0 and n % t:
        t -= 128
    return t if t > 0 else n


_DIMS = {"nn": (((1,), (0,)), ((), ())),
         "nt": (((1,), (1,)), ((), ())),
         "tn": (((0,), (0,)), ((), ()))}


def _matmul(a, b, mode, out_dtype, tiles, name):
    (m, k) = a.shape
    n = b.shape[1] if mode == "nn" else b.shape[0]
    tm, tn, tk = (_pick(d, t) for d, t in zip((m, n, k), tiles))
    nk = k // tk

    def product(a_ref, b_ref):
        return lax.dot_general(a_ref[...].astype(BF), b_ref[...].astype(BF), _DIMS[mode], preferred_element_type=F32)

    def body_whole_k(a_ref, b_ref, o_ref):
        o_ref[...] = product(a_ref, b_ref).astype(o_ref.dtype)

    def body_split_k(a_ref, b_ref, o_ref, acc_ref):
        kk = pl.program_id(2)

        @pl.when(kk == 0)
        def _():
            acc_ref[...] = jnp.zeros_like(acc_ref)

        acc_ref[...] += product(a_ref, b_ref)

        @pl.when(kk == nk - 1)
        def _():
            o_ref[...] = acc_ref[...].astype(o_ref.dtype)

    b_spec = (pl.BlockSpec((tn, tk), lambda i, j, kk: (j, kk)) if mode == "nt"
              else pl.BlockSpec((tk, tn), lambda i, j, kk: (kk, j)))
    return pl.pallas_call(
        body_whole_k if nk == 1 else body_split_k, name=name,
        out_shape=jax.ShapeDtypeStruct((m, n), out_dtype),
        grid=(m // tm, n // tn, nk),
        in_specs=[pl.BlockSpec((tm, tk), lambda i, j, kk: (i, kk)), b_spec],
        out_specs=pl.BlockSpec((tm, tn), lambda i, j, kk: (i, j)),
        scratch_shapes=[] if nk == 1 else [pltpu.VMEM((tm, tn), F32)],
        compiler_params=pltpu.CompilerParams(
            dimension_semantics=("parallel", "parallel", "arbitrary"),
            vmem_limit_bytes=VMEM_LIMIT),
    )(a, b)


def _rows2d(t, lead):
    return t.reshape(t.shape[:lead] + (math.prod(t.shape[lead:-1]), t.shape[-1]))


def _pair_sum(g, theirs, name):
    g3, t3 = _rows2d(g, 2), _rows2d(theirs, 1)
    _, r, w = t3.shape
    tr = _pick(r, 512)

    def body(g_ref, t_ref, o_ref):
        c = lax.axis_index("c")
        mine = jnp.where(c == 0, g_ref[0, 0], g_ref[0, 1])
        o_ref[0] = (mine.astype(F32) + t_ref[0].astype(F32)).astype(o_ref.dtype)

    out = pl.pallas_call(
        body, name=name,
        out_shape=jax.ShapeDtypeStruct(t3.shape, t3.dtype),
        grid=(4, r // tr),
        in_specs=[pl.BlockSpec((1, 2, tr, w), lambda q, i: (q, 0, i, 0)),
                  pl.BlockSpec((1, tr, w), lambda q, i: (q, i, 0))],
        out_specs=pl.BlockSpec((1, tr, w), lambda q, i: (q, i, 0)),
        compiler_params=pltpu.CompilerParams(dimension_semantics=("parallel", "parallel")),
    )(g3, t3)
    return out.reshape(theirs.shape)


def _adam_update(w, g, m, v):
    c1 = 1.0 - ADAM_B1 ** ADAM_STEP
    c2 = 1.0 - ADAM_B2 ** ADAM_STEP
    nm = ADAM_B1 * m + (1.0 - ADAM_B1) * g
    nv = ADAM_B2 * v + (1.0 - ADAM_B2) * (g * g)
    delta = -ADAM_LR * ((nm / c1) / (jnp.sqrt(nv / c2) + ADAM_EPS) + ADAM_WD * w)
    return delta, nm, nv


def _sum_adamw(parts, w, m, v, layer, carry, after, name):
    shape = w.shape
    cols = shape[-1]
    p3 = _rows2d(parts, 1)
    w3, m3, v3 = (_rows2d(t, 1) for t in (w, m, v))
    rows = w3.shape[1]
    tr = _pick(rows, 128)
    n_parts = p3.shape[0]

    def body(p_ref, w_ref, m_ref, v_ref, *rest):
        g_ref, d_ref, nm_ref, nv_ref = rest[-4:]
        g = p_ref[0, :, :cols].astype(F32)
        for q in range(1, n_parts):
            g = g + p_ref[q, :, :cols].astype(F32)
        d, nm, nv = _adam_update(w_ref[0], g, m_ref[0], v_ref[0])
        g_ref[0] = g
        d_ref[0] = d
        nm_ref[0] = nm
        nv_ref[0] = nv

    spec = pl.BlockSpec((1, tr, cols), lambda i: (layer, i, 0))
    extra = [] if carry is None else [_rows2d(t, 1) for t in carry]
    tail = [] if after is None else [after]
    out = pl.pallas_call(
        body, name=name,
        out_shape=[jax.ShapeDtypeStruct(w3.shape, F32)] * 4,
        grid=(rows // tr,),
        in_specs=[pl.BlockSpec((n_parts, tr, p3.shape[-1]), lambda i: (0, i, 0)), spec, spec, spec] + [_ANY] * len(extra + tail),
        out_specs=[spec] * 4,
        input_output_aliases={4 + i: i for i in range(len(extra))},
        compiler_params=pltpu.CompilerParams(dimension_semantics=("parallel",)),
    )(p3, w3, m3, v3, *extra, *tail)
    return tuple(t.reshape(shape) for t in out)


def _sum_adamw_t(parts, w, m, v, name):
    rows = parts[0].shape[2]
    tr = 128
    assert rows % tr == 0 and rows >= w.shape[0]

    def body(*refs):
        p_refs, (w_ref, m_ref, v_ref), (g_ref, d_ref, nm_ref, nv_ref) = refs[:DEPTH], refs[DEPTH:DEPTH + 3], refs[DEPTH + 3:]
        for l in range(DEPTH):
            g = p_refs[l][0].astype(F32)
            for q in range(1, p_refs[l].shape[0]):
                g = g + p_refs[l][q].astype(F32)
            g = g.T
            d, nm, nv = _adam_update(w_ref[:, l, :], g, m_ref[:, l, :], v_ref[:, l, :])
            g_ref[:, l, :] = g
            d_ref[:, l, :] = d
            nm_ref[:, l, :] = nm
            nv_ref[:, l, :] = nv

    spec = pl.BlockSpec((tr,) + w.shape[1:], lambda i: (i, 0, 0))
    return pl.pallas_call(
        body, name=name,
        out_shape=[jax.ShapeDtypeStruct(w.shape, F32)] * 4,
        grid=(rows // tr,),
        in_specs=[pl.BlockSpec((p.shape[0], p.shape[1], tr), lambda i: (0, 0, i)) for p in parts] + [spec] * 3,
        out_specs=[spec] * 4,
        compiler_params=pltpu.CompilerParams(dimension_semantics=("parallel",)),
    )(*parts, w, m, v)


def _adamw(w, g, m, v, name):
    rows, cols = w.shape
    tr = _pick(rows, 128)

    def body(w_ref, g_ref, m_ref, v_ref, d_ref, nm_ref, nv_ref):
        d, nm, nv = _adam_update(w_ref[...], g_ref[...], m_ref[...], v_ref[...])
        d_ref[...] = d
        nm_ref[...] = nm
        nv_ref[...] = nv

    spec = pl.BlockSpec((tr, cols), lambda i: (i, 0))
    return pl.pallas_call(
        body, name=name,
        out_shape=[jax.ShapeDtypeStruct((rows, cols), F32)] * 3,
        grid=(rows // tr,),
        in_specs=[spec] * 4, out_specs=[spec] * 3,
        compiler_params=pltpu.CompilerParams(dimension_semantics=("parallel",)),
    )(w, g, m, v)


_VJP = {"nn": (("nt", "gb"), ("tn", "ag")),
        "nt": (("nn", "gb"), ("tn", "ga")),
        "tn": (("nt", "bg"), ("nn", "ag"))}


def _make_dot(cast, precision):
    def raw(mode, a, b):
        return lax.dot_general(cast(a), cast(b), _DIMS[mode], precision=precision,
                               preferred_element_type=F32)

    @functools.partial(jax.custom_vjp, nondiff_argnums=(0,))
    def dot(mode, a, b):
        return raw(mode, a, b)

    def fwd(mode, a, b):
        return raw(mode, a, b), (a, b)

    def bwd(mode, res, g):
        a, b = res
        pick = {"a": a, "b": b, "g": g}
        (ma, ta), (mb, tb) = _VJP[mode]
        return dot(ma, pick[ta[0]], pick[ta[1]]), dot(mb, pick[tb[0]], pick[tb[1]])

    dot.defvjp(fwd, bwd)
    return dot


bdot = _make_dot(lambda t: t.astype(BF), None)
hdot = _make_dot(lambda t: t, lax.Precision.HIGHEST)


def _xdot(mode, a, b):
    return lax.dot_general(a, b, _DIMS[mode], precision=lax.Precision.HIGH, preferred_element_type=F32)


def _unit_lower_inverse(Ls):
    n = Ls[0].shape[0]
    batched = (((2,), (1,)), ((0,), (0,)))
    mm = lambda a, b: lax.dot_general(a, b, batched, precision=lax.Precision.HIGH, preferred_element_type=F32)
    eye = (lax.broadcasted_iota(jnp.int32, (n, n), 0) == lax.broadcasted_iota(jnp.int32, (n, n), 1)).astype(F32)
    p = jnp.stack(Ls)
    t_inv = eye[None] - p
    for _ in range(6):
        p = mm(p, p)
        t_inv = t_inv + mm(t_inv, p)
    return [t_inv[h] for h in range(len(Ls))]


@jax.custom_vjp
def _tri_solve(L, rhs, t_inv):
    return _xdot("nn", t_inv, rhs)


def _tri_solve_fwd(L, rhs, t_inv):
    sol = _xdot("nn", t_inv, rhs)
    return sol, (t_inv, sol)


def _tri_solve_bwd(res, dsol):
    t_inv, sol = res
    drhs = _xdot("tn", t_inv, dsol)
    return -_xdot("nt", drhs, sol), drhs, jnp.zeros_like(t_inv)


_tri_solve.defvjp(_tri_solve_fwd, _tri_solve_bwd)


def _sigmoid(x):
    return 1.0 / (1.0 + jnp.exp(-x))


def _softplus(x):
    return jnp.maximum(x, 0.0) + jnp.log(1.0 + jnp.exp(-jnp.abs(x)))


def _dn_chunk(S, xs, ba, z, cw, al, dt, dn, t_saved=None):
    return _dn_core(S, _conv_taps(xs, cw), ba, z, al, dt, dn, t_saved)


def _conv_taps(xs, cw):
    return xs[0] * cw[0] + xs[1] * cw[1] + xs[2] * cw[2] + xs[3] * cw[3]


def _dn_core(S, pre, ba, z, al, dt, dn, t_saved=None):
    C = DN_C
    qkv = pre * _sigmoid(pre)
    lane = lax.broadcasted_iota(jnp.int32, (1, 128), 1)
    sub = lax.broadcasted_iota(jnp.int32, (C, 1), 0)
    row_i = lax.broadcasted_iota(jnp.int32, (C, C), 0)
    col_i = lax.broadcasted_iota(jnp.int32, (C, C), 1)
    strict = row_i > col_i
    incl = row_i >= col_i
    g_all = jnp.where((lane >= 4) & (lane < 8), -jnp.exp(al) * _softplus(ba + dt), 0.0)
    gc_all = hdot("nn", incl.astype(F32), g_all)
    gc_all_t = gc_all.T
    beta_all = _sigmoid(ba)
    glast_all = jnp.sum(jnp.where(sub == C - 1, gc_all, 0.0), axis=0, keepdims=True)
    heads = []
    for h in range(DN_HEADS):
        q = qkv[:, 128 * h:128 * (h + 1)]
        k = qkv[:, 512 + 128 * h:512 + 128 * (h + 1)]
        v = qkv[:, 1024 + 128 * h:1024 + 128 * (h + 1)]
        q = q * lax.rsqrt(jnp.sum(q * q, axis=1, keepdims=True) + EPS) * (DN_D ** -0.5)
        k = k * lax.rsqrt(jnp.sum(k * k, axis=1, keepdims=True) + EPS)
        beta = jnp.sum(jnp.where(lane == h, beta_all, 0.0), axis=1, keepdims=True)
        gc = jnp.sum(jnp.where(lane == 4 + h, gc_all, 0.0), axis=1, keepdims=True)
        gc_row = jnp.sum(jnp.where(sub == 4 + h, gc_all_t, 0.0), axis=0, keepdims=True)
        g_last = jnp.sum(jnp.where(lane == 4 + h, glast_all, 0.0), axis=1, keepdims=True)
        diff = gc - gc_row
        kb = k * beta
        L = jnp.where(strict, bdot("nt", kb, k) * jnp.exp(jnp.where(strict, diff, 0.0)), 0.0)
        heads.append((q, k, v, beta, gc, g_last, diff, kb, L))
    t_invs = _unit_lower_inverse([hd[-1] for hd in heads]) if t_saved is None else t_saved
    ys, s_new = [], []
    for h, (q, k, v, beta, gc, g_last, diff, kb, L) in enumerate(heads):
        sol = _tri_solve(L, jnp.concatenate([v * beta, kb * jnp.exp(gc)], axis=1), t_invs[h])
        u, w = sol[:, :DN_D], sol[:, DN_D:]
        a_qk = jnp.where(incl, bdot("nt", q, k) * jnp.exp(jnp.where(incl, diff, 0.0)), 0.0)
        qg = q * jnp.exp(gc)
        kd = k * jnp.exp(g_last - gc)
        v_new = u - bdot("nn", w, S[h])
        o = bdot("nn", qg, S[h]) + bdot("nn", a_qk, v_new)
        s_new.append(S[h] * jnp.exp(g_last) + bdot("tn", kd, v_new))
        o = o * lax.rsqrt(jnp.mean(o * o, axis=1, keepdims=True) + EPS) * dn
        zh = z[:, 128 * h:128 * (h + 1)]
        ys.append(o * (zh * _sigmoid(zh)))
    return jnp.concatenate(ys, axis=1), tuple(s_new), tuple(t_invs)


def _load_shifted(xbuf, x_ref, halo_ref, first):
    xbuf[0:HALO, :] = jnp.where(first, 0.0, halo_ref[:, 0:1536])
    xbuf[HALO:HALO + DN_C, :] = x_ref[:, 0:1536]
    return [xbuf[HALO - 3 + k:HALO - 3 + k + DN_C, :] for k in range(4)]


def dn_forward(cols, jblk, cw, al, dt, dn, name):
    T = cols.shape[0]
    n = T // DN_C

    def body(x_ref, halo_ref, cw_ref, al_ref, dt_ref, dn_ref, y_ref, ss_ref, ts_ref, s_scr, xbuf):
        i = pl.program_id(0)

        @pl.when(i == 0)
        def _():
            s_scr[...] = jnp.zeros_like(s_scr)

        xs = _load_shifted(xbuf, x_ref, halo_ref, i == 0)
        ss_ref[0] = s_scr[...]
        S = [s_scr[h] for h in range(DN_HEADS)]
        cws = [cw_ref[k:k + 1, :] for k in range(4)]
        y, s_new, t_invs = _dn_chunk(S, xs, x_ref[:, 2048:2176], x_ref[:, 1536:2048], cws,
                                     al_ref[...], dt_ref[...], dn_ref[...])
        y_ref[...] = y.astype(y_ref.dtype)
        for h in range(DN_HEADS):
            s_scr[h] = s_new[h]
            ts_ref[0, h] = t_invs[h]

    per = DN_C // HALO
    full = lambda shape: pl.BlockSpec(shape, lambda i: (0,) * len(shape))
    return pl.pallas_call(
        body, name=name,
        out_shape=[jax.ShapeDtypeStruct((T, 512), Y_DTYPE),
                   jax.ShapeDtypeStruct((n, DN_HEADS, DN_D, DN_D), F32),
                   jax.ShapeDtypeStruct((n, DN_HEADS, DN_D, DN_D), F32)],
        grid=(n,),
        in_specs=[pl.BlockSpec((DN_C, DN_W), lambda i: (i, jblk)),
                  pl.BlockSpec((HALO, DN_W), lambda i: (jnp.maximum(i * per - 1, 0), jblk)),
                  full((4, 1536)), full((1, 128)), full((1, 128)), full((1, 128))],
        out_specs=[pl.BlockSpec((DN_C, 512), lambda i: (i, 0)),
                   pl.BlockSpec((1, DN_HEADS, DN_D, DN_D), lambda i: (i, 0, 0, 0)),
                   pl.BlockSpec((1, DN_HEADS, DN_D, DN_D), lambda i: (i, 0, 0, 0))],
        scratch_shapes=[pltpu.VMEM((DN_HEADS, DN_D, DN_D), F32), pltpu.VMEM((HALO + DN_C, 1536), F32)],
        compiler_params=pltpu.CompilerParams(dimension_semantics=("arbitrary",)),
    )(cols, cols, cw, al, dt, dn)


def dn_backward(cols, jblk, cw, al, dt, dn, ss, ts, dy, dcols, name):
    T = cols.shape[0]
    n = T // DN_C

    def body(x_ref, halo_ref, cw_ref, al_ref, dt_ref, dn_ref, ss_ref, ts_ref, dy_ref, dcols_in,
             dx_ref, dcw_ref, dal_ref, ddt_ref, ddn_ref, ds_scr, xbuf, dbuf, carry):
        i = pl.program_id(0)

        @pl.when(i == 0)
        def _():
            ds_scr[...] = jnp.zeros_like(ds_scr)
            carry[...] = jnp.zeros_like(carry)
            dcw_ref[...] = jnp.zeros_like(dcw_ref)
            dal_ref[...] = jnp.zeros_like(dal_ref)
            ddt_ref[...] = jnp.zeros_like(ddt_ref)
            ddn_ref[...] = jnp.zeros_like(ddn_ref)

        xs = _load_shifted(xbuf, x_ref, halo_ref, i == n - 1)
        S = [ss_ref[0, h] for h in range(DN_HEADS)]
        cws = [cw_ref[k:k + 1, :] for k in range(4)]

        t_saved = [ts_ref[0, h] for h in range(DN_HEADS)]

        def f(S, pre, ba, z, al, dt, dn):
            return _dn_core(S, pre, ba, z, al, dt, dn, t_saved)[:2]

        _, vjp = jax.vjp(f, S, _conv_taps(xs, cws), x_ref[:, 2048:2176], x_ref[:, 1536:2048],
                         al_ref[...], dt_ref[...], dn_ref[...])
        dS, dpre, dba, dz, dal, ddt, ddn = vjp((dy_ref[...], tuple(ds_scr[h] for h in range(DN_HEADS))))
        for h in range(DN_HEADS):
            ds_scr[h] = dS[h]
        dbuf[0:DN_C, :] = dpre
        dbuf[DN_C:DN_C + HALO, :] = carry[...]
        dx = (cws[3] * dpre + cws[2] * dbuf[1:1 + DN_C, :] + cws[1] * dbuf[2:2 + DN_C, :]
              + cws[0] * dbuf[3:3 + DN_C, :])
        dx_ref[...] = jnp.concatenate([dx, dz, dba, jnp.zeros((DN_C, DN_W - 2176), F32)], axis=1).astype(dx_ref.dtype)
        carry[...] = dpre[0:HALO, :]
        for k in range(4):
            dcw_ref[k:k + 1, :] += jnp.sum(dpre * xs[k], axis=0, keepdims=True)
        dal_ref[...] += dal
        ddt_ref[...] += ddt
        ddn_ref[...] += ddn

    per = DN_C // HALO
    rev = lambda i: n - 1 - i
    full = lambda shape: pl.BlockSpec(shape, lambda i: (0,) * len(shape))
    return pl.pallas_call(
        body, name=name,
        out_shape=[jax.ShapeDtypeStruct(dcols.shape, dcols.dtype),jax.ShapeDtypeStruct((4, 1536), F32),
                   jax.ShapeDtypeStruct((1, 128), F32), jax.ShapeDtypeStruct((1, 128), F32),
                   jax.ShapeDtypeStruct((1, 128), F32)],
        grid=(n,),
        in_specs=[pl.BlockSpec((DN_C, DN_W), lambda i: (rev(i), jblk)),
                  pl.BlockSpec((HALO, DN_W), lambda i: (jnp.maximum(rev(i) * per - 1, 0), jblk)),
                  full((4, 1536)), full((1, 128)), full((1, 128)), full((1, 128)),
                  pl.BlockSpec((1, DN_HEADS, DN_D, DN_D), lambda i: (rev(i), 0, 0, 0)),
                  pl.BlockSpec((1, DN_HEADS, DN_D, DN_D), lambda i: (rev(i), 0, 0, 0)),
                  pl.BlockSpec((DN_C, 512), lambda i: (rev(i), 0)), _ANY],
        out_specs=[pl.BlockSpec((DN_C, DN_W), lambda i: (rev(i), jblk)),
                   full((4, 1536)), full((1, 128)), full((1, 128)), full((1, 128))],
        scratch_shapes=[pltpu.VMEM((DN_HEADS, DN_D, DN_D), F32), pltpu.VMEM((HALO + DN_C, 1536), F32),
                        pltpu.VMEM((HALO + DN_C, 1536), F32), pltpu.VMEM((HALO, 1536), F32)],
        input_output_aliases={9: 0},
        compiler_params=pltpu.CompilerParams(dimension_semantics=("arbitrary",)),
    )(cols, cols, cw, al, dt, dn, ss, ts, dy, dcols)


def _full(shape):
    return pl.BlockSpec(shape, lambda i: (0,) * len(shape))


def _silu(x):
    return x * _sigmoid(x)


def _gelu(x):
    return 0.5 * x * (1.0 + jnp.tanh(0.7978845608028654 * (x + 0.044715 * (x * x * x))))


def _lane_col(mat, idx):
    lane = lax.broadcasted_iota(jnp.int32, (1, mat.shape[1]), 1)
    return jnp.sum(jnp.where(lane == idx, mat, 0.0), axis=1, keepdims=True)


def _gm_chunk(uv, z, gain, ws, bt):
    g = _gelu(uv)
    u, v = g[:, :512], g[:, 512:]
    v = v * lax.rsqrt(jnp.mean(v * v, axis=1, keepdims=True) + EPS) * gain
    row_i = lax.broadcasted_iota(jnp.int32, (BLK, BLK), 0)
    col_i = lax.broadcasted_iota(jnp.int32, (BLK, BLK), 1)
    causal = row_i >= col_i
    ss = []
    for grp in range(4):
        wg = jnp.where(causal, ws[grp], 0.0)
        ss.append(bdot("nn", wg, v[:, BLK * grp:BLK * (grp + 1)]) + _lane_col(bt, grp))
    return u * jnp.concatenate(ss, axis=1) * _silu(z)


def gm_forward(cols, jblk, gain, ws, bt, name):
    T = cols.shape[0]

    def body(x_ref, gain_ref, ws_ref, bt_ref, y_ref):
        y_ref[...] = _gm_chunk(x_ref[:, 0:1024], x_ref[:, 1024:1536], gain_ref[...],
                               [ws_ref[g] for g in range(4)], bt_ref[...]).astype(y_ref.dtype)

    return pl.pallas_call(
        body, name=name, out_shape=jax.ShapeDtypeStruct((T, 512), Y_DTYPE), grid=(T // BLK,),
        in_specs=[pl.BlockSpec((BLK, GM_W), lambda i: (i, jblk)),
                  _full((1, 512)), _full((4, BLK, BLK)), _full((BLK, BLK))],
        out_specs=pl.BlockSpec((BLK, 512), lambda i: (i, 0)),
        compiler_params=pltpu.CompilerParams(dimension_semantics=("parallel",)),
    )(cols, gain, ws, bt)


def gm_backward(cols, jblk, gain, ws, bt, dy, dcols, name):
    T = cols.shape[0]

    def body(x_ref, gain_ref, ws_ref, bt_ref, dy_ref, dcols_in, dx_ref, dgain_ref, dws_ref, dbt_ref):
        @pl.when(pl.program_id(0) == 0)
        def _():
            dgain_ref[...] = jnp.zeros_like(dgain_ref)
            dws_ref[...] = jnp.zeros_like(dws_ref)
            dbt_ref[...] = jnp.zeros_like(dbt_ref)

        _, vjp = jax.vjp(_gm_chunk, x_ref[:, 0:1024], x_ref[:, 1024:1536], gain_ref[...],
                         [ws_ref[g] for g in range(4)], bt_ref[...])
        duv, dz, dgain, dws, dbt = vjp(dy_ref[...])
        dx_ref[...] = jnp.concatenate([duv, dz], axis=1).astype(dx_ref.dtype)
        dgain_ref[...] += dgain
        for g in range(4):
            dws_ref[g] += dws[g]
        dbt_ref[...] += dbt

    return pl.pallas_call(
        body, name=name,
        out_shape=[jax.ShapeDtypeStruct(dcols.shape, dcols.dtype),jax.ShapeDtypeStruct((1, 512), F32),
                   jax.ShapeDtypeStruct((4, BLK, BLK), F32), jax.ShapeDtypeStruct((BLK, BLK), F32)],
        grid=(T // BLK,),
        in_specs=[pl.BlockSpec((BLK, GM_W), lambda i: (i, jblk)),
                  _full((1, 512)), _full((4, BLK, BLK)), _full((BLK, BLK)),
                  pl.BlockSpec((BLK, 512), lambda i: (i, 0)), _ANY],
        out_specs=[pl.BlockSpec((BLK, GM_W), lambda i: (i, jblk)),
                   _full((1, 512)), _full((4, BLK, BLK)), _full((BLK, BLK))],
        input_output_aliases={5: 0},
        compiler_params=pltpu.CompilerParams(dimension_semantics=("arbitrary",)),
    )(cols, gain, ws, bt, dy, dcols)


def _sw_block(first, q, kp, kc, vp, vc, z, sinks):
    P = BLK
    lane = lax.broadcasted_iota(jnp.int32, (1, 128), 1)
    r = lax.broadcasted_iota(jnp.int32, (128, 128), 0)
    c = lax.broadcasted_iota(jnp.int32, (128, 128), 1)
    swap = (c == (r + 64) % 128).astype(F32)
    k2 = jnp.concatenate([kp, kc], axis=0)
    v2 = jnp.concatenate([vp, vc], axis=0)
    k2s = bdot("nn", k2, swap)
    v2s = bdot("nn", v2, swap)
    qi = lax.broadcasted_iota(jnp.int32, (P, 2 * P), 0)
    kj = lax.broadcasted_iota(jnp.int32, (P, 2 * P), 1)
    dist = qi + P - kj
    valid = (dist >= 0) & (dist < P) & ((kj >= P) | jnp.logical_not(first))
    outs = []
    for j in range(4):
        acc = jnp.zeros((P, 128), F32)
        for half in range(2):
            h = 2 * j + half
            kv = h // 4
            in_half = (lane >= 64 * half) & (lane < 64 * half + 64)
            qh = jnp.where(in_half, q[:, 128 * j:128 * (j + 1)], 0.0)
            same = (half == kv)
            s = bdot("nt", qh, k2 if same else k2s) * (64 ** -0.5)
            s = jnp.where(valid, s, NEG_INF)
            sink = _lane_col(sinks, h)
            m = lax.stop_gradient(jnp.maximum(jnp.max(s, axis=1, keepdims=True), sink))
            e = jnp.exp(s - m)
            p = e / (jnp.sum(e, axis=1, keepdims=True) + jnp.exp(sink - m))
            o = bdot("nn", p, v2 if same else v2s)
            acc = acc + jnp.where(in_half, o, 0.0)
        outs.append(acc)
    return jnp.concatenate(outs, axis=1) * _silu(z)


def _sw_specs(jblk, idx):
    prev = lambda i: jnp.maximum(idx(i) - 1, 0)
    jk = (jblk * SW_W + 1024) // 128
    return [pl.BlockSpec((BLK, SW_W), lambda i: (idx(i), jblk)),
            pl.BlockSpec((BLK, 128), lambda i: (prev(i), jk)),
            pl.BlockSpec((BLK, 128), lambda i: (prev(i), jk + 1)), _full((1, 128))]


def sw_forward(cols, jblk, sinks, name):
    T = cols.shape[0]

    def body(x_ref, kp_ref, vp_ref, s_ref, y_ref):
        y_ref[...] = _sw_block(pl.program_id(0) == 0, x_ref[:, 0:512], kp_ref[...], x_ref[:, 1024:1152],
                               vp_ref[...], x_ref[:, 1152:1280], x_ref[:, 512:1024], s_ref[...]).astype(y_ref.dtype)

    return pl.pallas_call(
        body, name=name, out_shape=jax.ShapeDtypeStruct((T, 512), Y_DTYPE), grid=(T // BLK,),
        in_specs=_sw_specs(jblk, lambda i: i),
        out_specs=pl.BlockSpec((BLK, 512), lambda i: (i, 0)),
        compiler_params=pltpu.CompilerParams(dimension_semantics=("parallel",)),
    )(cols, cols, cols, sinks)


def sw_backward(cols, jblk, sinks, dy, dcols, name):
    T = cols.shape[0]
    n = T // BLK
    rev = lambda i: n - 1 - i

    def body(x_ref, kp_ref, vp_ref, s_ref, dy_ref, dcols_in, dx_ref, ds_ref, kcarry, vcarry):
        i = pl.program_id(0)

        @pl.when(i == 0)
        def _():
            kcarry[...] = jnp.zeros_like(kcarry)
            vcarry[...] = jnp.zeros_like(vcarry)
            ds_ref[...] = jnp.zeros_like(ds_ref)

        f = functools.partial(_sw_block, i == n - 1)
        _, vjp = jax.vjp(f, x_ref[:, 0:512], kp_ref[...], x_ref[:, 1024:1152], vp_ref[...], x_ref[:, 1152:1280],
                         x_ref[:, 512:1024], s_ref[...])
        dq, dkp, dkc, dvp, dvc, dz, dsk = vjp(dy_ref[...])
        dx_ref[...] = jnp.concatenate([dq, dz, dkc + kcarry[...], dvc + vcarry[...],
                                       jnp.zeros((BLK, SW_W - 1280), F32)], axis=1).astype(dx_ref.dtype)
        kcarry[...] = dkp
        vcarry[...] = dvp
        ds_ref[...] += dsk

    return pl.pallas_call(
        body, name=name,
        out_shape=[jax.ShapeDtypeStruct(dcols.shape, dcols.dtype),jax.ShapeDtypeStruct((1, 128), F32)],
        grid=(n,),
        in_specs=_sw_specs(jblk, rev) + [pl.BlockSpec((BLK, 512), lambda i: (rev(i), 0)), _ANY],
        out_specs=[pl.BlockSpec((BLK, SW_W), lambda i: (rev(i), jblk)), _full((1, 128))],
        scratch_shapes=[pltpu.VMEM((BLK, 128), F32), pltpu.VMEM((BLK, 128), F32)],
        input_output_aliases={5: 0},
        compiler_params=pltpu.CompilerParams(dimension_semantics=("arbitrary",)),
    )(cols, cols, cols, sinks, dy, dcols)


XM_TQ = 512


def _xm_block(q, z, mkv):
    outs = []
    for h in range(4):
        s = bdot("nt", q[:, 128 * h:128 * (h + 1)], mkv[:, 128 * h:128 * (h + 1)]) * (128 ** -0.5)
        m = lax.stop_gradient(jnp.max(s, axis=1, keepdims=True))
        e = jnp.exp(s - m)
        p = e / jnp.sum(e, axis=1, keepdims=True)
        outs.append(bdot("nn", p, mkv[:, 512 + 128 * h:512 + 128 * (h + 1)]))
    return jnp.concatenate(outs, axis=1) * _silu(z)


def xm_forward(cols, jblk, mkv, name):
    T = cols.shape[0]

    def body(x_ref, m_ref, y_ref):
        y_ref[...] = _xm_block(x_ref[:, 0:512], x_ref[:, 512:1024], m_ref[...]).astype(y_ref.dtype)

    return pl.pallas_call(
        body, name=name, out_shape=jax.ShapeDtypeStruct((T, 512), Y_DTYPE), grid=(T // XM_TQ,),
        in_specs=[pl.BlockSpec((XM_TQ, XM_W), lambda i: (i, jblk)), _full(mkv.shape)],
        out_specs=pl.BlockSpec((XM_TQ, 512), lambda i: (i, 0)),
        compiler_params=pltpu.CompilerParams(dimension_semantics=("parallel",)),
    )(cols, mkv)


def xm_backward(cols, jblk, mkv, dy, dcols, name):
    T = cols.shape[0]

    def body(x_ref, m_ref, dy_ref, dcols_in, dx_ref, dm_ref):
        @pl.when(pl.program_id(0) == 0)
        def _():
            dm_ref[...] = jnp.zeros_like(dm_ref)

        _, vjp = jax.vjp(_xm_block, x_ref[:, 0:512], x_ref[:, 512:1024], m_ref[...])
        dq, dz, dm = vjp(dy_ref[...])
        dx_ref[...] = jnp.concatenate([dq, dz], axis=1).astype(dx_ref.dtype)
        dm_ref[...] += dm

    return pl.pallas_call(
        body, name=name,
        out_shape=[jax.ShapeDtypeStruct(dcols.shape, dcols.dtype),jax.ShapeDtypeStruct(mkv.shape, F32)],
        grid=(T // XM_TQ,),
        in_specs=[pl.BlockSpec((XM_TQ, XM_W), lambda i: (i, jblk)), _full(mkv.shape),
                  pl.BlockSpec((XM_TQ, 512), lambda i: (i, 0)), _ANY],
        out_specs=[pl.BlockSpec((XM_TQ, XM_W), lambda i: (i, jblk)), _full(mkv.shape)],
        input_output_aliases={3: 0},
        compiler_params=pltpu.CompilerParams(dimension_semantics=("arbitrary",)),
    )(cols, mkv, dy, dcols)


def _rms(x, gain):
    return x * lax.rsqrt(jnp.mean(x * x, axis=1, keepdims=True) + EPS) * gain


def memkv_forward(mem, gain, w, name):
    def body(m_ref, g_ref, w_ref, o_ref):
        o_ref[...] = bdot("nn", _rms(m_ref[...], g_ref[...]), w_ref[...])

    return pl.pallas_call(body, name=name, out_shape=jax.ShapeDtypeStruct(mem.shape, F32),
                          compiler_params=pltpu.CompilerParams(vmem_limit_bytes=VMEM_LIMIT))(mem, gain, w)


def memkv_backward(mem, gain, w, dkv, name):
    def body(m_ref, g_ref, w_ref, d_ref, dg_ref, dw_ref):
        mem_v = m_ref[...]
        _, vjp = jax.vjp(lambda g, ww: bdot("nn", _rms(mem_v, g), ww), g_ref[...], w_ref[...].astype(F32))
        dg, dw = vjp(d_ref[...])
        dg_ref[...] = dg
        dw_ref[...] = dw

    return pl.pallas_call(body, name=name,
                          out_shape=[jax.ShapeDtypeStruct(gain.shape, F32), jax.ShapeDtypeStruct(w.shape, F32)],
                          compiler_params=pltpu.CompilerParams(vmem_limit_bytes=VMEM_LIMIT))(mem, gain, w, dkv)


MG_TB = 256


def _merge_block(ys, gl, wup, wout, gpost):
    merged = None
    for n in range(4):
        t = _sigmoid(gl[:, 1024 * n:1024 * (n + 1)]) * bdot("nn", ys[n], wup[n])
        merged = t if merged is None else merged + t
    out = bdot("nn", merged, wout)
    return _rms(out, gpost)


def merge_forward(ys, cols, jgate, x, wup, wout, gpost, name, next_gain=None, target=None):
    T, D = x.shape
    TB = 256
    n_extra = (next_gain is not None) + (target is not None)

    def body(ya, yb, yc, ym, gl_ref, x_ref, wup_ref, wout_ref, gp_ref, *rest):
        extra, outs = rest[:n_extra], rest[n_extra:]
        upd = _merge_block([ya[...], yb[...], yc[...], ym[...]], gl_ref[...],
                           [wup_ref[n] for n in range(4)], wout_ref[...], gp_ref[...])
        y = x_ref[...] + upd
        outs[0][...] = y
        outs = outs[1:]
        if next_gain is not None:
            h = _rms(y, extra[0][...])
            outs[0][...] = h.astype(BF)
            outs[1][...] = h.T.astype(BF)
            outs = outs[2:]
        if target is not None:
            l_ref, d_ref = outs

            @pl.when(pl.program_id(0) == 0)
            def _():
                l_ref[...] = jnp.zeros_like(l_ref)

            err = y - extra[-1][...]
            d_ref[...] = err * (1.0 / D)
            l_ref[...] += jnp.full(l_ref.shape, 0.5 * jnp.sum(jnp.mean(err * err, axis=1, keepdims=True)), F32)

    yspec = pl.BlockSpec((TB, 512), lambda i: (i, 0))
    xspec = pl.BlockSpec((TB, D), lambda i: (i, 0))
    extra_in, extra_specs = [], []
    out_shape, out_specs = [jax.ShapeDtypeStruct((T, D), F32)], [xspec]
    if next_gain is not None:
        extra_in, extra_specs = extra_in + [next_gain], extra_specs + [_full((1, D))]
        out_shape += [jax.ShapeDtypeStruct((T, D), BF), jax.ShapeDtypeStruct((D, T), BF)]
        out_specs += [xspec, pl.BlockSpec((D, TB), lambda i: (0, i))]
    if target is not None:
        extra_in, extra_specs = extra_in + [target], extra_specs + [xspec]
        out_shape += [jax.ShapeDtypeStruct((1, 128), F32), jax.ShapeDtypeStruct((T, D), F32)]
        out_specs += [_full((1, 128)), xspec]
    return pl.pallas_call(
        body, name=name, out_shape=out_shape, grid=(T // TB,),
        in_specs=[yspec] * 4 + [pl.BlockSpec((TB, 4096), lambda i: (i, jgate)), xspec,
                                _full(wup.shape), _full(wout.shape), _full((1, D))] + extra_specs,
        out_specs=out_specs,
        compiler_params=pltpu.CompilerParams(
            dimension_semantics=("parallel" if target is None else "arbitrary",), vmem_limit_bytes=VMEM_LIMIT),
    )(*ys, cols, x, wup, wout, gpost, *extra_in)


def _token_product(a, b, name):
    (T, m), n = a.shape, b.shape[1]

    def body(a_ref, b_ref, o_ref):
        o_ref[...] = lax.dot_general(a_ref[...].astype(BF), b_ref[...].astype(BF), _DIMS["tn"], preferred_element_type=F32)

    return pl.pallas_call(body, name=name, out_shape=jax.ShapeDtypeStruct((m, n), F32),
                          compiler_params=pltpu.CompilerParams(vmem_limit_bytes=VMEM_LIMIT))(a, b)


def merge_backward(ys, cols, jgate, wup, wout, gpost, dx, name):
    T = dx.shape[0]
    TB = MG_TB

    def body(ya, yb, yc, ym, gl_ref, wup_ref, wout_ref, gp_ref, dx_ref,
             dgl_ref, dya, dyb, dyc, dym, dpa, dpb, dpc, dpm, merged_ref, dout_ref, dgp_ref):
        @pl.when(pl.program_id(0) == 0)
        def _():
            dgp_ref[...] = jnp.zeros_like(dgp_ref)

        y_refs = (ya, yb, yc, ym)
        gates = [_sigmoid(gl_ref[:, 1024 * n:1024 * (n + 1)]) for n in range(4)]
        projs = [bdot("nn", y_refs[n][...], wup_ref[n]) for n in range(4)]
        merged = gates[0] * projs[0] + gates[1] * projs[1] + gates[2] * projs[2] + gates[3] * projs[3]
        out = bdot("nn", merged, wout_ref[...])
        _, vjp = jax.vjp(_rms, out, gp_ref[...])
        dout, dgp = vjp(dx_ref[...])
        dmerged = bdot("nt", dout, wout_ref[...])
        for n, (dy_ref, dp_ref) in enumerate(zip((dya, dyb, dyc, dym), (dpa, dpb, dpc, dpm))):
            dproj = dmerged * gates[n]
            dgl_ref[:, 1024 * n:1024 * (n + 1)] = (dmerged * projs[n] * gates[n] * (1.0 - gates[n])).astype(dgl_ref.dtype)
            dy_ref[...] = bdot("nt", dproj, wup_ref[n])
            dp_ref[...] = dproj.astype(BF)
        merged_ref[...] = merged.astype(BF)
        dout_ref[...] = dout.astype(BF)
        dgp_ref[...] += dgp

    yspec = pl.BlockSpec((TB, 512), lambda i: (i, 0))
    dspec = pl.BlockSpec((TB, 1024), lambda i: (i, 0))
    dcols, dya, dyb, dyc, dym, *dproj, merged, dout, dgp = pl.pallas_call(
        body, name=name,
        out_shape=[jax.ShapeDtypeStruct(cols.shape, BF)] + [jax.ShapeDtypeStruct((T, 512), F32)] * 4 + [
            jax.ShapeDtypeStruct((T, 1024), BF)] * 6 + [jax.ShapeDtypeStruct((1, 1024), F32)],
        grid=(T // TB,),
        in_specs=[yspec] * 4 + [pl.BlockSpec((TB, 4096), lambda i: (i, jgate)),
                                _full(wup.shape), _full(wout.shape), _full((1, 1024)), dspec],
        out_specs=[pl.BlockSpec((TB, 4096), lambda i: (i, jgate))] + [yspec] * 4 + [
            dspec] * 6 + [_full((1, 1024))],
        compiler_params=pltpu.CompilerParams(dimension_semantics=("arbitrary",), vmem_limit_bytes=VMEM_LIMIT),
    )(*ys, cols, wup, wout, gpost, dx)
    dwup = jnp.stack([_token_product(ys[n], dproj[n], "%s_w_up%d" % (name, n)) for n in range(4)])
    dwout = _token_product(merged, dout, name + "_w_out")
    return dcols, dya, dyb, dyc, dym, dwup, dwout, dgp


NB = 256


def prenorm_forward(x, gain, name):
    T, D = x.shape

    def body(x_ref, g_ref, o_ref, ot_ref):
        h = _rms(x_ref[...], g_ref[...])
        o_ref[...] = h.astype(BF)
        ot_ref[...] = h.T.astype(BF)

    return pl.pallas_call(
        body, name=name,
        out_shape=[jax.ShapeDtypeStruct((T, D), BF), jax.ShapeDtypeStruct((D, T), BF)], grid=(T // NB,),
        in_specs=[pl.BlockSpec((NB, D), lambda i: (i, 0)), _full((1, D))],
        out_specs=[pl.BlockSpec((NB, D), lambda i: (i, 0)), pl.BlockSpec((D, NB), lambda i: (0, i))],
        compiler_params=pltpu.CompilerParams(dimension_semantics=("parallel",)),
    )(x, gain)


def prenorm_backward(x, gain, dh, dres, name):
    T = x.shape[0]

    def body(x_ref, g_ref, dh_ref, dr_ref, dx_ref, dg_ref):
        @pl.when(pl.program_id(0) == 0)
        def _():
            dg_ref[...] = jnp.zeros_like(dg_ref)

        _, vjp = jax.vjp(_rms, x_ref[...], g_ref[...])
        dxn, dg = vjp(dh_ref[...])
        dx_ref[...] = dr_ref[...] + dxn
        dg_ref[...] += dg

    spec = pl.BlockSpec((NB, 1024), lambda i: (i, 0))
    return pl.pallas_call(
        body, name=name,
        out_shape=[jax.ShapeDtypeStruct(x.shape, F32), jax.ShapeDtypeStruct((1, 1024), F32)], grid=(T // NB,),
        in_specs=[spec, _full((1, 1024)), spec, spec], out_specs=[spec, _full((1, 1024))],
        compiler_params=pltpu.CompilerParams(dimension_semantics=("arbitrary",)),
    )(x, gain, dh, dres)


JB_GATE, JB_XM, JB_DN, JB_SW, JB_GM = 0, 4, 2, 5, 6
_ALIGNED_PIECES = ((5896, 4096), (4872, 512), (5384, 512), (0, 2048), (2048, 8), 504, (3592, 512), (4360, 512),
                   (4104, 128), (4232, 128), 256, (2056, 1024), (3080, 512))
_NATURAL_FROM_ALIGNED = ((5120, 2048), (7168, 8), (9216, 1024), (10240, 512), (7680, 512), (8704, 128), (8832, 128),
                         (8192, 512), (4096, 512), (4608, 512), (0, 4096))


def _natural_range(slots, start, width):
    out = []
    while width > 0:
        j, i = divmod(start, W_IN_SHARD)
        take = min(width, W_IN_SHARD - i)
        out.append(slots[j, :, i:i + take])
        start, width = start + take, width - take
    return out


def _aligned_w_in(slots, zero=0.0):
    parts = []
    for piece in _ALIGNED_PIECES:
        if isinstance(piece, int):
            parts.append(jnp.full(slots.shape[1:2] + (piece,), zero, slots.dtype))
        else:
            parts += _natural_range(slots, *piece)
    return jnp.concatenate(parts, axis=-1)


def _slots_of_aligned(d_al):
    slots = []
    for s in range(N_DEV):
        lo, hi = s * W_IN_SHARD, (s + 1) * W_IN_SHARD
        parts, nat = [], 0
        for a_start, width in _NATURAL_FROM_ALIGNED:
            b, e = max(lo, nat), min(hi, nat + width)
            if b < e:
                parts.append(d_al[..., a_start + b - nat:a_start + e - nat])
            nat += width
        parts.append(jnp.zeros(d_al.shape[:1] + (W_IN_SHARD_PAD - W_IN_SHARD,), d_al.dtype))
        slots.append(jnp.concatenate(parts, axis=-1))
    return jnp.stack(slots)


SMALL_VEC_W = 1024


def _pack_small(parts):
    rows = []
    for p in parts:
        flat = p.reshape(-1).astype(F32)
        r = -(-flat.shape[0] // SMALL_VEC_W)
        rows.append(jnp.pad(flat, (0, r * SMALL_VEC_W - flat.shape[0])).reshape(r, SMALL_VEC_W))
    vec = jnp.concatenate(rows, axis=0)
    return jnp.pad(vec, ((0, -vec.shape[0] % 8), (0, 0)))


def _unpack_small(vec, shapes):
    out, off = [], 0
    for s in shapes:
        n = math.prod(s)
        r = -(-n // SMALL_VEC_W)
        out.append(vec[off:off + r].reshape(-1)[:n].reshape(s))
        off += r
    return out


def _lanes(vec, at):
    return jnp.zeros((1, 128), F32).at[0, at:at + vec.shape[0]].set(vec)


SMALL_NAMES = ("norm_pre", "norm_post", "norm_mem", "a_log", "dt_bias", "dn_norm", "gm_norm",
               "spatial_w", "spatial_b", "sinks")


def _other_weights(s_mem, s_up, s_out):
    return (s_mem.reshape(D_MODEL, 2 * BRANCH_W),
            jnp.transpose(s_up, (1, 2, 0, 3)).reshape(N_BRANCH, BRANCH_W, D_MODEL), s_out.reshape(D_MODEL, D_MODEL))


def _grad_slots(d_in_al, d_mem, d_up, d_out):
    return [None if d_in_al is None else _slots_of_aligned(d_in_al), d_mem.astype(BF).reshape(N_DEV, 128, 2 * BRANCH_W),
            jnp.transpose(d_up.astype(BF).reshape(N_BRANCH, BRANCH_W, N_DEV, 128), (2, 0, 1, 3)),
            d_out.astype(BF).reshape(N_DEV, 128, D_MODEL)]


def _layer_params(l, small, conv_full, token):
    return dict(
        gpre=small["norm_pre"][l][None] + token, gpost=small["norm_post"][l][None], gmem=small["norm_mem"][l][None],
        cw=conv_full[l], al=_lanes(small["a_log"][l], 4), dt=_lanes(small["dt_bias"][l], 4),
        dnn=small["dn_norm"][l][None], gain=small["gm_norm"][l][None], ws=small["spatial_w"][l],
        bt=jnp.zeros((128, 128), F32).at[:, :GM_GROUPS].set(small["spatial_b"][l].T),
        sinks=_lanes(small["sinks"][l], 0))


def _layer_forward(l, xl, hs, mem, p, w_in_al, other_weights, **tail):
    t = "l%d_" % l
    h, h_t = hs
    cols = _matmul(h, w_in_al, "nn", F32, (1024, 1536, 1024), t + "w_in")
    ya, ss, ts = dn_forward(cols, JB_DN, p["cw"], p["al"], p["dt"], p["dnn"], t + "deltanet")
    yb = gm_forward(cols, JB_GM, p["gain"], p["ws"], p["bt"], t + "gmlp")
    yc = sw_forward(cols, JB_SW, p["sinks"], t + "swa")
    w_mem, w_up, w_out = other_weights(yc)
    mkv = memkv_forward(mem, p["gmem"], w_mem, t + "memkv")
    ym = xm_forward(cols, JB_XM, mkv, t + "memattn")
    outs = merge_forward([ya, yb, yc, ym], cols, JB_GATE, xl, w_up, w_out, p["gpost"], t + "merge", **tail)
    return outs, dict(p, x=xl, h_t=h_t, cols=cols, mkv=mkv, ss=ss, ts=ts, ys=[ya, yb, yc, ym]), (w_in_al, w_mem, w_up, w_out)


def _layer_backward(l, s, mem, weights, dx, token, early=None):
    w_in_al, w_mem, w_up, w_out = weights
    t = "l%d_" % l
    cols = s["cols"]
    dcols, dya, dyb, dyc, dym, dwup, dwout, dgpost = merge_backward(
        s["ys"], cols, JB_GATE, w_up, w_out, s["gpost"] + token, dx, t + "merge_bwd")
    dcols, dmkv = xm_backward(cols, JB_XM, s["mkv"], dym, dcols, t + "memattn_bwd")
    dgmem, dwmem = memkv_backward(mem, s["gmem"], w_mem, dmkv, t + "memkv_bwd")
    sinks = s["sinks"] if early is None else s["sinks"] + early(dwmem, dwup, dwout)
    dcols, dsinks = sw_backward(cols, JB_SW, sinks, dyc, dcols, t + "swa_bwd")
    dcols, dgain, dws, dbt = gm_backward(cols, JB_GM, s["gain"], s["ws"], s["bt"], dyb, dcols, t + "gmlp_bwd")
    dcols, dcw, dal, ddt, ddn = dn_backward(
        cols, JB_DN, s["cw"], s["al"], s["dt"], s["dnn"], s["ss"], s["ts"], dya, dcols, t + "deltanet_bwd")
    dh = _matmul(dcols, w_in_al, "nt", F32, (1024, 1024, 3584), t + "w_in_bwd_x")
    dwin = _matmul(s["h_t"], dcols, "nn", BF, (1024, 1536, 2048), t + "w_in_bwd_w")
    dx, dgpre = prenorm_backward(s["x"], s["gpre"], dh, dx, t + "prenorm_bwd")
    gsmall = dict(norm_pre=dgpre[0], norm_post=dgpost[0], norm_mem=dgmem[0], a_log=dal[0, 4:8], dt_bias=ddt[0, 4:8],
                  dn_norm=ddn[0], gm_norm=dgain[0], spatial_w=dws, spatial_b=dbt[:, :GM_GROUPS].T,
                  sinks=dsinks[0, :SW_HEADS], conv_w=dcw)
    return dx, gsmall, (dwin, dwmem, dwup, dwout)


def kernel(x, mem, norm_pre, norm_post, norm_mem, w_in, conv_w, a_log, dt_bias, dn_norm, gm_norm, spatial_w, spatial_b, sinks, w_mem_kv, w_up, w_out, loss_target, m_norm_pre, m_norm_post, m_norm_mem, m_w_in, m_conv_w, m_a_log, m_dt_bias, m_dn_norm, m_gm_norm, m_spatial_w, m_spatial_b, m_sinks, m_w_mem_kv, m_w_up, m_w_out, v_norm_pre, v_norm_post, v_norm_mem, v_w_in, v_conv_w, v_a_log, v_dt_bias, v_dn_norm, v_gm_norm, v_spatial_w, v_spatial_b, v_sinks, v_w_mem_kv, v_w_up, v_w_out):
    xi, yi, ci = _my_place()
    my_slot = 4 * xi + 2 * yi + ci
    conv_shard = conv_w.shape[-1]
    x2, mem2, target = x[0], mem[0], loss_target[0]

    w_in_pad = jnp.pad(w_in.astype(BF), ((0, 0), (0, 0), (0, W_IN_SHARD_PAD - W_IN_SHARD)))
    shards = [[w_in_pad[l], w_mem_kv[l].astype(BF), w_up[l].astype(BF), w_out[l].astype(BF)] for l in range(DEPTH)]
    w_in_slots0, conv_slots = _all_gather_slots([shards[0][0], conv_w], "gather_weights_l0")
    ag = list(_spread_start(shards[0][1:] + shards[1], "gather", "gather_weights_rest_start", after=w_in_slots0))
    conv_full = jnp.transpose(conv_slots, (1, 2, 0, 3)).reshape(DEPTH, CONV_W, N_DEV * conv_shard)
    small = dict(norm_pre=norm_pre, norm_post=norm_post, norm_mem=norm_mem, a_log=a_log,
                 dt_bias=dt_bias, dn_norm=dn_norm, gm_norm=gm_norm, spatial_w=spatial_w,
                 spatial_b=spatial_b, sinks=sinks)

    def arrived(which, after, name):
        ag[2], ag[3] = _spread_wait(ag[0], ag[1], ag[2], ag[3], which, after, name)
        return [ag[3][a] for a in which]

    p0, p1 = _layer_params(0, small, conv_full, ag[4][0, 0]), _layer_params(1, small, conv_full, 0.0)
    (x1, h1, h1_t), saved0, weights0 = _layer_forward(
        0, x2, prenorm_forward(x2, p0["gpre"], "l0_prenorm"), mem2, p0, _aligned_w_in(w_in_slots0, ag[4][0, 0]),
        lambda y: _other_weights(*arrived([0, 1, 2], y, "gather_weights_l0_rest_wait")),
        next_gain=p1["gpre"])
    w_in_slots1, = arrived([3], x1, "gather_weights_l1_w_in_wait")
    (x_out, loss, dx), saved1, weights1 = _layer_forward(
        1, x1, (h1, h1_t), mem2, p1, _aligned_w_in(w_in_slots1),
        lambda y: _other_weights(*arrived([4, 5, 6], y, "gather_weights_l1_rest_wait")), target=target)

    packed_names = SMALL_NAMES + ("conv_w",)
    dx, gsmall1, gbig1 = _layer_backward(1, saved1, mem2, weights1, dx, 0.0)
    small1 = [loss[0, :1]] + [gsmall1[n] for n in packed_names]
    sm1 = _spread_start([_pack_small(small1)], "gather", "gather_small_grads_l1_start")
    rs_send, rs_recv, rs_src, rs_land, rs_token = _spread_start(_grad_slots(*gbig1), "scatter", "exchange_grads_l1_start")
    rest0 = []

    def send_rest0(dwmem, dwup, dwout):
        rest0.extend(_spread_start(_grad_slots(None, dwmem, dwup, dwout)[1:], "scatter", "exchange_grads_l0_rest_start"))
        return rest0[4][0, 0]

    dx, gsmall0, gbig0 = _layer_backward(0, saved0, mem2, weights0, dx, rs_token[0, 0] + sm1[4][0, 0], send_rest0)
    _, parts1 = _spread_wait(rs_send, rs_recv, rs_src, rs_land, range(4), dx, "exchange_grads_l1_wait")

    small0 = [gsmall0[n] for n in packed_names]
    sm0 = _spread_start([_pack_small(small0)], "gather", "gather_small_grads_l0_start")

    g_win0 = _slots_of_aligned(gbig0[0])
    g_win0 = g_win0.reshape((N_DEV // 2, 2) + g_win0.shape[1:])
    theirs, = _exchange_sibling([g_win0], "exchange_sibling_l0")
    chip_sum = _pair_sum(g_win0, theirs, "pair_sum_l0")
    ch_send, ch_recv, ch_src, ch_land, ch_token = _spread_start([chip_sum], "chips", "exchange_chips_l0_start")

    _, (land1,) = _spread_wait(*sm1[:4], [0], ch_token, "gather_small_grads_l1_wait")
    _, (land0,) = _spread_wait(*sm0[:4], [0], land1, "gather_small_grads_l0_wait")
    tot1 = _unpack_small(_sum_slots(land1, "sum_small_grads_l1"), [p.shape for p in small1])
    tot0 = _unpack_small(_sum_slots(land0, "sum_small_grads_l0"), [p.shape for p in small0])
    loss_tot = tot1[0][0]
    grads = {n: jnp.stack([g0, g1]) for n, g0, g1 in zip(packed_names, tot0, tot1[1:])}
    grads["conv_w"] = lax.dynamic_slice_in_dim(grads["conv_w"], my_slot * conv_shard, conv_shard, axis=2)

    given = dict(norm_pre=(norm_pre, m_norm_pre, v_norm_pre), norm_post=(norm_post, m_norm_post, v_norm_post),
                 norm_mem=(norm_mem, m_norm_mem, v_norm_mem), a_log=(a_log, m_a_log, v_a_log),
                 dt_bias=(dt_bias, m_dt_bias, v_dt_bias), dn_norm=(dn_norm, m_dn_norm, v_dn_norm),
                 gm_norm=(gm_norm, m_gm_norm, v_gm_norm), spatial_w=(spatial_w, m_spatial_w, v_spatial_w),
                 spatial_b=(spatial_b, m_spatial_b, v_spatial_b), sinks=(sinks, m_sinks, v_sinks),
                 conv_w=(conv_w, m_conv_w, v_conv_w))
    pshapes = [given[n][0].shape for n in packed_names]
    pw, pm, pv = (_pack_small([given[n][i] for n in packed_names]) for i in range(3))
    pd, pnm, pnv = _adamw(pw + ch_token[0, 0], _pack_small([grads[n] for n in packed_names]), pm, pv, "adamw_small")
    upd = {n: t for n, t in zip(packed_names, zip(_unpack_small(pd, pshapes), _unpack_small(pnm, pshapes),
                                                  _unpack_small(pnv, pshapes)))}
    big = (("w_mem_kv", (w_mem_kv, m_w_mem_kv, v_w_mem_kv)), ("w_up", (w_up, m_w_up, v_w_up)),
           ("w_out", (w_out, m_w_out, v_w_out)))
    first = [_sum_adamw(parts1[1 + i], w, m, v, 1, None, ch_token, "adamw_%s_l1" % name)
             for i, (name, (w, m, v)) in enumerate(big)]
    _, parts0_rest = _spread_wait(*rest0[:4], range(3), first[-1][0], "exchange_grads_l0_rest_wait")
    for i, (name, (w, m, v)) in enumerate(big):
        g, d, nm, nv = _sum_adamw(parts0_rest[i], w, m, v, 0, first[i], None, "adamw_%s_l0" % name)
        grads[name], upd[name] = g, (d, nm, nv)
    _, (parts0_w_in,) = _spread_wait(ch_send, ch_recv, ch_src, ch_land, [0], upd["w_out"][0], "exchange_chips_l0_wait")
    w_in_t, m_w_in_t, v_w_in_t = (jnp.transpose(t, (2, 0, 1)) for t in (w_in, m_w_in, v_w_in))
    g, d, nm, nv = (jnp.transpose(t, (1, 2, 0)) for t in
                    _sum_adamw_t([parts0_w_in, parts1[0]], w_in_t, m_w_in_t, v_w_in_t, "adamw_w_in"))
    grads["w_in"], upd["w_in"] = g, (d, nm, nv)

    order = ("norm_pre", "norm_post", "norm_mem", "w_in", "conv_w", "a_log", "dt_bias", "dn_norm",
             "gm_norm", "spatial_w", "spatial_b", "sinks", "w_mem_kv", "w_up", "w_out")
    return (loss_tot, dx[None], *[grads[n] for n in order], *[upd[n][0] for n in order],
            *[upd[n][1] for n in order], *[upd[n][2] for n in order])
```

```python
import functools
import math

import jax
import jax.numpy as jnp
from jax import lax
from jax.experimental import pallas as pl
from jax.experimental.pallas import tpu as pltpu

MESH = pl.DeviceIdType.MESH
N_DEV = 8

D_MODEL = 1024
DEPTH = 2
N_BRANCH = 4
BRANCH_W = 512
DN_HEADS = 4
CONV_W = 4
GM_GROUPS = 4
SW_HEADS = 8
EPS = 1e-6
NEG_INF = -1e30

D_IN = 9992
W_IN_SHARD = D_IN // N_DEV
W_IN_SHARD_PAD = 1280
D_IN_AL = 10752
DN_W, SW_W, GM_W, XM_W = 2560, 1536, 1536, 1024

ADAM_LR = 0.001
ADAM_B1 = 0.9
ADAM_B2 = 0.999
ADAM_EPS = 1e-08
ADAM_WD = 0.01
ADAM_STEP = 10

VMEM_LIMIT = 56 * 1024 * 1024

BF = jnp.bfloat16
F32 = jnp.float32
DN_C = 128
DN_D = 128
HALO = 8
BLK = 128
Y_DTYPE = BF


def _my_place():
    return lax.axis_index("x"), lax.axis_index("y"), lax.axis_index("c")


_ANY = pl.BlockSpec(memory_space=pl.ANY)


def _all_gather_slots(parts, name):
    n = len(parts)

    def body(*refs):
        p_refs, out_refs = refs[:n], refs[n:2 * n]
        send_sems, recv_sems, local_sems = refs[2 * n:]
        x, y, c = _my_place()
        me, sibling = (x, y, c), (x, y, 1 - c)
        chips = [(1 - x, y), (x, 1 - y), (1 - x, 1 - y)]

        def copy(a, k, block, to, src=None):
            px, py, pc = block
            slot = out_refs[a].at[4 * px + 2 * py + pc]
            return pltpu.make_async_remote_copy(
                src_ref=slot if src is None else src, dst_ref=slot,
                send_sem=send_sems.at[7 * a + k], recv_sem=recv_sems.at[7 * a + k],
                device_id=to, device_id_type=MESH)

        mine = [pltpu.make_async_copy(p_refs[a], out_refs[a].at[4 * x + 2 * y + c], local_sems.at[a])
                for a in range(n)]
        for cp in mine:
            cp.start()
        first = []
        for a in range(n):
            first.append(copy(a, 0, me, sibling, src=p_refs[a]))
            first += [copy(a, 1 + j, me, (*chip, c), src=p_refs[a]) for j, chip in enumerate(chips)]
        for cp in first:
            cp.start()
        passed = []
        for j, chip in enumerate(chips):
            for a in range(n):
                copy(a, 1 + j, (*chip, c), me).wait_recv()
                fwd = copy(a, 4 + j, (*chip, c), sibling)
                fwd.start()
                passed.append(fwd)
        for a in range(n):
            copy(a, 0, sibling, me).wait_recv()
            for j, chip in enumerate(chips):
                copy(a, 4 + j, (*chip, 1 - c), me).wait_recv()
        for cp in first + passed:
            cp.wait_send()
        for cp in mine:
            cp.wait()

    return pl.pallas_call(
        body, name=name,
        out_shape=[jax.ShapeDtypeStruct((N_DEV,) + p.shape, p.dtype) for p in parts],
        in_specs=[_ANY] * n, out_specs=[_ANY] * n,
        scratch_shapes=[pltpu.SemaphoreType.DMA((7 * n,)), pltpu.SemaphoreType.DMA((7 * n,)),
                        pltpu.SemaphoreType.DMA((n,))],
    )(*parts)


def _exchange_sibling(parts, name):
    n = len(parts)

    def body(*refs):
        g_refs, out_refs = refs[:n], refs[n:2 * n]
        send_sems, recv_sems = refs[2 * n:]
        x, y, c = _my_place()
        copies = [pltpu.make_async_remote_copy(
            src_ref=g_refs[a].at[:, 1 - c], dst_ref=out_refs[a],
            send_sem=send_sems.at[a], recv_sem=recv_sems.at[a],
            device_id=(x, y, 1 - c), device_id_type=MESH) for a in range(n)]
        for cp in copies:
            cp.start()
        for cp in copies:
            cp.wait()

    return pl.pallas_call(
        body, name=name,
        out_shape=[jax.ShapeDtypeStruct((4,) + g.shape[2:], g.dtype) for g in parts],
        in_specs=[_ANY] * n, out_specs=[_ANY] * n,
        scratch_shapes=[pltpu.SemaphoreType.DMA((n,)), pltpu.SemaphoreType.DMA((n,))],
    )(*parts)


_HBM = pl.BlockSpec(memory_space=pltpu.HBM)
_SEM = pl.BlockSpec(memory_space=pltpu.SEMAPHORE)
_EFFECT = pltpu.SideEffectType.DATAFLOW_SIDE_EFFECTING


def _peer(x, y, c, k):
    return (1 - x if (k >> 2) & 1 else x, 1 - y if (k >> 1) & 1 else y, 1 - c if k & 1 else c)


def _spread_start(srcs, mode, name, after=None):
    n = len(srcs)
    tail = [] if after is None else [after]
    lands = [lax.empty((N_DEV,) + s.shape if mode == "gather" else s.shape, s.dtype) for s in srcs]
    peers = range(0, N_DEV, 2) if mode == "chips" else range(N_DEV)

    def body(*refs):
        src_refs, land_refs = refs[:n], refs[n:2 * n]
        send_sems, recv_sems = refs[2 * n + len(tail):2 * n + len(tail) + 2]
        token = refs[-1]
        x, y, c = _my_place()
        for a in range(n):
            for k in peers:
                px, py, pc = _peer(x, y, c, k)
                if mode == "chips":
                    src, mine = src_refs[a].at[2 * px + py], 2 * x + y
                else:
                    src = src_refs[a].at[4 * px + 2 * py + pc] if mode == "scatter" else src_refs[a]
                    mine = 4 * x + 2 * y + c
                pltpu.make_async_remote_copy(
                    src_ref=src, dst_ref=land_refs[a].at[mine],
                    send_sem=send_sems.at[a], recv_sem=recv_sems.at[a],
                    device_id=(px, py, pc), device_id_type=MESH).start()
        token[...] = jnp.zeros_like(token)

    out = pl.pallas_call(
        body, name=name,
        out_shape=[pltpu.SemaphoreType.DMA((n,)), pltpu.SemaphoreType.DMA((n,))]
        + [pltpu.HBM(s.shape, s.dtype) for s in srcs] + [pltpu.HBM(l.shape, l.dtype) for l in lands]
        + [jax.ShapeDtypeStruct((8, 128), F32)],
        in_specs=[_HBM] * (2 * n) + [_ANY] * len(tail),
        out_specs=[_SEM, _SEM] + [_HBM] * (2 * n) + [pl.BlockSpec(memory_space=pltpu.VMEM)],
        input_output_aliases={i: 2 + i for i in range(2 * n)},
        compiler_params=pltpu.CompilerParams(has_side_effects=_EFFECT),
    )(*[pltpu.with_memory_space_constraint(s, pltpu.HBM) for s in srcs],
      *[pltpu.with_memory_space_constraint(l, pltpu.HBM) for l in lands], *tail)
    return out[0], out[1], out[2:2 + n], out[2 + n:2 + 2 * n], out[-1]


def _spread_wait(send_sems, recv_sems, srcs, lands, which, after, name):
    n = len(srcs)

    def body(*refs):
        land_refs = refs[n:2 * n]
        send_sems, recv_sems = refs[2 * n:2 * n + 2]
        x, y, c = _my_place()
        for a in which:
            whole = pltpu.make_async_remote_copy(
                src_ref=land_refs[a], dst_ref=land_refs[a],
                send_sem=send_sems.at[a], recv_sem=recv_sems.at[a],
                device_id=(x, y, c), device_id_type=MESH)
            whole.wait_send()
            whole.wait_recv()

    out = pl.pallas_call(
        body, name=name,
        out_shape=[pltpu.HBM(s.shape, s.dtype) for s in srcs] + [pltpu.HBM(l.shape, l.dtype) for l in lands],
        in_specs=[_HBM] * (2 * n) + [_SEM, _SEM, _ANY],
        out_specs=[_HBM] * (2 * n),
        input_output_aliases={i: i for i in range(2 * n)},
        compiler_params=pltpu.CompilerParams(has_side_effects=_EFFECT),
    )(*srcs, *lands, send_sems, recv_sems, after)
    return out[:n], out[n:]


def _sum_slots(parts, name):
    def body(p_ref, o_ref):
        acc = p_ref[0]
        for s in range(1, parts.shape[0]):
            acc = acc + p_ref[s]
        o_ref[...] = acc

    return pl.pallas_call(body, name=name, out_shape=jax.ShapeDtypeStruct(parts.shape[1:], parts.dtype))(parts)


def _pick(n, pref):
    if n <= pref:
        return n
    t = pref - pref % 128
    while t > 0 and n % t:
        t -= 128
    return t if t > 0 else n


_DIMS = {"nn": (((1,), (0,)), ((), ())),
         "nt": (((1,), (1,)), ((), ())),
         "tn": (((0,), (0,)), ((), ()))}


def _matmul(a, b, mode, out_dtype, tiles, name):
    (m, k) = a.shape
    n = b.shape[1] if mode == "nn" else b.shape[0]
    tm, tn, tk = (_pick(d, t) for d, t in zip((m, n, k), tiles))
    nk = k // tk

    def product(a_ref, b_ref):
        return lax.dot_general(a_ref[...].astype(BF), b_ref[...].astype(BF), _DIMS[mode], preferred_element_type=F32)

    def body_whole_k(a_ref, b_ref, o_ref):
        o_ref[...] = product(a_ref, b_ref).astype(o_ref.dtype)

    def body_split_k(a_ref, b_ref, o_ref, acc_ref):
        kk = pl.program_id(2)

        @pl.when(kk == 0)
        def _():
            acc_ref[...] = jnp.zeros_like(acc_ref)

        acc_ref[...] += product(a_ref, b_ref)

        @pl.when(kk == nk - 1)
        def _():
            o_ref[...] = acc_ref[...].astype(o_ref.dtype)

    b_spec = (pl.BlockSpec((tn, tk), lambda i, j, kk: (j, kk)) if mode == "nt"
              else pl.BlockSpec((tk, tn), lambda i, j, kk: (kk, j)))
    return pl.pallas_call(
        body_whole_k if nk == 1 else body_split_k, name=name,
        out_shape=jax.ShapeDtypeStruct((m, n), out_dtype),
        grid=(m // tm, n // tn, nk),
        in_specs=[pl.BlockSpec((tm, tk), lambda i, j, kk: (i, kk)), b_spec],
        out_specs=pl.BlockSpec((tm, tn), lambda i, j, kk: (i, j)),
        scratch_shapes=[] if nk == 1 else [pltpu.VMEM((tm, tn), F32)],
        compiler_params=pltpu.CompilerParams(
            dimension_semantics=("parallel", "parallel", "arbitrary"),
            vmem_limit_bytes=VMEM_LIMIT),
    )(a, b)


def _rows2d(t, lead):
    return t.reshape(t.shape[:lead] + (math.prod(t.shape[lead:-1]), t.shape[-1]))


def _pair_sum(g, theirs, name):
    g3, t3 = _rows2d(g, 2), _rows2d(theirs, 1)
    _, r, w = t3.shape
    tr = _pick(r, 512)

    def body(g_ref, t_ref, o_ref):
        c = lax.axis_index("c")
        mine = jnp.where(c == 0, g_ref[0, 0], g_ref[0, 1])
        o_ref[0] = (mine.astype(F32) + t_ref[0].astype(F32)).astype(o_ref.dtype)

    out = pl.pallas_call(
        body, name=name,
        out_shape=jax.ShapeDtypeStruct(t3.shape, t3.dtype),
        grid=(4, r // tr),
        in_specs=[pl.BlockSpec((1, 2, tr, w), lambda q, i: (q, 0, i, 0)),
                  pl.BlockSpec((1, tr, w), lambda q, i: (q, i, 0))],
        out_specs=pl.BlockSpec((1, tr, w), lambda q, i: (q, i, 0)),
        compiler_params=pltpu.CompilerParams(dimension_semantics=("parallel", "parallel")),
    )(g3, t3)
    return out.reshape(theirs.shape)


def _adam_update(w, g, m, v):
    c1 = 1.0 - ADAM_B1 ** ADAM_STEP
    c2 = 1.0 - ADAM_B2 ** ADAM_STEP
    nm = ADAM_B1 * m + (1.0 - ADAM_B1) * g
    nv = ADAM_B2 * v + (1.0 - ADAM_B2) * (g * g)
    delta = -ADAM_LR * ((nm / c1) / (jnp.sqrt(nv / c2) + ADAM_EPS) + ADAM_WD * w)
    return delta, nm, nv


def _sum_adamw(parts, w, m, v, layer, carry, after, name):
    shape = w.shape
    cols = shape[-1]
    p3 = _rows2d(parts, 1)
    w3, m3, v3 = (_rows2d(t, 1) for t in (w, m, v))
    rows = w3.shape[1]
    tr = _pick(rows, 128)
    n_parts = p3.shape[0]

    def body(p_ref, w_ref, m_ref, v_ref, *rest):
        g_ref, d_ref, nm_ref, nv_ref = rest[-4:]
        g = p_ref[0, :, :cols].astype(F32)
        for q in range(1, n_parts):
            g = g + p_ref[q, :, :cols].astype(F32)
        d, nm, nv = _adam_update(w_ref[0], g, m_ref[0], v_ref[0])
        g_ref[0] = g
        d_ref[0] = d
        nm_ref[0] = nm
        nv_ref[0] = nv

    spec = pl.BlockSpec((1, tr, cols), lambda i: (layer, i, 0))
    extra = [] if carry is None else [_rows2d(t, 1) for t in carry]
    tail = [] if after is None else [after]
    out = pl.pallas_call(
        body, name=name,
        out_shape=[jax.ShapeDtypeStruct(w3.shape, F32)] * 4,
        grid=(rows // tr,),
        in_specs=[pl.BlockSpec((n_parts, tr, p3.shape[-1]), lambda i: (0, i, 0)), spec, spec, spec] + [_ANY] * len(extra + tail),
        out_specs=[spec] * 4,
        input_output_aliases={4 + i: i for i in range(len(extra))},
        compiler_params=pltpu.CompilerParams(dimension_semantics=("parallel",)),
    )(p3, w3, m3, v3, *extra, *tail)
    return tuple(t.reshape(shape) for t in out)


def _sum_adamw_t(parts, w, m, v, name):
    rows = parts[0].shape[2]
    tr = 128
    assert rows % tr == 0 and rows >= w.shape[0]

    def body(*refs):
        p_refs, (w_ref, m_ref, v_ref), (g_ref, d_ref, nm_ref, nv_ref) = refs[:DEPTH], refs[DEPTH:DEPTH + 3], refs[DEPTH + 3:]
        for l in range(DEPTH):
            g = p_refs[l][0].astype(F32)
            for q in range(1, p_refs[l].shape[0]):
                g = g + p_refs[l][q].astype(F32)
            g = g.T
            d, nm, nv = _adam_update(w_ref[:, l, :], g, m_ref[:, l, :], v_ref[:, l, :])
            g_ref[:, l, :] = g
            d_ref[:, l, :] = d
            nm_ref[:, l, :] = nm
            nv_ref[:, l, :] = nv

    spec = pl.BlockSpec((tr,) + w.shape[1:], lambda i: (i, 0, 0))
    return pl.pallas_call(
        body, name=name,
        out_shape=[jax.ShapeDtypeStruct(w.shape, F32)] * 4,
        grid=(rows // tr,),
        in_specs=[pl.BlockSpec((p.shape[0], p.shape[1], tr), lambda i: (0, 0, i)) for p in parts] + [spec] * 3,
        out_specs=[spec] * 4,
        compiler_params=pltpu.CompilerParams(dimension_semantics=("parallel",)),
    )(*parts, w, m, v)


def _adamw(w, g, m, v, name):
    rows, cols = w.shape
    tr = _pick(rows, 128)

    def body(w_ref, g_ref, m_ref, v_ref, d_ref, nm_ref, nv_ref):
        d, nm, nv = _adam_update(w_ref[...], g_ref[...], m_ref[...], v_ref[...])
        d_ref[...] = d
        nm_ref[...] = nm
        nv_ref[...] = nv

    spec = pl.BlockSpec((tr, cols), lambda i: (i, 0))
    return pl.pallas_call(
        body, name=name,
        out_shape=[jax.ShapeDtypeStruct((rows, cols), F32)] * 3,
        grid=(rows // tr,),
        in_specs=[spec] * 4, out_specs=[spec] * 3,
        compiler_params=pltpu.CompilerParams(dimension_semantics=("parallel",)),
    )(w, g, m, v)


_VJP = {"nn": (("nt", "gb"), ("tn", "ag")),
        "nt": (("nn", "gb"), ("tn", "ga")),
        "tn": (("nt", "bg"), ("nn", "ag"))}


def _make_dot(cast, precision):
    def raw(mode, a, b):
        return lax.dot_general(cast(a), cast(b), _DIMS[mode], precision=precision,
                               preferred_element_type=F32)

    @functools.partial(jax.custom_vjp, nondiff_argnums=(0,))
    def dot(mode, a, b):
        return raw(mode, a, b)

    def fwd(mode, a, b):
        return raw(mode, a, b), (a, b)

    def bwd(mode, res, g):
        a, b = res
        pick = {"a": a, "b": b, "g": g}
        (ma, ta), (mb, tb) = _VJP[mode]
        return dot(ma, pick[ta[0]], pick[ta[1]]), dot(mb, pick[tb[0]], pick[tb[1]])

    dot.defvjp(fwd, bwd)
    return dot


bdot = _make_dot(lambda t: t.astype(BF), None)
hdot = _make_dot(lambda t: t, lax.Precision.HIGHEST)


def _xdot(mode, a, b):
    return lax.dot_general(a, b, _DIMS[mode], precision=lax.Precision.HIGH, preferred_element_type=F32)


def _unit_lower_inverse(Ls):
    n = Ls[0].shape[0]
    batched = (((2,), (1,)), ((0,), (0,)))
    mm = lambda a, b: lax.dot_general(a, b, batched, precision=lax.Precision.HIGH, preferred_element_type=F32)
    eye = (lax.broadcasted_iota(jnp.int32, (n, n), 0) == lax.broadcasted_iota(jnp.int32, (n, n), 1)).astype(F32)
    p = jnp.stack(Ls)
    t_inv = eye[None] - p
    for _ in range(6):
        p = mm(p, p)
        t_inv = t_inv + mm(t_inv, p)
    return [t_inv[h] for h in range(len(Ls))]


@jax.custom_vjp
def _tri_solve(L, rhs, t_inv):
    return _xdot("nn", t_inv, rhs)


def _tri_solve_fwd(L, rhs, t_inv):
    sol = _xdot("nn", t_inv, rhs)
    return sol, (t_inv, sol)


def _tri_solve_bwd(res, dsol):
    t_inv, sol = res
    drhs = _xdot("tn", t_inv, dsol)
    return -_xdot("nt", drhs, sol), drhs, jnp.zeros_like(t_inv)


_tri_solve.defvjp(_tri_solve_fwd, _tri_solve_bwd)


def _sigmoid(x):
    return 1.0 / (1.0 + jnp.exp(-x))


def _softplus(x):
    return jnp.maximum(x, 0.0) + jnp.log(1.0 + jnp.exp(-jnp.abs(x)))


def _dn_chunk(S, xs, ba, z, cw, al, dt, dn, t_saved=None):
    return _dn_core(S, _conv_taps(xs, cw), ba, z, al, dt, dn, t_saved)


def _conv_taps(xs, cw):
    return xs[0] * cw[0] + xs[1] * cw[1] + xs[2] * cw[2] + xs[3] * cw[3]


def _dn_core(S, pre, ba, z, al, dt, dn, t_saved=None):
    C = DN_C
    qkv = pre * _sigmoid(pre)
    lane = lax.broadcasted_iota(jnp.int32, (1, 128), 1)
    sub = lax.broadcasted_iota(jnp.int32, (C, 1), 0)
    row_i = lax.broadcasted_iota(jnp.int32, (C, C), 0)
    col_i = lax.broadcasted_iota(jnp.int32, (C, C), 1)
    strict = row_i > col_i
    incl = row_i >= col_i
    g_all = jnp.where((lane >= 4) & (lane < 8), -jnp.exp(al) * _softplus(ba + dt), 0.0)
    gc_all = hdot("nn", incl.astype(F32), g_all)
    gc_all_t = gc_all.T
    beta_all = _sigmoid(ba)
    glast_all = jnp.sum(jnp.where(sub == C - 1, gc_all, 0.0), axis=0, keepdims=True)
    heads = []
    for h in range(DN_HEADS):
        q = qkv[:, 128 * h:128 * (h + 1)]
        k = qkv[:, 512 + 128 * h:512 + 128 * (h + 1)]
        v = qkv[:, 1024 + 128 * h:1024 + 128 * (h + 1)]
        q = q * lax.rsqrt(jnp.sum(q * q, axis=1, keepdims=True) + EPS) * (DN_D ** -0.5)
        k = k * lax.rsqrt(jnp.sum(k * k, axis=1, keepdims=True) + EPS)
        beta = jnp.sum(jnp.where(lane == h, beta_all, 0.0), axis=1, keepdims=True)
        gc = jnp.sum(jnp.where(lane == 4 + h, gc_all, 0.0), axis=1, keepdims=True)
        gc_row = jnp.sum(jnp.where(sub == 4 + h, gc_all_t, 0.0), axis=0, keepdims=True)
        g_last = jnp.sum(jnp.where(lane == 4 + h, glast_all, 0.0), axis=1, keepdims=True)
        diff = gc - gc_row
        kb = k * beta
        L = jnp.where(strict, bdot("nt", kb, k) * jnp.exp(jnp.where(strict, diff, 0.0)), 0.0)
        heads.append((q, k, v, beta, gc, g_last, diff, kb, L))
    t_invs = _unit_lower_inverse([hd[-1] for hd in heads]) if t_saved is None else t_saved
    ys, s_new = [], []
    for h, (q, k, v, beta, gc, g_last, diff, kb, L) in enumerate(heads):
        sol = _tri_solve(L, jnp.concatenate([v * beta, kb * jnp.exp(gc)], axis=1), t_invs[h])
        u, w = sol[:, :DN_D], sol[:, DN_D:]
        a_qk = jnp.where(incl, bdot("nt", q, k) * jnp.exp(jnp.where(incl, diff, 0.0)), 0.0)
        qg = q * jnp.exp(gc)
        kd = k * jnp.exp(g_last - gc)
        v_new = u - bdot("nn", w, S[h])
        o = bdot("nn", qg, S[h]) + bdot("nn", a_qk, v_new)
        s_new.append(S[h] * jnp.exp(g_last) + bdot("tn", kd, v_new))
        o = o * lax.rsqrt(jnp.mean(o * o, axis=1, keepdims=True) + EPS) * dn
        zh = z[:, 128 * h:128 * (h + 1)]
        ys.append(o * (zh * _sigmoid(zh)))
    return jnp.concatenate(ys, axis=1), tuple(s_new), tuple(t_invs)


def _load_shifted(xbuf, x_ref, halo_ref, first):
    xbuf[0:HALO, :] = jnp.where(first, 0.0, halo_ref[:, 0:1536])
    xbuf[HALO:HALO + DN_C, :] = x_ref[:, 0:1536]
    return [xbuf[HALO - 3 + k:HALO - 3 + k + DN_C, :] for k in range(4)]


def dn_forward(cols, jblk, cw, al, dt, dn, name):
    T = cols.shape[0]
    n = T // DN_C

    def body(x_ref, halo_ref, cw_ref, al_ref, dt_ref, dn_ref, y_ref, ss_ref, ts_ref, s_scr, xbuf):
        i = pl.program_id(0)

        @pl.when(i == 0)
        def _():
            s_scr[...] = jnp.zeros_like(s_scr)

        xs = _load_shifted(xbuf, x_ref, halo_ref, i == 0)
        ss_ref[0] = s_scr[...]
        S = [s_scr[h] for h in range(DN_HEADS)]
        cws = [cw_ref[k:k + 1, :] for k in range(4)]
        y, s_new, t_invs = _dn_chunk(S, xs, x_ref[:, 2048:2176], x_ref[:, 1536:2048], cws,
                                     al_ref[...], dt_ref[...], dn_ref[...])
        y_ref[...] = y.astype(y_ref.dtype)
        for h in range(DN_HEADS):
            s_scr[h] = s_new[h]
            ts_ref[0, h] = t_invs[h]

    per = DN_C // HALO
    full = lambda shape: pl.BlockSpec(shape, lambda i: (0,) * len(shape))
    return pl.pallas_call(
        body, name=name,
        out_shape=[jax.ShapeDtypeStruct((T, 512), Y_DTYPE),
                   jax.ShapeDtypeStruct((n, DN_HEADS, DN_D, DN_D), F32),
                   jax.ShapeDtypeStruct((n, DN_HEADS, DN_D, DN_D), F32)],
        grid=(n,),
        in_specs=[pl.BlockSpec((DN_C, DN_W), lambda i: (i, jblk)),
                  pl.BlockSpec((HALO, DN_W), lambda i: (jnp.maximum(i * per - 1, 0), jblk)),
                  full((4, 1536)), full((1, 128)), full((1, 128)), full((1, 128))],
        out_specs=[pl.BlockSpec((DN_C, 512), lambda i: (i, 0)),
                   pl.BlockSpec((1, DN_HEADS, DN_D, DN_D), lambda i: (i, 0, 0, 0)),
                   pl.BlockSpec((1, DN_HEADS, DN_D, DN_D), lambda i: (i, 0, 0, 0))],
        scratch_shapes=[pltpu.VMEM((DN_HEADS, DN_D, DN_D), F32), pltpu.VMEM((HALO + DN_C, 1536), F32)],
        compiler_params=pltpu.CompilerParams(dimension_semantics=("arbitrary",)),
    )(cols, cols, cw, al, dt, dn)


def dn_backward(cols, jblk, cw, al, dt, dn, ss, ts, dy, dcols, name):
    T = cols.shape[0]
    n = T // DN_C

    def body(x_ref, halo_ref, cw_ref, al_ref, dt_ref, dn_ref, ss_ref, ts_ref, dy_ref, dcols_in,
             dx_ref, dcw_ref, dal_ref, ddt_ref, ddn_ref, ds_scr, xbuf, dbuf, carry):
        i = pl.program_id(0)

        @pl.when(i == 0)
        def _():
            ds_scr[...] = jnp.zeros_like(ds_scr)
            carry[...] = jnp.zeros_like(carry)
            dcw_ref[...] = jnp.zeros_like(dcw_ref)
            dal_ref[...] = jnp.zeros_like(dal_ref)
            ddt_ref[...] = jnp.zeros_like(ddt_ref)
            ddn_ref[...] = jnp.zeros_like(ddn_ref)

        xs = _load_shifted(xbuf, x_ref, halo_ref, i == n - 1)
        S = [ss_ref[0, h] for h in range(DN_HEADS)]
        cws = [cw_ref[k:k + 1, :] for k in range(4)]

        t_saved = [ts_ref[0, h] for h in range(DN_HEADS)]

        def f(S, pre, ba, z, al, dt, dn):
            return _dn_core(S, pre, ba, z, al, dt, dn, t_saved)[:2]

        _, vjp = jax.vjp(f, S, _conv_taps(xs, cws), x_ref[:, 2048:2176], x_ref[:, 1536:2048],
                         al_ref[...], dt_ref[...], dn_ref[...])
        dS, dpre, dba, dz, dal, ddt, ddn = vjp((dy_ref[...], tuple(ds_scr[h] for h in range(DN_HEADS))))
        for h in range(DN_HEADS):
            ds_scr[h] = dS[h]
        dbuf[0:DN_C, :] = dpre
        dbuf[DN_C:DN_C + HALO, :] = carry[...]
        ahead = [dbuf[3 - k:3 - k + DN_C, :] for k in range(3)] + [dpre]
        dx = cws[3] * ahead[3] + cws[2] * ahead[2] + cws[1] * ahead[1] + cws[0] * ahead[0]
        dx_ref[...] = jnp.concatenate([dx, dz, dba, jnp.zeros((DN_C, DN_W - 2176), F32)], axis=1).astype(dx_ref.dtype)
        carry[...] = dpre[0:HALO, :]
        x_here = xbuf[HALO:HALO + DN_C, :]
        for k in range(4):
            dcw_ref[k:k + 1, :] += jnp.sum(x_here * ahead[k], axis=0, keepdims=True)
        dal_ref[...] += dal
        ddt_ref[...] += ddt
        ddn_ref[...] += ddn

    per = DN_C // HALO
    rev = lambda i: n - 1 - i
    full = lambda shape: pl.BlockSpec(shape, lambda i: (0,) * len(shape))
    return pl.pallas_call(
        body, name=name,
        out_shape=[jax.ShapeDtypeStruct(dcols.shape, dcols.dtype),jax.ShapeDtypeStruct((4, 1536), F32),
                   jax.ShapeDtypeStruct((1, 128), F32), jax.ShapeDtypeStruct((1, 128), F32),
                   jax.ShapeDtypeStruct((1, 128), F32)],
        grid=(n,),
        in_specs=[pl.BlockSpec((DN_C, DN_W), lambda i: (rev(i), jblk)),
                  pl.BlockSpec((HALO, DN_W), lambda i: (jnp.maximum(rev(i) * per - 1, 0), jblk)),
                  full((4, 1536)), full((1, 128)), full((1, 128)), full((1, 128)),
                  pl.BlockSpec((1, DN_HEADS, DN_D, DN_D), lambda i: (rev(i), 0, 0, 0)),
                  pl.BlockSpec((1, DN_HEADS, DN_D, DN_D), lambda i: (rev(i), 0, 0, 0)),
                  pl.BlockSpec((DN_C, 512), lambda i: (rev(i), 0)), _ANY],
        out_specs=[pl.BlockSpec((DN_C, DN_W), lambda i: (rev(i), jblk)),
                   full((4, 1536)), full((1, 128)), full((1, 128)), full((1, 128))],
        scratch_shapes=[pltpu.VMEM((DN_HEADS, DN_D, DN_D), F32), pltpu.VMEM((HALO + DN_C, 1536), F32),
                        pltpu.VMEM((HALO + DN_C, 1536), F32), pltpu.VMEM((HALO, 1536), F32)],
        input_output_aliases={9: 0},
        compiler_params=pltpu.CompilerParams(dimension_semantics=("arbitrary",)),
    )(cols, cols, cw, al, dt, dn, ss, ts, dy, dcols)


def _full(shape):
    return pl.BlockSpec(shape, lambda i: (0,) * len(shape))


def _silu(x):
    return x * _sigmoid(x)


def _gelu(x):
    return 0.5 * x * (1.0 + jnp.tanh(0.7978845608028654 * (x + 0.044715 * (x * x * x))))


def _lane_col(mat, idx):
    lane = lax.broadcasted_iota(jnp.int32, (1, mat.shape[1]), 1)
    return jnp.sum(jnp.where(lane == idx, mat, 0.0), axis=1, keepdims=True)


def _gm_chunk(uv, z, gain, ws, bt):
    g = _gelu(uv)
    u, v = g[:, :512], g[:, 512:]
    v = v * lax.rsqrt(jnp.mean(v * v, axis=1, keepdims=True) + EPS) * gain
    row_i = lax.broadcasted_iota(jnp.int32, (BLK, BLK), 0)
    col_i = lax.broadcasted_iota(jnp.int32, (BLK, BLK), 1)
    causal = row_i >= col_i
    ss = []
    for grp in range(4):
        wg = jnp.where(causal, ws[grp], 0.0)
        ss.append(bdot("nn", wg, v[:, BLK * grp:BLK * (grp + 1)]) + _lane_col(bt, grp))
    return u * jnp.concatenate(ss, axis=1) * _silu(z)


def gm_forward(cols, jblk, gain, ws, bt, name):
    T = cols.shape[0]

    def body(x_ref, gain_ref, ws_ref, bt_ref, y_ref):
        y_ref[...] = _gm_chunk(x_ref[:, 0:1024], x_ref[:, 1024:1536], gain_ref[...],
                               [ws_ref[g] for g in range(4)], bt_ref[...]).astype(y_ref.dtype)

    return pl.pallas_call(
        body, name=name, out_shape=jax.ShapeDtypeStruct((T, 512), Y_DTYPE), grid=(T // BLK,),
        in_specs=[pl.BlockSpec((BLK, GM_W), lambda i: (i, jblk)),
                  _full((1, 512)), _full((4, BLK, BLK)), _full((BLK, BLK))],
        out_specs=pl.BlockSpec((BLK, 512), lambda i: (i, 0)),
        compiler_params=pltpu.CompilerParams(dimension_semantics=("parallel",)),
    )(cols, gain, ws, bt)


def gm_backward(cols, jblk, gain, ws, bt, dy, dcols, name):
    T = cols.shape[0]

    def body(x_ref, gain_ref, ws_ref, bt_ref, dy_ref, dcols_in, dx_ref, dgain_ref, dws_ref, dbt_ref):
        @pl.when(pl.program_id(0) == 0)
        def _():
            dgain_ref[...] = jnp.zeros_like(dgain_ref)
            dws_ref[...] = jnp.zeros_like(dws_ref)
            dbt_ref[...] = jnp.zeros_like(dbt_ref)

        _, vjp = jax.vjp(_gm_chunk, x_ref[:, 0:1024], x_ref[:, 1024:1536], gain_ref[...],
                         [ws_ref[g] for g in range(4)], bt_ref[...])
        duv, dz, dgain, dws, dbt = vjp(dy_ref[...])
        dx_ref[...] = jnp.concatenate([duv, dz], axis=1).astype(dx_ref.dtype)
        dgain_ref[...] += dgain
        for g in range(4):
            dws_ref[g] += dws[g]
        dbt_ref[...] += dbt

    return pl.pallas_call(
        body, name=name,
        out_shape=[jax.ShapeDtypeStruct(dcols.shape, dcols.dtype),jax.ShapeDtypeStruct((1, 512), F32),
                   jax.ShapeDtypeStruct((4, BLK, BLK), F32), jax.ShapeDtypeStruct((BLK, BLK), F32)],
        grid=(T // BLK,),
        in_specs=[pl.BlockSpec((BLK, GM_W), lambda i: (i, jblk)),
                  _full((1, 512)), _full((4, BLK, BLK)), _full((BLK, BLK)),
                  pl.BlockSpec((BLK, 512), lambda i: (i, 0)), _ANY],
        out_specs=[pl.BlockSpec((BLK, GM_W), lambda i: (i, jblk)),
                   _full((1, 512)), _full((4, BLK, BLK)), _full((BLK, BLK))],
        input_output_aliases={5: 0},
        compiler_params=pltpu.CompilerParams(dimension_semantics=("arbitrary",)),
    )(cols, gain, ws, bt, dy, dcols)


def _sw_block(first, q, kp, kc, vp, vc, z, sinks):
    P = BLK
    lane = lax.broadcasted_iota(jnp.int32, (1, 128), 1)
    r = lax.broadcasted_iota(jnp.int32, (128, 128), 0)
    c = lax.broadcasted_iota(jnp.int32, (128, 128), 1)
    swap = (c == (r + 64) % 128).astype(F32)
    k2 = jnp.concatenate([kp, kc], axis=0)
    v2 = jnp.concatenate([vp, vc], axis=0)
    k2s = bdot("nn", k2, swap)
    v2s = bdot("nn", v2, swap)
    qi = lax.broadcasted_iota(jnp.int32, (P, 2 * P), 0)
    kj = lax.broadcasted_iota(jnp.int32, (P, 2 * P), 1)
    dist = qi + P - kj
    valid = (dist >= 0) & (dist < P) & ((kj >= P) | jnp.logical_not(first))
    outs = []
    for j in range(4):
        acc = jnp.zeros((P, 128), F32)
        for half in range(2):
            h = 2 * j + half
            kv = h // 4
            in_half = (lane >= 64 * half) & (lane < 64 * half + 64)
            qh = jnp.where(in_half, q[:, 128 * j:128 * (j + 1)], 0.0)
            same = (half == kv)
            s = bdot("nt", qh, k2 if same else k2s) * (64 ** -0.5)
            s = jnp.where(valid, s, NEG_INF)
            sink = _lane_col(sinks, h)
            m = lax.stop_gradient(jnp.maximum(jnp.max(s, axis=1, keepdims=True), sink))
            e = jnp.exp(s - m)
            p = e / (jnp.sum(e, axis=1, keepdims=True) + jnp.exp(sink - m))
            o = bdot("nn", p, v2 if same else v2s)
            acc = acc + jnp.where(in_half, o, 0.0)
        outs.append(acc)
    return jnp.concatenate(outs, axis=1) * _silu(z)


def _sw_specs(jblk, idx):
    prev = lambda i: jnp.maximum(idx(i) - 1, 0)
    jk = (jblk * SW_W + 1024) // 128
    return [pl.BlockSpec((BLK, SW_W), lambda i: (idx(i), jblk)),
            pl.BlockSpec((BLK, 128), lambda i: (prev(i), jk)),
            pl.BlockSpec((BLK, 128), lambda i: (prev(i), jk + 1)), _full((1, 128))]


def sw_forward(cols, jblk, sinks, name):
    T = cols.shape[0]

    def body(x_ref, kp_ref, vp_ref, s_ref, y_ref):
        y_ref[...] = _sw_block(pl.program_id(0) == 0, x_ref[:, 0:512], kp_ref[...], x_ref[:, 1024:1152],
                               vp_ref[...], x_ref[:, 1152:1280], x_ref[:, 512:1024], s_ref[...]).astype(y_ref.dtype)

    return pl.pallas_call(
        body, name=name, out_shape=jax.ShapeDtypeStruct((T, 512), Y_DTYPE), grid=(T // BLK,),
        in_specs=_sw_specs(jblk, lambda i: i),
        out_specs=pl.BlockSpec((BLK, 512), lambda i: (i, 0)),
        compiler_params=pltpu.CompilerParams(dimension_semantics=("parallel",)),
    )(cols, cols, cols, sinks)


def sw_backward(cols, jblk, sinks, dy, dcols, name):
    T = cols.shape[0]
    n = T // BLK
    rev = lambda i: n - 1 - i

    def body(x_ref, kp_ref, vp_ref, s_ref, dy_ref, dcols_in, dx_ref, ds_ref, kcarry, vcarry):
        i = pl.program_id(0)

        @pl.when(i == 0)
        def _():
            kcarry[...] = jnp.zeros_like(kcarry)
            vcarry[...] = jnp.zeros_like(vcarry)
            ds_ref[...] = jnp.zeros_like(ds_ref)

        f = functools.partial(_sw_block, i == n - 1)
        _, vjp = jax.vjp(f, x_ref[:, 0:512], kp_ref[...], x_ref[:, 1024:1152], vp_ref[...], x_ref[:, 1152:1280],
                         x_ref[:, 512:1024], s_ref[...])
        dq, dkp, dkc, dvp, dvc, dz, dsk = vjp(dy_ref[...])
        dx_ref[...] = jnp.concatenate([dq, dz, dkc + kcarry[...], dvc + vcarry[...],
                                       jnp.zeros((BLK, SW_W - 1280), F32)], axis=1).astype(dx_ref.dtype)
        kcarry[...] = dkp
        vcarry[...] = dvp
        ds_ref[...] += dsk

    return pl.pallas_call(
        body, name=name,
        out_shape=[jax.ShapeDtypeStruct(dcols.shape, dcols.dtype),jax.ShapeDtypeStruct((1, 128), F32)],
        grid=(n,),
        in_specs=_sw_specs(jblk, rev) + [pl.BlockSpec((BLK, 512), lambda i: (rev(i), 0)), _ANY],
        out_specs=[pl.BlockSpec((BLK, SW_W), lambda i: (rev(i), jblk)), _full((1, 128))],
        scratch_shapes=[pltpu.VMEM((BLK, 128), F32), pltpu.VMEM((BLK, 128), F32)],
        input_output_aliases={5: 0},
        compiler_params=pltpu.CompilerParams(dimension_semantics=("arbitrary",)),
    )(cols, cols, cols, sinks, dy, dcols)


XM_TQ = 512


def _xm_block(q, z, mkv):
    outs = []
    for h in range(4):
        s = bdot("nt", q[:, 128 * h:128 * (h + 1)], mkv[:, 128 * h:128 * (h + 1)]) * (128 ** -0.5)
        m = lax.stop_gradient(jnp.max(s, axis=1, keepdims=True))
        e = jnp.exp(s - m)
        p = e / jnp.sum(e, axis=1, keepdims=True)
        outs.append(bdot("nn", p, mkv[:, 512 + 128 * h:512 + 128 * (h + 1)]))
    return jnp.concatenate(outs, axis=1) * _silu(z)


def xm_forward(cols, jblk, mkv, name):
    T = cols.shape[0]

    def body(x_ref, m_ref, y_ref):
        y_ref[...] = _xm_block(x_ref[:, 0:512], x_ref[:, 512:1024], m_ref[...]).astype(y_ref.dtype)

    return pl.pallas_call(
        body, name=name, out_shape=jax.ShapeDtypeStruct((T, 512), Y_DTYPE), grid=(T // XM_TQ,),
        in_specs=[pl.BlockSpec((XM_TQ, XM_W), lambda i: (i, jblk)), _full(mkv.shape)],
        out_specs=pl.BlockSpec((XM_TQ, 512), lambda i: (i, 0)),
        compiler_params=pltpu.CompilerParams(dimension_semantics=("parallel",)),
    )(cols, mkv)


def xm_backward(cols, jblk, mkv, dy, dcols, name):
    T = cols.shape[0]

    def body(x_ref, m_ref, dy_ref, dcols_in, dx_ref, dm_ref):
        @pl.when(pl.program_id(0) == 0)
        def _():
            dm_ref[...] = jnp.zeros_like(dm_ref)

        _, vjp = jax.vjp(_xm_block, x_ref[:, 0:512], x_ref[:, 512:1024], m_ref[...])
        dq, dz, dm = vjp(dy_ref[...])
        dx_ref[...] = jnp.concatenate([dq, dz], axis=1).astype(dx_ref.dtype)
        dm_ref[...] += dm

    return pl.pallas_call(
        body, name=name,
        out_shape=[jax.ShapeDtypeStruct(dcols.shape, dcols.dtype),jax.ShapeDtypeStruct(mkv.shape, F32)],
        grid=(T // XM_TQ,),
        in_specs=[pl.BlockSpec((XM_TQ, XM_W), lambda i: (i, jblk)), _full(mkv.shape),
                  pl.BlockSpec((XM_TQ, 512), lambda i: (i, 0)), _ANY],
        out_specs=[pl.BlockSpec((XM_TQ, XM_W), lambda i: (i, jblk)), _full(mkv.shape)],
        input_output_aliases={3: 0},
        compiler_params=pltpu.CompilerParams(dimension_semantics=("arbitrary",)),
    )(cols, mkv, dy, dcols)


def _rms(x, gain):
    return x * lax.rsqrt(jnp.mean(x * x, axis=1, keepdims=True) + EPS) * gain


def memkv_forward(mem, gain, w, name):
    def body(m_ref, g_ref, w_ref, o_ref):
        o_ref[...] = bdot("nn", _rms(m_ref[...], g_ref[...]), w_ref[...])

    return pl.pallas_call(body, name=name, out_shape=jax.ShapeDtypeStruct(mem.shape, F32),
                          compiler_params=pltpu.CompilerParams(vmem_limit_bytes=VMEM_LIMIT))(mem, gain, w)


def memkv_backward(mem, gain, w, dkv, name):
    def body(m_ref, g_ref, w_ref, d_ref, dg_ref, dw_ref):
        mem_v = m_ref[...]
        _, vjp = jax.vjp(lambda g, ww: bdot("nn", _rms(mem_v, g), ww), g_ref[...], w_ref[...].astype(F32))
        dg, dw = vjp(d_ref[...])
        dg_ref[...] = dg
        dw_ref[...] = dw

    return pl.pallas_call(body, name=name,
                          out_shape=[jax.ShapeDtypeStruct(gain.shape, F32), jax.ShapeDtypeStruct(w.shape, F32)],
                          compiler_params=pltpu.CompilerParams(vmem_limit_bytes=VMEM_LIMIT))(mem, gain, w, dkv)


MG_TB = 256


def _merge_block(ys, gl, wup, wout, gpost):
    merged = None
    for n in range(4):
        t = _sigmoid(gl[:, 1024 * n:1024 * (n + 1)]) * bdot("nn", ys[n], wup[n])
        merged = t if merged is None else merged + t
    out = bdot("nn", merged, wout)
    return _rms(out, gpost)


def merge_forward(ys, cols, jgate, x, wup, wout, gpost, name, next_gain=None, target=None):
    T, D = x.shape
    TB = 256
    n_extra = (next_gain is not None) + (target is not None)

    def body(ya, yb, yc, ym, gl_ref, x_ref, wup_ref, wout_ref, gp_ref, *rest):
        extra, outs = rest[:n_extra], rest[n_extra:]
        upd = _merge_block([ya[...], yb[...], yc[...], ym[...]], gl_ref[...],
                           [wup_ref[n] for n in range(4)], wout_ref[...], gp_ref[...])
        y = x_ref[...] + upd
        outs[0][...] = y
        outs = outs[1:]
        if next_gain is not None:
            h = _rms(y, extra[0][...])
            outs[0][...] = h.astype(BF)
            outs[1][...] = h.T.astype(BF)
            outs = outs[2:]
        if target is not None:
            l_ref, d_ref = outs

            @pl.when(pl.program_id(0) == 0)
            def _():
                l_ref[...] = jnp.zeros_like(l_ref)

            err = y - extra[-1][...]
            d_ref[...] = err * (1.0 / D)
            l_ref[...] += jnp.full(l_ref.shape, 0.5 * jnp.sum(jnp.mean(err * err, axis=1, keepdims=True)), F32)

    yspec = pl.BlockSpec((TB, 512), lambda i: (i, 0))
    xspec = pl.BlockSpec((TB, D), lambda i: (i, 0))
    extra_in, extra_specs = [], []
    out_shape, out_specs = [jax.ShapeDtypeStruct((T, D), F32)], [xspec]
    if next_gain is not None:
        extra_in, extra_specs = extra_in + [next_gain], extra_specs + [_full((1, D))]
        out_shape += [jax.ShapeDtypeStruct((T, D), BF), jax.ShapeDtypeStruct((D, T), BF)]
        out_specs += [xspec, pl.BlockSpec((D, TB), lambda i: (0, i))]
    if target is not None:
        extra_in, extra_specs = extra_in + [target], extra_specs + [xspec]
        out_shape += [jax.ShapeDtypeStruct((1, 128), F32), jax.ShapeDtypeStruct((T, D), F32)]
        out_specs += [_full((1, 128)), xspec]
    return pl.pallas_call(
        body, name=name, out_shape=out_shape, grid=(T // TB,),
        in_specs=[yspec] * 4 + [pl.BlockSpec((TB, 4096), lambda i: (i, jgate)), xspec,
                                _full(wup.shape), _full(wout.shape), _full((1, D))] + extra_specs,
        out_specs=out_specs,
        compiler_params=pltpu.CompilerParams(
            dimension_semantics=("parallel" if target is None else "arbitrary",), vmem_limit_bytes=VMEM_LIMIT),
    )(*ys, cols, x, wup, wout, gpost, *extra_in)


def _token_product(a, b, name):
    (T, m), n = a.shape, b.shape[1]

    def body(a_ref, b_ref, o_ref):
        o_ref[...] = lax.dot_general(a_ref[...].astype(BF), b_ref[...].astype(BF), _DIMS["tn"], preferred_element_type=F32)

    return pl.pallas_call(body, name=name, out_shape=jax.ShapeDtypeStruct((m, n), F32),
                          compiler_params=pltpu.CompilerParams(vmem_limit_bytes=VMEM_LIMIT))(a, b)


def merge_backward(ys, cols, jgate, wup, wout, gpost, dx, name):
    T = dx.shape[0]
    TB = MG_TB

    def body(ya, yb, yc, ym, gl_ref, wup_ref, wout_ref, gp_ref, dx_ref,
             dgl_ref, dya, dyb, dyc, dym, dpa, dpb, dpc, dpm, merged_ref, dout_ref, dgp_ref):
        @pl.when(pl.program_id(0) == 0)
        def _():
            dgp_ref[...] = jnp.zeros_like(dgp_ref)

        y_refs = (ya, yb, yc, ym)
        gates = [_sigmoid(gl_ref[:, 1024 * n:1024 * (n + 1)]) for n in range(4)]
        projs = [bdot("nn", y_refs[n][...], wup_ref[n]) for n in range(4)]
        merged = gates[0] * projs[0] + gates[1] * projs[1] + gates[2] * projs[2] + gates[3] * projs[3]
        out = bdot("nn", merged, wout_ref[...])
        _, vjp = jax.vjp(_rms, out, gp_ref[...])
        dout, dgp = vjp(dx_ref[...])
        dmerged = bdot("nt", dout, wout_ref[...])
        for n, (dy_ref, dp_ref) in enumerate(zip((dya, dyb, dyc, dym), (dpa, dpb, dpc, dpm))):
            dproj = dmerged * gates[n]
            dgl_ref[:, 1024 * n:1024 * (n + 1)] = (dmerged * projs[n] * gates[n] * (1.0 - gates[n])).astype(dgl_ref.dtype)
            dy_ref[...] = bdot("nt", dproj, wup_ref[n])
            dp_ref[...] = dproj.astype(BF)
        merged_ref[...] = merged.astype(BF)
        dout_ref[...] = dout.astype(BF)
        dgp_ref[...] += dgp

    yspec = pl.BlockSpec((TB, 512), lambda i: (i, 0))
    dspec = pl.BlockSpec((TB, 1024), lambda i: (i, 0))
    dcols, dya, dyb, dyc, dym, *dproj, merged, dout, dgp = pl.pallas_call(
        body, name=name,
        out_shape=[jax.ShapeDtypeStruct(cols.shape, BF)] + [jax.ShapeDtypeStruct((T, 512), F32)] * 4 + [
            jax.ShapeDtypeStruct((T, 1024), BF)] * 6 + [jax.ShapeDtypeStruct((1, 1024), F32)],
        grid=(T // TB,),
        in_specs=[yspec] * 4 + [pl.BlockSpec((TB, 4096), lambda i: (i, jgate)),
                                _full(wup.shape), _full(wout.shape), _full((1, 1024)), dspec],
        out_specs=[pl.BlockSpec((TB, 4096), lambda i: (i, jgate))] + [yspec] * 4 + [
            dspec] * 6 + [_full((1, 1024))],
        compiler_params=pltpu.CompilerParams(dimension_semantics=("arbitrary",), vmem_limit_bytes=VMEM_LIMIT),
    )(*ys, cols, wup, wout, gpost, dx)
    dwup = jnp.stack([_token_product(ys[n], dproj[n], "%s_w_up%d" % (name, n)) for n in range(4)])
    dwout = _token_product(merged, dout, name + "_w_out")
    return dcols, dya, dyb, dyc, dym, dwup, dwout, dgp


NB = 256


def prenorm_forward(x, gain, name):
    T, D = x.shape

    def body(x_ref, g_ref, o_ref, ot_ref):
        h = _rms(x_ref[...], g_ref[...])
        o_ref[...] = h.astype(BF)
        ot_ref[...] = h.T.astype(BF)

    return pl.pallas_call(
        body, name=name,
        out_shape=[jax.ShapeDtypeStruct((T, D), BF), jax.ShapeDtypeStruct((D, T), BF)], grid=(T // NB,),
        in_specs=[pl.BlockSpec((NB, D), lambda i: (i, 0)), _full((1, D))],
        out_specs=[pl.BlockSpec((NB, D), lambda i: (i, 0)), pl.BlockSpec((D, NB), lambda i: (0, i))],
        compiler_params=pltpu.CompilerParams(dimension_semantics=("parallel",)),
    )(x, gain)


def prenorm_backward(x, gain, dh, dres, name):
    T = x.shape[0]

    def body(x_ref, g_ref, dh_ref, dr_ref, dx_ref, dg_ref):
        @pl.when(pl.program_id(0) == 0)
        def _():
            dg_ref[...] = jnp.zeros_like(dg_ref)

        _, vjp = jax.vjp(_rms, x_ref[...], g_ref[...])
        dxn, dg = vjp(dh_ref[...])
        dx_ref[...] = dr_ref[...] + dxn
        dg_ref[...] += dg

    spec = pl.BlockSpec((NB, 1024), lambda i: (i, 0))
    return pl.pallas_call(
        body, name=name,
        out_shape=[jax.ShapeDtypeStruct(x.shape, F32), jax.ShapeDtypeStruct((1, 1024), F32)], grid=(T // NB,),
        in_specs=[spec, _full((1, 1024)), spec, spec], out_specs=[spec, _full((1, 1024))],
        compiler_params=pltpu.CompilerParams(dimension_semantics=("arbitrary",)),
    )(x, gain, dh, dres)


JB_GATE, JB_XM, JB_DN, JB_SW, JB_GM = 0, 4, 2, 5, 6
_ALIGNED_PIECES = ((5896, 4096), (4872, 512), (5384, 512), (0, 2048), (2048, 8), 504, (3592, 512), (4360, 512),
                   (4104, 128), (4232, 128), 256, (2056, 1024), (3080, 512))
_NATURAL_FROM_ALIGNED = ((5120, 2048), (7168, 8), (9216, 1024), (10240, 512), (7680, 512), (8704, 128), (8832, 128),
                         (8192, 512), (4096, 512), (4608, 512), (0, 4096))


def _natural_range(slots, start, width):
    out = []
    while width > 0:
        j, i = divmod(start, W_IN_SHARD)
        take = min(width, W_IN_SHARD - i)
        out.append(slots[j, :, i:i + take])
        start, width = start + take, width - take
    return out


def _aligned_w_in(slots, zero=0.0):
    parts = []
    for piece in _ALIGNED_PIECES:
        if isinstance(piece, int):
            parts.append(jnp.full(slots.shape[1:2] + (piece,), zero, slots.dtype))
        else:
            parts += _natural_range(slots, *piece)
    return jnp.concatenate(parts, axis=-1)


def _slots_of_aligned(d_al):
    slots = []
    for s in range(N_DEV):
        lo, hi = s * W_IN_SHARD, (s + 1) * W_IN_SHARD
        parts, nat = [], 0
        for a_start, width in _NATURAL_FROM_ALIGNED:
            b, e = max(lo, nat), min(hi, nat + width)
            if b < e:
                parts.append(d_al[..., a_start + b - nat:a_start + e - nat])
            nat += width
        parts.append(jnp.zeros(d_al.shape[:1] + (W_IN_SHARD_PAD - W_IN_SHARD,), d_al.dtype))
        slots.append(jnp.concatenate(parts, axis=-1))
    return jnp.stack(slots)


SMALL_VEC_W = 1024


def _pack_small(parts):
    rows = []
    for p in parts:
        flat = p.reshape(-1).astype(F32)
        r = -(-flat.shape[0] // SMALL_VEC_W)
        rows.append(jnp.pad(flat, (0, r * SMALL_VEC_W - flat.shape[0])).reshape(r, SMALL_VEC_W))
    vec = jnp.concatenate(rows, axis=0)
    return jnp.pad(vec, ((0, -vec.shape[0] % 8), (0, 0)))


def _unpack_small(vec, shapes):
    out, off = [], 0
    for s in shapes:
        n = math.prod(s)
        r = -(-n // SMALL_VEC_W)
        out.append(vec[off:off + r].reshape(-1)[:n].reshape(s))
        off += r
    return out


def _lanes(vec, at):
    return jnp.zeros((1, 128), F32).at[0, at:at + vec.shape[0]].set(vec)


SMALL_NAMES = ("norm_pre", "norm_post", "norm_mem", "a_log", "dt_bias", "dn_norm", "gm_norm",
               "spatial_w", "spatial_b", "sinks")


def _other_weights(s_mem, s_up, s_out):
    return (s_mem.reshape(D_MODEL, 2 * BRANCH_W),
            jnp.transpose(s_up, (1, 2, 0, 3)).reshape(N_BRANCH, BRANCH_W, D_MODEL), s_out.reshape(D_MODEL, D_MODEL))


def _grad_slots(d_in_al, d_mem, d_up, d_out):
    return [None if d_in_al is None else _slots_of_aligned(d_in_al), d_mem.astype(BF).reshape(N_DEV, 128, 2 * BRANCH_W),
            jnp.transpose(d_up.astype(BF).reshape(N_BRANCH, BRANCH_W, N_DEV, 128), (2, 0, 1, 3)),
            d_out.astype(BF).reshape(N_DEV, 128, D_MODEL)]


def _layer_params(l, small, conv_full, token):
    return dict(
        gpre=small["norm_pre"][l][None] + token, gpost=small["norm_post"][l][None], gmem=small["norm_mem"][l][None],
        cw=conv_full[l], al=_lanes(small["a_log"][l], 4), dt=_lanes(small["dt_bias"][l], 4),
        dnn=small["dn_norm"][l][None], gain=small["gm_norm"][l][None], ws=small["spatial_w"][l],
        bt=jnp.zeros((128, 128), F32).at[:, :GM_GROUPS].set(small["spatial_b"][l].T),
        sinks=_lanes(small["sinks"][l], 0))


def _layer_forward(l, xl, hs, mem, p, w_in_al, other_weights, **tail):
    t = "l%d_" % l
    h, h_t = hs
    cols = _matmul(h, w_in_al, "nn", F32, (1024, 1536, 1024), t + "w_in")
    ya, ss, ts = dn_forward(cols, JB_DN, p["cw"], p["al"], p["dt"], p["dnn"], t + "deltanet")
    yb = gm_forward(cols, JB_GM, p["gain"], p["ws"], p["bt"], t + "gmlp")
    yc = sw_forward(cols, JB_SW, p["sinks"], t + "swa")
    w_mem, w_up, w_out = other_weights(yc)
    mkv = memkv_forward(mem, p["gmem"], w_mem, t + "memkv")
    ym = xm_forward(cols, JB_XM, mkv, t + "memattn")
    outs = merge_forward([ya, yb, yc, ym], cols, JB_GATE, xl, w_up, w_out, p["gpost"], t + "merge", **tail)
    return outs, dict(p, x=xl, h_t=h_t, cols=cols, mkv=mkv, ss=ss, ts=ts, ys=[ya, yb, yc, ym]), (w_in_al, w_mem, w_up, w_out)


def _layer_backward(l, s, mem, weights, dx, token, early=None):
    w_in_al, w_mem, w_up, w_out = weights
    t = "l%d_" % l
    cols = s["cols"]
    dcols, dya, dyb, dyc, dym, dwup, dwout, dgpost = merge_backward(
        s["ys"], cols, JB_GATE, w_up, w_out, s["gpost"] + token, dx, t + "merge_bwd")
    dcols, dmkv = xm_backward(cols, JB_XM, s["mkv"], dym, dcols, t + "memattn_bwd")
    dgmem, dwmem = memkv_backward(mem, s["gmem"], w_mem, dmkv, t + "memkv_bwd")
    sinks = s["sinks"] if early is None else s["sinks"] + early(dwmem, dwup, dwout)
    dcols, dsinks = sw_backward(cols, JB_SW, sinks, dyc, dcols, t + "swa_bwd")
    dcols, dgain, dws, dbt = gm_backward(cols, JB_GM, s["gain"], s["ws"], s["bt"], dyb, dcols, t + "gmlp_bwd")
    dcols, dcw, dal, ddt, ddn = dn_backward(
        cols, JB_DN, s["cw"], s["al"], s["dt"], s["dnn"], s["ss"], s["ts"], dya, dcols, t + "deltanet_bwd")
    dh = _matmul(dcols, w_in_al, "nt", F32, (1024, 1024, 3584), t + "w_in_bwd_x")
    dwin = _matmul(s["h_t"], dcols, "nn", BF, (1024, 1536, 2048), t + "w_in_bwd_w")
    dx, dgpre = prenorm_backward(s["x"], s["gpre"], dh, dx, t + "prenorm_bwd")
    gsmall = dict(norm_pre=dgpre[0], norm_post=dgpost[0], norm_mem=dgmem[0], a_log=dal[0, 4:8], dt_bias=ddt[0, 4:8],
                  dn_norm=ddn[0], gm_norm=dgain[0], spatial_w=dws, spatial_b=dbt[:, :GM_GROUPS].T,
                  sinks=dsinks[0, :SW_HEADS], conv_w=dcw)
    return dx, gsmall, (dwin, dwmem, dwup, dwout)


def kernel(x, mem, norm_pre, norm_post, norm_mem, w_in, conv_w, a_log, dt_bias, dn_norm, gm_norm, spatial_w, spatial_b, sinks, w_mem_kv, w_up, w_out, loss_target, m_norm_pre, m_norm_post, m_norm_mem, m_w_in, m_conv_w, m_a_log, m_dt_bias, m_dn_norm, m_gm_norm, m_spatial_w, m_spatial_b, m_sinks, m_w_mem_kv, m_w_up, m_w_out, v_norm_pre, v_norm_post, v_norm_mem, v_w_in, v_conv_w, v_a_log, v_dt_bias, v_dn_norm, v_gm_norm, v_spatial_w, v_spatial_b, v_sinks, v_w_mem_kv, v_w_up, v_w_out):
    xi, yi, ci = _my_place()
    my_slot = 4 * xi + 2 * yi + ci
    conv_shard = conv_w.shape[-1]
    x2, mem2, target = x[0], mem[0], loss_target[0]

    w_in_pad = jnp.pad(w_in.astype(BF), ((0, 0), (0, 0), (0, W_IN_SHARD_PAD - W_IN_SHARD)))
    shards = [[w_in_pad[l], w_mem_kv[l].astype(BF), w_up[l].astype(BF), w_out[l].astype(BF)] for l in range(DEPTH)]
    w_in_slots0, conv_slots = _all_gather_slots([shards[0][0], conv_w], "gather_weights_l0")
    ag = list(_spread_start(shards[0][1:] + shards[1], "gather", "gather_weights_rest_start", after=w_in_slots0))
    conv_full = jnp.transpose(conv_slots, (1, 2, 0, 3)).reshape(DEPTH, CONV_W, N_DEV * conv_shard)
    small = dict(norm_pre=norm_pre, norm_post=norm_post, norm_mem=norm_mem, a_log=a_log,
                 dt_bias=dt_bias, dn_norm=dn_norm, gm_norm=gm_norm, spatial_w=spatial_w,
                 spatial_b=spatial_b, sinks=sinks)

    def arrived(which, after, name):
        ag[2], ag[3] = _spread_wait(ag[0], ag[1], ag[2], ag[3], which, after, name)
        return [ag[3][a] for a in which]

    p0, p1 = _layer_params(0, small, conv_full, ag[4][0, 0]), _layer_params(1, small, conv_full, 0.0)
    (x1, h1, h1_t), saved0, weights0 = _layer_forward(
        0, x2, prenorm_forward(x2, p0["gpre"], "l0_prenorm"), mem2, p0, _aligned_w_in(w_in_slots0, ag[4][0, 0]),
        lambda y: _other_weights(*arrived([0, 1, 2], y, "gather_weights_l0_rest_wait")),
        next_gain=p1["gpre"])
    w_in_slots1, = arrived([3], x1, "gather_weights_l1_w_in_wait")
    (x_out, loss, dx), saved1, weights1 = _layer_forward(
        1, x1, (h1, h1_t), mem2, p1, _aligned_w_in(w_in_slots1),
        lambda y: _other_weights(*arrived([4, 5, 6], y, "gather_weights_l1_rest_wait")), target=target)

    packed_names = SMALL_NAMES + ("conv_w",)
    dx, gsmall1, gbig1 = _layer_backward(1, saved1, mem2, weights1, dx, 0.0)
    small1 = [loss[0, :1]] + [gsmall1[n] for n in packed_names]
    sm1 = _spread_start([_pack_small(small1)], "gather", "gather_small_grads_l1_start")
    rs_send, rs_recv, rs_src, rs_land, rs_token = _spread_start(_grad_slots(*gbig1), "scatter", "exchange_grads_l1_start")
    rest0 = []

    def send_rest0(dwmem, dwup, dwout):
        rest0.extend(_spread_start(_grad_slots(None, dwmem, dwup, dwout)[1:], "scatter", "exchange_grads_l0_rest_start"))
        return rest0[4][0, 0]

    dx, gsmall0, gbig0 = _layer_backward(0, saved0, mem2, weights0, dx, rs_token[0, 0] + sm1[4][0, 0], send_rest0)
    _, parts1 = _spread_wait(rs_send, rs_recv, rs_src, rs_land, range(4), dx, "exchange_grads_l1_wait")

    small0 = [gsmall0[n] for n in packed_names]
    sm0 = _spread_start([_pack_small(small0)], "gather", "gather_small_grads_l0_start")

    g_win0 = _slots_of_aligned(gbig0[0])
    g_win0 = g_win0.reshape((N_DEV // 2, 2) + g_win0.shape[1:])
    theirs, = _exchange_sibling([g_win0], "exchange_sibling_l0")
    chip_sum = _pair_sum(g_win0, theirs, "pair_sum_l0")
    ch_send, ch_recv, ch_src, ch_land, ch_token = _spread_start([chip_sum], "chips", "exchange_chips_l0_start")

    _, (land1,) = _spread_wait(*sm1[:4], [0], ch_token, "gather_small_grads_l1_wait")
    _, (land0,) = _spread_wait(*sm0[:4], [0], land1, "gather_small_grads_l0_wait")
    tot1 = _unpack_small(_sum_slots(land1, "sum_small_grads_l1"), [p.shape for p in small1])
    tot0 = _unpack_small(_sum_slots(land0, "sum_small_grads_l0"), [p.shape for p in small0])
    loss_tot = tot1[0][0]
    grads = {n: jnp.stack([g0, g1]) for n, g0, g1 in zip(packed_names, tot0, tot1[1:])}
    grads["conv_w"] = lax.dynamic_slice_in_dim(grads["conv_w"], my_slot * conv_shard, conv_shard, axis=2)

    given = dict(norm_pre=(norm_pre, m_norm_pre, v_norm_pre), norm_post=(norm_post, m_norm_post, v_norm_post),
                 norm_mem=(norm_mem, m_norm_mem, v_norm_mem), a_log=(a_log, m_a_log, v_a_log),
                 dt_bias=(dt_bias, m_dt_bias, v_dt_bias), dn_norm=(dn_norm, m_dn_norm, v_dn_norm),
                 gm_norm=(gm_norm, m_gm_norm, v_gm_norm), spatial_w=(spatial_w, m_spatial_w, v_spatial_w),
                 spatial_b=(spatial_b, m_spatial_b, v_spatial_b), sinks=(sinks, m_sinks, v_sinks),
                 conv_w=(conv_w, m_conv_w, v_conv_w))
    pshapes = [given[n][0].shape for n in packed_names]
    pw, pm, pv = (_pack_small([given[n][i] for n in packed_names]) for i in range(3))
    pd, pnm, pnv = _adamw(pw + ch_token[0, 0], _pack_small([grads[n] for n in packed_names]), pm, pv, "adamw_small")
    upd = {n: t for n, t in zip(packed_names, zip(_unpack_small(pd, pshapes), _unpack_small(pnm, pshapes),
                                                  _unpack_small(pnv, pshapes)))}
    big = (("w_mem_kv", (w_mem_kv, m_w_mem_kv, v_w_mem_kv)), ("w_up", (w_up, m_w_up, v_w_up)),
           ("w_out", (w_out, m_w_out, v_w_out)))
    first = [_sum_adamw(parts1[1 + i], w, m, v, 1, None, ch_token, "adamw_%s_l1" % name)
             for i, (name, (w, m, v)) in enumerate(big)]
    _, parts0_rest = _spread_wait(*rest0[:4], range(3), first[-1][0], "exchange_grads_l0_rest_wait")
    for i, (name, (w, m, v)) in enumerate(big):
        g, d, nm, nv = _sum_adamw(parts0_rest[i], w, m, v, 0, first[i], None, "adamw_%s_l0" % name)
        grads[name], upd[name] = g, (d, nm, nv)
    _, (parts0_w_in,) = _spread_wait(ch_send, ch_recv, ch_src, ch_land, [0], upd["w_out"][0], "exchange_chips_l0_wait")
    w_in_t, m_w_in_t, v_w_in_t = (jnp.transpose(t, (2, 0, 1)) for t in (w_in, m_w_in, v_w_in))
    g, d, nm, nv = (jnp.transpose(t, (1, 2, 0)) for t in
                    _sum_adamw_t([parts0_w_in, parts1[0]], w_in_t, m_w_in_t, v_w_in_t, "adamw_w_in"))
    grads["w_in"], upd["w_in"] = g, (d, nm, nv)

    order = ("norm_pre", "norm_post", "norm_mem", "w_in", "conv_w", "a_log", "dt_bias", "dn_norm",
             "gm_norm", "spatial_w", "spatial_b", "sinks", "w_mem_kv", "w_up", "w_out")
    return (loss_tot, dx[None], *[grads[n] for n in order], *[upd[n][0] for n in order],
            *[upd[n][1] for n in order], *[upd[n][2] for n in order])
```
